```python
import math
import jax, jax.numpy as jnp
from jax import lax
import numpy as np

D_MODEL = 1024
BATCH = 8
SEQ = 2048
DEPTH = 2
DEC_BATCH = 128
DEC_SEQ = 1
PAST_LEN = 16384
PAGE_SIZE = 128

N_EVEN = (DEPTH + 1) // 2
N_ODD = DEPTH // 2
S5_WIDTH = D_MODEL // 2
S5_GROUP = 16
S5_GROUPS = S5_WIDTH // S5_GROUP
S5_N = 64
RW_WIDTH = D_MODEL - S5_WIDTH
RW_HD = 64
RW_HEADS = RW_WIDTH // RW_HD
RW_LORA_W = 64
RW_LORA_A = 64
RW_LORA_G = 128
RW_PROJ = 3 * RW_WIDTH + RW_LORA_W + RW_LORA_A + RW_LORA_G
IN0 = S5_WIDTH + RW_PROJ
RET_DK = 256
RET_HEADS = D_MODEL // RET_DK
RET_DV = 2 * RET_DK
RET_CHUNK = 128
IN1 = 2 * RET_HEADS * RET_DK + 2 * RET_HEADS * RET_DV
N_MEM = 256
MEM_HEADS = 4
MEM_HD = D_MODEL // MEM_HEADS
MOE_GROUPS = 4
MOE_PER_GROUP = 4
MOE_EXPERTS = MOE_GROUPS * MOE_PER_GROUP
MOE_HIDDEN = 256
MOE_TOPK = 2
NORM_EPS = 1e-6
RW_GN_EPS = 64e-5
ROPE_BASE = 10000.0
F32 = jnp.float32

kernel_name = 'hybrid_s5_rwkv7_retnet_hmoe_step'


def rmsnorm(x, g):
    xf = x.astype(F32)
    y = xf * lax.rsqrt(jnp.mean(xf * xf, axis=-1, keepdims=True) + NORM_EPS)
    return (y * g.astype(F32)).astype(x.dtype)


def cmul(ar, ai, br, bi):
    return ar * br - ai * bi, ar * bi + ai * br


def s5_mixer(u, h_re, h_im, a_re, a_im, b_re, b_im, c_re, c_im, d_skip, log_dt, w_glu):
    bsz, t = u.shape[:2]
    ug = u.astype(F32).reshape(bsz, t, S5_GROUPS, S5_GROUP)
    dt = jnp.exp(log_dt.astype(F32))[:, None]
    ar, ai = a_re.astype(F32), a_im.astype(F32)
    mag = jnp.exp(dt * ar)
    abar_re, abar_im = mag * jnp.cos(dt * ai), mag * jnp.sin(dt * ai)
    den = ar * ar + ai * ai
    nr = abar_re - 1.0
    coef_re = (nr * ar + abar_im * ai) / den
    coef_im = (abar_im * ar - nr * ai) / den
    bb_re, bb_im = cmul(coef_re[..., None], coef_im[..., None], b_re.astype(F32), b_im.astype(F32))
    bu_re = jnp.einsum('btgc,gnc->btgn', ug, bb_re)
    bu_im = jnp.einsum('btgc,gnc->btgn', ug, bb_im)
    shp = bu_re.shape
    elems = (jnp.broadcast_to(abar_re, shp), jnp.broadcast_to(abar_im, shp), bu_re, bu_im)

    def combine(e1, e2):
        a1r, a1i, b1r, b1i = e1
        a2r, a2i, b2r, b2i = e2
        nar, nai = cmul(a2r, a2i, a1r, a1i)
        nbr, nbi = cmul(a2r, a2i, b1r, b1i)
        return nar, nai, nbr + b2r, nbi + b2i

    acum_re, acum_im, bcum_re, bcum_im = lax.associative_scan(combine, elems, axis=1)
    pr, pi = cmul(acum_re, acum_im, h_re.astype(F32)[:, None], h_im.astype(F32)[:, None])
    x_re, x_im = pr + bcum_re, pi + bcum_im
    y = (jnp.einsum('btgn,gcn->btgc', x_re, c_re.astype(F32))
         - jnp.einsum('btgn,gcn->btgc', x_im, c_im.astype(F32))
         + d_skip.astype(F32).reshape(S5_GROUPS, S5_GROUP) * ug)
    y = jax.nn.gelu(y.reshape(bsz, t, S5_WIDTH))
    y = y * jax.nn.sigmoid(y @ w_glu.astype(F32))
    return y.astype(u.dtype), x_re[:, -1], x_im[:, -1]


def rwkv7_mixer(p, shift_prev, s0, mu, w0, w2, a0, a2, g2, k_k, k_a, r_k, ln_w, ln_b):
    bsz, t = p.shape[:2]
    pf = p.astype(F32)
    p_prev = jnp.concatenate([shift_prev.astype(F32)[:, None], pf[:, :-1]], axis=1)
    xm = pf + (p_prev - pf) * mu.astype(F32)
    o1, o2, o3 = RW_WIDTH, 2 * RW_WIDTH, 3 * RW_WIDTH
    o4, o5 = o3 + RW_LORA_W, o3 + RW_LORA_W + RW_LORA_A
    r, k, v = xm[..., :o1], xm[..., o1:o2], xm[..., o2:o3]
    wd, ad, gd = xm[..., o3:o4], xm[..., o4:o5], xm[..., o5:]
    w = -jax.nn.softplus(-(w0 + jnp.tanh(wd) @ w2)) - 0.5
    decay = jnp.exp(-jnp.exp(w))
    a = jax.nn.sigmoid(a0 + ad @ a2)
    g = jax.nn.sigmoid(gd) @ g2

    def heads(z):
        return z.reshape(bsz, t, RW_HEADS, RW_HD)

    kk = heads(k * k_k)
    kk = kk / jnp.maximum(jnp.sqrt(jnp.sum(kk * kk, axis=-1, keepdims=True)), 1e-12)
    k = k * (1.0 + (a - 1.0) * k_a)
    rh, kh, vh, wh, ah = heads(r), heads(k), heads(v), heads(decay), heads(a)
    xs = jnp.moveaxis(jnp.stack([rh, wh, kh, vh, -kk, kk * ah], axis=0), 2, 0)

    def step(S, inp):
        r_t, w_t, k_t, v_t, a_t, b_t = inp[0], inp[1], inp[2], inp[3], inp[4], inp[5]
        sa = jnp.einsum('bhvk,bhk->bhv', S, a_t)
        S = S * w_t[:, :, None, :] + sa[..., None] * b_t[:, :, None, :] + v_t[..., None] * k_t[:, :, None, :]
        return S, jnp.einsum('bhvk,bhk->bhv', S, r_t)

    s_fin, o = lax.scan(step, s0.astype(F32), xs)
    o = jnp.moveaxis(o, 0, 1)
    mean = jnp.mean(o, axis=-1, keepdims=True)
    var = jnp.mean(jnp.square(o - mean), axis=-1, keepdims=True)
    o = ((o - mean) * lax.rsqrt(var + RW_GN_EPS)).reshape(bsz, t, RW_WIDTH) * ln_w + ln_b
    bonus = jnp.sum(rh * kh * r_k.reshape(RW_HEADS, RW_HD), axis=-1, keepdims=True) * vh
    o = (o + bonus.reshape(bsz, t, RW_WIDTH)) * g
    return o.astype(p.dtype), s_fin, pf[:, -1]


def rotary(x, pos):
    half = x.shape[-1] // 2
    inv = ROPE_BASE ** (-jnp.arange(half, dtype=F32) / half)
    ang = pos[:, None] * inv[None, :]
    cos, sin = jnp.cos(ang)[None, :, None, :], jnp.sin(ang)[None, :, None, :]
    x1, x2 = x[..., :half], x[..., half:]
    return jnp.concatenate([x1 * cos - x2 * sin, x1 * sin + x2 * cos], axis=-1)


def retention_mixer(q, k, v, s0, pos0):
    bsz, t = q.shape[:2]
    pos = pos0 + jnp.arange(t, dtype=F32)
    q = rotary(q.astype(F32), pos)
    k = rotary(k.astype(F32), pos) * (RET_DK ** -0.5)
    v = v.astype(F32)
    log_g = jnp.log(1.0 - 2.0 ** (-5.0 - jnp.arange(RET_HEADS, dtype=F32)))
    L = math.gcd(t, RET_CHUNK)
    nc = t // L
    idx = jnp.arange(L, dtype=F32)
    diff = idx[:, None] - idx[None, :]
    dmask = jnp.where(diff >= 0, jnp.exp(log_g[:, None, None] * jnp.maximum(diff, 0.0)), 0.0)
    q_dec = jnp.exp(log_g[None, :] * (idx[:, None] + 1.0))
    k_dec = jnp.exp(log_g[None, :] * (L - 1.0 - idx[:, None]))
    c_dec = jnp.exp(log_g * L)

    def chunks(z):
        return jnp.moveaxis(z.reshape(bsz, nc, L, RET_HEADS, z.shape[-1]), 1, 0)

    def step(S, inp):
        qc, kc, vc = inp
        sc = jnp.einsum('bihd,bjhd->bhij', qc, kc) * dmask
        o = (jnp.einsum('bhij,bjhv->bihv', sc, vc)
             + jnp.einsum('bihd,bhdv->bihv', qc * q_dec[None, :, :, None], S))
        S = S * c_dec[None, :, None, None] + jnp.einsum('bjhd,bjhv->bhdv', kc * k_dec[None, :, :, None], vc)
        return S, o

    s_fin, o = lax.scan(step, s0.astype(F32), (chunks(q), chunks(k), chunks(v)))
    o = jnp.moveaxis(o, 0, 1).reshape(bsz, t, RET_HEADS, RET_DV)
    return o, s_fin


def mem_kv(mem, g, w_k, w_v):
    h = rmsnorm(mem, g)
    b, m = mem.shape[:2]
    return (h @ w_k).reshape(b, m, MEM_HEADS, MEM_HD), (h @ w_v).reshape(b, m, MEM_HEADS, MEM_HD)


def cross_attend(h, mk, mv, w_q, w_o):
    b, t = h.shape[:2]
    q = (h @ w_q).reshape(b, t, MEM_HEADS, MEM_HD)
    s = jnp.einsum('bthd,bmhd->bhtm', q.astype(F32), mk.astype(F32)) * (MEM_HD ** -0.5)
    p = jax.nn.softmax(s, axis=-1)
    o = jnp.einsum('bhtm,bmhd->bthd', p, mv.astype(F32)).reshape(b, t, D_MODEL)
    return (o @ w_o.astype(F32)).astype(h.dtype)


def hier_moe(h, w_rc, b_rc, w_rf, b_rf, w1, w3, w2):
    shp = h.shape
    hf = h.reshape(-1, D_MODEL)
    n = hf.shape[0]
    lc = (hf @ w_rc + b_rc).astype(F32)
    pc = jax.nn.softmax(lc, axis=-1)
    g_idx = jnp.argmax(lc, axis=-1)
    p_g = jnp.take_along_axis(pc, g_idx[:, None], axis=1)
    lf = (hf @ w_rf + b_rf).astype(F32).reshape(n, MOE_GROUPS, MOE_PER_GROUP)
    lf_g = jnp.take_along_axis(lf, g_idx[:, None, None], axis=1)[:, 0]
    top_v, top_i = lax.top_k(jax.nn.softmax(lf_g, axis=-1), MOE_TOPK)
    wts = p_g * top_v / jnp.sum(top_v, axis=-1, keepdims=True)
    e_idx = g_idx[:, None] * MOE_PER_GROUP + top_i
    gate = jnp.sum(jax.nn.one_hot(e_idx, MOE_EXPERTS, dtype=F32) * wts[..., None], axis=1)
    hid = jax.nn.silu(jnp.einsum('nd,edf->nef', hf, w1)) * jnp.einsum('nd,edf->nef', hf, w3)
    y = jnp.einsum('nef,efd->nd', hid * gate[..., None].astype(hid.dtype), w2)
    return y.reshape(shp).astype(h.dtype)


def run_trunk(x, pos0, s5_re, s5_im, rw_state, rw_shift, ret_state, mem_k, mem_v, w):
    bsz, t = x.shape[:2]
    out_s5_re, out_s5_im, out_rw, out_shift, out_ret = [], [], [], [], []
    nq = RET_HEADS * RET_DK
    nv = RET_HEADS * RET_DV
    for layer in range(DEPTH):
        h = rmsnorm(x, w['norm_mix'][layer])
        if layer % 2 == 0:
            i = layer // 2
            proj = h @ w['w_in0'][i]
            y_s5, sr, si = s5_mixer(proj[..., :S5_WIDTH], s5_re[i], s5_im[i],
                                    w['s5_a_re'][i], w['s5_a_im'][i], w['s5_b_re'][i], w['s5_b_im'][i],
                                    w['s5_c_re'][i], w['s5_c_im'][i], w['s5_d'][i], w['s5_log_dt'][i],
                                    w['s5_w_glu'][i])
            y_rw, srw, sh = rwkv7_mixer(proj[..., S5_WIDTH:], rw_shift[i], rw_state[i],
                                        w['rw_mu'][i], w['rw_w0'][i], w['rw_w2'][i], w['rw_a0'][i],
                                        w['rw_a2'][i], w['rw_g2'][i], w['rw_k_k'][i], w['rw_k_a'][i],
                                        w['rw_r_k'][i], w['rw_ln_w'][i], w['rw_ln_b'][i])
            x = x + (jnp.concatenate([y_s5, y_rw], axis=-1) @ w['w_out0'][i]).astype(x.dtype)
            out_s5_re.append(sr)
            out_s5_im.append(si)
            out_rw.append(srw)
            out_shift.append(sh)
        else:
            j = layer // 2
            proj = h @ w['w_in1'][j]
            q = proj[..., :nq].reshape(bsz, t, RET_HEADS, RET_DK)
            k = proj[..., nq:2 * nq].reshape(bsz, t, RET_HEADS, RET_DK)
            v = proj[..., 2 * nq:2 * nq + nv].reshape(bsz, t, RET_HEADS, RET_DV)
            g = proj[..., 2 * nq + nv:]
            o, s_new = retention_mixer(q, k, v, ret_state[j], pos0)
            o = o * lax.rsqrt(jnp.mean(o * o, axis=-1, keepdims=True) + NORM_EPS)
            o = jax.nn.silu(g.astype(F32)) * o.reshape(bsz, t, nv)
            x = x + (o @ w['w_out1'][j]).astype(x.dtype)
            out_ret.append(s_new)
        h = rmsnorm(x, w['norm_mem'][layer])
        x = x + cross_attend(h, mem_k[layer], mem_v[layer], w['w_mq'][layer], w['w_mo'][layer])
        h = rmsnorm(x, w['norm_ffn'][layer])
        x = x + hier_moe(h, w['moe_w_rc'][layer], w['moe_b_rc'][layer], w['moe_w_rf'][layer],
                         w['moe_b_rf'][layer], w['moe_w1'][layer], w['moe_w3'][layer], w['moe_w2'][layer])
    y = rmsnorm(x, w['norm_final'])
    return y, jnp.stack(out_s5_re), jnp.stack(out_s5_im), jnp.stack(out_rw), jnp.stack(out_shift), jnp.stack(out_ret)


def setup_inputs(seed: int = 0) -> dict:
    key = jax.random.key(seed)
    ks = iter(jax.random.split(key, 64))

    def nrm(shape, scale):
        return jax.random.normal(next(ks), shape, F32) * scale

    def unif(shape, lo, hi):
        return jax.random.uniform(next(ks), shape, F32, lo, hi)

    d = D_MODEL
    x_prompt = nrm((BATCH, SEQ, d), 1.0)
    x_sample = nrm((DEC_BATCH, DEC_SEQ, d), 1.0)
    mem_prompt = nrm((BATCH, N_MEM, d), 1.0)
    state_s5_re = nrm((N_EVEN, DEC_BATCH, S5_GROUPS, S5_N), 0.1)
    state_s5_im = nrm((N_EVEN, DEC_BATCH, S5_GROUPS, S5_N), 0.1)
    state_rwkv = nrm((N_EVEN, DEC_BATCH, RW_HEADS, RW_HD, RW_HD), 0.1)
    state_shift = nrm((N_EVEN, DEC_BATCH, RW_PROJ), 1.0)
    state_ret = nrm((N_ODD, DEC_BATCH, RET_HEADS, RET_DK, RET_DV), 0.1)
    cache_mem_k = nrm((DEPTH, DEC_BATCH, N_MEM, MEM_HEADS, MEM_HD), 1.0)
    cache_mem_v = nrm((DEPTH, DEC_BATCH, N_MEM, MEM_HEADS, MEM_HD), 1.0)
    norm_mix = 1.0 + nrm((DEPTH, d), 0.02)
    norm_mem = 1.0 + nrm((DEPTH, d), 0.02)
    norm_ffn = 1.0 + nrm((DEPTH, d), 0.02)
    norm_final = 1.0 + nrm((d,), 0.02)
    w_in0 = nrm((N_EVEN, d, IN0), d ** -0.5)
    w_out0 = nrm((N_EVEN, S5_WIDTH + RW_WIDTH, d), (S5_WIDTH + RW_WIDTH) ** -0.5)
    n_idx = jnp.arange(S5_N, dtype=F32)
    s5_a_re = -0.5 + nrm((N_EVEN, S5_GROUPS, S5_N), 0.01)
    s5_a_im = jnp.pi * n_idx + nrm((N_EVEN, S5_GROUPS, S5_N), 0.01)
    s5_b_re = nrm((N_EVEN, S5_GROUPS, S5_N, S5_GROUP), S5_GROUP ** -0.5)
    s5_b_im = nrm((N_EVEN, S5_GROUPS, S5_N, S5_GROUP), S5_GROUP ** -0.5)
    s5_c_re = nrm((N_EVEN, S5_GROUPS, S5_GROUP, S5_N), S5_N ** -0.5)
    s5_c_im = nrm((N_EVEN, S5_GROUPS, S5_GROUP, S5_N), S5_N ** -0.5)
    s5_d = nrm((N_EVEN, S5_WIDTH), 0.5)
    s5_log_dt = unif((N_EVEN, S5_GROUPS), math.log(0.001), math.log(0.1))
    s5_w_glu = nrm((N_EVEN, S5_WIDTH, S5_WIDTH), S5_WIDTH ** -0.5)
    rw_mu = unif((N_EVEN, RW_PROJ), 0.0, 1.0)
    rw_w0 = nrm((N_EVEN, RW_WIDTH), 0.5) - 1.0
    rw_w2 = nrm((N_EVEN, RW_LORA_W, RW_WIDTH), 0.1)
    rw_a0 = nrm((N_EVEN, RW_WIDTH), 0.1)
    rw_a2 = nrm((N_EVEN, RW_LORA_A, RW_WIDTH), 0.1)
    rw_g2 = nrm((N_EVEN, RW_LORA_G, RW_WIDTH), RW_LORA_G ** -0.5)
    rw_k_k = 0.85 + nrm((N_EVEN, RW_WIDTH), 0.02)
    rw_k_a = 1.0 + nrm((N_EVEN, RW_WIDTH), 0.02)
    rw_r_k = nrm((N_EVEN, RW_WIDTH), 0.1)
    rw_ln_w = 1.0 + nrm((N_EVEN, RW_WIDTH), 0.02)
    rw_ln_b = nrm((N_EVEN, RW_WIDTH), 0.01)
    w_in1 = nrm((N_ODD, d, IN1), d ** -0.5)
    w_out1 = nrm((N_ODD, RET_HEADS * RET_DV, d), (RET_HEADS * RET_DV) ** -0.5)
    mem_norm = 1.0 + nrm((DEPTH, d), 0.02)
    w_mq = nrm((DEPTH, d, d), d ** -0.5)
    w_mk = nrm((DEPTH, d, d), d ** -0.5)
    w_mv = nrm((DEPTH, d, d), d ** -0.5)
    w_mo = nrm((DEPTH, d, d), d ** -0.5)
    moe_w_rc = nrm((DEPTH, d, MOE_GROUPS), d ** -0.5)
    moe_b_rc = nrm((DEPTH, MOE_GROUPS), 0.01)
    moe_w_rf = nrm((DEPTH, d, MOE_EXPERTS), d ** -0.5)
    moe_b_rf = nrm((DEPTH, MOE_EXPERTS), 0.01)
    moe_w1 = nrm((DEPTH, MOE_EXPERTS, d, MOE_HIDDEN), d ** -0.5)
    moe_w3 = nrm((DEPTH, MOE_EXPERTS, d, MOE_HIDDEN), d ** -0.5)
    moe_w2 = nrm((DEPTH, MOE_EXPERTS, MOE_HIDDEN, d), MOE_HIDDEN ** -0.5)
    return {'x_prompt': x_prompt, 'x_sample': x_sample, 'mem_prompt': mem_prompt,
            'state_s5_re': state_s5_re, 'state_s5_im': state_s5_im, 'state_rwkv': state_rwkv,
            'state_shift': state_shift, 'state_ret': state_ret,
            'cache_mem_k': cache_mem_k, 'cache_mem_v': cache_mem_v,
            'norm_mix': norm_mix, 'norm_mem': norm_mem, 'norm_ffn': norm_ffn, 'norm_final': norm_final,
            'w_in0': w_in0, 'w_out0': w_out0,
            's5_a_re': s5_a_re, 's5_a_im': s5_a_im, 's5_b_re': s5_b_re, 's5_b_im': s5_b_im,
            's5_c_re': s5_c_re, 's5_c_im': s5_c_im, 's5_d': s5_d, 's5_log_dt': s5_log_dt, 's5_w_glu': s5_w_glu,
            'rw_mu': rw_mu, 'rw_w0': rw_w0, 'rw_w2': rw_w2, 'rw_a0': rw_a0, 'rw_a2': rw_a2, 'rw_g2': rw_g2,
            'rw_k_k': rw_k_k, 'rw_k_a': rw_k_a, 'rw_r_k': rw_r_k, 'rw_ln_w': rw_ln_w, 'rw_ln_b': rw_ln_b,
            'w_in1': w_in1, 'w_out1': w_out1,
            'mem_norm': mem_norm, 'w_mq': w_mq, 'w_mk': w_mk, 'w_mv': w_mv, 'w_mo': w_mo,
            'moe_w_rc': moe_w_rc, 'moe_b_rc': moe_b_rc, 'moe_w_rf': moe_w_rf, 'moe_b_rf': moe_b_rf,
            'moe_w1': moe_w1, 'moe_w3': moe_w3, 'moe_w2': moe_w2}


def reference(x_prompt, x_sample, mem_prompt, state_s5_re, state_s5_im, state_rwkv, state_shift, state_ret,
              cache_mem_k, cache_mem_v, norm_mix, norm_mem, norm_ffn, norm_final, w_in0, w_out0,
              s5_a_re, s5_a_im, s5_b_re, s5_b_im, s5_c_re, s5_c_im, s5_d, s5_log_dt, s5_w_glu,
              rw_mu, rw_w0, rw_w2, rw_a0, rw_a2, rw_g2, rw_k_k, rw_k_a, rw_r_k, rw_ln_w, rw_ln_b,
              w_in1, w_out1, mem_norm, w_mq, w_mk, w_mv, w_mo,
              moe_w_rc, moe_b_rc, moe_w_rf, moe_b_rf, moe_w1, moe_w3, moe_w2):
    w = dict(norm_mix=norm_mix, norm_mem=norm_mem, norm_ffn=norm_ffn, norm_final=norm_final,
             w_in0=w_in0, w_out0=w_out0, s5_a_re=s5_a_re, s5_a_im=s5_a_im, s5_b_re=s5_b_re, s5_b_im=s5_b_im,
             s5_c_re=s5_c_re, s5_c_im=s5_c_im, s5_d=s5_d, s5_log_dt=s5_log_dt, s5_w_glu=s5_w_glu,
             rw_mu=rw_mu, rw_w0=rw_w0, rw_w2=rw_w2, rw_a0=rw_a0, rw_a2=rw_a2, rw_g2=rw_g2,
             rw_k_k=rw_k_k, rw_k_a=rw_k_a, rw_r_k=rw_r_k, rw_ln_w=rw_ln_w, rw_ln_b=rw_ln_b,
             w_in1=w_in1, w_out1=w_out1, w_mq=w_mq, w_mo=w_mo,
             moe_w_rc=moe_w_rc, moe_b_rc=moe_b_rc, moe_w_rf=moe_w_rf, moe_b_rf=moe_b_rf,
             moe_w1=moe_w1, moe_w3=moe_w3, moe_w2=moe_w2)
    kv = [mem_kv(mem_prompt, mem_norm[l], w_mk[l], w_mv[l]) for l in range(DEPTH)]
    mem_k_p = jnp.stack([a for a, _ in kv])
    mem_v_p = jnp.stack([b for _, b in kv])
    y_prompt, s5r_p, s5i_p, rw_p, sh_p, ret_p = run_trunk(
        x_prompt, 0.0,
        jnp.zeros((N_EVEN, BATCH, S5_GROUPS, S5_N), F32), jnp.zeros((N_EVEN, BATCH, S5_GROUPS, S5_N), F32),
        jnp.zeros((N_EVEN, BATCH, RW_HEADS, RW_HD, RW_HD), F32), jnp.zeros((N_EVEN, BATCH, RW_PROJ), F32),
        jnp.zeros((N_ODD, BATCH, RET_HEADS, RET_DK, RET_DV), F32),
        mem_k_p, mem_v_p, w)
    y_sample, s5r_s, s5i_s, rw_s, sh_s, ret_s = run_trunk(
        x_sample, float(PAST_LEN), state_s5_re, state_s5_im, state_rwkv, state_shift, state_ret,
        cache_mem_k, cache_mem_v, w)
    return (y_prompt, y_sample, s5r_p, s5i_p, rw_p, sh_p, ret_p, mem_k_p, mem_v_p, s5r_s, s5i_s, rw_s, sh_s, ret_s)
```

```python
import functools
import math

import jax
import jax.numpy as jnp
from jax import lax
from jax.experimental import pallas as pl
from jax.experimental.pallas import tpu as pltpu

F32 = jnp.float32
BF16 = jnp.bfloat16

D_MODEL = 1024
DEPTH = 2
PAST_LEN = 16384
S5_WIDTH = 512
S5_GROUP = 16
S5_GROUPS = 32
S5_N = 64
S5_STATE = S5_GROUPS * S5_N
S5_GBLK = 8
RW_WIDTH = 512
RW_HD = 64
RW_HEADS = 8
RW_LORA = 256
RW_PROJ = 3 * RW_WIDTH + RW_LORA
IN0 = S5_WIDTH + RW_PROJ
RET_DK = 256
RET_HEADS = 4
RET_DV = 512
RET_CHUNK = 128
NQ = RET_HEADS * RET_DK
NV = RET_HEADS * RET_DV
IN1 = 2 * NQ + 2 * NV
N_MEM = 256
MEM_HEADS = 4
MEM_HD = 256
MOE_GROUPS = 4
MOE_PER_GROUP = 4
MOE_EXPERTS = 16
MOE_HIDDEN = 256
NORM_EPS = 1e-6
RW_GN_EPS = 64e-5
ROPE_BASE = 10000.0

VMEM_LIMIT = 56 * 1024 * 1024


def _cparams(*sem):
    return pltpu.CompilerParams(dimension_semantics=sem, vmem_limit_bytes=VMEM_LIMIT)


def _bdot(a, b):
    return jnp.dot(a.astype(BF16), b.astype(BF16), preferred_element_type=F32)


def _dot_nt(a, b):
    return lax.dot_general(a.astype(BF16), b.astype(BF16), (((1,), (1,)), ((), ())),
                           preferred_element_type=F32)


def _dot_tn(a, b):
    return lax.dot_general(a.astype(BF16), b.astype(BF16), (((0,), (0,)), ((), ())),
                           preferred_element_type=F32)


def _split3(x):
    hi = x.astype(BF16)
    r1 = x - hi.astype(F32)
    mid = r1.astype(BF16)
    lo = (r1 - mid.astype(F32)).astype(BF16)
    return hi, mid, lo


def _dot_exact_rhs(x, m_bf16):
    hi, mid, lo = _split3(x)
    acc = jnp.dot(hi, m_bf16, preferred_element_type=F32)
    acc = acc + jnp.dot(mid, m_bf16, preferred_element_type=F32)
    return acc + jnp.dot(lo, m_bf16, preferred_element_type=F32)


def _rms(x, g):
    ms = jnp.mean(x * x, axis=-1, keepdims=True)
    return x * lax.rsqrt(ms + NORM_EPS) * g


def _linear_kernel(*refs, norm, two, res):
    it = iter(refs)
    x_ref = next(it)
    g_ref = next(it) if norm else None
    w_ref = next(it)
    x2_ref = next(it) if two else None
    w2_ref = next(it) if two else None
    r_ref = next(it) if res else None
    o_refs = list(it)
    x = x_ref[...].astype(F32)
    if norm:
        x = _rms(x, g_ref[...])
    xb = x.astype(BF16)
    x2b = x2_ref[...].astype(BF16) if two else None
    col = 0
    for o_ref in o_refs:
        m = o_ref.shape[-1]
        step = next((s for s in (512, 256) if m % s == 0), m)
        for j in range(m // step):
            sl = slice(col + j * step, col + (j + 1) * step)
            acc = jnp.dot(xb, w_ref[:, sl], preferred_element_type=F32)
            if two:
                acc = acc + jnp.dot(x2b, w2_ref[:, sl], preferred_element_type=F32)
            if res:
                acc = acc + r_ref[:, sl]
            o_ref[:, j * step:(j + 1) * step] = acc.astype(o_ref.dtype)
        col += m


def _row_spec(tm, width, tmajor_b):
    if tmajor_b is None:
        return pl.BlockSpec((tm, width), lambda i: (i, 0))
    nb, tiles_per_b = tmajor_b
    return pl.BlockSpec((tm, width), lambda i: (i % tiles_per_b, i // tiles_per_b))


def linear(x, w, *, gain=None, x2=None, w2=None, residual=None, out_dtype=F32, tm=512,
           x_tmajor=False, out_tmajor=False, batch=None, splits=None, name="linear"):
    if x_tmajor:
        t_len, nb, k = x.shape
        n = t_len * nb
    else:
        n, k = x.shape
        nb = batch
        t_len = n // nb if nb else None
    m = w.shape[1]
    tm = min(tm, n if not (x_tmajor or out_tmajor) else t_len)
    assert n % tm == 0
    tiles_per_b = (t_len // tm) if (x_tmajor or out_tmajor) else None
    args, specs = [], []

    def add_rows(a, tmajor):
        width = a.shape[-1]
        args.append(a.reshape(t_len, nb * width) if tmajor else a)
        specs.append(_row_spec(tm, width, (nb, tiles_per_b) if tmajor else None))

    add_rows(x, x_tmajor)
    if gain is not None:
        args.append(gain.reshape(1, k).astype(F32))
        specs.append(pl.BlockSpec((1, k), lambda i: (0, 0)))
    args.append(w)
    specs.append(pl.BlockSpec(w.shape, lambda i: (0, 0)))
    if x2 is not None:
        add_rows(x2, x_tmajor)
        args.append(w2)
        specs.append(pl.BlockSpec(w2.shape, lambda i: (0, 0)))
    if residual is not None:
        add_rows(residual, False)
    widths = tuple(splits) if splits else (m,)
    assert sum(widths) == m
    if out_tmajor:
        out_shape = [jax.ShapeDtypeStruct((t_len, nb * mw), out_dtype) for mw in widths]
    else:
        out_shape = [jax.ShapeDtypeStruct((n, mw), out_dtype) for mw in widths]
    out_specs = [_row_spec(tm, mw, (nb, tiles_per_b) if out_tmajor else None) for mw in widths]
    kern = functools.partial(_linear_kernel, norm=gain is not None, two=x2 is not None,
                             res=residual is not None)
    outs = pl.pallas_call(
        kern, grid=(n // tm,), in_specs=specs, out_specs=out_specs, out_shape=out_shape,
        compiler_params=_cparams("parallel"), name=name)(*args)
    if out_tmajor:
        outs = [o.reshape(t_len, nb, mw) for o, mw in zip(outs, widths)]
    return outs if splits else outs[0]


def _s5_kernel(u_ref, h_re_ref, h_im_ref, abar_re_ref, abar_im_ref, bb_re_ref, bb_im_ref,
               cc_re_ref, cc_im_ref, d_ref, wglu_ref, y_ref, s_re_ref, s_im_ref,
               x_re, x_im, st_re, st_im, *, tc, nb):
    c = pl.program_id(0)
    rows = tc * nb
    nblk = S5_GROUPS // S5_GBLK
    bw_in = S5_GBLK * S5_GROUP
    bw_st = S5_GBLK * S5_N

    @pl.when(c == 0)
    def _():
        st_re[...] = h_re_ref[...]
        st_im[...] = h_im_ref[...]

    u = u_ref[...].reshape(rows, S5_WIDTH)
    ub = u.astype(BF16)
    for gb in range(nblk):
        ui = ub[:, gb * bw_in:(gb + 1) * bw_in]
        x_re[:, gb * bw_st:(gb + 1) * bw_st] = jnp.dot(ui, bb_re_ref[gb], preferred_element_type=F32)
        x_im[:, gb * bw_st:(gb + 1) * bw_st] = jnp.dot(ui, bb_im_ref[gb], preferred_element_type=F32)

    lane_blk = 1024
    for lb in range(S5_STATE // lane_blk):
        sl = slice(lb * lane_blk, (lb + 1) * lane_blk)
        ar = jnp.broadcast_to(abar_re_ref[:, sl], (nb, lane_blk))
        ai = jnp.broadcast_to(abar_im_ref[:, sl], (nb, lane_blk))

        def body(t, carry, sl=sl, ar=ar, ai=ai):
            xr, xi = carry
            r0 = pl.multiple_of(t * nb, nb)
            br = x_re[pl.ds(r0, nb), sl]
            bi = x_im[pl.ds(r0, nb), sl]
            nr = ar * xr - ai * xi + br
            ni = ar * xi + ai * xr + bi
            x_re[pl.ds(r0, nb), sl] = nr
            x_im[pl.ds(r0, nb), sl] = ni
            return nr, ni

        fr, fi = lax.fori_loop(0, tc, body, (st_re[:, sl], st_im[:, sl]), unroll=min(tc, 4))
        st_re[:, sl] = fr
        st_im[:, sl] = fi

    for gb in range(nblk):
        xr = x_re[:, gb * bw_st:(gb + 1) * bw_st].astype(BF16)
        xi = x_im[:, gb * bw_st:(gb + 1) * bw_st].astype(BF16)
        yb = (jnp.dot(xr, cc_re_ref[gb], preferred_element_type=F32)
              - jnp.dot(xi, cc_im_ref[gb], preferred_element_type=F32))
        cs = slice(gb * bw_in, (gb + 1) * bw_in)
        yb = yb + d_ref[:, cs] * u[:, cs]
        x_re[:, cs] = jax.nn.gelu(yb)
    y = x_re[:, :S5_WIDTH]
    y = y * jax.nn.sigmoid(jnp.dot(y.astype(BF16), wglu_ref[...], preferred_element_type=F32))
    y_ref[...] = y.reshape(y_ref.shape).astype(y_ref.dtype)

    @pl.when(c == pl.num_programs(0) - 1)
    def _():
        s_re_ref[...] = st_re[...]
        s_im_ref[...] = st_im[...]


def _s5_params(a_re, a_im, b_re, b_im, c_re, c_im, log_dt):
    dt = jnp.exp(log_dt.astype(F32))[:, None]
    ar, ai = a_re.astype(F32), a_im.astype(F32)
    mag = jnp.exp(dt * ar)
    abar_re, abar_im = mag * jnp.cos(dt * ai), mag * jnp.sin(dt * ai)
    den = ar * ar + ai * ai
    nr = abar_re - 1.0
    coef_re = (nr * ar + abar_im * ai) / den
    coef_im = (abar_im * ar - nr * ai) / den
    cr, ci = coef_re[..., None], coef_im[..., None]
    brf, bif = b_re.astype(F32), b_im.astype(F32)
    bb_re = cr * brf - ci * bif
    bb_im = cr * bif + ci * brf
    nblk = S5_GROUPS // S5_GBLK
    eye = jnp.eye(S5_GBLK, dtype=F32)

    def blockdiag_in(bb):
        t = jnp.transpose(bb, (0, 2, 1)).reshape(nblk, S5_GBLK, S5_GROUP, S5_N)
        m = jnp.einsum('kgcn,gh->kgchn', t, eye)
        return m.reshape(nblk, S5_GBLK * S5_GROUP, S5_GBLK * S5_N).astype(BF16)

    def blockdiag_out(cc):
        t = jnp.transpose(cc.astype(F32), (0, 2, 1)).reshape(nblk, S5_GBLK, S5_N, S5_GROUP)
        m = jnp.einsum('khnc,hg->khngc', t, eye)
        return m.reshape(nblk, S5_GBLK * S5_N, S5_GBLK * S5_GROUP).astype(BF16)

    return (abar_re.reshape(1, S5_STATE), abar_im.reshape(1, S5_STATE),
            blockdiag_in(bb_re), blockdiag_in(bb_im), blockdiag_out(c_re), blockdiag_out(c_im))


def s5_mixer(proj_tm, h_re, h_im, params, d_skip, w_glu, *, tc):
    t_len, nb, _ = proj_tm.shape
    abar_re, abar_im, bb_re, bb_im, cc_re, cc_im = params
    tc = min(tc, t_len)
    assert t_len % tc == 0 and nb % 8 == 0
    rows = tc * nb
    full = lambda a: pl.BlockSpec(a.shape, lambda c: (0,) * a.ndim)
    args = (proj_tm, h_re, h_im, abar_re, abar_im, bb_re, bb_im, cc_re, cc_im,
            d_skip.reshape(1, S5_WIDTH).astype(F32), w_glu.astype(BF16))
    in_specs = [pl.BlockSpec((tc, nb, S5_WIDTH), lambda c: (c, 0, 0))] + [full(a) for a in args[1:]]
    out_shape = (jax.ShapeDtypeStruct((t_len, nb, S5_WIDTH), BF16),
                 jax.ShapeDtypeStruct((nb, S5_STATE), F32),
                 jax.ShapeDtypeStruct((nb, S5_STATE), F32))
    out_specs = (pl.BlockSpec((tc, nb, S5_WIDTH), lambda c: (c, 0, 0)),
                 pl.BlockSpec((nb, S5_STATE), lambda c: (0, 0)),
                 pl.BlockSpec((nb, S5_STATE), lambda c: (0, 0)))
    scratch = [pltpu.VMEM((rows, S5_STATE), F32), pltpu.VMEM((rows, S5_STATE), F32),
               pltpu.VMEM((nb, S5_STATE), F32), pltpu.VMEM((nb, S5_STATE), F32)]
    return pl.pallas_call(
        functools.partial(_s5_kernel, tc=tc, nb=nb), grid=(t_len // tc,), in_specs=in_specs,
        out_specs=out_specs, out_shape=out_shape, scratch_shapes=scratch,
        compiler_params=_cparams("arbitrary"), name="s5_mixer")(*args)


def _head_ones():
    i = lax.broadcasted_iota(jnp.int32, (RW_WIDTH, RW_WIDTH), 0) // RW_HD
    j = lax.broadcasted_iota(jnp.int32, (RW_WIDTH, RW_WIDTH), 1) // RW_HD
    return jnp.where(i == j, 1.0, 0.0).astype(BF16)


def _softplus(z):
    return jnp.maximum(z, 0.0) + jnp.log1p(jnp.exp(-jnp.abs(z)))


def _rw_prep(p, p_prev, prm, ones_bd):
    mu, w0, w2, a0, a2, g2, k_k, k_a = prm
    xm = p + (p_prev - p) * mu
    o1, o2, o3 = RW_WIDTH, 2 * RW_WIDTH, 3 * RW_WIDTH
    r, k, v = xm[:, :o1], xm[:, o1:o2], xm[:, o2:o3]
    wd, ad, gd = xm[:, o3:o3 + 64], xm[:, o3 + 64:o3 + 128], xm[:, o3 + 128:]
    w = -_softplus(-(w0 + _bdot(jnp.tanh(wd), w2))) - 0.5
    lw = -jnp.exp(w)
    a = jax.nn.sigmoid(a0 + _bdot(ad, a2))
    g = _bdot(jax.nn.sigmoid(gd), g2)
    kk = k * k_k
    ss = _dot_exact_rhs(kk * kk, ones_bd)
    kk = kk / jnp.maximum(jnp.sqrt(ss), 1e-12)
    k = k * (1.0 + (a - 1.0) * k_a)
    return r, lw, k, v, -kk, kk * a, g


def _rw_post(o, r, k, v, g, r_k, ln_w, ln_b, ones_bd):
    inv = 1.0 / RW_HD
    mean = _dot_exact_rhs(o, ones_bd) * inv
    d = o - mean
    var = _dot_exact_rhs(d * d, ones_bd) * inv
    on = d * lax.rsqrt(var + RW_GN_EPS) * ln_w + ln_b
    bonus = _dot_exact_rhs(r * k * r_k, ones_bd) * v
    return (on + bonus) * g


def _rw_chunk_kernel(p_ref, shift_ref, s0_ref, mu_ref, w0_ref, w2_ref, a0_ref, a2_ref, g2_ref,
                     kk_ref, ka_ref, rk_ref, lnw_ref, lnb_ref,
                     y_ref, sfin_ref, shout_ref, prev_scr, s_scr, o_scr, *, c_len):
    c = pl.program_id(1)
    nc = pl.num_programs(1)

    @pl.when(c == 0)
    def _():
        prev_scr[...] = shift_ref[0]
        s_scr[...] = s0_ref[0]

    ones_bd = _head_ones()
    p = p_ref[...]
    rolled = pltpu.roll(p, 1, 0)
    row = lax.broadcasted_iota(jnp.int32, p.shape, 0)
    p_prev = jnp.where(row == 0, prev_scr[...], rolled)
    prev_scr[...] = p[c_len - 1:c_len, :]
    prm = (mu_ref[...], w0_ref[...], w2_ref[...], a0_ref[...], a2_ref[...], g2_ref[...],
           kk_ref[...], ka_ref[...])
    r, lw, k, v, a, b, g = _rw_prep(p, p_prev, prm, ones_bd)

    ti = lax.broadcasted_iota(jnp.int32, (c_len, c_len), 0)
    si = lax.broadcasted_iota(jnp.int32, (c_len, c_len), 1)
    tril_incl = ti >= si
    tril_strict = ti > si
    lmat = jnp.where(tril_incl, 1.0, 0.0).astype(BF16)
    l_hi, l_mid, l_lo = _split3(lw)
    cum = (jnp.dot(lmat, l_hi, preferred_element_type=F32)
           + jnp.dot(lmat, l_mid, preferred_element_type=F32)
           + jnp.dot(lmat, l_lo, preferred_element_type=F32))
    tot = cum[c_len - 1:c_len, :]
    e_cum = jnp.exp(cum)
    e_neg = jnp.exp(-cum)
    e_rem = jnp.exp(tot - cum)
    at = a * jnp.exp(cum - lw)
    rt = r * e_cum
    bt = b * e_neg
    kt = k * e_neg
    bh = b * e_rem
    kh = k * e_rem
    e_tot = jnp.exp(tot)
    eye = jnp.where(ti == si, 1.0, 0.0)

    for h in range(RW_HEADS):
        hs = slice(h * RW_HD, (h + 1) * RW_HD)
        s_h = s_scr[h]
        lhs = jnp.concatenate([at[:, hs], rt[:, hs]], axis=0)
        rhs = jnp.concatenate([bt[:, hs], kt[:, hs]], axis=0)
        aa = _dot_nt(lhs, rhs)
        n_ab = jnp.where(tril_strict, aa[:c_len, :c_len], 0.0)
        a_ak = jnp.where(tril_strict, aa[:c_len, c_len:], 0.0)
        a_rb = jnp.where(tril_incl, aa[c_len:, :c_len], 0.0)
        a_rk = jnp.where(tril_incl, aa[c_len:, c_len:], 0.0)
        tinv = eye + n_ab
        pw = n_ab
        for _ in range(int(math.log2(c_len)) - 1):
            pw = _bdot(pw, pw)
            tinv = _bdot(tinv, eye + pw)
        v_h = v[:, hs]
        u = _bdot(tinv, _dot_nt(at[:, hs], s_h) + _bdot(a_ak, v_h))
        o_h = _dot_nt(rt[:, hs], s_h) + _bdot(a_rb, u) + _bdot(a_rk, v_h)
        o_scr[:, hs] = o_h
        s_scr[h] = s_h * e_tot[:, hs] + _dot_tn(u, bh[:, hs]) + _dot_tn(v_h, kh[:, hs])

    y = _rw_post(o_scr[...], r, k, v, g, rk_ref[...], lnw_ref[...], lnb_ref[...], ones_bd)
    y_ref[...] = y.astype(y_ref.dtype)

    @pl.when(c == nc - 1)
    def _():
        sfin_ref[0] = s_scr[...]
        shout_ref[0] = p[c_len - 1:c_len, :]


def _rw_param_args(mu, w0, w2, a0, a2, g2, k_k, k_a, r_k, ln_w, ln_b):
    row = lambda z: z.reshape(1, -1).astype(F32)
    return (row(mu), row(w0), w2.astype(BF16), row(a0), a2.astype(BF16), g2.astype(BF16),
            row(k_k), row(k_a), row(r_k), row(ln_w), row(ln_b))


def rwkv_prompt(p_tm, shift, s0, params, *, c_len=64):
    t_len, nb, _ = p_tm.shape
    assert t_len % c_len == 0
    prm = _rw_param_args(*params)
    const = lambda a: pl.BlockSpec(a.shape, lambda b, c: (0,) * a.ndim)
    in_specs = [pl.BlockSpec((c_len, RW_PROJ), lambda b, c: (c, b)),
                pl.BlockSpec((1, 1, RW_PROJ), lambda b, c: (b, 0, 0)),
                pl.BlockSpec((1, RW_HEADS, RW_HD, RW_HD), lambda b, c: (b, 0, 0, 0))] + [const(a) for a in prm]
    out_shape = (jax.ShapeDtypeStruct((t_len, nb * RW_WIDTH), BF16),
                 jax.ShapeDtypeStruct((nb, RW_HEADS, RW_HD, RW_HD), F32),
                 jax.ShapeDtypeStruct((nb, 1, RW_PROJ), F32))
    out_specs = (pl.BlockSpec((c_len, RW_WIDTH), lambda b, c: (c, b)),
                 pl.BlockSpec((1, RW_HEADS, RW_HD, RW_HD), lambda b, c: (b, 0, 0, 0)),
                 pl.BlockSpec((1, 1, RW_PROJ), lambda b, c: (b, 0, 0)))
    scratch = [pltpu.VMEM((1, RW_PROJ), F32), pltpu.VMEM((RW_HEADS, RW_HD, RW_HD), F32),
               pltpu.VMEM((c_len, RW_WIDTH), F32)]
    y, s_fin, sh = pl.pallas_call(
        functools.partial(_rw_chunk_kernel, c_len=c_len), grid=(nb, t_len // c_len),
        in_specs=in_specs, out_specs=out_specs, out_shape=out_shape, scratch_shapes=scratch,
        compiler_params=_cparams("parallel", "arbitrary"), name="rwkv_prompt")(
            p_tm.reshape(t_len, nb * RW_PROJ), shift.reshape(nb, 1, RW_PROJ), s0, *prm)
    return y.reshape(t_len, nb, RW_WIDTH), s_fin, sh.reshape(nb, RW_PROJ)


def _rw_step_prep_kernel(p_ref, shift_ref, mu_ref, w0_ref, w2_ref, a0_ref, a2_ref, g2_ref, kk_ref, ka_ref,
                         r_ref, w_ref, k_ref, v_ref, a_ref, b_ref, g_ref):
    prm = (mu_ref[...], w0_ref[...], w2_ref[...], a0_ref[...], a2_ref[...], g2_ref[...],
           kk_ref[...], ka_ref[...])
    r, lw, k, v, a, b, g = _rw_prep(p_ref[...], shift_ref[...], prm, _head_ones())
    r_ref[...] = r
    w_ref[...] = jnp.exp(lw)
    k_ref[...] = k
    v_ref[...] = v
    a_ref[...] = a
    b_ref[...] = b
    g_ref[...] = g


def _rw_step_core_kernel(s_ref, r_ref, w_ref, k_ref, a_ref, b_ref, v_ref, s_out_ref, o_ref):
    s = s_ref[...]
    sa = jnp.sum(s * a_ref[...], axis=-1, keepdims=True)
    s_new = s * w_ref[...] + sa * b_ref[...] + v_ref[...] * k_ref[...]
    s_out_ref[...] = s_new
    o_ref[...] = jnp.sum(s_new * r_ref[...], axis=-1, keepdims=True)


def _rw_step_post_kernel(o_ref, r_ref, k_ref, v_ref, g_ref, rk_ref, lnw_ref, lnb_ref, y_ref):
    y_ref[...] = _rw_post(o_ref[...], r_ref[...], k_ref[...], v_ref[...], g_ref[...],
                          rk_ref[...], lnw_ref[...], lnb_ref[...], _head_ones()).astype(y_ref.dtype)


def rwkv_step(p, shift, s0, params, *, bt=8):
    n = p.shape[0]
    prm = _rw_param_args(*params)
    vec = jax.ShapeDtypeStruct((n, RW_WIDTH), F32)
    r, w, k, v, a, b, g = pl.pallas_call(
        _rw_step_prep_kernel, out_shape=(vec,) * 7, name="rwkv_step_prep")(p, shift, *prm[:8])
    rows = lambda z: z.reshape(n, RW_HEADS, 1, RW_HD)
    row_spec = pl.BlockSpec((bt, RW_HEADS, 1, RW_HD), lambda i: (i, 0, 0, 0))
    col_spec = pl.BlockSpec((bt, RW_HEADS, RW_HD, 1), lambda i: (i, 0, 0, 0))
    st_spec = pl.BlockSpec((bt, RW_HEADS, RW_HD, RW_HD), lambda i: (i, 0, 0, 0))
    s_new, o = pl.pallas_call(
        _rw_step_core_kernel, grid=(n // bt,),
        in_specs=[st_spec] + [row_spec] * 5 + [col_spec], out_specs=(st_spec, col_spec),
        out_shape=(jax.ShapeDtypeStruct(s0.shape, F32), jax.ShapeDtypeStruct((n, RW_HEADS, RW_HD, 1), F32)),
        compiler_params=_cparams("parallel"), name="rwkv_step_core")(
            s0, rows(r), rows(w), rows(k), rows(a), rows(b), v.reshape(n, RW_HEADS, RW_HD, 1))
    y = pl.pallas_call(
        _rw_step_post_kernel, out_shape=jax.ShapeDtypeStruct((n, RW_WIDTH), BF16), name="rwkv_step_post")(
            o.reshape(n, RW_WIDTH), r, k, v, g, *prm[8:])
    return y, s_new


RET_LOG_G = tuple(math.log(1.0 - 2.0 ** (-5.0 - h)) for h in range(RET_HEADS))


def _rope_tables(pos, half):
    j = lax.broadcasted_iota(jnp.int32, (1, half), 1).astype(F32)
    inv = jnp.exp(j * (-math.log(ROPE_BASE) / half))
    ang = pos * inv
    return jnp.cos(ang), jnp.sin(ang)


def _rope(x, cos, sin):
    half = RET_DK // 2
    outs = []
    for h in range(RET_HEADS):
        x1 = x[:, h * RET_DK:h * RET_DK + half]
        x2 = x[:, h * RET_DK + half:(h + 1) * RET_DK]
        outs += [x1 * cos - x2 * sin, x1 * sin + x2 * cos]
    return jnp.concatenate(outs, axis=-1)


def _ret_norm_gate(o, g):
    o = o * lax.rsqrt(jnp.mean(o * o, axis=-1, keepdims=True) + NORM_EPS)
    return jax.nn.silu(g) * o


def _ret_chunk_kernel(q_ref, k_ref, v_ref, g_ref, y_ref, sfin_ref, s_scr, *, c_len):
    c = pl.program_id(1)

    @pl.when(c == 0)
    def _():
        s_scr[...] = jnp.zeros_like(s_scr)

    half = RET_DK // 2
    ti = lax.broadcasted_iota(jnp.int32, (c_len, 1), 0).astype(F32)
    pos = (c * c_len).astype(F32) + ti
    cos, sin = _rope_tables(pos, half)
    q = _rope(q_ref[...].astype(F32), cos, sin)
    k = _rope(k_ref[...].astype(F32), cos, sin) * (RET_DK ** -0.5)
    ii = lax.broadcasted_iota(jnp.int32, (c_len, c_len), 0)
    jj = lax.broadcasted_iota(jnp.int32, (c_len, c_len), 1)
    diff = (ii - jj).astype(F32)
    for h in range(RET_HEADS):
        lg = RET_LOG_G[h]
        dmask = jnp.where(diff >= 0, jnp.exp(lg * jnp.maximum(diff, 0.0)), 0.0)
        q_dec = jnp.exp(lg * (ti + 1.0))
        k_dec = jnp.exp(lg * (c_len - 1.0 - ti))
        c_dec = math.exp(lg * c_len)
        qh = q[:, h * RET_DK:(h + 1) * RET_DK]
        kh = k[:, h * RET_DK:(h + 1) * RET_DK]
        vh = v_ref[:, h * RET_DV:(h + 1) * RET_DV]
        s_h = s_scr[h]
        sc = _dot_nt(qh, kh) * dmask
        o = _bdot(sc, vh) + _bdot(qh * q_dec, s_h)
        s_scr[h] = s_h * c_dec + _dot_tn(kh * k_dec, vh)
        gh = g_ref[:, h * RET_DV:(h + 1) * RET_DV].astype(F32)
        y_ref[:, h * RET_DV:(h + 1) * RET_DV] = _ret_norm_gate(o, gh).astype(y_ref.dtype)

    @pl.when(c == pl.num_programs(1) - 1)
    def _():
        sfin_ref[0] = s_scr[...]


def retention_prompt(q, k, v, g, *, nb, c_len=RET_CHUNK):
    n = q.shape[0]
    t_len = n // nb
    nc = t_len // c_len
    spec = lambda w: pl.BlockSpec((c_len, w), lambda b, c: (b * nc + c, 0))
    st_spec = pl.BlockSpec((1, RET_HEADS, RET_DK, RET_DV), lambda b, c: (b, 0, 0, 0))
    return pl.pallas_call(
        functools.partial(_ret_chunk_kernel, c_len=c_len), grid=(nb, nc),
        in_specs=[spec(NQ), spec(NQ), spec(NV), spec(NV)], out_specs=(spec(NV), st_spec),
        out_shape=(jax.ShapeDtypeStruct((n, NV), BF16),
                   jax.ShapeDtypeStruct((nb, RET_HEADS, RET_DK, RET_DV), F32)),
        scratch_shapes=[pltpu.VMEM((RET_HEADS, RET_DK, RET_DV), F32)],
        compiler_params=_cparams("parallel", "arbitrary"), name="retention_prompt")(q, k, v, g)


def _ret_step_rope_kernel(q_ref, k_ref, qo_ref, ko_ref, *, pos0):
    pos = jnp.full((q_ref.shape[0], 1), pos0, F32)
    cos, sin = _rope_tables(pos, RET_DK // 2)
    qo_ref[...] = _rope(q_ref[...].astype(F32), cos, sin)
    ko_ref[...] = _rope(k_ref[...].astype(F32), cos, sin) * (RET_DK ** -0.5)


def _ret_step_core_kernel(s_ref, q_ref, k_ref, v_ref, g_ref, s_out_ref, y_ref):
    for h in range(RET_HEADS):
        gam = math.exp(RET_LOG_G[h])
        s_h = s_ref[0, h]
        qc = q_ref[0, h]
        kc = k_ref[0, h]
        vr = v_ref[0, h].astype(F32)
        qk = jnp.sum(qc * kc, axis=0, keepdims=True)
        o = qk * vr + jnp.sum((qc * gam) * s_h, axis=0, keepdims=True)
        s_out_ref[0, h] = s_h * gam + kc * vr
        y_ref[0, h] = _ret_norm_gate(o, g_ref[0, h].astype(F32)).astype(y_ref.dtype)


def retention_step(q, k, v, g, s0, *, pos0):
    n = q.shape[0]
    vec = jax.ShapeDtypeStruct((n, NQ), F32)
    qr, kr = pl.pallas_call(functools.partial(_ret_step_rope_kernel, pos0=pos0), out_shape=(vec, vec),
                            name="retention_step_rope")(q, k)
    col = lambda z: z.reshape(n, RET_HEADS, RET_DK, 1)
    row = lambda z: z.reshape(n, RET_HEADS, 1, RET_DV)
    st_spec = pl.BlockSpec((1, RET_HEADS, RET_DK, RET_DV), lambda i: (i, 0, 0, 0))
    col_spec = pl.BlockSpec((1, RET_HEADS, RET_DK, 1), lambda i: (i, 0, 0, 0))
    row_spec = pl.BlockSpec((1, RET_HEADS, 1, RET_DV), lambda i: (i, 0, 0, 0))
    s_new, y = pl.pallas_call(
        _ret_step_core_kernel, grid=(n,),
        in_specs=[st_spec, col_spec, col_spec, row_spec, row_spec], out_specs=(st_spec, row_spec),
        out_shape=(jax.ShapeDtypeStruct(s0.shape, F32), jax.ShapeDtypeStruct((n, RET_HEADS, 1, RET_DV), BF16)),
        compiler_params=_cparams("parallel"), name="retention_step_core")(
            s0, col(qr), col(kr), row(v), row(g))
    return y.reshape(n, NV), s_new


def _xattn_prompt_kernel(x_ref, g_ref, wq_ref, mk_ref, mv_ref, wo_ref, o_ref, att_scr):
    x = x_ref[...]
    q = jnp.dot(_rms(x, g_ref[...]).astype(BF16), wq_ref[...], preferred_element_type=F32)
    for h in range(MEM_HEADS):
        hs = slice(h * MEM_HD, (h + 1) * MEM_HD)
        s = _dot_nt(q[:, hs], mk_ref[0, :, hs]) * (MEM_HD ** -0.5)
        s = s - jnp.max(s, axis=-1, keepdims=True)
        e = jnp.exp(s)
        p = e / jnp.sum(e, axis=-1, keepdims=True)
        att_scr[:, hs] = _bdot(p, mv_ref[0, :, hs])
    o_ref[...] = x + jnp.dot(att_scr[...].astype(BF16), wo_ref[...], preferred_element_type=F32)


def xattn_prompt(x, gain, w_q, mem_k, mem_v, w_o, *, nb, tm=512):
    n = x.shape[0]
    tiles_per_b = n // nb // tm
    row = pl.BlockSpec((tm, D_MODEL), lambda i: (i, 0))
    wspec = pl.BlockSpec((D_MODEL, D_MODEL), lambda i: (0, 0))
    mspec = pl.BlockSpec((1, N_MEM, D_MODEL), lambda i: (i // tiles_per_b, 0, 0))
    return pl.pallas_call(
        _xattn_prompt_kernel, grid=(n // tm,),
        in_specs=[row, pl.BlockSpec((1, D_MODEL), lambda i: (0, 0)), wspec, mspec, mspec, wspec],
        out_specs=row, out_shape=jax.ShapeDtypeStruct((n, D_MODEL), F32),
        scratch_shapes=[pltpu.VMEM((tm, D_MODEL), F32)],
        compiler_params=_cparams("parallel"), name="xattn_prompt")(
            x, gain.reshape(1, D_MODEL), w_q, mem_k, mem_v, w_o)


def _xattn_step_kernel(q_ref, mk_ref, mv_ref, o_ref, *, tb):
    for i in range(tb):
        q = q_ref[i]
        for h in range(MEM_HEADS):
            hs = slice(h * MEM_HD, (h + 1) * MEM_HD)
            s = jnp.sum(mk_ref[0, i, :, hs] * q[:, hs], axis=-1, keepdims=True) * (MEM_HD ** -0.5)
            s = s - jnp.max(s, axis=0, keepdims=True)
            e = jnp.exp(s)
            p = e / jnp.sum(e, axis=0, keepdims=True)
            o_ref[i, :, hs] = jnp.sum(p * mv_ref[0, i, :, hs], axis=0, keepdims=True)


def xattn_step(q, cache_k, cache_v, layer, *, tb=4):
    n = q.shape[0]
    qspec = pl.BlockSpec((tb, 1, D_MODEL), lambda i: (i, 0, 0))
    cspec = pl.BlockSpec((1, tb, N_MEM, D_MODEL), lambda i: (layer, i, 0, 0))
    o = pl.pallas_call(
        functools.partial(_xattn_step_kernel, tb=tb), grid=(n // tb,),
        in_specs=[qspec, cspec, cspec], out_specs=qspec,
        out_shape=jax.ShapeDtypeStruct((n, 1, D_MODEL), F32),
        compiler_params=_cparams("parallel"), name="xattn_step")(q.reshape(n, 1, D_MODEL), cache_k, cache_v)
    return o.reshape(n, D_MODEL)


ROUTER_LANES = 128
NEG_BIG = -1e30


def _moe_gates(logits):
    lane = lax.broadcasted_iota(jnp.int32, logits.shape, 1)
    first = lambda mask: jnp.min(jnp.where(mask, lane, ROUTER_LANES), axis=-1, keepdims=True)
    is_c = lane < MOE_GROUPS
    lc = jnp.where(is_c, logits, NEG_BIG)
    mc = jnp.max(lc, axis=-1, keepdims=True)
    g_idx = first(lc == mc)
    p_g = 1.0 / jnp.sum(jnp.where(is_c, jnp.exp(lc - mc), 0.0), axis=-1, keepdims=True)
    fl = lane - MOE_GROUPS
    in_g = (fl >= 0) & (fl < MOE_EXPERTS) & ((fl // MOE_PER_GROUP) == g_idx)
    lf = jnp.where(in_g, logits, NEG_BIG)
    m1 = jnp.max(lf, axis=-1, keepdims=True)
    i1 = first(lf == m1)
    lf2 = jnp.where(lane == i1, NEG_BIG, lf)
    m2 = jnp.max(lf2, axis=-1, keepdims=True)
    i2 = first(lf2 == m2)
    e2 = jnp.exp(m2 - m1)
    w_top = 1.0 / (1.0 + e2)
    return p_g * (jnp.where(lane == i1, w_top, 0.0) + jnp.where(lane == i2, e2 * w_top, 0.0))


def _moe_kernel(x_ref, g_ref, wr_ref, br_ref, w1_ref, w3_ref, w2_ref, o_ref, h_scr, gate_scr):
    e = pl.program_id(1)

    @pl.when(e == 0)
    def _():
        x = x_ref[...]
        h = _rms(x, g_ref[...])
        h_scr[...] = h.astype(BF16)
        logits = jnp.dot(h, wr_ref[...], preferred_element_type=F32,
                         precision=lax.Precision.HIGHEST) + br_ref[...]
        gate_scr[...] = _moe_gates(logits)
        o_ref[...] = x

    hb = h_scr[...]
    a1 = jnp.dot(hb, w1_ref[0], preferred_element_type=F32)
    a3 = jnp.dot(hb, w3_ref[0], preferred_element_type=F32)
    lane = lax.broadcasted_iota(jnp.int32, gate_scr.shape, 1)
    ge = jnp.sum(jnp.where(lane == e + MOE_GROUPS, gate_scr[...], 0.0), axis=-1, keepdims=True)
    hid = jax.nn.silu(a1) * a3 * ge
    o_ref[...] += jnp.dot(hid.astype(BF16), w2_ref[0], preferred_element_type=F32)


def moe_dense(x, gain, w_r, b_r, w1, w3, w2, *, tm=512):
    n = x.shape[0]
    tm = min(tm, n)
    gain = gain.reshape(1, D_MODEL)
    row = pl.BlockSpec((tm, D_MODEL), lambda i, e: (i, 0))
    const2 = lambda a: pl.BlockSpec(a.shape, lambda i, e: (0, 0))
    return pl.pallas_call(
        _moe_kernel, grid=(n // tm, MOE_EXPERTS),
        in_specs=[row, const2(gain), const2(w_r), const2(b_r),
                  pl.BlockSpec((1, D_MODEL, MOE_HIDDEN), lambda i, e: (e, 0, 0)),
                  pl.BlockSpec((1, D_MODEL, MOE_HIDDEN), lambda i, e: (e, 0, 0)),
                  pl.BlockSpec((1, MOE_HIDDEN, D_MODEL), lambda i, e: (e, 0, 0))],
        out_specs=row, out_shape=jax.ShapeDtypeStruct((n, D_MODEL), F32),
        scratch_shapes=[pltpu.VMEM((tm, D_MODEL), BF16), pltpu.VMEM((tm, ROUTER_LANES), F32)],
        compiler_params=_cparams("parallel", "arbitrary"), name="moe")(x, gain, w_r, b_r, w1, w3, w2)


def _router_params(w_rc, b_rc, w_rf, b_rf):
    pad = ROUTER_LANES - MOE_GROUPS - MOE_EXPERTS
    w_r = jnp.concatenate([w_rc, w_rf, jnp.zeros((D_MODEL, pad), F32)], axis=1).astype(F32)
    b_r = jnp.concatenate([b_rc, b_rf, jnp.zeros((pad,), F32)]).reshape(1, ROUTER_LANES).astype(F32)
    return w_r, b_r


def _rmsnorm_kernel(x_ref, g_ref, o_ref):
    o_ref[...] = _rms(x_ref[...], g_ref[...])


def rmsnorm_rows(x, gain, *, tm=1024):
    n = x.shape[0]
    tm = min(tm, n)
    row = pl.BlockSpec((tm, D_MODEL), lambda i: (i, 0))
    return pl.pallas_call(
        _rmsnorm_kernel, grid=(n // tm,), in_specs=[row, pl.BlockSpec((1, D_MODEL), lambda i: (0, 0))],
        out_specs=row, out_shape=jax.ShapeDtypeStruct((n, D_MODEL), F32),
        compiler_params=_cparams("parallel"), name="rmsnorm")(x, gain.reshape(1, D_MODEL))


def _stack(parts):
    return parts[0][None] if len(parts) == 1 else jnp.stack(parts)


def _trunk(x, nb, w, s5_re, s5_im, rw_state, rw_shift, ret_state, mem_k, mem_v, pos0):
    n = x.shape[0]
    t_len = n // nb
    single = t_len == 1
    out_s5_re, out_s5_im, out_rw, out_shift, out_ret = [], [], [], [], []
    for layer in range(DEPTH):
        if layer % 2 == 0:
            i = layer // 2
            s5p = _s5_params(w['s5_a_re'][i], w['s5_a_im'][i], w['s5_b_re'][i], w['s5_b_im'][i],
                             w['s5_c_re'][i], w['s5_c_im'][i], w['s5_log_dt'][i])
            rwp = tuple(w[k][i] for k in ('rw_mu', 'rw_w0', 'rw_w2', 'rw_a0', 'rw_a2', 'rw_g2',
                                          'rw_k_k', 'rw_k_a', 'rw_r_k', 'rw_ln_w', 'rw_ln_b'))
            w_in = w['w_in0'][i].astype(BF16)
            w_out = w['w_out0'][i].astype(BF16)
            if single:
                u, p = linear(x, w_in, gain=w['norm_mix'][layer], splits=(S5_WIDTH, RW_PROJ))
                y_s5, sr, si = s5_mixer(u.reshape(1, nb, S5_WIDTH), s5_re[i], s5_im[i], s5p,
                                        w['s5_d'][i], w['s5_w_glu'][i], tc=1)
                y_rw, srw = rwkv_step(p, rw_shift[i], rw_state[i], rwp)
                sh = p
                x = linear(y_s5.reshape(nb, S5_WIDTH), w_out[:S5_WIDTH], x2=y_rw, w2=w_out[S5_WIDTH:], residual=x)
            else:
                u, p = linear(x, w_in, gain=w['norm_mix'][layer], splits=(S5_WIDTH, RW_PROJ),
                              out_tmajor=True, batch=nb)
                y_s5, sr, si = s5_mixer(u, s5_re[i], s5_im[i], s5p, w['s5_d'][i], w['s5_w_glu'][i], tc=128)
                y_rw, srw, sh = rwkv_prompt(p, rw_shift[i], rw_state[i], rwp)
                x = linear(y_s5, w_out[:S5_WIDTH], x2=y_rw, w2=w_out[S5_WIDTH:], residual=x, x_tmajor=True)
            out_s5_re.append(sr.reshape(nb, S5_GROUPS, S5_N))
            out_s5_im.append(si.reshape(nb, S5_GROUPS, S5_N))
            out_rw.append(srw)
            out_shift.append(sh)
        else:
            j = layer // 2
            q, k, v, g = linear(x, w['w_in1'][j].astype(BF16), gain=w['norm_mix'][layer], out_dtype=BF16,
                                splits=(NQ, NQ, NV, NV), tm=256)
            if single:
                y, s_new = retention_step(q, k, v, g, ret_state[j], pos0=pos0)
            else:
                y, s_new = retention_prompt(q, k, v, g, nb=nb)
            x = linear(y, w['w_out1'][j].astype(BF16), residual=x)
            out_ret.append(s_new)
        w_q = w['w_mq'][layer].astype(BF16)
        w_o = w['w_mo'][layer].astype(BF16)
        if single:
            q = linear(x, w_q, gain=w['norm_mem'][layer])
            att = xattn_step(q, mem_k, mem_v, layer)
            x = linear(att, w_o, residual=x)
        else:
            x = xattn_prompt(x, w['norm_mem'][layer], w_q, mem_k[layer], mem_v[layer], w_o, nb=nb)
        w_r, b_r = _router_params(w['moe_w_rc'][layer], w['moe_b_rc'][layer],
                                  w['moe_w_rf'][layer], w['moe_b_rf'][layer])
        x = moe_dense(x, w['norm_ffn'][layer], w_r, b_r, w['moe_w1'][layer].astype(BF16),
                      w['moe_w3'][layer].astype(BF16), w['moe_w2'][layer].astype(BF16))
    y = rmsnorm_rows(x, w['norm_final'])
    return (y, _stack(out_s5_re), _stack(out_s5_im), _stack(out_rw), _stack(out_shift), _stack(out_ret))


def kernel(x_prompt, x_sample, mem_prompt, state_s5_re, state_s5_im, state_rwkv, state_shift, state_ret, cache_mem_k, cache_mem_v, norm_mix, norm_mem, norm_ffn, norm_final, w_in0, w_out0, s5_a_re, s5_a_im, s5_b_re, s5_b_im, s5_c_re, s5_c_im, s5_d, s5_log_dt, s5_w_glu, rw_mu, rw_w0, rw_w2, rw_a0, rw_a2, rw_g2, rw_k_k, rw_k_a, rw_r_k, rw_ln_w, rw_ln_b, w_in1, w_out1, mem_norm, w_mq, w_mk, w_mv, w_mo, moe_w_rc, moe_b_rc, moe_w_rf, moe_b_rf, moe_w1, moe_w3, moe_w2):
    w = dict(norm_mix=norm_mix, norm_mem=norm_mem, norm_ffn=norm_ffn, norm_final=norm_final,
             w_in0=w_in0, w_out0=w_out0, s5_a_re=s5_a_re, s5_a_im=s5_a_im, s5_b_re=s5_b_re, s5_b_im=s5_b_im,
             s5_c_re=s5_c_re, s5_c_im=s5_c_im, s5_d=s5_d, s5_log_dt=s5_log_dt, s5_w_glu=s5_w_glu,
             rw_mu=rw_mu, rw_w0=rw_w0, rw_w2=rw_w2, rw_a0=rw_a0, rw_a2=rw_a2, rw_g2=rw_g2,
             rw_k_k=rw_k_k, rw_k_a=rw_k_a, rw_r_k=rw_r_k, rw_ln_w=rw_ln_w, rw_ln_b=rw_ln_b,
             w_in1=w_in1, w_out1=w_out1, w_mq=w_mq, w_mo=w_mo,
             moe_w_rc=moe_w_rc, moe_b_rc=moe_b_rc, moe_w_rf=moe_w_rf, moe_b_rf=moe_b_rf,
             moe_w1=moe_w1, moe_w3=moe_w3, moe_w2=moe_w2)
    nbp, t_len, _ = x_prompt.shape
    nbs = x_sample.shape[0]
    n_even, n_odd = state_s5_re.shape[0], state_ret.shape[0]

    mem = mem_prompt.reshape(nbp * N_MEM, D_MODEL)
    mem_k_l, mem_v_l = [], []
    for layer in range(DEPTH):
        w_kv = jnp.concatenate([w_mk[layer], w_mv[layer]], axis=1).astype(BF16)
        mk, mv = linear(mem, w_kv, gain=mem_norm[layer], splits=(D_MODEL, D_MODEL))
        mem_k_l.append(mk.reshape(nbp, N_MEM, D_MODEL))
        mem_v_l.append(mv.reshape(nbp, N_MEM, D_MODEL))
    kv_shape = (DEPTH, nbp, N_MEM, MEM_HEADS, MEM_HD)
    mem_k_p = jnp.stack(mem_k_l).reshape(kv_shape)
    mem_v_p = jnp.stack(mem_v_l).reshape(kv_shape)

    zeros = lambda *shape: jnp.zeros(shape, F32)
    y_p, s5r_p, s5i_p, rw_p, sh_p, ret_p = _trunk(
        x_prompt.reshape(nbp * t_len, D_MODEL), nbp, w,
        zeros(n_even, nbp, S5_STATE), zeros(n_even, nbp, S5_STATE),
        zeros(n_even, nbp, RW_HEADS, RW_HD, RW_HD), zeros(n_even, nbp, RW_PROJ),
        None, mem_k_l, mem_v_l, 0.0)
    y_s, s5r_s, s5i_s, rw_s, sh_s, ret_s = _trunk(
        x_sample.reshape(nbs, D_MODEL), nbs, w,
        state_s5_re.reshape(n_even, nbs, S5_STATE), state_s5_im.reshape(n_even, nbs, S5_STATE),
        state_rwkv, state_shift, state_ret,
        cache_mem_k.reshape(DEPTH, nbs, N_MEM, D_MODEL), cache_mem_v.reshape(DEPTH, nbs, N_MEM, D_MODEL),
        float(PAST_LEN))
    return (y_p.reshape(nbp, t_len, D_MODEL), y_s.reshape(nbs, 1, D_MODEL),
            s5r_p, s5i_p, rw_p, sh_p, ret_p, mem_k_p, mem_v_p, s5r_s, s5i_s, rw_s, sh_s, ret_s)
```

```python
import functools
import math

import jax
import jax.numpy as jnp
from jax import lax
from jax.experimental import pallas as pl
from jax.experimental.pallas import tpu as pltpu

F32 = jnp.float32
BF16 = jnp.bfloat16

D_MODEL = 1024
DEPTH = 2
PAST_LEN = 16384
S5_WIDTH = 512
S5_GROUP = 16
S5_GROUPS = 32
S5_N = 64
S5_STATE = S5_GROUPS * S5_N
S5_GBLK = 8
RW_WIDTH = 512
RW_HD = 64
RW_HEADS = 8
RW_LORA = 256
RW_PROJ = 3 * RW_WIDTH + RW_LORA
IN0 = S5_WIDTH + RW_PROJ
RET_DK = 256
RET_HEADS = 4
RET_DV = 512
RET_CHUNK = 128
NQ = RET_HEADS * RET_DK
NV = RET_HEADS * RET_DV
IN1 = 2 * NQ + 2 * NV
N_MEM = 256
MEM_HEADS = 4
MEM_HD = 256
MOE_GROUPS = 4
MOE_PER_GROUP = 4
MOE_EXPERTS = 16
MOE_HIDDEN = 256
NORM_EPS = 1e-6
RW_GN_EPS = 64e-5
ROPE_BASE = 10000.0

VMEM_LIMIT = 56 * 1024 * 1024


def _cparams(*sem):
    return pltpu.CompilerParams(dimension_semantics=sem, vmem_limit_bytes=VMEM_LIMIT)


def _bdot(a, b):
    return jnp.dot(a.astype(BF16), b.astype(BF16), preferred_element_type=F32)


def _dot_nt(a, b):
    return lax.dot_general(a.astype(BF16), b.astype(BF16), (((1,), (1,)), ((), ())),
                           preferred_element_type=F32)


def _dot_tn(a, b):
    return lax.dot_general(a.astype(BF16), b.astype(BF16), (((0,), (0,)), ((), ())),
                           preferred_element_type=F32)


def _split3(x):
    hi = x.astype(BF16)
    r1 = x - hi.astype(F32)
    mid = r1.astype(BF16)
    lo = (r1 - mid.astype(F32)).astype(BF16)
    return hi, mid, lo


def _dot_exact_rhs(x, m_bf16):
    hi, mid, lo = _split3(x)
    acc = jnp.dot(hi, m_bf16, preferred_element_type=F32)
    acc = acc + jnp.dot(mid, m_bf16, preferred_element_type=F32)
    return acc + jnp.dot(lo, m_bf16, preferred_element_type=F32)


def _rms(x, g):
    ms = jnp.mean(x * x, axis=-1, keepdims=True)
    return x * lax.rsqrt(ms + NORM_EPS) * g


def _linear_kernel(*refs, norm, two, res):
    it = iter(refs)
    x_ref = next(it)
    g_ref = next(it) if norm else None
    w_ref = next(it)
    x2_ref = next(it) if two else None
    w2_ref = next(it) if two else None
    r_ref = next(it) if res else None
    o_refs = list(it)
    x = x_ref[...].astype(F32)
    if norm:
        x = _rms(x, g_ref[...])
    xb = x.astype(BF16)
    x2b = x2_ref[...].astype(BF16) if two else None
    col = 0
    for o_ref in o_refs:
        m = o_ref.shape[-1]
        step = next((s for s in (512, 256) if m % s == 0), m)
        for j in range(m // step):
            sl = slice(col + j * step, col + (j + 1) * step)
            acc = jnp.dot(xb, w_ref[:, sl], preferred_element_type=F32)
            if two:
                acc = acc + jnp.dot(x2b, w2_ref[:, sl], preferred_element_type=F32)
            if res:
                acc = acc + r_ref[:, sl]
            o_ref[:, j * step:(j + 1) * step] = acc.astype(o_ref.dtype)
        col += m


def _row_spec(tm, width, tmajor_b):
    if tmajor_b is None:
        return pl.BlockSpec((tm, width), lambda i: (i, 0))
    nb, tiles_per_b = tmajor_b
    return pl.BlockSpec((tm, width), lambda i: (i % tiles_per_b, i // tiles_per_b))


def linear(x, w, *, gain=None, x2=None, w2=None, residual=None, out_dtype=F32, tm=512,
           x_tmajor=False, out_tmajor=False, batch=None, splits=None, name="linear"):
    if x_tmajor:
        t_len, nb, k = x.shape
        n = t_len * nb
    else:
        n, k = x.shape
        nb = batch
        t_len = n // nb if nb else None
    m = w.shape[1]
    tm = min(tm, n if not (x_tmajor or out_tmajor) else t_len)
    assert n % tm == 0
    tiles_per_b = (t_len // tm) if (x_tmajor or out_tmajor) else None
    args, specs = [], []

    def add_rows(a, tmajor):
        width = a.shape[-1]
        args.append(a.reshape(t_len, nb * width) if tmajor else a)
        specs.append(_row_spec(tm, width, (nb, tiles_per_b) if tmajor else None))

    add_rows(x, x_tmajor)
    if gain is not None:
        args.append(gain.reshape(1, k).astype(F32))
        specs.append(pl.BlockSpec((1, k), lambda i: (0, 0)))
    args.append(w)
    specs.append(pl.BlockSpec(w.shape, lambda i: (0, 0)))
    if x2 is not None:
        add_rows(x2, x_tmajor)
        args.append(w2)
        specs.append(pl.BlockSpec(w2.shape, lambda i: (0, 0)))
    if residual is not None:
        add_rows(residual, False)
    widths = tuple(splits) if splits else (m,)
    assert sum(widths) == m
    if out_tmajor:
        out_shape = [jax.ShapeDtypeStruct((t_len, nb * mw), out_dtype) for mw in widths]
    else:
        out_shape = [jax.ShapeDtypeStruct((n, mw), out_dtype) for mw in widths]
    out_specs = [_row_spec(tm, mw, (nb, tiles_per_b) if out_tmajor else None) for mw in widths]
    kern = functools.partial(_linear_kernel, norm=gain is not None, two=x2 is not None,
                             res=residual is not None)
    outs = pl.pallas_call(
        kern, grid=(n // tm,), in_specs=specs, out_specs=out_specs, out_shape=out_shape,
        compiler_params=_cparams("parallel"), name=name)(*args)
    if out_tmajor:
        outs = [o.reshape(t_len, nb, mw) for o, mw in zip(outs, widths)]
    return outs if splits else outs[0]


def _s5_kernel(u_ref, h_re_ref, h_im_ref, abar_re_ref, abar_im_ref, bb_re_ref, bb_im_ref,
               cc_re_ref, cc_im_ref, d_ref, wglu_ref, y_ref, s_re_ref, s_im_ref,
               x_re, x_im, st_re, st_im, *, tc, nb):
    c = pl.program_id(0)
    rows = tc * nb
    nblk = S5_GROUPS // S5_GBLK
    bw_in = S5_GBLK * S5_GROUP
    bw_st = S5_GBLK * S5_N

    @pl.when(c == 0)
    def _():
        st_re[...] = h_re_ref[...]
        st_im[...] = h_im_ref[...]

    u = u_ref[...].reshape(rows, S5_WIDTH)
    ub = u.astype(BF16)
    for gb in range(nblk):
        ui = ub[:, gb * bw_in:(gb + 1) * bw_in]
        x_re[:, gb * bw_st:(gb + 1) * bw_st] = jnp.dot(ui, bb_re_ref[gb], preferred_element_type=F32)
        x_im[:, gb * bw_st:(gb + 1) * bw_st] = jnp.dot(ui, bb_im_ref[gb], preferred_element_type=F32)

    lane_blk = 1024
    for lb in range(S5_STATE // lane_blk):
        sl = slice(lb * lane_blk, (lb + 1) * lane_blk)
        ar = jnp.broadcast_to(abar_re_ref[:, sl], (nb, lane_blk))
        ai = jnp.broadcast_to(abar_im_ref[:, sl], (nb, lane_blk))

        def body(t, carry, sl=sl, ar=ar, ai=ai):
            xr, xi = carry
            r0 = pl.multiple_of(t * nb, nb)
            br = x_re[pl.ds(r0, nb), sl]
            bi = x_im[pl.ds(r0, nb), sl]
            nr = ar * xr - ai * xi + br
            ni = ar * xi + ai * xr + bi
            x_re[pl.ds(r0, nb), sl] = nr
            x_im[pl.ds(r0, nb), sl] = ni
            return nr, ni

        fr, fi = lax.fori_loop(0, tc, body, (st_re[:, sl], st_im[:, sl]), unroll=min(tc, 4))
        st_re[:, sl] = fr
        st_im[:, sl] = fi

    for gb in range(nblk):
        xr = x_re[:, gb * bw_st:(gb + 1) * bw_st].astype(BF16)
        xi = x_im[:, gb * bw_st:(gb + 1) * bw_st].astype(BF16)
        yb = (jnp.dot(xr, cc_re_ref[gb], preferred_element_type=F32)
              - jnp.dot(xi, cc_im_ref[gb], preferred_element_type=F32))
        cs = slice(gb * bw_in, (gb + 1) * bw_in)
        yb = yb + d_ref[:, cs] * u[:, cs]
        x_re[:, cs] = jax.nn.gelu(yb)
    y = x_re[:, :S5_WIDTH]
    y = y * jax.nn.sigmoid(jnp.dot(y.astype(BF16), wglu_ref[...], preferred_element_type=F32))
    y_ref[...] = y.reshape(y_ref.shape).astype(y_ref.dtype)

    @pl.when(c == pl.num_programs(0) - 1)
    def _():
        s_re_ref[...] = st_re[...]
        s_im_ref[...] = st_im[...]


def _s5_params(a_re, a_im, b_re, b_im, c_re, c_im, log_dt):
    dt = jnp.exp(log_dt.astype(F32))[:, None]
    ar, ai = a_re.astype(F32), a_im.astype(F32)
    mag = jnp.exp(dt * ar)
    abar_re, abar_im = mag * jnp.cos(dt * ai), mag * jnp.sin(dt * ai)
    den = ar * ar + ai * ai
    nr = abar_re - 1.0
    coef_re = (nr * ar + abar_im * ai) / den
    coef_im = (abar_im * ar - nr * ai) / den
    cr, ci = coef_re[..., None], coef_im[..., None]
    brf, bif = b_re.astype(F32), b_im.astype(F32)
    bb_re = cr * brf - ci * bif
    bb_im = cr * bif + ci * brf
    nblk = S5_GROUPS // S5_GBLK
    eye = jnp.eye(S5_GBLK, dtype=F32)

    def blockdiag_in(bb):
        t = jnp.transpose(bb, (0, 2, 1)).reshape(nblk, S5_GBLK, S5_GROUP, S5_N)
        m = jnp.einsum('kgcn,gh->kgchn', t, eye)
        return m.reshape(nblk, S5_GBLK * S5_GROUP, S5_GBLK * S5_N).astype(BF16)

    def blockdiag_out(cc):
        t = jnp.transpose(cc.astype(F32), (0, 2, 1)).reshape(nblk, S5_GBLK, S5_N, S5_GROUP)
        m = jnp.einsum('khnc,hg->khngc', t, eye)
        return m.reshape(nblk, S5_GBLK * S5_N, S5_GBLK * S5_GROUP).astype(BF16)

    return (abar_re.reshape(1, S5_STATE), abar_im.reshape(1, S5_STATE),
            blockdiag_in(bb_re), blockdiag_in(bb_im), blockdiag_out(c_re), blockdiag_out(c_im))


def s5_mixer(proj_tm, h_re, h_im, params, d_skip, w_glu, *, tc):
    t_len, nb, _ = proj_tm.shape
    abar_re, abar_im, bb_re, bb_im, cc_re, cc_im = params
    tc = min(tc, t_len)
    assert t_len % tc == 0 and nb % 8 == 0
    rows = tc * nb
    full = lambda a: pl.BlockSpec(a.shape, lambda c: (0,) * a.ndim)
    args = (proj_tm, h_re, h_im, abar_re, abar_im, bb_re, bb_im, cc_re, cc_im,
            d_skip.reshape(1, S5_WIDTH).astype(F32), w_glu.astype(BF16))
    in_specs = [pl.BlockSpec((tc, nb, S5_WIDTH), lambda c: (c, 0, 0))] + [full(a) for a in args[1:]]
    out_shape = (jax.ShapeDtypeStruct((t_len, nb, S5_WIDTH), BF16),
                 jax.ShapeDtypeStruct((nb, S5_STATE), F32),
                 jax.ShapeDtypeStruct((nb, S5_STATE), F32))
    out_specs = (pl.BlockSpec((tc, nb, S5_WIDTH), lambda c: (c, 0, 0)),
                 pl.BlockSpec((nb, S5_STATE), lambda c: (0, 0)),
                 pl.BlockSpec((nb, S5_STATE), lambda c: (0, 0)))
    scratch = [pltpu.VMEM((rows, S5_STATE), F32), pltpu.VMEM((rows, S5_STATE), F32),
               pltpu.VMEM((nb, S5_STATE), F32), pltpu.VMEM((nb, S5_STATE), F32)]
    return pl.pallas_call(
        functools.partial(_s5_kernel, tc=tc, nb=nb), grid=(t_len // tc,), in_specs=in_specs,
        out_specs=out_specs, out_shape=out_shape, scratch_shapes=scratch,
        compiler_params=_cparams("arbitrary"), name="s5_mixer")(*args)


def _head_ones():
    i = lax.broadcasted_iota(jnp.int32, (RW_WIDTH, RW_WIDTH), 0) // RW_HD
    j = lax.broadcasted_iota(jnp.int32, (RW_WIDTH, RW_WIDTH), 1) // RW_HD
    return jnp.where(i == j, 1.0, 0.0).astype(BF16)


def _softplus(z):
    return jnp.maximum(z, 0.0) + jnp.log1p(jnp.exp(-jnp.abs(z)))


def _rw_prep(p, p_prev, prm, ones_bd):
    mu, w0, w2, a0, a2, g2, k_k, k_a = prm
    xm = p + (p_prev - p) * mu
    o1, o2, o3 = RW_WIDTH, 2 * RW_WIDTH, 3 * RW_WIDTH
    r, k, v = xm[:, :o1], xm[:, o1:o2], xm[:, o2:o3]
    wd, ad, gd = xm[:, o3:o3 + 64], xm[:, o3 + 64:o3 + 128], xm[:, o3 + 128:]
    w = -_softplus(-(w0 + _bdot(jnp.tanh(wd), w2))) - 0.5
    lw = -jnp.exp(w)
    a = jax.nn.sigmoid(a0 + _bdot(ad, a2))
    g = _bdot(jax.nn.sigmoid(gd), g2)
    kk = k * k_k
    ss = _dot_exact_rhs(kk * kk, ones_bd)
    kk = kk / jnp.maximum(jnp.sqrt(ss), 1e-12)
    k = k * (1.0 + (a - 1.0) * k_a)
    return r, lw, k, v, -kk, kk * a, g


def _rw_post(o, r, k, v, g, r_k, ln_w, ln_b, ones_bd):
    inv = 1.0 / RW_HD
    mean = _dot_exact_rhs(o, ones_bd) * inv
    d = o - mean
    var = _dot_exact_rhs(d * d, ones_bd) * inv
    on = d * lax.rsqrt(var + RW_GN_EPS) * ln_w + ln_b
    bonus = _dot_exact_rhs(r * k * r_k, ones_bd) * v
    return (on + bonus) * g


def _rw_chunk_kernel(p_ref, shift_ref, h0_ref, mu_ref, w0_ref, w2_ref, a0_ref, a2_ref, g2_ref,
                     kk_ref, ka_ref, rk_ref, lnw_ref, lnb_ref,
                     y_ref, hfin_ref, shout_ref, prev_scr, h_scr, o_scr, *, c_len, bs):
    c = pl.program_id(1)
    nc = pl.num_programs(1)
    cl = c_len

    @pl.when(c == 0)
    def _():
        prev_scr[...] = shift_ref[:, 0, :]
        h_scr[...] = h0_ref[...]

    ones_bd = _head_ones()
    row = lax.broadcasted_iota(jnp.int32, (cl, RW_PROJ), 0)
    ps, pprevs = [], []
    for bi in range(bs):
        p = p_ref[:, bi * RW_PROJ:(bi + 1) * RW_PROJ]
        pprevs.append(jnp.where(row == 0, prev_scr[bi:bi + 1, :], pltpu.roll(p, 1, 0)))
        ps.append(p)
    p_all = jnp.concatenate(ps, axis=0) if bs > 1 else ps[0]
    pprev_all = jnp.concatenate(pprevs, axis=0) if bs > 1 else pprevs[0]
    prm = (mu_ref[...], w0_ref[...], w2_ref[...], a0_ref[...], a2_ref[...], g2_ref[...],
           kk_ref[...], ka_ref[...])
    r, lw, k, v, a, b, g = _rw_prep(p_all, pprev_all, prm, ones_bd)

    ti = lax.broadcasted_iota(jnp.int32, (cl, cl), 0)
    si = lax.broadcasted_iota(jnp.int32, (cl, cl), 1)
    lmat = jnp.where(ti >= si, 1.0, 0.0).astype(BF16)
    eye = jnp.where(ti == si, 1.0, 0.0)
    mi = lax.broadcasted_iota(jnp.int32, (2 * cl, 3 * cl), 0)
    mj = lax.broadcasted_iota(jnp.int32, (2 * cl, 3 * cl), 1)
    t_row = jnp.where(mi >= cl, mi - cl, mi)
    s_col = jnp.where(mj < cl, mj, jnp.where(mj >= 2 * cl, mj - 2 * cl, -4 * cl))
    keep = (t_row - s_col) >= jnp.where(mi >= cl, 0, 1)
    eye_bf = eye.astype(BF16)

    lhs_l, rhs_l, vh_l, hcat_l, kb_l, etot_l = [], [], [], [], [], []
    for bi in range(bs):
        rs = slice(bi * cl, (bi + 1) * cl)
        lw_b = lw[rs]
        l_hi, l_mid, l_lo = _split3(lw_b)
        cum = (jnp.dot(lmat, l_hi, preferred_element_type=F32)
               + jnp.dot(lmat, l_mid, preferred_element_type=F32)
               + jnp.dot(lmat, l_lo, preferred_element_type=F32))
        tot = cum[cl - 1:cl, :]
        e_neg = jnp.exp(-cum)
        e_rem = jnp.exp(tot - cum)
        at = (a[rs] * jnp.exp(cum - lw_b)).astype(BF16)
        rt = (r[rs] * jnp.exp(cum)).astype(BF16)
        bt = (b[rs] * e_neg).astype(BF16)
        kt = (k[rs] * e_neg).astype(BF16)
        bh = (b[rs] * e_rem).astype(BF16)
        kh = (k[rs] * e_rem).astype(BF16)
        e_tot = jnp.exp(tot)
        vb = v[rs].astype(BF16)
        for h in range(RW_HEADS):
            hs = slice(h * RW_HD, (h + 1) * RW_HD)
            lhs_l.append(jnp.concatenate([at[:, hs], rt[:, hs]], axis=0))
            rhs_l.append(jnp.concatenate([kt[:, hs], eye_bf, bt[:, hs]], axis=0))
            vh_l.append(vb[:, hs])
            kb_l.append(jnp.concatenate([kh[:, hs], bh[:, hs]], axis=0))
            etot_l.append(jnp.sum(eye * e_tot[:, hs], axis=-1, keepdims=True))
            hcat_l.append(h_scr[bi, h])

    nitem = bs * RW_HEADS
    items = range(nitem)
    aa_l = [jnp.where(keep, _dot_nt(lhs_l[i], rhs_l[i]), 0.0).astype(BF16) for i in items]
    pw_l = [aa_l[i][:cl, 2 * cl:] for i in items]
    tinv_l = [eye_bf + pw_l[i] for i in items]
    for _ in range(int(math.log2(cl)) - 1):
        pw_l = [jnp.dot(pw_l[i], pw_l[i], preferred_element_type=F32).astype(BF16) for i in items]
        tinv_l = [jnp.dot(tinv_l[i], eye_bf + pw_l[i], preferred_element_type=F32).astype(BF16) for i in items]
    vh_cat = [jnp.concatenate([vh_l[i], hcat_l[i].astype(BF16)], axis=0) for i in items]
    x1_l = [jnp.dot(aa_l[i][:cl, :2 * cl], vh_cat[i], preferred_element_type=F32).astype(BF16) for i in items]
    u_l = [jnp.dot(tinv_l[i], x1_l[i], preferred_element_type=F32).astype(BF16) for i in items]
    o_l = [jnp.dot(aa_l[i][cl:, :], jnp.concatenate([vh_cat[i], u_l[i]], axis=0),
                   preferred_element_type=F32) for i in items]
    hn_l = [hcat_l[i] * etot_l[i]
            + lax.dot_general(kb_l[i], jnp.concatenate([vh_l[i], u_l[i]], axis=0), (((0,), (0,)), ((), ())),
                              preferred_element_type=F32) for i in items]

    for bi in range(bs):
        for h in range(RW_HEADS):
            i = bi * RW_HEADS + h
            o_scr[bi * cl:(bi + 1) * cl, h * RW_HD:(h + 1) * RW_HD] = o_l[i]
            h_scr[bi, h] = hn_l[i]
        prev_scr[bi:bi + 1, :] = ps[bi][cl - 1:cl, :]

    y = _rw_post(o_scr[...], r, k, v, g, rk_ref[...], lnw_ref[...], lnb_ref[...], ones_bd)
    for bi in range(bs):
        y_ref[:, bi * RW_WIDTH:(bi + 1) * RW_WIDTH] = y[bi * cl:(bi + 1) * cl].astype(y_ref.dtype)

    @pl.when(c == nc - 1)
    def _():
        hfin_ref[...] = h_scr[...]
        for bi in range(bs):
            shout_ref[bi] = ps[bi][cl - 1:cl, :]


def _rw_param_args(mu, w0, w2, a0, a2, g2, k_k, k_a, r_k, ln_w, ln_b):
    row = lambda z: z.reshape(1, -1).astype(F32)
    return (row(mu), row(w0), w2.astype(BF16), row(a0), a2.astype(BF16), g2.astype(BF16),
            row(k_k), row(k_a), row(r_k), row(ln_w), row(ln_b))


def rwkv_prompt(p_tm, shift, s0, params, *, bs=4):
    c_len = RW_HD
    t_len, nb, _ = p_tm.shape
    assert t_len % c_len == 0 and nb % bs == 0
    prm = _rw_param_args(*params)
    const = lambda a: pl.BlockSpec(a.shape, lambda b, c: (0,) * a.ndim)
    st_spec = pl.BlockSpec((bs, RW_HEADS, RW_HD, RW_HD), lambda b, c: (b, 0, 0, 0))
    sh_spec = pl.BlockSpec((bs, 1, RW_PROJ), lambda b, c: (b, 0, 0))
    in_specs = [pl.BlockSpec((c_len, bs * RW_PROJ), lambda b, c: (c, b)), sh_spec, st_spec] + [const(a) for a in prm]
    out_shape = (jax.ShapeDtypeStruct((t_len, nb * RW_WIDTH), BF16),
                 jax.ShapeDtypeStruct((nb, RW_HEADS, RW_HD, RW_HD), F32),
                 jax.ShapeDtypeStruct((nb, 1, RW_PROJ), F32))
    out_specs = (pl.BlockSpec((c_len, bs * RW_WIDTH), lambda b, c: (c, b)), st_spec, sh_spec)
    scratch = [pltpu.VMEM((bs, RW_PROJ), F32), pltpu.VMEM((bs, RW_HEADS, RW_HD, RW_HD), F32),
               pltpu.VMEM((bs * c_len, RW_WIDTH), F32)]
    h0 = jnp.swapaxes(s0, -1, -2)
    y, h_fin, sh = pl.pallas_call(
        functools.partial(_rw_chunk_kernel, c_len=c_len, bs=bs), grid=(nb // bs, t_len // c_len),
        in_specs=in_specs, out_specs=out_specs, out_shape=out_shape, scratch_shapes=scratch,
        compiler_params=_cparams("parallel", "arbitrary"), name="rwkv_prompt")(
            p_tm.reshape(t_len, nb * RW_PROJ), shift.reshape(nb, 1, RW_PROJ), h0, *prm)
    return y.reshape(t_len, nb, RW_WIDTH), jnp.swapaxes(h_fin, -1, -2), sh.reshape(nb, RW_PROJ)


def _rw_step_prep_kernel(p_ref, shift_ref, mu_ref, w0_ref, w2_ref, a0_ref, a2_ref, g2_ref, kk_ref, ka_ref,
                         r_ref, w_ref, k_ref, v_ref, a_ref, b_ref, g_ref):
    prm = (mu_ref[...], w0_ref[...], w2_ref[...], a0_ref[...], a2_ref[...], g2_ref[...],
           kk_ref[...], ka_ref[...])
    r, lw, k, v, a, b, g = _rw_prep(p_ref[...], shift_ref[...], prm, _head_ones())
    r_ref[...] = r
    w_ref[...] = jnp.exp(lw)
    k_ref[...] = k
    v_ref[...] = v
    a_ref[...] = a
    b_ref[...] = b
    g_ref[...] = g


def _rw_step_core_kernel(s_ref, r_ref, w_ref, k_ref, a_ref, b_ref, v_ref, s_out_ref, o_ref):
    s = s_ref[...]
    sa = jnp.sum(s * a_ref[...], axis=-1, keepdims=True)
    s_new = s * w_ref[...] + sa * b_ref[...] + v_ref[...] * k_ref[...]
    s_out_ref[...] = s_new
    o_ref[...] = jnp.sum(s_new * r_ref[...], axis=-1, keepdims=True)


def _rw_step_post_kernel(o_ref, r_ref, k_ref, v_ref, g_ref, rk_ref, lnw_ref, lnb_ref, y_ref):
    y_ref[...] = _rw_post(o_ref[...], r_ref[...], k_ref[...], v_ref[...], g_ref[...],
                          rk_ref[...], lnw_ref[...], lnb_ref[...], _head_ones()).astype(y_ref.dtype)


def rwkv_step(p, shift, s0, params, *, bt=8):
    n = p.shape[0]
    prm = _rw_param_args(*params)
    vec = jax.ShapeDtypeStruct((n, RW_WIDTH), F32)
    r, w, k, v, a, b, g = pl.pallas_call(
        _rw_step_prep_kernel, out_shape=(vec,) * 7, name="rwkv_step_prep")(p, shift, *prm[:8])
    rows = lambda z: z.reshape(n, RW_HEADS, 1, RW_HD)
    row_spec = pl.BlockSpec((bt, RW_HEADS, 1, RW_HD), lambda i: (i, 0, 0, 0))
    col_spec = pl.BlockSpec((bt, RW_HEADS, RW_HD, 1), lambda i: (i, 0, 0, 0))
    st_spec = pl.BlockSpec((bt, RW_HEADS, RW_HD, RW_HD), lambda i: (i, 0, 0, 0))
    s_new, o = pl.pallas_call(
        _rw_step_core_kernel, grid=(n // bt,),
        in_specs=[st_spec] + [row_spec] * 5 + [col_spec], out_specs=(st_spec, col_spec),
        out_shape=(jax.ShapeDtypeStruct(s0.shape, F32), jax.ShapeDtypeStruct((n, RW_HEADS, RW_HD, 1), F32)),
        compiler_params=_cparams("parallel"), name="rwkv_step_core")(
            s0, rows(r), rows(w), rows(k), rows(a), rows(b), v.reshape(n, RW_HEADS, RW_HD, 1))
    y = pl.pallas_call(
        _rw_step_post_kernel, out_shape=jax.ShapeDtypeStruct((n, RW_WIDTH), BF16), name="rwkv_step_post")(
            o.reshape(n, RW_WIDTH), r, k, v, g, *prm[8:])
    return y, s_new


RET_LOG_G = tuple(math.log(1.0 - 2.0 ** (-5.0 - h)) for h in range(RET_HEADS))


def _rope_tables(pos, half):
    j = lax.broadcasted_iota(jnp.int32, (1, half), 1).astype(F32)
    inv = jnp.exp(j * (-math.log(ROPE_BASE) / half))
    ang = pos * inv
    return jnp.cos(ang), jnp.sin(ang)


def _rope(x, cos, sin):
    half = RET_DK // 2
    outs = []
    for h in range(RET_HEADS):
        x1 = x[:, h * RET_DK:h * RET_DK + half]
        x2 = x[:, h * RET_DK + half:(h + 1) * RET_DK]
        outs += [x1 * cos - x2 * sin, x1 * sin + x2 * cos]
    return jnp.concatenate(outs, axis=-1)


def _ret_norm_gate(o, g):
    o = o * lax.rsqrt(jnp.mean(o * o, axis=-1, keepdims=True) + NORM_EPS)
    return jax.nn.silu(g) * o


def _ret_chunk_kernel(q_ref, k_ref, v_ref, g_ref, y_ref, sfin_ref, s_scr, *, c_len):
    c = pl.program_id(1)

    @pl.when(c == 0)
    def _():
        s_scr[...] = jnp.zeros_like(s_scr)

    half = RET_DK // 2
    ti = lax.broadcasted_iota(jnp.int32, (c_len, 1), 0).astype(F32)
    pos = (c * c_len).astype(F32) + ti
    cos, sin = _rope_tables(pos, half)
    q = _rope(q_ref[...].astype(F32), cos, sin)
    k = _rope(k_ref[...].astype(F32), cos, sin) * (RET_DK ** -0.5)
    ii = lax.broadcasted_iota(jnp.int32, (c_len, c_len), 0)
    jj = lax.broadcasted_iota(jnp.int32, (c_len, c_len), 1)
    diff = (ii - jj).astype(F32)
    for h in range(RET_HEADS):
        lg = RET_LOG_G[h]
        dmask = jnp.where(diff >= 0, jnp.exp(lg * jnp.maximum(diff, 0.0)), 0.0)
        q_dec = jnp.exp(lg * (ti + 1.0))
        k_dec = jnp.exp(lg * (c_len - 1.0 - ti))
        c_dec = math.exp(lg * c_len)
        qh = q[:, h * RET_DK:(h + 1) * RET_DK]
        kh = k[:, h * RET_DK:(h + 1) * RET_DK]
        vh = v_ref[:, h * RET_DV:(h + 1) * RET_DV]
        s_h = s_scr[h]
        sc = _dot_nt(qh, kh) * dmask
        o = _bdot(sc, vh) + _bdot(qh * q_dec, s_h)
        s_scr[h] = s_h * c_dec + _dot_tn(kh * k_dec, vh)
        gh = g_ref[:, h * RET_DV:(h + 1) * RET_DV].astype(F32)
        y_ref[:, h * RET_DV:(h + 1) * RET_DV] = _ret_norm_gate(o, gh).astype(y_ref.dtype)

    @pl.when(c == pl.num_programs(1) - 1)
    def _():
        sfin_ref[0] = s_scr[...]


def retention_prompt(q, k, v, g, *, nb, c_len=RET_CHUNK):
    n = q.shape[0]
    t_len = n // nb
    nc = t_len // c_len
    spec = lambda w: pl.BlockSpec((c_len, w), lambda b, c: (b * nc + c, 0))
    st_spec = pl.BlockSpec((1, RET_HEADS, RET_DK, RET_DV), lambda b, c: (b, 0, 0, 0))
    return pl.pallas_call(
        functools.partial(_ret_chunk_kernel, c_len=c_len), grid=(nb, nc),
        in_specs=[spec(NQ), spec(NQ), spec(NV), spec(NV)], out_specs=(spec(NV), st_spec),
        out_shape=(jax.ShapeDtypeStruct((n, NV), BF16),
                   jax.ShapeDtypeStruct((nb, RET_HEADS, RET_DK, RET_DV), F32)),
        scratch_shapes=[pltpu.VMEM((RET_HEADS, RET_DK, RET_DV), F32)],
        compiler_params=_cparams("parallel", "arbitrary"), name="retention_prompt")(q, k, v, g)


def _ret_step_rope_kernel(q_ref, k_ref, qo_ref, ko_ref, *, pos0):
    pos = jnp.full((q_ref.shape[0], 1), pos0, F32)
    cos, sin = _rope_tables(pos, RET_DK // 2)
    qo_ref[...] = _rope(q_ref[...].astype(F32), cos, sin)
    ko_ref[...] = _rope(k_ref[...].astype(F32), cos, sin) * (RET_DK ** -0.5)


def _ret_step_core_kernel(s_ref, q_ref, k_ref, v_ref, g_ref, s_out_ref, y_ref):
    for h in range(RET_HEADS):
        gam = math.exp(RET_LOG_G[h])
        s_h = s_ref[0, h]
        qc = q_ref[0, h]
        kc = k_ref[0, h]
        vr = v_ref[0, h].astype(F32)
        qk = jnp.sum(qc * kc, axis=0, keepdims=True)
        o = qk * vr + jnp.sum((qc * gam) * s_h, axis=0, keepdims=True)
        s_out_ref[0, h] = s_h * gam + kc * vr
        y_ref[0, h] = _ret_norm_gate(o, g_ref[0, h].astype(F32)).astype(y_ref.dtype)


def retention_step(q, k, v, g, s0, *, pos0):
    n = q.shape[0]
    vec = jax.ShapeDtypeStruct((n, NQ), F32)
    qr, kr = pl.pallas_call(functools.partial(_ret_step_rope_kernel, pos0=pos0), out_shape=(vec, vec),
                            name="retention_step_rope")(q, k)
    col = lambda z: z.reshape(n, RET_HEADS, RET_DK, 1)
    row = lambda z: z.reshape(n, RET_HEADS, 1, RET_DV)
    st_spec = pl.BlockSpec((1, RET_HEADS, RET_DK, RET_DV), lambda i: (i, 0, 0, 0))
    col_spec = pl.BlockSpec((1, RET_HEADS, RET_DK, 1), lambda i: (i, 0, 0, 0))
    row_spec = pl.BlockSpec((1, RET_HEADS, 1, RET_DV), lambda i: (i, 0, 0, 0))
    s_new, y = pl.pallas_call(
        _ret_step_core_kernel, grid=(n,),
        in_specs=[st_spec, col_spec, col_spec, row_spec, row_spec], out_specs=(st_spec, row_spec),
        out_shape=(jax.ShapeDtypeStruct(s0.shape, F32), jax.ShapeDtypeStruct((n, RET_HEADS, 1, RET_DV), BF16)),
        compiler_params=_cparams("parallel"), name="retention_step_core")(
            s0, col(qr), col(kr), row(v), row(g))
    return y.reshape(n, NV), s_new


def _xattn_prompt_kernel(x_ref, g_ref, wq_ref, mk_ref, mv_ref, wo_ref, o_ref, att_scr):
    x = x_ref[...]
    q = jnp.dot(_rms(x, g_ref[...]).astype(BF16), wq_ref[...], preferred_element_type=F32)
    for h in range(MEM_HEADS):
        hs = slice(h * MEM_HD, (h + 1) * MEM_HD)
        s = _dot_nt(q[:, hs], mk_ref[0, :, hs]) * (MEM_HD ** -0.5)
        s = s - jnp.max(s, axis=-1, keepdims=True)
        e = jnp.exp(s)
        p = e / jnp.sum(e, axis=-1, keepdims=True)
        att_scr[:, hs] = _bdot(p, mv_ref[0, :, hs])
    o_ref[...] = x + jnp.dot(att_scr[...].astype(BF16), wo_ref[...], preferred_element_type=F32)


def xattn_prompt(x, gain, w_q, mem_k, mem_v, w_o, *, nb, tm=512):
    n = x.shape[0]
    tiles_per_b = n // nb // tm
    row = pl.BlockSpec((tm, D_MODEL), lambda i: (i, 0))
    wspec = pl.BlockSpec((D_MODEL, D_MODEL), lambda i: (0, 0))
    mspec = pl.BlockSpec((1, N_MEM, D_MODEL), lambda i: (i // tiles_per_b, 0, 0))
    return pl.pallas_call(
        _xattn_prompt_kernel, grid=(n // tm,),
        in_specs=[row, pl.BlockSpec((1, D_MODEL), lambda i: (0, 0)), wspec, mspec, mspec, wspec],
        out_specs=row, out_shape=jax.ShapeDtypeStruct((n, D_MODEL), F32),
        scratch_shapes=[pltpu.VMEM((tm, D_MODEL), F32)],
        compiler_params=_cparams("parallel"), name="xattn_prompt")(
            x, gain.reshape(1, D_MODEL), w_q, mem_k, mem_v, w_o)


def _xattn_step_kernel(q_ref, mk_ref, mv_ref, o_ref, *, tb):
    for i in range(tb):
        q = q_ref[i]
        for h in range(MEM_HEADS):
            hs = slice(h * MEM_HD, (h + 1) * MEM_HD)
            s = jnp.sum(mk_ref[0, i, :, hs] * q[:, hs], axis=-1, keepdims=True) * (MEM_HD ** -0.5)
            s = s - jnp.max(s, axis=0, keepdims=True)
            e = jnp.exp(s)
            p = e / jnp.sum(e, axis=0, keepdims=True)
            o_ref[i, :, hs] = jnp.sum(p * mv_ref[0, i, :, hs], axis=0, keepdims=True)


def xattn_step(q, cache_k, cache_v, layer, *, tb=4):
    n = q.shape[0]
    qspec = pl.BlockSpec((tb, 1, D_MODEL), lambda i: (i, 0, 0))
    cspec = pl.BlockSpec((1, tb, N_MEM, D_MODEL), lambda i: (layer, i, 0, 0))
    o = pl.pallas_call(
        functools.partial(_xattn_step_kernel, tb=tb), grid=(n // tb,),
        in_specs=[qspec, cspec, cspec], out_specs=qspec,
        out_shape=jax.ShapeDtypeStruct((n, 1, D_MODEL), F32),
        compiler_params=_cparams("parallel"), name="xattn_step")(q.reshape(n, 1, D_MODEL), cache_k, cache_v)
    return o.reshape(n, D_MODEL)


ROUTER_LANES = 128
NEG_BIG = -1e30


def _moe_gates(logits):
    lane = lax.broadcasted_iota(jnp.int32, logits.shape, 1)
    first = lambda mask: jnp.min(jnp.where(mask, lane, ROUTER_LANES), axis=-1, keepdims=True)
    is_c = lane < MOE_GROUPS
    lc = jnp.where(is_c, logits, NEG_BIG)
    mc = jnp.max(lc, axis=-1, keepdims=True)
    g_idx = first(lc == mc)
    p_g = 1.0 / jnp.sum(jnp.where(is_c, jnp.exp(lc - mc), 0.0), axis=-1, keepdims=True)
    fl = lane - MOE_GROUPS
    in_g = (fl >= 0) & (fl < MOE_EXPERTS) & ((fl // MOE_PER_GROUP) == g_idx)
    lf = jnp.where(in_g, logits, NEG_BIG)
    m1 = jnp.max(lf, axis=-1, keepdims=True)
    i1 = first(lf == m1)
    lf2 = jnp.where(lane == i1, NEG_BIG, lf)
    m2 = jnp.max(lf2, axis=-1, keepdims=True)
    i2 = first(lf2 == m2)
    e2 = jnp.exp(m2 - m1)
    w_top = 1.0 / (1.0 + e2)
    return p_g * (jnp.where(lane == i1, w_top, 0.0) + jnp.where(lane == i2, e2 * w_top, 0.0))


def _moe_kernel(x_ref, g_ref, wr_ref, br_ref, w1_ref, w3_ref, w2_ref, o_ref, h_scr, gate_scr):
    e = pl.program_id(1)

    @pl.when(e == 0)
    def _():
        x = x_ref[...]
        h = _rms(x, g_ref[...])
        h_scr[...] = h.astype(BF16)
        logits = jnp.dot(h, wr_ref[...], preferred_element_type=F32,
                         precision=lax.Precision.HIGHEST) + br_ref[...]
        gate_scr[...] = _moe_gates(logits)
        o_ref[...] = x

    hb = h_scr[...]
    a1 = jnp.dot(hb, w1_ref[0], preferred_element_type=F32)
    a3 = jnp.dot(hb, w3_ref[0], preferred_element_type=F32)
    lane = lax.broadcasted_iota(jnp.int32, gate_scr.shape, 1)
    ge = jnp.sum(jnp.where(lane == e + MOE_GROUPS, gate_scr[...], 0.0), axis=-1, keepdims=True)
    hid = jax.nn.silu(a1) * a3 * ge
    o_ref[...] += jnp.dot(hid.astype(BF16), w2_ref[0], preferred_element_type=F32)


def moe_dense(x, gain, w_r, b_r, w1, w3, w2, *, tm=512):
    n = x.shape[0]
    tm = min(tm, n)
    gain = gain.reshape(1, D_MODEL)
    row = pl.BlockSpec((tm, D_MODEL), lambda i, e: (i, 0))
    const2 = lambda a: pl.BlockSpec(a.shape, lambda i, e: (0, 0))
    return pl.pallas_call(
        _moe_kernel, grid=(n // tm, MOE_EXPERTS),
        in_specs=[row, const2(gain), const2(w_r), const2(b_r),
                  pl.BlockSpec((1, D_MODEL, MOE_HIDDEN), lambda i, e: (e, 0, 0)),
                  pl.BlockSpec((1, D_MODEL, MOE_HIDDEN), lambda i, e: (e, 0, 0)),
                  pl.BlockSpec((1, MOE_HIDDEN, D_MODEL), lambda i, e: (e, 0, 0))],
        out_specs=row, out_shape=jax.ShapeDtypeStruct((n, D_MODEL), F32),
        scratch_shapes=[pltpu.VMEM((tm, D_MODEL), BF16), pltpu.VMEM((tm, ROUTER_LANES), F32)],
        compiler_params=_cparams("parallel", "arbitrary"), name="moe")(x, gain, w_r, b_r, w1, w3, w2)


def _router_params(w_rc, b_rc, w_rf, b_rf):
    pad = ROUTER_LANES - MOE_GROUPS - MOE_EXPERTS
    w_r = jnp.concatenate([w_rc, w_rf, jnp.zeros((D_MODEL, pad), F32)], axis=1).astype(F32)
    b_r = jnp.concatenate([b_rc, b_rf, jnp.zeros((pad,), F32)]).reshape(1, ROUTER_LANES).astype(F32)
    return w_r, b_r


def _rmsnorm_kernel(x_ref, g_ref, o_ref):
    o_ref[...] = _rms(x_ref[...], g_ref[...])


def rmsnorm_rows(x, gain, *, tm=1024):
    n = x.shape[0]
    tm = min(tm, n)
    row = pl.BlockSpec((tm, D_MODEL), lambda i: (i, 0))
    return pl.pallas_call(
        _rmsnorm_kernel, grid=(n // tm,), in_specs=[row, pl.BlockSpec((1, D_MODEL), lambda i: (0, 0))],
        out_specs=row, out_shape=jax.ShapeDtypeStruct((n, D_MODEL), F32),
        compiler_params=_cparams("parallel"), name="rmsnorm")(x, gain.reshape(1, D_MODEL))


def _stack(parts):
    return parts[0][None] if len(parts) == 1 else jnp.stack(parts)


def _trunk(x, nb, w, s5_re, s5_im, rw_state, rw_shift, ret_state, mem_k, mem_v, pos0):
    n = x.shape[0]
    t_len = n // nb
    single = t_len == 1
    out_s5_re, out_s5_im, out_rw, out_shift, out_ret = [], [], [], [], []
    for layer in range(DEPTH):
        if layer % 2 == 0:
            i = layer // 2
            s5p = _s5_params(w['s5_a_re'][i], w['s5_a_im'][i], w['s5_b_re'][i], w['s5_b_im'][i],
                             w['s5_c_re'][i], w['s5_c_im'][i], w['s5_log_dt'][i])
            rwp = tuple(w[k][i] for k in ('rw_mu', 'rw_w0', 'rw_w2', 'rw_a0', 'rw_a2', 'rw_g2',
                                          'rw_k_k', 'rw_k_a', 'rw_r_k', 'rw_ln_w', 'rw_ln_b'))
            w_in = w['w_in0'][i].astype(BF16)
            w_out = w['w_out0'][i].astype(BF16)
            if single:
                u, p = linear(x, w_in, gain=w['norm_mix'][layer], splits=(S5_WIDTH, RW_PROJ))
                y_s5, sr, si = s5_mixer(u.reshape(1, nb, S5_WIDTH), s5_re[i], s5_im[i], s5p,
                                        w['s5_d'][i], w['s5_w_glu'][i], tc=1)
                y_rw, srw = rwkv_step(p, rw_shift[i], rw_state[i], rwp)
                sh = p
                x = linear(y_s5.reshape(nb, S5_WIDTH), w_out[:S5_WIDTH], x2=y_rw, w2=w_out[S5_WIDTH:], residual=x)
            else:
                u, p = linear(x, w_in, gain=w['norm_mix'][layer], splits=(S5_WIDTH, RW_PROJ),
                              out_tmajor=True, batch=nb)
                y_s5, sr, si = s5_mixer(u, s5_re[i], s5_im[i], s5p, w['s5_d'][i], w['s5_w_glu'][i], tc=128)
                y_rw, srw, sh = rwkv_prompt(p, rw_shift[i], rw_state[i], rwp)
                x = linear(y_s5, w_out[:S5_WIDTH], x2=y_rw, w2=w_out[S5_WIDTH:], residual=x, x_tmajor=True)
            out_s5_re.append(sr.reshape(nb, S5_GROUPS, S5_N))
            out_s5_im.append(si.reshape(nb, S5_GROUPS, S5_N))
            out_rw.append(srw)
            out_shift.append(sh)
        else:
            j = layer // 2
            q, k, v, g = linear(x, w['w_in1'][j].astype(BF16), gain=w['norm_mix'][layer], out_dtype=BF16,
                                splits=(NQ, NQ, NV, NV), tm=256)
            if single:
                y, s_new = retention_step(q, k, v, g, ret_state[j], pos0=pos0)
            else:
                y, s_new = retention_prompt(q, k, v, g, nb=nb)
            x = linear(y, w['w_out1'][j].astype(BF16), residual=x)
            out_ret.append(s_new)
        w_q = w['w_mq'][layer].astype(BF16)
        w_o = w['w_mo'][layer].astype(BF16)
        if single:
            q = linear(x, w_q, gain=w['norm_mem'][layer])
            att = xattn_step(q, mem_k, mem_v, layer)
            x = linear(att, w_o, residual=x)
        else:
            x = xattn_prompt(x, w['norm_mem'][layer], w_q, mem_k[layer], mem_v[layer], w_o, nb=nb)
        w_r, b_r = _router_params(w['moe_w_rc'][layer], w['moe_b_rc'][layer],
                                  w['moe_w_rf'][layer], w['moe_b_rf'][layer])
        x = moe_dense(x, w['norm_ffn'][layer], w_r, b_r, w['moe_w1'][layer].astype(BF16),
                      w['moe_w3'][layer].astype(BF16), w['moe_w2'][layer].astype(BF16))
    y = rmsnorm_rows(x, w['norm_final'])
    return (y, _stack(out_s5_re), _stack(out_s5_im), _stack(out_rw), _stack(out_shift), _stack(out_ret))


def kernel(x_prompt, x_sample, mem_prompt, state_s5_re, state_s5_im, state_rwkv, state_shift, state_ret, cache_mem_k, cache_mem_v, norm_mix, norm_mem, norm_ffn, norm_final, w_in0, w_out0, s5_a_re, s5_a_im, s5_b_re, s5_b_im, s5_c_re, s5_c_im, s5_d, s5_log_dt, s5_w_glu, rw_mu, rw_w0, rw_w2, rw_a0, rw_a2, rw_g2, rw_k_k, rw_k_a, rw_r_k, rw_ln_w, rw_ln_b, w_in1, w_out1, mem_norm, w_mq, w_mk, w_mv, w_mo, moe_w_rc, moe_b_rc, moe_w_rf, moe_b_rf, moe_w1, moe_w3, moe_w2):
    w = dict(norm_mix=norm_mix, norm_mem=norm_mem, norm_ffn=norm_ffn, norm_final=norm_final,
             w_in0=w_in0, w_out0=w_out0, s5_a_re=s5_a_re, s5_a_im=s5_a_im, s5_b_re=s5_b_re, s5_b_im=s5_b_im,
             s5_c_re=s5_c_re, s5_c_im=s5_c_im, s5_d=s5_d, s5_log_dt=s5_log_dt, s5_w_glu=s5_w_glu,
             rw_mu=rw_mu, rw_w0=rw_w0, rw_w2=rw_w2, rw_a0=rw_a0, rw_a2=rw_a2, rw_g2=rw_g2,
             rw_k_k=rw_k_k, rw_k_a=rw_k_a, rw_r_k=rw_r_k, rw_ln_w=rw_ln_w, rw_ln_b=rw_ln_b,
             w_in1=w_in1, w_out1=w_out1, w_mq=w_mq, w_mo=w_mo,
             moe_w_rc=moe_w_rc, moe_b_rc=moe_b_rc, moe_w_rf=moe_w_rf, moe_b_rf=moe_b_rf,
             moe_w1=moe_w1, moe_w3=moe_w3, moe_w2=moe_w2)
    nbp, t_len, _ = x_prompt.shape
    nbs = x_sample.shape[0]
    n_even, n_odd = state_s5_re.shape[0], state_ret.shape[0]

    mem = mem_prompt.reshape(nbp * N_MEM, D_MODEL)
    mem_k_l, mem_v_l = [], []
    for layer in range(DEPTH):
        w_kv = jnp.concatenate([w_mk[layer], w_mv[layer]], axis=1).astype(BF16)
        mk, mv = linear(mem, w_kv, gain=mem_norm[layer], splits=(D_MODEL, D_MODEL))
        mem_k_l.append(mk.reshape(nbp, N_MEM, D_MODEL))
        mem_v_l.append(mv.reshape(nbp, N_MEM, D_MODEL))
    kv_shape = (DEPTH, nbp, N_MEM, MEM_HEADS, MEM_HD)
    mem_k_p = jnp.stack(mem_k_l).reshape(kv_shape)
    mem_v_p = jnp.stack(mem_v_l).reshape(kv_shape)

    zeros = lambda *shape: jnp.zeros(shape, F32)
    y_p, s5r_p, s5i_p, rw_p, sh_p, ret_p = _trunk(
        x_prompt.reshape(nbp * t_len, D_MODEL), nbp, w,
        zeros(n_even, nbp, S5_STATE), zeros(n_even, nbp, S5_STATE),
        zeros(n_even, nbp, RW_HEADS, RW_HD, RW_HD), zeros(n_even, nbp, RW_PROJ),
        None, mem_k_l, mem_v_l, 0.0)
    y_s, s5r_s, s5i_s, rw_s, sh_s, ret_s = _trunk(
        x_sample.reshape(nbs, D_MODEL), nbs, w,
        state_s5_re.reshape(n_even, nbs, S5_STATE), state_s5_im.reshape(n_even, nbs, S5_STATE),
        state_rwkv, state_shift, state_ret,
        cache_mem_k.reshape(DEPTH, nbs, N_MEM, D_MODEL), cache_mem_v.reshape(DEPTH, nbs, N_MEM, D_MODEL),
        float(PAST_LEN))
    return (y_p.reshape(nbp, t_len, D_MODEL), y_s.reshape(nbs, 1, D_MODEL),
            s5r_p, s5i_p, rw_p, sh_p, ret_p, mem_k_p, mem_v_p, s5r_s, s5i_s, rw_s, sh_s, ret_s)
```

```python
import functools
import math

import jax
import jax.numpy as jnp
from jax import lax
from jax.experimental import pallas as pl
from jax.experimental.pallas import tpu as pltpu

F32 = jnp.float32
BF16 = jnp.bfloat16

D_MODEL = 1024
DEPTH = 2
PAST_LEN = 16384
S5_WIDTH = 512
S5_GROUP = 16
S5_GROUPS = 32
S5_N = 64
S5_STATE = S5_GROUPS * S5_N
S5_GBLK = 8
RW_WIDTH = 512
RW_HD = 64
RW_HEADS = 8
RW_LORA = 256
RW_PROJ = 3 * RW_WIDTH + RW_LORA
IN0 = S5_WIDTH + RW_PROJ
RET_DK = 256
RET_HEADS = 4
RET_DV = 512
RET_CHUNK = 128
NQ = RET_HEADS * RET_DK
NV = RET_HEADS * RET_DV
IN1 = 2 * NQ + 2 * NV
N_MEM = 256
MEM_HEADS = 4
MEM_HD = 256
MOE_GROUPS = 4
MOE_PER_GROUP = 4
MOE_EXPERTS = 16
MOE_HIDDEN = 256
NORM_EPS = 1e-6
RW_GN_EPS = 64e-5
ROPE_BASE = 10000.0

VMEM_LIMIT = 56 * 1024 * 1024


def _cparams(*sem):
    return pltpu.CompilerParams(dimension_semantics=sem, vmem_limit_bytes=VMEM_LIMIT)


def _bdot(a, b):
    return jnp.dot(a.astype(BF16), b.astype(BF16), preferred_element_type=F32)


def _dot_nt(a, b):
    return lax.dot_general(a.astype(BF16), b.astype(BF16), (((1,), (1,)), ((), ())),
                           preferred_element_type=F32)


def _dot_tn(a, b):
    return lax.dot_general(a.astype(BF16), b.astype(BF16), (((0,), (0,)), ((), ())),
                           preferred_element_type=F32)


def _split3(x):
    hi = x.astype(BF16)
    r1 = x - hi.astype(F32)
    mid = r1.astype(BF16)
    lo = (r1 - mid.astype(F32)).astype(BF16)
    return hi, mid, lo


def _dot_exact_rhs(x, m_bf16):
    hi, mid, lo = _split3(x)
    acc = jnp.dot(hi, m_bf16, preferred_element_type=F32)
    acc = acc + jnp.dot(mid, m_bf16, preferred_element_type=F32)
    return acc + jnp.dot(lo, m_bf16, preferred_element_type=F32)


def _rms(x, g):
    ms = jnp.mean(x * x, axis=-1, keepdims=True)
    return x * lax.rsqrt(ms + NORM_EPS) * g


def _linear_kernel(*refs, norm, two, res):
    it = iter(refs)
    x_ref = next(it)
    g_ref = next(it) if norm else None
    w_ref = next(it)
    x2_ref = next(it) if two else None
    w2_ref = next(it) if two else None
    r_ref = next(it) if res else None
    o_refs = list(it)
    x = x_ref[...].astype(F32)
    if norm:
        x = _rms(x, g_ref[...])
    xb = x.astype(BF16)
    x2b = x2_ref[...].astype(BF16) if two else None
    col = 0
    for o_ref in o_refs:
        m = o_ref.shape[-1]
        step = next((s for s in (512, 256) if m % s == 0), m)
        for j in range(m // step):
            sl = slice(col + j * step, col + (j + 1) * step)
            acc = jnp.dot(xb, w_ref[:, sl], preferred_element_type=F32)
            if two:
                acc = acc + jnp.dot(x2b, w2_ref[:, sl], preferred_element_type=F32)
            if res:
                acc = acc + r_ref[:, sl]
            o_ref[:, j * step:(j + 1) * step] = acc.astype(o_ref.dtype)
        col += m


def _row_spec(tm, width, tmajor_b):
    if tmajor_b is None:
        return pl.BlockSpec((tm, width), lambda i: (i, 0))
    nb, tiles_per_b = tmajor_b
    return pl.BlockSpec((tm, width), lambda i: (i % tiles_per_b, i // tiles_per_b))


def linear(x, w, *, gain=None, x2=None, w2=None, residual=None, out_dtype=F32, tm=512,
           x_tmajor=False, out_tmajor=False, batch=None, splits=None, name="linear"):
    if x_tmajor:
        t_len, nb, k = x.shape
        n = t_len * nb
    else:
        n, k = x.shape
        nb = batch
        t_len = n // nb if nb else None
    m = w.shape[1]
    tm = min(tm, n if not (x_tmajor or out_tmajor) else t_len)
    assert n % tm == 0
    tiles_per_b = (t_len // tm) if (x_tmajor or out_tmajor) else None
    args, specs = [], []

    def add_rows(a, tmajor):
        width = a.shape[-1]
        args.append(a.reshape(t_len, nb * width) if tmajor else a)
        specs.append(_row_spec(tm, width, (nb, tiles_per_b) if tmajor else None))

    add_rows(x, x_tmajor)
    if gain is not None:
        args.append(gain.reshape(1, k).astype(F32))
        specs.append(pl.BlockSpec((1, k), lambda i: (0, 0)))
    args.append(w)
    specs.append(pl.BlockSpec(w.shape, lambda i: (0, 0)))
    if x2 is not None:
        add_rows(x2, x_tmajor)
        args.append(w2)
        specs.append(pl.BlockSpec(w2.shape, lambda i: (0, 0)))
    if residual is not None:
        add_rows(residual, False)
    widths = tuple(splits) if splits else (m,)
    assert sum(widths) == m
    if out_tmajor:
        out_shape = [jax.ShapeDtypeStruct((t_len, nb * mw), out_dtype) for mw in widths]
    else:
        out_shape = [jax.ShapeDtypeStruct((n, mw), out_dtype) for mw in widths]
    out_specs = [_row_spec(tm, mw, (nb, tiles_per_b) if out_tmajor else None) for mw in widths]
    kern = functools.partial(_linear_kernel, norm=gain is not None, two=x2 is not None,
                             res=residual is not None)
    outs = pl.pallas_call(
        kern, grid=(n // tm,), in_specs=specs, out_specs=out_specs, out_shape=out_shape,
        compiler_params=_cparams("parallel"), name=name)(*args)
    if out_tmajor:
        outs = [o.reshape(t_len, nb, mw) for o, mw in zip(outs, widths)]
    return outs if splits else outs[0]


def _s5_kernel(u_ref, h_re_ref, h_im_ref, abar_re_ref, abar_im_ref, bb_re_ref, bb_im_ref,
               cc_re_ref, cc_im_ref, d_ref, wglu_ref, y_ref, s_re_ref, s_im_ref,
               x_re, x_im, st_re, st_im, il_scr, *, tc, nb, flat):
    c = pl.program_id(0)
    nlb = S5_WIDTH // 128
    rows = tc * nb
    nblk = S5_GROUPS // S5_GBLK
    bw_in = S5_GBLK * S5_GROUP
    bw_st = S5_GBLK * S5_N

    @pl.when(c == 0)
    def _():
        st_re[...] = h_re_ref[...]
        st_im[...] = h_im_ref[...]

    if flat:
        for b in range(nb):
            for j in range(nlb):
                il_scr[j, pl.ds(b, tc, stride=nb), :] = u_ref[:, b * S5_WIDTH + j * 128:b * S5_WIDTH + (j + 1) * 128]
        u = jnp.concatenate([il_scr[j] for j in range(nlb)], axis=-1)
    else:
        u = u_ref[...].reshape(rows, S5_WIDTH)
    ub = u.astype(BF16)
    for gb in range(nblk):
        ui = ub[:, gb * bw_in:(gb + 1) * bw_in]
        x_re[:, gb * bw_st:(gb + 1) * bw_st] = jnp.dot(ui, bb_re_ref[gb], preferred_element_type=F32)
        x_im[:, gb * bw_st:(gb + 1) * bw_st] = jnp.dot(ui, bb_im_ref[gb], preferred_element_type=F32)

    lane_blk = 1024
    for lb in range(S5_STATE // lane_blk):
        sl = slice(lb * lane_blk, (lb + 1) * lane_blk)
        ar = jnp.broadcast_to(abar_re_ref[:, sl], (nb, lane_blk))
        ai = jnp.broadcast_to(abar_im_ref[:, sl], (nb, lane_blk))

        def body(t, carry, sl=sl, ar=ar, ai=ai):
            xr, xi = carry
            r0 = pl.multiple_of(t * nb, nb)
            br = x_re[pl.ds(r0, nb), sl]
            bi = x_im[pl.ds(r0, nb), sl]
            nr = ar * xr - ai * xi + br
            ni = ar * xi + ai * xr + bi
            x_re[pl.ds(r0, nb), sl] = nr
            x_im[pl.ds(r0, nb), sl] = ni
            return nr, ni

        fr, fi = lax.fori_loop(0, tc, body, (st_re[:, sl], st_im[:, sl]), unroll=min(tc, 4))
        st_re[:, sl] = fr
        st_im[:, sl] = fi

    for gb in range(nblk):
        xr = x_re[:, gb * bw_st:(gb + 1) * bw_st].astype(BF16)
        xi = x_im[:, gb * bw_st:(gb + 1) * bw_st].astype(BF16)
        yb = (jnp.dot(xr, cc_re_ref[gb], preferred_element_type=F32)
              - jnp.dot(xi, cc_im_ref[gb], preferred_element_type=F32))
        cs = slice(gb * bw_in, (gb + 1) * bw_in)
        yb = yb + d_ref[:, cs] * u[:, cs]
        x_re[:, cs] = jax.nn.gelu(yb)
    y = x_re[:, :S5_WIDTH]
    y = y * jax.nn.sigmoid(jnp.dot(y.astype(BF16), wglu_ref[...], preferred_element_type=F32))
    if flat:
        for j in range(nlb):
            il_scr[j] = y[:, j * 128:(j + 1) * 128]
        for b in range(nb):
            for j in range(nlb):
                y_ref[:, b * S5_WIDTH + j * 128:b * S5_WIDTH + (j + 1) * 128] = (
                    il_scr[j, pl.ds(b, tc, stride=nb), :].astype(y_ref.dtype))
    else:
        y_ref[...] = y.reshape(y_ref.shape).astype(y_ref.dtype)

    @pl.when(c == pl.num_programs(0) - 1)
    def _():
        s_re_ref[...] = st_re[...]
        s_im_ref[...] = st_im[...]


def _s5_params(a_re, a_im, b_re, b_im, c_re, c_im, log_dt):
    dt = jnp.exp(log_dt.astype(F32))[:, None]
    ar, ai = a_re.astype(F32), a_im.astype(F32)
    mag = jnp.exp(dt * ar)
    abar_re, abar_im = mag * jnp.cos(dt * ai), mag * jnp.sin(dt * ai)
    den = ar * ar + ai * ai
    nr = abar_re - 1.0
    coef_re = (nr * ar + abar_im * ai) / den
    coef_im = (abar_im * ar - nr * ai) / den
    cr, ci = coef_re[..., None], coef_im[..., None]
    brf, bif = b_re.astype(F32), b_im.astype(F32)
    bb_re = cr * brf - ci * bif
    bb_im = cr * bif + ci * brf
    nblk = S5_GROUPS // S5_GBLK
    eye = jnp.eye(S5_GBLK, dtype=F32)

    def blockdiag_in(bb):
        t = jnp.transpose(bb, (0, 2, 1)).reshape(nblk, S5_GBLK, S5_GROUP, S5_N)
        m = jnp.einsum('kgcn,gh->kgchn', t, eye)
        return m.reshape(nblk, S5_GBLK * S5_GROUP, S5_GBLK * S5_N).astype(BF16)

    def blockdiag_out(cc):
        t = jnp.transpose(cc.astype(F32), (0, 2, 1)).reshape(nblk, S5_GBLK, S5_N, S5_GROUP)
        m = jnp.einsum('khnc,hg->khngc', t, eye)
        return m.reshape(nblk, S5_GBLK * S5_N, S5_GBLK * S5_GROUP).astype(BF16)

    return (abar_re.reshape(1, S5_STATE), abar_im.reshape(1, S5_STATE),
            blockdiag_in(bb_re), blockdiag_in(bb_im), blockdiag_out(c_re), blockdiag_out(c_im))


def s5_mixer(u_tm, h_re, h_im, params, d_skip, w_glu, *, tc):
    t_len, nb, _ = u_tm.shape
    abar_re, abar_im, bb_re, bb_im, cc_re, cc_im = params
    tc = min(tc, t_len)
    assert t_len % tc == 0 and nb % 8 == 0
    rows = tc * nb
    flat = t_len > 1
    full = lambda a: pl.BlockSpec(a.shape, lambda c: (0,) * a.ndim)
    if flat:
        u_arg = u_tm.reshape(t_len, nb * S5_WIDTH)
        io_spec = pl.BlockSpec((tc, nb * S5_WIDTH), lambda c: (c, 0))
        y_shape = jax.ShapeDtypeStruct((t_len, nb * S5_WIDTH), BF16)
    else:
        u_arg = u_tm
        io_spec = pl.BlockSpec((tc, nb, S5_WIDTH), lambda c: (c, 0, 0))
        y_shape = jax.ShapeDtypeStruct((t_len, nb, S5_WIDTH), BF16)
    args = (u_arg, h_re, h_im, abar_re, abar_im, bb_re, bb_im, cc_re, cc_im,
            d_skip.reshape(1, S5_WIDTH).astype(F32), w_glu.astype(BF16))
    in_specs = [io_spec] + [full(a) for a in args[1:]]
    st_shape = jax.ShapeDtypeStruct((nb, S5_STATE), F32)
    st_spec = pl.BlockSpec((nb, S5_STATE), lambda c: (0, 0))
    scratch = [pltpu.VMEM((rows, S5_STATE), F32), pltpu.VMEM((rows, S5_STATE), F32),
               pltpu.VMEM((nb, S5_STATE), F32), pltpu.VMEM((nb, S5_STATE), F32),
               pltpu.VMEM((S5_WIDTH // 128, rows if flat else 8, 128), F32)]
    y, s_re, s_im = pl.pallas_call(
        functools.partial(_s5_kernel, tc=tc, nb=nb, flat=flat), grid=(t_len // tc,), in_specs=in_specs,
        out_specs=(io_spec, st_spec, st_spec), out_shape=(y_shape, st_shape, st_shape),
        scratch_shapes=scratch, compiler_params=_cparams("arbitrary"), name="s5_mixer")(*args)
    return y.reshape(t_len, nb, S5_WIDTH), s_re, s_im


def _head_ones():
    i = lax.broadcasted_iota(jnp.int32, (RW_WIDTH, RW_WIDTH), 0) // RW_HD
    j = lax.broadcasted_iota(jnp.int32, (RW_WIDTH, RW_WIDTH), 1) // RW_HD
    return jnp.where(i == j, 1.0, 0.0).astype(BF16)


def _softplus(z):
    return jnp.maximum(z, 0.0) + jnp.log1p(jnp.exp(-jnp.abs(z)))


def _rw_prep(p, p_prev, prm, ones_bd):
    mu, w0, w2, a0, a2, g2, k_k, k_a = prm
    xm = p + (p_prev - p) * mu
    o1, o2, o3 = RW_WIDTH, 2 * RW_WIDTH, 3 * RW_WIDTH
    r, k, v = xm[:, :o1], xm[:, o1:o2], xm[:, o2:o3]
    wd, ad, gd = xm[:, o3:o3 + 64], xm[:, o3 + 64:o3 + 128], xm[:, o3 + 128:]
    w = -_softplus(-(w0 + _bdot(jnp.tanh(wd), w2))) - 0.5
    lw = -jnp.exp(w)
    a = jax.nn.sigmoid(a0 + _bdot(ad, a2))
    g = _bdot(jax.nn.sigmoid(gd), g2)
    kk = k * k_k
    ss = _dot_exact_rhs(kk * kk, ones_bd)
    kk = kk / jnp.maximum(jnp.sqrt(ss), 1e-12)
    k = k * (1.0 + (a - 1.0) * k_a)
    return r, lw, k, v, -kk, kk * a, g


def _rw_post(o, r, k, v, g, r_k, ln_w, ln_b, ones_bd):
    inv = 1.0 / RW_HD
    mean = _dot_exact_rhs(o, ones_bd) * inv
    d = o - mean
    var = _dot_exact_rhs(d * d, ones_bd) * inv
    on = d * lax.rsqrt(var + RW_GN_EPS) * ln_w + ln_b
    bonus = _dot_exact_rhs(r * k * r_k, ones_bd) * v
    return (on + bonus) * g


def _rw_chunk_kernel(p_ref, shift_ref, h0_ref, mu_ref, w0_ref, w2_ref, a0_ref, a2_ref, g2_ref,
                     kk_ref, ka_ref, rk_ref, lnw_ref, lnb_ref,
                     y_ref, hfin_ref, shout_ref, prev_scr, h_scr, o_scr, *, c_len, bs):
    c = pl.program_id(1)
    nc = pl.num_programs(1)
    cl = c_len

    @pl.when(c == 0)
    def _():
        prev_scr[...] = shift_ref[:, 0, :]
        h_scr[...] = h0_ref[...]

    ones_bd = _head_ones()
    row = lax.broadcasted_iota(jnp.int32, (cl, RW_PROJ), 0)
    ps, pprevs = [], []
    for bi in range(bs):
        p = p_ref[:, bi * RW_PROJ:(bi + 1) * RW_PROJ]
        pprevs.append(jnp.where(row == 0, prev_scr[bi:bi + 1, :], pltpu.roll(p, 1, 0)))
        ps.append(p)
    p_all = jnp.concatenate(ps, axis=0) if bs > 1 else ps[0]
    pprev_all = jnp.concatenate(pprevs, axis=0) if bs > 1 else pprevs[0]
    prm = (mu_ref[...], w0_ref[...], w2_ref[...], a0_ref[...], a2_ref[...], g2_ref[...],
           kk_ref[...], ka_ref[...])
    r, lw, k, v, a, b, g = _rw_prep(p_all, pprev_all, prm, ones_bd)

    ti = lax.broadcasted_iota(jnp.int32, (cl, cl), 0)
    si = lax.broadcasted_iota(jnp.int32, (cl, cl), 1)
    lmat = jnp.where(ti >= si, 1.0, 0.0).astype(BF16)
    eye = jnp.where(ti == si, 1.0, 0.0)
    mi = lax.broadcasted_iota(jnp.int32, (2 * cl, 3 * cl), 0)
    mj = lax.broadcasted_iota(jnp.int32, (2 * cl, 3 * cl), 1)
    t_row = jnp.where(mi >= cl, mi - cl, mi)
    s_col = jnp.where(mj < cl, mj, jnp.where(mj >= 2 * cl, mj - 2 * cl, -4 * cl))
    keep = (t_row - s_col) >= jnp.where(mi >= cl, 0, 1)
    eye_bf = eye.astype(BF16)

    lhs_l, rhs_l, vh_l, hcat_l, kb_l, etot_l = [], [], [], [], [], []
    for bi in range(bs):
        rs = slice(bi * cl, (bi + 1) * cl)
        lw_b = lw[rs]
        l_hi, l_mid, l_lo = _split3(lw_b)
        cum = (jnp.dot(lmat, l_hi, preferred_element_type=F32)
               + jnp.dot(lmat, l_mid, preferred_element_type=F32)
               + jnp.dot(lmat, l_lo, preferred_element_type=F32))
        tot = cum[cl - 1:cl, :]
        e_neg = jnp.exp(-cum)
        e_rem = jnp.exp(tot - cum)
        at = (a[rs] * jnp.exp(cum - lw_b)).astype(BF16)
        rt = (r[rs] * jnp.exp(cum)).astype(BF16)
        bt = (b[rs] * e_neg).astype(BF16)
        kt = (k[rs] * e_neg).astype(BF16)
        bh = (b[rs] * e_rem).astype(BF16)
        kh = (k[rs] * e_rem).astype(BF16)
        e_tot = jnp.exp(tot)
        vb = v[rs].astype(BF16)
        for h in range(RW_HEADS):
            hs = slice(h * RW_HD, (h + 1) * RW_HD)
            lhs_l.append(jnp.concatenate([at[:, hs], rt[:, hs]], axis=0))
            rhs_l.append(jnp.concatenate([kt[:, hs], eye_bf, bt[:, hs]], axis=0))
            vh_l.append(vb[:, hs])
            kb_l.append(jnp.concatenate([kh[:, hs], bh[:, hs]], axis=0))
            etot_l.append(jnp.sum(eye * e_tot[:, hs], axis=-1, keepdims=True))
            hcat_l.append(h_scr[bi, h])

    nitem = bs * RW_HEADS
    items = range(nitem)
    aa_l = [jnp.where(keep, _dot_nt(lhs_l[i], rhs_l[i]), 0.0).astype(BF16) for i in items]
    pw_l = [aa_l[i][:cl, 2 * cl:] for i in items]
    tinv_l = [eye_bf + pw_l[i] for i in items]
    for _ in range(int(math.log2(cl)) - 1):
        pw_l = [jnp.dot(pw_l[i], pw_l[i], preferred_element_type=F32).astype(BF16) for i in items]
        tinv_l = [jnp.dot(tinv_l[i], eye_bf + pw_l[i], preferred_element_type=F32).astype(BF16) for i in items]
    vh_cat = [jnp.concatenate([vh_l[i], hcat_l[i].astype(BF16)], axis=0) for i in items]
    x1_l = [jnp.dot(aa_l[i][:cl, :2 * cl], vh_cat[i], preferred_element_type=F32).astype(BF16) for i in items]
    u_l = [jnp.dot(tinv_l[i], x1_l[i], preferred_element_type=F32).astype(BF16) for i in items]
    o_l = [jnp.dot(aa_l[i][cl:, :], jnp.concatenate([vh_cat[i], u_l[i]], axis=0),
                   preferred_element_type=F32) for i in items]
    hn_l = [hcat_l[i] * etot_l[i]
            + lax.dot_general(kb_l[i], jnp.concatenate([vh_l[i], u_l[i]], axis=0), (((0,), (0,)), ((), ())),
                              preferred_element_type=F32) for i in items]

    for bi in range(bs):
        for h in range(RW_HEADS):
            i = bi * RW_HEADS + h
            o_scr[bi * cl:(bi + 1) * cl, h * RW_HD:(h + 1) * RW_HD] = o_l[i]
            h_scr[bi, h] = hn_l[i]
        prev_scr[bi:bi + 1, :] = ps[bi][cl - 1:cl, :]

    y = _rw_post(o_scr[...], r, k, v, g, rk_ref[...], lnw_ref[...], lnb_ref[...], ones_bd)
    for bi in range(bs):
        y_ref[:, bi * RW_WIDTH:(bi + 1) * RW_WIDTH] = y[bi * cl:(bi + 1) * cl].astype(y_ref.dtype)

    @pl.when(c == nc - 1)
    def _():
        hfin_ref[...] = h_scr[...]
        for bi in range(bs):
            shout_ref[bi] = ps[bi][cl - 1:cl, :]


def _rw_param_args(mu, w0, w2, a0, a2, g2, k_k, k_a, r_k, ln_w, ln_b):
    row = lambda z: z.reshape(1, -1).astype(F32)
    return (row(mu), row(w0), w2.astype(BF16), row(a0), a2.astype(BF16), g2.astype(BF16),
            row(k_k), row(k_a), row(r_k), row(ln_w), row(ln_b))


def rwkv_prompt(p_tm, shift, s0, params, *, bs=4):
    c_len = RW_HD
    t_len, nb, _ = p_tm.shape
    assert t_len % c_len == 0 and nb % bs == 0
    prm = _rw_param_args(*params)
    const = lambda a: pl.BlockSpec(a.shape, lambda b, c: (0,) * a.ndim)
    st_spec = pl.BlockSpec((bs, RW_HEADS, RW_HD, RW_HD), lambda b, c: (b, 0, 0, 0))
    sh_spec = pl.BlockSpec((bs, 1, RW_PROJ), lambda b, c: (b, 0, 0))
    in_specs = [pl.BlockSpec((c_len, bs * RW_PROJ), lambda b, c: (c, b)), sh_spec, st_spec] + [const(a) for a in prm]
    out_shape = (jax.ShapeDtypeStruct((t_len, nb * RW_WIDTH), BF16),
                 jax.ShapeDtypeStruct((nb, RW_HEADS, RW_HD, RW_HD), F32),
                 jax.ShapeDtypeStruct((nb, 1, RW_PROJ), F32))
    out_specs = (pl.BlockSpec((c_len, bs * RW_WIDTH), lambda b, c: (c, b)), st_spec, sh_spec)
    scratch = [pltpu.VMEM((bs, RW_PROJ), F32), pltpu.VMEM((bs, RW_HEADS, RW_HD, RW_HD), F32),
               pltpu.VMEM((bs * c_len, RW_WIDTH), F32)]
    h0 = jnp.swapaxes(s0, -1, -2)
    y, h_fin, sh = pl.pallas_call(
        functools.partial(_rw_chunk_kernel, c_len=c_len, bs=bs), grid=(nb // bs, t_len // c_len),
        in_specs=in_specs, out_specs=out_specs, out_shape=out_shape, scratch_shapes=scratch,
        compiler_params=_cparams("parallel", "arbitrary"), name="rwkv_prompt")(
            p_tm.reshape(t_len, nb * RW_PROJ), shift.reshape(nb, 1, RW_PROJ), h0, *prm)
    return y.reshape(t_len, nb, RW_WIDTH), jnp.swapaxes(h_fin, -1, -2), sh.reshape(nb, RW_PROJ)


def _rw_step_prep_kernel(p_ref, shift_ref, mu_ref, w0_ref, w2_ref, a0_ref, a2_ref, g2_ref, kk_ref, ka_ref,
                         r_ref, w_ref, k_ref, v_ref, a_ref, b_ref, g_ref):
    prm = (mu_ref[...], w0_ref[...], w2_ref[...], a0_ref[...], a2_ref[...], g2_ref[...],
           kk_ref[...], ka_ref[...])
    r, lw, k, v, a, b, g = _rw_prep(p_ref[...], shift_ref[...], prm, _head_ones())
    r_ref[...] = r
    w_ref[...] = jnp.exp(lw)
    k_ref[...] = k
    v_ref[...] = v
    a_ref[...] = a
    b_ref[...] = b
    g_ref[...] = g


def _rw_step_core_kernel(s_ref, r_ref, w_ref, k_ref, a_ref, b_ref, v_ref, s_out_ref, o_ref):
    s = s_ref[...]
    sa = jnp.sum(s * a_ref[...], axis=-1, keepdims=True)
    s_new = s * w_ref[...] + sa * b_ref[...] + v_ref[...] * k_ref[...]
    s_out_ref[...] = s_new
    o_ref[...] = jnp.sum(s_new * r_ref[...], axis=-1, keepdims=True)


def _rw_step_post_kernel(o_ref, r_ref, k_ref, v_ref, g_ref, rk_ref, lnw_ref, lnb_ref, y_ref):
    y_ref[...] = _rw_post(o_ref[...], r_ref[...], k_ref[...], v_ref[...], g_ref[...],
                          rk_ref[...], lnw_ref[...], lnb_ref[...], _head_ones()).astype(y_ref.dtype)


def rwkv_step(p, shift, s0, params, *, bt=8):
    n = p.shape[0]
    prm = _rw_param_args(*params)
    vec = jax.ShapeDtypeStruct((n, RW_WIDTH), F32)
    r, w, k, v, a, b, g = pl.pallas_call(
        _rw_step_prep_kernel, out_shape=(vec,) * 7, name="rwkv_step_prep")(p, shift, *prm[:8])
    rows = lambda z: z.reshape(n, RW_HEADS, 1, RW_HD)
    row_spec = pl.BlockSpec((bt, RW_HEADS, 1, RW_HD), lambda i: (i, 0, 0, 0))
    col_spec = pl.BlockSpec((bt, RW_HEADS, RW_HD, 1), lambda i: (i, 0, 0, 0))
    st_spec = pl.BlockSpec((bt, RW_HEADS, RW_HD, RW_HD), lambda i: (i, 0, 0, 0))
    s_new, o = pl.pallas_call(
        _rw_step_core_kernel, grid=(n // bt,),
        in_specs=[st_spec] + [row_spec] * 5 + [col_spec], out_specs=(st_spec, col_spec),
        out_shape=(jax.ShapeDtypeStruct(s0.shape, F32), jax.ShapeDtypeStruct((n, RW_HEADS, RW_HD, 1), F32)),
        compiler_params=_cparams("parallel"), name="rwkv_step_core")(
            s0, rows(r), rows(w), rows(k), rows(a), rows(b), v.reshape(n, RW_HEADS, RW_HD, 1))
    y = pl.pallas_call(
        _rw_step_post_kernel, out_shape=jax.ShapeDtypeStruct((n, RW_WIDTH), BF16), name="rwkv_step_post")(
            o.reshape(n, RW_WIDTH), r, k, v, g, *prm[8:])
    return y, s_new


RET_LOG_G = tuple(math.log(1.0 - 2.0 ** (-5.0 - h)) for h in range(RET_HEADS))


def _rope_tables(pos, half):
    j = lax.broadcasted_iota(jnp.int32, (1, half), 1).astype(F32)
    inv = jnp.exp(j * (-math.log(ROPE_BASE) / half))
    ang = pos * inv
    return jnp.cos(ang), jnp.sin(ang)


def _rope(x, cos, sin):
    half = RET_DK // 2
    outs = []
    for h in range(RET_HEADS):
        x1 = x[:, h * RET_DK:h * RET_DK + half]
        x2 = x[:, h * RET_DK + half:(h + 1) * RET_DK]
        outs += [x1 * cos - x2 * sin, x1 * sin + x2 * cos]
    return jnp.concatenate(outs, axis=-1)


def _ret_norm_gate(o, g):
    o = o * lax.rsqrt(jnp.mean(o * o, axis=-1, keepdims=True) + NORM_EPS)
    return jax.nn.silu(g) * o


def _ret_chunk_kernel(q_ref, k_ref, v_ref, g_ref, y_ref, sfin_ref, s_scr, *, c_len):
    c = pl.program_id(1)

    @pl.when(c == 0)
    def _():
        s_scr[...] = jnp.zeros_like(s_scr)

    half = RET_DK // 2
    ti = lax.broadcasted_iota(jnp.int32, (c_len, 1), 0).astype(F32)
    pos = (c * c_len).astype(F32) + ti
    cos, sin = _rope_tables(pos, half)
    q = _rope(q_ref[...].astype(F32), cos, sin)
    k = _rope(k_ref[...].astype(F32), cos, sin) * (RET_DK ** -0.5)
    ii = lax.broadcasted_iota(jnp.int32, (c_len, c_len), 0)
    jj = lax.broadcasted_iota(jnp.int32, (c_len, c_len), 1)
    diff = (ii - jj).astype(F32)
    for h in range(RET_HEADS):
        lg = RET_LOG_G[h]
        dmask = jnp.where(diff >= 0, jnp.exp(lg * jnp.maximum(diff, 0.0)), 0.0)
        q_dec = jnp.exp(lg * (ti + 1.0))
        k_dec = jnp.exp(lg * (c_len - 1.0 - ti))
        c_dec = math.exp(lg * c_len)
        qh = q[:, h * RET_DK:(h + 1) * RET_DK]
        kh = k[:, h * RET_DK:(h + 1) * RET_DK]
        vh = v_ref[:, h * RET_DV:(h + 1) * RET_DV]
        s_h = s_scr[h]
        sc = _dot_nt(qh, kh) * dmask
        o = _bdot(sc, vh) + _bdot(qh * q_dec, s_h)
        s_scr[h] = s_h * c_dec + _dot_tn(kh * k_dec, vh)
        gh = g_ref[:, h * RET_DV:(h + 1) * RET_DV].astype(F32)
        y_ref[:, h * RET_DV:(h + 1) * RET_DV] = _ret_norm_gate(o, gh).astype(y_ref.dtype)

    @pl.when(c == pl.num_programs(1) - 1)
    def _():
        sfin_ref[0] = s_scr[...]


def retention_prompt(q, k, v, g, *, nb, c_len=RET_CHUNK):
    n = q.shape[0]
    t_len = n // nb
    nc = t_len // c_len
    spec = lambda w: pl.BlockSpec((c_len, w), lambda b, c: (b * nc + c, 0))
    st_spec = pl.BlockSpec((1, RET_HEADS, RET_DK, RET_DV), lambda b, c: (b, 0, 0, 0))
    return pl.pallas_call(
        functools.partial(_ret_chunk_kernel, c_len=c_len), grid=(nb, nc),
        in_specs=[spec(NQ), spec(NQ), spec(NV), spec(NV)], out_specs=(spec(NV), st_spec),
        out_shape=(jax.ShapeDtypeStruct((n, NV), BF16),
                   jax.ShapeDtypeStruct((nb, RET_HEADS, RET_DK, RET_DV), F32)),
        scratch_shapes=[pltpu.VMEM((RET_HEADS, RET_DK, RET_DV), F32)],
        compiler_params=_cparams("parallel", "arbitrary"), name="retention_prompt")(q, k, v, g)


def _ret_step_rope_kernel(q_ref, k_ref, qo_ref, ko_ref, *, pos0):
    pos = jnp.full((q_ref.shape[0], 1), pos0, F32)
    cos, sin = _rope_tables(pos, RET_DK // 2)
    qo_ref[...] = _rope(q_ref[...].astype(F32), cos, sin)
    ko_ref[...] = _rope(k_ref[...].astype(F32), cos, sin) * (RET_DK ** -0.5)


def _ret_step_core_kernel(s_ref, q_ref, k_ref, v_ref, g_ref, s_out_ref, y_ref):
    for h in range(RET_HEADS):
        gam = math.exp(RET_LOG_G[h])
        s_h = s_ref[0, h]
        qc = q_ref[0, h]
        kc = k_ref[0, h]
        vr = v_ref[0, h].astype(F32)
        qk = jnp.sum(qc * kc, axis=0, keepdims=True)
        o = qk * vr + jnp.sum((qc * gam) * s_h, axis=0, keepdims=True)
        s_out_ref[0, h] = s_h * gam + kc * vr
        y_ref[0, h] = _ret_norm_gate(o, g_ref[0, h].astype(F32)).astype(y_ref.dtype)


def retention_step(q, k, v, g, s0, *, pos0):
    n = q.shape[0]
    vec = jax.ShapeDtypeStruct((n, NQ), F32)
    qr, kr = pl.pallas_call(functools.partial(_ret_step_rope_kernel, pos0=pos0), out_shape=(vec, vec),
                            name="retention_step_rope")(q, k)
    col = lambda z: z.reshape(n, RET_HEADS, RET_DK, 1)
    row = lambda z: z.reshape(n, RET_HEADS, 1, RET_DV)
    st_spec = pl.BlockSpec((1, RET_HEADS, RET_DK, RET_DV), lambda i: (i, 0, 0, 0))
    col_spec = pl.BlockSpec((1, RET_HEADS, RET_DK, 1), lambda i: (i, 0, 0, 0))
    row_spec = pl.BlockSpec((1, RET_HEADS, 1, RET_DV), lambda i: (i, 0, 0, 0))
    s_new, y = pl.pallas_call(
        _ret_step_core_kernel, grid=(n,),
        in_specs=[st_spec, col_spec, col_spec, row_spec, row_spec], out_specs=(st_spec, row_spec),
        out_shape=(jax.ShapeDtypeStruct(s0.shape, F32), jax.ShapeDtypeStruct((n, RET_HEADS, 1, RET_DV), BF16)),
        compiler_params=_cparams("parallel"), name="retention_step_core")(
            s0, col(qr), col(kr), row(v), row(g))
    return y.reshape(n, NV), s_new


def _xattn_prompt_kernel(x_ref, g_ref, wq_ref, mk_ref, mv_ref, wo_ref, o_ref, att_scr):
    x = x_ref[...]
    q = jnp.dot(_rms(x, g_ref[...]).astype(BF16), wq_ref[...], preferred_element_type=F32)
    for h in range(MEM_HEADS):
        hs = slice(h * MEM_HD, (h + 1) * MEM_HD)
        s = _dot_nt(q[:, hs], mk_ref[0, :, hs]) * (MEM_HD ** -0.5)
        s = s - jnp.max(s, axis=-1, keepdims=True)
        e = jnp.exp(s)
        p = e / jnp.sum(e, axis=-1, keepdims=True)
        att_scr[:, hs] = _bdot(p, mv_ref[0, :, hs])
    o_ref[...] = x + jnp.dot(att_scr[...].astype(BF16), wo_ref[...], preferred_element_type=F32)


def xattn_prompt(x, gain, w_q, mem_k, mem_v, w_o, *, nb, tm=512):
    n = x.shape[0]
    tiles_per_b = n // nb // tm
    row = pl.BlockSpec((tm, D_MODEL), lambda i: (i, 0))
    wspec = pl.BlockSpec((D_MODEL, D_MODEL), lambda i: (0, 0))
    mspec = pl.BlockSpec((1, N_MEM, D_MODEL), lambda i: (i // tiles_per_b, 0, 0))
    return pl.pallas_call(
        _xattn_prompt_kernel, grid=(n // tm,),
        in_specs=[row, pl.BlockSpec((1, D_MODEL), lambda i: (0, 0)), wspec, mspec, mspec, wspec],
        out_specs=row, out_shape=jax.ShapeDtypeStruct((n, D_MODEL), F32),
        scratch_shapes=[pltpu.VMEM((tm, D_MODEL), F32)],
        compiler_params=_cparams("parallel"), name="xattn_prompt")(
            x, gain.reshape(1, D_MODEL), w_q, mem_k, mem_v, w_o)


def _xattn_step_kernel(q_ref, mk_ref, mv_ref, o_ref, *, tb):
    for i in range(tb):
        q = q_ref[i]
        for h in range(MEM_HEADS):
            hs = slice(h * MEM_HD, (h + 1) * MEM_HD)
            s = jnp.sum(mk_ref[0, i, :, h, :] * q[:, hs], axis=-1, keepdims=True) * (MEM_HD ** -0.5)
            s = s - jnp.max(s, axis=0, keepdims=True)
            e = jnp.exp(s)
            p = e / jnp.sum(e, axis=0, keepdims=True)
            o_ref[i, :, hs] = jnp.sum(p * mv_ref[0, i, :, h, :], axis=0, keepdims=True)


def xattn_step(q, cache_k, cache_v, layer, *, tb=2):
    n = q.shape[0]
    qspec = pl.BlockSpec((tb, 1, D_MODEL), lambda i: (i, 0, 0))
    cspec = pl.BlockSpec((1, tb, N_MEM, MEM_HEADS, MEM_HD), lambda i: (layer, i, 0, 0, 0))
    o = pl.pallas_call(
        functools.partial(_xattn_step_kernel, tb=tb), grid=(n // tb,),
        in_specs=[qspec, cspec, cspec], out_specs=qspec,
        out_shape=jax.ShapeDtypeStruct((n, 1, D_MODEL), F32),
        compiler_params=_cparams("parallel"), name="xattn_step")(q.reshape(n, 1, D_MODEL), cache_k, cache_v)
    return o.reshape(n, D_MODEL)


ROUTER_LANES = 128
NEG_BIG = -1e30


def _moe_gates(logits):
    lane = lax.broadcasted_iota(jnp.int32, logits.shape, 1)
    first = lambda mask: jnp.min(jnp.where(mask, lane, ROUTER_LANES), axis=-1, keepdims=True)
    is_c = lane < MOE_GROUPS
    lc = jnp.where(is_c, logits, NEG_BIG)
    mc = jnp.max(lc, axis=-1, keepdims=True)
    g_idx = first(lc == mc)
    p_g = 1.0 / jnp.sum(jnp.where(is_c, jnp.exp(lc - mc), 0.0), axis=-1, keepdims=True)
    fl = lane - MOE_GROUPS
    in_g = (fl >= 0) & (fl < MOE_EXPERTS) & ((fl // MOE_PER_GROUP) == g_idx)
    lf = jnp.where(in_g, logits, NEG_BIG)
    m1 = jnp.max(lf, axis=-1, keepdims=True)
    i1 = first(lf == m1)
    lf2 = jnp.where(lane == i1, NEG_BIG, lf)
    m2 = jnp.max(lf2, axis=-1, keepdims=True)
    i2 = first(lf2 == m2)
    e2 = jnp.exp(m2 - m1)
    w_top = 1.0 / (1.0 + e2)
    gate = p_g * (jnp.where(lane == i1, w_top, 0.0) + jnp.where(lane == i2, e2 * w_top, 0.0))
    return gate, g_idx


MOE_TM = 512
MOE_GW = MOE_PER_GROUP * MOE_HIDDEN
HG_W = D_MODEL + ROUTER_LANES


def _group_ffn(hb, gate, grp, w1_ref, w3_ref, w2_ref):
    lane = lax.broadcasted_iota(jnp.int32, gate.shape, 1)
    acc = None
    for e in range(MOE_PER_GROUP):
        es = slice(e * MOE_HIDDEN, (e + 1) * MOE_HIDDEN)
        a1 = jnp.dot(hb, w1_ref[0, :, es], preferred_element_type=F32)
        a3 = jnp.dot(hb, w3_ref[0, :, es], preferred_element_type=F32)
        ge = jnp.sum(jnp.where(lane == MOE_GROUPS + grp * MOE_PER_GROUP + e, gate, 0.0), axis=-1, keepdims=True)
        hid = (jax.nn.silu(a1) * a3 * ge).astype(BF16)
        part = jnp.dot(hid, w2_ref[0, es, :], preferred_element_type=F32)
        acc = part if acc is None else acc + part
    return acc


def _moe_kernel(x_ref, g_ref, wr_ref, br_ref, w1_ref, w3_ref, w2_ref, o_ref, h_scr, gate_scr):
    grp = pl.program_id(1)

    @pl.when(grp == 0)
    def _():
        x = x_ref[...]
        h = _rms(x, g_ref[...])
        h_scr[...] = h.astype(BF16)
        logits = jnp.dot(h, wr_ref[...], preferred_element_type=F32,
                         precision=lax.Precision.HIGHEST) + br_ref[...]
        gate_scr[...] = _moe_gates(logits)[0]
        o_ref[...] = x

    o_ref[...] += _group_ffn(h_scr[...], gate_scr[...], grp, w1_ref, w3_ref, w2_ref)


def moe_dense(x, gain, w_r, b_r, w1g, w3g, w2g, layer, *, tm=512):
    n = x.shape[0]
    tm = min(tm, n)
    gain = gain.reshape(1, D_MODEL)
    row = pl.BlockSpec((tm, D_MODEL), lambda i, g: (i, 0))
    const2 = lambda a: pl.BlockSpec(a.shape, lambda i, g: (0, 0))
    goff = layer * MOE_GROUPS
    return pl.pallas_call(
        _moe_kernel, grid=(n // tm, MOE_GROUPS),
        in_specs=[row, const2(gain), const2(w_r), const2(b_r),
                  pl.BlockSpec((1, D_MODEL, MOE_GW), lambda i, g: (goff + g, 0, 0)),
                  pl.BlockSpec((1, D_MODEL, MOE_GW), lambda i, g: (goff + g, 0, 0)),
                  pl.BlockSpec((1, MOE_GW, D_MODEL), lambda i, g: (goff + g, 0, 0))],
        out_specs=row, out_shape=jax.ShapeDtypeStruct((n, D_MODEL), F32),
        scratch_shapes=[pltpu.VMEM((tm, D_MODEL), BF16), pltpu.VMEM((tm, ROUTER_LANES), F32)],
        compiler_params=_cparams("parallel", "arbitrary"), name="moe")(x, gain, w_r, b_r, w1g, w3g, w2g)


def _moe_route_kernel(x_ref, g_ref, wr_ref, br_ref, hg_ref, gid_ref):
    h = _rms(x_ref[...], g_ref[...])
    logits = jnp.dot(h, wr_ref[...], preferred_element_type=F32, precision=lax.Precision.HIGHEST) + br_ref[...]
    gate, g_idx = _moe_gates(logits)
    hg_ref[:, :D_MODEL] = h
    hg_ref[:, D_MODEL:] = gate
    gid_ref[...] = jnp.broadcast_to(g_idx, gid_ref.shape)


def _gather_rows_kernel(idx_ref, src_ref, *rest, rows, add):
    if add:
        add_ref, out_ref, buf, sem = rest
    else:
        out_ref, buf, sem = rest

    def start(r, carry):
        pltpu.make_async_copy(src_ref.at[pl.ds(idx_ref[0, 0, r], 1)], buf.at[pl.ds(r, 1)], sem).start()
        return carry

    lax.fori_loop(0, rows, start, 0, unroll=8)
    pltpu.make_async_copy(src_ref.at[pl.ds(0, rows)], buf, sem).wait()
    if add:
        out_ref[...] = add_ref[...] + buf[...]
    else:
        out_ref[...] = buf[...]


def gather_rows(src, idx, *, add=None, rows=MOE_TM):
    n_out = idx.shape[0]
    width = src.shape[1]
    nt = n_out // rows
    row = pl.BlockSpec((rows, width), lambda t: (t, 0))
    in_specs = [pl.BlockSpec((1, 1, rows), lambda t: (t, 0, 0), memory_space=pltpu.SMEM),
                pl.BlockSpec(memory_space=pl.ANY)]
    args = [idx.reshape(nt, 1, rows), src]
    if add is not None:
        in_specs.append(row)
        args.append(add)
    return pl.pallas_call(
        functools.partial(_gather_rows_kernel, rows=rows, add=add is not None), grid=(nt,),
        in_specs=in_specs, out_specs=row, out_shape=jax.ShapeDtypeStruct((n_out, width), F32),
        scratch_shapes=[pltpu.VMEM((rows, width), F32), pltpu.SemaphoreType.DMA(())],
        compiler_params=_cparams("arbitrary"), name="gather_rows")(*args)


def _moe_group_kernel(tg_ref, tv_ref, hg_ref, w1_ref, w3_ref, w2_ref, o_ref):
    t = pl.program_id(0)

    @pl.when(tv_ref[t] != 0)
    def _():
        o_ref[...] = _group_ffn(hg_ref[:, :D_MODEL].astype(BF16), hg_ref[:, D_MODEL:], tg_ref[t],
                                w1_ref, w3_ref, w2_ref)

    @pl.when(tv_ref[t] == 0)
    def _():
        o_ref[...] = jnp.zeros_like(o_ref)


def _moe_sort_plan(gid, n, tm):
    n_tiles = n // tm + MOE_GROUPS
    onehot = (gid[:, None] == jnp.arange(MOE_GROUPS, dtype=jnp.int32)[None, :]).astype(jnp.int32)
    csum = jnp.cumsum(onehot, axis=0)
    rank = jnp.take_along_axis(csum, gid[:, None], axis=1)[:, 0] - 1
    counts = csum[-1]
    tiles_per_g = (counts + tm - 1) // tm
    tile_end = jnp.cumsum(tiles_per_g)
    tile_off = tile_end - tiles_per_g
    dest = tile_off[gid] * tm + rank
    row_src = jnp.zeros((n_tiles * tm,), jnp.int32).at[dest].set(jnp.arange(n, dtype=jnp.int32))
    tix = jnp.arange(n_tiles, dtype=jnp.int32)
    tile_group = jnp.minimum(jnp.sum((tix[:, None] >= tile_end[None, :]).astype(jnp.int32), axis=1),
                             MOE_GROUPS - 1)
    tile_valid = (tix < tile_end[-1]).astype(jnp.int32)
    return dest, row_src, tile_group, tile_valid


def moe_sorted(x, gain, w_r, b_r, w1g, w3g, w2g, layer, *, tm=MOE_TM):
    n = x.shape[0]
    goff = layer * MOE_GROUPS
    row = lambda w: pl.BlockSpec((tm, w), lambda i: (i, 0))
    const = lambda a: pl.BlockSpec(a.shape, lambda i: (0, 0))
    gain = gain.reshape(1, D_MODEL)
    hg, gid = pl.pallas_call(
        _moe_route_kernel, grid=(n // tm,), in_specs=[row(D_MODEL), const(gain), const(w_r), const(b_r)],
        out_specs=(row(HG_W), row(ROUTER_LANES)),
        out_shape=(jax.ShapeDtypeStruct((n, HG_W), F32), jax.ShapeDtypeStruct((n, ROUTER_LANES), jnp.int32)),
        compiler_params=_cparams("parallel"), name="moe_route")(x, gain, w_r, b_r)
    dest, row_src, tile_group, tile_valid = _moe_sort_plan(gid[:, 0], n, tm)
    hg_sorted = gather_rows(hg, row_src, rows=tm)
    n_tiles = row_src.shape[0] // tm
    wspec = pl.BlockSpec((1, D_MODEL, MOE_GW), lambda t, tg, tv: (goff + tg[t], 0, 0))
    grid_spec = pltpu.PrefetchScalarGridSpec(
        num_scalar_prefetch=2, grid=(n_tiles,),
        in_specs=[pl.BlockSpec((tm, HG_W), lambda t, tg, tv: (t, 0)), wspec, wspec,
                  pl.BlockSpec((1, MOE_GW, D_MODEL), lambda t, tg, tv: (goff + tg[t], 0, 0))],
        out_specs=pl.BlockSpec((tm, D_MODEL), lambda t, tg, tv: (t, 0)))
    out_sorted = pl.pallas_call(
        _moe_group_kernel, grid_spec=grid_spec,
        out_shape=jax.ShapeDtypeStruct((n_tiles * tm, D_MODEL), F32),
        compiler_params=_cparams("arbitrary"), name="moe_group")(tile_group, tile_valid, hg_sorted, w1g, w3g, w2g)
    return gather_rows(out_sorted, dest, add=x, rows=tm)


def _group_weights(w1, w3, w2):
    ng = w1.shape[0] * MOE_GROUPS
    side = lambda w: jnp.transpose(w.astype(BF16).reshape(ng, MOE_PER_GROUP, D_MODEL, MOE_HIDDEN),
                                   (0, 2, 1, 3)).reshape(ng, D_MODEL, MOE_GW)
    return side(w1), side(w3), w2.astype(BF16).reshape(ng, MOE_GW, D_MODEL)


def _router_params(w_rc, b_rc, w_rf, b_rf):
    pad = ROUTER_LANES - MOE_GROUPS - MOE_EXPERTS
    w_r = jnp.concatenate([w_rc, w_rf, jnp.zeros((D_MODEL, pad), F32)], axis=1).astype(F32)
    b_r = jnp.concatenate([b_rc, b_rf, jnp.zeros((pad,), F32)]).reshape(1, ROUTER_LANES).astype(F32)
    return w_r, b_r


def _rmsnorm_kernel(x_ref, g_ref, o_ref):
    o_ref[...] = _rms(x_ref[...], g_ref[...])


def rmsnorm_rows(x, gain, *, tm=1024):
    n = x.shape[0]
    tm = min(tm, n)
    row = pl.BlockSpec((tm, D_MODEL), lambda i: (i, 0))
    return pl.pallas_call(
        _rmsnorm_kernel, grid=(n // tm,), in_specs=[row, pl.BlockSpec((1, D_MODEL), lambda i: (0, 0))],
        out_specs=row, out_shape=jax.ShapeDtypeStruct((n, D_MODEL), F32),
        compiler_params=_cparams("parallel"), name="rmsnorm")(x, gain.reshape(1, D_MODEL))


def _stack(parts):
    return parts[0][None] if len(parts) == 1 else jnp.stack(parts)


def _trunk(x, nb, w, s5_re, s5_im, rw_state, rw_shift, ret_state, mem_k, mem_v, pos0):
    n = x.shape[0]
    t_len = n // nb
    single = t_len == 1
    out_s5_re, out_s5_im, out_rw, out_shift, out_ret = [], [], [], [], []
    for layer in range(DEPTH):
        if layer % 2 == 0:
            i = layer // 2
            s5p = w['s5p'][i]
            rwp = tuple(w[k][i] for k in ('rw_mu', 'rw_w0', 'rw_w2', 'rw_a0', 'rw_a2', 'rw_g2',
                                          'rw_k_k', 'rw_k_a', 'rw_r_k', 'rw_ln_w', 'rw_ln_b'))
            w_in = w['w_in0_bf'][i]
            w_out = w['w_out0_bf'][i]
            if single:
                u, p = linear(x, w_in, gain=w['norm_mix'][layer], splits=(S5_WIDTH, RW_PROJ))
                y_s5, sr, si = s5_mixer(u.reshape(1, nb, S5_WIDTH), s5_re[i], s5_im[i], s5p,
                                        w['s5_d'][i], w['s5_w_glu'][i], tc=1)
                y_rw, srw = rwkv_step(p, rw_shift[i], rw_state[i], rwp)
                sh = p
                x = linear(y_s5.reshape(nb, S5_WIDTH), w_out[:S5_WIDTH], x2=y_rw, w2=w_out[S5_WIDTH:], residual=x)
            else:
                u, p = linear(x, w_in, gain=w['norm_mix'][layer], splits=(S5_WIDTH, RW_PROJ),
                              out_tmajor=True, batch=nb)
                y_s5, sr, si = s5_mixer(u, s5_re[i], s5_im[i], s5p, w['s5_d'][i], w['s5_w_glu'][i], tc=128)
                y_rw, srw, sh = rwkv_prompt(p, rw_shift[i], rw_state[i], rwp)
                x = linear(y_s5, w_out[:S5_WIDTH], x2=y_rw, w2=w_out[S5_WIDTH:], residual=x, x_tmajor=True)
            out_s5_re.append(sr.reshape(nb, S5_GROUPS, S5_N))
            out_s5_im.append(si.reshape(nb, S5_GROUPS, S5_N))
            out_rw.append(srw)
            out_shift.append(sh)
        else:
            j = layer // 2
            q, k, v, g = linear(x, w['w_in1_bf'][j], gain=w['norm_mix'][layer], out_dtype=BF16,
                                splits=(NQ, NQ, NV, NV), tm=256)
            if single:
                y, s_new = retention_step(q, k, v, g, ret_state[j], pos0=pos0)
            else:
                y, s_new = retention_prompt(q, k, v, g, nb=nb)
            x = linear(y, w['w_out1_bf'][j], residual=x)
            out_ret.append(s_new)
        w_q = w['w_mq_bf'][layer]
        w_o = w['w_mo_bf'][layer]
        if single:
            q = linear(x, w_q, gain=w['norm_mem'][layer])
            att = xattn_step(q, mem_k, mem_v, layer)
            x = linear(att, w_o, residual=x)
        else:
            x = xattn_prompt(x, w['norm_mem'][layer], w_q, mem_k[layer], mem_v[layer], w_o, nb=nb)
        w_r, b_r = w['router'][layer]
        moe = moe_sorted if n >= 4 * MOE_TM else moe_dense
        x = moe(x, w['norm_ffn'][layer], w_r, b_r, *w['moe_g'], layer)
    y = rmsnorm_rows(x, w['norm_final'])
    return (y, _stack(out_s5_re), _stack(out_s5_im), _stack(out_rw), _stack(out_shift), _stack(out_ret))


def kernel(x_prompt, x_sample, mem_prompt, state_s5_re, state_s5_im, state_rwkv, state_shift, state_ret, cache_mem_k, cache_mem_v, norm_mix, norm_mem, norm_ffn, norm_final, w_in0, w_out0, s5_a_re, s5_a_im, s5_b_re, s5_b_im, s5_c_re, s5_c_im, s5_d, s5_log_dt, s5_w_glu, rw_mu, rw_w0, rw_w2, rw_a0, rw_a2, rw_g2, rw_k_k, rw_k_a, rw_r_k, rw_ln_w, rw_ln_b, w_in1, w_out1, mem_norm, w_mq, w_mk, w_mv, w_mo, moe_w_rc, moe_b_rc, moe_w_rf, moe_b_rf, moe_w1, moe_w3, moe_w2):
    w = dict(norm_mix=norm_mix, norm_mem=norm_mem, norm_ffn=norm_ffn, norm_final=norm_final,
             w_in0=w_in0, w_out0=w_out0, s5_a_re=s5_a_re, s5_a_im=s5_a_im, s5_b_re=s5_b_re, s5_b_im=s5_b_im,
             s5_c_re=s5_c_re, s5_c_im=s5_c_im, s5_d=s5_d, s5_log_dt=s5_log_dt, s5_w_glu=s5_w_glu,
             rw_mu=rw_mu, rw_w0=rw_w0, rw_w2=rw_w2, rw_a0=rw_a0, rw_a2=rw_a2, rw_g2=rw_g2,
             rw_k_k=rw_k_k, rw_k_a=rw_k_a, rw_r_k=rw_r_k, rw_ln_w=rw_ln_w, rw_ln_b=rw_ln_b,
             w_in1=w_in1, w_out1=w_out1, w_mq=w_mq, w_mo=w_mo,
             moe_w_rc=moe_w_rc, moe_b_rc=moe_b_rc, moe_w_rf=moe_w_rf, moe_b_rf=moe_b_rf,
             moe_w1=moe_w1, moe_w3=moe_w3, moe_w2=moe_w2)
    nbp, t_len, _ = x_prompt.shape
    nbs = x_sample.shape[0]
    n_even, n_odd = state_s5_re.shape[0], state_ret.shape[0]
    for name in ('w_in0', 'w_out0', 'w_in1', 'w_out1', 'w_mq', 'w_mo'):
        w[name + '_bf'] = w[name].astype(BF16)
    w['s5p'] = [_s5_params(s5_a_re[i], s5_a_im[i], s5_b_re[i], s5_b_im[i], s5_c_re[i], s5_c_im[i], s5_log_dt[i])
                for i in range(n_even)]
    w['router'] = [_router_params(moe_w_rc[l], moe_b_rc[l], moe_w_rf[l], moe_b_rf[l]) for l in range(DEPTH)]
    w['moe_g'] = _group_weights(moe_w1, moe_w3, moe_w2)

    mem = mem_prompt.reshape(nbp * N_MEM, D_MODEL)
    mem_k_l, mem_v_l = [], []
    for layer in range(DEPTH):
        w_kv = jnp.concatenate([w_mk[layer], w_mv[layer]], axis=1).astype(BF16)
        mk, mv = linear(mem, w_kv, gain=mem_norm[layer], splits=(D_MODEL, D_MODEL))
        mem_k_l.append(mk.reshape(nbp, N_MEM, D_MODEL))
        mem_v_l.append(mv.reshape(nbp, N_MEM, D_MODEL))
    kv_shape = (DEPTH, nbp, N_MEM, MEM_HEADS, MEM_HD)
    mem_k_p = jnp.stack(mem_k_l).reshape(kv_shape)
    mem_v_p = jnp.stack(mem_v_l).reshape(kv_shape)

    zeros = lambda *shape: jnp.zeros(shape, F32)
    y_p, s5r_p, s5i_p, rw_p, sh_p, ret_p = _trunk(
        x_prompt.reshape(nbp * t_len, D_MODEL), nbp, w,
        zeros(n_even, nbp, S5_STATE), zeros(n_even, nbp, S5_STATE),
        zeros(n_even, nbp, RW_HEADS, RW_HD, RW_HD), zeros(n_even, nbp, RW_PROJ),
        None, mem_k_l, mem_v_l, 0.0)
    y_s, s5r_s, s5i_s, rw_s, sh_s, ret_s = _trunk(
        x_sample.reshape(nbs, D_MODEL), nbs, w,
        state_s5_re.reshape(n_even, nbs, S5_STATE), state_s5_im.reshape(n_even, nbs, S5_STATE),
        state_rwkv, state_shift, state_ret,
        cache_mem_k, cache_mem_v, float(PAST_LEN))
    return (y_p.reshape(nbp, t_len, D_MODEL), y_s.reshape(nbs, 1, D_MODEL),
            s5r_p, s5i_p, rw_p, sh_p, ret_p, mem_k_p, mem_v_p, s5r_s, s5i_s, rw_s, sh_s, ret_s)
```

```python
import functools
import math

import jax
import jax.numpy as jnp
from jax import lax
from jax.experimental import pallas as pl
from jax.experimental.pallas import tpu as pltpu

F32 = jnp.float32
BF16 = jnp.bfloat16

D_MODEL = 1024
DEPTH = 2
PAST_LEN = 16384
S5_WIDTH = 512
S5_GROUP = 16
S5_GROUPS = 32
S5_N = 64
S5_STATE = S5_GROUPS * S5_N
S5_GBLK = 8
RW_WIDTH = 512
RW_HD = 64
RW_HEADS = 8
RW_LORA = 256
RW_PROJ = 3 * RW_WIDTH + RW_LORA
IN0 = S5_WIDTH + RW_PROJ
RET_DK = 256
RET_HEADS = 4
RET_DV = 512
RET_CHUNK = 128
NQ = RET_HEADS * RET_DK
NV = RET_HEADS * RET_DV
IN1 = 2 * NQ + 2 * NV
N_MEM = 256
MEM_HEADS = 4
MEM_HD = 256
MOE_GROUPS = 4
MOE_PER_GROUP = 4
MOE_EXPERTS = 16
MOE_HIDDEN = 256
NORM_EPS = 1e-6
RW_GN_EPS = 64e-5
ROPE_BASE = 10000.0

VMEM_LIMIT = 56 * 1024 * 1024


def _cparams(*sem):
    return pltpu.CompilerParams(dimension_semantics=sem, vmem_limit_bytes=VMEM_LIMIT)


def _bdot(a, b):
    return jnp.dot(a.astype(BF16), b.astype(BF16), preferred_element_type=F32)


def _dot_nt(a, b):
    return lax.dot_general(a.astype(BF16), b.astype(BF16), (((1,), (1,)), ((), ())),
                           preferred_element_type=F32)


def _dot_tn(a, b):
    return lax.dot_general(a.astype(BF16), b.astype(BF16), (((0,), (0,)), ((), ())),
                           preferred_element_type=F32)


def _split3(x):
    hi = x.astype(BF16)
    r1 = x - hi.astype(F32)
    mid = r1.astype(BF16)
    lo = (r1 - mid.astype(F32)).astype(BF16)
    return hi, mid, lo


def _dot_exact_rhs(x, m_bf16):
    hi, mid, lo = _split3(x)
    acc = jnp.dot(hi, m_bf16, preferred_element_type=F32)
    acc = acc + jnp.dot(mid, m_bf16, preferred_element_type=F32)
    return acc + jnp.dot(lo, m_bf16, preferred_element_type=F32)


def _rms(x, g):
    ms = jnp.mean(x * x, axis=-1, keepdims=True)
    return x * lax.rsqrt(ms + NORM_EPS) * g


def _linear_kernel(*refs, norm, two, res):
    it = iter(refs)
    x_ref = next(it)
    g_ref = next(it) if norm else None
    w_ref = next(it)
    x2_ref = next(it) if two else None
    w2_ref = next(it) if two else None
    r_ref = next(it) if res else None
    o_refs = list(it)
    x = x_ref[...].astype(F32)
    if norm:
        x = _rms(x, g_ref[...])
    xb = x.astype(BF16)
    x2b = x2_ref[...].astype(BF16) if two else None
    col = 0
    for o_ref in o_refs:
        m = o_ref.shape[-1]
        step = next((s for s in (512, 256) if m % s == 0), m)
        for j in range(m // step):
            sl = slice(col + j * step, col + (j + 1) * step)
            acc = jnp.dot(xb, w_ref[:, sl], preferred_element_type=F32)
            if two:
                acc = acc + jnp.dot(x2b, w2_ref[:, sl], preferred_element_type=F32)
            if res:
                acc = acc + r_ref[:, sl]
            o_ref[:, j * step:(j + 1) * step] = acc.astype(o_ref.dtype)
        col += m


def _row_spec(tm, width, tmajor_b):
    if tmajor_b is None:
        return pl.BlockSpec((tm, width), lambda i: (i, 0))
    nb, tiles_per_b = tmajor_b
    return pl.BlockSpec((tm, width), lambda i: (i % tiles_per_b, i // tiles_per_b))


def linear(x, w, *, gain=None, x2=None, w2=None, residual=None, out_dtype=F32, tm=512,
           x_tmajor=False, out_tmajor=False, batch=None, splits=None, name="linear"):
    if x_tmajor:
        t_len, nb, k = x.shape
        n = t_len * nb
    else:
        n, k = x.shape
        nb = batch
        t_len = n // nb if nb else None
    m = w.shape[1]
    tm = min(tm, n if not (x_tmajor or out_tmajor) else t_len)
    assert n % tm == 0
    tiles_per_b = (t_len // tm) if (x_tmajor or out_tmajor) else None
    args, specs = [], []

    def add_rows(a, tmajor):
        width = a.shape[-1]
        args.append(a.reshape(t_len, nb * width) if tmajor else a)
        specs.append(_row_spec(tm, width, (nb, tiles_per_b) if tmajor else None))

    add_rows(x, x_tmajor)
    if gain is not None:
        args.append(gain.reshape(1, k).astype(F32))
        specs.append(pl.BlockSpec((1, k), lambda i: (0, 0)))
    args.append(w)
    specs.append(pl.BlockSpec(w.shape, lambda i: (0, 0)))
    if x2 is not None:
        add_rows(x2, x_tmajor)
        args.append(w2)
        specs.append(pl.BlockSpec(w2.shape, lambda i: (0, 0)))
    if residual is not None:
        add_rows(residual, False)
    widths = tuple(splits) if splits else (m,)
    assert sum(widths) == m
    if out_tmajor:
        out_shape = [jax.ShapeDtypeStruct((t_len, nb * mw), out_dtype) for mw in widths]
    else:
        out_shape = [jax.ShapeDtypeStruct((n, mw), out_dtype) for mw in widths]
    out_specs = [_row_spec(tm, mw, (nb, tiles_per_b) if out_tmajor else None) for mw in widths]
    kern = functools.partial(_linear_kernel, norm=gain is not None, two=x2 is not None,
                             res=residual is not None)
    outs = pl.pallas_call(
        kern, grid=(n // tm,), in_specs=specs, out_specs=out_specs, out_shape=out_shape,
        compiler_params=_cparams("parallel"), name=name)(*args)
    if out_tmajor:
        outs = [o.reshape(t_len, nb, mw) for o, mw in zip(outs, widths)]
    return outs if splits else outs[0]


def _s5_kernel(u_ref, h_re_ref, h_im_ref, abar_re_ref, abar_im_ref, bb_re_ref, bb_im_ref,
               cc_re_ref, cc_im_ref, d_ref, wglu_ref, y_ref, s_re_ref, s_im_ref,
               x_re, x_im, st_re, st_im, il_scr, *, tc, nb, flat):
    c = pl.program_id(0)
    nlb = S5_WIDTH // 128
    rows = tc * nb
    nblk = S5_GROUPS // S5_GBLK
    bw_in = S5_GBLK * S5_GROUP
    bw_st = S5_GBLK * S5_N

    @pl.when(c == 0)
    def _():
        st_re[...] = h_re_ref[...]
        st_im[...] = h_im_ref[...]

    if flat:
        for b in range(nb):
            for j in range(nlb):
                il_scr[j, pl.ds(b, tc, stride=nb), :] = u_ref[:, b * S5_WIDTH + j * 128:b * S5_WIDTH + (j + 1) * 128]
        u = jnp.concatenate([il_scr[j] for j in range(nlb)], axis=-1)
    else:
        u = u_ref[...].reshape(rows, S5_WIDTH)
    ub = u.astype(BF16)
    for gb in range(nblk):
        ui = ub[:, gb * bw_in:(gb + 1) * bw_in]
        x_re[:, gb * bw_st:(gb + 1) * bw_st] = jnp.dot(ui, bb_re_ref[gb], preferred_element_type=F32)
        x_im[:, gb * bw_st:(gb + 1) * bw_st] = jnp.dot(ui, bb_im_ref[gb], preferred_element_type=F32)

    lane_blk = 1024
    for lb in range(S5_STATE // lane_blk):
        sl = slice(lb * lane_blk, (lb + 1) * lane_blk)
        ar = jnp.broadcast_to(abar_re_ref[:, sl], (nb, lane_blk))
        ai = jnp.broadcast_to(abar_im_ref[:, sl], (nb, lane_blk))

        def body(t, carry, sl=sl, ar=ar, ai=ai):
            xr, xi = carry
            r0 = pl.multiple_of(t * nb, nb)
            br = x_re[pl.ds(r0, nb), sl]
            bi = x_im[pl.ds(r0, nb), sl]
            nr = ar * xr - ai * xi + br
            ni = ar * xi + ai * xr + bi
            x_re[pl.ds(r0, nb), sl] = nr
            x_im[pl.ds(r0, nb), sl] = ni
            return nr, ni

        fr, fi = lax.fori_loop(0, tc, body, (st_re[:, sl], st_im[:, sl]), unroll=min(tc, 4))
        st_re[:, sl] = fr
        st_im[:, sl] = fi

    for gb in range(nblk):
        xr = x_re[:, gb * bw_st:(gb + 1) * bw_st].astype(BF16)
        xi = x_im[:, gb * bw_st:(gb + 1) * bw_st].astype(BF16)
        yb = (jnp.dot(xr, cc_re_ref[gb], preferred_element_type=F32)
              - jnp.dot(xi, cc_im_ref[gb], preferred_element_type=F32))
        cs = slice(gb * bw_in, (gb + 1) * bw_in)
        yb = yb + d_ref[:, cs] * u[:, cs]
        x_re[:, cs] = jax.nn.gelu(yb)
    y = x_re[:, :S5_WIDTH]
    y = y * jax.nn.sigmoid(jnp.dot(y.astype(BF16), wglu_ref[...], preferred_element_type=F32))
    if flat:
        for j in range(nlb):
            il_scr[j] = y[:, j * 128:(j + 1) * 128]
        for b in range(nb):
            for j in range(nlb):
                y_ref[:, b * S5_WIDTH + j * 128:b * S5_WIDTH + (j + 1) * 128] = (
                    il_scr[j, pl.ds(b, tc, stride=nb), :].astype(y_ref.dtype))
    else:
        y_ref[...] = y.reshape(y_ref.shape).astype(y_ref.dtype)

    @pl.when(c == pl.num_programs(0) - 1)
    def _():
        s_re_ref[...] = st_re[...]
        s_im_ref[...] = st_im[...]


def _s5_params(a_re, a_im, b_re, b_im, c_re, c_im, log_dt):
    dt = jnp.exp(log_dt.astype(F32))[:, None]
    ar, ai = a_re.astype(F32), a_im.astype(F32)
    mag = jnp.exp(dt * ar)
    abar_re, abar_im = mag * jnp.cos(dt * ai), mag * jnp.sin(dt * ai)
    den = ar * ar + ai * ai
    nr = abar_re - 1.0
    coef_re = (nr * ar + abar_im * ai) / den
    coef_im = (abar_im * ar - nr * ai) / den
    cr, ci = coef_re[..., None], coef_im[..., None]
    brf, bif = b_re.astype(F32), b_im.astype(F32)
    bb_re = cr * brf - ci * bif
    bb_im = cr * bif + ci * brf
    nblk = S5_GROUPS // S5_GBLK
    eye = jnp.eye(S5_GBLK, dtype=F32)

    def blockdiag_in(bb):
        t = jnp.transpose(bb, (0, 2, 1)).reshape(nblk, S5_GBLK, S5_GROUP, S5_N)
        m = jnp.einsum('kgcn,gh->kgchn', t, eye)
        return m.reshape(nblk, S5_GBLK * S5_GROUP, S5_GBLK * S5_N).astype(BF16)

    def blockdiag_out(cc):
        t = jnp.transpose(cc.astype(F32), (0, 2, 1)).reshape(nblk, S5_GBLK, S5_N, S5_GROUP)
        m = jnp.einsum('khnc,hg->khngc', t, eye)
        return m.reshape(nblk, S5_GBLK * S5_N, S5_GBLK * S5_GROUP).astype(BF16)

    return (abar_re.reshape(1, S5_STATE), abar_im.reshape(1, S5_STATE),
            blockdiag_in(bb_re), blockdiag_in(bb_im), blockdiag_out(c_re), blockdiag_out(c_im))


def s5_mixer(u_tm, h_re, h_im, params, d_skip, w_glu, *, tc):
    t_len, nb, _ = u_tm.shape
    abar_re, abar_im, bb_re, bb_im, cc_re, cc_im = params
    tc = min(tc, t_len)
    assert t_len % tc == 0 and nb % 8 == 0
    rows = tc * nb
    flat = t_len > 1
    full = lambda a: pl.BlockSpec(a.shape, lambda c: (0,) * a.ndim)
    if flat:
        u_arg = u_tm.reshape(t_len, nb * S5_WIDTH)
        io_spec = pl.BlockSpec((tc, nb * S5_WIDTH), lambda c: (c, 0))
        y_shape = jax.ShapeDtypeStruct((t_len, nb * S5_WIDTH), BF16)
    else:
        u_arg = u_tm
        io_spec = pl.BlockSpec((tc, nb, S5_WIDTH), lambda c: (c, 0, 0))
        y_shape = jax.ShapeDtypeStruct((t_len, nb, S5_WIDTH), BF16)
    args = (u_arg, h_re, h_im, abar_re, abar_im, bb_re, bb_im, cc_re, cc_im,
            d_skip.reshape(1, S5_WIDTH).astype(F32), w_glu.astype(BF16))
    in_specs = [io_spec] + [full(a) for a in args[1:]]
    st_shape = jax.ShapeDtypeStruct((nb, S5_STATE), F32)
    st_spec = pl.BlockSpec((nb, S5_STATE), lambda c: (0, 0))
    scratch = [pltpu.VMEM((rows, S5_STATE), F32), pltpu.VMEM((rows, S5_STATE), F32),
               pltpu.VMEM((nb, S5_STATE), F32), pltpu.VMEM((nb, S5_STATE), F32),
               pltpu.VMEM((S5_WIDTH // 128, rows if flat else 8, 128), F32)]
    y, s_re, s_im = pl.pallas_call(
        functools.partial(_s5_kernel, tc=tc, nb=nb, flat=flat), grid=(t_len // tc,), in_specs=in_specs,
        out_specs=(io_spec, st_spec, st_spec), out_shape=(y_shape, st_shape, st_shape),
        scratch_shapes=scratch, compiler_params=_cparams("arbitrary"), name="s5_mixer")(*args)
    return y.reshape(t_len, nb, S5_WIDTH), s_re, s_im


def _head_ones():
    i = lax.broadcasted_iota(jnp.int32, (RW_WIDTH, RW_WIDTH), 0) // RW_HD
    j = lax.broadcasted_iota(jnp.int32, (RW_WIDTH, RW_WIDTH), 1) // RW_HD
    return jnp.where(i == j, 1.0, 0.0).astype(BF16)


def _softplus(z):
    return jnp.maximum(z, 0.0) + jnp.log1p(jnp.exp(-jnp.abs(z)))


def _rw_prep(p, p_prev, prm, ones_bd):
    mu, w0, w2, a0, a2, g2, k_k, k_a = prm
    xm = p + (p_prev - p) * mu
    o1, o2, o3 = RW_WIDTH, 2 * RW_WIDTH, 3 * RW_WIDTH
    r, k, v = xm[:, :o1], xm[:, o1:o2], xm[:, o2:o3]
    wd, ad, gd = xm[:, o3:o3 + 64], xm[:, o3 + 64:o3 + 128], xm[:, o3 + 128:]
    w = -_softplus(-(w0 + _bdot(jnp.tanh(wd), w2))) - 0.5
    lw = -jnp.exp(w)
    a = jax.nn.sigmoid(a0 + _bdot(ad, a2))
    g = _bdot(jax.nn.sigmoid(gd), g2)
    kk = k * k_k
    ss = _dot_exact_rhs(kk * kk, ones_bd)
    kk = kk / jnp.maximum(jnp.sqrt(ss), 1e-12)
    k = k * (1.0 + (a - 1.0) * k_a)
    return r, lw, k, v, -kk, kk * a, g


def _rw_post(o, r, k, v, g, r_k, ln_w, ln_b, ones_bd):
    inv = 1.0 / RW_HD
    mean = _dot_exact_rhs(o, ones_bd) * inv
    d = o - mean
    var = _dot_exact_rhs(d * d, ones_bd) * inv
    on = d * lax.rsqrt(var + RW_GN_EPS) * ln_w + ln_b
    bonus = _dot_exact_rhs(r * k * r_k, ones_bd) * v
    return (on + bonus) * g


def _rw_chunk_kernel(p_ref, shift_ref, h0_ref, mu_ref, w0_ref, w2_ref, a0_ref, a2_ref, g2_ref,
                     kk_ref, ka_ref, rk_ref, lnw_ref, lnb_ref,
                     y_ref, hfin_ref, shout_ref, prev_scr, h_scr, o_scr, *, c_len, bs):
    c = pl.program_id(1)
    nc = pl.num_programs(1)
    cl = c_len

    @pl.when(c == 0)
    def _():
        prev_scr[...] = shift_ref[:, 0, :]
        h_scr[...] = h0_ref[...]

    ones_bd = _head_ones()
    row = lax.broadcasted_iota(jnp.int32, (cl, RW_PROJ), 0)
    ps, pprevs = [], []
    for bi in range(bs):
        p = p_ref[:, bi * RW_PROJ:(bi + 1) * RW_PROJ]
        pprevs.append(jnp.where(row == 0, prev_scr[bi:bi + 1, :], pltpu.roll(p, 1, 0)))
        ps.append(p)
    p_all = jnp.concatenate(ps, axis=0) if bs > 1 else ps[0]
    pprev_all = jnp.concatenate(pprevs, axis=0) if bs > 1 else pprevs[0]
    prm = (mu_ref[...], w0_ref[...], w2_ref[...], a0_ref[...], a2_ref[...], g2_ref[...],
           kk_ref[...], ka_ref[...])
    r, lw, k, v, a, b, g = _rw_prep(p_all, pprev_all, prm, ones_bd)

    ti = lax.broadcasted_iota(jnp.int32, (cl, cl), 0)
    si = lax.broadcasted_iota(jnp.int32, (cl, cl), 1)
    lmat = jnp.where(ti >= si, 1.0, 0.0).astype(BF16)
    eye = jnp.where(ti == si, 1.0, 0.0)
    mi = lax.broadcasted_iota(jnp.int32, (2 * cl, 3 * cl), 0)
    mj = lax.broadcasted_iota(jnp.int32, (2 * cl, 3 * cl), 1)
    t_row = jnp.where(mi >= cl, mi - cl, mi)
    s_col = jnp.where(mj < cl, mj, jnp.where(mj >= 2 * cl, mj - 2 * cl, -4 * cl))
    keep = (t_row - s_col) >= jnp.where(mi >= cl, 0, 1)
    eye_bf = eye.astype(BF16)

    lhs_l, rhs_l, vh_l, hcat_l, kb_l, etot_l = [], [], [], [], [], []
    for bi in range(bs):
        rs = slice(bi * cl, (bi + 1) * cl)
        lw_b = lw[rs]
        l_hi, l_mid, l_lo = _split3(lw_b)
        cum = (jnp.dot(lmat, l_hi, preferred_element_type=F32)
               + jnp.dot(lmat, l_mid, preferred_element_type=F32)
               + jnp.dot(lmat, l_lo, preferred_element_type=F32))
        tot = cum[cl - 1:cl, :]
        e_neg = jnp.exp(-cum)
        e_rem = jnp.exp(tot - cum)
        at = (a[rs] * jnp.exp(cum - lw_b)).astype(BF16)
        rt = (r[rs] * jnp.exp(cum)).astype(BF16)
        bt = (b[rs] * e_neg).astype(BF16)
        kt = (k[rs] * e_neg).astype(BF16)
        bh = (b[rs] * e_rem).astype(BF16)
        kh = (k[rs] * e_rem).astype(BF16)
        e_tot = jnp.exp(tot)
        vb = v[rs].astype(BF16)
        for h in range(RW_HEADS):
            hs = slice(h * RW_HD, (h + 1) * RW_HD)
            lhs_l.append(jnp.concatenate([at[:, hs], rt[:, hs]], axis=0))
            rhs_l.append(jnp.concatenate([kt[:, hs], eye_bf, bt[:, hs]], axis=0))
            vh_l.append(vb[:, hs])
            kb_l.append(jnp.concatenate([kh[:, hs], bh[:, hs]], axis=0))
            etot_l.append(jnp.sum(eye * e_tot[:, hs], axis=-1, keepdims=True))
            hcat_l.append(h_scr[bi, h])

    nitem = bs * RW_HEADS
    items = range(nitem)
    aa_l = [jnp.where(keep, _dot_nt(lhs_l[i], rhs_l[i]), 0.0).astype(BF16) for i in items]
    pw_l = [aa_l[i][:cl, 2 * cl:] for i in items]
    tinv_l = [eye_bf + pw_l[i] for i in items]
    for _ in range(int(math.log2(cl)) - 1):
        pw_l = [jnp.dot(pw_l[i], pw_l[i], preferred_element_type=F32).astype(BF16) for i in items]
        tinv_l = [jnp.dot(tinv_l[i], eye_bf + pw_l[i], preferred_element_type=F32).astype(BF16) for i in items]
    vh_cat = [jnp.concatenate([vh_l[i], hcat_l[i].astype(BF16)], axis=0) for i in items]
    x1_l = [jnp.dot(aa_l[i][:cl, :2 * cl], vh_cat[i], preferred_element_type=F32).astype(BF16) for i in items]
    u_l = [jnp.dot(tinv_l[i], x1_l[i], preferred_element_type=F32).astype(BF16) for i in items]
    o_l = [jnp.dot(aa_l[i][cl:, :], jnp.concatenate([vh_cat[i], u_l[i]], axis=0),
                   preferred_element_type=F32) for i in items]
    hn_l = [hcat_l[i] * etot_l[i]
            + lax.dot_general(kb_l[i], jnp.concatenate([vh_l[i], u_l[i]], axis=0), (((0,), (0,)), ((), ())),
                              preferred_element_type=F32) for i in items]

    for bi in range(bs):
        for h in range(RW_HEADS):
            i = bi * RW_HEADS + h
            o_scr[bi * cl:(bi + 1) * cl, h * RW_HD:(h + 1) * RW_HD] = o_l[i]
            h_scr[bi, h] = hn_l[i]
        prev_scr[bi:bi + 1, :] = ps[bi][cl - 1:cl, :]

    y = _rw_post(o_scr[...], r, k, v, g, rk_ref[...], lnw_ref[...], lnb_ref[...], ones_bd)
    for bi in range(bs):
        y_ref[:, bi * RW_WIDTH:(bi + 1) * RW_WIDTH] = y[bi * cl:(bi + 1) * cl].astype(y_ref.dtype)

    @pl.when(c == nc - 1)
    def _():
        hfin_ref[...] = h_scr[...]
        for bi in range(bs):
            shout_ref[bi] = ps[bi][cl - 1:cl, :]


def _rw_param_args(mu, w0, w2, a0, a2, g2, k_k, k_a, r_k, ln_w, ln_b):
    row = lambda z: z.reshape(1, -1).astype(F32)
    return (row(mu), row(w0), w2.astype(BF16), row(a0), a2.astype(BF16), g2.astype(BF16),
            row(k_k), row(k_a), row(r_k), row(ln_w), row(ln_b))


def rwkv_prompt(p_tm, shift, s0, params, *, bs=4):
    c_len = RW_HD
    t_len, nb, _ = p_tm.shape
    assert t_len % c_len == 0 and nb % bs == 0
    prm = _rw_param_args(*params)
    const = lambda a: pl.BlockSpec(a.shape, lambda b, c: (0,) * a.ndim)
    st_spec = pl.BlockSpec((bs, RW_HEADS, RW_HD, RW_HD), lambda b, c: (b, 0, 0, 0))
    sh_spec = pl.BlockSpec((bs, 1, RW_PROJ), lambda b, c: (b, 0, 0))
    in_specs = [pl.BlockSpec((c_len, bs * RW_PROJ), lambda b, c: (c, b)), sh_spec, st_spec] + [const(a) for a in prm]
    out_shape = (jax.ShapeDtypeStruct((t_len, nb * RW_WIDTH), BF16),
                 jax.ShapeDtypeStruct((nb, RW_HEADS, RW_HD, RW_HD), F32),
                 jax.ShapeDtypeStruct((nb, 1, RW_PROJ), F32))
    out_specs = (pl.BlockSpec((c_len, bs * RW_WIDTH), lambda b, c: (c, b)), st_spec, sh_spec)
    scratch = [pltpu.VMEM((bs, RW_PROJ), F32), pltpu.VMEM((bs, RW_HEADS, RW_HD, RW_HD), F32),
               pltpu.VMEM((bs * c_len, RW_WIDTH), F32)]
    h0 = jnp.swapaxes(s0, -1, -2)
    y, h_fin, sh = pl.pallas_call(
        functools.partial(_rw_chunk_kernel, c_len=c_len, bs=bs), grid=(nb // bs, t_len // c_len),
        in_specs=in_specs, out_specs=out_specs, out_shape=out_shape, scratch_shapes=scratch,
        compiler_params=_cparams("parallel", "arbitrary"), name="rwkv_prompt")(
            p_tm.reshape(t_len, nb * RW_PROJ), shift.reshape(nb, 1, RW_PROJ), h0, *prm)
    return y.reshape(t_len, nb, RW_WIDTH), jnp.swapaxes(h_fin, -1, -2), sh.reshape(nb, RW_PROJ)


def _rw_step_prep_kernel(p_ref, shift_ref, mu_ref, w0_ref, w2_ref, a0_ref, a2_ref, g2_ref, kk_ref, ka_ref,
                         r_ref, w_ref, k_ref, v_ref, a_ref, b_ref, g_ref):
    prm = (mu_ref[...], w0_ref[...], w2_ref[...], a0_ref[...], a2_ref[...], g2_ref[...],
           kk_ref[...], ka_ref[...])
    r, lw, k, v, a, b, g = _rw_prep(p_ref[...], shift_ref[...], prm, _head_ones())
    r_ref[...] = r
    w_ref[...] = jnp.exp(lw)
    k_ref[...] = k
    v_ref[...] = v
    a_ref[...] = a
    b_ref[...] = b
    g_ref[...] = g


def _rw_step_core_kernel(s_ref, r_ref, w_ref, k_ref, a_ref, b_ref, v_ref, s_out_ref, o_ref):
    s = s_ref[...]
    sa = jnp.sum(s * a_ref[...], axis=-1, keepdims=True)
    s_new = s * w_ref[...] + sa * b_ref[...] + v_ref[...] * k_ref[...]
    s_out_ref[...] = s_new
    o_ref[...] = jnp.sum(s_new * r_ref[...], axis=-1, keepdims=True)


def _rw_step_post_kernel(o_ref, r_ref, k_ref, v_ref, g_ref, rk_ref, lnw_ref, lnb_ref, y_ref):
    y_ref[...] = _rw_post(o_ref[...], r_ref[...], k_ref[...], v_ref[...], g_ref[...],
                          rk_ref[...], lnw_ref[...], lnb_ref[...], _head_ones()).astype(y_ref.dtype)


def rwkv_step(p, shift, s0, params, *, bt=8):
    n = p.shape[0]
    prm = _rw_param_args(*params)
    vec = jax.ShapeDtypeStruct((n, RW_WIDTH), F32)
    r, w, k, v, a, b, g = pl.pallas_call(
        _rw_step_prep_kernel, out_shape=(vec,) * 7, name="rwkv_step_prep")(p, shift, *prm[:8])
    rows = lambda z: z.reshape(n, RW_HEADS, 1, RW_HD)
    row_spec = pl.BlockSpec((bt, RW_HEADS, 1, RW_HD), lambda i: (i, 0, 0, 0))
    col_spec = pl.BlockSpec((bt, RW_HEADS, RW_HD, 1), lambda i: (i, 0, 0, 0))
    st_spec = pl.BlockSpec((bt, RW_HEADS, RW_HD, RW_HD), lambda i: (i, 0, 0, 0))
    s_new, o = pl.pallas_call(
        _rw_step_core_kernel, grid=(n // bt,),
        in_specs=[st_spec] + [row_spec] * 5 + [col_spec], out_specs=(st_spec, col_spec),
        out_shape=(jax.ShapeDtypeStruct(s0.shape, F32), jax.ShapeDtypeStruct((n, RW_HEADS, RW_HD, 1), F32)),
        compiler_params=_cparams("parallel"), name="rwkv_step_core")(
            s0, rows(r), rows(w), rows(k), rows(a), rows(b), v.reshape(n, RW_HEADS, RW_HD, 1))
    y = pl.pallas_call(
        _rw_step_post_kernel, out_shape=jax.ShapeDtypeStruct((n, RW_WIDTH), BF16), name="rwkv_step_post")(
            o.reshape(n, RW_WIDTH), r, k, v, g, *prm[8:])
    return y, s_new


RET_LOG_G = tuple(math.log(1.0 - 2.0 ** (-5.0 - h)) for h in range(RET_HEADS))


def _rope_tables(pos, half):
    j = lax.broadcasted_iota(jnp.int32, (1, half), 1).astype(F32)
    inv = jnp.exp(j * (-math.log(ROPE_BASE) / half))
    ang = pos * inv
    return jnp.cos(ang), jnp.sin(ang)


def _rope(x, cos, sin):
    half = RET_DK // 2
    outs = []
    for h in range(RET_HEADS):
        x1 = x[:, h * RET_DK:h * RET_DK + half]
        x2 = x[:, h * RET_DK + half:(h + 1) * RET_DK]
        outs += [x1 * cos - x2 * sin, x1 * sin + x2 * cos]
    return jnp.concatenate(outs, axis=-1)


def _ret_norm_gate(o, g):
    o = o * lax.rsqrt(jnp.mean(o * o, axis=-1, keepdims=True) + NORM_EPS)
    return jax.nn.silu(g) * o


def _ret_tables_kernel(cos_ref, sin_ref, dmask_ref, qdec_ref, kdec_ref, *, c_len):
    t_len = cos_ref.shape[0]
    pos = lax.broadcasted_iota(jnp.int32, (t_len, 1), 0).astype(F32)
    cos, sin = _rope_tables(pos, RET_DK // 2)
    cos_ref[...] = cos
    sin_ref[...] = sin
    ti = lax.broadcasted_iota(jnp.int32, (c_len, 1), 0).astype(F32)
    ii = lax.broadcasted_iota(jnp.int32, (c_len, c_len), 0)
    jj = lax.broadcasted_iota(jnp.int32, (c_len, c_len), 1)
    diff = (ii - jj).astype(F32)
    for h in range(RET_HEADS):
        lg = RET_LOG_G[h]
        dmask_ref[h] = jnp.where(diff >= 0, jnp.exp(lg * jnp.maximum(diff, 0.0)), 0.0)
        qdec_ref[h] = jnp.exp(lg * (ti + 1.0))
        kdec_ref[h] = jnp.exp(lg * (c_len - 1.0 - ti))


def _ret_chunk_kernel(q_ref, k_ref, v_ref, g_ref, cos_ref, sin_ref, dmask_ref, qdec_ref, kdec_ref,
                      y_ref, sfin_ref, s_scr, *, c_len):
    c = pl.program_id(1)

    @pl.when(c == 0)
    def _():
        s_scr[...] = jnp.zeros_like(s_scr)

    cos, sin = cos_ref[...], sin_ref[...]
    q = _rope(q_ref[...].astype(F32), cos, sin)
    k = _rope(k_ref[...].astype(F32), cos, sin) * (RET_DK ** -0.5)
    for h in range(RET_HEADS):
        c_dec = math.exp(RET_LOG_G[h] * c_len)
        qh = q[:, h * RET_DK:(h + 1) * RET_DK]
        kh = k[:, h * RET_DK:(h + 1) * RET_DK]
        vh = v_ref[:, h * RET_DV:(h + 1) * RET_DV]
        s_h = s_scr[h]
        sc = _dot_nt(qh, kh) * dmask_ref[h]
        o = _bdot(sc, vh) + _bdot(qh * qdec_ref[h], s_h)
        s_scr[h] = s_h * c_dec + _dot_tn(kh * kdec_ref[h], vh)
        gh = g_ref[:, h * RET_DV:(h + 1) * RET_DV].astype(F32)
        y_ref[:, h * RET_DV:(h + 1) * RET_DV] = _ret_norm_gate(o, gh).astype(y_ref.dtype)

    @pl.when(c == pl.num_programs(1) - 1)
    def _():
        sfin_ref[0] = s_scr[...]


def retention_prompt(q, k, v, g, *, nb, c_len=RET_CHUNK):
    n = q.shape[0]
    t_len = n // nb
    nc = t_len // c_len
    half = RET_DK // 2
    tabs = pl.pallas_call(
        functools.partial(_ret_tables_kernel, c_len=c_len),
        out_shape=(jax.ShapeDtypeStruct((t_len, half), F32), jax.ShapeDtypeStruct((t_len, half), F32),
                   jax.ShapeDtypeStruct((RET_HEADS, c_len, c_len), F32),
                   jax.ShapeDtypeStruct((RET_HEADS, c_len, 1), F32),
                   jax.ShapeDtypeStruct((RET_HEADS, c_len, 1), F32)),
        name="retention_tables")()
    spec = lambda w: pl.BlockSpec((c_len, w), lambda b, c: (b * nc + c, 0))
    pos_spec = pl.BlockSpec((c_len, half), lambda b, c: (c, 0))
    const = lambda a: pl.BlockSpec(a.shape, lambda b, c: (0, 0, 0))
    st_spec = pl.BlockSpec((1, RET_HEADS, RET_DK, RET_DV), lambda b, c: (b, 0, 0, 0))
    return pl.pallas_call(
        functools.partial(_ret_chunk_kernel, c_len=c_len), grid=(nb, nc),
        in_specs=[spec(NQ), spec(NQ), spec(NV), spec(NV), pos_spec, pos_spec] + [const(a) for a in tabs[2:]],
        out_specs=(spec(NV), st_spec),
        out_shape=(jax.ShapeDtypeStruct((n, NV), BF16),
                   jax.ShapeDtypeStruct((nb, RET_HEADS, RET_DK, RET_DV), F32)),
        scratch_shapes=[pltpu.VMEM((RET_HEADS, RET_DK, RET_DV), F32)],
        compiler_params=_cparams("parallel", "arbitrary"), name="retention_prompt")(q, k, v, g, *tabs)


def _ret_step_rope_kernel(q_ref, k_ref, qo_ref, ko_ref, *, pos0):
    pos = jnp.full((q_ref.shape[0], 1), pos0, F32)
    cos, sin = _rope_tables(pos, RET_DK // 2)
    qo_ref[...] = _rope(q_ref[...].astype(F32), cos, sin)
    ko_ref[...] = _rope(k_ref[...].astype(F32), cos, sin) * (RET_DK ** -0.5)


def _ret_step_core_kernel(s_ref, q_ref, k_ref, v_ref, g_ref, s_out_ref, y_ref):
    for i in range(s_ref.shape[0]):
        for h in range(RET_HEADS):
            gam = math.exp(RET_LOG_G[h])
            s_h = s_ref[i, h]
            qc = q_ref[i, h]
            kc = k_ref[i, h]
            vr = v_ref[i, h].astype(F32)
            qk = jnp.sum(qc * kc, axis=0, keepdims=True)
            o = qk * vr + jnp.sum((qc * gam) * s_h, axis=0, keepdims=True)
            s_out_ref[i, h] = s_h * gam + kc * vr
            y_ref[i, h] = _ret_norm_gate(o, g_ref[i, h].astype(F32)).astype(y_ref.dtype)


def retention_step(q, k, v, g, s0, *, pos0):
    n = q.shape[0]
    vec = jax.ShapeDtypeStruct((n, NQ), F32)
    qr, kr = pl.pallas_call(functools.partial(_ret_step_rope_kernel, pos0=pos0), out_shape=(vec, vec),
                            name="retention_step_rope")(q, k)
    col = lambda z: z.reshape(n, RET_HEADS, RET_DK, 1)
    row = lambda z: z.reshape(n, RET_HEADS, 1, RET_DV)
    tb = 2
    st_spec = pl.BlockSpec((tb, RET_HEADS, RET_DK, RET_DV), lambda i: (i, 0, 0, 0))
    col_spec = pl.BlockSpec((tb, RET_HEADS, RET_DK, 1), lambda i: (i, 0, 0, 0))
    row_spec = pl.BlockSpec((tb, RET_HEADS, 1, RET_DV), lambda i: (i, 0, 0, 0))
    s_new, y = pl.pallas_call(
        _ret_step_core_kernel, grid=(n // tb,),
        in_specs=[st_spec, col_spec, col_spec, row_spec, row_spec], out_specs=(st_spec, row_spec),
        out_shape=(jax.ShapeDtypeStruct(s0.shape, F32), jax.ShapeDtypeStruct((n, RET_HEADS, 1, RET_DV), BF16)),
        compiler_params=_cparams("parallel"), name="retention_step_core")(
            s0, col(qr), col(kr), row(v), row(g))
    return y.reshape(n, NV), s_new


def _xattn_prompt_kernel(x_ref, g_ref, wq_ref, mk_ref, mv_ref, wo_ref, o_ref, att_scr):
    x = x_ref[...]
    q = jnp.dot(_rms(x, g_ref[...]).astype(BF16), wq_ref[...], preferred_element_type=F32)
    for h in range(MEM_HEADS):
        hs = slice(h * MEM_HD, (h + 1) * MEM_HD)
        s = _dot_nt(q[:, hs], mk_ref[0, :, hs]) * (MEM_HD ** -0.5)
        s = s - jnp.max(s, axis=-1, keepdims=True)
        e = jnp.exp(s)
        p = e / jnp.sum(e, axis=-1, keepdims=True)
        att_scr[:, hs] = _bdot(p, mv_ref[0, :, hs])
    o_ref[...] = x + jnp.dot(att_scr[...].astype(BF16), wo_ref[...], preferred_element_type=F32)


def xattn_prompt(x, gain, w_q, mem_k, mem_v, w_o, *, nb, tm=512):
    n = x.shape[0]
    tiles_per_b = n // nb // tm
    row = pl.BlockSpec((tm, D_MODEL), lambda i: (i, 0))
    wspec = pl.BlockSpec((D_MODEL, D_MODEL), lambda i: (0, 0))
    mspec = pl.BlockSpec((1, N_MEM, D_MODEL), lambda i: (i // tiles_per_b, 0, 0))
    return pl.pallas_call(
        _xattn_prompt_kernel, grid=(n // tm,),
        in_specs=[row, pl.BlockSpec((1, D_MODEL), lambda i: (0, 0)), wspec, mspec, mspec, wspec],
        out_specs=row, out_shape=jax.ShapeDtypeStruct((n, D_MODEL), F32),
        scratch_shapes=[pltpu.VMEM((tm, D_MODEL), F32)],
        compiler_params=_cparams("parallel"), name="xattn_prompt")(
            x, gain.reshape(1, D_MODEL), w_q, mem_k, mem_v, w_o)


def _xattn_step_kernel(q_ref, mk_ref, mv_ref, o_ref, *, tb):
    half = N_MEM // 2
    both = lambda z: jnp.concatenate([z, z], axis=1)
    fold = lambda z, op: op(z[:, :MEM_HEADS], z[:, MEM_HEADS:])
    for i in range(tb):
        k8 = jnp.concatenate([mk_ref[0, i, :half], mk_ref[0, i, half:]], axis=1)
        v8 = jnp.concatenate([mv_ref[0, i, :half], mv_ref[0, i, half:]], axis=1)
        q8 = jnp.concatenate([q_ref[i], q_ref[i]], axis=0)
        s = jnp.sum(k8 * q8[None], axis=-1, keepdims=True) * (MEM_HD ** -0.5)
        smax = both(fold(jnp.max(s, axis=0, keepdims=True), jnp.maximum))
        e = jnp.exp(s - smax)
        den = both(fold(jnp.sum(e, axis=0, keepdims=True), jnp.add))
        o8 = jnp.sum((e / den) * v8, axis=0)
        o_ref[i] = o8[:MEM_HEADS] + o8[MEM_HEADS:]


def xattn_step(q, cache_k, cache_v, layer, *, tb=2):
    n = q.shape[0]
    qspec = pl.BlockSpec((tb, MEM_HEADS, MEM_HD), lambda i: (i, 0, 0))
    cspec = pl.BlockSpec((1, tb, N_MEM, MEM_HEADS, MEM_HD), lambda i: (layer, i, 0, 0, 0))
    o = pl.pallas_call(
        functools.partial(_xattn_step_kernel, tb=tb), grid=(n // tb,),
        in_specs=[qspec, cspec, cspec], out_specs=qspec,
        out_shape=jax.ShapeDtypeStruct((n, MEM_HEADS, MEM_HD), F32),
        compiler_params=_cparams("parallel"), name="xattn_step")(
            q.reshape(n, MEM_HEADS, MEM_HD), cache_k, cache_v)
    return o.reshape(n, D_MODEL)


ROUTER_LANES = 128
NEG_BIG = -1e30


def _moe_gates(logits):
    lane = lax.broadcasted_iota(jnp.int32, logits.shape, 1)
    first = lambda mask: jnp.min(jnp.where(mask, lane, ROUTER_LANES), axis=-1, keepdims=True)
    is_c = lane < MOE_GROUPS
    lc = jnp.where(is_c, logits, NEG_BIG)
    mc = jnp.max(lc, axis=-1, keepdims=True)
    g_idx = first(lc == mc)
    p_g = 1.0 / jnp.sum(jnp.where(is_c, jnp.exp(lc - mc), 0.0), axis=-1, keepdims=True)
    fl = lane - MOE_GROUPS
    in_g = (fl >= 0) & (fl < MOE_EXPERTS) & ((fl // MOE_PER_GROUP) == g_idx)
    lf = jnp.where(in_g, logits, NEG_BIG)
    m1 = jnp.max(lf, axis=-1, keepdims=True)
    i1 = first(lf == m1)
    lf2 = jnp.where(lane == i1, NEG_BIG, lf)
    m2 = jnp.max(lf2, axis=-1, keepdims=True)
    i2 = first(lf2 == m2)
    e2 = jnp.exp(m2 - m1)
    w_top = 1.0 / (1.0 + e2)
    gate = p_g * (jnp.where(lane == i1, w_top, 0.0) + jnp.where(lane == i2, e2 * w_top, 0.0))
    return gate, g_idx


MOE_TM = 512
MOE_GW = MOE_PER_GROUP * MOE_HIDDEN
HG_W = D_MODEL + ROUTER_LANES


def _group_ffn(hb, gate, grp, w1_ref, w3_ref, w2_ref):
    lane = lax.broadcasted_iota(jnp.int32, gate.shape, 1)
    acc = None
    for e in range(MOE_PER_GROUP):
        es = slice(e * MOE_HIDDEN, (e + 1) * MOE_HIDDEN)
        a1 = jnp.dot(hb, w1_ref[0, :, es], preferred_element_type=F32)
        a3 = jnp.dot(hb, w3_ref[0, :, es], preferred_element_type=F32)
        ge = jnp.sum(jnp.where(lane == MOE_GROUPS + grp * MOE_PER_GROUP + e, gate, 0.0), axis=-1, keepdims=True)
        hid = (jax.nn.silu(a1) * a3 * ge).astype(BF16)
        part = jnp.dot(hid, w2_ref[0, es, :], preferred_element_type=F32)
        acc = part if acc is None else acc + part
    return acc


def _moe_kernel(x_ref, g_ref, wr_ref, br_ref, w1_ref, w3_ref, w2_ref, *rest, final_norm):
    if final_norm:
        fin_ref, o_ref, h_scr, gate_scr = rest
    else:
        o_ref, h_scr, gate_scr = rest
    grp = pl.program_id(1)

    @pl.when(grp == 0)
    def _():
        x = x_ref[...]
        h = _rms(x, g_ref[...])
        h_scr[...] = h.astype(BF16)
        logits = jnp.dot(h, wr_ref[...], preferred_element_type=F32,
                         precision=lax.Precision.HIGHEST) + br_ref[...]
        gate_scr[...] = _moe_gates(logits)[0]
        o_ref[...] = x

    o_ref[...] += _group_ffn(h_scr[...], gate_scr[...], grp, w1_ref, w3_ref, w2_ref)

    if final_norm:
        @pl.when(grp == MOE_GROUPS - 1)
        def _():
            o_ref[...] = _rms(o_ref[...], fin_ref[...])


def moe_dense(x, gain, w_r, b_r, w1g, w3g, w2g, layer, *, tm=512, final_gain=None):
    n = x.shape[0]
    tm = min(tm, n)
    gain = gain.reshape(1, D_MODEL)
    row = pl.BlockSpec((tm, D_MODEL), lambda i, g: (i, 0))
    const2 = lambda a: pl.BlockSpec(a.shape, lambda i, g: (0, 0))
    goff = layer * MOE_GROUPS
    wspec = pl.BlockSpec((1, D_MODEL, MOE_GW), lambda i, g: (goff + g, 0, 0))
    args = [x, gain, w_r, b_r, w1g, w3g, w2g]
    in_specs = [row, const2(gain), const2(w_r), const2(b_r), wspec, wspec,
                pl.BlockSpec((1, MOE_GW, D_MODEL), lambda i, g: (goff + g, 0, 0))]
    if final_gain is not None:
        args.append(final_gain.reshape(1, D_MODEL))
        in_specs.append(const2(args[-1]))
    return pl.pallas_call(
        functools.partial(_moe_kernel, final_norm=final_gain is not None), grid=(n // tm, MOE_GROUPS),
        in_specs=in_specs, out_specs=row, out_shape=jax.ShapeDtypeStruct((n, D_MODEL), F32),
        scratch_shapes=[pltpu.VMEM((tm, D_MODEL), BF16), pltpu.VMEM((tm, ROUTER_LANES), F32)],
        compiler_params=_cparams("parallel", "arbitrary"), name="moe")(*args)


def _moe_route_kernel(x_ref, g_ref, wr_ref, br_ref, hg_ref, gid_ref):
    h = _rms(x_ref[...], g_ref[...])
    logits = jnp.dot(h, wr_ref[...], preferred_element_type=F32, precision=lax.Precision.HIGHEST) + br_ref[...]
    gate, g_idx = _moe_gates(logits)
    hg_ref[:, :D_MODEL] = h
    hg_ref[:, D_MODEL:] = gate
    gid_ref[...] = jnp.broadcast_to(g_idx, gid_ref.shape)


def _gather_rows_kernel(idx_ref, src_ref, *rest, rows, add):
    if add:
        add_ref, out_ref, buf, sem = rest
    else:
        out_ref, buf, sem = rest

    def start(r, carry):
        pltpu.make_async_copy(src_ref.at[pl.ds(idx_ref[0, 0, r], 1)], buf.at[pl.ds(r, 1)], sem).start()
        return carry

    lax.fori_loop(0, rows, start, 0, unroll=8)
    pltpu.make_async_copy(src_ref.at[pl.ds(0, rows)], buf, sem).wait()
    if add:
        out_ref[...] = add_ref[...] + buf[...]
    else:
        out_ref[...] = buf[...]


def gather_rows(src, idx, *, add=None, rows=MOE_TM):
    n_out = idx.shape[0]
    width = src.shape[1]
    nt = n_out // rows
    row = pl.BlockSpec((rows, width), lambda t: (t, 0))
    in_specs = [pl.BlockSpec((1, 1, rows), lambda t: (t, 0, 0), memory_space=pltpu.SMEM),
                pl.BlockSpec(memory_space=pl.ANY)]
    args = [idx.reshape(nt, 1, rows), src]
    if add is not None:
        in_specs.append(row)
        args.append(add)
    return pl.pallas_call(
        functools.partial(_gather_rows_kernel, rows=rows, add=add is not None), grid=(nt,),
        in_specs=in_specs, out_specs=row, out_shape=jax.ShapeDtypeStruct((n_out, width), F32),
        scratch_shapes=[pltpu.VMEM((rows, width), F32), pltpu.SemaphoreType.DMA(())],
        compiler_params=_cparams("arbitrary"), name="gather_rows")(*args)


def _moe_group_kernel(tg_ref, tv_ref, hg_ref, w1_ref, w3_ref, w2_ref, o_ref):
    t = pl.program_id(0)

    @pl.when(tv_ref[t] != 0)
    def _():
        o_ref[...] = _group_ffn(hg_ref[:, :D_MODEL].astype(BF16), hg_ref[:, D_MODEL:], tg_ref[t],
                                w1_ref, w3_ref, w2_ref)

    @pl.when(tv_ref[t] == 0)
    def _():
        o_ref[...] = jnp.zeros_like(o_ref)


def _moe_sort_plan(gid, n, tm):
    n_tiles = n // tm + MOE_GROUPS
    onehot = (gid[:, None] == jnp.arange(MOE_GROUPS, dtype=jnp.int32)[None, :]).astype(jnp.int32)
    csum = jnp.cumsum(onehot, axis=0)
    rank = jnp.take_along_axis(csum, gid[:, None], axis=1)[:, 0] - 1
    counts = csum[-1]
    tiles_per_g = (counts + tm - 1) // tm
    tile_end = jnp.cumsum(tiles_per_g)
    tile_off = tile_end - tiles_per_g
    dest = tile_off[gid] * tm + rank
    row_src = jnp.zeros((n_tiles * tm,), jnp.int32).at[dest].set(jnp.arange(n, dtype=jnp.int32))
    tix = jnp.arange(n_tiles, dtype=jnp.int32)
    tile_group = jnp.minimum(jnp.sum((tix[:, None] >= tile_end[None, :]).astype(jnp.int32), axis=1),
                             MOE_GROUPS - 1)
    tile_valid = (tix < tile_end[-1]).astype(jnp.int32)
    return dest, row_src, tile_group, tile_valid


def moe_sorted(x, gain, w_r, b_r, w1g, w3g, w2g, layer, *, tm=MOE_TM):
    n = x.shape[0]
    goff = layer * MOE_GROUPS
    row = lambda w: pl.BlockSpec((tm, w), lambda i: (i, 0))
    const = lambda a: pl.BlockSpec(a.shape, lambda i: (0, 0))
    gain = gain.reshape(1, D_MODEL)
    hg, gid = pl.pallas_call(
        _moe_route_kernel, grid=(n // tm,), in_specs=[row(D_MODEL), const(gain), const(w_r), const(b_r)],
        out_specs=(row(HG_W), row(ROUTER_LANES)),
        out_shape=(jax.ShapeDtypeStruct((n, HG_W), F32), jax.ShapeDtypeStruct((n, ROUTER_LANES), jnp.int32)),
        compiler_params=_cparams("parallel"), name="moe_route")(x, gain, w_r, b_r)
    dest, row_src, tile_group, tile_valid = _moe_sort_plan(gid[:, 0], n, tm)
    hg_sorted = gather_rows(hg, row_src, rows=tm)
    n_tiles = row_src.shape[0] // tm
    wspec = pl.BlockSpec((1, D_MODEL, MOE_GW), lambda t, tg, tv: (goff + tg[t], 0, 0))
    grid_spec = pltpu.PrefetchScalarGridSpec(
        num_scalar_prefetch=2, grid=(n_tiles,),
        in_specs=[pl.BlockSpec((tm, HG_W), lambda t, tg, tv: (t, 0)), wspec, wspec,
                  pl.BlockSpec((1, MOE_GW, D_MODEL), lambda t, tg, tv: (goff + tg[t], 0, 0))],
        out_specs=pl.BlockSpec((tm, D_MODEL), lambda t, tg, tv: (t, 0)))
    out_sorted = pl.pallas_call(
        _moe_group_kernel, grid_spec=grid_spec,
        out_shape=jax.ShapeDtypeStruct((n_tiles * tm, D_MODEL), F32),
        compiler_params=_cparams("arbitrary"), name="moe_group")(tile_group, tile_valid, hg_sorted, w1g, w3g, w2g)
    return gather_rows(out_sorted, dest, add=x, rows=tm)


def _group_weights(w1, w3, w2):
    ng = w1.shape[0] * MOE_GROUPS
    side = lambda w: jnp.transpose(w.astype(BF16).reshape(ng, MOE_PER_GROUP, D_MODEL, MOE_HIDDEN),
                                   (0, 2, 1, 3)).reshape(ng, D_MODEL, MOE_GW)
    return side(w1), side(w3), w2.astype(BF16).reshape(ng, MOE_GW, D_MODEL)


def _router_params(w_rc, b_rc, w_rf, b_rf):
    pad = ROUTER_LANES - MOE_GROUPS - MOE_EXPERTS
    w_r = jnp.concatenate([w_rc, w_rf, jnp.zeros((D_MODEL, pad), F32)], axis=1).astype(F32)
    b_r = jnp.concatenate([b_rc, b_rf, jnp.zeros((pad,), F32)]).reshape(1, ROUTER_LANES).astype(F32)
    return w_r, b_r


def _rmsnorm_kernel(x_ref, g_ref, o_ref):
    o_ref[...] = _rms(x_ref[...], g_ref[...])


def rmsnorm_rows(x, gain, *, tm=1024):
    n = x.shape[0]
    tm = min(tm, n)
    row = pl.BlockSpec((tm, D_MODEL), lambda i: (i, 0))
    return pl.pallas_call(
        _rmsnorm_kernel, grid=(n // tm,), in_specs=[row, pl.BlockSpec((1, D_MODEL), lambda i: (0, 0))],
        out_specs=row, out_shape=jax.ShapeDtypeStruct((n, D_MODEL), F32),
        compiler_params=_cparams("parallel"), name="rmsnorm")(x, gain.reshape(1, D_MODEL))


def _stack(parts):
    return parts[0][None] if len(parts) == 1 else jnp.stack(parts)


def _trunk(x, nb, w, s5_re, s5_im, rw_state, rw_shift, ret_state, mem_k, mem_v, pos0):
    n = x.shape[0]
    t_len = n // nb
    single = t_len == 1
    out_s5_re, out_s5_im, out_rw, out_shift, out_ret = [], [], [], [], []
    for layer in range(DEPTH):
        if layer % 2 == 0:
            i = layer // 2
            s5p = w['s5p'][i]
            rwp = tuple(w[k][i] for k in ('rw_mu', 'rw_w0', 'rw_w2', 'rw_a0', 'rw_a2', 'rw_g2',
                                          'rw_k_k', 'rw_k_a', 'rw_r_k', 'rw_ln_w', 'rw_ln_b'))
            w_in = w['w_in0_bf'][i]
            w_out = w['w_out0_bf'][i]
            if single:
                u, p = linear(x, w_in, gain=w['norm_mix'][layer], splits=(S5_WIDTH, RW_PROJ))
                y_s5, sr, si = s5_mixer(u.reshape(1, nb, S5_WIDTH), s5_re[i], s5_im[i], s5p,
                                        w['s5_d'][i], w['s5_w_glu'][i], tc=1)
                y_rw, srw = rwkv_step(p, rw_shift[i], rw_state[i], rwp)
                sh = p
                x = linear(y_s5.reshape(nb, S5_WIDTH), w_out[:S5_WIDTH], x2=y_rw, w2=w_out[S5_WIDTH:], residual=x)
            else:
                u, p = linear(x, w_in, gain=w['norm_mix'][layer], splits=(S5_WIDTH, RW_PROJ),
                              out_tmajor=True, batch=nb)
                y_s5, sr, si = s5_mixer(u, s5_re[i], s5_im[i], s5p, w['s5_d'][i], w['s5_w_glu'][i], tc=128)
                y_rw, srw, sh = rwkv_prompt(p, rw_shift[i], rw_state[i], rwp)
                x = linear(y_s5, w_out[:S5_WIDTH], x2=y_rw, w2=w_out[S5_WIDTH:], residual=x, x_tmajor=True)
            out_s5_re.append(sr.reshape(nb, S5_GROUPS, S5_N))
            out_s5_im.append(si.reshape(nb, S5_GROUPS, S5_N))
            out_rw.append(srw)
            out_shift.append(sh)
        else:
            j = layer // 2
            q, k, v, g = linear(x, w['w_in1_bf'][j], gain=w['norm_mix'][layer], out_dtype=BF16,
                                splits=(NQ, NQ, NV, NV), tm=256)
            if single:
                y, s_new = retention_step(q, k, v, g, ret_state[j], pos0=pos0)
            else:
                y, s_new = retention_prompt(q, k, v, g, nb=nb)
            x = linear(y, w['w_out1_bf'][j], residual=x)
            out_ret.append(s_new)
        w_q = w['w_mq_bf'][layer]
        w_o = w['w_mo_bf'][layer]
        if single:
            q = linear(x, w_q, gain=w['norm_mem'][layer])
            att = xattn_step(q, mem_k, mem_v, layer)
            x = linear(att, w_o, residual=x)
        else:
            x = xattn_prompt(x, w['norm_mem'][layer], w_q, mem_k[layer], mem_v[layer], w_o, nb=nb)
        w_r, b_r = w['router'][layer]
        x = moe_dense(x, w['norm_ffn'][layer], w_r, b_r, *w['moe_g'], layer, tm=1024,
                      final_gain=w['norm_final'] if layer == DEPTH - 1 else None)
    y = x
    return (y, _stack(out_s5_re), _stack(out_s5_im), _stack(out_rw), _stack(out_shift), _stack(out_ret))


def kernel(x_prompt, x_sample, mem_prompt, state_s5_re, state_s5_im, state_rwkv, state_shift, state_ret, cache_mem_k, cache_mem_v, norm_mix, norm_mem, norm_ffn, norm_final, w_in0, w_out0, s5_a_re, s5_a_im, s5_b_re, s5_b_im, s5_c_re, s5_c_im, s5_d, s5_log_dt, s5_w_glu, rw_mu, rw_w0, rw_w2, rw_a0, rw_a2, rw_g2, rw_k_k, rw_k_a, rw_r_k, rw_ln_w, rw_ln_b, w_in1, w_out1, mem_norm, w_mq, w_mk, w_mv, w_mo, moe_w_rc, moe_b_rc, moe_w_rf, moe_b_rf, moe_w1, moe_w3, moe_w2):
    w = dict(norm_mix=norm_mix, norm_mem=norm_mem, norm_ffn=norm_ffn, norm_final=norm_final,
             w_in0=w_in0, w_out0=w_out0, s5_a_re=s5_a_re, s5_a_im=s5_a_im, s5_b_re=s5_b_re, s5_b_im=s5_b_im,
             s5_c_re=s5_c_re, s5_c_im=s5_c_im, s5_d=s5_d, s5_log_dt=s5_log_dt, s5_w_glu=s5_w_glu,
             rw_mu=rw_mu, rw_w0=rw_w0, rw_w2=rw_w2, rw_a0=rw_a0, rw_a2=rw_a2, rw_g2=rw_g2,
             rw_k_k=rw_k_k, rw_k_a=rw_k_a, rw_r_k=rw_r_k, rw_ln_w=rw_ln_w, rw_ln_b=rw_ln_b,
             w_in1=w_in1, w_out1=w_out1, w_mq=w_mq, w_mo=w_mo,
             moe_w_rc=moe_w_rc, moe_b_rc=moe_b_rc, moe_w_rf=moe_w_rf, moe_b_rf=moe_b_rf,
             moe_w1=moe_w1, moe_w3=moe_w3, moe_w2=moe_w2)
    nbp, t_len, _ = x_prompt.shape
    nbs = x_sample.shape[0]
    n_even, n_odd = state_s5_re.shape[0], state_ret.shape[0]
    for name in ('w_in0', 'w_out0', 'w_in1', 'w_out1', 'w_mq', 'w_mo'):
        w[name + '_bf'] = w[name].astype(BF16)
    w['s5p'] = [_s5_params(s5_a_re[i], s5_a_im[i], s5_b_re[i], s5_b_im[i], s5_c_re[i], s5_c_im[i], s5_log_dt[i])
                for i in range(n_even)]
    w['router'] = [_router_params(moe_w_rc[l], moe_b_rc[l], moe_w_rf[l], moe_b_rf[l]) for l in range(DEPTH)]
    w['moe_g'] = _group_weights(moe_w1, moe_w3, moe_w2)

    mem = mem_prompt.reshape(nbp * N_MEM, D_MODEL)
    mem_k_l, mem_v_l = [], []
    for layer in range(DEPTH):
        w_kv = jnp.concatenate([w_mk[layer], w_mv[layer]], axis=1).astype(BF16)
        mk, mv = linear(mem, w_kv, gain=mem_norm[layer], splits=(D_MODEL, D_MODEL))
        mem_k_l.append(mk.reshape(nbp, N_MEM, D_MODEL))
        mem_v_l.append(mv.reshape(nbp, N_MEM, D_MODEL))
    kv_shape = (DEPTH, nbp, N_MEM, MEM_HEADS, MEM_HD)
    mem_k_p = jnp.stack(mem_k_l).reshape(kv_shape)
    mem_v_p = jnp.stack(mem_v_l).reshape(kv_shape)

    zeros = lambda *shape: jnp.zeros(shape, F32)
    y_p, s5r_p, s5i_p, rw_p, sh_p, ret_p = _trunk(
        x_prompt.reshape(nbp * t_len, D_MODEL), nbp, w,
        zeros(n_even, nbp, S5_STATE), zeros(n_even, nbp, S5_STATE),
        zeros(n_even, nbp, RW_HEADS, RW_HD, RW_HD), zeros(n_even, nbp, RW_PROJ),
        None, mem_k_l, mem_v_l, 0.0)
    y_s, s5r_s, s5i_s, rw_s, sh_s, ret_s = _trunk(
        x_sample.reshape(nbs, D_MODEL), nbs, w,
        state_s5_re.reshape(n_even, nbs, S5_STATE), state_s5_im.reshape(n_even, nbs, S5_STATE),
        state_rwkv, state_shift, state_ret,
        cache_mem_k, cache_mem_v, float(PAST_LEN))
    return (y_p.reshape(nbp, t_len, D_MODEL), y_s.reshape(nbs, 1, D_MODEL),
            s5r_p, s5i_p, rw_p, sh_p, ret_p, mem_k_p, mem_v_p, s5r_s, s5i_s, rw_s, sh_s, ret_s)
```

```python
import functools
import math

import jax
import jax.numpy as jnp
from jax import lax
from jax.experimental import pallas as pl
from jax.experimental.pallas import tpu as pltpu

F32 = jnp.float32
BF16 = jnp.bfloat16

D_MODEL = 1024
DEPTH = 2
PAST_LEN = 16384
S5_WIDTH = 512
S5_GROUP = 16
S5_GROUPS = 32
S5_N = 64
S5_STATE = S5_GROUPS * S5_N
S5_GBLK = 8
RW_WIDTH = 512
RW_HD = 64
RW_HEADS = 8
RW_LORA = 256
RW_PROJ = 3 * RW_WIDTH + RW_LORA
IN0 = S5_WIDTH + RW_PROJ
RET_DK = 256
RET_HEADS = 4
RET_DV = 512
RET_CHUNK = 256
NQ = RET_HEADS * RET_DK
NV = RET_HEADS * RET_DV
IN1 = 2 * NQ + 2 * NV
N_MEM = 256
MEM_HEADS = 4
MEM_HD = 256
MOE_GROUPS = 4
MOE_PER_GROUP = 4
MOE_EXPERTS = 16
MOE_HIDDEN = 256
NORM_EPS = 1e-6
RW_GN_EPS = 64e-5
ROPE_BASE = 10000.0

VMEM_LIMIT = 56 * 1024 * 1024


def _cparams(*sem):
    return pltpu.CompilerParams(dimension_semantics=sem, vmem_limit_bytes=VMEM_LIMIT)


def _bdot(a, b):
    return jnp.dot(a.astype(BF16), b.astype(BF16), preferred_element_type=F32)


def _dot_nt(a, b):
    return lax.dot_general(a.astype(BF16), b.astype(BF16), (((1,), (1,)), ((), ())),
                           preferred_element_type=F32)


def _dot_tn(a, b):
    return lax.dot_general(a.astype(BF16), b.astype(BF16), (((0,), (0,)), ((), ())),
                           preferred_element_type=F32)


def _split3(x):
    hi = x.astype(BF16)
    r1 = x - hi.astype(F32)
    mid = r1.astype(BF16)
    lo = (r1 - mid.astype(F32)).astype(BF16)
    return hi, mid, lo


def _dot_exact_rhs(x, m_bf16, passes=3):
    hi, mid, lo = _split3(x)
    acc = jnp.dot(hi, m_bf16, preferred_element_type=F32)
    if passes > 1:
        acc = acc + jnp.dot(mid, m_bf16, preferred_element_type=F32)
    if passes > 2:
        acc = acc + jnp.dot(lo, m_bf16, preferred_element_type=F32)
    return acc


def _rms(x, g):
    ms = jnp.mean(x * x, axis=-1, keepdims=True)
    return x * lax.rsqrt(ms + NORM_EPS) * g


def _linear_kernel(*refs, norm, two, res, n_dup):
    it = iter(refs)
    x_ref = next(it)
    g_ref = next(it) if norm else None
    w_ref = next(it)
    x2_ref = next(it) if two else None
    w2_ref = next(it) if two else None
    r_ref = next(it) if res else None
    o_refs = list(it)
    dup_refs = o_refs[len(o_refs) - n_dup:] if n_dup else []
    o_refs = o_refs[:len(o_refs) - n_dup]
    x = x_ref[...].astype(F32)
    if norm:
        x = _rms(x, g_ref[...])
    xb = x.astype(BF16)
    x2b = x2_ref[...].astype(BF16) if two else None
    col = 0
    for o_ref in o_refs:
        m = o_ref.shape[-1]
        step = next((s for s in (512, 256) if m % s == 0), m)
        for j in range(m // step):
            sl = slice(col + j * step, col + (j + 1) * step)
            acc = jnp.dot(xb, w_ref[:, sl], preferred_element_type=F32)
            if two:
                acc = acc + jnp.dot(x2b, w2_ref[:, sl], preferred_element_type=F32)
            if res:
                acc = acc + r_ref[:, sl]
            o_ref[:, j * step:(j + 1) * step] = acc.astype(o_ref.dtype)
        col += m
    for o_ref, d_ref in zip(o_refs, dup_refs):
        hd = d_ref.shape[-1]
        for h in range(d_ref.shape[1]):
            d_ref[:, h, :] = o_ref[:, h * hd:(h + 1) * hd].astype(d_ref.dtype)


def _row_spec(tm, width, tmajor_b):
    if tmajor_b is None:
        return pl.BlockSpec((tm, width), lambda i: (i, 0))
    nb, tiles_per_b = tmajor_b
    return pl.BlockSpec((tm, width), lambda i: (i % tiles_per_b, i // tiles_per_b))


def linear(x, w, *, gain=None, x2=None, w2=None, residual=None, out_dtype=F32, tm=512,
           x_tmajor=False, out_tmajor=False, batch=None, splits=None, head_copies=None, name="linear"):
    if x_tmajor:
        t_len, nb, k = x.shape
        n = t_len * nb
    else:
        n, k = x.shape
        nb = batch
        t_len = n // nb if nb else None
    m = w.shape[1]
    tm = min(tm, n if not (x_tmajor or out_tmajor) else t_len)
    assert n % tm == 0
    tiles_per_b = (t_len // tm) if (x_tmajor or out_tmajor) else None
    args, specs = [], []

    def add_rows(a, tmajor):
        width = a.shape[-1]
        args.append(a.reshape(t_len, nb * width) if tmajor else a)
        specs.append(_row_spec(tm, width, (nb, tiles_per_b) if tmajor else None))

    add_rows(x, x_tmajor)
    if gain is not None:
        args.append(gain.reshape(1, k).astype(F32))
        specs.append(pl.BlockSpec((1, k), lambda i: (0, 0)))
    args.append(w)
    specs.append(pl.BlockSpec(w.shape, lambda i: (0, 0)))
    if x2 is not None:
        add_rows(x2, x_tmajor)
        args.append(w2)
        specs.append(pl.BlockSpec(w2.shape, lambda i: (0, 0)))
    if residual is not None:
        add_rows(residual, False)
    widths = tuple(splits) if splits else (m,)
    assert sum(widths) == m
    if out_tmajor:
        out_shape = [jax.ShapeDtypeStruct((t_len, nb * mw), out_dtype) for mw in widths]
    else:
        out_shape = [jax.ShapeDtypeStruct((n, mw), out_dtype) for mw in widths]
    out_specs = [_row_spec(tm, mw, (nb, tiles_per_b) if out_tmajor else None) for mw in widths]
    n_dup = 0
    if head_copies is not None:
        heads, hd = head_copies
        assert not out_tmajor and all(mw == heads * hd for mw in widths)
        n_dup = len(widths)
        out_shape += [jax.ShapeDtypeStruct((n, heads, hd), out_dtype)] * n_dup
        out_specs += [pl.BlockSpec((tm, heads, hd), lambda i: (i, 0, 0))] * n_dup
    kern = functools.partial(_linear_kernel, norm=gain is not None, two=x2 is not None,
                             res=residual is not None, n_dup=n_dup)
    outs = pl.pallas_call(
        kern, grid=(n // tm,), in_specs=specs, out_specs=out_specs, out_shape=out_shape,
        compiler_params=_cparams("parallel"), name=name)(*args)
    if out_tmajor:
        outs = [o.reshape(t_len, nb, mw) for o, mw in zip(outs, widths)]
    return outs if (splits or n_dup) else outs[0]


def _s5_kernel(u_ref, h_re_ref, h_im_ref, abar_re_ref, abar_im_ref, bb_re_ref, bb_im_ref,
               cc_re_ref, cc_im_ref, d_ref, wglu_ref, y_ref, s_re_ref, s_im_ref,
               x_re, x_im, st_re, st_im, il_scr, *, tc, nb, flat):
    c = pl.program_id(0)
    nlb = S5_WIDTH // 128
    rows = tc * nb
    nblk = S5_GROUPS // S5_GBLK
    bw_in = S5_GBLK * S5_GROUP
    bw_st = S5_GBLK * S5_N

    @pl.when(c == 0)
    def _():
        st_re[...] = h_re_ref[...]
        st_im[...] = h_im_ref[...]

    if flat:
        for b in range(nb):
            for j in range(nlb):
                il_scr[j, pl.ds(b, tc, stride=nb), :] = u_ref[:, b * S5_WIDTH + j * 128:b * S5_WIDTH + (j + 1) * 128]
        u = jnp.concatenate([il_scr[j] for j in range(nlb)], axis=-1)
    else:
        u = u_ref[...].reshape(rows, S5_WIDTH)
    ub = u.astype(BF16)
    for gb in range(nblk):
        ui = ub[:, gb * bw_in:(gb + 1) * bw_in]
        x_re[:, gb * bw_st:(gb + 1) * bw_st] = jnp.dot(ui, bb_re_ref[gb], preferred_element_type=F32)
        x_im[:, gb * bw_st:(gb + 1) * bw_st] = jnp.dot(ui, bb_im_ref[gb], preferred_element_type=F32)

    lane_blk = 1024
    for lb in range(S5_STATE // lane_blk):
        sl = slice(lb * lane_blk, (lb + 1) * lane_blk)
        ar = jnp.broadcast_to(abar_re_ref[:, sl], (nb, lane_blk))
        ai = jnp.broadcast_to(abar_im_ref[:, sl], (nb, lane_blk))

        def body(t, carry, sl=sl, ar=ar, ai=ai):
            xr, xi = carry
            r0 = pl.multiple_of(t * nb, nb)
            br = x_re[pl.ds(r0, nb), sl]
            bi = x_im[pl.ds(r0, nb), sl]
            nr = ar * xr - ai * xi + br
            ni = ar * xi + ai * xr + bi
            x_re[pl.ds(r0, nb), sl] = nr
            x_im[pl.ds(r0, nb), sl] = ni
            return nr, ni

        fr, fi = lax.fori_loop(0, tc, body, (st_re[:, sl], st_im[:, sl]), unroll=min(tc, 4))
        st_re[:, sl] = fr
        st_im[:, sl] = fi

    for gb in range(nblk):
        xr = x_re[:, gb * bw_st:(gb + 1) * bw_st].astype(BF16)
        xi = x_im[:, gb * bw_st:(gb + 1) * bw_st].astype(BF16)
        yb = (jnp.dot(xr, cc_re_ref[gb], preferred_element_type=F32)
              - jnp.dot(xi, cc_im_ref[gb], preferred_element_type=F32))
        cs = slice(gb * bw_in, (gb + 1) * bw_in)
        yb = yb + d_ref[:, cs] * u[:, cs]
        x_re[:, cs] = jax.nn.gelu(yb)
    y = x_re[:, :S5_WIDTH]
    y = y * jax.nn.sigmoid(jnp.dot(y.astype(BF16), wglu_ref[...], preferred_element_type=F32))
    if flat:
        for j in range(nlb):
            il_scr[j] = y[:, j * 128:(j + 1) * 128]
        for b in range(nb):
            for j in range(nlb):
                y_ref[:, b * S5_WIDTH + j * 128:b * S5_WIDTH + (j + 1) * 128] = (
                    il_scr[j, pl.ds(b, tc, stride=nb), :].astype(y_ref.dtype))
    else:
        y_ref[...] = y.reshape(y_ref.shape).astype(y_ref.dtype)

    @pl.when(c == pl.num_programs(0) - 1)
    def _():
        s_re_ref[...] = st_re[...]
        s_im_ref[...] = st_im[...]


def _s5_params(a_re, a_im, b_re, b_im, c_re, c_im, log_dt):
    dt = jnp.exp(log_dt.astype(F32))[:, None]
    ar, ai = a_re.astype(F32), a_im.astype(F32)
    mag = jnp.exp(dt * ar)
    abar_re, abar_im = mag * jnp.cos(dt * ai), mag * jnp.sin(dt * ai)
    den = ar * ar + ai * ai
    nr = abar_re - 1.0
    coef_re = (nr * ar + abar_im * ai) / den
    coef_im = (abar_im * ar - nr * ai) / den
    cr, ci = coef_re[..., None], coef_im[..., None]
    brf, bif = b_re.astype(F32), b_im.astype(F32)
    bb_re = cr * brf - ci * bif
    bb_im = cr * bif + ci * brf
    nblk = S5_GROUPS // S5_GBLK
    eye = jnp.eye(S5_GBLK, dtype=F32)

    def blockdiag_in(bb):
        t = jnp.transpose(bb, (0, 2, 1)).reshape(nblk, S5_GBLK, S5_GROUP, S5_N)
        m = jnp.einsum('kgcn,gh->kgchn', t, eye)
        return m.reshape(nblk, S5_GBLK * S5_GROUP, S5_GBLK * S5_N).astype(BF16)

    def blockdiag_out(cc):
        t = jnp.transpose(cc.astype(F32), (0, 2, 1)).reshape(nblk, S5_GBLK, S5_N, S5_GROUP)
        m = jnp.einsum('khnc,hg->khngc', t, eye)
        return m.reshape(nblk, S5_GBLK * S5_N, S5_GBLK * S5_GROUP).astype(BF16)

    return (abar_re.reshape(1, S5_STATE), abar_im.reshape(1, S5_STATE),
            blockdiag_in(bb_re), blockdiag_in(bb_im), blockdiag_out(c_re), blockdiag_out(c_im))


def s5_mixer(u_tm, h_re, h_im, params, d_skip, w_glu, *, tc):
    t_len, nb, _ = u_tm.shape
    abar_re, abar_im, bb_re, bb_im, cc_re, cc_im = params
    tc = min(tc, t_len)
    assert t_len % tc == 0 and nb % 8 == 0
    rows = tc * nb
    flat = t_len > 1
    full = lambda a: pl.BlockSpec(a.shape, lambda c: (0,) * a.ndim)
    if flat:
        u_arg = u_tm.reshape(t_len, nb * S5_WIDTH)
        io_spec = pl.BlockSpec((tc, nb * S5_WIDTH), lambda c: (c, 0))
        y_shape = jax.ShapeDtypeStruct((t_len, nb * S5_WIDTH), BF16)
    else:
        u_arg = u_tm
        io_spec = pl.BlockSpec((tc, nb, S5_WIDTH), lambda c: (c, 0, 0))
        y_shape = jax.ShapeDtypeStruct((t_len, nb, S5_WIDTH), BF16)
    args = (u_arg, h_re, h_im, abar_re, abar_im, bb_re, bb_im, cc_re, cc_im,
            d_skip.reshape(1, S5_WIDTH).astype(F32), w_glu.astype(BF16))
    in_specs = [io_spec] + [full(a) for a in args[1:]]
    st_shape = jax.ShapeDtypeStruct((nb, S5_STATE), F32)
    st_spec = pl.BlockSpec((nb, S5_STATE), lambda c: (0, 0))
    scratch = [pltpu.VMEM((rows, S5_STATE), F32), pltpu.VMEM((rows, S5_STATE), F32),
               pltpu.VMEM((nb, S5_STATE), F32), pltpu.VMEM((nb, S5_STATE), F32),
               pltpu.VMEM((S5_WIDTH // 128, rows if flat else 8, 128), F32)]
    y, s_re, s_im = pl.pallas_call(
        functools.partial(_s5_kernel, tc=tc, nb=nb, flat=flat), grid=(t_len // tc,), in_specs=in_specs,
        out_specs=(io_spec, st_spec, st_spec), out_shape=(y_shape, st_shape, st_shape),
        scratch_shapes=scratch, compiler_params=_cparams("arbitrary"), name="s5_mixer")(*args)
    return y.reshape(t_len, nb, S5_WIDTH), s_re, s_im


def _head_ones():
    i = lax.broadcasted_iota(jnp.int32, (RW_WIDTH, RW_WIDTH), 0) // RW_HD
    j = lax.broadcasted_iota(jnp.int32, (RW_WIDTH, RW_WIDTH), 1) // RW_HD
    return jnp.where(i == j, 1.0, 0.0).astype(BF16)


def _softplus(z):
    return jnp.maximum(z, 0.0) + jnp.log1p(jnp.exp(-jnp.abs(z)))


def _rw_prep(p, p_prev, prm, ones_bd):
    mu, w0, w2, a0, a2, g2, k_k, k_a = prm
    xm = p + (p_prev - p) * mu
    o1, o2, o3 = RW_WIDTH, 2 * RW_WIDTH, 3 * RW_WIDTH
    r, k, v = xm[:, :o1], xm[:, o1:o2], xm[:, o2:o3]
    wd, ad, gd = xm[:, o3:o3 + 64], xm[:, o3 + 64:o3 + 128], xm[:, o3 + 128:]
    w = -_softplus(-(w0 + _bdot(jnp.tanh(wd), w2))) - 0.5
    lw = -jnp.exp(w)
    a = jax.nn.sigmoid(a0 + _bdot(ad, a2))
    g = _bdot(jax.nn.sigmoid(gd), g2)
    kk = k * k_k
    ss = _dot_exact_rhs(kk * kk, ones_bd, passes=1)
    kk = kk / jnp.maximum(jnp.sqrt(ss), 1e-12)
    k = k * (1.0 + (a - 1.0) * k_a)
    return r, lw, k, v, -kk, kk * a, g


def _rw_post(o, r, k, v, g, r_k, ln_w, ln_b, ones_bd):
    inv = 1.0 / RW_HD
    mean = _dot_exact_rhs(o, ones_bd, passes=2) * inv
    d = o - mean
    var = _dot_exact_rhs(d * d, ones_bd, passes=1) * inv
    on = d * lax.rsqrt(var + RW_GN_EPS) * ln_w + ln_b
    bonus = _dot_exact_rhs(r * k * r_k, ones_bd, passes=1) * v
    return (on + bonus) * g


def _rw_chunk_kernel(p_ref, shift_ref, h0_ref, mu_ref, w0_ref, w2_ref, a0_ref, a2_ref, g2_ref,
                     kk_ref, ka_ref, rk_ref, lnw_ref, lnb_ref,
                     y_ref, hfin_ref, shout_ref, prev_scr, h_scr, o_scr, *, c_len, bs):
    c = pl.program_id(1)
    nc = pl.num_programs(1)
    cl = c_len

    @pl.when(c == 0)
    def _():
        prev_scr[...] = shift_ref[:, 0, :]
        h_scr[...] = h0_ref[...]

    ones_bd = _head_ones()
    row = lax.broadcasted_iota(jnp.int32, (cl, RW_PROJ), 0)
    ps, pprevs = [], []
    for bi in range(bs):
        p = p_ref[:, bi * RW_PROJ:(bi + 1) * RW_PROJ]
        pprevs.append(jnp.where(row == 0, prev_scr[bi:bi + 1, :], pltpu.roll(p, 1, 0)))
        ps.append(p)
    p_all = jnp.concatenate(ps, axis=0) if bs > 1 else ps[0]
    pprev_all = jnp.concatenate(pprevs, axis=0) if bs > 1 else pprevs[0]
    prm = (mu_ref[...], w0_ref[...], w2_ref[...], a0_ref[...], a2_ref[...], g2_ref[...],
           kk_ref[...], ka_ref[...])
    r, lw, k, v, a, b, g = _rw_prep(p_all, pprev_all, prm, ones_bd)

    ti = lax.broadcasted_iota(jnp.int32, (cl, cl), 0)
    si = lax.broadcasted_iota(jnp.int32, (cl, cl), 1)
    lmat = jnp.where(ti >= si, 1.0, 0.0).astype(BF16)
    eye = jnp.where(ti == si, 1.0, 0.0)
    mi = lax.broadcasted_iota(jnp.int32, (2 * cl, 3 * cl), 0)
    mj = lax.broadcasted_iota(jnp.int32, (2 * cl, 3 * cl), 1)
    t_row = jnp.where(mi >= cl, mi - cl, mi)
    s_col = jnp.where(mj < cl, mj, jnp.where(mj >= 2 * cl, mj - 2 * cl, -4 * cl))
    keep = (t_row - s_col) >= jnp.where(mi >= cl, 0, 1)
    eye_bf = eye.astype(BF16)

    lhs_l, rhs_l, vh_l, hcat_l, kb_l, etot_l = [], [], [], [], [], []
    for bi in range(bs):
        rs = slice(bi * cl, (bi + 1) * cl)
        lw_b = lw[rs]
        l_hi, l_mid, l_lo = _split3(lw_b)
        cum = (jnp.dot(lmat, l_hi, preferred_element_type=F32)
               + jnp.dot(lmat, l_mid, preferred_element_type=F32)
               + jnp.dot(lmat, l_lo, preferred_element_type=F32))
        tot = cum[cl - 1:cl, :]
        e_neg = jnp.exp(-cum)
        e_rem = jnp.exp(tot - cum)
        at = (a[rs] * jnp.exp(cum - lw_b)).astype(BF16)
        rt = (r[rs] * jnp.exp(cum)).astype(BF16)
        bt = (b[rs] * e_neg).astype(BF16)
        kt = (k[rs] * e_neg).astype(BF16)
        bh = (b[rs] * e_rem).astype(BF16)
        kh = (k[rs] * e_rem).astype(BF16)
        e_tot = jnp.exp(tot)
        vb = v[rs].astype(BF16)
        for h in range(RW_HEADS):
            hs = slice(h * RW_HD, (h + 1) * RW_HD)
            lhs_l.append(jnp.concatenate([at[:, hs], rt[:, hs]], axis=0))
            rhs_l.append(jnp.concatenate([kt[:, hs], eye_bf, bt[:, hs]], axis=0))
            vh_l.append(vb[:, hs])
            kb_l.append(jnp.concatenate([kh[:, hs], bh[:, hs]], axis=0))
            etot_l.append(jnp.sum(eye * e_tot[:, hs], axis=-1, keepdims=True))
            hcat_l.append(h_scr[bi, h])

    nitem = bs * RW_HEADS
    items = range(nitem)
    aa_l = [jnp.where(keep, _dot_nt(lhs_l[i], rhs_l[i]), 0.0).astype(BF16) for i in items]
    pw_l = [aa_l[i][:cl, 2 * cl:] for i in items]
    tinv_l = [eye_bf + pw_l[i] for i in items]
    for _ in range(int(math.log2(cl)) - 1):
        pw_l = [jnp.dot(pw_l[i], pw_l[i], preferred_element_type=F32).astype(BF16) for i in items]
        tinv_l = [jnp.dot(tinv_l[i], eye_bf + pw_l[i], preferred_element_type=F32).astype(BF16) for i in items]
    vh_cat = [jnp.concatenate([vh_l[i], hcat_l[i].astype(BF16)], axis=0) for i in items]
    x1_l = [jnp.dot(aa_l[i][:cl, :2 * cl], vh_cat[i], preferred_element_type=F32).astype(BF16) for i in items]
    u_l = [jnp.dot(tinv_l[i], x1_l[i], preferred_element_type=F32).astype(BF16) for i in items]
    o_l = [jnp.dot(aa_l[i][cl:, :], jnp.concatenate([vh_cat[i], u_l[i]], axis=0),
                   preferred_element_type=F32) for i in items]
    hn_l = [hcat_l[i] * etot_l[i]
            + lax.dot_general(kb_l[i], jnp.concatenate([vh_l[i], u_l[i]], axis=0), (((0,), (0,)), ((), ())),
                              preferred_element_type=F32) for i in items]

    for bi in range(bs):
        for h in range(RW_HEADS):
            i = bi * RW_HEADS + h
            o_scr[bi * cl:(bi + 1) * cl, h * RW_HD:(h + 1) * RW_HD] = o_l[i]
            h_scr[bi, h] = hn_l[i]
        prev_scr[bi:bi + 1, :] = ps[bi][cl - 1:cl, :]

    y = _rw_post(o_scr[...], r, k, v, g, rk_ref[...], lnw_ref[...], lnb_ref[...], ones_bd)
    for bi in range(bs):
        y_ref[:, bi * RW_WIDTH:(bi + 1) * RW_WIDTH] = y[bi * cl:(bi + 1) * cl].astype(y_ref.dtype)

    @pl.when(c == nc - 1)
    def _():
        hfin_ref[...] = h_scr[...]
        for bi in range(bs):
            shout_ref[bi] = ps[bi][cl - 1:cl, :]


def _rw_param_args(mu, w0, w2, a0, a2, g2, k_k, k_a, r_k, ln_w, ln_b):
    row = lambda z: z.reshape(1, -1).astype(F32)
    return (row(mu), row(w0), w2.astype(BF16), row(a0), a2.astype(BF16), g2.astype(BF16),
            row(k_k), row(k_a), row(r_k), row(ln_w), row(ln_b))


def rwkv_prompt(p_tm, shift, s0, params, *, bs=4):
    c_len = RW_HD
    t_len, nb, _ = p_tm.shape
    assert t_len % c_len == 0 and nb % bs == 0
    prm = _rw_param_args(*params)
    const = lambda a: pl.BlockSpec(a.shape, lambda b, c: (0,) * a.ndim)
    st_spec = pl.BlockSpec((bs, RW_HEADS, RW_HD, RW_HD), lambda b, c: (b, 0, 0, 0))
    sh_spec = pl.BlockSpec((bs, 1, RW_PROJ), lambda b, c: (b, 0, 0))
    in_specs = [pl.BlockSpec((c_len, bs * RW_PROJ), lambda b, c: (c, b)), sh_spec, st_spec] + [const(a) for a in prm]
    out_shape = (jax.ShapeDtypeStruct((t_len, nb * RW_WIDTH), BF16),
                 jax.ShapeDtypeStruct((nb, RW_HEADS, RW_HD, RW_HD), F32),
                 jax.ShapeDtypeStruct((nb, 1, RW_PROJ), F32))
    out_specs = (pl.BlockSpec((c_len, bs * RW_WIDTH), lambda b, c: (c, b)), st_spec, sh_spec)
    scratch = [pltpu.VMEM((bs, RW_PROJ), F32), pltpu.VMEM((bs, RW_HEADS, RW_HD, RW_HD), F32),
               pltpu.VMEM((bs * c_len, RW_WIDTH), F32)]
    h0 = jnp.swapaxes(s0, -1, -2)
    y, h_fin, sh = pl.pallas_call(
        functools.partial(_rw_chunk_kernel, c_len=c_len, bs=bs), grid=(nb // bs, t_len // c_len),
        in_specs=in_specs, out_specs=out_specs, out_shape=out_shape, scratch_shapes=scratch,
        compiler_params=_cparams("parallel", "arbitrary"), name="rwkv_prompt")(
            p_tm.reshape(t_len, nb * RW_PROJ), shift.reshape(nb, 1, RW_PROJ), h0, *prm)
    return y.reshape(t_len, nb, RW_WIDTH), jnp.swapaxes(h_fin, -1, -2), sh.reshape(nb, RW_PROJ)


def _rw_step_prep_kernel(p_ref, shift_ref, mu_ref, w0_ref, w2_ref, a0_ref, a2_ref, g2_ref, kk_ref, ka_ref,
                         r_ref, w_ref, k_ref, v_ref, a_ref, b_ref, g_ref):
    prm = (mu_ref[...], w0_ref[...], w2_ref[...], a0_ref[...], a2_ref[...], g2_ref[...],
           kk_ref[...], ka_ref[...])
    r, lw, k, v, a, b, g = _rw_prep(p_ref[...], shift_ref[...], prm, _head_ones())
    r_ref[...] = r
    w_ref[...] = jnp.exp(lw)
    k_ref[...] = k
    v_ref[...] = v
    a_ref[...] = a
    b_ref[...] = b
    g_ref[...] = g


def _rw_step_core_kernel(s_ref, r_ref, w_ref, k_ref, a_ref, b_ref, v_ref, s_out_ref, o_ref):
    s = s_ref[...]
    sa = jnp.sum(s * a_ref[...], axis=-1, keepdims=True)
    s_new = s * w_ref[...] + sa * b_ref[...] + v_ref[...] * k_ref[...]
    s_out_ref[...] = s_new
    o_ref[...] = jnp.sum(s_new * r_ref[...], axis=-1, keepdims=True)


def _rw_step_post_kernel(o_ref, r_ref, k_ref, v_ref, g_ref, rk_ref, lnw_ref, lnb_ref, y_ref):
    y_ref[...] = _rw_post(o_ref[...], r_ref[...], k_ref[...], v_ref[...], g_ref[...],
                          rk_ref[...], lnw_ref[...], lnb_ref[...], _head_ones()).astype(y_ref.dtype)


def rwkv_step(p, shift, s0, params, *, bt=8):
    n = p.shape[0]
    prm = _rw_param_args(*params)
    vec = jax.ShapeDtypeStruct((n, RW_WIDTH), F32)
    r, w, k, v, a, b, g = pl.pallas_call(
        _rw_step_prep_kernel, out_shape=(vec,) * 7, name="rwkv_step_prep")(p, shift, *prm[:8])
    rows = lambda z: z.reshape(n, RW_HEADS, 1, RW_HD)
    row_spec = pl.BlockSpec((bt, RW_HEADS, 1, RW_HD), lambda i: (i, 0, 0, 0))
    col_spec = pl.BlockSpec((bt, RW_HEADS, RW_HD, 1), lambda i: (i, 0, 0, 0))
    st_spec = pl.BlockSpec((bt, RW_HEADS, RW_HD, RW_HD), lambda i: (i, 0, 0, 0))
    s_new, o = pl.pallas_call(
        _rw_step_core_kernel, grid=(n // bt,),
        in_specs=[st_spec] + [row_spec] * 5 + [col_spec], out_specs=(st_spec, col_spec),
        out_shape=(jax.ShapeDtypeStruct(s0.shape, F32), jax.ShapeDtypeStruct((n, RW_HEADS, RW_HD, 1), F32)),
        compiler_params=_cparams("parallel"), name="rwkv_step_core")(
            s0, rows(r), rows(w), rows(k), rows(a), rows(b), v.reshape(n, RW_HEADS, RW_HD, 1))
    y = pl.pallas_call(
        _rw_step_post_kernel, out_shape=jax.ShapeDtypeStruct((n, RW_WIDTH), BF16), name="rwkv_step_post")(
            o.reshape(n, RW_WIDTH), r, k, v, g, *prm[8:])
    return y, s_new


RET_LOG_G = tuple(math.log(1.0 - 2.0 ** (-5.0 - h)) for h in range(RET_HEADS))


def _rope_tables(pos, half):
    j = lax.broadcasted_iota(jnp.int32, (1, half), 1).astype(F32)
    inv = jnp.exp(j * (-math.log(ROPE_BASE) / half))
    ang = pos * inv
    return jnp.cos(ang), jnp.sin(ang)


def _rope(x, cos, sin):
    half = RET_DK // 2
    outs = []
    for h in range(RET_HEADS):
        x1 = x[:, h * RET_DK:h * RET_DK + half]
        x2 = x[:, h * RET_DK + half:(h + 1) * RET_DK]
        outs += [x1 * cos - x2 * sin, x1 * sin + x2 * cos]
    return jnp.concatenate(outs, axis=-1)


def _ret_norm_gate(o, g):
    o = o * lax.rsqrt(jnp.mean(o * o, axis=-1, keepdims=True) + NORM_EPS)
    return jax.nn.silu(g) * o


def _ret_tables_kernel(cos_ref, sin_ref, dmask_ref, qdec_ref, kdec_ref, *, c_len):
    t_len = cos_ref.shape[0]
    pos = lax.broadcasted_iota(jnp.int32, (t_len, 1), 0).astype(F32)
    cos, sin = _rope_tables(pos, RET_DK // 2)
    cos_ref[...] = cos
    sin_ref[...] = sin
    ti = lax.broadcasted_iota(jnp.int32, (c_len, 1), 0).astype(F32)
    ii = lax.broadcasted_iota(jnp.int32, (c_len, c_len), 0)
    jj = lax.broadcasted_iota(jnp.int32, (c_len, c_len), 1)
    diff = (ii - jj).astype(F32)
    for h in range(RET_HEADS):
        lg = RET_LOG_G[h]
        dmask_ref[h] = jnp.where(diff >= 0, jnp.exp(lg * jnp.maximum(diff, 0.0)), 0.0)
        qdec_ref[h] = jnp.exp(lg * (ti + 1.0))
        kdec_ref[h] = jnp.exp(lg * (c_len - 1.0 - ti))


def _ret_chunk_kernel(q_ref, k_ref, v_ref, g_ref, cos_ref, sin_ref, dmask_ref, qdec_ref, kdec_ref,
                      y_ref, sfin_ref, s_scr, *, c_len):
    c = pl.program_id(1)

    @pl.when(c == 0)
    def _():
        s_scr[...] = jnp.zeros_like(s_scr)

    cos, sin = cos_ref[...], sin_ref[...]
    q = _rope(q_ref[...].astype(F32), cos, sin)
    k = _rope(k_ref[...].astype(F32), cos, sin) * (RET_DK ** -0.5)
    for h in range(RET_HEADS):
        c_dec = math.exp(RET_LOG_G[h] * c_len)
        qh = q[:, h * RET_DK:(h + 1) * RET_DK]
        kh = k[:, h * RET_DK:(h + 1) * RET_DK]
        vh = v_ref[:, h * RET_DV:(h + 1) * RET_DV]
        s_h = s_scr[h]
        sc = _dot_nt(qh, kh) * dmask_ref[h]
        o = _bdot(sc, vh) + _bdot(qh * qdec_ref[h], s_h)
        s_scr[h] = s_h * c_dec + _dot_tn(kh * kdec_ref[h], vh)
        gh = g_ref[:, h * RET_DV:(h + 1) * RET_DV].astype(F32)
        y_ref[:, h * RET_DV:(h + 1) * RET_DV] = _ret_norm_gate(o, gh).astype(y_ref.dtype)

    @pl.when(c == pl.num_programs(1) - 1)
    def _():
        sfin_ref[0] = s_scr[...]


def retention_prompt(q, k, v, g, *, nb, c_len=RET_CHUNK):
    n = q.shape[0]
    t_len = n // nb
    nc = t_len // c_len
    half = RET_DK // 2
    tabs = pl.pallas_call(
        functools.partial(_ret_tables_kernel, c_len=c_len),
        out_shape=(jax.ShapeDtypeStruct((t_len, half), F32), jax.ShapeDtypeStruct((t_len, half), F32),
                   jax.ShapeDtypeStruct((RET_HEADS, c_len, c_len), F32),
                   jax.ShapeDtypeStruct((RET_HEADS, c_len, 1), F32),
                   jax.ShapeDtypeStruct((RET_HEADS, c_len, 1), F32)),
        name="retention_tables")()
    spec = lambda w: pl.BlockSpec((c_len, w), lambda b, c: (b * nc + c, 0))
    pos_spec = pl.BlockSpec((c_len, half), lambda b, c: (c, 0))
    const = lambda a: pl.BlockSpec(a.shape, lambda b, c: (0, 0, 0))
    st_spec = pl.BlockSpec((1, RET_HEADS, RET_DK, RET_DV), lambda b, c: (b, 0, 0, 0))
    return pl.pallas_call(
        functools.partial(_ret_chunk_kernel, c_len=c_len), grid=(nb, nc),
        in_specs=[spec(NQ), spec(NQ), spec(NV), spec(NV), pos_spec, pos_spec] + [const(a) for a in tabs[2:]],
        out_specs=(spec(NV), st_spec),
        out_shape=(jax.ShapeDtypeStruct((n, NV), BF16),
                   jax.ShapeDtypeStruct((nb, RET_HEADS, RET_DK, RET_DV), F32)),
        scratch_shapes=[pltpu.VMEM((RET_HEADS, RET_DK, RET_DV), F32)],
        compiler_params=_cparams("parallel", "arbitrary"), name="retention_prompt")(q, k, v, g, *tabs)


def _ret_step_rope_kernel(q_ref, k_ref, qo_ref, ko_ref, *, pos0):
    pos = jnp.full((q_ref.shape[0], 1), pos0, F32)
    cos, sin = _rope_tables(pos, RET_DK // 2)
    qo_ref[...] = _rope(q_ref[...].astype(F32), cos, sin)
    ko_ref[...] = _rope(k_ref[...].astype(F32), cos, sin) * (RET_DK ** -0.5)


def _ret_step_core_kernel(s_ref, q_ref, k_ref, v_ref, g_ref, s_out_ref, y_ref):
    for i in range(s_ref.shape[0]):
        for h in range(RET_HEADS):
            gam = math.exp(RET_LOG_G[h])
            s_h = s_ref[i, h]
            qc = q_ref[i, h]
            kc = k_ref[i, h]
            vr = v_ref[i, h].astype(F32)
            qk = jnp.sum(qc * kc, axis=0, keepdims=True)
            o = qk * vr + jnp.sum((qc * gam) * s_h, axis=0, keepdims=True)
            s_out_ref[i, h] = s_h * gam + kc * vr
            y_ref[i, h] = _ret_norm_gate(o, g_ref[i, h].astype(F32)).astype(y_ref.dtype)


def retention_step(q, k, v, g, s0, *, pos0):
    n = q.shape[0]
    vec = jax.ShapeDtypeStruct((n, NQ), F32)
    qr, kr = pl.pallas_call(functools.partial(_ret_step_rope_kernel, pos0=pos0), out_shape=(vec, vec),
                            name="retention_step_rope")(q, k)
    col = lambda z: z.reshape(n, RET_HEADS, RET_DK, 1)
    row = lambda z: z.reshape(n, RET_HEADS, 1, RET_DV)
    tb = 2
    st_spec = pl.BlockSpec((tb, RET_HEADS, RET_DK, RET_DV), lambda i: (i, 0, 0, 0))
    col_spec = pl.BlockSpec((tb, RET_HEADS, RET_DK, 1), lambda i: (i, 0, 0, 0))
    row_spec = pl.BlockSpec((tb, RET_HEADS, 1, RET_DV), lambda i: (i, 0, 0, 0))
    s_new, y = pl.pallas_call(
        _ret_step_core_kernel, grid=(n // tb,),
        in_specs=[st_spec, col_spec, col_spec, row_spec, row_spec], out_specs=(st_spec, row_spec),
        out_shape=(jax.ShapeDtypeStruct(s0.shape, F32), jax.ShapeDtypeStruct((n, RET_HEADS, 1, RET_DV), BF16)),
        compiler_params=_cparams("parallel"), name="retention_step_core")(
            s0, col(qr), col(kr), row(v), row(g))
    return y.reshape(n, NV), s_new


def _xattn_prompt_kernel(x_ref, g_ref, wq_ref, mk_ref, mv_ref, wo_ref, o_ref, att_scr):
    x = x_ref[...]
    q = jnp.dot(_rms(x, g_ref[...]).astype(BF16), wq_ref[...], preferred_element_type=F32)
    for h in range(MEM_HEADS):
        hs = slice(h * MEM_HD, (h + 1) * MEM_HD)
        s = _dot_nt(q[:, hs], mk_ref[0, :, hs]) * (MEM_HD ** -0.5)
        s = s - jnp.max(s, axis=-1, keepdims=True)
        e = jnp.exp(s)
        p = e / jnp.sum(e, axis=-1, keepdims=True)
        att_scr[:, hs] = _bdot(p, mv_ref[0, :, hs])
    o_ref[...] = x + jnp.dot(att_scr[...].astype(BF16), wo_ref[...], preferred_element_type=F32)


def xattn_prompt(x, gain, w_q, mem_k, mem_v, w_o, *, nb, tm=512):
    n = x.shape[0]
    tiles_per_b = n // nb // tm
    row = pl.BlockSpec((tm, D_MODEL), lambda i: (i, 0))
    wspec = pl.BlockSpec((D_MODEL, D_MODEL), lambda i: (0, 0))
    mspec = pl.BlockSpec((1, N_MEM, D_MODEL), lambda i: (i // tiles_per_b, 0, 0))
    return pl.pallas_call(
        _xattn_prompt_kernel, grid=(n // tm,),
        in_specs=[row, pl.BlockSpec((1, D_MODEL), lambda i: (0, 0)), wspec, mspec, mspec, wspec],
        out_specs=row, out_shape=jax.ShapeDtypeStruct((n, D_MODEL), F32),
        scratch_shapes=[pltpu.VMEM((tm, D_MODEL), F32)],
        compiler_params=_cparams("parallel"), name="xattn_prompt")(
            x, gain.reshape(1, D_MODEL), w_q, mem_k, mem_v, w_o)


def _xattn_step_kernel(q_ref, mk_ref, mv_ref, o_ref, *, tb):
    half = N_MEM // 2
    both = lambda z: jnp.concatenate([z, z], axis=1)
    fold = lambda z, op: op(z[:, :MEM_HEADS], z[:, MEM_HEADS:])
    for i in range(tb):
        k8 = jnp.concatenate([mk_ref[0, i, :half], mk_ref[0, i, half:]], axis=1)
        v8 = jnp.concatenate([mv_ref[0, i, :half], mv_ref[0, i, half:]], axis=1)
        q8 = jnp.concatenate([q_ref[i], q_ref[i]], axis=0)
        s = jnp.sum(k8 * q8[None], axis=-1, keepdims=True) * (MEM_HD ** -0.5)
        smax = both(fold(jnp.max(s, axis=0, keepdims=True), jnp.maximum))
        e = jnp.exp(s - smax)
        den = both(fold(jnp.sum(e, axis=0, keepdims=True), jnp.add))
        o8 = jnp.sum((e / den) * v8, axis=0)
        o_ref[i] = o8[:MEM_HEADS] + o8[MEM_HEADS:]


def xattn_step(q, cache_k, cache_v, layer, *, tb=2):
    n = q.shape[0]
    qspec = pl.BlockSpec((tb, MEM_HEADS, MEM_HD), lambda i: (i, 0, 0))
    cspec = pl.BlockSpec((1, tb, N_MEM, MEM_HEADS, MEM_HD), lambda i: (layer, i, 0, 0, 0))
    o = pl.pallas_call(
        functools.partial(_xattn_step_kernel, tb=tb), grid=(n // tb,),
        in_specs=[qspec, cspec, cspec], out_specs=qspec,
        out_shape=jax.ShapeDtypeStruct((n, MEM_HEADS, MEM_HD), F32),
        compiler_params=_cparams("parallel"), name="xattn_step")(
            q.reshape(n, MEM_HEADS, MEM_HD), cache_k, cache_v)
    return o.reshape(n, D_MODEL)


ROUTER_LANES = 128
NEG_BIG = -1e30


def _moe_gates(logits):
    lane = lax.broadcasted_iota(jnp.int32, logits.shape, 1)
    first = lambda mask: jnp.min(jnp.where(mask, lane, ROUTER_LANES), axis=-1, keepdims=True)
    is_c = lane < MOE_GROUPS
    lc = jnp.where(is_c, logits, NEG_BIG)
    mc = jnp.max(lc, axis=-1, keepdims=True)
    g_idx = first(lc == mc)
    p_g = 1.0 / jnp.sum(jnp.where(is_c, jnp.exp(lc - mc), 0.0), axis=-1, keepdims=True)
    fl = lane - MOE_GROUPS
    in_g = (fl >= 0) & (fl < MOE_EXPERTS) & ((fl // MOE_PER_GROUP) == g_idx)
    lf = jnp.where(in_g, logits, NEG_BIG)
    m1 = jnp.max(lf, axis=-1, keepdims=True)
    i1 = first(lf == m1)
    lf2 = jnp.where(lane == i1, NEG_BIG, lf)
    m2 = jnp.max(lf2, axis=-1, keepdims=True)
    i2 = first(lf2 == m2)
    e2 = jnp.exp(m2 - m1)
    w_top = 1.0 / (1.0 + e2)
    gate = p_g * (jnp.where(lane == i1, w_top, 0.0) + jnp.where(lane == i2, e2 * w_top, 0.0))
    return gate, g_idx


MOE_TM = 512
MOE_GW = MOE_PER_GROUP * MOE_HIDDEN
HG_W = D_MODEL + ROUTER_LANES


def _router_logits(h, wr_ref, br_ref):
    h_hi = h.astype(BF16)
    h_lo = (h - h_hi.astype(F32)).astype(BF16)
    acc = jnp.dot(h_hi, wr_ref[0], preferred_element_type=F32)
    acc = acc + jnp.dot(h_hi, wr_ref[1], preferred_element_type=F32)
    acc = acc + jnp.dot(h_lo, wr_ref[0], preferred_element_type=F32)
    return acc + br_ref[...]


def _group_ffn(hb, gate, grp, w1_ref, w3_ref, w2_ref):
    lane = lax.broadcasted_iota(jnp.int32, gate.shape, 1)
    acc = None
    for e in range(MOE_PER_GROUP):
        a1 = jnp.dot(hb, w1_ref[e].astype(BF16), preferred_element_type=F32)
        a3 = jnp.dot(hb, w3_ref[e].astype(BF16), preferred_element_type=F32)
        ge = jnp.sum(jnp.where(lane == MOE_GROUPS + grp * MOE_PER_GROUP + e, gate, 0.0), axis=-1, keepdims=True)
        hid = (jax.nn.silu(a1) * a3 * ge).astype(BF16)
        part = jnp.dot(hid, w2_ref[e].astype(BF16), preferred_element_type=F32)
        acc = part if acc is None else acc + part
    return acc


def _moe_kernel(x_ref, g_ref, wr_ref, br_ref, w1_ref, w3_ref, w2_ref, *rest, final_norm):
    if final_norm:
        fin_ref, o_ref, h_scr, gate_scr = rest
    else:
        o_ref, h_scr, gate_scr = rest
    grp = pl.program_id(1)

    @pl.when(grp == 0)
    def _():
        x = x_ref[...]
        h = _rms(x, g_ref[...])
        h_scr[...] = h.astype(BF16)
        gate_scr[...] = _moe_gates(_router_logits(h, wr_ref, br_ref))[0]
        o_ref[...] = x

    o_ref[...] += _group_ffn(h_scr[...], gate_scr[...], grp, w1_ref, w3_ref, w2_ref)

    if final_norm:
        @pl.when(grp == MOE_GROUPS - 1)
        def _():
            o_ref[...] = _rms(o_ref[...], fin_ref[...])


def moe_dense(x, gain, w_r, b_r, w1g, w3g, w2g, layer, *, tm=512, final_gain=None):
    n = x.shape[0]
    tm = min(tm, n)
    gain = gain.reshape(1, D_MODEL)
    row = pl.BlockSpec((tm, D_MODEL), lambda i, g: (i, 0))
    const2 = lambda a: pl.BlockSpec(a.shape, lambda i, g: (0,) * a.ndim)
    goff = layer * MOE_GROUPS
    wspec = pl.BlockSpec((MOE_PER_GROUP, D_MODEL, MOE_HIDDEN), lambda i, g: (goff + g, 0, 0))
    args = [x, gain, w_r, b_r, w1g, w3g, w2g]
    in_specs = [row, const2(gain), const2(w_r), const2(b_r), wspec, wspec,
                pl.BlockSpec((MOE_PER_GROUP, MOE_HIDDEN, D_MODEL), lambda i, g: (goff + g, 0, 0))]
    if final_gain is not None:
        args.append(final_gain.reshape(1, D_MODEL))
        in_specs.append(const2(args[-1]))
    return pl.pallas_call(
        functools.partial(_moe_kernel, final_norm=final_gain is not None), grid=(n // tm, MOE_GROUPS),
        in_specs=in_specs, out_specs=row, out_shape=jax.ShapeDtypeStruct((n, D_MODEL), F32),
        scratch_shapes=[pltpu.VMEM((tm, D_MODEL), BF16), pltpu.VMEM((tm, ROUTER_LANES), F32)],
        compiler_params=_cparams("parallel", "arbitrary"), name="moe")(*args)


def _moe_route_kernel(x_ref, g_ref, wr_ref, br_ref, hg_ref, gid_ref):
    h = _rms(x_ref[...], g_ref[...])
    gate, g_idx = _moe_gates(_router_logits(h, wr_ref, br_ref))
    hg_ref[:, :D_MODEL] = h
    hg_ref[:, D_MODEL:] = gate
    gid_ref[...] = jnp.broadcast_to(g_idx, gid_ref.shape)


def _gather_rows_kernel(idx_ref, src_ref, *rest, rows, add):
    if add:
        add_ref, out_ref, buf, sem = rest
    else:
        out_ref, buf, sem = rest

    def start(r, carry):
        pltpu.make_async_copy(src_ref.at[pl.ds(idx_ref[0, 0, r], 1)], buf.at[pl.ds(r, 1)], sem).start()
        return carry

    lax.fori_loop(0, rows, start, 0, unroll=8)
    pltpu.make_async_copy(src_ref.at[pl.ds(0, rows)], buf, sem).wait()
    if add:
        out_ref[...] = add_ref[...] + buf[...]
    else:
        out_ref[...] = buf[...]


def gather_rows(src, idx, *, add=None, rows=MOE_TM):
    n_out = idx.shape[0]
    width = src.shape[1]
    nt = n_out // rows
    row = pl.BlockSpec((rows, width), lambda t: (t, 0))
    in_specs = [pl.BlockSpec((1, 1, rows), lambda t: (t, 0, 0), memory_space=pltpu.SMEM),
                pl.BlockSpec(memory_space=pl.ANY)]
    args = [idx.reshape(nt, 1, rows), src]
    if add is not None:
        in_specs.append(row)
        args.append(add)
    return pl.pallas_call(
        functools.partial(_gather_rows_kernel, rows=rows, add=add is not None), grid=(nt,),
        in_specs=in_specs, out_specs=row, out_shape=jax.ShapeDtypeStruct((n_out, width), F32),
        scratch_shapes=[pltpu.VMEM((rows, width), F32), pltpu.SemaphoreType.DMA(())],
        compiler_params=_cparams("arbitrary"), name="gather_rows")(*args)


def _moe_group_kernel(tg_ref, tv_ref, hg_ref, w1_ref, w3_ref, w2_ref, o_ref):
    t = pl.program_id(0)

    @pl.when(tv_ref[t] != 0)
    def _():
        o_ref[...] = _group_ffn(hg_ref[:, :D_MODEL].astype(BF16), hg_ref[:, D_MODEL:], tg_ref[t],
                                w1_ref, w3_ref, w2_ref)

    @pl.when(tv_ref[t] == 0)
    def _():
        o_ref[...] = jnp.zeros_like(o_ref)


def _moe_sort_plan(gid, n, tm):
    n_tiles = n // tm + MOE_GROUPS
    onehot = (gid[:, None] == jnp.arange(MOE_GROUPS, dtype=jnp.int32)[None, :]).astype(jnp.int32)
    csum = jnp.cumsum(onehot, axis=0)
    rank = jnp.take_along_axis(csum, gid[:, None], axis=1)[:, 0] - 1
    counts = csum[-1]
    tiles_per_g = (counts + tm - 1) // tm
    tile_end = jnp.cumsum(tiles_per_g)
    tile_off = tile_end - tiles_per_g
    dest = tile_off[gid] * tm + rank
    row_src = jnp.zeros((n_tiles * tm,), jnp.int32).at[dest].set(jnp.arange(n, dtype=jnp.int32))
    tix = jnp.arange(n_tiles, dtype=jnp.int32)
    tile_group = jnp.minimum(jnp.sum((tix[:, None] >= tile_end[None, :]).astype(jnp.int32), axis=1),
                             MOE_GROUPS - 1)
    tile_valid = (tix < tile_end[-1]).astype(jnp.int32)
    return dest, row_src, tile_group, tile_valid


def moe_sorted(x, gain, w_r, b_r, w1g, w3g, w2g, layer, *, tm=MOE_TM):
    n = x.shape[0]
    goff = layer * MOE_GROUPS
    row = lambda w: pl.BlockSpec((tm, w), lambda i: (i, 0))
    const = lambda a: pl.BlockSpec(a.shape, lambda i: (0,) * a.ndim)
    gain = gain.reshape(1, D_MODEL)
    hg, gid = pl.pallas_call(
        _moe_route_kernel, grid=(n // tm,), in_specs=[row(D_MODEL), const(gain), const(w_r), const(b_r)],
        out_specs=(row(HG_W), row(ROUTER_LANES)),
        out_shape=(jax.ShapeDtypeStruct((n, HG_W), F32), jax.ShapeDtypeStruct((n, ROUTER_LANES), jnp.int32)),
        compiler_params=_cparams("parallel"), name="moe_route")(x, gain, w_r, b_r)
    dest, row_src, tile_group, tile_valid = _moe_sort_plan(gid[:, 0], n, tm)
    hg_sorted = gather_rows(hg, row_src, rows=tm)
    n_tiles = row_src.shape[0] // tm
    wspec = pl.BlockSpec((MOE_PER_GROUP, D_MODEL, MOE_HIDDEN), lambda t, tg, tv: (goff + tg[t], 0, 0))
    grid_spec = pltpu.PrefetchScalarGridSpec(
        num_scalar_prefetch=2, grid=(n_tiles,),
        in_specs=[pl.BlockSpec((tm, HG_W), lambda t, tg, tv: (t, 0)), wspec, wspec,
                  pl.BlockSpec((MOE_PER_GROUP, MOE_HIDDEN, D_MODEL), lambda t, tg, tv: (goff + tg[t], 0, 0))],
        out_specs=pl.BlockSpec((tm, D_MODEL), lambda t, tg, tv: (t, 0)))
    out_sorted = pl.pallas_call(
        _moe_group_kernel, grid_spec=grid_spec,
        out_shape=jax.ShapeDtypeStruct((n_tiles * tm, D_MODEL), F32),
        compiler_params=_cparams("arbitrary"), name="moe_group")(tile_group, tile_valid, hg_sorted, w1g, w3g, w2g)
    return gather_rows(out_sorted, dest, add=x, rows=tm)


def _group_weights(w1, w3, w2):
    ne = w1.shape[0] * MOE_EXPERTS
    return (w1.reshape(ne, D_MODEL, MOE_HIDDEN), w3.reshape(ne, D_MODEL, MOE_HIDDEN),
            w2.reshape(ne, MOE_HIDDEN, D_MODEL))


def _router_params(w_rc, b_rc, w_rf, b_rf):
    pad = ROUTER_LANES - MOE_GROUPS - MOE_EXPERTS
    w_r = jnp.concatenate([w_rc, w_rf, jnp.zeros((D_MODEL, pad), F32)], axis=1).astype(F32)
    b_r = jnp.concatenate([b_rc, b_rf, jnp.zeros((pad,), F32)]).reshape(1, ROUTER_LANES).astype(F32)
    w_hi = w_r.astype(BF16)
    w_lo = (w_r - w_hi.astype(F32)).astype(BF16)
    return jnp.stack([w_hi, w_lo]), b_r


def _rmsnorm_kernel(x_ref, g_ref, o_ref):
    o_ref[...] = _rms(x_ref[...], g_ref[...])


def rmsnorm_rows(x, gain, *, tm=1024):
    n = x.shape[0]
    tm = min(tm, n)
    row = pl.BlockSpec((tm, D_MODEL), lambda i: (i, 0))
    return pl.pallas_call(
        _rmsnorm_kernel, grid=(n // tm,), in_specs=[row, pl.BlockSpec((1, D_MODEL), lambda i: (0, 0))],
        out_specs=row, out_shape=jax.ShapeDtypeStruct((n, D_MODEL), F32),
        compiler_params=_cparams("parallel"), name="rmsnorm")(x, gain.reshape(1, D_MODEL))


def _stack(parts):
    return parts[0][None] if len(parts) == 1 else jnp.stack(parts)


def _trunk(x, nb, w, s5_re, s5_im, rw_state, rw_shift, ret_state, mem_k, mem_v, pos0):
    n = x.shape[0]
    t_len = n // nb
    single = t_len == 1
    out_s5_re, out_s5_im, out_rw, out_shift, out_ret = [], [], [], [], []
    for layer in range(DEPTH):
        if layer % 2 == 0:
            i = layer // 2
            s5p = w['s5p'][i]
            rwp = tuple(w[k][i] for k in ('rw_mu', 'rw_w0', 'rw_w2', 'rw_a0', 'rw_a2', 'rw_g2',
                                          'rw_k_k', 'rw_k_a', 'rw_r_k', 'rw_ln_w', 'rw_ln_b'))
            w_in = w['w_in0_bf'][i]
            w_out = w['w_out0_bf'][i]
            if single:
                u, p = linear(x, w_in, gain=w['norm_mix'][layer], splits=(S5_WIDTH, RW_PROJ))
                y_s5, sr, si = s5_mixer(u.reshape(1, nb, S5_WIDTH), s5_re[i], s5_im[i], s5p,
                                        w['s5_d'][i], w['s5_w_glu'][i], tc=1)
                y_rw, srw = rwkv_step(p, rw_shift[i], rw_state[i], rwp)
                sh = p
                x = linear(y_s5.reshape(nb, S5_WIDTH), w_out[:S5_WIDTH], x2=y_rw, w2=w_out[S5_WIDTH:], residual=x)
            else:
                u, p = linear(x, w_in, gain=w['norm_mix'][layer], splits=(S5_WIDTH, RW_PROJ),
                              out_tmajor=True, batch=nb)
                y_s5, sr, si = s5_mixer(u, s5_re[i], s5_im[i], s5p, w['s5_d'][i], w['s5_w_glu'][i], tc=128)
                y_rw, srw, sh = rwkv_prompt(p, rw_shift[i], rw_state[i], rwp)
                x = linear(y_s5, w_out[:S5_WIDTH], x2=y_rw, w2=w_out[S5_WIDTH:], residual=x, x_tmajor=True)
            out_s5_re.append(sr.reshape(nb, S5_GROUPS, S5_N))
            out_s5_im.append(si.reshape(nb, S5_GROUPS, S5_N))
            out_rw.append(srw)
            out_shift.append(sh)
        else:
            j = layer // 2
            q, k, v, g = linear(x, w['w_in1_bf'][j], gain=w['norm_mix'][layer], out_dtype=BF16,
                                splits=(NQ, NQ, NV, NV), tm=256)
            if single:
                y, s_new = retention_step(q, k, v, g, ret_state[j], pos0=pos0)
            else:
                y, s_new = retention_prompt(q, k, v, g, nb=nb)
            x = linear(y, w['w_out1_bf'][j], residual=x)
            out_ret.append(s_new)
        w_q = w['w_mq_bf'][layer]
        w_o = w['w_mo_bf'][layer]
        if single:
            q = linear(x, w_q, gain=w['norm_mem'][layer])
            att = xattn_step(q, mem_k, mem_v, layer)
            x = linear(att, w_o, residual=x)
        else:
            x = xattn_prompt(x, w['norm_mem'][layer], w_q, mem_k[layer], mem_v[layer], w_o, nb=nb)
        w_r, b_r = w['router'][layer]
        x = moe_dense(x, w['norm_ffn'][layer], w_r, b_r, *w['moe_g'], layer, tm=1024,
                      final_gain=w['norm_final'] if layer == DEPTH - 1 else None)
    y = x
    return (y, _stack(out_s5_re), _stack(out_s5_im), _stack(out_rw), _stack(out_shift), _stack(out_ret))


def kernel(x_prompt, x_sample, mem_prompt, state_s5_re, state_s5_im, state_rwkv, state_shift, state_ret, cache_mem_k, cache_mem_v, norm_mix, norm_mem, norm_ffn, norm_final, w_in0, w_out0, s5_a_re, s5_a_im, s5_b_re, s5_b_im, s5_c_re, s5_c_im, s5_d, s5_log_dt, s5_w_glu, rw_mu, rw_w0, rw_w2, rw_a0, rw_a2, rw_g2, rw_k_k, rw_k_a, rw_r_k, rw_ln_w, rw_ln_b, w_in1, w_out1, mem_norm, w_mq, w_mk, w_mv, w_mo, moe_w_rc, moe_b_rc, moe_w_rf, moe_b_rf, moe_w1, moe_w3, moe_w2):
    w = dict(norm_mix=norm_mix, norm_mem=norm_mem, norm_ffn=norm_ffn, norm_final=norm_final,
             w_in0=w_in0, w_out0=w_out0, s5_a_re=s5_a_re, s5_a_im=s5_a_im, s5_b_re=s5_b_re, s5_b_im=s5_b_im,
             s5_c_re=s5_c_re, s5_c_im=s5_c_im, s5_d=s5_d, s5_log_dt=s5_log_dt, s5_w_glu=s5_w_glu,
             rw_mu=rw_mu, rw_w0=rw_w0, rw_w2=rw_w2, rw_a0=rw_a0, rw_a2=rw_a2, rw_g2=rw_g2,
             rw_k_k=rw_k_k, rw_k_a=rw_k_a, rw_r_k=rw_r_k, rw_ln_w=rw_ln_w, rw_ln_b=rw_ln_b,
             w_in1=w_in1, w_out1=w_out1, w_mq=w_mq, w_mo=w_mo,
             moe_w_rc=moe_w_rc, moe_b_rc=moe_b_rc, moe_w_rf=moe_w_rf, moe_b_rf=moe_b_rf,
             moe_w1=moe_w1, moe_w3=moe_w3, moe_w2=moe_w2)
    nbp, t_len, _ = x_prompt.shape
    nbs = x_sample.shape[0]
    n_even, n_odd = state_s5_re.shape[0], state_ret.shape[0]
    for name in ('w_in0', 'w_out0', 'w_in1', 'w_out1', 'w_mq', 'w_mo'):
        w[name + '_bf'] = w[name].astype(BF16)
    w['s5p'] = [_s5_params(s5_a_re[i], s5_a_im[i], s5_b_re[i], s5_b_im[i], s5_c_re[i], s5_c_im[i], s5_log_dt[i])
                for i in range(n_even)]
    w['router'] = [_router_params(moe_w_rc[l], moe_b_rc[l], moe_w_rf[l], moe_b_rf[l]) for l in range(DEPTH)]
    w['moe_g'] = _group_weights(moe_w1, moe_w3, moe_w2)

    mem = mem_prompt.reshape(nbp * N_MEM, D_MODEL)
    mem_k_l, mem_v_l, mem_k_h, mem_v_h = [], [], [], []
    for layer in range(DEPTH):
        w_kv = jnp.concatenate([w_mk[layer], w_mv[layer]], axis=1).astype(BF16)
        mk, mv, mk_h, mv_h = linear(mem, w_kv, gain=mem_norm[layer], splits=(D_MODEL, D_MODEL),
                                    head_copies=(MEM_HEADS, MEM_HD))
        mem_k_l.append(mk.reshape(nbp, N_MEM, D_MODEL))
        mem_v_l.append(mv.reshape(nbp, N_MEM, D_MODEL))
        mem_k_h.append(mk_h.reshape(nbp, N_MEM, MEM_HEADS, MEM_HD))
        mem_v_h.append(mv_h.reshape(nbp, N_MEM, MEM_HEADS, MEM_HD))
    mem_k_p = jnp.stack(mem_k_h)
    mem_v_p = jnp.stack(mem_v_h)

    zeros = lambda *shape: jnp.zeros(shape, F32)
    y_p, s5r_p, s5i_p, rw_p, sh_p, ret_p = _trunk(
        x_prompt.reshape(nbp * t_len, D_MODEL), nbp, w,
        zeros(n_even, nbp, S5_STATE), zeros(n_even, nbp, S5_STATE),
        zeros(n_even, nbp, RW_HEADS, RW_HD, RW_HD), zeros(n_even, nbp, RW_PROJ),
        None, mem_k_l, mem_v_l, 0.0)
    y_s, s5r_s, s5i_s, rw_s, sh_s, ret_s = _trunk(
        x_sample.reshape(nbs, D_MODEL), nbs, w,
        state_s5_re.reshape(n_even, nbs, S5_STATE), state_s5_im.reshape(n_even, nbs, S5_STATE),
        state_rwkv, state_shift, state_ret,
        cache_mem_k, cache_mem_v, float(PAST_LEN))
    return (y_p.reshape(nbp, t_len, D_MODEL), y_s.reshape(nbs, 1, D_MODEL),
            s5r_p, s5i_p, rw_p, sh_p, ret_p, mem_k_p, mem_v_p, s5r_s, s5i_s, rw_s, sh_s, ret_s)
```

```python
import collections
import functools
import math

import jax
import jax.numpy as jnp
from jax import lax
from jax.experimental import pallas as pl
from jax.experimental.pallas import tpu as pltpu

F32 = jnp.float32
BF16 = jnp.bfloat16

D_MODEL = 1024
DEPTH = 2
PAST_LEN = 16384
S5_WIDTH = 512
S5_GROUP = 16
S5_GROUPS = 32
S5_N = 64
S5_STATE = S5_GROUPS * S5_N
S5_GBLK = 8
RW_WIDTH = 512
RW_HD = 64
RW_HEADS = 8
RW_LORA = 256
RW_PROJ = 3 * RW_WIDTH + RW_LORA
IN0 = S5_WIDTH + RW_PROJ
RET_DK = 256
RET_HEADS = 4
RET_DV = 512
RET_CHUNK = 256
NQ = RET_HEADS * RET_DK
NV = RET_HEADS * RET_DV
IN1 = 2 * NQ + 2 * NV
N_MEM = 256
MEM_HEADS = 4
MEM_HD = 256
MOE_GROUPS = 4
MOE_PER_GROUP = 4
MOE_EXPERTS = 16
MOE_HIDDEN = 256
NORM_EPS = 1e-6
RW_GN_EPS = 64e-5
ROPE_BASE = 10000.0

VMEM_LIMIT = 56 * 1024 * 1024


def _cparams(*sem):
    return pltpu.CompilerParams(dimension_semantics=sem, vmem_limit_bytes=VMEM_LIMIT)


def _bdot(a, b):
    return jnp.dot(a.astype(BF16), b.astype(BF16), preferred_element_type=F32)


def _dot_nt(a, b):
    return lax.dot_general(a.astype(BF16), b.astype(BF16), (((1,), (1,)), ((), ())),
                           preferred_element_type=F32)


def _dot_tn(a, b):
    return lax.dot_general(a.astype(BF16), b.astype(BF16), (((0,), (0,)), ((), ())),
                           preferred_element_type=F32)


def _split3(x):
    hi = x.astype(BF16)
    r1 = x - hi.astype(F32)
    mid = r1.astype(BF16)
    lo = (r1 - mid.astype(F32)).astype(BF16)
    return hi, mid, lo


def _dot_exact_rhs(x, m_bf16, passes=3):
    hi, mid, lo = _split3(x)
    acc = jnp.dot(hi, m_bf16, preferred_element_type=F32)
    if passes > 1:
        acc = acc + jnp.dot(mid, m_bf16, preferred_element_type=F32)
    if passes > 2:
        acc = acc + jnp.dot(lo, m_bf16, preferred_element_type=F32)
    return acc


def _rms(x, g):
    ms = jnp.mean(x * x, axis=-1, keepdims=True)
    return x * lax.rsqrt(ms + NORM_EPS) * g


def _linear_kernel(*refs, norm, two, res, n_dup):
    it = iter(refs)
    x_ref = next(it)
    g_ref = next(it) if norm else None
    w_ref = next(it)
    x2_ref = next(it) if two else None
    w2_ref = next(it) if two else None
    r_ref = next(it) if res else None
    o_refs = list(it)
    dup_refs = o_refs[len(o_refs) - n_dup:] if n_dup else []
    o_refs = o_refs[:len(o_refs) - n_dup]
    x = x_ref[...].astype(F32)
    if norm:
        x = _rms(x, g_ref[...])
    xb = x.astype(BF16)
    x2b = x2_ref[...].astype(BF16) if two else None
    col = 0
    for o_ref in o_refs:
        m = o_ref.shape[-1]
        step = next((s for s in (512, 256) if m % s == 0), m)
        for j in range(m // step):
            sl = slice(col + j * step, col + (j + 1) * step)
            acc = jnp.dot(xb, w_ref[:, sl], preferred_element_type=F32)
            if two:
                acc = acc + jnp.dot(x2b, w2_ref[:, sl], preferred_element_type=F32)
            if res:
                acc = acc + r_ref[:, sl]
            o_ref[:, j * step:(j + 1) * step] = acc.astype(o_ref.dtype)
        col += m
    for o_ref, d_ref in zip(o_refs, dup_refs):
        hd = d_ref.shape[-1]
        for h in range(d_ref.shape[1]):
            d_ref[:, h, :] = o_ref[:, h * hd:(h + 1) * hd].astype(d_ref.dtype)


def _row_spec(tm, width, tmajor_b):
    if tmajor_b is None:
        return pl.BlockSpec((tm, width), lambda i: (i, 0))
    nb, tiles_per_b = tmajor_b
    return pl.BlockSpec((tm, width), lambda i: (i % tiles_per_b, i // tiles_per_b))


def linear(x, w, *, gain=None, x2=None, w2=None, residual=None, out_dtype=F32, tm=512,
           x_tmajor=False, out_tmajor=False, batch=None, splits=None, head_copies=None, name="linear"):
    if x_tmajor:
        t_len, nb, k = x.shape
        n = t_len * nb
    else:
        n, k = x.shape
        nb = batch
        t_len = n // nb if nb else None
    m = w.shape[1]
    tm = min(tm, n if not (x_tmajor or out_tmajor) else t_len)
    assert n % tm == 0
    tiles_per_b = (t_len // tm) if (x_tmajor or out_tmajor) else None
    args, specs = [], []

    def add_rows(a, tmajor):
        width = a.shape[-1]
        args.append(a.reshape(t_len, nb * width) if tmajor else a)
        specs.append(_row_spec(tm, width, (nb, tiles_per_b) if tmajor else None))

    add_rows(x, x_tmajor)
    if gain is not None:
        args.append(gain.reshape(1, k).astype(F32))
        specs.append(pl.BlockSpec((1, k), lambda i: (0, 0)))
    args.append(w)
    specs.append(pl.BlockSpec(w.shape, lambda i: (0, 0)))
    if x2 is not None:
        add_rows(x2, x_tmajor)
        args.append(w2)
        specs.append(pl.BlockSpec(w2.shape, lambda i: (0, 0)))
    if residual is not None:
        add_rows(residual, False)
    widths = tuple(splits) if splits else (m,)
    assert sum(widths) == m
    if out_tmajor:
        out_shape = [jax.ShapeDtypeStruct((t_len, nb * mw), out_dtype) for mw in widths]
    else:
        out_shape = [jax.ShapeDtypeStruct((n, mw), out_dtype) for mw in widths]
    out_specs = [_row_spec(tm, mw, (nb, tiles_per_b) if out_tmajor else None) for mw in widths]
    n_dup = 0
    if head_copies is not None:
        heads, hd = head_copies
        assert not out_tmajor and all(mw == heads * hd for mw in widths)
        n_dup = len(widths)
        out_shape += [jax.ShapeDtypeStruct((n, heads, hd), out_dtype)] * n_dup
        out_specs += [pl.BlockSpec((tm, heads, hd), lambda i: (i, 0, 0))] * n_dup
    kern = functools.partial(_linear_kernel, norm=gain is not None, two=x2 is not None,
                             res=residual is not None, n_dup=n_dup)
    outs = pl.pallas_call(
        kern, grid=(n // tm,), in_specs=specs, out_specs=out_specs, out_shape=out_shape,
        compiler_params=_cparams("parallel"), name=name)(*args)
    if out_tmajor:
        outs = [o.reshape(t_len, nb, mw) for o, mw in zip(outs, widths)]
    return outs if (splits or n_dup) else outs[0]


def _s5_kernel(u_ref, h_re_ref, h_im_ref, abar_re_ref, abar_im_ref, bb_re_ref, bb_im_ref,
               cc_re_ref, cc_im_ref, d_ref, wglu_ref, y_ref, s_re_ref, s_im_ref,
               x_re, x_im, st_re, st_im, il_scr, *, tc, nb, flat):
    c = pl.program_id(0)
    nlb = S5_WIDTH // 128
    rows = tc * nb
    nblk = S5_GROUPS // S5_GBLK
    bw_in = S5_GBLK * S5_GROUP
    bw_st = S5_GBLK * S5_N

    @pl.when(c == 0)
    def _():
        st_re[...] = h_re_ref[...]
        st_im[...] = h_im_ref[...]

    if flat:
        for b in range(nb):
            for j in range(nlb):
                il_scr[j, pl.ds(b, tc, stride=nb), :] = u_ref[:, b * S5_WIDTH + j * 128:b * S5_WIDTH + (j + 1) * 128]
        u = jnp.concatenate([il_scr[j] for j in range(nlb)], axis=-1)
    else:
        u = u_ref[...].reshape(rows, S5_WIDTH)
    ub = u.astype(BF16)
    for gb in range(nblk):
        ui = ub[:, gb * bw_in:(gb + 1) * bw_in]
        x_re[:, gb * bw_st:(gb + 1) * bw_st] = jnp.dot(ui, bb_re_ref[gb], preferred_element_type=F32)
        x_im[:, gb * bw_st:(gb + 1) * bw_st] = jnp.dot(ui, bb_im_ref[gb], preferred_element_type=F32)

    lane_blk = 1024
    for lb in range(S5_STATE // lane_blk):
        sl = slice(lb * lane_blk, (lb + 1) * lane_blk)
        ar = jnp.broadcast_to(abar_re_ref[:, sl], (nb, lane_blk))
        ai = jnp.broadcast_to(abar_im_ref[:, sl], (nb, lane_blk))

        def body(t, carry, sl=sl, ar=ar, ai=ai):
            xr, xi = carry
            r0 = pl.multiple_of(t * nb, nb)
            br = x_re[pl.ds(r0, nb), sl]
            bi = x_im[pl.ds(r0, nb), sl]
            nr = ar * xr - ai * xi + br
            ni = ar * xi + ai * xr + bi
            x_re[pl.ds(r0, nb), sl] = nr
            x_im[pl.ds(r0, nb), sl] = ni
            return nr, ni

        fr, fi = lax.fori_loop(0, tc, body, (st_re[:, sl], st_im[:, sl]), unroll=min(tc, 4))
        st_re[:, sl] = fr
        st_im[:, sl] = fi

    for gb in range(nblk):
        xr = x_re[:, gb * bw_st:(gb + 1) * bw_st].astype(BF16)
        xi = x_im[:, gb * bw_st:(gb + 1) * bw_st].astype(BF16)
        yb = (jnp.dot(xr, cc_re_ref[gb], preferred_element_type=F32)
              - jnp.dot(xi, cc_im_ref[gb], preferred_element_type=F32))
        cs = slice(gb * bw_in, (gb + 1) * bw_in)
        yb = yb + d_ref[:, cs] * u[:, cs]
        x_re[:, cs] = jax.nn.gelu(yb)
    y = x_re[:, :S5_WIDTH]
    y = y * jax.nn.sigmoid(jnp.dot(y.astype(BF16), wglu_ref[...], preferred_element_type=F32))
    if flat:
        for j in range(nlb):
            il_scr[j] = y[:, j * 128:(j + 1) * 128]
        for b in range(nb):
            for j in range(nlb):
                y_ref[:, b * S5_WIDTH + j * 128:b * S5_WIDTH + (j + 1) * 128] = (
                    il_scr[j, pl.ds(b, tc, stride=nb), :].astype(y_ref.dtype))
    else:
        y_ref[...] = y.reshape(y_ref.shape).astype(y_ref.dtype)

    @pl.when(c == pl.num_programs(0) - 1)
    def _():
        s_re_ref[...] = st_re[...]
        s_im_ref[...] = st_im[...]


def _s5_params(a_re, a_im, b_re, b_im, c_re, c_im, log_dt):
    dt = jnp.exp(log_dt.astype(F32))[:, None]
    ar, ai = a_re.astype(F32), a_im.astype(F32)
    mag = jnp.exp(dt * ar)
    abar_re, abar_im = mag * jnp.cos(dt * ai), mag * jnp.sin(dt * ai)
    den = ar * ar + ai * ai
    nr = abar_re - 1.0
    coef_re = (nr * ar + abar_im * ai) / den
    coef_im = (abar_im * ar - nr * ai) / den
    cr, ci = coef_re[..., None], coef_im[..., None]
    brf, bif = b_re.astype(F32), b_im.astype(F32)
    bb_re = cr * brf - ci * bif
    bb_im = cr * bif + ci * brf
    nblk = S5_GROUPS // S5_GBLK
    eye = jnp.eye(S5_GBLK, dtype=F32)

    def blockdiag_in(bb):
        t = jnp.transpose(bb, (0, 2, 1)).reshape(nblk, S5_GBLK, S5_GROUP, S5_N)
        m = jnp.einsum('kgcn,gh->kgchn', t, eye)
        return m.reshape(nblk, S5_GBLK * S5_GROUP, S5_GBLK * S5_N).astype(BF16)

    def blockdiag_out(cc):
        t = jnp.transpose(cc.astype(F32), (0, 2, 1)).reshape(nblk, S5_GBLK, S5_N, S5_GROUP)
        m = jnp.einsum('khnc,hg->khngc', t, eye)
        return m.reshape(nblk, S5_GBLK * S5_N, S5_GBLK * S5_GROUP).astype(BF16)

    return (abar_re.reshape(1, S5_STATE), abar_im.reshape(1, S5_STATE),
            blockdiag_in(bb_re), blockdiag_in(bb_im), blockdiag_out(c_re), blockdiag_out(c_im))


def s5_mixer(u_tm, h_re, h_im, params, d_skip, w_glu, *, tc):
    t_len, nb, _ = u_tm.shape
    abar_re, abar_im, bb_re, bb_im, cc_re, cc_im = params
    tc = min(tc, t_len)
    assert t_len % tc == 0 and nb % 8 == 0
    rows = tc * nb
    flat = t_len > 1
    full = lambda a: pl.BlockSpec(a.shape, lambda c: (0,) * a.ndim)
    if flat:
        u_arg = u_tm.reshape(t_len, nb * S5_WIDTH)
        io_spec = pl.BlockSpec((tc, nb * S5_WIDTH), lambda c: (c, 0))
        y_shape = jax.ShapeDtypeStruct((t_len, nb * S5_WIDTH), BF16)
    else:
        u_arg = u_tm
        io_spec = pl.BlockSpec((tc, nb, S5_WIDTH), lambda c: (c, 0, 0))
        y_shape = jax.ShapeDtypeStruct((t_len, nb, S5_WIDTH), BF16)
    args = (u_arg, h_re, h_im, abar_re, abar_im, bb_re, bb_im, cc_re, cc_im,
            d_skip.reshape(1, S5_WIDTH).astype(F32), w_glu.astype(BF16))
    in_specs = [io_spec] + [full(a) for a in args[1:]]
    st_shape = jax.ShapeDtypeStruct((nb, S5_STATE), F32)
    st_spec = pl.BlockSpec((nb, S5_STATE), lambda c: (0, 0))
    scratch = [pltpu.VMEM((rows, S5_STATE), F32), pltpu.VMEM((rows, S5_STATE), F32),
               pltpu.VMEM((nb, S5_STATE), F32), pltpu.VMEM((nb, S5_STATE), F32),
               pltpu.VMEM((S5_WIDTH // 128, rows if flat else 8, 128), F32)]
    y, s_re, s_im = pl.pallas_call(
        functools.partial(_s5_kernel, tc=tc, nb=nb, flat=flat), grid=(t_len // tc,), in_specs=in_specs,
        out_specs=(io_spec, st_spec, st_spec), out_shape=(y_shape, st_shape, st_shape),
        scratch_shapes=scratch, compiler_params=_cparams("arbitrary"), name="s5_mixer")(*args)
    return y.reshape(t_len, nb, S5_WIDTH), s_re, s_im


def _head_ones():
    i = lax.broadcasted_iota(jnp.int32, (RW_WIDTH, RW_WIDTH), 0) // RW_HD
    j = lax.broadcasted_iota(jnp.int32, (RW_WIDTH, RW_WIDTH), 1) // RW_HD
    return jnp.where(i == j, 1.0, 0.0).astype(BF16)


def _softplus(z):
    return jnp.maximum(z, 0.0) + jnp.log1p(jnp.exp(-jnp.abs(z)))


def _rw_prep(p, p_prev, prm, ones_bd):
    mu, w0, w2, a0, a2, g2, k_k, k_a = prm
    xm = p + (p_prev - p) * mu
    o1, o2, o3 = RW_WIDTH, 2 * RW_WIDTH, 3 * RW_WIDTH
    r, k, v = xm[:, :o1], xm[:, o1:o2], xm[:, o2:o3]
    wd, ad, gd = xm[:, o3:o3 + 64], xm[:, o3 + 64:o3 + 128], xm[:, o3 + 128:]
    w = -_softplus(-(w0 + _bdot(jnp.tanh(wd), w2))) - 0.5
    lw = -jnp.exp(w)
    a = jax.nn.sigmoid(a0 + _bdot(ad, a2))
    g = _bdot(jax.nn.sigmoid(gd), g2)
    kk = k * k_k
    ss = _dot_exact_rhs(kk * kk, ones_bd, passes=1)
    kk = kk / jnp.maximum(jnp.sqrt(ss), 1e-12)
    k = k * (1.0 + (a - 1.0) * k_a)
    return r, lw, k, v, -kk, kk * a, g


def _rw_post(o, r, k, v, g, r_k, ln_w, ln_b, ones_bd):
    inv = 1.0 / RW_HD
    mean = _dot_exact_rhs(o, ones_bd, passes=2) * inv
    d = o - mean
    var = _dot_exact_rhs(d * d, ones_bd, passes=1) * inv
    on = d * lax.rsqrt(var + RW_GN_EPS) * ln_w + ln_b
    bonus = _dot_exact_rhs(r * k * r_k, ones_bd, passes=1) * v
    return (on + bonus) * g


def _rw_chunk_kernel(p_ref, shift_ref, h0_ref, mu_ref, w0_ref, w2_ref, a0_ref, a2_ref, g2_ref,
                     kk_ref, ka_ref, rk_ref, lnw_ref, lnb_ref,
                     y_ref, hfin_ref, shout_ref, prev_scr, h_scr, o_scr, *, c_len, bs, side=None):
    c = pl.program_id(1)
    nc = pl.num_programs(1)
    cl = c_len

    @pl.when(c == 0)
    def _():
        prev_scr[...] = shift_ref[:, 0, :]
        h_scr[...] = h0_ref[...]

    ones_bd = _head_ones()
    row = lax.broadcasted_iota(jnp.int32, (cl, RW_PROJ), 0)
    ps, pprevs = [], []
    for bi in range(bs):
        p = p_ref[:, bi * RW_PROJ:(bi + 1) * RW_PROJ]
        pprevs.append(jnp.where(row == 0, prev_scr[bi:bi + 1, :], pltpu.roll(p, 1, 0)))
        ps.append(p)
    p_all = jnp.concatenate(ps, axis=0) if bs > 1 else ps[0]
    pprev_all = jnp.concatenate(pprevs, axis=0) if bs > 1 else pprevs[0]
    prm = (mu_ref[...], w0_ref[...], w2_ref[...], a0_ref[...], a2_ref[...], g2_ref[...],
           kk_ref[...], ka_ref[...])
    r, lw, k, v, a, b, g = _rw_prep(p_all, pprev_all, prm, ones_bd)

    ti = lax.broadcasted_iota(jnp.int32, (cl, cl), 0)
    si = lax.broadcasted_iota(jnp.int32, (cl, cl), 1)
    lmat = jnp.where(ti >= si, 1.0, 0.0).astype(BF16)
    eye = jnp.where(ti == si, 1.0, 0.0)
    mi = lax.broadcasted_iota(jnp.int32, (2 * cl, 3 * cl), 0)
    mj = lax.broadcasted_iota(jnp.int32, (2 * cl, 3 * cl), 1)
    t_row = jnp.where(mi >= cl, mi - cl, mi)
    s_col = jnp.where(mj < cl, mj, jnp.where(mj >= 2 * cl, mj - 2 * cl, -4 * cl))
    keep = (t_row - s_col) >= jnp.where(mi >= cl, 0, 1)
    eye_bf = eye.astype(BF16)

    lhs_l, rhs_l, vh_l, hcat_l, kb_l, etot_l = [], [], [], [], [], []
    for bi in range(bs):
        rs = slice(bi * cl, (bi + 1) * cl)
        lw_b = lw[rs]
        l_hi, l_mid, l_lo = _split3(lw_b)
        cum = (jnp.dot(lmat, l_hi, preferred_element_type=F32)
               + jnp.dot(lmat, l_mid, preferred_element_type=F32)
               + jnp.dot(lmat, l_lo, preferred_element_type=F32))
        tot = cum[cl - 1:cl, :]
        e_neg = jnp.exp(-cum)
        e_rem = jnp.exp(tot - cum)
        at = (a[rs] * jnp.exp(cum - lw_b)).astype(BF16)
        rt = (r[rs] * jnp.exp(cum)).astype(BF16)
        bt = (b[rs] * e_neg).astype(BF16)
        kt = (k[rs] * e_neg).astype(BF16)
        bh = (b[rs] * e_rem).astype(BF16)
        kh = (k[rs] * e_rem).astype(BF16)
        e_tot = jnp.exp(tot)
        vb = v[rs].astype(BF16)
        for h in range(RW_HEADS):
            hs = slice(h * RW_HD, (h + 1) * RW_HD)
            lhs_l.append(jnp.concatenate([at[:, hs], rt[:, hs]], axis=0))
            rhs_l.append(jnp.concatenate([kt[:, hs], eye_bf, bt[:, hs]], axis=0))
            vh_l.append(vb[:, hs])
            kb_l.append(jnp.concatenate([kh[:, hs], bh[:, hs]], axis=0))
            etot_l.append(jnp.sum(eye * e_tot[:, hs], axis=-1, keepdims=True))
            hcat_l.append(h_scr[bi, h])

    nitem = bs * RW_HEADS
    items = range(nitem)
    aa_l = [jnp.where(keep, _dot_nt(lhs_l[i], rhs_l[i]), 0.0).astype(BF16) for i in items]
    pw_l = [aa_l[i][:cl, 2 * cl:] for i in items]
    tinv_l = [eye_bf + pw_l[i] for i in items]
    for _ in range(int(math.log2(cl)) - 1):
        pw_l = [jnp.dot(pw_l[i], pw_l[i], preferred_element_type=F32).astype(BF16) for i in items]
        tinv_l = [jnp.dot(tinv_l[i], eye_bf + pw_l[i], preferred_element_type=F32).astype(BF16) for i in items]
    vh_cat = [jnp.concatenate([vh_l[i], hcat_l[i].astype(BF16)], axis=0) for i in items]
    x1_l = [jnp.dot(aa_l[i][:cl, :2 * cl], vh_cat[i], preferred_element_type=F32).astype(BF16) for i in items]
    u_l = [jnp.dot(tinv_l[i], x1_l[i], preferred_element_type=F32).astype(BF16) for i in items]
    o_l = [jnp.dot(aa_l[i][cl:, :], jnp.concatenate([vh_cat[i], u_l[i]], axis=0),
                   preferred_element_type=F32) for i in items]
    hn_l = [hcat_l[i] * etot_l[i]
            + lax.dot_general(kb_l[i], jnp.concatenate([vh_l[i], u_l[i]], axis=0), (((0,), (0,)), ((), ())),
                              preferred_element_type=F32) for i in items]

    for bi in range(bs):
        for h in range(RW_HEADS):
            i = bi * RW_HEADS + h
            o_scr[bi * cl:(bi + 1) * cl, h * RW_HD:(h + 1) * RW_HD] = o_l[i]
            h_scr[bi, h] = hn_l[i]
        prev_scr[bi:bi + 1, :] = ps[bi][cl - 1:cl, :]

    y = _rw_post(o_scr[...], r, k, v, g, rk_ref[...], lnw_ref[...], lnb_ref[...], ones_bd)
    for bi in range(bs):
        y_ref[:, bi * RW_WIDTH:(bi + 1) * RW_WIDTH] = y[bi * cl:(bi + 1) * cl].astype(y_ref.dtype)
    if side is not None:
        side()

    @pl.when(c == nc - 1)
    def _():
        hfin_ref[...] = h_scr[...]
        for bi in range(bs):
            shout_ref[bi] = ps[bi][cl - 1:cl, :]


def _rw_param_args(mu, w0, w2, a0, a2, g2, k_k, k_a, r_k, ln_w, ln_b):
    row = lambda z: z.reshape(1, -1).astype(F32)
    return (row(mu), row(w0), w2.astype(BF16), row(a0), a2.astype(BF16), g2.astype(BF16),
            row(k_k), row(k_a), row(r_k), row(ln_w), row(ln_b))


def rwkv_prompt(p_tm, shift, s0, params, *, bs=4, side=None):
    c_len = RW_HD
    t_len, nb, _ = p_tm.shape
    assert t_len % c_len == 0 and nb % bs == 0
    prm = _rw_param_args(*params)
    const = lambda a: pl.BlockSpec(a.shape, lambda b, c: (0,) * a.ndim)
    st_spec = pl.BlockSpec((bs, RW_HEADS, RW_HD, RW_HD), lambda b, c: (b, 0, 0, 0))
    sh_spec = pl.BlockSpec((bs, 1, RW_PROJ), lambda b, c: (b, 0, 0))
    in_specs = [pl.BlockSpec((c_len, bs * RW_PROJ), lambda b, c: (c, b)), sh_spec, st_spec] + [const(a) for a in prm]
    out_shape = (jax.ShapeDtypeStruct((t_len, nb * RW_WIDTH), BF16),
                 jax.ShapeDtypeStruct((nb, RW_HEADS, RW_HD, RW_HD), F32),
                 jax.ShapeDtypeStruct((nb, 1, RW_PROJ), F32))
    out_specs = (pl.BlockSpec((c_len, bs * RW_WIDTH), lambda b, c: (c, b)), st_spec, sh_spec)
    scratch = [pltpu.VMEM((bs, RW_PROJ), F32), pltpu.VMEM((bs, RW_HEADS, RW_HD, RW_HD), F32),
               pltpu.VMEM((bs * c_len, RW_WIDTH), F32)]
    h0 = jnp.swapaxes(s0, -1, -2)
    args = [p_tm.reshape(t_len, nb * RW_PROJ), shift.reshape(nb, 1, RW_PROJ), h0, *prm]
    kern = functools.partial(_rw_chunk_kernel, c_len=c_len, bs=bs)
    grid = (nb // bs, t_len // c_len)
    out_shape, out_specs = list(out_shape), list(out_specs)
    if side is not None:
        assert side.steps == grid[0] * grid[1]
        kern = _with_side(kern, len(args), 3, side)
        args += list(side.args)
        in_specs += side.in_specs(grid[1])
        out_shape += list(side.out_shape)
        out_specs += side.out_specs(grid[1])
    outs = pl.pallas_call(
        kern, grid=grid, in_specs=in_specs, out_specs=out_specs, out_shape=out_shape,
        scratch_shapes=scratch, compiler_params=_cparams("parallel", "arbitrary"), name="rwkv_prompt")(*args)
    y, h_fin, sh = outs[:3]
    res = (y.reshape(t_len, nb, RW_WIDTH), jnp.swapaxes(h_fin, -1, -2), sh.reshape(nb, RW_PROJ))
    return res if side is None else (res, outs[3:])


def _rw_step_prep_kernel(p_ref, shift_ref, mu_ref, w0_ref, w2_ref, a0_ref, a2_ref, g2_ref, kk_ref, ka_ref,
                         r_ref, w_ref, k_ref, v_ref, a_ref, b_ref, g_ref):
    prm = (mu_ref[...], w0_ref[...], w2_ref[...], a0_ref[...], a2_ref[...], g2_ref[...],
           kk_ref[...], ka_ref[...])
    r, lw, k, v, a, b, g = _rw_prep(p_ref[...], shift_ref[...], prm, _head_ones())
    r_ref[...] = r
    w_ref[...] = jnp.exp(lw)
    k_ref[...] = k
    v_ref[...] = v
    a_ref[...] = a
    b_ref[...] = b
    g_ref[...] = g


def _rw_step_core_kernel(s_ref, r_ref, w_ref, k_ref, a_ref, b_ref, v_ref, s_out_ref, o_ref):
    s = s_ref[...]
    sa = jnp.sum(s * a_ref[...], axis=-1, keepdims=True)
    s_new = s * w_ref[...] + sa * b_ref[...] + v_ref[...] * k_ref[...]
    s_out_ref[...] = s_new
    o_ref[...] = jnp.sum(s_new * r_ref[...], axis=-1, keepdims=True)


def _rw_step_post_kernel(o_ref, r_ref, k_ref, v_ref, g_ref, rk_ref, lnw_ref, lnb_ref, y_ref):
    y_ref[...] = _rw_post(o_ref[...], r_ref[...], k_ref[...], v_ref[...], g_ref[...],
                          rk_ref[...], lnw_ref[...], lnb_ref[...], _head_ones()).astype(y_ref.dtype)


def rwkv_step(p, shift, s0, params, *, bt=8):
    n = p.shape[0]
    prm = _rw_param_args(*params)
    vec = jax.ShapeDtypeStruct((n, RW_WIDTH), F32)
    r, w, k, v, a, b, g = pl.pallas_call(
        _rw_step_prep_kernel, out_shape=(vec,) * 7, name="rwkv_step_prep")(p, shift, *prm[:8])
    rows = lambda z: z.reshape(n, RW_HEADS, 1, RW_HD)
    row_spec = pl.BlockSpec((bt, RW_HEADS, 1, RW_HD), lambda i: (i, 0, 0, 0))
    col_spec = pl.BlockSpec((bt, RW_HEADS, RW_HD, 1), lambda i: (i, 0, 0, 0))
    st_spec = pl.BlockSpec((bt, RW_HEADS, RW_HD, RW_HD), lambda i: (i, 0, 0, 0))
    s_new, o = pl.pallas_call(
        _rw_step_core_kernel, grid=(n // bt,),
        in_specs=[st_spec] + [row_spec] * 5 + [col_spec], out_specs=(st_spec, col_spec),
        out_shape=(jax.ShapeDtypeStruct(s0.shape, F32), jax.ShapeDtypeStruct((n, RW_HEADS, RW_HD, 1), F32)),
        compiler_params=_cparams("parallel"), name="rwkv_step_core")(
            s0, rows(r), rows(w), rows(k), rows(a), rows(b), v.reshape(n, RW_HEADS, RW_HD, 1))
    y = pl.pallas_call(
        _rw_step_post_kernel, out_shape=jax.ShapeDtypeStruct((n, RW_WIDTH), BF16), name="rwkv_step_post")(
            o.reshape(n, RW_WIDTH), r, k, v, g, *prm[8:])
    return y, s_new


RET_LOG_G = tuple(math.log(1.0 - 2.0 ** (-5.0 - h)) for h in range(RET_HEADS))


def _rope_tables(pos, half):
    j = lax.broadcasted_iota(jnp.int32, (1, half), 1).astype(F32)
    inv = jnp.exp(j * (-math.log(ROPE_BASE) / half))
    ang = pos * inv
    return jnp.cos(ang), jnp.sin(ang)


def _rope(x, cos, sin):
    half = RET_DK // 2
    outs = []
    for h in range(RET_HEADS):
        x1 = x[:, h * RET_DK:h * RET_DK + half]
        x2 = x[:, h * RET_DK + half:(h + 1) * RET_DK]
        outs += [x1 * cos - x2 * sin, x1 * sin + x2 * cos]
    return jnp.concatenate(outs, axis=-1)


def _ret_norm_gate(o, g):
    o = o * lax.rsqrt(jnp.mean(o * o, axis=-1, keepdims=True) + NORM_EPS)
    return jax.nn.silu(g) * o


def _ret_tables_kernel(cos_ref, sin_ref, dmask_ref, qdec_ref, kdec_ref, *, c_len):
    t_len = cos_ref.shape[0]
    pos = lax.broadcasted_iota(jnp.int32, (t_len, 1), 0).astype(F32)
    cos, sin = _rope_tables(pos, RET_DK // 2)
    cos_ref[...] = cos
    sin_ref[...] = sin
    ti = lax.broadcasted_iota(jnp.int32, (c_len, 1), 0).astype(F32)
    ii = lax.broadcasted_iota(jnp.int32, (c_len, c_len), 0)
    jj = lax.broadcasted_iota(jnp.int32, (c_len, c_len), 1)
    diff = (ii - jj).astype(F32)
    for h in range(RET_HEADS):
        lg = RET_LOG_G[h]
        dmask_ref[h] = jnp.where(diff >= 0, jnp.exp(lg * jnp.maximum(diff, 0.0)), 0.0)
        qdec_ref[h] = jnp.exp(lg * (ti + 1.0))
        kdec_ref[h] = jnp.exp(lg * (c_len - 1.0 - ti))


def _ret_chunk_kernel(q_ref, k_ref, v_ref, g_ref, cos_ref, sin_ref, dmask_ref, qdec_ref, kdec_ref,
                      y_ref, sfin_ref, s_scr, *, c_len):
    c = pl.program_id(1)

    @pl.when(c == 0)
    def _():
        s_scr[...] = jnp.zeros_like(s_scr)

    cos, sin = cos_ref[...], sin_ref[...]
    q = _rope(q_ref[...].astype(F32), cos, sin)
    k = _rope(k_ref[...].astype(F32), cos, sin) * (RET_DK ** -0.5)
    for h in range(RET_HEADS):
        c_dec = math.exp(RET_LOG_G[h] * c_len)
        qh = q[:, h * RET_DK:(h + 1) * RET_DK]
        kh = k[:, h * RET_DK:(h + 1) * RET_DK]
        vh = v_ref[:, h * RET_DV:(h + 1) * RET_DV]
        s_h = s_scr[h]
        sc = _dot_nt(qh, kh) * dmask_ref[h]
        o = _bdot(sc, vh) + _bdot(qh * qdec_ref[h], s_h)
        s_scr[h] = s_h * c_dec + _dot_tn(kh * kdec_ref[h], vh)
        gh = g_ref[:, h * RET_DV:(h + 1) * RET_DV].astype(F32)
        y_ref[:, h * RET_DV:(h + 1) * RET_DV] = _ret_norm_gate(o, gh).astype(y_ref.dtype)

    @pl.when(c == pl.num_programs(1) - 1)
    def _():
        sfin_ref[0] = s_scr[...]


def retention_prompt(q, k, v, g, *, nb, c_len=RET_CHUNK):
    n = q.shape[0]
    t_len = n // nb
    nc = t_len // c_len
    half = RET_DK // 2
    tabs = pl.pallas_call(
        functools.partial(_ret_tables_kernel, c_len=c_len),
        out_shape=(jax.ShapeDtypeStruct((t_len, half), F32), jax.ShapeDtypeStruct((t_len, half), F32),
                   jax.ShapeDtypeStruct((RET_HEADS, c_len, c_len), F32),
                   jax.ShapeDtypeStruct((RET_HEADS, c_len, 1), F32),
                   jax.ShapeDtypeStruct((RET_HEADS, c_len, 1), F32)),
        name="retention_tables")()
    spec = lambda w: pl.BlockSpec((c_len, w), lambda b, c: (b * nc + c, 0))
    pos_spec = pl.BlockSpec((c_len, half), lambda b, c: (c, 0))
    const = lambda a: pl.BlockSpec(a.shape, lambda b, c: (0, 0, 0))
    st_spec = pl.BlockSpec((1, RET_HEADS, RET_DK, RET_DV), lambda b, c: (b, 0, 0, 0))
    return pl.pallas_call(
        functools.partial(_ret_chunk_kernel, c_len=c_len), grid=(nb, nc),
        in_specs=[spec(NQ), spec(NQ), spec(NV), spec(NV), pos_spec, pos_spec] + [const(a) for a in tabs[2:]],
        out_specs=(spec(NV), st_spec),
        out_shape=(jax.ShapeDtypeStruct((n, NV), BF16),
                   jax.ShapeDtypeStruct((nb, RET_HEADS, RET_DK, RET_DV), F32)),
        scratch_shapes=[pltpu.VMEM((RET_HEADS, RET_DK, RET_DV), F32)],
        compiler_params=_cparams("parallel", "arbitrary"), name="retention_prompt")(q, k, v, g, *tabs)


def _ret_step_rope_kernel(q_ref, k_ref, qo_ref, ko_ref, *, pos0):
    pos = jnp.full((q_ref.shape[0], 1), pos0, F32)
    cos, sin = _rope_tables(pos, RET_DK // 2)
    qo_ref[...] = _rope(q_ref[...].astype(F32), cos, sin)
    ko_ref[...] = _rope(k_ref[...].astype(F32), cos, sin) * (RET_DK ** -0.5)


def _ret_step_core_kernel(s_ref, q_ref, k_ref, v_ref, g_ref, s_out_ref, y_ref):
    for i in range(s_ref.shape[0]):
        for h in range(RET_HEADS):
            gam = math.exp(RET_LOG_G[h])
            s_h = s_ref[i, h]
            qc = q_ref[i, h]
            kc = k_ref[i, h]
            vr = v_ref[i, h].astype(F32)
            qk = jnp.sum(qc * kc, axis=0, keepdims=True)
            o = qk * vr + jnp.sum((qc * gam) * s_h, axis=0, keepdims=True)
            s_out_ref[i, h] = s_h * gam + kc * vr
            y_ref[i, h] = _ret_norm_gate(o, g_ref[i, h].astype(F32)).astype(y_ref.dtype)


def retention_step_job(q, k, v, g, s0, *, pos0, tb):
    n = q.shape[0]
    vec = jax.ShapeDtypeStruct((n, NQ), F32)
    qr, kr = pl.pallas_call(functools.partial(_ret_step_rope_kernel, pos0=pos0), out_shape=(vec, vec),
                            name="retention_step_rope")(q, k)
    col = lambda z: z.reshape(n, RET_HEADS, RET_DK, 1)
    row = lambda z: z.reshape(n, RET_HEADS, 1, RET_DV)

    def spec(rows, cols):
        return lambda inner: pl.BlockSpec((tb, RET_HEADS, rows, cols), lambda i, j: (i * inner + j, 0, 0, 0))

    st, cl, rw = spec(RET_DK, RET_DV), spec(RET_DK, 1), spec(1, RET_DV)
    return SideJob(
        body=_ret_step_core_kernel,
        args=(s0, col(qr), col(kr), row(v), row(g)),
        in_specs=lambda inner: [st(inner), cl(inner), cl(inner), rw(inner), rw(inner)],
        out_shape=(jax.ShapeDtypeStruct(s0.shape, F32), jax.ShapeDtypeStruct((n, RET_HEADS, 1, RET_DV), BF16)),
        out_specs=lambda inner: [st(inner), rw(inner)],
        steps=n // tb)


def _xattn_prompt_kernel(x_ref, g_ref, wq_ref, mk_ref, mv_ref, wo_ref, o_ref, att_scr):
    x = x_ref[...]
    q = jnp.dot(_rms(x, g_ref[...]).astype(BF16), wq_ref[...], preferred_element_type=F32)
    for h in range(MEM_HEADS):
        hs = slice(h * MEM_HD, (h + 1) * MEM_HD)
        s = _dot_nt(q[:, hs], mk_ref[0, :, hs]) * (MEM_HD ** -0.5)
        s = s - jnp.max(s, axis=-1, keepdims=True)
        e = jnp.exp(s)
        p = e / jnp.sum(e, axis=-1, keepdims=True)
        att_scr[:, hs] = _bdot(p, mv_ref[0, :, hs])
    o_ref[...] = x + jnp.dot(att_scr[...].astype(BF16), wo_ref[...], preferred_element_type=F32)


def xattn_prompt(x, gain, w_q, mem_k, mem_v, w_o, *, nb, tm=512):
    n = x.shape[0]
    tiles_per_b = n // nb // tm
    row = pl.BlockSpec((tm, D_MODEL), lambda i: (i, 0))
    wspec = pl.BlockSpec((D_MODEL, D_MODEL), lambda i: (0, 0))
    mspec = pl.BlockSpec((1, N_MEM, D_MODEL), lambda i: (i // tiles_per_b, 0, 0))
    return pl.pallas_call(
        _xattn_prompt_kernel, grid=(n // tm,),
        in_specs=[row, pl.BlockSpec((1, D_MODEL), lambda i: (0, 0)), wspec, mspec, mspec, wspec],
        out_specs=row, out_shape=jax.ShapeDtypeStruct((n, D_MODEL), F32),
        scratch_shapes=[pltpu.VMEM((tm, D_MODEL), F32)],
        compiler_params=_cparams("parallel"), name="xattn_prompt")(
            x, gain.reshape(1, D_MODEL), w_q, mem_k, mem_v, w_o)


def _xattn_step_kernel(q_ref, mk_ref, mv_ref, o_ref, *, tb):
    half = N_MEM // 2
    both = lambda z: jnp.concatenate([z, z], axis=1)
    fold = lambda z, op: op(z[:, :MEM_HEADS], z[:, MEM_HEADS:])
    for i in range(tb):
        k8 = jnp.concatenate([mk_ref[0, i, :half], mk_ref[0, i, half:]], axis=1)
        v8 = jnp.concatenate([mv_ref[0, i, :half], mv_ref[0, i, half:]], axis=1)
        q8 = jnp.concatenate([q_ref[i], q_ref[i]], axis=0)
        s = jnp.sum(k8 * q8[None], axis=-1, keepdims=True) * (MEM_HD ** -0.5)
        smax = both(fold(jnp.max(s, axis=0, keepdims=True), jnp.maximum))
        e = jnp.exp(s - smax)
        den = both(fold(jnp.sum(e, axis=0, keepdims=True), jnp.add))
        o8 = jnp.sum((e / den) * v8, axis=0)
        o_ref[i] = o8[:MEM_HEADS] + o8[MEM_HEADS:]


def xattn_step_job(q, cache_k, cache_v, layer, *, tb):
    n = q.shape[0]

    def specs(inner):
        qspec = pl.BlockSpec((tb, MEM_HEADS, MEM_HD), lambda i, j: (i * inner + j, 0, 0))
        cspec = pl.BlockSpec((1, tb, N_MEM, MEM_HEADS, MEM_HD), lambda i, j: (layer, i * inner + j, 0, 0, 0))
        return qspec, cspec

    return SideJob(
        body=functools.partial(_xattn_step_kernel, tb=tb),
        args=(q.reshape(n, MEM_HEADS, MEM_HD), cache_k, cache_v),
        in_specs=lambda inner: [specs(inner)[0], specs(inner)[1], specs(inner)[1]],
        out_shape=(jax.ShapeDtypeStruct((n, MEM_HEADS, MEM_HD), F32),),
        out_specs=lambda inner: [specs(inner)[0]],
        steps=n // tb)


def run_job(job, name):
    return pl.pallas_call(
        job.body, grid=(job.steps, 1), in_specs=job.in_specs(1), out_specs=job.out_specs(1),
        out_shape=list(job.out_shape), compiler_params=_cparams("parallel", "arbitrary"), name=name)(*job.args)


ROUTER_LANES = 128
NEG_BIG = -1e30


def _moe_gates(logits):
    lane = lax.broadcasted_iota(jnp.int32, logits.shape, 1)
    first = lambda mask: jnp.min(jnp.where(mask, lane, ROUTER_LANES), axis=-1, keepdims=True)
    is_c = lane < MOE_GROUPS
    lc = jnp.where(is_c, logits, NEG_BIG)
    mc = jnp.max(lc, axis=-1, keepdims=True)
    g_idx = first(lc == mc)
    p_g = 1.0 / jnp.sum(jnp.where(is_c, jnp.exp(lc - mc), 0.0), axis=-1, keepdims=True)
    fl = lane - MOE_GROUPS
    in_g = (fl >= 0) & (fl < MOE_EXPERTS) & ((fl // MOE_PER_GROUP) == g_idx)
    lf = jnp.where(in_g, logits, NEG_BIG)
    m1 = jnp.max(lf, axis=-1, keepdims=True)
    i1 = first(lf == m1)
    lf2 = jnp.where(lane == i1, NEG_BIG, lf)
    m2 = jnp.max(lf2, axis=-1, keepdims=True)
    i2 = first(lf2 == m2)
    e2 = jnp.exp(m2 - m1)
    w_top = 1.0 / (1.0 + e2)
    gate = p_g * (jnp.where(lane == i1, w_top, 0.0) + jnp.where(lane == i2, e2 * w_top, 0.0))
    return gate, g_idx


MOE_EPS = 2
MOE_STEPS = MOE_EXPERTS // MOE_EPS

SideJob = collections.namedtuple("SideJob", "body args in_specs out_shape out_specs steps")


def _with_side(main_kernel, n_in, n_out, side):
    ns_in, ns_out = len(side.args), len(side.out_shape)

    def kern(*refs):
        m_in = refs[:n_in]
        s_in = refs[n_in:n_in + ns_in]
        m_out = refs[n_in + ns_in:n_in + ns_in + n_out]
        s_out = refs[n_in + ns_in + n_out:n_in + ns_in + n_out + ns_out]
        scratch = refs[n_in + ns_in + n_out + ns_out:]
        main_kernel(*m_in, *m_out, *scratch, side=lambda: side.body(*s_in, *s_out))

    return kern


def _router_logits(h, wr_ref, br_ref):
    h_hi = h.astype(BF16)
    h_lo = (h - h_hi.astype(F32)).astype(BF16)
    acc = jnp.dot(h_hi, wr_ref[0], preferred_element_type=F32)
    acc = acc + jnp.dot(h_hi, wr_ref[1], preferred_element_type=F32)
    acc = acc + jnp.dot(h_lo, wr_ref[0], preferred_element_type=F32)
    return acc + br_ref[...]


def _experts_ffn(hb, gate, e0, w1_ref, w3_ref, w2_ref):
    lane = lax.broadcasted_iota(jnp.int32, gate.shape, 1)
    acc = None
    for e in range(MOE_EPS):
        a1 = jnp.dot(hb, w1_ref[e].astype(BF16), preferred_element_type=F32)
        a3 = jnp.dot(hb, w3_ref[e].astype(BF16), preferred_element_type=F32)
        ge = jnp.sum(jnp.where(lane == MOE_GROUPS + e0 + e, gate, 0.0), axis=-1, keepdims=True)
        hid = (jax.nn.silu(a1) * a3 * ge).astype(BF16)
        part = jnp.dot(hid, w2_ref[e].astype(BF16), preferred_element_type=F32)
        acc = part if acc is None else acc + part
    return acc


def _moe_kernel(x_ref, g_ref, wr_ref, br_ref, w1_ref, w3_ref, w2_ref, *rest, final_norm, side=None):
    if final_norm:
        fin_ref, o_ref, h_scr, gate_scr = rest
    else:
        o_ref, h_scr, gate_scr = rest
    step = pl.program_id(1)

    @pl.when(step == 0)
    def _():
        x = x_ref[...]
        h = _rms(x, g_ref[...])
        h_scr[...] = h.astype(BF16)
        gate_scr[...] = _moe_gates(_router_logits(h, wr_ref, br_ref))[0]
        o_ref[...] = x

    o_ref[...] += _experts_ffn(h_scr[...], gate_scr[...], step * MOE_EPS, w1_ref, w3_ref, w2_ref)
    if side is not None:
        side()

    if final_norm:
        @pl.when(step == MOE_STEPS - 1)
        def _():
            o_ref[...] = _rms(o_ref[...], fin_ref[...])


def moe_dense(x, gain, w_r, b_r, w1, w3, w2, layer, *, tm=512, final_gain=None, side=None):
    n = x.shape[0]
    tm = min(tm, n)
    gain = gain.reshape(1, D_MODEL)
    row = pl.BlockSpec((tm, D_MODEL), lambda i, s: (i, 0))
    const2 = lambda a: pl.BlockSpec(a.shape, lambda i, s: (0,) * a.ndim)
    soff = layer * MOE_STEPS
    wspec = pl.BlockSpec((MOE_EPS, D_MODEL, MOE_HIDDEN), lambda i, s: (soff + s, 0, 0))
    args = [x, gain, w_r, b_r, w1, w3, w2]
    in_specs = [row, const2(gain), const2(w_r), const2(b_r), wspec, wspec,
                pl.BlockSpec((MOE_EPS, MOE_HIDDEN, D_MODEL), lambda i, s: (soff + s, 0, 0))]
    if final_gain is not None:
        args.append(final_gain.reshape(1, D_MODEL))
        in_specs.append(const2(args[-1]))
    kern = functools.partial(_moe_kernel, final_norm=final_gain is not None)
    out_shape = [jax.ShapeDtypeStruct((n, D_MODEL), F32)]
    out_specs = [row]
    grid = (n // tm, MOE_STEPS)
    if side is not None:
        assert side.steps == grid[0] * grid[1]
        kern = _with_side(kern, len(args), 1, side)
        args += list(side.args)
        in_specs += side.in_specs(MOE_STEPS)
        out_shape += list(side.out_shape)
        out_specs += side.out_specs(MOE_STEPS)
    outs = pl.pallas_call(
        kern, grid=grid, in_specs=in_specs, out_specs=out_specs, out_shape=out_shape,
        scratch_shapes=[pltpu.VMEM((tm, D_MODEL), BF16), pltpu.VMEM((tm, ROUTER_LANES), F32)],
        compiler_params=_cparams("parallel", "arbitrary"), name="moe")(*args)
    return outs[0] if side is None else (outs[0], outs[1:])


def _group_weights(w1, w3, w2):
    ne = w1.shape[0] * MOE_EXPERTS
    return (w1.reshape(ne, D_MODEL, MOE_HIDDEN), w3.reshape(ne, D_MODEL, MOE_HIDDEN),
            w2.reshape(ne, MOE_HIDDEN, D_MODEL))


def _router_params(w_rc, b_rc, w_rf, b_rf):
    pad = ROUTER_LANES - MOE_GROUPS - MOE_EXPERTS
    w_r = jnp.concatenate([w_rc, w_rf, jnp.zeros((D_MODEL, pad), F32)], axis=1).astype(F32)
    b_r = jnp.concatenate([b_rc, b_rf, jnp.zeros((pad,), F32)]).reshape(1, ROUTER_LANES).astype(F32)
    w_hi = w_r.astype(BF16)
    w_lo = (w_r - w_hi.astype(F32)).astype(BF16)
    return jnp.stack([w_hi, w_lo]), b_r


def _forward(xp, xs, nbp, w, st, mem_k, mem_v, cache_k, cache_v):
    assert DEPTH == 2
    nbs = xs.shape[0]
    moe_tm = 1024
    moe_steps_p = (xp.shape[0] // moe_tm) * MOE_STEPS
    rwp = tuple(w[k][0] for k in ('rw_mu', 'rw_w0', 'rw_w2', 'rw_a0', 'rw_a2', 'rw_g2',
                                  'rw_k_k', 'rw_k_a', 'rw_r_k', 'rw_ln_w', 'rw_ln_b'))
    w_in0, w_out0 = w['w_in0_bf'][0], w['w_out0_bf'][0]
    moe = lambda x, layer, **kw: moe_dense(x, w['norm_ffn'][layer], *w['router'][layer], *w['moe_g'], layer, **kw)

    u, p_s = linear(xs, w_in0, gain=w['norm_mix'][0], splits=(S5_WIDTH, RW_PROJ))
    y_s5, s5r_s, s5i_s = s5_mixer(u.reshape(1, nbs, S5_WIDTH), st['s5_re'], st['s5_im'], w['s5p'][0],
                                  w['s5_d'][0], w['s5_w_glu'][0], tc=1)
    y_rw, rw_s = rwkv_step(p_s, st['shift'], st['rwkv'], rwp)
    xs = linear(y_s5.reshape(nbs, S5_WIDTH), w_out0[:S5_WIDTH], x2=y_rw, w2=w_out0[S5_WIDTH:], residual=xs)
    q_s = linear(xs, w['w_mq_bf'][0], gain=w['norm_mem'][0])

    zeros = lambda *shape: jnp.zeros(shape, F32)
    u, p_p = linear(xp, w_in0, gain=w['norm_mix'][0], splits=(S5_WIDTH, RW_PROJ), out_tmajor=True, batch=nbp)
    y_s5, s5r_p, s5i_p = s5_mixer(u, zeros(nbp, S5_STATE), zeros(nbp, S5_STATE), w['s5p'][0],
                                  w['s5_d'][0], w['s5_w_glu'][0], tc=128)
    rw_bs = 4
    rw_steps = (nbp // rw_bs) * (p_p.shape[0] // RW_HD)
    job = xattn_step_job(q_s, cache_k, cache_v, 0, tb=nbs // rw_steps)
    (y_rw, rw_p, sh_p), (att_s,) = rwkv_prompt(p_p, zeros(nbp, RW_PROJ), zeros(nbp, RW_HEADS, RW_HD, RW_HD),
                                               rwp, bs=rw_bs, side=job)
    xp = linear(y_s5, w_out0[:S5_WIDTH], x2=y_rw, w2=w_out0[S5_WIDTH:], residual=xp, x_tmajor=True)
    xp = xattn_prompt(xp, w['norm_mem'][0], w['w_mq_bf'][0], mem_k[0], mem_v[0], w['w_mo_bf'][0], nb=nbp)

    xs = linear(att_s.reshape(nbs, D_MODEL), w['w_mo_bf'][0], residual=xs)
    xs = moe(xs, 0)
    q, k, v, g = linear(xs, w['w_in1_bf'][0], gain=w['norm_mix'][1], out_dtype=BF16, splits=(NQ, NQ, NV, NV))
    job = retention_step_job(q, k, v, g, st['ret'], pos0=float(PAST_LEN), tb=nbs // moe_steps_p)
    xp, (ret_s, y_ret) = moe(xp, 0, tm=moe_tm, side=job)
    xs = linear(y_ret.reshape(nbs, NV), w['w_out1_bf'][0], residual=xs)
    q_s = linear(xs, w['w_mq_bf'][1], gain=w['norm_mem'][1])

    q, k, v, g = linear(xp, w['w_in1_bf'][0], gain=w['norm_mix'][1], out_dtype=BF16, splits=(NQ, NQ, NV, NV),
                        tm=256)
    y, ret_p = retention_prompt(q, k, v, g, nb=nbp)
    xp = linear(y, w['w_out1_bf'][0], residual=xp)
    xp = xattn_prompt(xp, w['norm_mem'][1], w['w_mq_bf'][1], mem_k[1], mem_v[1], w['w_mo_bf'][1], nb=nbp)
    job = xattn_step_job(q_s, cache_k, cache_v, 1, tb=nbs // moe_steps_p)
    y_p, (att_s,) = moe(xp, 1, tm=moe_tm, final_gain=w['norm_final'], side=job)
    xs = linear(att_s.reshape(nbs, D_MODEL), w['w_mo_bf'][1], residual=xs)
    y_s = moe(xs, 1, final_gain=w['norm_final'])

    grp = lambda z, nb: z.reshape(1, nb, S5_GROUPS, S5_N)
    prompt_out = (y_p, grp(s5r_p, nbp), grp(s5i_p, nbp), rw_p[None], sh_p[None], ret_p[None])
    sample_out = (y_s, grp(s5r_s, nbs), grp(s5i_s, nbs), rw_s[None], p_s[None], ret_s[None])
    return prompt_out, sample_out


def kernel(x_prompt, x_sample, mem_prompt, state_s5_re, state_s5_im, state_rwkv, state_shift, state_ret, cache_mem_k, cache_mem_v, norm_mix, norm_mem, norm_ffn, norm_final, w_in0, w_out0, s5_a_re, s5_a_im, s5_b_re, s5_b_im, s5_c_re, s5_c_im, s5_d, s5_log_dt, s5_w_glu, rw_mu, rw_w0, rw_w2, rw_a0, rw_a2, rw_g2, rw_k_k, rw_k_a, rw_r_k, rw_ln_w, rw_ln_b, w_in1, w_out1, mem_norm, w_mq, w_mk, w_mv, w_mo, moe_w_rc, moe_b_rc, moe_w_rf, moe_b_rf, moe_w1, moe_w3, moe_w2):
    w = dict(norm_mix=norm_mix, norm_mem=norm_mem, norm_ffn=norm_ffn, norm_final=norm_final,
             w_in0=w_in0, w_out0=w_out0, s5_a_re=s5_a_re, s5_a_im=s5_a_im, s5_b_re=s5_b_re, s5_b_im=s5_b_im,
             s5_c_re=s5_c_re, s5_c_im=s5_c_im, s5_d=s5_d, s5_log_dt=s5_log_dt, s5_w_glu=s5_w_glu,
             rw_mu=rw_mu, rw_w0=rw_w0, rw_w2=rw_w2, rw_a0=rw_a0, rw_a2=rw_a2, rw_g2=rw_g2,
             rw_k_k=rw_k_k, rw_k_a=rw_k_a, rw_r_k=rw_r_k, rw_ln_w=rw_ln_w, rw_ln_b=rw_ln_b,
             w_in1=w_in1, w_out1=w_out1, w_mq=w_mq, w_mo=w_mo,
             moe_w_rc=moe_w_rc, moe_b_rc=moe_b_rc, moe_w_rf=moe_w_rf, moe_b_rf=moe_b_rf,
             moe_w1=moe_w1, moe_w3=moe_w3, moe_w2=moe_w2)
    nbp, t_len, _ = x_prompt.shape
    nbs = x_sample.shape[0]
    n_even, n_odd = state_s5_re.shape[0], state_ret.shape[0]
    for name in ('w_in0', 'w_out0', 'w_in1', 'w_out1', 'w_mq', 'w_mo'):
        w[name + '_bf'] = w[name].astype(BF16)
    w['s5p'] = [_s5_params(s5_a_re[i], s5_a_im[i], s5_b_re[i], s5_b_im[i], s5_c_re[i], s5_c_im[i], s5_log_dt[i])
                for i in range(n_even)]
    w['router'] = [_router_params(moe_w_rc[l], moe_b_rc[l], moe_w_rf[l], moe_b_rf[l]) for l in range(DEPTH)]
    w['moe_g'] = _group_weights(moe_w1, moe_w3, moe_w2)

    mem = mem_prompt.reshape(nbp * N_MEM, D_MODEL)
    mem_k_l, mem_v_l, mem_k_h, mem_v_h = [], [], [], []
    for layer in range(DEPTH):
        w_kv = jnp.concatenate([w_mk[layer], w_mv[layer]], axis=1).astype(BF16)
        mk, mv, mk_h, mv_h = linear(mem, w_kv, gain=mem_norm[layer], splits=(D_MODEL, D_MODEL),
                                    head_copies=(MEM_HEADS, MEM_HD))
        mem_k_l.append(mk.reshape(nbp, N_MEM, D_MODEL))
        mem_v_l.append(mv.reshape(nbp, N_MEM, D_MODEL))
        mem_k_h.append(mk_h.reshape(nbp, N_MEM, MEM_HEADS, MEM_HD))
        mem_v_h.append(mv_h.reshape(nbp, N_MEM, MEM_HEADS, MEM_HD))
    mem_k_p = jnp.stack(mem_k_h)
    mem_v_p = jnp.stack(mem_v_h)

    assert n_even == 1 and n_odd == 1
    st = dict(s5_re=state_s5_re.reshape(nbs, S5_STATE), s5_im=state_s5_im.reshape(nbs, S5_STATE),
              rwkv=state_rwkv[0], shift=state_shift[0], ret=state_ret[0])
    (y_p, s5r_p, s5i_p, rw_p, sh_p, ret_p), (y_s, s5r_s, s5i_s, rw_s, sh_s, ret_s) = _forward(
        x_prompt.reshape(nbp * t_len, D_MODEL), x_sample.reshape(nbs, D_MODEL), nbp, w, st,
        mem_k_l, mem_v_l, cache_mem_k, cache_mem_v)
    return (y_p.reshape(nbp, t_len, D_MODEL), y_s.reshape(nbs, 1, D_MODEL),
            s5r_p, s5i_p, rw_p, sh_p, ret_p, mem_k_p, mem_v_p, s5r_s, s5i_s, rw_s, sh_s, ret_s)
```

```python
import collections
import functools
import math

import jax
import jax.numpy as jnp
from jax import lax
from jax.experimental import pallas as pl
from jax.experimental.pallas import tpu as pltpu

F32 = jnp.float32
BF16 = jnp.bfloat16

D_MODEL = 1024
DEPTH = 2
PAST_LEN = 16384
S5_WIDTH = 512
S5_GROUP = 16
S5_GROUPS = 32
S5_N = 64
S5_STATE = S5_GROUPS * S5_N
S5_GBLK = 8
RW_WIDTH = 512
RW_HD = 64
RW_HEADS = 8
RW_LORA = 256
RW_PROJ = 3 * RW_WIDTH + RW_LORA
IN0 = S5_WIDTH + RW_PROJ
RET_DK = 256
RET_HEADS = 4
RET_DV = 512
RET_CHUNK = 256
NQ = RET_HEADS * RET_DK
NV = RET_HEADS * RET_DV
IN1 = 2 * NQ + 2 * NV
N_MEM = 256
MEM_HEADS = 4
MEM_HD = 256
MOE_GROUPS = 4
MOE_PER_GROUP = 4
MOE_EXPERTS = 16
MOE_HIDDEN = 256
NORM_EPS = 1e-6
RW_GN_EPS = 64e-5
ROPE_BASE = 10000.0

VMEM_LIMIT = 56 * 1024 * 1024


def _cparams(*sem):
    return pltpu.CompilerParams(dimension_semantics=sem, vmem_limit_bytes=VMEM_LIMIT)


def _bdot(a, b):
    return jnp.dot(a.astype(BF16), b.astype(BF16), preferred_element_type=F32)


def _dot_nt(a, b):
    return lax.dot_general(a.astype(BF16), b.astype(BF16), (((1,), (1,)), ((), ())),
                           preferred_element_type=F32)


def _dot_tn(a, b):
    return lax.dot_general(a.astype(BF16), b.astype(BF16), (((0,), (0,)), ((), ())),
                           preferred_element_type=F32)


def _split3(x):
    hi = x.astype(BF16)
    r1 = x - hi.astype(F32)
    mid = r1.astype(BF16)
    lo = (r1 - mid.astype(F32)).astype(BF16)
    return hi, mid, lo


def _dot_exact_rhs(x, m_bf16, passes=3):
    hi, mid, lo = _split3(x)
    acc = jnp.dot(hi, m_bf16, preferred_element_type=F32)
    if passes > 1:
        acc = acc + jnp.dot(mid, m_bf16, preferred_element_type=F32)
    if passes > 2:
        acc = acc + jnp.dot(lo, m_bf16, preferred_element_type=F32)
    return acc


def _rms(x, g):
    ms = jnp.mean(x * x, axis=-1, keepdims=True)
    return x * lax.rsqrt(ms + NORM_EPS) * g


def _linear_kernel(*refs, norm, two, res):
    it = iter(refs)
    x_ref = next(it)
    g_ref = next(it) if norm else None
    w_ref = next(it)
    x2_ref = next(it) if two else None
    w2_ref = next(it) if two else None
    r_ref = next(it) if res else None
    o_refs = list(it)
    x = x_ref[...].astype(F32)
    if norm:
        x = _rms(x, g_ref[...])
    xb = x.astype(BF16)
    x2b = x2_ref[...].astype(BF16) if two else None
    col = 0
    for o_ref in o_refs:
        m = o_ref.shape[-1]
        step = next((s for s in (512, 256) if m % s == 0), m)
        for j in range(m // step):
            sl = slice(col + j * step, col + (j + 1) * step)
            acc = jnp.dot(xb, w_ref[:, sl], preferred_element_type=F32)
            if two:
                acc = acc + jnp.dot(x2b, w2_ref[:, sl], preferred_element_type=F32)
            if res:
                acc = acc + r_ref[:, sl]
            o_ref[:, j * step:(j + 1) * step] = acc.astype(o_ref.dtype)
        col += m


def _row_spec(tm, width, tmajor_b):
    if tmajor_b is None:
        return pl.BlockSpec((tm, width), lambda i: (i, 0))
    nb, tiles_per_b = tmajor_b
    return pl.BlockSpec((tm, width), lambda i: (i % tiles_per_b, i // tiles_per_b))


def linear(x, w, *, gain=None, x2=None, w2=None, residual=None, out_dtype=F32, tm=512,
           x_tmajor=False, out_tmajor=False, batch=None, splits=None, name="linear"):
    if x_tmajor:
        t_len, nb, k = x.shape
        n = t_len * nb
    else:
        n, k = x.shape
        nb = batch
        t_len = n // nb if nb else None
    m = w.shape[1]
    tm = min(tm, n if not (x_tmajor or out_tmajor) else t_len)
    assert n % tm == 0
    tiles_per_b = (t_len // tm) if (x_tmajor or out_tmajor) else None
    args, specs = [], []

    def add_rows(a, tmajor):
        width = a.shape[-1]
        args.append(a.reshape(t_len, nb * width) if tmajor else a)
        specs.append(_row_spec(tm, width, (nb, tiles_per_b) if tmajor else None))

    add_rows(x, x_tmajor)
    if gain is not None:
        args.append(gain.reshape(1, k).astype(F32))
        specs.append(pl.BlockSpec((1, k), lambda i: (0, 0)))
    args.append(w)
    specs.append(pl.BlockSpec(w.shape, lambda i: (0, 0)))
    if x2 is not None:
        add_rows(x2, x_tmajor)
        args.append(w2)
        specs.append(pl.BlockSpec(w2.shape, lambda i: (0, 0)))
    if residual is not None:
        add_rows(residual, False)
    widths = tuple(splits) if splits else (m,)
    assert sum(widths) == m
    if out_tmajor:
        out_shape = [jax.ShapeDtypeStruct((t_len, nb * mw), out_dtype) for mw in widths]
    else:
        out_shape = [jax.ShapeDtypeStruct((n, mw), out_dtype) for mw in widths]
    out_specs = [_row_spec(tm, mw, (nb, tiles_per_b) if out_tmajor else None) for mw in widths]
    kern = functools.partial(_linear_kernel, norm=gain is not None, two=x2 is not None,
                             res=residual is not None)
    outs = pl.pallas_call(
        kern, grid=(n // tm,), in_specs=specs, out_specs=out_specs, out_shape=out_shape,
        compiler_params=_cparams("parallel"), name=name)(*args)
    if out_tmajor:
        outs = [o.reshape(t_len, nb, mw) for o, mw in zip(outs, widths)]
    return outs if splits else outs[0]


def _s5_kernel(u_ref, h_re_ref, h_im_ref, abar_re_ref, abar_im_ref, bb_re_ref, bb_im_ref,
               cc_re_ref, cc_im_ref, d_ref, wglu_ref, y_ref, s_re_ref, s_im_ref,
               x_re, x_im, st_re, st_im, il_scr, *, tc, nb, flat):
    c = pl.program_id(0)
    nlb = S5_WIDTH // 128
    rows = tc * nb
    nblk = S5_GROUPS // S5_GBLK
    bw_in = S5_GBLK * S5_GROUP
    bw_st = S5_GBLK * S5_N

    @pl.when(c == 0)
    def _():
        st_re[...] = h_re_ref[...]
        st_im[...] = h_im_ref[...]

    if flat:
        for b in range(nb):
            for j in range(nlb):
                il_scr[j, pl.ds(b, tc, stride=nb), :] = u_ref[:, b * S5_WIDTH + j * 128:b * S5_WIDTH + (j + 1) * 128]
        u = jnp.concatenate([il_scr[j] for j in range(nlb)], axis=-1)
    else:
        u = u_ref[...].reshape(rows, S5_WIDTH)
    ub = u.astype(BF16)
    for gb in range(nblk):
        ui = ub[:, gb * bw_in:(gb + 1) * bw_in]
        x_re[:, gb * bw_st:(gb + 1) * bw_st] = jnp.dot(ui, bb_re_ref[gb], preferred_element_type=F32)
        x_im[:, gb * bw_st:(gb + 1) * bw_st] = jnp.dot(ui, bb_im_ref[gb], preferred_element_type=F32)

    lane_blk = 1024
    for lb in range(S5_STATE // lane_blk):
        sl = slice(lb * lane_blk, (lb + 1) * lane_blk)
        ar = jnp.broadcast_to(abar_re_ref[:, sl], (nb, lane_blk))
        ai = jnp.broadcast_to(abar_im_ref[:, sl], (nb, lane_blk))

        def body(t, carry, sl=sl, ar=ar, ai=ai):
            xr, xi = carry
            r0 = pl.multiple_of(t * nb, nb)
            br = x_re[pl.ds(r0, nb), sl]
            bi = x_im[pl.ds(r0, nb), sl]
            nr = ar * xr - ai * xi + br
            ni = ar * xi + ai * xr + bi
            x_re[pl.ds(r0, nb), sl] = nr
            x_im[pl.ds(r0, nb), sl] = ni
            return nr, ni

        fr, fi = lax.fori_loop(0, tc, body, (st_re[:, sl], st_im[:, sl]), unroll=min(tc, 4))
        st_re[:, sl] = fr
        st_im[:, sl] = fi

    for gb in range(nblk):
        xr = x_re[:, gb * bw_st:(gb + 1) * bw_st].astype(BF16)
        xi = x_im[:, gb * bw_st:(gb + 1) * bw_st].astype(BF16)
        yb = (jnp.dot(xr, cc_re_ref[gb], preferred_element_type=F32)
              - jnp.dot(xi, cc_im_ref[gb], preferred_element_type=F32))
        cs = slice(gb * bw_in, (gb + 1) * bw_in)
        yb = yb + d_ref[:, cs] * u[:, cs]
        x_re[:, cs] = jax.nn.gelu(yb)
    y = x_re[:, :S5_WIDTH]
    y = y * jax.nn.sigmoid(jnp.dot(y.astype(BF16), wglu_ref[...], preferred_element_type=F32))
    if flat:
        for j in range(nlb):
            il_scr[j] = y[:, j * 128:(j + 1) * 128]
        for b in range(nb):
            for j in range(nlb):
                y_ref[:, b * S5_WIDTH + j * 128:b * S5_WIDTH + (j + 1) * 128] = (
                    il_scr[j, pl.ds(b, tc, stride=nb), :].astype(y_ref.dtype))
    else:
        y_ref[...] = y.reshape(y_ref.shape).astype(y_ref.dtype)

    @pl.when(c == pl.num_programs(0) - 1)
    def _():
        s_re_ref[...] = st_re[...]
        s_im_ref[...] = st_im[...]


def _s5_params(a_re, a_im, b_re, b_im, c_re, c_im, log_dt):
    dt = jnp.exp(log_dt.astype(F32))[:, None]
    ar, ai = a_re.astype(F32), a_im.astype(F32)
    mag = jnp.exp(dt * ar)
    abar_re, abar_im = mag * jnp.cos(dt * ai), mag * jnp.sin(dt * ai)
    den = ar * ar + ai * ai
    nr = abar_re - 1.0
    coef_re = (nr * ar + abar_im * ai) / den
    coef_im = (abar_im * ar - nr * ai) / den
    cr, ci = coef_re[..., None], coef_im[..., None]
    brf, bif = b_re.astype(F32), b_im.astype(F32)
    bb_re = cr * brf - ci * bif
    bb_im = cr * bif + ci * brf
    nblk = S5_GROUPS // S5_GBLK
    eye = jnp.eye(S5_GBLK, dtype=F32)

    def blockdiag_in(bb):
        t = jnp.transpose(bb, (0, 2, 1)).reshape(nblk, S5_GBLK, S5_GROUP, S5_N)
        m = jnp.einsum('kgcn,gh->kgchn', t, eye)
        return m.reshape(nblk, S5_GBLK * S5_GROUP, S5_GBLK * S5_N).astype(BF16)

    def blockdiag_out(cc):
        t = jnp.transpose(cc.astype(F32), (0, 2, 1)).reshape(nblk, S5_GBLK, S5_N, S5_GROUP)
        m = jnp.einsum('khnc,hg->khngc', t, eye)
        return m.reshape(nblk, S5_GBLK * S5_N, S5_GBLK * S5_GROUP).astype(BF16)

    return (abar_re.reshape(1, S5_STATE), abar_im.reshape(1, S5_STATE),
            blockdiag_in(bb_re), blockdiag_in(bb_im), blockdiag_out(c_re), blockdiag_out(c_im))


def s5_mixer(u_tm, h_re, h_im, params, d_skip, w_glu, *, tc):
    t_len, nb, _ = u_tm.shape
    abar_re, abar_im, bb_re, bb_im, cc_re, cc_im = params
    tc = min(tc, t_len)
    assert t_len % tc == 0 and nb % 8 == 0
    rows = tc * nb
    flat = t_len > 1
    full = lambda a: pl.BlockSpec(a.shape, lambda c: (0,) * a.ndim)
    if flat:
        u_arg = u_tm.reshape(t_len, nb * S5_WIDTH)
        io_spec = pl.BlockSpec((tc, nb * S5_WIDTH), lambda c: (c, 0))
        y_shape = jax.ShapeDtypeStruct((t_len, nb * S5_WIDTH), BF16)
    else:
        u_arg = u_tm
        io_spec = pl.BlockSpec((tc, nb, S5_WIDTH), lambda c: (c, 0, 0))
        y_shape = jax.ShapeDtypeStruct((t_len, nb, S5_WIDTH), BF16)
    args = (u_arg, h_re, h_im, abar_re, abar_im, bb_re, bb_im, cc_re, cc_im,
            d_skip.reshape(1, S5_WIDTH).astype(F32), w_glu.astype(BF16))
    in_specs = [io_spec] + [full(a) for a in args[1:]]
    st_shape = jax.ShapeDtypeStruct((nb, S5_STATE), F32)
    st_spec = pl.BlockSpec((nb, S5_STATE), lambda c: (0, 0))
    scratch = [pltpu.VMEM((rows, S5_STATE), F32), pltpu.VMEM((rows, S5_STATE), F32),
               pltpu.VMEM((nb, S5_STATE), F32), pltpu.VMEM((nb, S5_STATE), F32),
               pltpu.VMEM((S5_WIDTH // 128, rows if flat else 8, 128), F32)]
    y, s_re, s_im = pl.pallas_call(
        functools.partial(_s5_kernel, tc=tc, nb=nb, flat=flat), grid=(t_len // tc,), in_specs=in_specs,
        out_specs=(io_spec, st_spec, st_spec), out_shape=(y_shape, st_shape, st_shape),
        scratch_shapes=scratch, compiler_params=_cparams("arbitrary"), name="s5_mixer")(*args)
    return y.reshape(t_len, nb, S5_WIDTH), s_re, s_im


def _head_ones():
    i = lax.broadcasted_iota(jnp.int32, (RW_WIDTH, RW_WIDTH), 0) // RW_HD
    j = lax.broadcasted_iota(jnp.int32, (RW_WIDTH, RW_WIDTH), 1) // RW_HD
    return jnp.where(i == j, 1.0, 0.0).astype(BF16)


def _softplus(z):
    return jnp.maximum(z, 0.0) + jnp.log1p(jnp.exp(-jnp.abs(z)))


def _rw_prep(p, p_prev, prm, ones_bd):
    mu, w0, w2, a0, a2, g2, k_k, k_a = prm
    xm = p + (p_prev - p) * mu
    o1, o2, o3 = RW_WIDTH, 2 * RW_WIDTH, 3 * RW_WIDTH
    r, k, v = xm[:, :o1], xm[:, o1:o2], xm[:, o2:o3]
    wd, ad, gd = xm[:, o3:o3 + 64], xm[:, o3 + 64:o3 + 128], xm[:, o3 + 128:]
    w = -_softplus(-(w0 + _bdot(jnp.tanh(wd), w2))) - 0.5
    lw = -jnp.exp(w)
    a = jax.nn.sigmoid(a0 + _bdot(ad, a2))
    g = _bdot(jax.nn.sigmoid(gd), g2)
    kk = k * k_k
    ss = _dot_exact_rhs(kk * kk, ones_bd, passes=1)
    kk = kk / jnp.maximum(jnp.sqrt(ss), 1e-12)
    k = k * (1.0 + (a - 1.0) * k_a)
    return r, lw, k, v, -kk, kk * a, g


def _rw_post(o, r, k, v, g, r_k, ln_w, ln_b, ones_bd):
    inv = 1.0 / RW_HD
    mean = _dot_exact_rhs(o, ones_bd, passes=2) * inv
    d = o - mean
    var = _dot_exact_rhs(d * d, ones_bd, passes=1) * inv
    on = d * lax.rsqrt(var + RW_GN_EPS) * ln_w + ln_b
    bonus = _dot_exact_rhs(r * k * r_k, ones_bd, passes=1) * v
    return (on + bonus) * g


def _rw_chunk_kernel(p_ref, shift_ref, h0_ref, mu_ref, w0_ref, w2_ref, a0_ref, a2_ref, g2_ref,
                     kk_ref, ka_ref, rk_ref, lnw_ref, lnb_ref,
                     y_ref, hfin_ref, shout_ref, prev_scr, h_scr, o_scr, *, c_len, bs, side=None):
    c = pl.program_id(1)
    nc = pl.num_programs(1)
    cl = c_len

    @pl.when(c == 0)
    def _():
        prev_scr[...] = shift_ref[:, 0, :]
        h_scr[...] = h0_ref[...]

    ones_bd = _head_ones()
    row = lax.broadcasted_iota(jnp.int32, (cl, RW_PROJ), 0)
    ps, pprevs = [], []
    for bi in range(bs):
        p = p_ref[:, bi * RW_PROJ:(bi + 1) * RW_PROJ]
        pprevs.append(jnp.where(row == 0, prev_scr[bi:bi + 1, :], pltpu.roll(p, 1, 0)))
        ps.append(p)
    p_all = jnp.concatenate(ps, axis=0) if bs > 1 else ps[0]
    pprev_all = jnp.concatenate(pprevs, axis=0) if bs > 1 else pprevs[0]
    prm = (mu_ref[...], w0_ref[...], w2_ref[...], a0_ref[...], a2_ref[...], g2_ref[...],
           kk_ref[...], ka_ref[...])
    r, lw, k, v, a, b, g = _rw_prep(p_all, pprev_all, prm, ones_bd)

    ti = lax.broadcasted_iota(jnp.int32, (cl, cl), 0)
    si = lax.broadcasted_iota(jnp.int32, (cl, cl), 1)
    lmat = jnp.where(ti >= si, 1.0, 0.0).astype(BF16)
    eye = jnp.where(ti == si, 1.0, 0.0)
    mi = lax.broadcasted_iota(jnp.int32, (2 * cl, 3 * cl), 0)
    mj = lax.broadcasted_iota(jnp.int32, (2 * cl, 3 * cl), 1)
    t_row = jnp.where(mi >= cl, mi - cl, mi)
    s_col = jnp.where(mj < cl, mj, jnp.where(mj >= 2 * cl, mj - 2 * cl, -4 * cl))
    keep = (t_row - s_col) >= jnp.where(mi >= cl, 0, 1)
    eye_bf = eye.astype(BF16)

    lhs_l, rhs_l, vh_l, hcat_l, kb_l, etot_l = [], [], [], [], [], []
    for bi in range(bs):
        rs = slice(bi * cl, (bi + 1) * cl)
        lw_b = lw[rs]
        l_hi, l_mid, l_lo = _split3(lw_b)
        cum = (jnp.dot(lmat, l_hi, preferred_element_type=F32)
               + jnp.dot(lmat, l_mid, preferred_element_type=F32)
               + jnp.dot(lmat, l_lo, preferred_element_type=F32))
        tot = cum[cl - 1:cl, :]
        e_neg = jnp.exp(-cum)
        e_rem = jnp.exp(tot - cum)
        at = (a[rs] * jnp.exp(cum - lw_b)).astype(BF16)
        rt = (r[rs] * jnp.exp(cum)).astype(BF16)
        bt = (b[rs] * e_neg).astype(BF16)
        kt = (k[rs] * e_neg).astype(BF16)
        bh = (b[rs] * e_rem).astype(BF16)
        kh = (k[rs] * e_rem).astype(BF16)
        e_tot = jnp.exp(tot)
        vb = v[rs].astype(BF16)
        for h in range(RW_HEADS):
            hs = slice(h * RW_HD, (h + 1) * RW_HD)
            lhs_l.append(jnp.concatenate([at[:, hs], rt[:, hs]], axis=0))
            rhs_l.append(jnp.concatenate([kt[:, hs], eye_bf, bt[:, hs]], axis=0))
            vh_l.append(vb[:, hs])
            kb_l.append(jnp.concatenate([kh[:, hs], bh[:, hs]], axis=0))
            etot_l.append(jnp.sum(eye * e_tot[:, hs], axis=-1, keepdims=True))
            hcat_l.append(h_scr[bi, h])

    nitem = bs * RW_HEADS
    items = range(nitem)
    aa_l = [jnp.where(keep, _dot_nt(lhs_l[i], rhs_l[i]), 0.0).astype(BF16) for i in items]
    pw_l = [aa_l[i][:cl, 2 * cl:] for i in items]
    tinv_l = [eye_bf + pw_l[i] for i in items]
    for _ in range(int(math.log2(cl)) - 1):
        pw_l = [jnp.dot(pw_l[i], pw_l[i], preferred_element_type=F32).astype(BF16) for i in items]
        tinv_l = [jnp.dot(tinv_l[i], eye_bf + pw_l[i], preferred_element_type=F32).astype(BF16) for i in items]
    vh_cat = [jnp.concatenate([vh_l[i], hcat_l[i].astype(BF16)], axis=0) for i in items]
    x1_l = [jnp.dot(aa_l[i][:cl, :2 * cl], vh_cat[i], preferred_element_type=F32).astype(BF16) for i in items]
    u_l = [jnp.dot(tinv_l[i], x1_l[i], preferred_element_type=F32).astype(BF16) for i in items]
    o_l = [jnp.dot(aa_l[i][cl:, :], jnp.concatenate([vh_cat[i], u_l[i]], axis=0),
                   preferred_element_type=F32) for i in items]
    hn_l = [hcat_l[i] * etot_l[i]
            + lax.dot_general(kb_l[i], jnp.concatenate([vh_l[i], u_l[i]], axis=0), (((0,), (0,)), ((), ())),
                              preferred_element_type=F32) for i in items]

    for bi in range(bs):
        for h in range(RW_HEADS):
            i = bi * RW_HEADS + h
            o_scr[bi * cl:(bi + 1) * cl, h * RW_HD:(h + 1) * RW_HD] = o_l[i]
            h_scr[bi, h] = hn_l[i]
        prev_scr[bi:bi + 1, :] = ps[bi][cl - 1:cl, :]

    y = _rw_post(o_scr[...], r, k, v, g, rk_ref[...], lnw_ref[...], lnb_ref[...], ones_bd)
    for bi in range(bs):
        y_ref[:, bi * RW_WIDTH:(bi + 1) * RW_WIDTH] = y[bi * cl:(bi + 1) * cl].astype(y_ref.dtype)
    if side is not None:
        side()

    @pl.when(c == nc - 1)
    def _():
        hfin_ref[...] = h_scr[...]
        for bi in range(bs):
            shout_ref[bi] = ps[bi][cl - 1:cl, :]


def _rw_param_args(mu, w0, w2, a0, a2, g2, k_k, k_a, r_k, ln_w, ln_b):
    row = lambda z: z.reshape(1, -1).astype(F32)
    return (row(mu), row(w0), w2.astype(BF16), row(a0), a2.astype(BF16), g2.astype(BF16),
            row(k_k), row(k_a), row(r_k), row(ln_w), row(ln_b))


def rwkv_prompt(p_tm, shift, s0, params, *, bs=4, side=None):
    c_len = RW_HD
    t_len, nb, _ = p_tm.shape
    assert t_len % c_len == 0 and nb % bs == 0
    prm = _rw_param_args(*params)
    const = lambda a: pl.BlockSpec(a.shape, lambda b, c: (0,) * a.ndim)
    st_spec = pl.BlockSpec((bs, RW_HEADS, RW_HD, RW_HD), lambda b, c: (b, 0, 0, 0))
    sh_spec = pl.BlockSpec((bs, 1, RW_PROJ), lambda b, c: (b, 0, 0))
    in_specs = [pl.BlockSpec((c_len, bs * RW_PROJ), lambda b, c: (c, b)), sh_spec, st_spec] + [const(a) for a in prm]
    out_shape = (jax.ShapeDtypeStruct((t_len, nb * RW_WIDTH), BF16),
                 jax.ShapeDtypeStruct((nb, RW_HEADS, RW_HD, RW_HD), F32),
                 jax.ShapeDtypeStruct((nb, 1, RW_PROJ), F32))
    out_specs = (pl.BlockSpec((c_len, bs * RW_WIDTH), lambda b, c: (c, b)), st_spec, sh_spec)
    scratch = [pltpu.VMEM((bs, RW_PROJ), F32), pltpu.VMEM((bs, RW_HEADS, RW_HD, RW_HD), F32),
               pltpu.VMEM((bs * c_len, RW_WIDTH), F32)]
    h0 = jnp.swapaxes(s0, -1, -2)
    args = [p_tm.reshape(t_len, nb * RW_PROJ), shift.reshape(nb, 1, RW_PROJ), h0, *prm]
    kern = functools.partial(_rw_chunk_kernel, c_len=c_len, bs=bs)
    grid = (nb // bs, t_len // c_len)
    out_shape, out_specs = list(out_shape), list(out_specs)
    if side is not None:
        assert side.steps == grid[0] * grid[1]
        kern = _with_side(kern, len(args), 3, side)
        args += list(side.args)
        in_specs += side.in_specs(grid[1])
        out_shape += list(side.out_shape)
        out_specs += side.out_specs(grid[1])
    outs = pl.pallas_call(
        kern, grid=grid, in_specs=in_specs, out_specs=out_specs, out_shape=out_shape,
        scratch_shapes=scratch, compiler_params=_cparams("parallel", "arbitrary"), name="rwkv_prompt")(*args)
    y, h_fin, sh = outs[:3]
    res = (y.reshape(t_len, nb, RW_WIDTH), jnp.swapaxes(h_fin, -1, -2), sh.reshape(nb, RW_PROJ))
    return res if side is None else (res, outs[3:])


def _rw_step_prep_kernel(p_ref, shift_ref, mu_ref, w0_ref, w2_ref, a0_ref, a2_ref, g2_ref, kk_ref, ka_ref,
                         r_ref, k_ref, v_ref, g_ref, rt_ref, wt_ref, kt_ref, at_ref, bt_ref, vt_ref):
    prm = (mu_ref[...], w0_ref[...], w2_ref[...], a0_ref[...], a2_ref[...], g2_ref[...],
           kk_ref[...], ka_ref[...])
    r, lw, k, v, a, b, g = _rw_prep(p_ref[...], shift_ref[...], prm, _head_ones())
    r_ref[...] = r
    k_ref[...] = k
    v_ref[...] = v
    g_ref[...] = g
    rt_ref[...] = r.T
    wt_ref[...] = jnp.exp(lw).T
    kt_ref[...] = k.T
    at_ref[...] = a.T
    bt_ref[...] = b.T
    vt_ref[...] = v.T


def _rw_step_core_kernel(s_ref, r_ref, w_ref, k_ref, a_ref, b_ref, v_ref, s_out_ref, o_ref):
    r, w, k, a, b = r_ref[0], w_ref[0], k_ref[0], a_ref[0], b_ref[0]
    for j in range(s_ref.shape[1]):
        s = s_ref[0, j]
        sa = jnp.sum(s * a, axis=0, keepdims=True)
        s_new = s * w + sa * b + v_ref[0, j:j + 1, :] * k
        s_out_ref[0, j] = s_new
        o_ref[0, j:j + 1, :] = jnp.sum(s_new * r, axis=0, keepdims=True)


def _rw_step_post_kernel(ot_ref, r_ref, k_ref, v_ref, g_ref, rk_ref, lnw_ref, lnb_ref, y_ref):
    y_ref[...] = _rw_post(ot_ref[...].T, r_ref[...], k_ref[...], v_ref[...], g_ref[...],
                          rk_ref[...], lnw_ref[...], lnb_ref[...], _head_ones()).astype(y_ref.dtype)


def rwkv_step(p, shift, s0, params, *, vb=8):
    n = p.shape[0]
    prm = _rw_param_args(*params)
    vec = jax.ShapeDtypeStruct((n, RW_WIDTH), F32)
    vec_t = jax.ShapeDtypeStruct((RW_WIDTH, n), F32)
    r, k, v, g, rt, wt, kt, at, bt, vt = pl.pallas_call(
        _rw_step_prep_kernel, out_shape=(vec,) * 4 + (vec_t,) * 6, name="rwkv_step_prep")(p, shift, *prm[:8])
    heads = lambda z: z.reshape(RW_HEADS, RW_HD, n)
    k_spec = pl.BlockSpec((1, RW_HD, n), lambda h, j: (h, 0, 0))
    v_spec = pl.BlockSpec((1, vb, n), lambda h, j: (h, j, 0))
    st_spec = pl.BlockSpec((1, vb, RW_HD, n), lambda h, j: (h, j, 0, 0))
    st = jnp.transpose(s0, (1, 2, 3, 0))
    s_new, ot = pl.pallas_call(
        _rw_step_core_kernel, grid=(RW_HEADS, RW_HD // vb),
        in_specs=[st_spec] + [k_spec] * 5 + [v_spec], out_specs=(st_spec, v_spec),
        out_shape=(jax.ShapeDtypeStruct(st.shape, F32), jax.ShapeDtypeStruct((RW_HEADS, RW_HD, n), F32)),
        compiler_params=_cparams("parallel", "parallel"), name="rwkv_step_core")(
            st, heads(rt), heads(wt), heads(kt), heads(at), heads(bt), heads(vt))
    y = pl.pallas_call(
        _rw_step_post_kernel, out_shape=jax.ShapeDtypeStruct((n, RW_WIDTH), BF16), name="rwkv_step_post")(
            ot.reshape(RW_WIDTH, n), r, k, v, g, *prm[8:])
    return y, jnp.transpose(s_new, (3, 0, 1, 2))


RET_LOG_G = tuple(math.log(1.0 - 2.0 ** (-5.0 - h)) for h in range(RET_HEADS))


def _rope_tables(pos, half):
    j = lax.broadcasted_iota(jnp.int32, (1, half), 1).astype(F32)
    inv = jnp.exp(j * (-math.log(ROPE_BASE) / half))
    ang = pos * inv
    return jnp.cos(ang), jnp.sin(ang)


def _rope(x, cos, sin):
    half = RET_DK // 2
    outs = []
    for h in range(RET_HEADS):
        x1 = x[:, h * RET_DK:h * RET_DK + half]
        x2 = x[:, h * RET_DK + half:(h + 1) * RET_DK]
        outs += [x1 * cos - x2 * sin, x1 * sin + x2 * cos]
    return jnp.concatenate(outs, axis=-1)


def _ret_norm_gate(o, g):
    o = o * lax.rsqrt(jnp.mean(o * o, axis=-1, keepdims=True) + NORM_EPS)
    return jax.nn.silu(g) * o


def _ret_tables_kernel(cos_ref, sin_ref, dmask_ref, qdec_ref, kdec_ref, *, c_len):
    t_len = cos_ref.shape[0]
    pos = lax.broadcasted_iota(jnp.int32, (t_len, 1), 0).astype(F32)
    cos, sin = _rope_tables(pos, RET_DK // 2)
    cos_ref[...] = cos
    sin_ref[...] = sin
    ti = lax.broadcasted_iota(jnp.int32, (c_len, 1), 0).astype(F32)
    ii = lax.broadcasted_iota(jnp.int32, (c_len, c_len), 0)
    jj = lax.broadcasted_iota(jnp.int32, (c_len, c_len), 1)
    diff = (ii - jj).astype(F32)
    for h in range(RET_HEADS):
        lg = RET_LOG_G[h]
        dmask_ref[h] = jnp.where(diff >= 0, jnp.exp(lg * jnp.maximum(diff, 0.0)), 0.0)
        qdec_ref[h] = jnp.exp(lg * (ti + 1.0))
        kdec_ref[h] = jnp.exp(lg * (c_len - 1.0 - ti))


def _ret_chunk_kernel(q_ref, k_ref, v_ref, g_ref, cos_ref, sin_ref, dmask_ref, qdec_ref, kdec_ref,
                      y_ref, sfin_ref, s_scr, *, c_len):
    c = pl.program_id(1)

    @pl.when(c == 0)
    def _():
        s_scr[...] = jnp.zeros_like(s_scr)

    cos, sin = cos_ref[...], sin_ref[...]
    q = _rope(q_ref[...].astype(F32), cos, sin)
    k = _rope(k_ref[...].astype(F32), cos, sin) * (RET_DK ** -0.5)
    for h in range(RET_HEADS):
        c_dec = math.exp(RET_LOG_G[h] * c_len)
        qh = q[:, h * RET_DK:(h + 1) * RET_DK]
        kh = k[:, h * RET_DK:(h + 1) * RET_DK]
        vh = v_ref[:, h * RET_DV:(h + 1) * RET_DV]
        s_h = s_scr[h]
        sc = _dot_nt(qh, kh) * dmask_ref[h]
        o = _bdot(sc, vh) + _bdot(qh * qdec_ref[h], s_h)
        s_scr[h] = s_h * c_dec + _dot_tn(kh * kdec_ref[h], vh)
        gh = g_ref[:, h * RET_DV:(h + 1) * RET_DV].astype(F32)
        y_ref[:, h * RET_DV:(h + 1) * RET_DV] = _ret_norm_gate(o, gh).astype(y_ref.dtype)

    @pl.when(c == pl.num_programs(1) - 1)
    def _():
        sfin_ref[0] = s_scr[...]


def retention_prompt(q, k, v, g, *, nb, c_len=RET_CHUNK):
    n = q.shape[0]
    t_len = n // nb
    nc = t_len // c_len
    half = RET_DK // 2
    tabs = pl.pallas_call(
        functools.partial(_ret_tables_kernel, c_len=c_len),
        out_shape=(jax.ShapeDtypeStruct((t_len, half), F32), jax.ShapeDtypeStruct((t_len, half), F32),
                   jax.ShapeDtypeStruct((RET_HEADS, c_len, c_len), F32),
                   jax.ShapeDtypeStruct((RET_HEADS, c_len, 1), F32),
                   jax.ShapeDtypeStruct((RET_HEADS, c_len, 1), F32)),
        name="retention_tables")()
    spec = lambda w: pl.BlockSpec((c_len, w), lambda b, c: (b * nc + c, 0))
    pos_spec = pl.BlockSpec((c_len, half), lambda b, c: (c, 0))
    const = lambda a: pl.BlockSpec(a.shape, lambda b, c: (0, 0, 0))
    st_spec = pl.BlockSpec((1, RET_HEADS, RET_DK, RET_DV), lambda b, c: (b, 0, 0, 0))
    return pl.pallas_call(
        functools.partial(_ret_chunk_kernel, c_len=c_len), grid=(nb, nc),
        in_specs=[spec(NQ), spec(NQ), spec(NV), spec(NV), pos_spec, pos_spec] + [const(a) for a in tabs[2:]],
        out_specs=(spec(NV), st_spec),
        out_shape=(jax.ShapeDtypeStruct((n, NV), BF16),
                   jax.ShapeDtypeStruct((nb, RET_HEADS, RET_DK, RET_DV), F32)),
        scratch_shapes=[pltpu.VMEM((RET_HEADS, RET_DK, RET_DV), F32)],
        compiler_params=_cparams("parallel", "arbitrary"), name="retention_prompt")(q, k, v, g, *tabs)


def _ret_step_rope_kernel(q_ref, k_ref, qo_ref, ko_ref, *, pos0):
    pos = jnp.full((q_ref.shape[0], 1), pos0, F32)
    cos, sin = _rope_tables(pos, RET_DK // 2)
    qo_ref[...] = _rope(q_ref[...].astype(F32), cos, sin)
    ko_ref[...] = _rope(k_ref[...].astype(F32), cos, sin) * (RET_DK ** -0.5)


def _ret_step_core_kernel(s_ref, q_ref, k_ref, v_ref, g_ref, s_out_ref, y_ref):
    for i in range(s_ref.shape[0]):
        for h in range(RET_HEADS):
            gam = math.exp(RET_LOG_G[h])
            s_h = s_ref[i, h]
            qc = q_ref[i, h]
            kc = k_ref[i, h]
            vr = v_ref[i, h].astype(F32)
            qk = jnp.sum(qc * kc, axis=0, keepdims=True)
            o = qk * vr + jnp.sum((qc * gam) * s_h, axis=0, keepdims=True)
            s_out_ref[i, h] = s_h * gam + kc * vr
            y_ref[i, h] = _ret_norm_gate(o, g_ref[i, h].astype(F32)).astype(y_ref.dtype)


def retention_step_job(q, k, v, g, s0, *, pos0, tb):
    n = q.shape[0]
    vec = jax.ShapeDtypeStruct((n, NQ), F32)
    qr, kr = pl.pallas_call(functools.partial(_ret_step_rope_kernel, pos0=pos0), out_shape=(vec, vec),
                            name="retention_step_rope")(q, k)
    col = lambda z: z.reshape(n, RET_HEADS, RET_DK, 1)
    row = lambda z: z.reshape(n, RET_HEADS, 1, RET_DV)

    def spec(rows, cols):
        return lambda inner: pl.BlockSpec((tb, RET_HEADS, rows, cols), lambda i, j: (i * inner + j, 0, 0, 0))

    st, cl, rw = spec(RET_DK, RET_DV), spec(RET_DK, 1), spec(1, RET_DV)
    return SideJob(
        body=_ret_step_core_kernel,
        args=(s0, col(qr), col(kr), row(v), row(g)),
        in_specs=lambda inner: [st(inner), cl(inner), cl(inner), rw(inner), rw(inner)],
        out_shape=(jax.ShapeDtypeStruct(s0.shape, F32), jax.ShapeDtypeStruct((n, RET_HEADS, 1, RET_DV), BF16)),
        out_specs=lambda inner: [st(inner), rw(inner)],
        steps=n // tb)


def _xattn_prompt_kernel(x_ref, g_ref, wq_ref, mk_ref, mv_ref, wo_ref, o_ref, att_scr):
    x = x_ref[...]
    q = jnp.dot(_rms(x, g_ref[...]).astype(BF16), wq_ref[...], preferred_element_type=F32)
    for h in range(MEM_HEADS):
        hs = slice(h * MEM_HD, (h + 1) * MEM_HD)
        s = _dot_nt(q[:, hs], mk_ref[0, :, hs]) * (MEM_HD ** -0.5)
        s = s - jnp.max(s, axis=-1, keepdims=True)
        e = jnp.exp(s)
        p = e / jnp.sum(e, axis=-1, keepdims=True)
        att_scr[:, hs] = _bdot(p, mv_ref[0, :, hs])
    o_ref[...] = x + jnp.dot(att_scr[...].astype(BF16), wo_ref[...], preferred_element_type=F32)


def xattn_prompt(x, gain, w_q, mem_k, mem_v, w_o, layer, *, nb, tm=512):
    n = x.shape[0]
    tiles_per_b = n // nb // tm
    mem_k = mem_k.reshape(-1, N_MEM, D_MODEL)
    mem_v = mem_v.reshape(-1, N_MEM, D_MODEL)
    row = pl.BlockSpec((tm, D_MODEL), lambda i: (i, 0))
    wspec = pl.BlockSpec((D_MODEL, D_MODEL), lambda i: (0, 0))
    mspec = pl.BlockSpec((1, N_MEM, D_MODEL), lambda i: (layer * nb + i // tiles_per_b, 0, 0))
    return pl.pallas_call(
        _xattn_prompt_kernel, grid=(n // tm,),
        in_specs=[row, pl.BlockSpec((1, D_MODEL), lambda i: (0, 0)), wspec, mspec, mspec, wspec],
        out_specs=row, out_shape=jax.ShapeDtypeStruct((n, D_MODEL), F32),
        scratch_shapes=[pltpu.VMEM((tm, D_MODEL), F32)],
        compiler_params=_cparams("parallel"), name="xattn_prompt")(
            x, gain.reshape(1, D_MODEL), w_q, mem_k, mem_v, w_o)


def _xattn_step_kernel(q_ref, mk_ref, mv_ref, o_ref, *, tb):
    half = N_MEM // 2
    both = lambda z: jnp.concatenate([z, z], axis=1)
    fold = lambda z, op: op(z[:, :MEM_HEADS], z[:, MEM_HEADS:])
    for i in range(tb):
        k8 = jnp.concatenate([mk_ref[0, i, :half], mk_ref[0, i, half:]], axis=1)
        v8 = jnp.concatenate([mv_ref[0, i, :half], mv_ref[0, i, half:]], axis=1)
        q8 = jnp.concatenate([q_ref[i], q_ref[i]], axis=0)
        s = jnp.sum(k8 * q8[None], axis=-1, keepdims=True) * (MEM_HD ** -0.5)
        smax = both(fold(jnp.max(s, axis=0, keepdims=True), jnp.maximum))
        e = jnp.exp(s - smax)
        den = both(fold(jnp.sum(e, axis=0, keepdims=True), jnp.add))
        o8 = jnp.sum((e / den) * v8, axis=0)
        o_ref[i] = o8[:MEM_HEADS] + o8[MEM_HEADS:]


def xattn_step_job(q, cache_k, cache_v, layer, *, tb):
    n = q.shape[0]

    def specs(inner):
        qspec = pl.BlockSpec((tb, MEM_HEADS, MEM_HD), lambda i, j: (i * inner + j, 0, 0))
        cspec = pl.BlockSpec((1, tb, N_MEM, MEM_HEADS, MEM_HD), lambda i, j: (layer, i * inner + j, 0, 0, 0))
        return qspec, cspec

    return SideJob(
        body=functools.partial(_xattn_step_kernel, tb=tb),
        args=(q.reshape(n, MEM_HEADS, MEM_HD), cache_k, cache_v),
        in_specs=lambda inner: [specs(inner)[0], specs(inner)[1], specs(inner)[1]],
        out_shape=(jax.ShapeDtypeStruct((n, MEM_HEADS, MEM_HD), F32),),
        out_specs=lambda inner: [specs(inner)[0]],
        steps=n // tb)


def run_job(job, name):
    return pl.pallas_call(
        job.body, grid=(job.steps, 1), in_specs=job.in_specs(1), out_specs=job.out_specs(1),
        out_shape=list(job.out_shape), compiler_params=_cparams("parallel", "arbitrary"), name=name)(*job.args)


ROUTER_LANES = 128
NEG_BIG = -1e30


def _moe_gates(logits):
    lane = lax.broadcasted_iota(jnp.int32, logits.shape, 1)
    first = lambda mask: jnp.min(jnp.where(mask, lane, ROUTER_LANES), axis=-1, keepdims=True)
    is_c = lane < MOE_GROUPS
    lc = jnp.where(is_c, logits, NEG_BIG)
    mc = jnp.max(lc, axis=-1, keepdims=True)
    g_idx = first(lc == mc)
    p_g = 1.0 / jnp.sum(jnp.where(is_c, jnp.exp(lc - mc), 0.0), axis=-1, keepdims=True)
    fl = lane - MOE_GROUPS
    in_g = (fl >= 0) & (fl < MOE_EXPERTS) & ((fl // MOE_PER_GROUP) == g_idx)
    lf = jnp.where(in_g, logits, NEG_BIG)
    m1 = jnp.max(lf, axis=-1, keepdims=True)
    i1 = first(lf == m1)
    lf2 = jnp.where(lane == i1, NEG_BIG, lf)
    m2 = jnp.max(lf2, axis=-1, keepdims=True)
    i2 = first(lf2 == m2)
    e2 = jnp.exp(m2 - m1)
    w_top = 1.0 / (1.0 + e2)
    gate = p_g * (jnp.where(lane == i1, w_top, 0.0) + jnp.where(lane == i2, e2 * w_top, 0.0))
    return gate, g_idx


MOE_EPS = 2
MOE_STEPS = MOE_EXPERTS // MOE_EPS

SideJob = collections.namedtuple("SideJob", "body args in_specs out_shape out_specs steps")


def _with_side(main_kernel, n_in, n_out, side):
    ns_in, ns_out = len(side.args), len(side.out_shape)

    def kern(*refs):
        m_in = refs[:n_in]
        s_in = refs[n_in:n_in + ns_in]
        m_out = refs[n_in + ns_in:n_in + ns_in + n_out]
        s_out = refs[n_in + ns_in + n_out:n_in + ns_in + n_out + ns_out]
        scratch = refs[n_in + ns_in + n_out + ns_out:]
        main_kernel(*m_in, *m_out, *scratch, side=lambda: side.body(*s_in, *s_out))

    return kern


def _router_logits(h, wr_ref, br_ref):
    h_hi = h.astype(BF16)
    h_lo = (h - h_hi.astype(F32)).astype(BF16)
    acc = jnp.dot(h_hi, wr_ref[0], preferred_element_type=F32)
    acc = acc + jnp.dot(h_hi, wr_ref[1], preferred_element_type=F32)
    acc = acc + jnp.dot(h_lo, wr_ref[0], preferred_element_type=F32)
    return acc + br_ref[...]


def _experts_ffn(hb, gate, e0, w1_ref, w3_ref, w2_ref):
    lane = lax.broadcasted_iota(jnp.int32, gate.shape, 1)
    acc = None
    for e in range(MOE_EPS):
        a1 = jnp.dot(hb, w1_ref[e].astype(BF16), preferred_element_type=F32)
        a3 = jnp.dot(hb, w3_ref[e].astype(BF16), preferred_element_type=F32)
        ge = jnp.sum(jnp.where(lane == MOE_GROUPS + e0 + e, gate, 0.0), axis=-1, keepdims=True)
        hid = (jax.nn.silu(a1) * a3 * ge).astype(BF16)
        part = jnp.dot(hid, w2_ref[e].astype(BF16), preferred_element_type=F32)
        acc = part if acc is None else acc + part
    return acc


def _moe_kernel(x_ref, g_ref, wr_ref, br_ref, w1_ref, w3_ref, w2_ref, *rest, final_norm, side=None):
    if final_norm:
        fin_ref, o_ref, h_scr, gate_scr = rest
    else:
        o_ref, h_scr, gate_scr = rest
    step = pl.program_id(1)

    @pl.when(step == 0)
    def _():
        x = x_ref[...]
        h = _rms(x, g_ref[...])
        h_scr[...] = h.astype(BF16)
        gate_scr[...] = _moe_gates(_router_logits(h, wr_ref, br_ref))[0]
        o_ref[...] = x

    o_ref[...] += _experts_ffn(h_scr[...], gate_scr[...], step * MOE_EPS, w1_ref, w3_ref, w2_ref)
    if side is not None:
        side()

    if final_norm:
        @pl.when(step == MOE_STEPS - 1)
        def _():
            o_ref[...] = _rms(o_ref[...], fin_ref[...])


def moe_dense(x, gain, w_r, b_r, w1, w3, w2, layer, *, tm=512, final_gain=None, side=None):
    n = x.shape[0]
    tm = min(tm, n)
    gain = gain.reshape(1, D_MODEL)
    row = pl.BlockSpec((tm, D_MODEL), lambda i, s: (i, 0))
    const2 = lambda a: pl.BlockSpec(a.shape, lambda i, s: (0,) * a.ndim)
    soff = layer * MOE_STEPS
    wspec = pl.BlockSpec((MOE_EPS, D_MODEL, MOE_HIDDEN), lambda i, s: (soff + s, 0, 0))
    args = [x, gain, w_r, b_r, w1, w3, w2]
    in_specs = [row, const2(gain), const2(w_r), const2(b_r), wspec, wspec,
                pl.BlockSpec((MOE_EPS, MOE_HIDDEN, D_MODEL), lambda i, s: (soff + s, 0, 0))]
    if final_gain is not None:
        args.append(final_gain.reshape(1, D_MODEL))
        in_specs.append(const2(args[-1]))
    kern = functools.partial(_moe_kernel, final_norm=final_gain is not None)
    out_shape = [jax.ShapeDtypeStruct((n, D_MODEL), F32)]
    out_specs = [row]
    grid = (n // tm, MOE_STEPS)
    if side is not None:
        assert side.steps == grid[0] * grid[1]
        kern = _with_side(kern, len(args), 1, side)
        args += list(side.args)
        in_specs += side.in_specs(MOE_STEPS)
        out_shape += list(side.out_shape)
        out_specs += side.out_specs(MOE_STEPS)
    outs = pl.pallas_call(
        kern, grid=grid, in_specs=in_specs, out_specs=out_specs, out_shape=out_shape,
        scratch_shapes=[pltpu.VMEM((tm, D_MODEL), BF16), pltpu.VMEM((tm, ROUTER_LANES), F32)],
        compiler_params=_cparams("parallel", "arbitrary"), name="moe")(*args)
    return outs[0] if side is None else (outs[0], outs[1:])


def _group_weights(w1, w3, w2):
    ne = w1.shape[0] * MOE_EXPERTS
    return (w1.astype(BF16).reshape(ne, D_MODEL, MOE_HIDDEN), w3.astype(BF16).reshape(ne, D_MODEL, MOE_HIDDEN),
            w2.astype(BF16).reshape(ne, MOE_HIDDEN, D_MODEL))


def _router_params(w_rc, b_rc, w_rf, b_rf):
    pad = ROUTER_LANES - MOE_GROUPS - MOE_EXPERTS
    w_r = jnp.concatenate([w_rc, w_rf, jnp.zeros((D_MODEL, pad), F32)], axis=1).astype(F32)
    b_r = jnp.concatenate([b_rc, b_rf, jnp.zeros((pad,), F32)]).reshape(1, ROUTER_LANES).astype(F32)
    w_hi = w_r.astype(BF16)
    w_lo = (w_r - w_hi.astype(F32)).astype(BF16)
    return jnp.stack([w_hi, w_lo]), b_r


def _mem_kv_kernel(x_ref, g_ref, w_ref, kf_ref, vf_ref, kh_ref, vh_ref):
    h = _rms(x_ref[...], g_ref[0]).astype(BF16)
    for col, f_ref, h_ref in ((0, kf_ref, kh_ref), (D_MODEL, vf_ref, vh_ref)):
        acc = jnp.dot(h, w_ref[0, :, col:col + D_MODEL], preferred_element_type=F32)
        f_ref[0] = acc
        for hd in range(MEM_HEADS):
            h_ref[0, :, hd, :] = acc[:, hd * MEM_HD:(hd + 1) * MEM_HD]


def mem_kv(mem, gains, w_kv, *, tm=512):
    rows = mem.shape[0]
    nl = w_kv.shape[0]
    flat = jax.ShapeDtypeStruct((nl, rows, D_MODEL), F32)
    head = jax.ShapeDtypeStruct((nl, rows, MEM_HEADS, MEM_HD), F32)
    fspec = pl.BlockSpec((1, tm, D_MODEL), lambda l, i: (l, i, 0))
    hspec = pl.BlockSpec((1, tm, MEM_HEADS, MEM_HD), lambda l, i: (l, i, 0, 0))
    return pl.pallas_call(
        _mem_kv_kernel, grid=(nl, rows // tm),
        in_specs=[pl.BlockSpec((tm, D_MODEL), lambda l, i: (i, 0)),
                  pl.BlockSpec((1, 1, D_MODEL), lambda l, i: (l, 0, 0)),
                  pl.BlockSpec((1, D_MODEL, 2 * D_MODEL), lambda l, i: (l, 0, 0))],
        out_specs=(fspec, fspec, hspec, hspec), out_shape=(flat, flat, head, head),
        compiler_params=_cparams("parallel", "parallel"), name="mem_kv")(
            mem, gains.reshape(nl, 1, D_MODEL), w_kv)


def _forward(xp, xs, nbp, w, st, mem_k, mem_v, cache_k, cache_v):
    assert DEPTH == 2
    nbs = xs.shape[0]
    moe_tm = 1024
    moe_steps_p = (xp.shape[0] // moe_tm) * MOE_STEPS
    rwp = tuple(w[k][0] for k in ('rw_mu', 'rw_w0', 'rw_w2', 'rw_a0', 'rw_a2', 'rw_g2',
                                  'rw_k_k', 'rw_k_a', 'rw_r_k', 'rw_ln_w', 'rw_ln_b'))
    w_in0, w_out0 = w['w_in0_bf'][0], w['w_out0_bf'][0]
    moe = lambda x, layer, **kw: moe_dense(x, w['norm_ffn'][layer], *w['router'][layer], *w['moe_g'], layer, **kw)

    u, p_s = linear(xs, w_in0, gain=w['norm_mix'][0], splits=(S5_WIDTH, RW_PROJ))
    y_s5, s5r_s, s5i_s = s5_mixer(u.reshape(1, nbs, S5_WIDTH), st['s5_re'], st['s5_im'], w['s5p'][0],
                                  w['s5_d'][0], w['s5_w_glu'][0], tc=1)
    y_rw, rw_s = rwkv_step(p_s, st['shift'], st['rwkv'], rwp)
    xs = linear(y_s5.reshape(nbs, S5_WIDTH), w_out0[:S5_WIDTH], x2=y_rw, w2=w_out0[S5_WIDTH:], residual=xs)
    q_s = linear(xs, w['w_mq_bf'][0], gain=w['norm_mem'][0])

    zeros = lambda *shape: jnp.zeros(shape, F32)
    u, p_p = linear(xp, w_in0, gain=w['norm_mix'][0], splits=(S5_WIDTH, RW_PROJ), out_tmajor=True, batch=nbp)
    y_s5, s5r_p, s5i_p = s5_mixer(u, zeros(nbp, S5_STATE), zeros(nbp, S5_STATE), w['s5p'][0],
                                  w['s5_d'][0], w['s5_w_glu'][0], tc=128)
    rw_bs = 4
    rw_steps = (nbp // rw_bs) * (p_p.shape[0] // RW_HD)
    job = xattn_step_job(q_s, cache_k, cache_v, 0, tb=nbs // rw_steps)
    (y_rw, rw_p, sh_p), (att_s,) = rwkv_prompt(p_p, zeros(nbp, RW_PROJ), zeros(nbp, RW_HEADS, RW_HD, RW_HD),
                                               rwp, bs=rw_bs, side=job)
    xp = linear(y_s5, w_out0[:S5_WIDTH], x2=y_rw, w2=w_out0[S5_WIDTH:], residual=xp, x_tmajor=True)
    xp = xattn_prompt(xp, w['norm_mem'][0], w['w_mq_bf'][0], mem_k, mem_v, w['w_mo_bf'][0], 0, nb=nbp)

    xs = linear(att_s.reshape(nbs, D_MODEL), w['w_mo_bf'][0], residual=xs)
    xs = moe(xs, 0)
    q, k, v, g = linear(xs, w['w_in1_bf'][0], gain=w['norm_mix'][1], out_dtype=BF16, splits=(NQ, NQ, NV, NV))
    job = retention_step_job(q, k, v, g, st['ret'], pos0=float(PAST_LEN), tb=nbs // moe_steps_p)
    xp, (ret_s, y_ret) = moe(xp, 0, tm=moe_tm, side=job)
    xs = linear(y_ret.reshape(nbs, NV), w['w_out1_bf'][0], residual=xs)
    q_s = linear(xs, w['w_mq_bf'][1], gain=w['norm_mem'][1])

    q, k, v, g = linear(xp, w['w_in1_bf'][0], gain=w['norm_mix'][1], out_dtype=BF16, splits=(NQ, NQ, NV, NV),
                        tm=256)
    y, ret_p = retention_prompt(q, k, v, g, nb=nbp)
    xp = linear(y, w['w_out1_bf'][0], residual=xp)
    xp = xattn_prompt(xp, w['norm_mem'][1], w['w_mq_bf'][1], mem_k, mem_v, w['w_mo_bf'][1], 1, nb=nbp)
    job = xattn_step_job(q_s, cache_k, cache_v, 1, tb=nbs // moe_steps_p)
    y_p, (att_s,) = moe(xp, 1, tm=moe_tm, final_gain=w['norm_final'], side=job)
    xs = linear(att_s.reshape(nbs, D_MODEL), w['w_mo_bf'][1], residual=xs)
    y_s = moe(xs, 1, final_gain=w['norm_final'])

    grp = lambda z, nb: z.reshape(1, nb, S5_GROUPS, S5_N)
    prompt_out = (y_p, grp(s5r_p, nbp), grp(s5i_p, nbp), rw_p[None], sh_p[None], ret_p[None])
    sample_out = (y_s, grp(s5r_s, nbs), grp(s5i_s, nbs), rw_s[None], p_s[None], ret_s[None])
    return prompt_out, sample_out


def kernel(x_prompt, x_sample, mem_prompt, state_s5_re, state_s5_im, state_rwkv, state_shift, state_ret, cache_mem_k, cache_mem_v, norm_mix, norm_mem, norm_ffn, norm_final, w_in0, w_out0, s5_a_re, s5_a_im, s5_b_re, s5_b_im, s5_c_re, s5_c_im, s5_d, s5_log_dt, s5_w_glu, rw_mu, rw_w0, rw_w2, rw_a0, rw_a2, rw_g2, rw_k_k, rw_k_a, rw_r_k, rw_ln_w, rw_ln_b, w_in1, w_out1, mem_norm, w_mq, w_mk, w_mv, w_mo, moe_w_rc, moe_b_rc, moe_w_rf, moe_b_rf, moe_w1, moe_w3, moe_w2):
    w = dict(norm_mix=norm_mix, norm_mem=norm_mem, norm_ffn=norm_ffn, norm_final=norm_final,
             w_in0=w_in0, w_out0=w_out0, s5_a_re=s5_a_re, s5_a_im=s5_a_im, s5_b_re=s5_b_re, s5_b_im=s5_b_im,
             s5_c_re=s5_c_re, s5_c_im=s5_c_im, s5_d=s5_d, s5_log_dt=s5_log_dt, s5_w_glu=s5_w_glu,
             rw_mu=rw_mu, rw_w0=rw_w0, rw_w2=rw_w2, rw_a0=rw_a0, rw_a2=rw_a2, rw_g2=rw_g2,
             rw_k_k=rw_k_k, rw_k_a=rw_k_a, rw_r_k=rw_r_k, rw_ln_w=rw_ln_w, rw_ln_b=rw_ln_b,
             w_in1=w_in1, w_out1=w_out1, w_mq=w_mq, w_mo=w_mo,
             moe_w_rc=moe_w_rc, moe_b_rc=moe_b_rc, moe_w_rf=moe_w_rf, moe_b_rf=moe_b_rf,
             moe_w1=moe_w1, moe_w3=moe_w3, moe_w2=moe_w2)
    nbp, t_len, _ = x_prompt.shape
    nbs = x_sample.shape[0]
    n_even, n_odd = state_s5_re.shape[0], state_ret.shape[0]
    for name in ('w_in0', 'w_out0', 'w_in1', 'w_out1', 'w_mq', 'w_mo'):
        w[name + '_bf'] = w[name].astype(BF16)
    w['s5p'] = [_s5_params(s5_a_re[i], s5_a_im[i], s5_b_re[i], s5_b_im[i], s5_c_re[i], s5_c_im[i], s5_log_dt[i])
                for i in range(n_even)]
    w['router'] = [_router_params(moe_w_rc[l], moe_b_rc[l], moe_w_rf[l], moe_b_rf[l]) for l in range(DEPTH)]
    w['moe_g'] = _group_weights(moe_w1, moe_w3, moe_w2)

    mem = mem_prompt.reshape(nbp * N_MEM, D_MODEL)
    w_kv = jnp.concatenate([w_mk, w_mv], axis=2).astype(BF16)
    mk, mv, mk_h, mv_h = mem_kv(mem, mem_norm, w_kv)
    mem_k_l = mk.reshape(DEPTH, nbp, N_MEM, D_MODEL)
    mem_v_l = mv.reshape(DEPTH, nbp, N_MEM, D_MODEL)
    mem_k_p = mk_h.reshape(DEPTH, nbp, N_MEM, MEM_HEADS, MEM_HD)
    mem_v_p = mv_h.reshape(DEPTH, nbp, N_MEM, MEM_HEADS, MEM_HD)

    assert n_even == 1 and n_odd == 1
    st = dict(s5_re=state_s5_re.reshape(nbs, S5_STATE), s5_im=state_s5_im.reshape(nbs, S5_STATE),
              rwkv=state_rwkv[0], shift=state_shift[0], ret=state_ret[0])
    (y_p, s5r_p, s5i_p, rw_p, sh_p, ret_p), (y_s, s5r_s, s5i_s, rw_s, sh_s, ret_s) = _forward(
        x_prompt.reshape(nbp * t_len, D_MODEL), x_sample.reshape(nbs, D_MODEL), nbp, w, st,
        mem_k_l, mem_v_l, cache_mem_k, cache_mem_v)
    return (y_p.reshape(nbp, t_len, D_MODEL), y_s.reshape(nbs, 1, D_MODEL),
            s5r_p, s5i_p, rw_p, sh_p, ret_p, mem_k_p, mem_v_p, s5r_s, s5i_s, rw_s, sh_s, ret_s)
```

```python
import collections
import functools
import math

import jax
import jax.numpy as jnp
from jax import lax
from jax.experimental import pallas as pl
from jax.experimental.pallas import tpu as pltpu

F32 = jnp.float32
BF16 = jnp.bfloat16

D_MODEL = 1024
DEPTH = 2
PAST_LEN = 16384
S5_WIDTH = 512
S5_GROUP = 16
S5_GROUPS = 32
S5_N = 64
S5_STATE = S5_GROUPS * S5_N
S5_GBLK = 8
RW_WIDTH = 512
RW_HD = 64
RW_HEADS = 8
RW_LORA = 256
RW_PROJ = 3 * RW_WIDTH + RW_LORA
IN0 = S5_WIDTH + RW_PROJ
RET_DK = 256
RET_HEADS = 4
RET_DV = 512
RET_CHUNK = 256
NQ = RET_HEADS * RET_DK
NV = RET_HEADS * RET_DV
IN1 = 2 * NQ + 2 * NV
N_MEM = 256
MEM_HEADS = 4
MEM_HD = 256
MOE_GROUPS = 4
MOE_PER_GROUP = 4
MOE_EXPERTS = 16
MOE_HIDDEN = 256
NORM_EPS = 1e-6
RW_GN_EPS = 64e-5
ROPE_BASE = 10000.0

VMEM_LIMIT = 56 * 1024 * 1024


def _cparams(*sem):
    return pltpu.CompilerParams(dimension_semantics=sem, vmem_limit_bytes=VMEM_LIMIT)


def _bdot(a, b):
    return jnp.dot(a.astype(BF16), b.astype(BF16), preferred_element_type=F32)


def _dot_nt(a, b):
    return lax.dot_general(a.astype(BF16), b.astype(BF16), (((1,), (1,)), ((), ())),
                           preferred_element_type=F32)


def _dot_tn(a, b):
    return lax.dot_general(a.astype(BF16), b.astype(BF16), (((0,), (0,)), ((), ())),
                           preferred_element_type=F32)


def _split3(x):
    hi = x.astype(BF16)
    r1 = x - hi.astype(F32)
    mid = r1.astype(BF16)
    lo = (r1 - mid.astype(F32)).astype(BF16)
    return hi, mid, lo


def _dot_exact_rhs(x, m_bf16, passes=3):
    hi, mid, lo = _split3(x)
    acc = jnp.dot(hi, m_bf16, preferred_element_type=F32)
    if passes > 1:
        acc = acc + jnp.dot(mid, m_bf16, preferred_element_type=F32)
    if passes > 2:
        acc = acc + jnp.dot(lo, m_bf16, preferred_element_type=F32)
    return acc


def _rms(x, g):
    ms = jnp.mean(x * x, axis=-1, keepdims=True)
    return x * lax.rsqrt(ms + NORM_EPS) * g


def _linear_kernel(*refs, norm, two, res):
    it = iter(refs)
    x_ref = next(it)
    g_ref = next(it) if norm else None
    w_ref = next(it)
    x2_ref = next(it) if two else None
    w2_ref = next(it) if two else None
    r_ref = next(it) if res else None
    o_refs = list(it)
    x = x_ref[...].astype(F32)
    if norm:
        x = _rms(x, g_ref[...])
    xb = x.astype(BF16)
    x2b = x2_ref[...].astype(BF16) if two else None
    col = 0
    for o_ref in o_refs:
        m = o_ref.shape[-1]
        step = next((s for s in (512, 256) if m % s == 0), m)
        for j in range(m // step):
            sl = slice(col + j * step, col + (j + 1) * step)
            acc = jnp.dot(xb, w_ref[:, sl], preferred_element_type=F32)
            if two:
                acc = acc + jnp.dot(x2b, w2_ref[:, sl], preferred_element_type=F32)
            if res:
                acc = acc + r_ref[:, sl]
            o_ref[:, j * step:(j + 1) * step] = acc.astype(o_ref.dtype)
        col += m


def _row_spec(tm, width, tmajor_b):
    if tmajor_b is None:
        return pl.BlockSpec((tm, width), lambda i: (i, 0))
    nb, tiles_per_b = tmajor_b
    return pl.BlockSpec((tm, width), lambda i: (i % tiles_per_b, i // tiles_per_b))


def linear(x, w, *, gain=None, x2=None, w2=None, residual=None, out_dtype=F32, tm=512,
           x_tmajor=False, out_tmajor=False, batch=None, splits=None, name="linear"):
    if x_tmajor:
        t_len, nb, k = x.shape
        n = t_len * nb
    else:
        n, k = x.shape
        nb = batch
        t_len = n // nb if nb else None
    m = w.shape[1]
    tm = min(tm, n if not (x_tmajor or out_tmajor) else t_len)
    assert n % tm == 0
    tiles_per_b = (t_len // tm) if (x_tmajor or out_tmajor) else None
    args, specs = [], []

    def add_rows(a, tmajor):
        width = a.shape[-1]
        args.append(a.reshape(t_len, nb * width) if tmajor else a)
        specs.append(_row_spec(tm, width, (nb, tiles_per_b) if tmajor else None))

    add_rows(x, x_tmajor)
    if gain is not None:
        args.append(gain.reshape(1, k).astype(F32))
        specs.append(pl.BlockSpec((1, k), lambda i: (0, 0)))
    args.append(w)
    specs.append(pl.BlockSpec(w.shape, lambda i: (0, 0)))
    if x2 is not None:
        add_rows(x2, x_tmajor)
        args.append(w2)
        specs.append(pl.BlockSpec(w2.shape, lambda i: (0, 0)))
    if residual is not None:
        add_rows(residual, False)
    widths = tuple(splits) if splits else (m,)
    assert sum(widths) == m
    if out_tmajor:
        out_shape = [jax.ShapeDtypeStruct((t_len, nb * mw), out_dtype) for mw in widths]
    else:
        out_shape = [jax.ShapeDtypeStruct((n, mw), out_dtype) for mw in widths]
    out_specs = [_row_spec(tm, mw, (nb, tiles_per_b) if out_tmajor else None) for mw in widths]
    kern = functools.partial(_linear_kernel, norm=gain is not None, two=x2 is not None,
                             res=residual is not None)
    outs = pl.pallas_call(
        kern, grid=(n // tm,), in_specs=specs, out_specs=out_specs, out_shape=out_shape,
        compiler_params=_cparams("parallel"), name=name)(*args)
    if out_tmajor:
        outs = [o.reshape(t_len, nb, mw) for o, mw in zip(outs, widths)]
    return outs if splits else outs[0]


def _s5_kernel(u_ref, h_re_ref, h_im_ref, abar_re_ref, abar_im_ref, bb_re_ref, bb_im_ref,
               cc_re_ref, cc_im_ref, d_ref, wglu_ref, y_ref, s_re_ref, s_im_ref,
               x_re, x_im, st_re, st_im, il_scr, *, tc, nb, flat):
    c = pl.program_id(0)
    nlb = S5_WIDTH // 128
    rows = tc * nb
    nblk = S5_GROUPS // S5_GBLK
    bw_in = S5_GBLK * S5_GROUP
    bw_st = S5_GBLK * S5_N

    @pl.when(c == 0)
    def _():
        st_re[...] = h_re_ref[...]
        st_im[...] = h_im_ref[...]

    if flat:
        for b in range(nb):
            for j in range(nlb):
                il_scr[j, pl.ds(b, tc, stride=nb), :] = u_ref[:, b * S5_WIDTH + j * 128:b * S5_WIDTH + (j + 1) * 128]
        u = jnp.concatenate([il_scr[j] for j in range(nlb)], axis=-1)
    else:
        u = u_ref[...].reshape(rows, S5_WIDTH)
    ub = u.astype(BF16)
    for gb in range(nblk):
        ui = ub[:, gb * bw_in:(gb + 1) * bw_in]
        x_re[:, gb * bw_st:(gb + 1) * bw_st] = jnp.dot(ui, bb_re_ref[gb], preferred_element_type=F32)
        x_im[:, gb * bw_st:(gb + 1) * bw_st] = jnp.dot(ui, bb_im_ref[gb], preferred_element_type=F32)

    lane_blk = 1024
    for lb in range(S5_STATE // lane_blk):
        sl = slice(lb * lane_blk, (lb + 1) * lane_blk)
        ar = jnp.broadcast_to(abar_re_ref[:, sl], (nb, lane_blk))
        ai = jnp.broadcast_to(abar_im_ref[:, sl], (nb, lane_blk))

        def body(t, carry, sl=sl, ar=ar, ai=ai):
            xr, xi = carry
            r0 = pl.multiple_of(t * nb, nb)
            br = x_re[pl.ds(r0, nb), sl]
            bi = x_im[pl.ds(r0, nb), sl]
            nr = ar * xr - ai * xi + br
            ni = ar * xi + ai * xr + bi
            x_re[pl.ds(r0, nb), sl] = nr
            x_im[pl.ds(r0, nb), sl] = ni
            return nr, ni

        fr, fi = lax.fori_loop(0, tc, body, (st_re[:, sl], st_im[:, sl]), unroll=min(tc, 4))
        st_re[:, sl] = fr
        st_im[:, sl] = fi

    for gb in range(nblk):
        xr = x_re[:, gb * bw_st:(gb + 1) * bw_st].astype(BF16)
        xi = x_im[:, gb * bw_st:(gb + 1) * bw_st].astype(BF16)
        yb = (jnp.dot(xr, cc_re_ref[gb], preferred_element_type=F32)
              - jnp.dot(xi, cc_im_ref[gb], preferred_element_type=F32))
        cs = slice(gb * bw_in, (gb + 1) * bw_in)
        yb = yb + d_ref[:, cs] * u[:, cs]
        x_re[:, cs] = jax.nn.gelu(yb)
    y = x_re[:, :S5_WIDTH]
    y = y * jax.nn.sigmoid(jnp.dot(y.astype(BF16), wglu_ref[...], preferred_element_type=F32))
    if flat:
        for j in range(nlb):
            il_scr[j] = y[:, j * 128:(j + 1) * 128]
        for b in range(nb):
            for j in range(nlb):
                y_ref[:, b * S5_WIDTH + j * 128:b * S5_WIDTH + (j + 1) * 128] = (
                    il_scr[j, pl.ds(b, tc, stride=nb), :].astype(y_ref.dtype))
    else:
        y_ref[...] = y.reshape(y_ref.shape).astype(y_ref.dtype)

    @pl.when(c == pl.num_programs(0) - 1)
    def _():
        s_re_ref[...] = st_re[...]
        s_im_ref[...] = st_im[...]


def _s5_params(a_re, a_im, b_re, b_im, c_re, c_im, log_dt):
    dt = jnp.exp(log_dt.astype(F32))[:, None]
    ar, ai = a_re.astype(F32), a_im.astype(F32)
    mag = jnp.exp(dt * ar)
    abar_re, abar_im = mag * jnp.cos(dt * ai), mag * jnp.sin(dt * ai)
    den = ar * ar + ai * ai
    nr = abar_re - 1.0
    coef_re = (nr * ar + abar_im * ai) / den
    coef_im = (abar_im * ar - nr * ai) / den
    cr, ci = coef_re[..., None], coef_im[..., None]
    brf, bif = b_re.astype(F32), b_im.astype(F32)
    bb_re = cr * brf - ci * bif
    bb_im = cr * bif + ci * brf
    nblk = S5_GROUPS // S5_GBLK
    eye = jnp.eye(S5_GBLK, dtype=F32)

    def blockdiag_in(bb):
        t = jnp.transpose(bb, (0, 2, 1)).reshape(nblk, S5_GBLK, S5_GROUP, S5_N)
        m = jnp.einsum('kgcn,gh->kgchn', t, eye)
        return m.reshape(nblk, S5_GBLK * S5_GROUP, S5_GBLK * S5_N).astype(BF16)

    def blockdiag_out(cc):
        t = jnp.transpose(cc.astype(F32), (0, 2, 1)).reshape(nblk, S5_GBLK, S5_N, S5_GROUP)
        m = jnp.einsum('khnc,hg->khngc', t, eye)
        return m.reshape(nblk, S5_GBLK * S5_N, S5_GBLK * S5_GROUP).astype(BF16)

    return (abar_re.reshape(1, S5_STATE), abar_im.reshape(1, S5_STATE),
            blockdiag_in(bb_re), blockdiag_in(bb_im), blockdiag_out(c_re), blockdiag_out(c_im))


def s5_mixer(u_tm, h_re, h_im, params, d_skip, w_glu, *, tc):
    t_len, nb, _ = u_tm.shape
    abar_re, abar_im, bb_re, bb_im, cc_re, cc_im = params
    tc = min(tc, t_len)
    assert t_len % tc == 0 and nb % 8 == 0
    rows = tc * nb
    flat = t_len > 1
    full = lambda a: pl.BlockSpec(a.shape, lambda c: (0,) * a.ndim)
    if flat:
        u_arg = u_tm.reshape(t_len, nb * S5_WIDTH)
        io_spec = pl.BlockSpec((tc, nb * S5_WIDTH), lambda c: (c, 0))
        y_shape = jax.ShapeDtypeStruct((t_len, nb * S5_WIDTH), BF16)
    else:
        u_arg = u_tm
        io_spec = pl.BlockSpec((tc, nb, S5_WIDTH), lambda c: (c, 0, 0))
        y_shape = jax.ShapeDtypeStruct((t_len, nb, S5_WIDTH), BF16)
    args = (u_arg, h_re, h_im, abar_re, abar_im, bb_re, bb_im, cc_re, cc_im,
            d_skip.reshape(1, S5_WIDTH).astype(F32), w_glu.astype(BF16))
    in_specs = [io_spec] + [full(a) for a in args[1:]]
    st_shape = jax.ShapeDtypeStruct((nb, S5_STATE), F32)
    st_spec = pl.BlockSpec((nb, S5_STATE), lambda c: (0, 0))
    scratch = [pltpu.VMEM((rows, S5_STATE), F32), pltpu.VMEM((rows, S5_STATE), F32),
               pltpu.VMEM((nb, S5_STATE), F32), pltpu.VMEM((nb, S5_STATE), F32),
               pltpu.VMEM((S5_WIDTH // 128, rows if flat else 8, 128), F32)]
    y, s_re, s_im = pl.pallas_call(
        functools.partial(_s5_kernel, tc=tc, nb=nb, flat=flat), grid=(t_len // tc,), in_specs=in_specs,
        out_specs=(io_spec, st_spec, st_spec), out_shape=(y_shape, st_shape, st_shape),
        scratch_shapes=scratch, compiler_params=_cparams("arbitrary"), name="s5_mixer")(*args)
    return y.reshape(t_len, nb, S5_WIDTH), s_re, s_im


def _head_ones():
    i = lax.broadcasted_iota(jnp.int32, (RW_WIDTH, RW_WIDTH), 0) // RW_HD
    j = lax.broadcasted_iota(jnp.int32, (RW_WIDTH, RW_WIDTH), 1) // RW_HD
    return jnp.where(i == j, 1.0, 0.0).astype(BF16)


def _softplus(z):
    return jnp.maximum(z, 0.0) + jnp.log1p(jnp.exp(-jnp.abs(z)))


def _rw_prep(p, p_prev, prm, ones_bd):
    mu, w0, w2, a0, a2, g2, k_k, k_a = prm
    xm = p + (p_prev - p) * mu
    o1, o2, o3 = RW_WIDTH, 2 * RW_WIDTH, 3 * RW_WIDTH
    r, k, v = xm[:, :o1], xm[:, o1:o2], xm[:, o2:o3]
    wd, ad, gd = xm[:, o3:o3 + 64], xm[:, o3 + 64:o3 + 128], xm[:, o3 + 128:]
    w = -_softplus(-(w0 + _bdot(jnp.tanh(wd), w2))) - 0.5
    lw = -jnp.exp(w)
    a = jax.nn.sigmoid(a0 + _bdot(ad, a2))
    g = _bdot(jax.nn.sigmoid(gd), g2)
    kk = k * k_k
    ss = _dot_exact_rhs(kk * kk, ones_bd, passes=1)
    kk = kk / jnp.maximum(jnp.sqrt(ss), 1e-12)
    k = k * (1.0 + (a - 1.0) * k_a)
    return r, lw, k, v, -kk, kk * a, g


def _rw_post(o, r, k, v, g, r_k, ln_w, ln_b, ones_bd):
    inv = 1.0 / RW_HD
    mean = _dot_exact_rhs(o, ones_bd, passes=2) * inv
    d = o - mean
    var = _dot_exact_rhs(d * d, ones_bd, passes=1) * inv
    on = d * lax.rsqrt(var + RW_GN_EPS) * ln_w + ln_b
    bonus = _dot_exact_rhs(r * k * r_k, ones_bd, passes=1) * v
    return (on + bonus) * g


def _rw_chunk_kernel(p_ref, shift_ref, h0_ref, mu_ref, w0_ref, w2_ref, a0_ref, a2_ref, g2_ref,
                     kk_ref, ka_ref, rk_ref, lnw_ref, lnb_ref,
                     y_ref, hfin_ref, shout_ref, prev_scr, h_scr, o_scr, *, c_len, bs, side=None):
    c = pl.program_id(1)
    nc = pl.num_programs(1)
    cl = c_len

    @pl.when(c == 0)
    def _():
        prev_scr[...] = shift_ref[:, 0, :]
        h_scr[...] = h0_ref[...]

    ones_bd = _head_ones()
    row = lax.broadcasted_iota(jnp.int32, (cl, RW_PROJ), 0)
    ps, pprevs = [], []
    for bi in range(bs):
        p = p_ref[:, bi * RW_PROJ:(bi + 1) * RW_PROJ]
        pprevs.append(jnp.where(row == 0, prev_scr[bi:bi + 1, :], pltpu.roll(p, 1, 0)))
        ps.append(p)
    p_all = jnp.concatenate(ps, axis=0) if bs > 1 else ps[0]
    pprev_all = jnp.concatenate(pprevs, axis=0) if bs > 1 else pprevs[0]
    prm = (mu_ref[...], w0_ref[...], w2_ref[...], a0_ref[...], a2_ref[...], g2_ref[...],
           kk_ref[...], ka_ref[...])
    r, lw, k, v, a, b, g = _rw_prep(p_all, pprev_all, prm, ones_bd)

    ti = lax.broadcasted_iota(jnp.int32, (cl, cl), 0)
    si = lax.broadcasted_iota(jnp.int32, (cl, cl), 1)
    lmat = jnp.where(ti >= si, 1.0, 0.0).astype(BF16)
    eye = jnp.where(ti == si, 1.0, 0.0)
    mi = lax.broadcasted_iota(jnp.int32, (2 * cl, 3 * cl), 0)
    mj = lax.broadcasted_iota(jnp.int32, (2 * cl, 3 * cl), 1)
    t_row = jnp.where(mi >= cl, mi - cl, mi)
    s_col = jnp.where(mj < cl, mj, jnp.where(mj >= 2 * cl, mj - 2 * cl, -4 * cl))
    keep = (t_row - s_col) >= jnp.where(mi >= cl, 0, 1)
    eye_bf = eye.astype(BF16)

    lhs_l, rhs_l, vh_l, hcat_l, kb_l, etot_l = [], [], [], [], [], []
    for bi in range(bs):
        rs = slice(bi * cl, (bi + 1) * cl)
        lw_b = lw[rs]
        l_hi, l_mid, l_lo = _split3(lw_b)
        cum = (jnp.dot(lmat, l_hi, preferred_element_type=F32)
               + jnp.dot(lmat, l_mid, preferred_element_type=F32)
               + jnp.dot(lmat, l_lo, preferred_element_type=F32))
        tot = cum[cl - 1:cl, :]
        e_neg = jnp.exp(-cum)
        e_rem = jnp.exp(tot - cum)
        at = (a[rs] * jnp.exp(cum - lw_b)).astype(BF16)
        rt = (r[rs] * jnp.exp(cum)).astype(BF16)
        bt = (b[rs] * e_neg).astype(BF16)
        kt = (k[rs] * e_neg).astype(BF16)
        bh = (b[rs] * e_rem).astype(BF16)
        kh = (k[rs] * e_rem).astype(BF16)
        e_tot = jnp.exp(tot)
        vb = v[rs].astype(BF16)
        for h in range(RW_HEADS):
            hs = slice(h * RW_HD, (h + 1) * RW_HD)
            lhs_l.append(jnp.concatenate([at[:, hs], rt[:, hs]], axis=0))
            rhs_l.append(jnp.concatenate([kt[:, hs], eye_bf, bt[:, hs]], axis=0))
            vh_l.append(vb[:, hs])
            kb_l.append(jnp.concatenate([kh[:, hs], bh[:, hs]], axis=0))
            etot_l.append(jnp.sum(eye * e_tot[:, hs], axis=-1, keepdims=True))
            hcat_l.append(h_scr[bi, h])

    nitem = bs * RW_HEADS
    items = range(nitem)
    aa_l = [jnp.where(keep, _dot_nt(lhs_l[i], rhs_l[i]), 0.0).astype(BF16) for i in items]
    pw_l = [aa_l[i][:cl, 2 * cl:] for i in items]
    tinv_l = [eye_bf + pw_l[i] for i in items]
    for _ in range(int(math.log2(cl)) - 1):
        pw_l = [jnp.dot(pw_l[i], pw_l[i], preferred_element_type=F32).astype(BF16) for i in items]
        tinv_l = [jnp.dot(tinv_l[i], eye_bf + pw_l[i], preferred_element_type=F32).astype(BF16) for i in items]
    vh_cat = [jnp.concatenate([vh_l[i], hcat_l[i].astype(BF16)], axis=0) for i in items]
    x1_l = [jnp.dot(aa_l[i][:cl, :2 * cl], vh_cat[i], preferred_element_type=F32).astype(BF16) for i in items]
    u_l = [jnp.dot(tinv_l[i], x1_l[i], preferred_element_type=F32).astype(BF16) for i in items]
    o_l = [jnp.dot(aa_l[i][cl:, :], jnp.concatenate([vh_cat[i], u_l[i]], axis=0),
                   preferred_element_type=F32) for i in items]
    hn_l = [hcat_l[i] * etot_l[i]
            + lax.dot_general(kb_l[i], jnp.concatenate([vh_l[i], u_l[i]], axis=0), (((0,), (0,)), ((), ())),
                              preferred_element_type=F32) for i in items]

    for bi in range(bs):
        for h in range(RW_HEADS):
            i = bi * RW_HEADS + h
            o_scr[bi * cl:(bi + 1) * cl, h * RW_HD:(h + 1) * RW_HD] = o_l[i]
            h_scr[bi, h] = hn_l[i]
        prev_scr[bi:bi + 1, :] = ps[bi][cl - 1:cl, :]

    y = _rw_post(o_scr[...], r, k, v, g, rk_ref[...], lnw_ref[...], lnb_ref[...], ones_bd)
    for bi in range(bs):
        y_ref[:, bi * RW_WIDTH:(bi + 1) * RW_WIDTH] = y[bi * cl:(bi + 1) * cl].astype(y_ref.dtype)
    if side is not None:
        side()

    @pl.when(c == nc - 1)
    def _():
        hfin_ref[...] = h_scr[...]
        for bi in range(bs):
            shout_ref[bi] = ps[bi][cl - 1:cl, :]


def _rw_param_args(mu, w0, w2, a0, a2, g2, k_k, k_a, r_k, ln_w, ln_b):
    row = lambda z: z.reshape(1, -1).astype(F32)
    return (row(mu), row(w0), w2.astype(BF16), row(a0), a2.astype(BF16), g2.astype(BF16),
            row(k_k), row(k_a), row(r_k), row(ln_w), row(ln_b))


def rwkv_prompt(p_tm, shift, s0, params, *, bs=4, side=None):
    c_len = RW_HD
    t_len, nb, _ = p_tm.shape
    assert t_len % c_len == 0 and nb % bs == 0
    prm = _rw_param_args(*params)
    const = lambda a: pl.BlockSpec(a.shape, lambda b, c: (0,) * a.ndim)
    st_spec = pl.BlockSpec((bs, RW_HEADS, RW_HD, RW_HD), lambda b, c: (b, 0, 0, 0))
    sh_spec = pl.BlockSpec((bs, 1, RW_PROJ), lambda b, c: (b, 0, 0))
    in_specs = [pl.BlockSpec((c_len, bs * RW_PROJ), lambda b, c: (c, b)), sh_spec, st_spec] + [const(a) for a in prm]
    out_shape = (jax.ShapeDtypeStruct((t_len, nb * RW_WIDTH), BF16),
                 jax.ShapeDtypeStruct((nb, RW_HEADS, RW_HD, RW_HD), F32),
                 jax.ShapeDtypeStruct((nb, 1, RW_PROJ), F32))
    out_specs = (pl.BlockSpec((c_len, bs * RW_WIDTH), lambda b, c: (c, b)), st_spec, sh_spec)
    scratch = [pltpu.VMEM((bs, RW_PROJ), F32), pltpu.VMEM((bs, RW_HEADS, RW_HD, RW_HD), F32),
               pltpu.VMEM((bs * c_len, RW_WIDTH), F32)]
    h0 = jnp.swapaxes(s0, -1, -2)
    args = [p_tm.reshape(t_len, nb * RW_PROJ), shift.reshape(nb, 1, RW_PROJ), h0, *prm]
    kern = functools.partial(_rw_chunk_kernel, c_len=c_len, bs=bs)
    grid = (nb // bs, t_len // c_len)
    out_shape, out_specs = list(out_shape), list(out_specs)
    if side is not None:
        assert side.steps == grid[0] * grid[1]
        kern = _with_side(kern, len(args), 3, side)
        args += list(side.args)
        in_specs += side.in_specs(grid[1])
        out_shape += list(side.out_shape)
        out_specs += side.out_specs(grid[1])
    outs = pl.pallas_call(
        kern, grid=grid, in_specs=in_specs, out_specs=out_specs, out_shape=out_shape,
        scratch_shapes=scratch, compiler_params=_cparams("parallel", "arbitrary"), name="rwkv_prompt")(*args)
    y, h_fin, sh = outs[:3]
    res = (y.reshape(t_len, nb, RW_WIDTH), jnp.swapaxes(h_fin, -1, -2), sh.reshape(nb, RW_PROJ))
    return res if side is None else (res, outs[3:])


def _rw_step_prep_kernel(p_ref, shift_ref, mu_ref, w0_ref, w2_ref, a0_ref, a2_ref, g2_ref, kk_ref, ka_ref,
                         r_ref, k_ref, v_ref, g_ref, rt_ref, wt_ref, kt_ref, at_ref, bt_ref, vt_ref):
    prm = (mu_ref[...], w0_ref[...], w2_ref[...], a0_ref[...], a2_ref[...], g2_ref[...],
           kk_ref[...], ka_ref[...])
    r, lw, k, v, a, b, g = _rw_prep(p_ref[...], shift_ref[...], prm, _head_ones())
    r_ref[...] = r
    k_ref[...] = k
    v_ref[...] = v
    g_ref[...] = g
    rt_ref[...] = r.T
    wt_ref[...] = jnp.exp(lw).T
    kt_ref[...] = k.T
    at_ref[...] = a.T
    bt_ref[...] = b.T
    vt_ref[...] = v.T


def _rw_step_core_kernel(s_ref, r_ref, w_ref, k_ref, a_ref, b_ref, v_ref, s_out_ref, o_ref):
    r, w, k, a, b = r_ref[0], w_ref[0], k_ref[0], a_ref[0], b_ref[0]
    for j in range(s_ref.shape[1]):
        s = s_ref[0, j]
        sa = jnp.sum(s * a, axis=0, keepdims=True)
        s_new = s * w + sa * b + v_ref[0, j:j + 1, :] * k
        s_out_ref[0, j] = s_new
        o_ref[0, j:j + 1, :] = jnp.sum(s_new * r, axis=0, keepdims=True)


def _rw_step_post_kernel(ot_ref, r_ref, k_ref, v_ref, g_ref, rk_ref, lnw_ref, lnb_ref, y_ref):
    y_ref[...] = _rw_post(ot_ref[...].T, r_ref[...], k_ref[...], v_ref[...], g_ref[...],
                          rk_ref[...], lnw_ref[...], lnb_ref[...], _head_ones()).astype(y_ref.dtype)


def rwkv_step(p, shift, s0, params, *, vb=32):
    n = p.shape[0]
    prm = _rw_param_args(*params)
    vec = jax.ShapeDtypeStruct((n, RW_WIDTH), F32)
    vec_t = jax.ShapeDtypeStruct((RW_WIDTH, n), F32)
    r, k, v, g, rt, wt, kt, at, bt, vt = pl.pallas_call(
        _rw_step_prep_kernel, out_shape=(vec,) * 4 + (vec_t,) * 6, name="rwkv_step_prep")(p, shift, *prm[:8])
    heads = lambda z: z.reshape(RW_HEADS, RW_HD, n)
    k_spec = pl.BlockSpec((1, RW_HD, n), lambda h, j: (h, 0, 0))
    v_spec = pl.BlockSpec((1, vb, n), lambda h, j: (h, j, 0))
    st_spec = pl.BlockSpec((1, vb, RW_HD, n), lambda h, j: (h, j, 0, 0))
    st = jnp.transpose(s0, (1, 2, 3, 0))
    s_new, ot = pl.pallas_call(
        _rw_step_core_kernel, grid=(RW_HEADS, RW_HD // vb),
        in_specs=[st_spec] + [k_spec] * 5 + [v_spec], out_specs=(st_spec, v_spec),
        out_shape=(jax.ShapeDtypeStruct(st.shape, F32), jax.ShapeDtypeStruct((RW_HEADS, RW_HD, n), F32)),
        compiler_params=_cparams("parallel", "parallel"), name="rwkv_step_core")(
            st, heads(rt), heads(wt), heads(kt), heads(at), heads(bt), heads(vt))
    y = pl.pallas_call(
        _rw_step_post_kernel, out_shape=jax.ShapeDtypeStruct((n, RW_WIDTH), BF16), name="rwkv_step_post")(
            ot.reshape(RW_WIDTH, n), r, k, v, g, *prm[8:])
    return y, jnp.transpose(s_new, (3, 0, 1, 2))


RET_LOG_G = tuple(math.log(1.0 - 2.0 ** (-5.0 - h)) for h in range(RET_HEADS))


def _rope_tables(pos, half):
    j = lax.broadcasted_iota(jnp.int32, (1, half), 1).astype(F32)
    inv = jnp.exp(j * (-math.log(ROPE_BASE) / half))
    ang = pos * inv
    return jnp.cos(ang), jnp.sin(ang)


def _rope(x, cos, sin):
    half = RET_DK // 2
    outs = []
    for h in range(RET_HEADS):
        x1 = x[:, h * RET_DK:h * RET_DK + half]
        x2 = x[:, h * RET_DK + half:(h + 1) * RET_DK]
        outs += [x1 * cos - x2 * sin, x1 * sin + x2 * cos]
    return jnp.concatenate(outs, axis=-1)


def _ret_norm_gate(o, g):
    o = o * lax.rsqrt(jnp.mean(o * o, axis=-1, keepdims=True) + NORM_EPS)
    return jax.nn.silu(g) * o


def _ret_tables_kernel(cos_ref, sin_ref, dmask_ref, qdec_ref, kdec_ref, *, c_len):
    t_len = cos_ref.shape[0]
    pos = lax.broadcasted_iota(jnp.int32, (t_len, 1), 0).astype(F32)
    cos, sin = _rope_tables(pos, RET_DK // 2)
    cos_ref[...] = cos
    sin_ref[...] = sin
    ti = lax.broadcasted_iota(jnp.int32, (c_len, 1), 0).astype(F32)
    ii = lax.broadcasted_iota(jnp.int32, (c_len, c_len), 0)
    jj = lax.broadcasted_iota(jnp.int32, (c_len, c_len), 1)
    diff = (ii - jj).astype(F32)
    for h in range(RET_HEADS):
        lg = RET_LOG_G[h]
        dmask_ref[h] = jnp.where(diff >= 0, jnp.exp(lg * jnp.maximum(diff, 0.0)), 0.0)
        qdec_ref[h] = jnp.exp(lg * (ti + 1.0))
        kdec_ref[h] = jnp.exp(lg * (c_len - 1.0 - ti))


def _ret_layer_kernel(x_ref, gain_ref, win_ref, wout_ref, cos_ref, sin_ref, dmask_ref, qdec_ref, kdec_ref,
                      o_ref, sfin_ref, s_scr, y_scr, *, c_len):
    c = pl.program_id(1)

    @pl.when(c == 0)
    def _():
        s_scr[...] = jnp.zeros_like(s_scr)

    x = x_ref[...]
    hb = _rms(x, gain_ref[...]).astype(BF16)
    proj = lambda lo, width: jnp.dot(hb, win_ref[:, lo:lo + width], preferred_element_type=F32)
    cos, sin = cos_ref[...], sin_ref[...]
    q = _rope(proj(0, NQ), cos, sin)
    k = _rope(proj(NQ, NQ), cos, sin) * (RET_DK ** -0.5)
    for h in range(RET_HEADS):
        c_dec = math.exp(RET_LOG_G[h] * c_len)
        qh = q[:, h * RET_DK:(h + 1) * RET_DK]
        kh = k[:, h * RET_DK:(h + 1) * RET_DK]
        vh = proj(2 * NQ + h * RET_DV, RET_DV).astype(BF16)
        s_h = s_scr[h]
        sc = _dot_nt(qh, kh) * dmask_ref[h]
        o = _bdot(sc, vh) + _bdot(qh * qdec_ref[h], s_h)
        s_scr[h] = s_h * c_dec + _dot_tn(kh * kdec_ref[h], vh)
        gh = proj(2 * NQ + NV + h * RET_DV, RET_DV)
        y_scr[:, h * RET_DV:(h + 1) * RET_DV] = _ret_norm_gate(o, gh).astype(BF16)
    o_ref[...] = x + jnp.dot(y_scr[...], wout_ref[...], preferred_element_type=F32)

    @pl.when(c == pl.num_programs(1) - 1)
    def _():
        sfin_ref[0] = s_scr[...]


def retention_layer_prompt(x, gain, w_in, w_out, *, nb, c_len=RET_CHUNK):
    n = x.shape[0]
    t_len = n // nb
    nc = t_len // c_len
    half = RET_DK // 2
    tabs = pl.pallas_call(
        functools.partial(_ret_tables_kernel, c_len=c_len),
        out_shape=(jax.ShapeDtypeStruct((t_len, half), F32), jax.ShapeDtypeStruct((t_len, half), F32),
                   jax.ShapeDtypeStruct((RET_HEADS, c_len, c_len), F32),
                   jax.ShapeDtypeStruct((RET_HEADS, c_len, 1), F32),
                   jax.ShapeDtypeStruct((RET_HEADS, c_len, 1), F32)),
        name="retention_tables")()
    row = pl.BlockSpec((c_len, D_MODEL), lambda b, c: (b * nc + c, 0))
    pos_spec = pl.BlockSpec((c_len, half), lambda b, c: (c, 0))
    const = lambda a: pl.BlockSpec(a.shape, lambda b, c: (0,) * a.ndim)
    st_spec = pl.BlockSpec((1, RET_HEADS, RET_DK, RET_DV), lambda b, c: (b, 0, 0, 0))
    gain = gain.reshape(1, D_MODEL)
    return pl.pallas_call(
        functools.partial(_ret_layer_kernel, c_len=c_len), grid=(nb, nc),
        in_specs=[row, const(gain), const(w_in), const(w_out), pos_spec, pos_spec] + [const(a) for a in tabs[2:]],
        out_specs=(row, st_spec),
        out_shape=(jax.ShapeDtypeStruct((n, D_MODEL), F32),
                   jax.ShapeDtypeStruct((nb, RET_HEADS, RET_DK, RET_DV), F32)),
        scratch_shapes=[pltpu.VMEM((RET_HEADS, RET_DK, RET_DV), F32), pltpu.VMEM((c_len, NV), BF16)],
        compiler_params=_cparams("parallel", "arbitrary"), name="retention_layer")(
            x, gain, w_in, w_out, *tabs)


def _ret_step_rope_kernel(q_ref, k_ref, qo_ref, ko_ref, *, pos0):
    pos = jnp.full((q_ref.shape[0], 1), pos0, F32)
    cos, sin = _rope_tables(pos, RET_DK // 2)
    qo_ref[...] = _rope(q_ref[...].astype(F32), cos, sin).T
    ko_ref[...] = (_rope(k_ref[...].astype(F32), cos, sin) * (RET_DK ** -0.5)).T


def _ret_step_core_kernel(s_ref, qt_ref, kt_ref, v_ref, g_ref, s_out_ref, y_ref):
    tb = s_ref.shape[0]
    step = pl.program_id(0) * pl.num_programs(1) + pl.program_id(1)
    lane = lax.broadcasted_iota(jnp.int32, qt_ref.shape, 1)
    for i in range(tb):
        mine = lane == step * tb + i
        q_col = jnp.sum(jnp.where(mine, qt_ref[...], 0.0), axis=-1, keepdims=True)
        k_col = jnp.sum(jnp.where(mine, kt_ref[...], 0.0), axis=-1, keepdims=True)
        for h in range(RET_HEADS):
            gam = math.exp(RET_LOG_G[h])
            s_h = s_ref[i, h]
            qc = q_col[h * RET_DK:(h + 1) * RET_DK]
            kc = k_col[h * RET_DK:(h + 1) * RET_DK]
            vs = slice(h * RET_DV, (h + 1) * RET_DV)
            vr = v_ref[i, :, vs].astype(F32)
            qk = jnp.sum(qc * kc, axis=0, keepdims=True)
            o = qk * vr + jnp.sum((qc * gam) * s_h, axis=0, keepdims=True)
            s_out_ref[i, h] = s_h * gam + kc * vr
            y_ref[i, :, vs] = _ret_norm_gate(o, g_ref[i, :, vs].astype(F32)).astype(y_ref.dtype)


def retention_step_job(q, k, v, g, s0, *, pos0, tb):
    n = q.shape[0]
    vec_t = jax.ShapeDtypeStruct((NQ, n), F32)
    qt, kt = pl.pallas_call(functools.partial(_ret_step_rope_kernel, pos0=pos0), out_shape=(vec_t, vec_t),
                            name="retention_step_rope")(q, k)
    st = lambda inner: pl.BlockSpec((tb, RET_HEADS, RET_DK, RET_DV), lambda i, j: (i * inner + j, 0, 0, 0))
    rw = lambda inner: pl.BlockSpec((tb, 1, NV), lambda i, j: (i * inner + j, 0, 0))
    whole = lambda inner: pl.BlockSpec((NQ, n), lambda i, j: (0, 0))
    return SideJob(
        body=_ret_step_core_kernel,
        args=(s0, qt, kt, v.reshape(n, 1, NV), g.reshape(n, 1, NV)),
        in_specs=lambda inner: [st(inner), whole(inner), whole(inner), rw(inner), rw(inner)],
        out_shape=(jax.ShapeDtypeStruct(s0.shape, F32), jax.ShapeDtypeStruct((n, 1, NV), BF16)),
        out_specs=lambda inner: [st(inner), rw(inner)],
        steps=n // tb)


def _xattn_prompt_kernel(x_ref, g_ref, wq_ref, mk_ref, mv_ref, wo_ref, o_ref, att_scr):
    x = x_ref[...]
    q = jnp.dot(_rms(x, g_ref[...]).astype(BF16), wq_ref[...], preferred_element_type=F32)
    for h in range(MEM_HEADS):
        hs = slice(h * MEM_HD, (h + 1) * MEM_HD)
        s = _dot_nt(q[:, hs], mk_ref[0, :, hs]) * (MEM_HD ** -0.5)
        s = s - jnp.max(s, axis=-1, keepdims=True)
        e = jnp.exp(s)
        p = e / jnp.sum(e, axis=-1, keepdims=True)
        att_scr[:, hs] = _bdot(p, mv_ref[0, :, hs])
    o_ref[...] = x + jnp.dot(att_scr[...].astype(BF16), wo_ref[...], preferred_element_type=F32)


def xattn_prompt(x, gain, w_q, mem_k, mem_v, w_o, layer, *, nb, tm=512):
    n = x.shape[0]
    tiles_per_b = n // nb // tm
    mem_k = mem_k.reshape(-1, N_MEM, D_MODEL)
    mem_v = mem_v.reshape(-1, N_MEM, D_MODEL)
    row = pl.BlockSpec((tm, D_MODEL), lambda i: (i, 0))
    wspec = pl.BlockSpec((D_MODEL, D_MODEL), lambda i: (0, 0))
    mspec = pl.BlockSpec((1, N_MEM, D_MODEL), lambda i: (layer * nb + i // tiles_per_b, 0, 0))
    return pl.pallas_call(
        _xattn_prompt_kernel, grid=(n // tm,),
        in_specs=[row, pl.BlockSpec((1, D_MODEL), lambda i: (0, 0)), wspec, mspec, mspec, wspec],
        out_specs=row, out_shape=jax.ShapeDtypeStruct((n, D_MODEL), F32),
        scratch_shapes=[pltpu.VMEM((tm, D_MODEL), F32)],
        compiler_params=_cparams("parallel"), name="xattn_prompt")(
            x, gain.reshape(1, D_MODEL), w_q, mem_k, mem_v, w_o)


def _xattn_step_kernel(q_ref, mk_ref, mv_ref, o_ref, *, tb):
    half = N_MEM // 2
    both = lambda z: jnp.concatenate([z, z], axis=1)
    fold = lambda z, op: op(z[:, :MEM_HEADS], z[:, MEM_HEADS:])
    for i in range(tb):
        k8 = jnp.concatenate([mk_ref[0, i, :half], mk_ref[0, i, half:]], axis=1)
        v8 = jnp.concatenate([mv_ref[0, i, :half], mv_ref[0, i, half:]], axis=1)
        q8 = jnp.concatenate([q_ref[i], q_ref[i]], axis=0)
        s = jnp.sum(k8 * q8[None], axis=-1, keepdims=True) * (MEM_HD ** -0.5)
        smax = both(fold(jnp.max(s, axis=0, keepdims=True), jnp.maximum))
        e = jnp.exp(s - smax)
        den = both(fold(jnp.sum(e, axis=0, keepdims=True), jnp.add))
        o8 = jnp.sum((e / den) * v8, axis=0)
        o_ref[i] = o8[:MEM_HEADS] + o8[MEM_HEADS:]


def xattn_step_job(q, cache_k, cache_v, layer, *, tb):
    n = q.shape[0]

    def specs(inner):
        qspec = pl.BlockSpec((tb, MEM_HEADS, MEM_HD), lambda i, j: (i * inner + j, 0, 0))
        cspec = pl.BlockSpec((1, tb, N_MEM, MEM_HEADS, MEM_HD), lambda i, j: (layer, i * inner + j, 0, 0, 0))
        return qspec, cspec

    return SideJob(
        body=functools.partial(_xattn_step_kernel, tb=tb),
        args=(q.reshape(n, MEM_HEADS, MEM_HD), cache_k, cache_v),
        in_specs=lambda inner: [specs(inner)[0], specs(inner)[1], specs(inner)[1]],
        out_shape=(jax.ShapeDtypeStruct((n, MEM_HEADS, MEM_HD), F32),),
        out_specs=lambda inner: [specs(inner)[0]],
        steps=n // tb)


def run_job(job, name):
    return pl.pallas_call(
        job.body, grid=(job.steps, 1), in_specs=job.in_specs(1), out_specs=job.out_specs(1),
        out_shape=list(job.out_shape), compiler_params=_cparams("parallel", "arbitrary"), name=name)(*job.args)


ROUTER_LANES = 128
NEG_BIG = -1e30


def _moe_gates(logits):
    lane = lax.broadcasted_iota(jnp.int32, logits.shape, 1)
    first = lambda mask: jnp.min(jnp.where(mask, lane, ROUTER_LANES), axis=-1, keepdims=True)
    is_c = lane < MOE_GROUPS
    lc = jnp.where(is_c, logits, NEG_BIG)
    mc = jnp.max(lc, axis=-1, keepdims=True)
    g_idx = first(lc == mc)
    p_g = 1.0 / jnp.sum(jnp.where(is_c, jnp.exp(lc - mc), 0.0), axis=-1, keepdims=True)
    fl = lane - MOE_GROUPS
    in_g = (fl >= 0) & (fl < MOE_EXPERTS) & ((fl // MOE_PER_GROUP) == g_idx)
    lf = jnp.where(in_g, logits, NEG_BIG)
    m1 = jnp.max(lf, axis=-1, keepdims=True)
    i1 = first(lf == m1)
    lf2 = jnp.where(lane == i1, NEG_BIG, lf)
    m2 = jnp.max(lf2, axis=-1, keepdims=True)
    i2 = first(lf2 == m2)
    e2 = jnp.exp(m2 - m1)
    w_top = 1.0 / (1.0 + e2)
    gate = p_g * (jnp.where(lane == i1, w_top, 0.0) + jnp.where(lane == i2, e2 * w_top, 0.0))
    return gate, g_idx


MOE_EPS = 2
MOE_STEPS = MOE_EXPERTS // MOE_EPS

SideJob = collections.namedtuple("SideJob", "body args in_specs out_shape out_specs steps")


def _with_side(main_kernel, n_in, n_out, side):
    ns_in, ns_out = len(side.args), len(side.out_shape)

    def kern(*refs):
        m_in = refs[:n_in]
        s_in = refs[n_in:n_in + ns_in]
        m_out = refs[n_in + ns_in:n_in + ns_in + n_out]
        s_out = refs[n_in + ns_in + n_out:n_in + ns_in + n_out + ns_out]
        scratch = refs[n_in + ns_in + n_out + ns_out:]
        main_kernel(*m_in, *m_out, *scratch, side=lambda: side.body(*s_in, *s_out))

    return kern


def _router_logits(h, wr_ref, br_ref):
    h_hi = h.astype(BF16)
    h_lo = (h - h_hi.astype(F32)).astype(BF16)
    acc = jnp.dot(h_hi, wr_ref[0], preferred_element_type=F32)
    acc = acc + jnp.dot(h_hi, wr_ref[1], preferred_element_type=F32)
    acc = acc + jnp.dot(h_lo, wr_ref[0], preferred_element_type=F32)
    return acc + br_ref[...]


def _experts_ffn(hb, gate, e0, w1_ref, w3_ref, w2_ref):
    lane = lax.broadcasted_iota(jnp.int32, gate.shape, 1)
    acc = None
    for e in range(MOE_EPS):
        a1 = jnp.dot(hb, w1_ref[e].astype(BF16), preferred_element_type=F32)
        a3 = jnp.dot(hb, w3_ref[e].astype(BF16), preferred_element_type=F32)
        ge = jnp.sum(jnp.where(lane == MOE_GROUPS + e0 + e, gate, 0.0), axis=-1, keepdims=True)
        hid = (jax.nn.silu(a1) * a3 * ge).astype(BF16)
        part = jnp.dot(hid, w2_ref[e].astype(BF16), preferred_element_type=F32)
        acc = part if acc is None else acc + part
    return acc


def _moe_kernel(x_ref, g_ref, wr_ref, br_ref, w1_ref, w3_ref, w2_ref, *rest, final_norm, side=None):
    if final_norm:
        fin_ref, o_ref, h_scr, gate_scr = rest
    else:
        o_ref, h_scr, gate_scr = rest
    step = pl.program_id(1)

    @pl.when(step == 0)
    def _():
        x = x_ref[...]
        h = _rms(x, g_ref[...])
        h_scr[...] = h.astype(BF16)
        gate_scr[...] = _moe_gates(_router_logits(h, wr_ref, br_ref))[0]
        o_ref[...] = x

    o_ref[...] += _experts_ffn(h_scr[...], gate_scr[...], step * MOE_EPS, w1_ref, w3_ref, w2_ref)
    if side is not None:
        side()

    if final_norm:
        @pl.when(step == MOE_STEPS - 1)
        def _():
            o_ref[...] = _rms(o_ref[...], fin_ref[...])


def moe_dense(x, gain, w_r, b_r, w1, w3, w2, layer, *, tm=512, final_gain=None, side=None):
    n = x.shape[0]
    tm = min(tm, n)
    gain = gain.reshape(1, D_MODEL)
    row = pl.BlockSpec((tm, D_MODEL), lambda i, s: (i, 0))
    const2 = lambda a: pl.BlockSpec(a.shape, lambda i, s: (0,) * a.ndim)
    soff = layer * MOE_STEPS
    wspec = pl.BlockSpec((MOE_EPS, D_MODEL, MOE_HIDDEN), lambda i, s: (soff + s, 0, 0))
    args = [x, gain, w_r, b_r, w1, w3, w2]
    in_specs = [row, const2(gain), const2(w_r), const2(b_r), wspec, wspec,
                pl.BlockSpec((MOE_EPS, MOE_HIDDEN, D_MODEL), lambda i, s: (soff + s, 0, 0))]
    if final_gain is not None:
        args.append(final_gain.reshape(1, D_MODEL))
        in_specs.append(const2(args[-1]))
    kern = functools.partial(_moe_kernel, final_norm=final_gain is not None)
    out_shape = [jax.ShapeDtypeStruct((n, D_MODEL), F32)]
    out_specs = [row]
    grid = (n // tm, MOE_STEPS)
    if side is not None:
        assert side.steps == grid[0] * grid[1]
        kern = _with_side(kern, len(args), 1, side)
        args += list(side.args)
        in_specs += side.in_specs(MOE_STEPS)
        out_shape += list(side.out_shape)
        out_specs += side.out_specs(MOE_STEPS)
    outs = pl.pallas_call(
        kern, grid=grid, in_specs=in_specs, out_specs=out_specs, out_shape=out_shape,
        scratch_shapes=[pltpu.VMEM((tm, D_MODEL), BF16), pltpu.VMEM((tm, ROUTER_LANES), F32)],
        compiler_params=_cparams("parallel", "arbitrary"), name="moe")(*args)
    return outs[0] if side is None else (outs[0], outs[1:])


def _group_weights(w1, w3, w2):
    ne = w1.shape[0] * MOE_EXPERTS
    return (w1.astype(BF16).reshape(ne, D_MODEL, MOE_HIDDEN), w3.astype(BF16).reshape(ne, D_MODEL, MOE_HIDDEN),
            w2.astype(BF16).reshape(ne, MOE_HIDDEN, D_MODEL))


def _router_params(w_rc, b_rc, w_rf, b_rf):
    pad = ROUTER_LANES - MOE_GROUPS - MOE_EXPERTS
    w_r = jnp.concatenate([w_rc, w_rf, jnp.zeros((D_MODEL, pad), F32)], axis=1).astype(F32)
    b_r = jnp.concatenate([b_rc, b_rf, jnp.zeros((pad,), F32)]).reshape(1, ROUTER_LANES).astype(F32)
    w_hi = w_r.astype(BF16)
    w_lo = (w_r - w_hi.astype(F32)).astype(BF16)
    return jnp.stack([w_hi, w_lo]), b_r


def _mem_kv_kernel(x_ref, g_ref, w_ref, kf_ref, vf_ref, kh_ref, vh_ref):
    h = _rms(x_ref[...], g_ref[0]).astype(BF16)
    for col, f_ref, h_ref in ((0, kf_ref, kh_ref), (D_MODEL, vf_ref, vh_ref)):
        acc = jnp.dot(h, w_ref[0, :, col:col + D_MODEL], preferred_element_type=F32)
        f_ref[0] = acc
        for hd in range(MEM_HEADS):
            h_ref[0, :, hd, :] = acc[:, hd * MEM_HD:(hd + 1) * MEM_HD]


def mem_kv(mem, gains, w_kv, *, tm=512):
    rows = mem.shape[0]
    nl = w_kv.shape[0]
    flat = jax.ShapeDtypeStruct((nl, rows, D_MODEL), F32)
    head = jax.ShapeDtypeStruct((nl, rows, MEM_HEADS, MEM_HD), F32)
    fspec = pl.BlockSpec((1, tm, D_MODEL), lambda l, i: (l, i, 0))
    hspec = pl.BlockSpec((1, tm, MEM_HEADS, MEM_HD), lambda l, i: (l, i, 0, 0))
    return pl.pallas_call(
        _mem_kv_kernel, grid=(nl, rows // tm),
        in_specs=[pl.BlockSpec((tm, D_MODEL), lambda l, i: (i, 0)),
                  pl.BlockSpec((1, 1, D_MODEL), lambda l, i: (l, 0, 0)),
                  pl.BlockSpec((1, D_MODEL, 2 * D_MODEL), lambda l, i: (l, 0, 0))],
        out_specs=(fspec, fspec, hspec, hspec), out_shape=(flat, flat, head, head),
        compiler_params=_cparams("parallel", "parallel"), name="mem_kv")(
            mem, gains.reshape(nl, 1, D_MODEL), w_kv)


def _forward(xp, xs, nbp, w, st, mem_k, mem_v, cache_k, cache_v):
    assert DEPTH == 2
    nbs = xs.shape[0]
    moe_tm = 1024
    moe_steps_p = (xp.shape[0] // moe_tm) * MOE_STEPS
    rwp = tuple(w[k][0] for k in ('rw_mu', 'rw_w0', 'rw_w2', 'rw_a0', 'rw_a2', 'rw_g2',
                                  'rw_k_k', 'rw_k_a', 'rw_r_k', 'rw_ln_w', 'rw_ln_b'))
    w_in0, w_out0 = w['w_in0_bf'][0], w['w_out0_bf'][0]
    moe = lambda x, layer, **kw: moe_dense(x, w['norm_ffn'][layer], *w['router'][layer], *w['moe_g'], layer, **kw)

    u, p_s = linear(xs, w_in0, gain=w['norm_mix'][0], splits=(S5_WIDTH, RW_PROJ))
    y_s5, s5r_s, s5i_s = s5_mixer(u.reshape(1, nbs, S5_WIDTH), st['s5_re'], st['s5_im'], w['s5p'][0],
                                  w['s5_d'][0], w['s5_w_glu'][0], tc=1)
    y_rw, rw_s = rwkv_step(p_s, st['shift'], st['rwkv'], rwp)
    xs = linear(y_s5.reshape(nbs, S5_WIDTH), w_out0[:S5_WIDTH], x2=y_rw, w2=w_out0[S5_WIDTH:], residual=xs)
    q_s = linear(xs, w['w_mq_bf'][0], gain=w['norm_mem'][0])

    zeros = lambda *shape: jnp.zeros(shape, F32)
    u, p_p = linear(xp, w_in0, gain=w['norm_mix'][0], splits=(S5_WIDTH, RW_PROJ), out_tmajor=True, batch=nbp)
    y_s5, s5r_p, s5i_p = s5_mixer(u, zeros(nbp, S5_STATE), zeros(nbp, S5_STATE), w['s5p'][0],
                                  w['s5_d'][0], w['s5_w_glu'][0], tc=128)
    rw_bs = 4
    rw_steps = (nbp // rw_bs) * (p_p.shape[0] // RW_HD)
    job = xattn_step_job(q_s, cache_k, cache_v, 0, tb=nbs // rw_steps)
    (y_rw, rw_p, sh_p), (att_s,) = rwkv_prompt(p_p, zeros(nbp, RW_PROJ), zeros(nbp, RW_HEADS, RW_HD, RW_HD),
                                               rwp, bs=rw_bs, side=job)
    xp = linear(y_s5, w_out0[:S5_WIDTH], x2=y_rw, w2=w_out0[S5_WIDTH:], residual=xp, x_tmajor=True)
    xp = xattn_prompt(xp, w['norm_mem'][0], w['w_mq_bf'][0], mem_k, mem_v, w['w_mo_bf'][0], 0, nb=nbp)

    xs = linear(att_s.reshape(nbs, D_MODEL), w['w_mo_bf'][0], residual=xs)
    xs = moe(xs, 0)
    q, k, v, g = linear(xs, w['w_in1_bf'][0], gain=w['norm_mix'][1], out_dtype=BF16, splits=(NQ, NQ, NV, NV))
    job = retention_step_job(q, k, v, g, st['ret'], pos0=float(PAST_LEN), tb=nbs // moe_steps_p)
    xp, (ret_s, y_ret) = moe(xp, 0, tm=moe_tm, side=job)
    xs = linear(y_ret.reshape(nbs, NV), w['w_out1_bf'][0], residual=xs)
    q_s = linear(xs, w['w_mq_bf'][1], gain=w['norm_mem'][1])

    xp, ret_p = retention_layer_prompt(xp, w['norm_mix'][1], w['w_in1_bf'][0], w['w_out1_bf'][0], nb=nbp)
    xp = xattn_prompt(xp, w['norm_mem'][1], w['w_mq_bf'][1], mem_k, mem_v, w['w_mo_bf'][1], 1, nb=nbp)
    job = xattn_step_job(q_s, cache_k, cache_v, 1, tb=nbs // moe_steps_p)
    y_p, (att_s,) = moe(xp, 1, tm=moe_tm, final_gain=w['norm_final'], side=job)
    xs = linear(att_s.reshape(nbs, D_MODEL), w['w_mo_bf'][1], residual=xs)
    y_s = moe(xs, 1, final_gain=w['norm_final'])

    grp = lambda z, nb: z.reshape(1, nb, S5_GROUPS, S5_N)
    prompt_out = (y_p, grp(s5r_p, nbp), grp(s5i_p, nbp), rw_p[None], sh_p[None], ret_p[None])
    sample_out = (y_s, grp(s5r_s, nbs), grp(s5i_s, nbs), rw_s[None], p_s[None], ret_s[None])
    return prompt_out, sample_out


def kernel(x_prompt, x_sample, mem_prompt, state_s5_re, state_s5_im, state_rwkv, state_shift, state_ret, cache_mem_k, cache_mem_v, norm_mix, norm_mem, norm_ffn, norm_final, w_in0, w_out0, s5_a_re, s5_a_im, s5_b_re, s5_b_im, s5_c_re, s5_c_im, s5_d, s5_log_dt, s5_w_glu, rw_mu, rw_w0, rw_w2, rw_a0, rw_a2, rw_g2, rw_k_k, rw_k_a, rw_r_k, rw_ln_w, rw_ln_b, w_in1, w_out1, mem_norm, w_mq, w_mk, w_mv, w_mo, moe_w_rc, moe_b_rc, moe_w_rf, moe_b_rf, moe_w1, moe_w3, moe_w2):
    w = dict(norm_mix=norm_mix, norm_mem=norm_mem, norm_ffn=norm_ffn, norm_final=norm_final,
             w_in0=w_in0, w_out0=w_out0, s5_a_re=s5_a_re, s5_a_im=s5_a_im, s5_b_re=s5_b_re, s5_b_im=s5_b_im,
             s5_c_re=s5_c_re, s5_c_im=s5_c_im, s5_d=s5_d, s5_log_dt=s5_log_dt, s5_w_glu=s5_w_glu,
             rw_mu=rw_mu, rw_w0=rw_w0, rw_w2=rw_w2, rw_a0=rw_a0, rw_a2=rw_a2, rw_g2=rw_g2,
             rw_k_k=rw_k_k, rw_k_a=rw_k_a, rw_r_k=rw_r_k, rw_ln_w=rw_ln_w, rw_ln_b=rw_ln_b,
             w_in1=w_in1, w_out1=w_out1, w_mq=w_mq, w_mo=w_mo,
             moe_w_rc=moe_w_rc, moe_b_rc=moe_b_rc, moe_w_rf=moe_w_rf, moe_b_rf=moe_b_rf,
             moe_w1=moe_w1, moe_w3=moe_w3, moe_w2=moe_w2)
    nbp, t_len, _ = x_prompt.shape
    nbs = x_sample.shape[0]
    n_even, n_odd = state_s5_re.shape[0], state_ret.shape[0]
    for name in ('w_in0', 'w_out0', 'w_in1', 'w_out1', 'w_mq', 'w_mo'):
        w[name + '_bf'] = w[name].astype(BF16)
    w['s5p'] = [_s5_params(s5_a_re[i], s5_a_im[i], s5_b_re[i], s5_b_im[i], s5_c_re[i], s5_c_im[i], s5_log_dt[i])
                for i in range(n_even)]
    w['router'] = [_router_params(moe_w_rc[l], moe_b_rc[l], moe_w_rf[l], moe_b_rf[l]) for l in range(DEPTH)]
    w['moe_g'] = _group_weights(moe_w1, moe_w3, moe_w2)

    mem = mem_prompt.reshape(nbp * N_MEM, D_MODEL)
    w_kv = jnp.concatenate([w_mk, w_mv], axis=2).astype(BF16)
    mk, mv, mk_h, mv_h = mem_kv(mem, mem_norm, w_kv)
    mem_k_l = mk.reshape(DEPTH, nbp, N_MEM, D_MODEL)
    mem_v_l = mv.reshape(DEPTH, nbp, N_MEM, D_MODEL)
    mem_k_p = mk_h.reshape(DEPTH, nbp, N_MEM, MEM_HEADS, MEM_HD)
    mem_v_p = mv_h.reshape(DEPTH, nbp, N_MEM, MEM_HEADS, MEM_HD)

    assert n_even == 1 and n_odd == 1
    st = dict(s5_re=state_s5_re.reshape(nbs, S5_STATE), s5_im=state_s5_im.reshape(nbs, S5_STATE),
              rwkv=state_rwkv[0], shift=state_shift[0], ret=state_ret[0])
    (y_p, s5r_p, s5i_p, rw_p, sh_p, ret_p), (y_s, s5r_s, s5i_s, rw_s, sh_s, ret_s) = _forward(
        x_prompt.reshape(nbp * t_len, D_MODEL), x_sample.reshape(nbs, D_MODEL), nbp, w, st,
        mem_k_l, mem_v_l, cache_mem_k, cache_mem_v)
    return (y_p.reshape(nbp, t_len, D_MODEL), y_s.reshape(nbs, 1, D_MODEL),
            s5r_p, s5i_p, rw_p, sh_p, ret_p, mem_k_p, mem_v_p, s5r_s, s5i_s, rw_s, sh_s, ret_s)
```

```python
import collections
import functools
import math

import jax
import jax.numpy as jnp
from jax import lax
from jax.experimental import pallas as pl
from jax.experimental.pallas import tpu as pltpu

F32 = jnp.float32
BF16 = jnp.bfloat16

D_MODEL = 1024
DEPTH = 2
PAST_LEN = 16384
S5_WIDTH = 512
S5_GROUP = 16
S5_GROUPS = 32
S5_N = 64
S5_STATE = S5_GROUPS * S5_N
S5_GBLK = 8
RW_WIDTH = 512
RW_HD = 64
RW_HEADS = 8
RW_LORA = 256
RW_PROJ = 3 * RW_WIDTH + RW_LORA
IN0 = S5_WIDTH + RW_PROJ
RET_DK = 256
RET_HEADS = 4
RET_DV = 512
RET_CHUNK = 256
NQ = RET_HEADS * RET_DK
NV = RET_HEADS * RET_DV
IN1 = 2 * NQ + 2 * NV
N_MEM = 256
MEM_HEADS = 4
MEM_HD = 256
MOE_GROUPS = 4
MOE_PER_GROUP = 4
MOE_EXPERTS = 16
MOE_HIDDEN = 256
NORM_EPS = 1e-6
RW_GN_EPS = 64e-5
ROPE_BASE = 10000.0

VMEM_LIMIT = 56 * 1024 * 1024


def _cparams(*sem):
    return pltpu.CompilerParams(dimension_semantics=sem, vmem_limit_bytes=VMEM_LIMIT)


def _bdot(a, b):
    return jnp.dot(a.astype(BF16), b.astype(BF16), preferred_element_type=F32)


def _dot_nt(a, b):
    return lax.dot_general(a.astype(BF16), b.astype(BF16), (((1,), (1,)), ((), ())),
                           preferred_element_type=F32)


def _dot_tn(a, b):
    return lax.dot_general(a.astype(BF16), b.astype(BF16), (((0,), (0,)), ((), ())),
                           preferred_element_type=F32)


def _split3(x):
    hi = x.astype(BF16)
    r1 = x - hi.astype(F32)
    mid = r1.astype(BF16)
    lo = (r1 - mid.astype(F32)).astype(BF16)
    return hi, mid, lo


def _dot_exact_rhs(x, m_bf16, passes=3):
    hi, mid, lo = _split3(x)
    acc = jnp.dot(hi, m_bf16, preferred_element_type=F32)
    if passes > 1:
        acc = acc + jnp.dot(mid, m_bf16, preferred_element_type=F32)
    if passes > 2:
        acc = acc + jnp.dot(lo, m_bf16, preferred_element_type=F32)
    return acc


def _rms(x, g):
    ms = jnp.mean(x * x, axis=-1, keepdims=True)
    return x * lax.rsqrt(ms + NORM_EPS) * g


def _linear_kernel(*refs, norm, two, res):
    it = iter(refs)
    x_ref = next(it)
    g_ref = next(it) if norm else None
    w_ref = next(it)
    x2_ref = next(it) if two else None
    w2_ref = next(it) if two else None
    r_ref = next(it) if res else None
    o_refs = list(it)
    x = x_ref[...].astype(F32)
    if norm:
        x = _rms(x, g_ref[...])
    xb = x.astype(BF16)
    x2b = x2_ref[...].astype(BF16) if two else None
    col = 0
    for o_ref in o_refs:
        m = o_ref.shape[-1]
        step = next((s for s in (512, 256) if m % s == 0), m)
        for j in range(m // step):
            sl = slice(col + j * step, col + (j + 1) * step)
            acc = jnp.dot(xb, w_ref[:, sl], preferred_element_type=F32)
            if two:
                acc = acc + jnp.dot(x2b, w2_ref[:, sl], preferred_element_type=F32)
            if res:
                acc = acc + r_ref[:, sl]
            o_ref[:, j * step:(j + 1) * step] = acc.astype(o_ref.dtype)
        col += m


def _row_spec(tm, width, tmajor_b):
    if tmajor_b is None:
        return pl.BlockSpec((tm, width), lambda i: (i, 0))
    nb, tiles_per_b = tmajor_b
    return pl.BlockSpec((tm, width), lambda i: (i % tiles_per_b, i // tiles_per_b))


def linear(x, w, *, gain=None, x2=None, w2=None, residual=None, out_dtype=F32, tm=512,
           x_tmajor=False, out_tmajor=False, batch=None, splits=None, name="linear"):
    if x_tmajor:
        t_len, nb, k = x.shape
        n = t_len * nb
    else:
        n, k = x.shape
        nb = batch
        t_len = n // nb if nb else None
    m = w.shape[1]
    tm = min(tm, n if not (x_tmajor or out_tmajor) else t_len)
    assert n % tm == 0
    tiles_per_b = (t_len // tm) if (x_tmajor or out_tmajor) else None
    args, specs = [], []

    def add_rows(a, tmajor):
        width = a.shape[-1]
        args.append(a.reshape(t_len, nb * width) if tmajor else a)
        specs.append(_row_spec(tm, width, (nb, tiles_per_b) if tmajor else None))

    add_rows(x, x_tmajor)
    if gain is not None:
        args.append(gain.reshape(1, k).astype(F32))
        specs.append(pl.BlockSpec((1, k), lambda i: (0, 0)))
    args.append(w)
    specs.append(pl.BlockSpec(w.shape, lambda i: (0, 0)))
    if x2 is not None:
        add_rows(x2, x_tmajor)
        args.append(w2)
        specs.append(pl.BlockSpec(w2.shape, lambda i: (0, 0)))
    if residual is not None:
        add_rows(residual, False)
    widths = tuple(splits) if splits else (m,)
    assert sum(widths) == m
    if out_tmajor:
        out_shape = [jax.ShapeDtypeStruct((t_len, nb * mw), out_dtype) for mw in widths]
    else:
        out_shape = [jax.ShapeDtypeStruct((n, mw), out_dtype) for mw in widths]
    out_specs = [_row_spec(tm, mw, (nb, tiles_per_b) if out_tmajor else None) for mw in widths]
    kern = functools.partial(_linear_kernel, norm=gain is not None, two=x2 is not None,
                             res=residual is not None)
    outs = pl.pallas_call(
        kern, grid=(n // tm,), in_specs=specs, out_specs=out_specs, out_shape=out_shape,
        compiler_params=_cparams("parallel"), name=name)(*args)
    if out_tmajor:
        outs = [o.reshape(t_len, nb, mw) for o, mw in zip(outs, widths)]
    return outs if splits else outs[0]


def _s5_kernel(u_ref, h_re_ref, h_im_ref, abar_re_ref, abar_im_ref, bb_re_ref, bb_im_ref,
               cc_re_ref, cc_im_ref, d_ref, wglu_ref, y_ref, s_re_ref, s_im_ref,
               x_re, x_im, st_re, st_im, il_scr, *, tc, nb, flat):
    c = pl.program_id(0)
    nlb = S5_WIDTH // 128
    rows = tc * nb
    nblk = S5_GROUPS // S5_GBLK
    bw_in = S5_GBLK * S5_GROUP
    bw_st = S5_GBLK * S5_N

    @pl.when(c == 0)
    def _():
        st_re[...] = h_re_ref[...]
        st_im[...] = h_im_ref[...]

    if flat:
        for b in range(nb):
            for j in range(nlb):
                il_scr[j, pl.ds(b, tc, stride=nb), :] = u_ref[:, b * S5_WIDTH + j * 128:b * S5_WIDTH + (j + 1) * 128]
        u = jnp.concatenate([il_scr[j] for j in range(nlb)], axis=-1)
    else:
        u = u_ref[...].reshape(rows, S5_WIDTH)
    ub = u.astype(BF16)
    for gb in range(nblk):
        ui = ub[:, gb * bw_in:(gb + 1) * bw_in]
        x_re[:, gb * bw_st:(gb + 1) * bw_st] = jnp.dot(ui, bb_re_ref[gb], preferred_element_type=F32)
        x_im[:, gb * bw_st:(gb + 1) * bw_st] = jnp.dot(ui, bb_im_ref[gb], preferred_element_type=F32)

    lane_blk = 1024
    for lb in range(S5_STATE // lane_blk):
        sl = slice(lb * lane_blk, (lb + 1) * lane_blk)
        ar = jnp.broadcast_to(abar_re_ref[:, sl], (nb, lane_blk))
        ai = jnp.broadcast_to(abar_im_ref[:, sl], (nb, lane_blk))

        def body(t, carry, sl=sl, ar=ar, ai=ai):
            xr, xi = carry
            r0 = pl.multiple_of(t * nb, nb)
            br = x_re[pl.ds(r0, nb), sl]
            bi = x_im[pl.ds(r0, nb), sl]
            nr = ar * xr - ai * xi + br
            ni = ar * xi + ai * xr + bi
            x_re[pl.ds(r0, nb), sl] = nr
            x_im[pl.ds(r0, nb), sl] = ni
            return nr, ni

        fr, fi = lax.fori_loop(0, tc, body, (st_re[:, sl], st_im[:, sl]), unroll=min(tc, 4))
        st_re[:, sl] = fr
        st_im[:, sl] = fi

    for gb in range(nblk):
        xr = x_re[:, gb * bw_st:(gb + 1) * bw_st].astype(BF16)
        xi = x_im[:, gb * bw_st:(gb + 1) * bw_st].astype(BF16)
        yb = (jnp.dot(xr, cc_re_ref[gb], preferred_element_type=F32)
              - jnp.dot(xi, cc_im_ref[gb], preferred_element_type=F32))
        cs = slice(gb * bw_in, (gb + 1) * bw_in)
        yb = yb + d_ref[:, cs] * u[:, cs]
        x_re[:, cs] = jax.nn.gelu(yb)
    y = x_re[:, :S5_WIDTH]
    y = y * jax.nn.sigmoid(jnp.dot(y.astype(BF16), wglu_ref[...], preferred_element_type=F32))
    if flat:
        for j in range(nlb):
            il_scr[j] = y[:, j * 128:(j + 1) * 128]
        for b in range(nb):
            for j in range(nlb):
                y_ref[:, b * S5_WIDTH + j * 128:b * S5_WIDTH + (j + 1) * 128] = (
                    il_scr[j, pl.ds(b, tc, stride=nb), :].astype(y_ref.dtype))
    else:
        y_ref[...] = y.reshape(y_ref.shape).astype(y_ref.dtype)

    @pl.when(c == pl.num_programs(0) - 1)
    def _():
        s_re_ref[...] = st_re[...]
        s_im_ref[...] = st_im[...]


def _s5_params(a_re, a_im, b_re, b_im, c_re, c_im, log_dt):
    dt = jnp.exp(log_dt.astype(F32))[:, None]
    ar, ai = a_re.astype(F32), a_im.astype(F32)
    mag = jnp.exp(dt * ar)
    abar_re, abar_im = mag * jnp.cos(dt * ai), mag * jnp.sin(dt * ai)
    den = ar * ar + ai * ai
    nr = abar_re - 1.0
    coef_re = (nr * ar + abar_im * ai) / den
    coef_im = (abar_im * ar - nr * ai) / den
    cr, ci = coef_re[..., None], coef_im[..., None]
    brf, bif = b_re.astype(F32), b_im.astype(F32)
    bb_re = cr * brf - ci * bif
    bb_im = cr * bif + ci * brf
    nblk = S5_GROUPS // S5_GBLK
    eye = jnp.eye(S5_GBLK, dtype=F32)

    def blockdiag_in(bb):
        t = jnp.transpose(bb, (0, 2, 1)).reshape(nblk, S5_GBLK, S5_GROUP, S5_N)
        m = jnp.einsum('kgcn,gh->kgchn', t, eye)
        return m.reshape(nblk, S5_GBLK * S5_GROUP, S5_GBLK * S5_N).astype(BF16)

    def blockdiag_out(cc):
        t = jnp.transpose(cc.astype(F32), (0, 2, 1)).reshape(nblk, S5_GBLK, S5_N, S5_GROUP)
        m = jnp.einsum('khnc,hg->khngc', t, eye)
        return m.reshape(nblk, S5_GBLK * S5_N, S5_GBLK * S5_GROUP).astype(BF16)

    return (abar_re.reshape(1, S5_STATE), abar_im.reshape(1, S5_STATE),
            blockdiag_in(bb_re), blockdiag_in(bb_im), blockdiag_out(c_re), blockdiag_out(c_im))


def s5_mixer(u_tm, h_re, h_im, params, d_skip, w_glu, *, tc):
    t_len, nb, _ = u_tm.shape
    abar_re, abar_im, bb_re, bb_im, cc_re, cc_im = params
    tc = min(tc, t_len)
    assert t_len % tc == 0 and nb % 8 == 0
    rows = tc * nb
    flat = t_len > 1
    full = lambda a: pl.BlockSpec(a.shape, lambda c: (0,) * a.ndim)
    if flat:
        u_arg = u_tm.reshape(t_len, nb * S5_WIDTH)
        io_spec = pl.BlockSpec((tc, nb * S5_WIDTH), lambda c: (c, 0))
        y_shape = jax.ShapeDtypeStruct((t_len, nb * S5_WIDTH), BF16)
    else:
        u_arg = u_tm
        io_spec = pl.BlockSpec((tc, nb, S5_WIDTH), lambda c: (c, 0, 0))
        y_shape = jax.ShapeDtypeStruct((t_len, nb, S5_WIDTH), BF16)
    args = (u_arg, h_re, h_im, abar_re, abar_im, bb_re, bb_im, cc_re, cc_im,
            d_skip.reshape(1, S5_WIDTH).astype(F32), w_glu.astype(BF16))
    in_specs = [io_spec] + [full(a) for a in args[1:]]
    st_shape = jax.ShapeDtypeStruct((nb, S5_STATE), F32)
    st_spec = pl.BlockSpec((nb, S5_STATE), lambda c: (0, 0))
    scratch = [pltpu.VMEM((rows, S5_STATE), F32), pltpu.VMEM((rows, S5_STATE), F32),
               pltpu.VMEM((nb, S5_STATE), F32), pltpu.VMEM((nb, S5_STATE), F32),
               pltpu.VMEM((S5_WIDTH // 128, rows if flat else 8, 128), F32)]
    y, s_re, s_im = pl.pallas_call(
        functools.partial(_s5_kernel, tc=tc, nb=nb, flat=flat), grid=(t_len // tc,), in_specs=in_specs,
        out_specs=(io_spec, st_spec, st_spec), out_shape=(y_shape, st_shape, st_shape),
        scratch_shapes=scratch, compiler_params=_cparams("arbitrary"), name="s5_mixer")(*args)
    return y.reshape(t_len, nb, S5_WIDTH), s_re, s_im


def _head_ones():
    i = lax.broadcasted_iota(jnp.int32, (RW_WIDTH, RW_WIDTH), 0) // RW_HD
    j = lax.broadcasted_iota(jnp.int32, (RW_WIDTH, RW_WIDTH), 1) // RW_HD
    return jnp.where(i == j, 1.0, 0.0).astype(BF16)


def _softplus(z):
    return jnp.maximum(z, 0.0) + jnp.log1p(jnp.exp(-jnp.abs(z)))


def _rw_prep(p, p_prev, prm, ones_bd):
    mu, w0, w2, a0, a2, g2, k_k, k_a = prm
    xm = p + (p_prev - p) * mu
    o1, o2, o3 = RW_WIDTH, 2 * RW_WIDTH, 3 * RW_WIDTH
    r, k, v = xm[:, :o1], xm[:, o1:o2], xm[:, o2:o3]
    wd, ad, gd = xm[:, o3:o3 + 64], xm[:, o3 + 64:o3 + 128], xm[:, o3 + 128:]
    w = -_softplus(-(w0 + _bdot(jnp.tanh(wd), w2))) - 0.5
    lw = -jnp.exp(w)
    a = jax.nn.sigmoid(a0 + _bdot(ad, a2))
    g = _bdot(jax.nn.sigmoid(gd), g2)
    kk = k * k_k
    ss = _dot_exact_rhs(kk * kk, ones_bd, passes=1)
    kk = kk / jnp.maximum(jnp.sqrt(ss), 1e-12)
    k = k * (1.0 + (a - 1.0) * k_a)
    return r, lw, k, v, -kk, kk * a, g


def _rw_post(o, r, k, v, g, r_k, ln_w, ln_b, ones_bd):
    inv = 1.0 / RW_HD
    mean = _dot_exact_rhs(o, ones_bd, passes=2) * inv
    d = o - mean
    var = _dot_exact_rhs(d * d, ones_bd, passes=1) * inv
    on = d * lax.rsqrt(var + RW_GN_EPS) * ln_w + ln_b
    bonus = _dot_exact_rhs(r * k * r_k, ones_bd, passes=1) * v
    return (on + bonus) * g


def _rw_chunk_kernel(p_ref, shift_ref, h0_ref, mu_ref, w0_ref, w2_ref, a0_ref, a2_ref, g2_ref,
                     kk_ref, ka_ref, rk_ref, lnw_ref, lnb_ref,
                     y_ref, hfin_ref, shout_ref, prev_scr, h_scr, o_scr, *, c_len, bs, side=None):
    c = pl.program_id(1)
    nc = pl.num_programs(1)
    cl = c_len

    @pl.when(c == 0)
    def _():
        prev_scr[...] = shift_ref[:, 0, :]
        h_scr[...] = h0_ref[...]

    ones_bd = _head_ones()
    row = lax.broadcasted_iota(jnp.int32, (cl, RW_PROJ), 0)
    ps, pprevs = [], []
    for bi in range(bs):
        p = p_ref[:, bi * RW_PROJ:(bi + 1) * RW_PROJ]
        pprevs.append(jnp.where(row == 0, prev_scr[bi:bi + 1, :], pltpu.roll(p, 1, 0)))
        ps.append(p)
    p_all = jnp.concatenate(ps, axis=0) if bs > 1 else ps[0]
    pprev_all = jnp.concatenate(pprevs, axis=0) if bs > 1 else pprevs[0]
    prm = (mu_ref[...], w0_ref[...], w2_ref[...], a0_ref[...], a2_ref[...], g2_ref[...],
           kk_ref[...], ka_ref[...])
    r, lw, k, v, a, b, g = _rw_prep(p_all, pprev_all, prm, ones_bd)

    ti = lax.broadcasted_iota(jnp.int32, (cl, cl), 0)
    si = lax.broadcasted_iota(jnp.int32, (cl, cl), 1)
    lmat = jnp.where(ti >= si, 1.0, 0.0).astype(BF16)
    eye = jnp.where(ti == si, 1.0, 0.0)
    mi = lax.broadcasted_iota(jnp.int32, (2 * cl, 3 * cl), 0)
    mj = lax.broadcasted_iota(jnp.int32, (2 * cl, 3 * cl), 1)
    t_row = jnp.where(mi >= cl, mi - cl, mi)
    s_col = jnp.where(mj < cl, mj, jnp.where(mj >= 2 * cl, mj - 2 * cl, -4 * cl))
    keep = (t_row - s_col) >= jnp.where(mi >= cl, 0, 1)
    eye_bf = eye.astype(BF16)

    lhs_l, rhs_l, vh_l, hcat_l, kb_l, etot_l = [], [], [], [], [], []
    for bi in range(bs):
        rs = slice(bi * cl, (bi + 1) * cl)
        lw_b = lw[rs]
        l_hi, l_mid, l_lo = _split3(lw_b)
        cum = (jnp.dot(lmat, l_hi, preferred_element_type=F32)
               + jnp.dot(lmat, l_mid, preferred_element_type=F32)
               + jnp.dot(lmat, l_lo, preferred_element_type=F32))
        tot = cum[cl - 1:cl, :]
        e_neg = jnp.exp(-cum)
        e_rem = jnp.exp(tot - cum)
        at = (a[rs] * jnp.exp(cum - lw_b)).astype(BF16)
        rt = (r[rs] * jnp.exp(cum)).astype(BF16)
        bt = (b[rs] * e_neg).astype(BF16)
        kt = (k[rs] * e_neg).astype(BF16)
        bh = (b[rs] * e_rem).astype(BF16)
        kh = (k[rs] * e_rem).astype(BF16)
        e_tot = jnp.exp(tot)
        vb = v[rs].astype(BF16)
        for h in range(RW_HEADS):
            hs = slice(h * RW_HD, (h + 1) * RW_HD)
            lhs_l.append(jnp.concatenate([at[:, hs], rt[:, hs]], axis=0))
            rhs_l.append(jnp.concatenate([kt[:, hs], eye_bf, bt[:, hs]], axis=0))
            vh_l.append(vb[:, hs])
            kb_l.append(jnp.concatenate([kh[:, hs], bh[:, hs]], axis=0))
            etot_l.append(jnp.sum(eye * e_tot[:, hs], axis=-1, keepdims=True))
            hcat_l.append(h_scr[bi, h])

    nitem = bs * RW_HEADS
    items = range(nitem)
    aa_l = [jnp.where(keep, _dot_nt(lhs_l[i], rhs_l[i]), 0.0).astype(BF16) for i in items]
    pw_l = [aa_l[i][:cl, 2 * cl:] for i in items]
    tinv_l = [eye_bf + pw_l[i] for i in items]
    for _ in range(int(math.log2(cl)) - 1):
        pw_l = [jnp.dot(pw_l[i], pw_l[i], preferred_element_type=F32).astype(BF16) for i in items]
        tinv_l = [jnp.dot(tinv_l[i], eye_bf + pw_l[i], preferred_element_type=F32).astype(BF16) for i in items]
    vh_cat = [jnp.concatenate([vh_l[i], hcat_l[i].astype(BF16)], axis=0) for i in items]
    x1_l = [jnp.dot(aa_l[i][:cl, :2 * cl], vh_cat[i], preferred_element_type=F32).astype(BF16) for i in items]
    u_l = [jnp.dot(tinv_l[i], x1_l[i], preferred_element_type=F32).astype(BF16) for i in items]
    o_l = [jnp.dot(aa_l[i][cl:, :], jnp.concatenate([vh_cat[i], u_l[i]], axis=0),
                   preferred_element_type=F32) for i in items]
    hn_l = [hcat_l[i] * etot_l[i]
            + lax.dot_general(kb_l[i], jnp.concatenate([vh_l[i], u_l[i]], axis=0), (((0,), (0,)), ((), ())),
                              preferred_element_type=F32) for i in items]

    for bi in range(bs):
        for h in range(RW_HEADS):
            i = bi * RW_HEADS + h
            o_scr[bi * cl:(bi + 1) * cl, h * RW_HD:(h + 1) * RW_HD] = o_l[i]
            h_scr[bi, h] = hn_l[i]
        prev_scr[bi:bi + 1, :] = ps[bi][cl - 1:cl, :]

    y = _rw_post(o_scr[...], r, k, v, g, rk_ref[...], lnw_ref[...], lnb_ref[...], ones_bd)
    for bi in range(bs):
        y_ref[:, bi * RW_WIDTH:(bi + 1) * RW_WIDTH] = y[bi * cl:(bi + 1) * cl].astype(y_ref.dtype)
    if side is not None:
        side()

    @pl.when(c == nc - 1)
    def _():
        hfin_ref[...] = h_scr[...]
        for bi in range(bs):
            shout_ref[bi] = ps[bi][cl - 1:cl, :]


def _rw_param_args(mu, w0, w2, a0, a2, g2, k_k, k_a, r_k, ln_w, ln_b):
    row = lambda z: z.reshape(1, -1).astype(F32)
    return (row(mu), row(w0), w2.astype(BF16), row(a0), a2.astype(BF16), g2.astype(BF16),
            row(k_k), row(k_a), row(r_k), row(ln_w), row(ln_b))


def rwkv_prompt(p_tm, shift, s0, params, *, bs=4, side=None):
    c_len = RW_HD
    t_len, nb, _ = p_tm.shape
    assert t_len % c_len == 0 and nb % bs == 0
    prm = _rw_param_args(*params)
    const = lambda a: pl.BlockSpec(a.shape, lambda b, c: (0,) * a.ndim)
    st_spec = pl.BlockSpec((bs, RW_HEADS, RW_HD, RW_HD), lambda b, c: (b, 0, 0, 0))
    sh_spec = pl.BlockSpec((bs, 1, RW_PROJ), lambda b, c: (b, 0, 0))
    in_specs = [pl.BlockSpec((c_len, bs * RW_PROJ), lambda b, c: (c, b)), sh_spec, st_spec] + [const(a) for a in prm]
    out_shape = (jax.ShapeDtypeStruct((t_len, nb * RW_WIDTH), BF16),
                 jax.ShapeDtypeStruct((nb, RW_HEADS, RW_HD, RW_HD), F32),
                 jax.ShapeDtypeStruct((nb, 1, RW_PROJ), F32))
    out_specs = (pl.BlockSpec((c_len, bs * RW_WIDTH), lambda b, c: (c, b)), st_spec, sh_spec)
    scratch = [pltpu.VMEM((bs, RW_PROJ), F32), pltpu.VMEM((bs, RW_HEADS, RW_HD, RW_HD), F32),
               pltpu.VMEM((bs * c_len, RW_WIDTH), F32)]
    h0 = jnp.swapaxes(s0, -1, -2)
    args = [p_tm.reshape(t_len, nb * RW_PROJ), shift.reshape(nb, 1, RW_PROJ), h0, *prm]
    kern = functools.partial(_rw_chunk_kernel, c_len=c_len, bs=bs)
    grid = (nb // bs, t_len // c_len)
    out_shape, out_specs = list(out_shape), list(out_specs)
    if side is not None:
        assert side.steps == grid[0] * grid[1]
        kern = _with_side(kern, len(args), 3, side)
        args += list(side.args)
        in_specs += side.in_specs(grid[1])
        out_shape += list(side.out_shape)
        out_specs += side.out_specs(grid[1])
    outs = pl.pallas_call(
        kern, grid=grid, in_specs=in_specs, out_specs=out_specs, out_shape=out_shape,
        scratch_shapes=scratch, compiler_params=_cparams("parallel", "arbitrary"), name="rwkv_prompt")(*args)
    y, h_fin, sh = outs[:3]
    res = (y.reshape(t_len, nb, RW_WIDTH), jnp.swapaxes(h_fin, -1, -2), sh.reshape(nb, RW_PROJ))
    return res if side is None else (res, outs[3:])


def _rw_step_prep_kernel(p_ref, shift_ref, mu_ref, w0_ref, w2_ref, a0_ref, a2_ref, g2_ref, kk_ref, ka_ref,
                         r_ref, k_ref, v_ref, g_ref, rt_ref, wt_ref, kt_ref, at_ref, bt_ref, vt_ref):
    prm = (mu_ref[...], w0_ref[...], w2_ref[...], a0_ref[...], a2_ref[...], g2_ref[...],
           kk_ref[...], ka_ref[...])
    r, lw, k, v, a, b, g = _rw_prep(p_ref[...], shift_ref[...], prm, _head_ones())
    r_ref[...] = r
    k_ref[...] = k
    v_ref[...] = v
    g_ref[...] = g
    rt_ref[...] = r.T
    wt_ref[...] = jnp.exp(lw).T
    kt_ref[...] = k.T
    at_ref[...] = a.T
    bt_ref[...] = b.T
    vt_ref[...] = v.T


def _rw_step_core_kernel(s_ref, r_ref, w_ref, k_ref, a_ref, b_ref, v_ref, s_out_ref, o_ref):
    r, w, k, a, b = r_ref[0], w_ref[0], k_ref[0], a_ref[0], b_ref[0]
    for j in range(s_ref.shape[1]):
        s = s_ref[0, j]
        sa = jnp.sum(s * a, axis=0, keepdims=True)
        s_new = s * w + sa * b + v_ref[0, j:j + 1, :] * k
        s_out_ref[0, j] = s_new
        o_ref[0, j:j + 1, :] = jnp.sum(s_new * r, axis=0, keepdims=True)


def _rw_step_post_kernel(ot_ref, r_ref, k_ref, v_ref, g_ref, rk_ref, lnw_ref, lnb_ref, y_ref):
    y_ref[...] = _rw_post(ot_ref[...].T, r_ref[...], k_ref[...], v_ref[...], g_ref[...],
                          rk_ref[...], lnw_ref[...], lnb_ref[...], _head_ones()).astype(y_ref.dtype)


def rwkv_step(p, shift, s0, params, *, vb=32):
    n = p.shape[0]
    prm = _rw_param_args(*params)
    vec = jax.ShapeDtypeStruct((n, RW_WIDTH), F32)
    vec_t = jax.ShapeDtypeStruct((RW_WIDTH, n), F32)
    r, k, v, g, rt, wt, kt, at, bt, vt = pl.pallas_call(
        _rw_step_prep_kernel, out_shape=(vec,) * 4 + (vec_t,) * 6, name="rwkv_step_prep")(p, shift, *prm[:8])
    heads = lambda z: z.reshape(RW_HEADS, RW_HD, n)
    k_spec = pl.BlockSpec((1, RW_HD, n), lambda h, j: (h, 0, 0))
    v_spec = pl.BlockSpec((1, vb, n), lambda h, j: (h, j, 0))
    st_spec = pl.BlockSpec((1, vb, RW_HD, n), lambda h, j: (h, j, 0, 0))
    st = jnp.transpose(s0, (1, 2, 3, 0))
    s_new, ot = pl.pallas_call(
        _rw_step_core_kernel, grid=(RW_HEADS, RW_HD // vb),
        in_specs=[st_spec] + [k_spec] * 5 + [v_spec], out_specs=(st_spec, v_spec),
        out_shape=(jax.ShapeDtypeStruct(st.shape, F32), jax.ShapeDtypeStruct((RW_HEADS, RW_HD, n), F32)),
        compiler_params=_cparams("parallel", "parallel"), name="rwkv_step_core")(
            st, heads(rt), heads(wt), heads(kt), heads(at), heads(bt), heads(vt))
    y = pl.pallas_call(
        _rw_step_post_kernel, out_shape=jax.ShapeDtypeStruct((n, RW_WIDTH), BF16), name="rwkv_step_post")(
            ot.reshape(RW_WIDTH, n), r, k, v, g, *prm[8:])
    return y, jnp.transpose(s_new, (3, 0, 1, 2))


RET_LOG_G = tuple(math.log(1.0 - 2.0 ** (-5.0 - h)) for h in range(RET_HEADS))


def _rope_tables(pos, half):
    j = lax.broadcasted_iota(jnp.int32, (1, half), 1).astype(F32)
    inv = jnp.exp(j * (-math.log(ROPE_BASE) / half))
    ang = pos * inv
    return jnp.cos(ang), jnp.sin(ang)


def _rope(x, cos, sin):
    half = RET_DK // 2
    outs = []
    for h in range(RET_HEADS):
        x1 = x[:, h * RET_DK:h * RET_DK + half]
        x2 = x[:, h * RET_DK + half:(h + 1) * RET_DK]
        outs += [x1 * cos - x2 * sin, x1 * sin + x2 * cos]
    return jnp.concatenate(outs, axis=-1)


def _ret_norm_gate(o, g):
    o = o * lax.rsqrt(jnp.mean(o * o, axis=-1, keepdims=True) + NORM_EPS)
    return jax.nn.silu(g) * o


def _ret_tables_kernel(cos_ref, sin_ref, dmask_ref, qdec_ref, kdec_ref, *, c_len):
    t_len = cos_ref.shape[0]
    pos = lax.broadcasted_iota(jnp.int32, (t_len, 1), 0).astype(F32)
    cos, sin = _rope_tables(pos, RET_DK // 2)
    cos_ref[...] = cos
    sin_ref[...] = sin
    ti = lax.broadcasted_iota(jnp.int32, (c_len, 1), 0).astype(F32)
    ii = lax.broadcasted_iota(jnp.int32, (c_len, c_len), 0)
    jj = lax.broadcasted_iota(jnp.int32, (c_len, c_len), 1)
    diff = (ii - jj).astype(F32)
    for h in range(RET_HEADS):
        lg = RET_LOG_G[h]
        dmask_ref[h] = jnp.where(diff >= 0, jnp.exp(lg * jnp.maximum(diff, 0.0)), 0.0)
        qdec_ref[h] = jnp.exp(lg * (ti + 1.0))
        kdec_ref[h] = jnp.exp(lg * (c_len - 1.0 - ti))


def _ret_layer_kernel(x_ref, gain_ref, win_ref, wout_ref, cos_ref, sin_ref, dmask_ref, qdec_ref, kdec_ref,
                      o_ref, sfin_ref, s_scr, y_scr, *, c_len):
    c = pl.program_id(1)

    @pl.when(c == 0)
    def _():
        s_scr[...] = jnp.zeros_like(s_scr)

    x = x_ref[...]
    hb = _rms(x, gain_ref[...]).astype(BF16)
    proj = lambda lo, width: jnp.dot(hb, win_ref[:, lo:lo + width], preferred_element_type=F32)
    cos, sin = cos_ref[...], sin_ref[...]
    q = _rope(proj(0, NQ), cos, sin)
    k = _rope(proj(NQ, NQ), cos, sin) * (RET_DK ** -0.5)
    for h in range(RET_HEADS):
        c_dec = math.exp(RET_LOG_G[h] * c_len)
        qh = q[:, h * RET_DK:(h + 1) * RET_DK]
        kh = k[:, h * RET_DK:(h + 1) * RET_DK]
        vh = proj(2 * NQ + h * RET_DV, RET_DV).astype(BF16)
        s_h = s_scr[h]
        sc = _dot_nt(qh, kh) * dmask_ref[h]
        o = _bdot(sc, vh) + _bdot(qh * qdec_ref[h], s_h)
        s_scr[h] = s_h * c_dec + _dot_tn(kh * kdec_ref[h], vh)
        gh = proj(2 * NQ + NV + h * RET_DV, RET_DV)
        y_scr[:, h * RET_DV:(h + 1) * RET_DV] = _ret_norm_gate(o, gh).astype(BF16)
    o_ref[...] = x + jnp.dot(y_scr[...], wout_ref[...], preferred_element_type=F32)

    @pl.when(c == pl.num_programs(1) - 1)
    def _():
        sfin_ref[0] = s_scr[...]


def retention_layer_prompt(x, gain, w_in, w_out, *, nb, c_len=RET_CHUNK):
    n = x.shape[0]
    t_len = n // nb
    nc = t_len // c_len
    half = RET_DK // 2
    tabs = pl.pallas_call(
        functools.partial(_ret_tables_kernel, c_len=c_len),
        out_shape=(jax.ShapeDtypeStruct((t_len, half), F32), jax.ShapeDtypeStruct((t_len, half), F32),
                   jax.ShapeDtypeStruct((RET_HEADS, c_len, c_len), F32),
                   jax.ShapeDtypeStruct((RET_HEADS, c_len, 1), F32),
                   jax.ShapeDtypeStruct((RET_HEADS, c_len, 1), F32)),
        name="retention_tables")()
    row = pl.BlockSpec((c_len, D_MODEL), lambda b, c: (b * nc + c, 0))
    pos_spec = pl.BlockSpec((c_len, half), lambda b, c: (c, 0))
    const = lambda a: pl.BlockSpec(a.shape, lambda b, c: (0,) * a.ndim)
    st_spec = pl.BlockSpec((1, RET_HEADS, RET_DK, RET_DV), lambda b, c: (b, 0, 0, 0))
    gain = gain.reshape(1, D_MODEL)
    return pl.pallas_call(
        functools.partial(_ret_layer_kernel, c_len=c_len), grid=(nb, nc),
        in_specs=[row, const(gain), const(w_in), const(w_out), pos_spec, pos_spec] + [const(a) for a in tabs[2:]],
        out_specs=(row, st_spec),
        out_shape=(jax.ShapeDtypeStruct((n, D_MODEL), F32),
                   jax.ShapeDtypeStruct((nb, RET_HEADS, RET_DK, RET_DV), F32)),
        scratch_shapes=[pltpu.VMEM((RET_HEADS, RET_DK, RET_DV), F32), pltpu.VMEM((c_len, NV), BF16)],
        compiler_params=_cparams("parallel", "arbitrary"), name="retention_layer")(
            x, gain, w_in, w_out, *tabs)


def _ret_step_rope_kernel(q_ref, k_ref, qo_ref, ko_ref, *, pos0):
    pos = jnp.full((q_ref.shape[0], 1), pos0, F32)
    cos, sin = _rope_tables(pos, RET_DK // 2)
    qo_ref[...] = _rope(q_ref[...].astype(F32), cos, sin).T
    ko_ref[...] = (_rope(k_ref[...].astype(F32), cos, sin) * (RET_DK ** -0.5)).T


def _ret_step_core_kernel(s_ref, qt_ref, kt_ref, v_ref, g_ref, s_out_ref, y_ref):
    tb = s_ref.shape[0]
    step = pl.program_id(0) * pl.num_programs(1) + pl.program_id(1)
    lane = lax.broadcasted_iota(jnp.int32, qt_ref.shape, 1)
    for i in range(tb):
        mine = lane == step * tb + i
        q_col = jnp.sum(jnp.where(mine, qt_ref[...], 0.0), axis=-1, keepdims=True)
        k_col = jnp.sum(jnp.where(mine, kt_ref[...], 0.0), axis=-1, keepdims=True)
        for h in range(RET_HEADS):
            gam = math.exp(RET_LOG_G[h])
            s_h = s_ref[i, h]
            qc = q_col[h * RET_DK:(h + 1) * RET_DK]
            kc = k_col[h * RET_DK:(h + 1) * RET_DK]
            vs = slice(h * RET_DV, (h + 1) * RET_DV)
            vr = v_ref[i, :, vs].astype(F32)
            qk = jnp.sum(qc * kc, axis=0, keepdims=True)
            o = qk * vr + jnp.sum((qc * gam) * s_h, axis=0, keepdims=True)
            s_out_ref[i, h] = s_h * gam + kc * vr
            y_ref[i, :, vs] = _ret_norm_gate(o, g_ref[i, :, vs].astype(F32)).astype(y_ref.dtype)


def retention_step_job(q, k, v, g, s0, *, pos0, tb):
    n = q.shape[0]
    vec_t = jax.ShapeDtypeStruct((NQ, n), F32)
    qt, kt = pl.pallas_call(functools.partial(_ret_step_rope_kernel, pos0=pos0), out_shape=(vec_t, vec_t),
                            name="retention_step_rope")(q, k)
    st = lambda inner: pl.BlockSpec((tb, RET_HEADS, RET_DK, RET_DV), lambda i, j: (i * inner + j, 0, 0, 0))
    rw = lambda inner: pl.BlockSpec((tb, 1, NV), lambda i, j: (i * inner + j, 0, 0))
    whole = lambda inner: pl.BlockSpec((NQ, n), lambda i, j: (0, 0))
    return SideJob(
        body=_ret_step_core_kernel,
        args=(s0, qt, kt, v.reshape(n, 1, NV), g.reshape(n, 1, NV)),
        in_specs=lambda inner: [st(inner), whole(inner), whole(inner), rw(inner), rw(inner)],
        out_shape=(jax.ShapeDtypeStruct(s0.shape, F32), jax.ShapeDtypeStruct((n, 1, NV), BF16)),
        out_specs=lambda inner: [st(inner), rw(inner)],
        steps=n // tb)


def _xattn_prompt_kernel(x_ref, g_ref, wq_ref, mk_ref, mv_ref, wo_ref, o_ref, att_scr):
    x = x_ref[...]
    q = jnp.dot(_rms(x, g_ref[...]).astype(BF16), wq_ref[...], preferred_element_type=F32)
    for h in range(MEM_HEADS):
        hs = slice(h * MEM_HD, (h + 1) * MEM_HD)
        s = _dot_nt(q[:, hs], mk_ref[0, :, hs]) * (MEM_HD ** -0.5)
        s = s - jnp.max(s, axis=-1, keepdims=True)
        e = jnp.exp(s)
        p = e / jnp.sum(e, axis=-1, keepdims=True)
        att_scr[:, hs] = _bdot(p, mv_ref[0, :, hs])
    o_ref[...] = x + jnp.dot(att_scr[...].astype(BF16), wo_ref[...], preferred_element_type=F32)


def xattn_prompt(x, gain, w_q, mem_k, mem_v, w_o, layer, *, nb, tm=512):
    n = x.shape[0]
    tiles_per_b = n // nb // tm
    mem_k = mem_k.reshape(-1, N_MEM, D_MODEL)
    mem_v = mem_v.reshape(-1, N_MEM, D_MODEL)
    row = pl.BlockSpec((tm, D_MODEL), lambda i: (i, 0))
    wspec = pl.BlockSpec((D_MODEL, D_MODEL), lambda i: (0, 0))
    mspec = pl.BlockSpec((1, N_MEM, D_MODEL), lambda i: (layer * nb + i // tiles_per_b, 0, 0))
    return pl.pallas_call(
        _xattn_prompt_kernel, grid=(n // tm,),
        in_specs=[row, pl.BlockSpec((1, D_MODEL), lambda i: (0, 0)), wspec, mspec, mspec, wspec],
        out_specs=row, out_shape=jax.ShapeDtypeStruct((n, D_MODEL), F32),
        scratch_shapes=[pltpu.VMEM((tm, D_MODEL), F32)],
        compiler_params=_cparams("parallel"), name="xattn_prompt")(
            x, gain.reshape(1, D_MODEL), w_q, mem_k, mem_v, w_o)


def _xattn_step_kernel(q_ref, mk_ref, mv_ref, o_ref, *, tb):
    half = N_MEM // 2
    both = lambda z: jnp.concatenate([z, z], axis=1)
    fold = lambda z, op: op(z[:, :MEM_HEADS], z[:, MEM_HEADS:])
    for i in range(tb):
        k8 = jnp.concatenate([mk_ref[0, i, :half], mk_ref[0, i, half:]], axis=1)
        v8 = jnp.concatenate([mv_ref[0, i, :half], mv_ref[0, i, half:]], axis=1)
        q8 = jnp.concatenate([q_ref[i], q_ref[i]], axis=0)
        s = jnp.sum(k8 * q8[None], axis=-1, keepdims=True) * (MEM_HD ** -0.5)
        smax = both(fold(jnp.max(s, axis=0, keepdims=True), jnp.maximum))
        e = jnp.exp(s - smax)
        den = both(fold(jnp.sum(e, axis=0, keepdims=True), jnp.add))
        o8 = jnp.sum((e / den) * v8, axis=0)
        o_ref[i] = o8[:MEM_HEADS] + o8[MEM_HEADS:]


def xattn_step_job(q, cache_k, cache_v, layer, *, tb):
    n = q.shape[0]

    def specs(inner):
        qspec = pl.BlockSpec((tb, MEM_HEADS, MEM_HD), lambda i, j: (i * inner + j, 0, 0))
        cspec = pl.BlockSpec((1, tb, N_MEM, MEM_HEADS, MEM_HD), lambda i, j: (layer, i * inner + j, 0, 0, 0))
        return qspec, cspec

    return SideJob(
        body=functools.partial(_xattn_step_kernel, tb=tb),
        args=(q.reshape(n, MEM_HEADS, MEM_HD), cache_k, cache_v),
        in_specs=lambda inner: [specs(inner)[0], specs(inner)[1], specs(inner)[1]],
        out_shape=(jax.ShapeDtypeStruct((n, MEM_HEADS, MEM_HD), F32),),
        out_specs=lambda inner: [specs(inner)[0]],
        steps=n // tb)


def run_job(job, name):
    return pl.pallas_call(
        job.body, grid=(job.steps, 1), in_specs=job.in_specs(1), out_specs=job.out_specs(1),
        out_shape=list(job.out_shape), compiler_params=_cparams("parallel", "arbitrary"), name=name)(*job.args)


ROUTER_LANES = 128
NEG_BIG = -1e30


def _moe_gates(logits):
    lane = lax.broadcasted_iota(jnp.int32, logits.shape, 1)
    first = lambda mask: jnp.min(jnp.where(mask, lane, ROUTER_LANES), axis=-1, keepdims=True)
    is_c = lane < MOE_GROUPS
    lc = jnp.where(is_c, logits, NEG_BIG)
    mc = jnp.max(lc, axis=-1, keepdims=True)
    g_idx = first(lc == mc)
    p_g = 1.0 / jnp.sum(jnp.where(is_c, jnp.exp(lc - mc), 0.0), axis=-1, keepdims=True)
    fl = lane - MOE_GROUPS
    in_g = (fl >= 0) & (fl < MOE_EXPERTS) & ((fl // MOE_PER_GROUP) == g_idx)
    lf = jnp.where(in_g, logits, NEG_BIG)
    m1 = jnp.max(lf, axis=-1, keepdims=True)
    i1 = first(lf == m1)
    lf2 = jnp.where(lane == i1, NEG_BIG, lf)
    m2 = jnp.max(lf2, axis=-1, keepdims=True)
    i2 = first(lf2 == m2)
    e2 = jnp.exp(m2 - m1)
    w_top = 1.0 / (1.0 + e2)
    gate = p_g * (jnp.where(lane == i1, w_top, 0.0) + jnp.where(lane == i2, e2 * w_top, 0.0))
    return gate, g_idx


MOE_CAP = 384
MOE_EPS = 2
MOE_STEPS = MOE_EXPERTS // MOE_EPS

SideJob = collections.namedtuple("SideJob", "body args in_specs out_shape out_specs steps")


def _with_side(main_kernel, n_in, n_out, side):
    ns_in, ns_out = len(side.args), len(side.out_shape)

    def kern(*refs):
        m_in = refs[:n_in]
        s_in = refs[n_in:n_in + ns_in]
        m_out = refs[n_in + ns_in:n_in + ns_in + n_out]
        s_out = refs[n_in + ns_in + n_out:n_in + ns_in + n_out + ns_out]
        scratch = refs[n_in + ns_in + n_out + ns_out:]
        main_kernel(*m_in, *m_out, *scratch, side=lambda: side.body(*s_in, *s_out))

    return kern


def _router_logits(h, wr_ref, br_ref):
    h_hi = h.astype(BF16)
    h_lo = (h - h_hi.astype(F32)).astype(BF16)
    acc = jnp.dot(h_hi, wr_ref[0], preferred_element_type=F32)
    acc = acc + jnp.dot(h_hi, wr_ref[1], preferred_element_type=F32)
    acc = acc + jnp.dot(h_lo, wr_ref[0], preferred_element_type=F32)
    return acc + br_ref[...]


def _experts_ffn(hb, gate, e0, w1_ref, w3_ref, w2_ref):
    lane = lax.broadcasted_iota(jnp.int32, gate.shape, 1)
    acc = None
    for e in range(MOE_EPS):
        a1 = jnp.dot(hb, w1_ref[e].astype(BF16), preferred_element_type=F32)
        a3 = jnp.dot(hb, w3_ref[e].astype(BF16), preferred_element_type=F32)
        ge = jnp.sum(jnp.where(lane == MOE_GROUPS + e0 + e, gate, 0.0), axis=-1, keepdims=True)
        hid = (jax.nn.silu(a1) * a3 * ge).astype(BF16)
        part = jnp.dot(hid, w2_ref[e].astype(BF16), preferred_element_type=F32)
        acc = part if acc is None else acc + part
    return acc


def _moe_kernel(x_ref, g_ref, wr_ref, br_ref, w1_ref, w3_ref, w2_ref, *rest, final_norm, cap, side=None):
    if final_norm:
        fin_ref, o_ref = rest[:2]
        rest = rest[2:]
    else:
        o_ref = rest[0]
        rest = rest[1:]
    h_scr, gate_scr, oh_scr, rk_scr, ohr_scr, rkr_scr, hg_scr, gg_scr, yg_scr, cnt_smem = rest
    tm = x_ref.shape[0]
    step = pl.program_id(1)
    grp = step // (MOE_PER_GROUP // MOE_EPS)
    first_half = step % (MOE_PER_GROUP // MOE_EPS) == 0
    last_half = step % (MOE_PER_GROUP // MOE_EPS) == MOE_PER_GROUP // MOE_EPS - 1

    @pl.when(step == 0)
    def _():
        x = x_ref[...]
        h = _rms(x, g_ref[...])
        h_scr[...] = h.astype(BF16)
        gate, g_idx = _moe_gates(_router_logits(h, wr_ref, br_ref))
        gate_scr[...] = gate
        o_ref[...] = x
        lane = lax.broadcasted_iota(jnp.int32, gate.shape, 1)
        onehot = jnp.where(lane == g_idx, 1.0, 0.0)
        ri = lax.broadcasted_iota(jnp.int32, (tm, tm), 0)
        ci = lax.broadcasted_iota(jnp.int32, (tm, tm), 1)
        earlier = jnp.where(ri > ci, 1.0, 0.0).astype(BF16)
        rank = jnp.dot(earlier, onehot.astype(BF16), preferred_element_type=F32)
        oh_scr[...] = onehot
        rk_scr[...] = rank
        ohr_scr[...] = onehot.T[:8]
        rkr_scr[...] = rank.T[:8]
        cnt = jnp.sum(onehot, axis=0, keepdims=True)
        for gi in range(MOE_GROUPS):
            cnt_smem[gi] = cnt[0, gi].astype(jnp.int32)

    lane = lax.broadcasted_iota(jnp.int32, (tm, ROUTER_LANES), 1)

    def gather_mat(base):
        slot = jnp.where(ohr_scr[pl.ds(grp, 1), :] > 0.5, rkr_scr[pl.ds(grp, 1), :] - base, -1.0)
        c = lax.broadcasted_iota(jnp.int32, (cap, tm), 0).astype(F32)
        return jnp.where(c == slot, 1.0, 0.0).astype(BF16)

    def scatter_mat(base):
        member = jnp.sum(jnp.where(lane == grp, oh_scr[...], 0.0), axis=-1, keepdims=True)
        rank = jnp.sum(jnp.where(lane == grp, rk_scr[...], 0.0), axis=-1, keepdims=True)
        slot = jnp.where(member > 0.5, rank - base, -1.0)
        c = lax.broadcasted_iota(jnp.int32, (tm, cap), 1).astype(F32)
        return jnp.where(c == slot, 1.0, 0.0).astype(BF16)

    def gather(base):
        pg = gather_mat(base)
        gate = gate_scr[...]
        g_hi = gate.astype(BF16)
        g_lo = (gate - g_hi.astype(F32)).astype(BF16)
        gg = jnp.dot(pg, g_hi, preferred_element_type=F32) + jnp.dot(pg, g_lo, preferred_element_type=F32)
        return jnp.dot(pg, h_scr[...], preferred_element_type=F32).astype(BF16), gg

    @pl.when(first_half)
    def _():
        hg, gg = gather(0.0)
        hg_scr[...] = hg
        gg_scr[...] = gg
        yg_scr[...] = jnp.zeros_like(yg_scr)

    yg_scr[...] += _experts_ffn(hg_scr[...], gg_scr[...], step * MOE_EPS, w1_ref, w3_ref, w2_ref)
    if side is not None:
        side()

    @pl.when(last_half)
    def _():
        o_ref[...] += jnp.dot(scatter_mat(0.0), yg_scr[...].astype(BF16), preferred_element_type=F32)

    def extra_round(r, carry):
        base = (r * cap).astype(F32)
        hg, gg = gather(base)
        y = _experts_ffn(hg, gg, step * MOE_EPS, w1_ref, w3_ref, w2_ref)
        o_ref[...] += jnp.dot(scatter_mat(base), y.astype(BF16), preferred_element_type=F32)
        return carry

    lax.fori_loop(1, (cnt_smem[grp] + cap - 1) // cap, extra_round, 0)

    if final_norm:
        @pl.when(step == MOE_STEPS - 1)
        def _():
            o_ref[...] = _rms(o_ref[...], fin_ref[...])


def moe_dense(x, gain, w_r, b_r, w1, w3, w2, layer, *, tm=512, cap=MOE_CAP, final_gain=None, side=None):
    n = x.shape[0]
    tm = min(tm, n)
    cap = min(cap, tm)
    gain = gain.reshape(1, D_MODEL)
    row = pl.BlockSpec((tm, D_MODEL), lambda i, s: (i, 0))
    const2 = lambda a: pl.BlockSpec(a.shape, lambda i, s: (0,) * a.ndim)
    soff = layer * MOE_STEPS
    wspec = pl.BlockSpec((MOE_EPS, D_MODEL, MOE_HIDDEN), lambda i, s: (soff + s, 0, 0))
    args = [x, gain, w_r, b_r, w1, w3, w2]
    in_specs = [row, const2(gain), const2(w_r), const2(b_r), wspec, wspec,
                pl.BlockSpec((MOE_EPS, MOE_HIDDEN, D_MODEL), lambda i, s: (soff + s, 0, 0))]
    if final_gain is not None:
        args.append(final_gain.reshape(1, D_MODEL))
        in_specs.append(const2(args[-1]))
    kern = functools.partial(_moe_kernel, final_norm=final_gain is not None, cap=cap)
    out_shape = [jax.ShapeDtypeStruct((n, D_MODEL), F32)]
    out_specs = [row]
    grid = (n // tm, MOE_STEPS)
    if side is not None:
        assert side.steps == grid[0] * grid[1]
        kern = _with_side(kern, len(args), 1, side)
        args += list(side.args)
        in_specs += side.in_specs(MOE_STEPS)
        out_shape += list(side.out_shape)
        out_specs += side.out_specs(MOE_STEPS)
    outs = pl.pallas_call(
        kern, grid=grid, in_specs=in_specs, out_specs=out_specs, out_shape=out_shape,
        scratch_shapes=[pltpu.VMEM((tm, D_MODEL), BF16),
                        pltpu.VMEM((tm, ROUTER_LANES), F32),
                        pltpu.VMEM((tm, ROUTER_LANES), F32),
                        pltpu.VMEM((tm, ROUTER_LANES), F32),
                        pltpu.VMEM((8, tm), F32),
                        pltpu.VMEM((8, tm), F32),
                        pltpu.VMEM((cap, D_MODEL), BF16),
                        pltpu.VMEM((cap, ROUTER_LANES), F32),
                        pltpu.VMEM((cap, D_MODEL), F32),
                        pltpu.SMEM((MOE_GROUPS,), jnp.int32)],
        compiler_params=_cparams("parallel", "arbitrary"), name="moe")(*args)
    return outs[0] if side is None else (outs[0], outs[1:])


def _group_weights(w1, w3, w2):
    ne = w1.shape[0] * MOE_EXPERTS
    return (w1.astype(BF16).reshape(ne, D_MODEL, MOE_HIDDEN), w3.astype(BF16).reshape(ne, D_MODEL, MOE_HIDDEN),
            w2.astype(BF16).reshape(ne, MOE_HIDDEN, D_MODEL))


def _router_params(w_rc, b_rc, w_rf, b_rf):
    pad = ROUTER_LANES - MOE_GROUPS - MOE_EXPERTS
    w_r = jnp.concatenate([w_rc, w_rf, jnp.zeros((D_MODEL, pad), F32)], axis=1).astype(F32)
    b_r = jnp.concatenate([b_rc, b_rf, jnp.zeros((pad,), F32)]).reshape(1, ROUTER_LANES).astype(F32)
    w_hi = w_r.astype(BF16)
    w_lo = (w_r - w_hi.astype(F32)).astype(BF16)
    return jnp.stack([w_hi, w_lo]), b_r


def _mem_kv_kernel(x_ref, g_ref, w_ref, kf_ref, vf_ref, kh_ref, vh_ref):
    h = _rms(x_ref[...], g_ref[0]).astype(BF16)
    for col, f_ref, h_ref in ((0, kf_ref, kh_ref), (D_MODEL, vf_ref, vh_ref)):
        acc = jnp.dot(h, w_ref[0, :, col:col + D_MODEL], preferred_element_type=F32)
        f_ref[0] = acc
        for hd in range(MEM_HEADS):
            h_ref[0, :, hd, :] = acc[:, hd * MEM_HD:(hd + 1) * MEM_HD]


def mem_kv(mem, gains, w_kv, *, tm=512):
    rows = mem.shape[0]
    nl = w_kv.shape[0]
    flat = jax.ShapeDtypeStruct((nl, rows, D_MODEL), F32)
    head = jax.ShapeDtypeStruct((nl, rows, MEM_HEADS, MEM_HD), F32)
    fspec = pl.BlockSpec((1, tm, D_MODEL), lambda l, i: (l, i, 0))
    hspec = pl.BlockSpec((1, tm, MEM_HEADS, MEM_HD), lambda l, i: (l, i, 0, 0))
    return pl.pallas_call(
        _mem_kv_kernel, grid=(nl, rows // tm),
        in_specs=[pl.BlockSpec((tm, D_MODEL), lambda l, i: (i, 0)),
                  pl.BlockSpec((1, 1, D_MODEL), lambda l, i: (l, 0, 0)),
                  pl.BlockSpec((1, D_MODEL, 2 * D_MODEL), lambda l, i: (l, 0, 0))],
        out_specs=(fspec, fspec, hspec, hspec), out_shape=(flat, flat, head, head),
        compiler_params=_cparams("parallel", "parallel"), name="mem_kv")(
            mem, gains.reshape(nl, 1, D_MODEL), w_kv)


def _forward(xp, xs, nbp, w, st, mem_k, mem_v, cache_k, cache_v):
    assert DEPTH == 2
    nbs = xs.shape[0]
    moe_tm = 1024
    moe_steps_p = (xp.shape[0] // moe_tm) * MOE_STEPS
    rwp = tuple(w[k][0] for k in ('rw_mu', 'rw_w0', 'rw_w2', 'rw_a0', 'rw_a2', 'rw_g2',
                                  'rw_k_k', 'rw_k_a', 'rw_r_k', 'rw_ln_w', 'rw_ln_b'))
    w_in0, w_out0 = w['w_in0_bf'][0], w['w_out0_bf'][0]
    moe = lambda x, layer, **kw: moe_dense(x, w['norm_ffn'][layer], *w['router'][layer], *w['moe_g'], layer, **kw)

    u, p_s = linear(xs, w_in0, gain=w['norm_mix'][0], splits=(S5_WIDTH, RW_PROJ))
    y_s5, s5r_s, s5i_s = s5_mixer(u.reshape(1, nbs, S5_WIDTH), st['s5_re'], st['s5_im'], w['s5p'][0],
                                  w['s5_d'][0], w['s5_w_glu'][0], tc=1)
    y_rw, rw_s = rwkv_step(p_s, st['shift'], st['rwkv'], rwp)
    xs = linear(y_s5.reshape(nbs, S5_WIDTH), w_out0[:S5_WIDTH], x2=y_rw, w2=w_out0[S5_WIDTH:], residual=xs)
    q_s = linear(xs, w['w_mq_bf'][0], gain=w['norm_mem'][0])

    zeros = lambda *shape: jnp.zeros(shape, F32)
    u, p_p = linear(xp, w_in0, gain=w['norm_mix'][0], splits=(S5_WIDTH, RW_PROJ), out_tmajor=True, batch=nbp)
    y_s5, s5r_p, s5i_p = s5_mixer(u, zeros(nbp, S5_STATE), zeros(nbp, S5_STATE), w['s5p'][0],
                                  w['s5_d'][0], w['s5_w_glu'][0], tc=128)
    rw_bs = 8
    rw_steps = (nbp // rw_bs) * (p_p.shape[0] // RW_HD)
    job = xattn_step_job(q_s, cache_k, cache_v, 0, tb=nbs // rw_steps)
    (y_rw, rw_p, sh_p), (att_s,) = rwkv_prompt(p_p, zeros(nbp, RW_PROJ), zeros(nbp, RW_HEADS, RW_HD, RW_HD),
                                               rwp, bs=rw_bs, side=job)
    xp = linear(y_s5, w_out0[:S5_WIDTH], x2=y_rw, w2=w_out0[S5_WIDTH:], residual=xp, x_tmajor=True)
    xp = xattn_prompt(xp, w['norm_mem'][0], w['w_mq_bf'][0], mem_k, mem_v, w['w_mo_bf'][0], 0, nb=nbp)

    xs = linear(att_s.reshape(nbs, D_MODEL), w['w_mo_bf'][0], residual=xs)
    xs = moe(xs, 0)
    q, k, v, g = linear(xs, w['w_in1_bf'][0], gain=w['norm_mix'][1], out_dtype=BF16, splits=(NQ, NQ, NV, NV))
    job = retention_step_job(q, k, v, g, st['ret'], pos0=float(PAST_LEN), tb=nbs // moe_steps_p)
    xp, (ret_s, y_ret) = moe(xp, 0, tm=moe_tm, side=job)
    xs = linear(y_ret.reshape(nbs, NV), w['w_out1_bf'][0], residual=xs)
    q_s = linear(xs, w['w_mq_bf'][1], gain=w['norm_mem'][1])

    xp, ret_p = retention_layer_prompt(xp, w['norm_mix'][1], w['w_in1_bf'][0], w['w_out1_bf'][0], nb=nbp)
    xp = xattn_prompt(xp, w['norm_mem'][1], w['w_mq_bf'][1], mem_k, mem_v, w['w_mo_bf'][1], 1, nb=nbp)
    job = xattn_step_job(q_s, cache_k, cache_v, 1, tb=nbs // moe_steps_p)
    y_p, (att_s,) = moe(xp, 1, tm=moe_tm, final_gain=w['norm_final'], side=job)
    xs = linear(att_s.reshape(nbs, D_MODEL), w['w_mo_bf'][1], residual=xs)
    y_s = moe(xs, 1, final_gain=w['norm_final'])

    grp = lambda z, nb: z.reshape(1, nb, S5_GROUPS, S5_N)
    prompt_out = (y_p, grp(s5r_p, nbp), grp(s5i_p, nbp), rw_p[None], sh_p[None], ret_p[None])
    sample_out = (y_s, grp(s5r_s, nbs), grp(s5i_s, nbs), rw_s[None], p_s[None], ret_s[None])
    return prompt_out, sample_out


def kernel(x_prompt, x_sample, mem_prompt, state_s5_re, state_s5_im, state_rwkv, state_shift, state_ret, cache_mem_k, cache_mem_v, norm_mix, norm_mem, norm_ffn, norm_final, w_in0, w_out0, s5_a_re, s5_a_im, s5_b_re, s5_b_im, s5_c_re, s5_c_im, s5_d, s5_log_dt, s5_w_glu, rw_mu, rw_w0, rw_w2, rw_a0, rw_a2, rw_g2, rw_k_k, rw_k_a, rw_r_k, rw_ln_w, rw_ln_b, w_in1, w_out1, mem_norm, w_mq, w_mk, w_mv, w_mo, moe_w_rc, moe_b_rc, moe_w_rf, moe_b_rf, moe_w1, moe_w3, moe_w2):
    w = dict(norm_mix=norm_mix, norm_mem=norm_mem, norm_ffn=norm_ffn, norm_final=norm_final,
             w_in0=w_in0, w_out0=w_out0, s5_a_re=s5_a_re, s5_a_im=s5_a_im, s5_b_re=s5_b_re, s5_b_im=s5_b_im,
             s5_c_re=s5_c_re, s5_c_im=s5_c_im, s5_d=s5_d, s5_log_dt=s5_log_dt, s5_w_glu=s5_w_glu,
             rw_mu=rw_mu, rw_w0=rw_w0, rw_w2=rw_w2, rw_a0=rw_a0, rw_a2=rw_a2, rw_g2=rw_g2,
             rw_k_k=rw_k_k, rw_k_a=rw_k_a, rw_r_k=rw_r_k, rw_ln_w=rw_ln_w, rw_ln_b=rw_ln_b,
             w_in1=w_in1, w_out1=w_out1, w_mq=w_mq, w_mo=w_mo,
             moe_w_rc=moe_w_rc, moe_b_rc=moe_b_rc, moe_w_rf=moe_w_rf, moe_b_rf=moe_b_rf,
             moe_w1=moe_w1, moe_w3=moe_w3, moe_w2=moe_w2)
    nbp, t_len, _ = x_prompt.shape
    nbs = x_sample.shape[0]
    n_even, n_odd = state_s5_re.shape[0], state_ret.shape[0]
    for name in ('w_in0', 'w_out0', 'w_in1', 'w_out1', 'w_mq', 'w_mo'):
        w[name + '_bf'] = w[name].astype(BF16)
    w['s5p'] = [_s5_params(s5_a_re[i], s5_a_im[i], s5_b_re[i], s5_b_im[i], s5_c_re[i], s5_c_im[i], s5_log_dt[i])
                for i in range(n_even)]
    w['router'] = [_router_params(moe_w_rc[l], moe_b_rc[l], moe_w_rf[l], moe_b_rf[l]) for l in range(DEPTH)]
    w['moe_g'] = _group_weights(moe_w1, moe_w3, moe_w2)

    mem = mem_prompt.reshape(nbp * N_MEM, D_MODEL)
    w_kv = jnp.concatenate([w_mk, w_mv], axis=2).astype(BF16)
    mk, mv, mk_h, mv_h = mem_kv(mem, mem_norm, w_kv)
    mem_k_l = mk.reshape(DEPTH, nbp, N_MEM, D_MODEL)
    mem_v_l = mv.reshape(DEPTH, nbp, N_MEM, D_MODEL)
    mem_k_p = mk_h.reshape(DEPTH, nbp, N_MEM, MEM_HEADS, MEM_HD)
    mem_v_p = mv_h.reshape(DEPTH, nbp, N_MEM, MEM_HEADS, MEM_HD)

    assert n_even == 1 and n_odd == 1
    st = dict(s5_re=state_s5_re.reshape(nbs, S5_STATE), s5_im=state_s5_im.reshape(nbs, S5_STATE),
              rwkv=state_rwkv[0], shift=state_shift[0], ret=state_ret[0])
    (y_p, s5r_p, s5i_p, rw_p, sh_p, ret_p), (y_s, s5r_s, s5i_s, rw_s, sh_s, ret_s) = _forward(
        x_prompt.reshape(nbp * t_len, D_MODEL), x_sample.reshape(nbs, D_MODEL), nbp, w, st,
        mem_k_l, mem_v_l, cache_mem_k, cache_mem_v)
    return (y_p.reshape(nbp, t_len, D_MODEL), y_s.reshape(nbs, 1, D_MODEL),
            s5r_p, s5i_p, rw_p, sh_p, ret_p, mem_k_p, mem_v_p, s5r_s, s5i_s, rw_s, sh_s, ret_s)
```

```python
import collections
import functools
import math

import jax
import jax.numpy as jnp
from jax import lax
from jax.experimental import pallas as pl
from jax.experimental.pallas import tpu as pltpu

F32 = jnp.float32
BF16 = jnp.bfloat16

D_MODEL = 1024
DEPTH = 2
PAST_LEN = 16384
S5_WIDTH = 512
S5_GROUP = 16
S5_GROUPS = 32
S5_N = 64
S5_STATE = S5_GROUPS * S5_N
S5_GBLK = 8
RW_WIDTH = 512
RW_HD = 64
RW_HEADS = 8
RW_LORA = 256
RW_PROJ = 3 * RW_WIDTH + RW_LORA
IN0 = S5_WIDTH + RW_PROJ
RET_DK = 256
RET_HEADS = 4
RET_DV = 512
RET_CHUNK = 256
NQ = RET_HEADS * RET_DK
NV = RET_HEADS * RET_DV
IN1 = 2 * NQ + 2 * NV
N_MEM = 256
MEM_HEADS = 4
MEM_HD = 256
MOE_GROUPS = 4
MOE_PER_GROUP = 4
MOE_EXPERTS = 16
MOE_HIDDEN = 256
NORM_EPS = 1e-6
RW_GN_EPS = 64e-5
ROPE_BASE = 10000.0

VMEM_LIMIT = 56 * 1024 * 1024


def _cparams(*sem):
    return pltpu.CompilerParams(dimension_semantics=sem, vmem_limit_bytes=VMEM_LIMIT)


def _bdot(a, b):
    return jnp.dot(a.astype(BF16), b.astype(BF16), preferred_element_type=F32)


def _dot_nt(a, b):
    return lax.dot_general(a.astype(BF16), b.astype(BF16), (((1,), (1,)), ((), ())),
                           preferred_element_type=F32)


def _dot_tn(a, b):
    return lax.dot_general(a.astype(BF16), b.astype(BF16), (((0,), (0,)), ((), ())),
                           preferred_element_type=F32)


def _split3(x):
    hi = x.astype(BF16)
    r1 = x - hi.astype(F32)
    mid = r1.astype(BF16)
    lo = (r1 - mid.astype(F32)).astype(BF16)
    return hi, mid, lo


def _dot_exact_rhs(x, m_bf16, passes=3):
    hi, mid, lo = _split3(x)
    acc = jnp.dot(hi, m_bf16, preferred_element_type=F32)
    if passes > 1:
        acc = acc + jnp.dot(mid, m_bf16, preferred_element_type=F32)
    if passes > 2:
        acc = acc + jnp.dot(lo, m_bf16, preferred_element_type=F32)
    return acc


def _rms(x, g):
    ms = jnp.mean(x * x, axis=-1, keepdims=True)
    return x * lax.rsqrt(ms + NORM_EPS) * g


def _linear_kernel(*refs, norm, two, res):
    it = iter(refs)
    x_ref = next(it)
    g_ref = next(it) if norm else None
    w_ref = next(it)
    x2_ref = next(it) if two else None
    w2_ref = next(it) if two else None
    r_ref = next(it) if res else None
    o_refs = list(it)
    x = x_ref[...].astype(F32)
    if norm:
        x = _rms(x, g_ref[...])
    xb = x.astype(BF16)
    x2b = x2_ref[...].astype(BF16) if two else None
    col = 0
    for o_ref in o_refs:
        m = o_ref.shape[-1]
        step = next((s for s in (512, 256) if m % s == 0), m)
        for j in range(m // step):
            sl = slice(col + j * step, col + (j + 1) * step)
            acc = jnp.dot(xb, w_ref[:, sl], preferred_element_type=F32)
            if two:
                acc = acc + jnp.dot(x2b, w2_ref[:, sl], preferred_element_type=F32)
            if res:
                acc = acc + r_ref[:, sl]
            o_ref[:, j * step:(j + 1) * step] = acc.astype(o_ref.dtype)
        col += m


def _row_spec(tm, width, tmajor_b):
    if tmajor_b is None:
        return pl.BlockSpec((tm, width), lambda i: (i, 0))
    nb, tiles_per_b = tmajor_b
    return pl.BlockSpec((tm, width), lambda i: (i % tiles_per_b, i // tiles_per_b))


def linear(x, w, *, gain=None, x2=None, w2=None, residual=None, out_dtype=F32, tm=512,
           x_tmajor=False, out_tmajor=False, batch=None, splits=None, name="linear"):
    if x_tmajor:
        t_len, nb, k = x.shape
        n = t_len * nb
    else:
        n, k = x.shape
        nb = batch
        t_len = n // nb if nb else None
    m = w.shape[1]
    tm = min(tm, n if not (x_tmajor or out_tmajor) else t_len)
    assert n % tm == 0
    tiles_per_b = (t_len // tm) if (x_tmajor or out_tmajor) else None
    args, specs = [], []

    def add_rows(a, tmajor):
        width = a.shape[-1]
        args.append(a.reshape(t_len, nb * width) if tmajor else a)
        specs.append(_row_spec(tm, width, (nb, tiles_per_b) if tmajor else None))

    add_rows(x, x_tmajor)
    if gain is not None:
        args.append(gain.reshape(1, k).astype(F32))
        specs.append(pl.BlockSpec((1, k), lambda i: (0, 0)))
    args.append(w)
    specs.append(pl.BlockSpec(w.shape, lambda i: (0, 0)))
    if x2 is not None:
        add_rows(x2, x_tmajor)
        args.append(w2)
        specs.append(pl.BlockSpec(w2.shape, lambda i: (0, 0)))
    if residual is not None:
        add_rows(residual, False)
    widths = tuple(splits) if splits else (m,)
    assert sum(widths) == m
    if out_tmajor:
        out_shape = [jax.ShapeDtypeStruct((t_len, nb * mw), out_dtype) for mw in widths]
    else:
        out_shape = [jax.ShapeDtypeStruct((n, mw), out_dtype) for mw in widths]
    out_specs = [_row_spec(tm, mw, (nb, tiles_per_b) if out_tmajor else None) for mw in widths]
    kern = functools.partial(_linear_kernel, norm=gain is not None, two=x2 is not None,
                             res=residual is not None)
    outs = pl.pallas_call(
        kern, grid=(n // tm,), in_specs=specs, out_specs=out_specs, out_shape=out_shape,
        compiler_params=_cparams("parallel"), name=name)(*args)
    if out_tmajor:
        outs = [o.reshape(t_len, nb, mw) for o, mw in zip(outs, widths)]
    return outs if splits else outs[0]


def _s5_kernel(u_ref, h_re_ref, h_im_ref, abar_re_ref, abar_im_ref, bb_re_ref, bb_im_ref,
               cc_re_ref, cc_im_ref, d_ref, wglu_ref, y_ref, s_re_ref, s_im_ref,
               x_re, x_im, st_re, st_im, il_scr, *, tc, nb, flat):
    c = pl.program_id(0)
    nlb = S5_WIDTH // 128
    rows = tc * nb
    nblk = S5_GROUPS // S5_GBLK
    bw_in = S5_GBLK * S5_GROUP
    bw_st = S5_GBLK * S5_N

    @pl.when(c == 0)
    def _():
        st_re[...] = h_re_ref[...]
        st_im[...] = h_im_ref[...]

    if flat:
        for b in range(nb):
            for j in range(nlb):
                il_scr[j, pl.ds(b, tc, stride=nb), :] = u_ref[:, b * S5_WIDTH + j * 128:b * S5_WIDTH + (j + 1) * 128]
        u = jnp.concatenate([il_scr[j] for j in range(nlb)], axis=-1)
    else:
        u = u_ref[...].reshape(rows, S5_WIDTH)
    ub = u.astype(BF16)
    for gb in range(nblk):
        ui = ub[:, gb * bw_in:(gb + 1) * bw_in]
        x_re[:, gb * bw_st:(gb + 1) * bw_st] = jnp.dot(ui, bb_re_ref[gb], preferred_element_type=F32)
        x_im[:, gb * bw_st:(gb + 1) * bw_st] = jnp.dot(ui, bb_im_ref[gb], preferred_element_type=F32)

    lane_blk = 1024
    for lb in range(S5_STATE // lane_blk):
        sl = slice(lb * lane_blk, (lb + 1) * lane_blk)
        ar = jnp.broadcast_to(abar_re_ref[:, sl], (nb, lane_blk))
        ai = jnp.broadcast_to(abar_im_ref[:, sl], (nb, lane_blk))

        def body(t, carry, sl=sl, ar=ar, ai=ai):
            xr, xi = carry
            r0 = pl.multiple_of(t * nb, nb)
            br = x_re[pl.ds(r0, nb), sl]
            bi = x_im[pl.ds(r0, nb), sl]
            nr = ar * xr - ai * xi + br
            ni = ar * xi + ai * xr + bi
            x_re[pl.ds(r0, nb), sl] = nr
            x_im[pl.ds(r0, nb), sl] = ni
            return nr, ni

        fr, fi = lax.fori_loop(0, tc, body, (st_re[:, sl], st_im[:, sl]), unroll=min(tc, 4))
        st_re[:, sl] = fr
        st_im[:, sl] = fi

    for gb in range(nblk):
        xr = x_re[:, gb * bw_st:(gb + 1) * bw_st].astype(BF16)
        xi = x_im[:, gb * bw_st:(gb + 1) * bw_st].astype(BF16)
        yb = (jnp.dot(xr, cc_re_ref[gb], preferred_element_type=F32)
              - jnp.dot(xi, cc_im_ref[gb], preferred_element_type=F32))
        cs = slice(gb * bw_in, (gb + 1) * bw_in)
        yb = yb + d_ref[:, cs] * u[:, cs]
        x_re[:, cs] = jax.nn.gelu(yb)
    y = x_re[:, :S5_WIDTH]
    y = y * jax.nn.sigmoid(jnp.dot(y.astype(BF16), wglu_ref[...], preferred_element_type=F32))
    if flat:
        for j in range(nlb):
            il_scr[j] = y[:, j * 128:(j + 1) * 128]
        for b in range(nb):
            for j in range(nlb):
                y_ref[:, b * S5_WIDTH + j * 128:b * S5_WIDTH + (j + 1) * 128] = (
                    il_scr[j, pl.ds(b, tc, stride=nb), :].astype(y_ref.dtype))
    else:
        y_ref[...] = y.reshape(y_ref.shape).astype(y_ref.dtype)

    @pl.when(c == pl.num_programs(0) - 1)
    def _():
        s_re_ref[...] = st_re[...]
        s_im_ref[...] = st_im[...]


def _s5_params(a_re, a_im, b_re, b_im, c_re, c_im, log_dt):
    dt = jnp.exp(log_dt.astype(F32))[:, None]
    ar, ai = a_re.astype(F32), a_im.astype(F32)
    mag = jnp.exp(dt * ar)
    abar_re, abar_im = mag * jnp.cos(dt * ai), mag * jnp.sin(dt * ai)
    den = ar * ar + ai * ai
    nr = abar_re - 1.0
    coef_re = (nr * ar + abar_im * ai) / den
    coef_im = (abar_im * ar - nr * ai) / den
    cr, ci = coef_re[..., None], coef_im[..., None]
    brf, bif = b_re.astype(F32), b_im.astype(F32)
    bb_re = cr * brf - ci * bif
    bb_im = cr * bif + ci * brf
    nblk = S5_GROUPS // S5_GBLK
    eye = jnp.eye(S5_GBLK, dtype=F32)

    def blockdiag_in(bb):
        t = jnp.transpose(bb, (0, 2, 1)).reshape(nblk, S5_GBLK, S5_GROUP, S5_N)
        m = jnp.einsum('kgcn,gh->kgchn', t, eye)
        return m.reshape(nblk, S5_GBLK * S5_GROUP, S5_GBLK * S5_N).astype(BF16)

    def blockdiag_out(cc):
        t = jnp.transpose(cc.astype(F32), (0, 2, 1)).reshape(nblk, S5_GBLK, S5_N, S5_GROUP)
        m = jnp.einsum('khnc,hg->khngc', t, eye)
        return m.reshape(nblk, S5_GBLK * S5_N, S5_GBLK * S5_GROUP).astype(BF16)

    return (abar_re.reshape(1, S5_STATE), abar_im.reshape(1, S5_STATE),
            blockdiag_in(bb_re), blockdiag_in(bb_im), blockdiag_out(c_re), blockdiag_out(c_im))


def s5_mixer(u_tm, h_re, h_im, params, d_skip, w_glu, *, tc):
    t_len, nb, _ = u_tm.shape
    abar_re, abar_im, bb_re, bb_im, cc_re, cc_im = params
    tc = min(tc, t_len)
    assert t_len % tc == 0 and nb % 8 == 0
    rows = tc * nb
    flat = t_len > 1
    full = lambda a: pl.BlockSpec(a.shape, lambda c: (0,) * a.ndim)
    if flat:
        u_arg = u_tm.reshape(t_len, nb * S5_WIDTH)
        io_spec = pl.BlockSpec((tc, nb * S5_WIDTH), lambda c: (c, 0))
        y_shape = jax.ShapeDtypeStruct((t_len, nb * S5_WIDTH), BF16)
    else:
        u_arg = u_tm
        io_spec = pl.BlockSpec((tc, nb, S5_WIDTH), lambda c: (c, 0, 0))
        y_shape = jax.ShapeDtypeStruct((t_len, nb, S5_WIDTH), BF16)
    args = (u_arg, h_re, h_im, abar_re, abar_im, bb_re, bb_im, cc_re, cc_im,
            d_skip.reshape(1, S5_WIDTH).astype(F32), w_glu.astype(BF16))
    in_specs = [io_spec] + [full(a) for a in args[1:]]
    st_shape = jax.ShapeDtypeStruct((nb, S5_STATE), F32)
    st_spec = pl.BlockSpec((nb, S5_STATE), lambda c: (0, 0))
    scratch = [pltpu.VMEM((rows, S5_STATE), F32), pltpu.VMEM((rows, S5_STATE), F32),
               pltpu.VMEM((nb, S5_STATE), F32), pltpu.VMEM((nb, S5_STATE), F32),
               pltpu.VMEM((S5_WIDTH // 128, rows if flat else 8, 128), F32)]
    y, s_re, s_im = pl.pallas_call(
        functools.partial(_s5_kernel, tc=tc, nb=nb, flat=flat), grid=(t_len // tc,), in_specs=in_specs,
        out_specs=(io_spec, st_spec, st_spec), out_shape=(y_shape, st_shape, st_shape),
        scratch_shapes=scratch, compiler_params=_cparams("arbitrary"), name="s5_mixer")(*args)
    return y.reshape(t_len, nb, S5_WIDTH), s_re, s_im


def _head_ones():
    i = lax.broadcasted_iota(jnp.int32, (RW_WIDTH, RW_WIDTH), 0) // RW_HD
    j = lax.broadcasted_iota(jnp.int32, (RW_WIDTH, RW_WIDTH), 1) // RW_HD
    return jnp.where(i == j, 1.0, 0.0).astype(BF16)


def _softplus(z):
    return jnp.maximum(z, 0.0) + jnp.log1p(jnp.exp(-jnp.abs(z)))


def _rw_prep(p, p_prev, prm, ones_bd):
    mu, w0, w2, a0, a2, g2, k_k, k_a = prm
    xm = p + (p_prev - p) * mu
    o1, o2, o3 = RW_WIDTH, 2 * RW_WIDTH, 3 * RW_WIDTH
    r, k, v = xm[:, :o1], xm[:, o1:o2], xm[:, o2:o3]
    wd, ad, gd = xm[:, o3:o3 + 64], xm[:, o3 + 64:o3 + 128], xm[:, o3 + 128:]
    w = -_softplus(-(w0 + _bdot(jnp.tanh(wd), w2))) - 0.5
    lw = -jnp.exp(w)
    a = jax.nn.sigmoid(a0 + _bdot(ad, a2))
    g = _bdot(jax.nn.sigmoid(gd), g2)
    kk = k * k_k
    ss = _dot_exact_rhs(kk * kk, ones_bd, passes=1)
    kk = kk / jnp.maximum(jnp.sqrt(ss), 1e-12)
    k = k * (1.0 + (a - 1.0) * k_a)
    return r, lw, k, v, -kk, kk * a, g


def _rw_post(o, r, k, v, g, r_k, ln_w, ln_b, ones_bd):
    inv = 1.0 / RW_HD
    mean = _dot_exact_rhs(o, ones_bd, passes=2) * inv
    d = o - mean
    var = _dot_exact_rhs(d * d, ones_bd, passes=1) * inv
    on = d * lax.rsqrt(var + RW_GN_EPS) * ln_w + ln_b
    bonus = _dot_exact_rhs(r * k * r_k, ones_bd, passes=1) * v
    return (on + bonus) * g


def _rw_chunk_kernel(p_ref, shift_ref, h0_ref, mu_ref, w0_ref, w2_ref, a0_ref, a2_ref, g2_ref,
                     kk_ref, ka_ref, rk_ref, lnw_ref, lnb_ref,
                     y_ref, hfin_ref, shout_ref, prev_scr, h_scr, o_scr, *, c_len, bs, side=None):
    c = pl.program_id(1)
    nc = pl.num_programs(1)
    cl = c_len

    @pl.when(c == 0)
    def _():
        prev_scr[...] = shift_ref[:, 0, :]
        h_scr[...] = h0_ref[...]

    ones_bd = _head_ones()
    row = lax.broadcasted_iota(jnp.int32, (cl, RW_PROJ), 0)
    ps, pprevs = [], []
    for bi in range(bs):
        p = p_ref[:, bi * RW_PROJ:(bi + 1) * RW_PROJ]
        pprevs.append(jnp.where(row == 0, prev_scr[bi:bi + 1, :], pltpu.roll(p, 1, 0)))
        ps.append(p)
    p_all = jnp.concatenate(ps, axis=0) if bs > 1 else ps[0]
    pprev_all = jnp.concatenate(pprevs, axis=0) if bs > 1 else pprevs[0]
    prm = (mu_ref[...], w0_ref[...], w2_ref[...], a0_ref[...], a2_ref[...], g2_ref[...],
           kk_ref[...], ka_ref[...])
    r, lw, k, v, a, b, g = _rw_prep(p_all, pprev_all, prm, ones_bd)

    ti = lax.broadcasted_iota(jnp.int32, (cl, cl), 0)
    si = lax.broadcasted_iota(jnp.int32, (cl, cl), 1)
    lmat = jnp.where(ti >= si, 1.0, 0.0).astype(BF16)
    eye = jnp.where(ti == si, 1.0, 0.0)
    mi = lax.broadcasted_iota(jnp.int32, (2 * cl, 3 * cl), 0)
    mj = lax.broadcasted_iota(jnp.int32, (2 * cl, 3 * cl), 1)
    t_row = jnp.where(mi >= cl, mi - cl, mi)
    s_col = jnp.where(mj < cl, mj, jnp.where(mj >= 2 * cl, mj - 2 * cl, -4 * cl))
    keep = (t_row - s_col) >= jnp.where(mi >= cl, 0, 1)
    eye_bf = eye.astype(BF16)

    lhs_l, rhs_l, vh_l, hcat_l, kb_l, etot_l = [], [], [], [], [], []
    for bi in range(bs):
        rs = slice(bi * cl, (bi + 1) * cl)
        lw_b = lw[rs]
        l_hi, l_mid, l_lo = _split3(lw_b)
        cum = (jnp.dot(lmat, l_hi, preferred_element_type=F32)
               + jnp.dot(lmat, l_mid, preferred_element_type=F32)
               + jnp.dot(lmat, l_lo, preferred_element_type=F32))
        tot = cum[cl - 1:cl, :]
        e_neg = jnp.exp(-cum)
        e_rem = jnp.exp(tot - cum)
        at = (a[rs] * jnp.exp(cum - lw_b)).astype(BF16)
        rt = (r[rs] * jnp.exp(cum)).astype(BF16)
        bt = (b[rs] * e_neg).astype(BF16)
        kt = (k[rs] * e_neg).astype(BF16)
        bh = (b[rs] * e_rem).astype(BF16)
        kh = (k[rs] * e_rem).astype(BF16)
        e_tot = jnp.exp(tot)
        vb = v[rs].astype(BF16)
        for h in range(RW_HEADS):
            hs = slice(h * RW_HD, (h + 1) * RW_HD)
            lhs_l.append(jnp.concatenate([at[:, hs], rt[:, hs]], axis=0))
            rhs_l.append(jnp.concatenate([kt[:, hs], eye_bf, bt[:, hs]], axis=0))
            vh_l.append(vb[:, hs])
            kb_l.append(jnp.concatenate([kh[:, hs], bh[:, hs]], axis=0))
            etot_l.append(jnp.sum(eye * e_tot[:, hs], axis=-1, keepdims=True))
            hcat_l.append(h_scr[bi, h])

    nitem = bs * RW_HEADS
    items = range(nitem)
    aa_l = [jnp.where(keep, _dot_nt(lhs_l[i], rhs_l[i]), 0.0).astype(BF16) for i in items]
    pw_l = [aa_l[i][:cl, 2 * cl:] for i in items]
    tinv_l = [eye_bf + pw_l[i] for i in items]
    for _ in range(int(math.log2(cl)) - 1):
        pw_l = [jnp.dot(pw_l[i], pw_l[i], preferred_element_type=F32).astype(BF16) for i in items]
        tinv_l = [jnp.dot(tinv_l[i], eye_bf + pw_l[i], preferred_element_type=F32).astype(BF16) for i in items]
    vh_cat = [jnp.concatenate([vh_l[i], hcat_l[i].astype(BF16)], axis=0) for i in items]
    x1_l = [jnp.dot(aa_l[i][:cl, :2 * cl], vh_cat[i], preferred_element_type=F32).astype(BF16) for i in items]
    u_l = [jnp.dot(tinv_l[i], x1_l[i], preferred_element_type=F32).astype(BF16) for i in items]
    o_l = [jnp.dot(aa_l[i][cl:, :], jnp.concatenate([vh_cat[i], u_l[i]], axis=0),
                   preferred_element_type=F32) for i in items]
    hn_l = [hcat_l[i] * etot_l[i]
            + lax.dot_general(kb_l[i], jnp.concatenate([vh_l[i], u_l[i]], axis=0), (((0,), (0,)), ((), ())),
                              preferred_element_type=F32) for i in items]

    for bi in range(bs):
        for h in range(RW_HEADS):
            i = bi * RW_HEADS + h
            o_scr[bi * cl:(bi + 1) * cl, h * RW_HD:(h + 1) * RW_HD] = o_l[i]
            h_scr[bi, h] = hn_l[i]
        prev_scr[bi:bi + 1, :] = ps[bi][cl - 1:cl, :]

    y = _rw_post(o_scr[...], r, k, v, g, rk_ref[...], lnw_ref[...], lnb_ref[...], ones_bd)
    for bi in range(bs):
        y_ref[:, bi * RW_WIDTH:(bi + 1) * RW_WIDTH] = y[bi * cl:(bi + 1) * cl].astype(y_ref.dtype)
    if side is not None:
        side()

    @pl.when(c == nc - 1)
    def _():
        hfin_ref[...] = h_scr[...]
        for bi in range(bs):
            shout_ref[bi] = ps[bi][cl - 1:cl, :]


def _rw_param_args(mu, w0, w2, a0, a2, g2, k_k, k_a, r_k, ln_w, ln_b):
    row = lambda z: z.reshape(1, -1).astype(F32)
    return (row(mu), row(w0), w2.astype(BF16), row(a0), a2.astype(BF16), g2.astype(BF16),
            row(k_k), row(k_a), row(r_k), row(ln_w), row(ln_b))


def rwkv_prompt(p_tm, shift, s0, params, *, bs=4, side=None):
    c_len = RW_HD
    t_len, nb, _ = p_tm.shape
    assert t_len % c_len == 0 and nb % bs == 0
    prm = _rw_param_args(*params)
    const = lambda a: pl.BlockSpec(a.shape, lambda b, c: (0,) * a.ndim)
    st_spec = pl.BlockSpec((bs, RW_HEADS, RW_HD, RW_HD), lambda b, c: (b, 0, 0, 0))
    sh_spec = pl.BlockSpec((bs, 1, RW_PROJ), lambda b, c: (b, 0, 0))
    in_specs = [pl.BlockSpec((c_len, bs * RW_PROJ), lambda b, c: (c, b)), sh_spec, st_spec] + [const(a) for a in prm]
    out_shape = (jax.ShapeDtypeStruct((t_len, nb * RW_WIDTH), BF16),
                 jax.ShapeDtypeStruct((nb, RW_HEADS, RW_HD, RW_HD), F32),
                 jax.ShapeDtypeStruct((nb, 1, RW_PROJ), F32))
    out_specs = (pl.BlockSpec((c_len, bs * RW_WIDTH), lambda b, c: (c, b)), st_spec, sh_spec)
    scratch = [pltpu.VMEM((bs, RW_PROJ), F32), pltpu.VMEM((bs, RW_HEADS, RW_HD, RW_HD), F32),
               pltpu.VMEM((bs * c_len, RW_WIDTH), F32)]
    h0 = jnp.swapaxes(s0, -1, -2)
    args = [p_tm.reshape(t_len, nb * RW_PROJ), shift.reshape(nb, 1, RW_PROJ), h0, *prm]
    kern = functools.partial(_rw_chunk_kernel, c_len=c_len, bs=bs)
    grid = (nb // bs, t_len // c_len)
    out_shape, out_specs = list(out_shape), list(out_specs)
    if side is not None:
        assert side.steps == grid[0] * grid[1]
        kern = _with_side(kern, len(args), 3, side)
        args += list(side.args)
        in_specs += side.in_specs(grid[1])
        out_shape += list(side.out_shape)
        out_specs += side.out_specs(grid[1])
    outs = pl.pallas_call(
        kern, grid=grid, in_specs=in_specs, out_specs=out_specs, out_shape=out_shape,
        scratch_shapes=scratch, compiler_params=_cparams("parallel", "arbitrary"), name="rwkv_prompt")(*args)
    y, h_fin, sh = outs[:3]
    res = (y.reshape(t_len, nb, RW_WIDTH), jnp.swapaxes(h_fin, -1, -2), sh.reshape(nb, RW_PROJ))
    return res if side is None else (res, outs[3:])


def _rw_step_prep_kernel(p_ref, shift_ref, mu_ref, w0_ref, w2_ref, a0_ref, a2_ref, g2_ref, kk_ref, ka_ref,
                         r_ref, k_ref, v_ref, g_ref, rt_ref, wt_ref, kt_ref, at_ref, bt_ref, vt_ref):
    prm = (mu_ref[...], w0_ref[...], w2_ref[...], a0_ref[...], a2_ref[...], g2_ref[...],
           kk_ref[...], ka_ref[...])
    r, lw, k, v, a, b, g = _rw_prep(p_ref[...], shift_ref[...], prm, _head_ones())
    r_ref[...] = r
    k_ref[...] = k
    v_ref[...] = v
    g_ref[...] = g
    rt_ref[...] = r.T
    wt_ref[...] = jnp.exp(lw).T
    kt_ref[...] = k.T
    at_ref[...] = a.T
    bt_ref[...] = b.T
    vt_ref[...] = v.T


def _rw_step_core_kernel(s_ref, r_ref, w_ref, k_ref, a_ref, b_ref, v_ref, s_out_ref, o_ref):
    r, w, k, a, b = r_ref[0], w_ref[0], k_ref[0], a_ref[0], b_ref[0]
    for j in range(s_ref.shape[1]):
        s = s_ref[0, j]
        sa = jnp.sum(s * a, axis=0, keepdims=True)
        s_new = s * w + sa * b + v_ref[0, j:j + 1, :] * k
        s_out_ref[0, j] = s_new
        o_ref[0, j:j + 1, :] = jnp.sum(s_new * r, axis=0, keepdims=True)


def _rw_step_post_kernel(ot_ref, r_ref, k_ref, v_ref, g_ref, rk_ref, lnw_ref, lnb_ref, y_ref):
    y_ref[...] = _rw_post(ot_ref[...].T, r_ref[...], k_ref[...], v_ref[...], g_ref[...],
                          rk_ref[...], lnw_ref[...], lnb_ref[...], _head_ones()).astype(y_ref.dtype)


def rwkv_step(p, shift, s0, params, *, vb=32):
    n = p.shape[0]
    prm = _rw_param_args(*params)
    vec = jax.ShapeDtypeStruct((n, RW_WIDTH), F32)
    vec_t = jax.ShapeDtypeStruct((RW_WIDTH, n), F32)
    r, k, v, g, rt, wt, kt, at, bt, vt = pl.pallas_call(
        _rw_step_prep_kernel, out_shape=(vec,) * 4 + (vec_t,) * 6, name="rwkv_step_prep")(p, shift, *prm[:8])
    heads = lambda z: z.reshape(RW_HEADS, RW_HD, n)
    k_spec = pl.BlockSpec((1, RW_HD, n), lambda h, j: (h, 0, 0))
    v_spec = pl.BlockSpec((1, vb, n), lambda h, j: (h, j, 0))
    st_spec = pl.BlockSpec((1, vb, RW_HD, n), lambda h, j: (h, j, 0, 0))
    st = jnp.transpose(s0, (1, 2, 3, 0))
    s_new, ot = pl.pallas_call(
        _rw_step_core_kernel, grid=(RW_HEADS, RW_HD // vb),
        in_specs=[st_spec] + [k_spec] * 5 + [v_spec], out_specs=(st_spec, v_spec),
        out_shape=(jax.ShapeDtypeStruct(st.shape, F32), jax.ShapeDtypeStruct((RW_HEADS, RW_HD, n), F32)),
        compiler_params=_cparams("parallel", "parallel"), name="rwkv_step_core")(
            st, heads(rt), heads(wt), heads(kt), heads(at), heads(bt), heads(vt))
    y = pl.pallas_call(
        _rw_step_post_kernel, out_shape=jax.ShapeDtypeStruct((n, RW_WIDTH), BF16), name="rwkv_step_post")(
            ot.reshape(RW_WIDTH, n), r, k, v, g, *prm[8:])
    return y, jnp.transpose(s_new, (3, 0, 1, 2))


RET_LOG_G = tuple(math.log(1.0 - 2.0 ** (-5.0 - h)) for h in range(RET_HEADS))


def _rope_tables(pos, half):
    j = lax.broadcasted_iota(jnp.int32, (1, half), 1).astype(F32)
    inv = jnp.exp(j * (-math.log(ROPE_BASE) / half))
    ang = pos * inv
    return jnp.cos(ang), jnp.sin(ang)


def _rope(x, cos, sin):
    half = RET_DK // 2
    outs = []
    for h in range(RET_HEADS):
        x1 = x[:, h * RET_DK:h * RET_DK + half]
        x2 = x[:, h * RET_DK + half:(h + 1) * RET_DK]
        outs += [x1 * cos - x2 * sin, x1 * sin + x2 * cos]
    return jnp.concatenate(outs, axis=-1)


def _ret_norm_gate(o, g):
    o = o * lax.rsqrt(jnp.mean(o * o, axis=-1, keepdims=True) + NORM_EPS)
    return jax.nn.silu(g) * o


def _ret_tables_kernel(cos_ref, sin_ref, dmask_ref, qdec_ref, kdec_ref, *, c_len):
    t_len = cos_ref.shape[0]
    pos = lax.broadcasted_iota(jnp.int32, (t_len, 1), 0).astype(F32)
    cos, sin = _rope_tables(pos, RET_DK // 2)
    cos_ref[...] = cos
    sin_ref[...] = sin
    ti = lax.broadcasted_iota(jnp.int32, (c_len, 1), 0).astype(F32)
    ii = lax.broadcasted_iota(jnp.int32, (c_len, c_len), 0)
    jj = lax.broadcasted_iota(jnp.int32, (c_len, c_len), 1)
    diff = (ii - jj).astype(F32)
    for h in range(RET_HEADS):
        lg = RET_LOG_G[h]
        dmask_ref[h] = jnp.where(diff >= 0, jnp.exp(lg * jnp.maximum(diff, 0.0)), 0.0)
        qdec_ref[h] = jnp.exp(lg * (ti + 1.0))
        kdec_ref[h] = jnp.exp(lg * (c_len - 1.0 - ti))


def _ret_layer_kernel(x_ref, gain_ref, win_ref, wout_ref, cos_ref, sin_ref, dmask_ref, qdec_ref, kdec_ref,
                      o_ref, sfin_ref, s_scr, y_scr, *, c_len):
    c = pl.program_id(1)

    @pl.when(c == 0)
    def _():
        s_scr[...] = jnp.zeros_like(s_scr)

    x = x_ref[...]
    hb = _rms(x, gain_ref[...]).astype(BF16)
    proj = lambda lo, width: jnp.dot(hb, win_ref[:, lo:lo + width], preferred_element_type=F32)
    cos, sin = cos_ref[...], sin_ref[...]
    q = _rope(proj(0, NQ), cos, sin)
    k = _rope(proj(NQ, NQ), cos, sin) * (RET_DK ** -0.5)
    for h in range(RET_HEADS):
        c_dec = math.exp(RET_LOG_G[h] * c_len)
        qh = q[:, h * RET_DK:(h + 1) * RET_DK]
        kh = k[:, h * RET_DK:(h + 1) * RET_DK]
        vh = proj(2 * NQ + h * RET_DV, RET_DV).astype(BF16)
        s_h = s_scr[h]
        sc = _dot_nt(qh, kh) * dmask_ref[h]
        o = _bdot(sc, vh) + _bdot(qh * qdec_ref[h], s_h)
        s_scr[h] = s_h * c_dec + _dot_tn(kh * kdec_ref[h], vh)
        gh = proj(2 * NQ + NV + h * RET_DV, RET_DV)
        y_scr[:, h * RET_DV:(h + 1) * RET_DV] = _ret_norm_gate(o, gh).astype(BF16)
    o_ref[...] = x + jnp.dot(y_scr[...], wout_ref[...], preferred_element_type=F32)

    @pl.when(c == pl.num_programs(1) - 1)
    def _():
        sfin_ref[0] = s_scr[...]


def retention_layer_prompt(x, gain, w_in, w_out, *, nb, c_len=RET_CHUNK):
    n = x.shape[0]
    t_len = n // nb
    nc = t_len // c_len
    half = RET_DK // 2
    tabs = pl.pallas_call(
        functools.partial(_ret_tables_kernel, c_len=c_len),
        out_shape=(jax.ShapeDtypeStruct((t_len, half), F32), jax.ShapeDtypeStruct((t_len, half), F32),
                   jax.ShapeDtypeStruct((RET_HEADS, c_len, c_len), F32),
                   jax.ShapeDtypeStruct((RET_HEADS, c_len, 1), F32),
                   jax.ShapeDtypeStruct((RET_HEADS, c_len, 1), F32)),
        name="retention_tables")()
    row = pl.BlockSpec((c_len, D_MODEL), lambda b, c: (b * nc + c, 0))
    pos_spec = pl.BlockSpec((c_len, half), lambda b, c: (c, 0))
    const = lambda a: pl.BlockSpec(a.shape, lambda b, c: (0,) * a.ndim)
    st_spec = pl.BlockSpec((1, RET_HEADS, RET_DK, RET_DV), lambda b, c: (b, 0, 0, 0))
    gain = gain.reshape(1, D_MODEL)
    return pl.pallas_call(
        functools.partial(_ret_layer_kernel, c_len=c_len), grid=(nb, nc),
        in_specs=[row, const(gain), const(w_in), const(w_out), pos_spec, pos_spec] + [const(a) for a in tabs[2:]],
        out_specs=(row, st_spec),
        out_shape=(jax.ShapeDtypeStruct((n, D_MODEL), F32),
                   jax.ShapeDtypeStruct((nb, RET_HEADS, RET_DK, RET_DV), F32)),
        scratch_shapes=[pltpu.VMEM((RET_HEADS, RET_DK, RET_DV), F32), pltpu.VMEM((c_len, NV), BF16)],
        compiler_params=_cparams("parallel", "arbitrary"), name="retention_layer")(
            x, gain, w_in, w_out, *tabs)


def _ret_step_rope_kernel(q_ref, k_ref, qo_ref, ko_ref, *, pos0):
    pos = jnp.full((q_ref.shape[0], 1), pos0, F32)
    cos, sin = _rope_tables(pos, RET_DK // 2)
    qo_ref[...] = _rope(q_ref[...].astype(F32), cos, sin).T
    ko_ref[...] = (_rope(k_ref[...].astype(F32), cos, sin) * (RET_DK ** -0.5)).T


def _ret_step_core_kernel(s_ref, qt_ref, kt_ref, v_ref, g_ref, s_out_ref, y_ref):
    tb = s_ref.shape[0]
    step = pl.program_id(0) * pl.num_programs(1) + pl.program_id(1)
    lane = lax.broadcasted_iota(jnp.int32, qt_ref.shape, 1)
    for i in range(tb):
        mine = lane == step * tb + i
        q_col = jnp.sum(jnp.where(mine, qt_ref[...], 0.0), axis=-1, keepdims=True)
        k_col = jnp.sum(jnp.where(mine, kt_ref[...], 0.0), axis=-1, keepdims=True)
        for h in range(RET_HEADS):
            gam = math.exp(RET_LOG_G[h])
            s_h = s_ref[i, h]
            qc = q_col[h * RET_DK:(h + 1) * RET_DK]
            kc = k_col[h * RET_DK:(h + 1) * RET_DK]
            vs = slice(h * RET_DV, (h + 1) * RET_DV)
            vr = v_ref[i, :, vs].astype(F32)
            qk = jnp.sum(qc * kc, axis=0, keepdims=True)
            o = qk * vr + jnp.sum((qc * gam) * s_h, axis=0, keepdims=True)
            s_out_ref[i, h] = s_h * gam + kc * vr
            y_ref[i, :, vs] = _ret_norm_gate(o, g_ref[i, :, vs].astype(F32)).astype(y_ref.dtype)


def retention_step_job(q, k, v, g, s0, *, pos0, tb):
    n = q.shape[0]
    vec_t = jax.ShapeDtypeStruct((NQ, n), F32)
    qt, kt = pl.pallas_call(functools.partial(_ret_step_rope_kernel, pos0=pos0), out_shape=(vec_t, vec_t),
                            name="retention_step_rope")(q, k)
    st = lambda inner: pl.BlockSpec((tb, RET_HEADS, RET_DK, RET_DV), lambda i, j: (i * inner + j, 0, 0, 0))
    rw = lambda inner: pl.BlockSpec((tb, 1, NV), lambda i, j: (i * inner + j, 0, 0))
    whole = lambda inner: pl.BlockSpec((NQ, n), lambda i, j: (0, 0))
    return SideJob(
        body=_ret_step_core_kernel,
        args=(s0, qt, kt, v.reshape(n, 1, NV), g.reshape(n, 1, NV)),
        in_specs=lambda inner: [st(inner), whole(inner), whole(inner), rw(inner), rw(inner)],
        out_shape=(jax.ShapeDtypeStruct(s0.shape, F32), jax.ShapeDtypeStruct((n, 1, NV), BF16)),
        out_specs=lambda inner: [st(inner), rw(inner)],
        steps=n // tb)


def _xattn_prompt_kernel(x_ref, g_ref, wq_ref, mk_ref, mv_ref, wo_ref, *rest, pre):
    if pre:
        ya_ref, wa_ref, yb_ref, wb_ref, o_ref, att_scr = rest
        x = (x_ref[...] + jnp.dot(ya_ref[...], wa_ref[...], preferred_element_type=F32)
             + jnp.dot(yb_ref[...], wb_ref[...], preferred_element_type=F32))
    else:
        o_ref, att_scr = rest
        x = x_ref[...]
    q = jnp.dot(_rms(x, g_ref[...]).astype(BF16), wq_ref[...], preferred_element_type=F32)
    for h in range(MEM_HEADS):
        hs = slice(h * MEM_HD, (h + 1) * MEM_HD)
        s = _dot_nt(q[:, hs], mk_ref[0, :, hs]) * (MEM_HD ** -0.5)
        s = s - jnp.max(s, axis=-1, keepdims=True)
        e = jnp.exp(s)
        p = e / jnp.sum(e, axis=-1, keepdims=True)
        att_scr[:, hs] = _bdot(p, mv_ref[0, :, hs])
    o_ref[...] = x + jnp.dot(att_scr[...].astype(BF16), wo_ref[...], preferred_element_type=F32)


def xattn_prompt(x, gain, w_q, mem_k, mem_v, w_o, layer, *, nb, tm=1024, pre=None):
    n = x.shape[0]
    tiles_per_b = n // nb // tm
    mem_k = mem_k.reshape(-1, N_MEM, D_MODEL)
    mem_v = mem_v.reshape(-1, N_MEM, D_MODEL)
    row = pl.BlockSpec((tm, D_MODEL), lambda i: (i, 0))
    wspec = pl.BlockSpec((D_MODEL, D_MODEL), lambda i: (0, 0))
    mspec = pl.BlockSpec((1, N_MEM, D_MODEL), lambda i: (layer * nb + i // tiles_per_b, 0, 0))
    args = [x, gain.reshape(1, D_MODEL), w_q, mem_k, mem_v, w_o]
    in_specs = [row, pl.BlockSpec((1, D_MODEL), lambda i: (0, 0)), wspec, mspec, mspec, wspec]
    if pre is not None:
        for y, wy in (pre[:2], pre[2:]):
            t_len, _, kw = y.shape
            args += [y.reshape(t_len, nb * kw), wy]
            in_specs += [_row_spec(tm, kw, (nb, tiles_per_b)), pl.BlockSpec(wy.shape, lambda i: (0, 0))]
    return pl.pallas_call(
        functools.partial(_xattn_prompt_kernel, pre=pre is not None), grid=(n // tm,),
        in_specs=in_specs, out_specs=row, out_shape=jax.ShapeDtypeStruct((n, D_MODEL), F32),
        scratch_shapes=[pltpu.VMEM((tm, D_MODEL), F32)],
        compiler_params=_cparams("parallel"), name="xattn_prompt")(*args)


def _xattn_step_kernel(q_ref, mk_ref, mv_ref, o_ref, *, tb):
    half = N_MEM // 2
    both = lambda z: jnp.concatenate([z, z], axis=1)
    fold = lambda z, op: op(z[:, :MEM_HEADS], z[:, MEM_HEADS:])
    for i in range(tb):
        k8 = jnp.concatenate([mk_ref[0, i, :half], mk_ref[0, i, half:]], axis=1)
        v8 = jnp.concatenate([mv_ref[0, i, :half], mv_ref[0, i, half:]], axis=1)
        q8 = jnp.concatenate([q_ref[i], q_ref[i]], axis=0)
        s = jnp.sum(k8 * q8[None], axis=-1, keepdims=True) * (MEM_HD ** -0.5)
        smax = both(fold(jnp.max(s, axis=0, keepdims=True), jnp.maximum))
        e = jnp.exp(s - smax)
        den = both(fold(jnp.sum(e, axis=0, keepdims=True), jnp.add))
        o8 = jnp.sum((e / den) * v8, axis=0)
        o_ref[i] = o8[:MEM_HEADS] + o8[MEM_HEADS:]


def xattn_step_job(q, cache_k, cache_v, layer, *, tb):
    n = q.shape[0]

    def specs(inner):
        qspec = pl.BlockSpec((tb, MEM_HEADS, MEM_HD), lambda i, j: (i * inner + j, 0, 0))
        cspec = pl.BlockSpec((1, tb, N_MEM, MEM_HEADS, MEM_HD), lambda i, j: (layer, i * inner + j, 0, 0, 0))
        return qspec, cspec

    return SideJob(
        body=functools.partial(_xattn_step_kernel, tb=tb),
        args=(q.reshape(n, MEM_HEADS, MEM_HD), cache_k, cache_v),
        in_specs=lambda inner: [specs(inner)[0], specs(inner)[1], specs(inner)[1]],
        out_shape=(jax.ShapeDtypeStruct((n, MEM_HEADS, MEM_HD), F32),),
        out_specs=lambda inner: [specs(inner)[0]],
        steps=n // tb)


def run_job(job, name):
    return pl.pallas_call(
        job.body, grid=(job.steps, 1), in_specs=job.in_specs(1), out_specs=job.out_specs(1),
        out_shape=list(job.out_shape), compiler_params=_cparams("parallel", "arbitrary"), name=name)(*job.args)


ROUTER_LANES = 128
NEG_BIG = -1e30


def _moe_gates(logits):
    lane = lax.broadcasted_iota(jnp.int32, logits.shape, 1)
    first = lambda mask: jnp.min(jnp.where(mask, lane, ROUTER_LANES), axis=-1, keepdims=True)
    is_c = lane < MOE_GROUPS
    lc = jnp.where(is_c, logits, NEG_BIG)
    mc = jnp.max(lc, axis=-1, keepdims=True)
    g_idx = first(lc == mc)
    p_g = 1.0 / jnp.sum(jnp.where(is_c, jnp.exp(lc - mc), 0.0), axis=-1, keepdims=True)
    fl = lane - MOE_GROUPS
    in_g = (fl >= 0) & (fl < MOE_EXPERTS) & ((fl // MOE_PER_GROUP) == g_idx)
    lf = jnp.where(in_g, logits, NEG_BIG)
    m1 = jnp.max(lf, axis=-1, keepdims=True)
    i1 = first(lf == m1)
    lf2 = jnp.where(lane == i1, NEG_BIG, lf)
    m2 = jnp.max(lf2, axis=-1, keepdims=True)
    i2 = first(lf2 == m2)
    e2 = jnp.exp(m2 - m1)
    w_top = 1.0 / (1.0 + e2)
    gate = p_g * (jnp.where(lane == i1, w_top, 0.0) + jnp.where(lane == i2, e2 * w_top, 0.0))
    return gate, g_idx


MOE_CAP = 320
MOE_EPS = 2
MOE_STEPS = MOE_EXPERTS // MOE_EPS

SideJob = collections.namedtuple("SideJob", "body args in_specs out_shape out_specs steps")


def _with_side(main_kernel, n_in, n_out, side):
    ns_in, ns_out = len(side.args), len(side.out_shape)

    def kern(*refs):
        m_in = refs[:n_in]
        s_in = refs[n_in:n_in + ns_in]
        m_out = refs[n_in + ns_in:n_in + ns_in + n_out]
        s_out = refs[n_in + ns_in + n_out:n_in + ns_in + n_out + ns_out]
        scratch = refs[n_in + ns_in + n_out + ns_out:]
        main_kernel(*m_in, *m_out, *scratch, side=lambda: side.body(*s_in, *s_out))

    return kern


def _router_logits(h, wr_ref, br_ref):
    h_hi = h.astype(BF16)
    h_lo = (h - h_hi.astype(F32)).astype(BF16)
    rows = h.shape[0]
    res = jnp.dot(jnp.concatenate([h_hi, h_lo], axis=0), wr_ref[...], preferred_element_type=F32)
    acc = (res[:rows, :ROUTER_LANES] + res[:rows, ROUTER_LANES:]) + (res[rows:, :ROUTER_LANES] + res[rows:, ROUTER_LANES:])
    return acc + br_ref[...]


def _experts_ffn(hb, gate, e0, w1_ref, w3_ref, w2_ref):
    lane = lax.broadcasted_iota(jnp.int32, gate.shape, 1)
    acc = None
    for e in range(MOE_EPS):
        a1 = jnp.dot(hb, w1_ref[e].astype(BF16), preferred_element_type=F32)
        a3 = jnp.dot(hb, w3_ref[e].astype(BF16), preferred_element_type=F32)
        ge = jnp.sum(jnp.where(lane == MOE_GROUPS + e0 + e, gate, 0.0), axis=-1, keepdims=True)
        hid = (jax.nn.silu(a1) * a3 * ge).astype(BF16)
        part = jnp.dot(hid, w2_ref[e].astype(BF16), preferred_element_type=F32)
        acc = part if acc is None else acc + part
    return acc


def _moe_kernel(x_ref, g_ref, wr_ref, br_ref, w1_ref, w3_ref, w2_ref, *rest, final_norm, cap, side=None):
    if final_norm:
        fin_ref, o_ref = rest[:2]
        rest = rest[2:]
    else:
        o_ref = rest[0]
        rest = rest[1:]
    h_scr, oh_scr, rk_scr, ohr_scr, rkr_scr, hg_scr, gg_scr, yg_scr, cnt_smem = rest
    tm = x_ref.shape[0]
    step = pl.program_id(1)
    grp = step // (MOE_PER_GROUP // MOE_EPS)
    first_half = step % (MOE_PER_GROUP // MOE_EPS) == 0
    last_half = step % (MOE_PER_GROUP // MOE_EPS) == MOE_PER_GROUP // MOE_EPS - 1

    @pl.when(step == 0)
    def _():
        x = x_ref[...]
        h = _rms(x, g_ref[...])
        gate, g_idx = _moe_gates(_router_logits(h, wr_ref, br_ref))
        g_hi = gate.astype(BF16)
        h_scr[:, :D_MODEL] = h.astype(BF16)
        h_scr[:, D_MODEL:D_MODEL + ROUTER_LANES] = g_hi
        h_scr[:, D_MODEL + ROUTER_LANES:] = (gate - g_hi.astype(F32)).astype(BF16)
        o_ref[...] = x
        lane = lax.broadcasted_iota(jnp.int32, gate.shape, 1)
        onehot = jnp.where(lane == g_idx, 1.0, 0.0)
        ri = lax.broadcasted_iota(jnp.int32, (tm, tm), 0)
        ci = lax.broadcasted_iota(jnp.int32, (tm, tm), 1)
        earlier = jnp.where(ri > ci, 1.0, 0.0).astype(BF16)
        rank = jnp.dot(earlier, onehot.astype(BF16), preferred_element_type=F32)
        oh_scr[...] = onehot
        rk_scr[...] = rank
        ohr_scr[...] = onehot.T[:8]
        rkr_scr[...] = rank.T[:8]
        cnt = jnp.sum(onehot, axis=0, keepdims=True)
        for gi in range(MOE_GROUPS):
            cnt_smem[gi] = cnt[0, gi].astype(jnp.int32)

    lane = lax.broadcasted_iota(jnp.int32, (tm, ROUTER_LANES), 1)

    def gather_mat(base):
        slot = jnp.where(ohr_scr[pl.ds(grp, 1), :] > 0.5, rkr_scr[pl.ds(grp, 1), :] - base, -1.0)
        c = lax.broadcasted_iota(jnp.int32, (cap, tm), 0).astype(F32)
        return jnp.where(c == slot, 1.0, 0.0).astype(BF16)

    def scatter_mat(base):
        member = jnp.sum(jnp.where(lane == grp, oh_scr[...], 0.0), axis=-1, keepdims=True)
        rank = jnp.sum(jnp.where(lane == grp, rk_scr[...], 0.0), axis=-1, keepdims=True)
        slot = jnp.where(member > 0.5, rank - base, -1.0)
        c = lax.broadcasted_iota(jnp.int32, (tm, cap), 1).astype(F32)
        return jnp.where(c == slot, 1.0, 0.0).astype(BF16)

    def gather(base):
        got = jnp.dot(gather_mat(base), h_scr[...], preferred_element_type=F32)
        gg = got[:, D_MODEL:D_MODEL + ROUTER_LANES] + got[:, D_MODEL + ROUTER_LANES:]
        return got[:, :D_MODEL].astype(BF16), gg

    @pl.when(first_half)
    def _():
        hg, gg = gather(0.0)
        hg_scr[...] = hg
        gg_scr[...] = gg
        yg_scr[...] = jnp.zeros_like(yg_scr)

    yg_scr[...] += _experts_ffn(hg_scr[...], gg_scr[...], step * MOE_EPS, w1_ref, w3_ref, w2_ref)
    if side is not None:
        side()

    @pl.when(last_half)
    def _():
        o_ref[...] += jnp.dot(scatter_mat(0.0), yg_scr[...].astype(BF16), preferred_element_type=F32)

    def extra_round(r, carry):
        base = (r * cap).astype(F32)
        hg, gg = gather(base)
        y = _experts_ffn(hg, gg, step * MOE_EPS, w1_ref, w3_ref, w2_ref)
        o_ref[...] += jnp.dot(scatter_mat(base), y.astype(BF16), preferred_element_type=F32)
        return carry

    lax.fori_loop(1, (cnt_smem[grp] + cap - 1) // cap, extra_round, 0)

    if final_norm:
        @pl.when(step == MOE_STEPS - 1)
        def _():
            o_ref[...] = _rms(o_ref[...], fin_ref[...])


def moe_dense(x, gain, w_r, b_r, w1, w3, w2, layer, *, tm=512, cap=MOE_CAP, final_gain=None, side=None):
    n = x.shape[0]
    tm = min(tm, n)
    cap = min(cap, tm)
    gain = gain.reshape(1, D_MODEL)
    row = pl.BlockSpec((tm, D_MODEL), lambda i, s: (i, 0))
    const2 = lambda a: pl.BlockSpec(a.shape, lambda i, s: (0,) * a.ndim)
    soff = layer * MOE_STEPS
    wspec = pl.BlockSpec((MOE_EPS, D_MODEL, MOE_HIDDEN), lambda i, s: (soff + s, 0, 0))
    args = [x, gain, w_r, b_r, w1, w3, w2]
    in_specs = [row, const2(gain), const2(w_r), const2(b_r), wspec, wspec,
                pl.BlockSpec((MOE_EPS, MOE_HIDDEN, D_MODEL), lambda i, s: (soff + s, 0, 0))]
    if final_gain is not None:
        args.append(final_gain.reshape(1, D_MODEL))
        in_specs.append(const2(args[-1]))
    kern = functools.partial(_moe_kernel, final_norm=final_gain is not None, cap=cap)
    out_shape = [jax.ShapeDtypeStruct((n, D_MODEL), F32)]
    out_specs = [row]
    grid = (n // tm, MOE_STEPS)
    if side is not None:
        assert side.steps == grid[0] * grid[1]
        kern = _with_side(kern, len(args), 1, side)
        args += list(side.args)
        in_specs += side.in_specs(MOE_STEPS)
        out_shape += list(side.out_shape)
        out_specs += side.out_specs(MOE_STEPS)
    outs = pl.pallas_call(
        kern, grid=grid, in_specs=in_specs, out_specs=out_specs, out_shape=out_shape,
        scratch_shapes=[pltpu.VMEM((tm, D_MODEL + 2 * ROUTER_LANES), BF16),
                        pltpu.VMEM((tm, ROUTER_LANES), F32),
                        pltpu.VMEM((tm, ROUTER_LANES), F32),
                        pltpu.VMEM((8, tm), F32),
                        pltpu.VMEM((8, tm), F32),
                        pltpu.VMEM((cap, D_MODEL), BF16),
                        pltpu.VMEM((cap, ROUTER_LANES), F32),
                        pltpu.VMEM((cap, D_MODEL), F32),
                        pltpu.SMEM((MOE_GROUPS,), jnp.int32)],
        compiler_params=_cparams("parallel", "arbitrary"), name="moe")(*args)
    return outs[0] if side is None else (outs[0], outs[1:])


def _group_weights(w1, w3, w2):
    ne = w1.shape[0] * MOE_EXPERTS
    return (w1.astype(BF16).reshape(ne, D_MODEL, MOE_HIDDEN), w3.astype(BF16).reshape(ne, D_MODEL, MOE_HIDDEN),
            w2.astype(BF16).reshape(ne, MOE_HIDDEN, D_MODEL))


def _router_params(w_rc, b_rc, w_rf, b_rf):
    pad = ROUTER_LANES - MOE_GROUPS - MOE_EXPERTS
    w_r = jnp.concatenate([w_rc, w_rf, jnp.zeros((D_MODEL, pad), F32)], axis=1).astype(F32)
    b_r = jnp.concatenate([b_rc, b_rf, jnp.zeros((pad,), F32)]).reshape(1, ROUTER_LANES).astype(F32)
    w_hi = w_r.astype(BF16)
    w_lo = (w_r - w_hi.astype(F32)).astype(BF16)
    return jnp.concatenate([w_hi, w_lo], axis=1), b_r


def _mem_kv_kernel(x_ref, g_ref, w_ref, kf_ref, vf_ref, kh_ref, vh_ref):
    h = _rms(x_ref[...], g_ref[0]).astype(BF16)
    for col, f_ref, h_ref in ((0, kf_ref, kh_ref), (D_MODEL, vf_ref, vh_ref)):
        acc = jnp.dot(h, w_ref[0, :, col:col + D_MODEL], preferred_element_type=F32)
        f_ref[0] = acc
        for hd in range(MEM_HEADS):
            h_ref[0, :, hd, :] = acc[:, hd * MEM_HD:(hd + 1) * MEM_HD]


def mem_kv(mem, gains, w_kv, *, tm=512):
    rows = mem.shape[0]
    nl = w_kv.shape[0]
    flat = jax.ShapeDtypeStruct((nl, rows, D_MODEL), F32)
    head = jax.ShapeDtypeStruct((nl, rows, MEM_HEADS, MEM_HD), F32)
    fspec = pl.BlockSpec((1, tm, D_MODEL), lambda l, i: (l, i, 0))
    hspec = pl.BlockSpec((1, tm, MEM_HEADS, MEM_HD), lambda l, i: (l, i, 0, 0))
    return pl.pallas_call(
        _mem_kv_kernel, grid=(nl, rows // tm),
        in_specs=[pl.BlockSpec((tm, D_MODEL), lambda l, i: (i, 0)),
                  pl.BlockSpec((1, 1, D_MODEL), lambda l, i: (l, 0, 0)),
                  pl.BlockSpec((1, D_MODEL, 2 * D_MODEL), lambda l, i: (l, 0, 0))],
        out_specs=(fspec, fspec, hspec, hspec), out_shape=(flat, flat, head, head),
        compiler_params=_cparams("parallel", "parallel"), name="mem_kv")(
            mem, gains.reshape(nl, 1, D_MODEL), w_kv)


def _forward(xp, xs, nbp, w, st, mem_k, mem_v, cache_k, cache_v):
    assert DEPTH == 2
    nbs = xs.shape[0]
    moe_tm = 1024
    moe_steps_p = (xp.shape[0] // moe_tm) * MOE_STEPS
    rwp = tuple(w[k][0] for k in ('rw_mu', 'rw_w0', 'rw_w2', 'rw_a0', 'rw_a2', 'rw_g2',
                                  'rw_k_k', 'rw_k_a', 'rw_r_k', 'rw_ln_w', 'rw_ln_b'))
    w_in0, w_out0 = w['w_in0_bf'][0], w['w_out0_bf'][0]
    moe = lambda x, layer, **kw: moe_dense(x, w['norm_ffn'][layer], *w['router'][layer], *w['moe_g'], layer, **kw)

    u, p_s = linear(xs, w_in0, gain=w['norm_mix'][0], splits=(S5_WIDTH, RW_PROJ))
    y_s5, s5r_s, s5i_s = s5_mixer(u.reshape(1, nbs, S5_WIDTH), st['s5_re'], st['s5_im'], w['s5p'][0],
                                  w['s5_d'][0], w['s5_w_glu'][0], tc=1)
    y_rw, rw_s = rwkv_step(p_s, st['shift'], st['rwkv'], rwp)
    xs = linear(y_s5.reshape(nbs, S5_WIDTH), w_out0[:S5_WIDTH], x2=y_rw, w2=w_out0[S5_WIDTH:], residual=xs)
    q_s = linear(xs, w['w_mq_bf'][0], gain=w['norm_mem'][0])

    zeros = lambda *shape: jnp.zeros(shape, F32)
    u, p_p = linear(xp, w_in0, gain=w['norm_mix'][0], splits=(S5_WIDTH, RW_PROJ), out_tmajor=True, batch=nbp)
    y_s5, s5r_p, s5i_p = s5_mixer(u, zeros(nbp, S5_STATE), zeros(nbp, S5_STATE), w['s5p'][0],
                                  w['s5_d'][0], w['s5_w_glu'][0], tc=128)
    rw_bs = 8
    rw_steps = (nbp // rw_bs) * (p_p.shape[0] // RW_HD)
    job = xattn_step_job(q_s, cache_k, cache_v, 0, tb=nbs // rw_steps)
    (y_rw, rw_p, sh_p), (att_s,) = rwkv_prompt(p_p, zeros(nbp, RW_PROJ), zeros(nbp, RW_HEADS, RW_HD, RW_HD),
                                               rwp, bs=rw_bs, side=job)
    xp = xattn_prompt(xp, w['norm_mem'][0], w['w_mq_bf'][0], mem_k, mem_v, w['w_mo_bf'][0], 0, nb=nbp,
                      pre=(y_s5, w_out0[:S5_WIDTH], y_rw, w_out0[S5_WIDTH:]))

    xs = linear(att_s.reshape(nbs, D_MODEL), w['w_mo_bf'][0], residual=xs)
    xs = moe(xs, 0)
    q, k, v, g = linear(xs, w['w_in1_bf'][0], gain=w['norm_mix'][1], out_dtype=BF16, splits=(NQ, NQ, NV, NV))
    job = retention_step_job(q, k, v, g, st['ret'], pos0=float(PAST_LEN), tb=nbs // moe_steps_p)
    xp, (ret_s, y_ret) = moe(xp, 0, tm=moe_tm, side=job)
    xs = linear(y_ret.reshape(nbs, NV), w['w_out1_bf'][0], residual=xs)
    q_s = linear(xs, w['w_mq_bf'][1], gain=w['norm_mem'][1])

    xp, ret_p = retention_layer_prompt(xp, w['norm_mix'][1], w['w_in1_bf'][0], w['w_out1_bf'][0], nb=nbp)
    xp = xattn_prompt(xp, w['norm_mem'][1], w['w_mq_bf'][1], mem_k, mem_v, w['w_mo_bf'][1], 1, nb=nbp)
    job = xattn_step_job(q_s, cache_k, cache_v, 1, tb=nbs // moe_steps_p)
    y_p, (att_s,) = moe(xp, 1, tm=moe_tm, final_gain=w['norm_final'], side=job)
    xs = linear(att_s.reshape(nbs, D_MODEL), w['w_mo_bf'][1], residual=xs)
    y_s = moe(xs, 1, final_gain=w['norm_final'])

    grp = lambda z, nb: z.reshape(1, nb, S5_GROUPS, S5_N)
    prompt_out = (y_p, grp(s5r_p, nbp), grp(s5i_p, nbp), rw_p[None], sh_p[None], ret_p[None])
    sample_out = (y_s, grp(s5r_s, nbs), grp(s5i_s, nbs), rw_s[None], p_s[None], ret_s[None])
    return prompt_out, sample_out


def kernel(x_prompt, x_sample, mem_prompt, state_s5_re, state_s5_im, state_rwkv, state_shift, state_ret, cache_mem_k, cache_mem_v, norm_mix, norm_mem, norm_ffn, norm_final, w_in0, w_out0, s5_a_re, s5_a_im, s5_b_re, s5_b_im, s5_c_re, s5_c_im, s5_d, s5_log_dt, s5_w_glu, rw_mu, rw_w0, rw_w2, rw_a0, rw_a2, rw_g2, rw_k_k, rw_k_a, rw_r_k, rw_ln_w, rw_ln_b, w_in1, w_out1, mem_norm, w_mq, w_mk, w_mv, w_mo, moe_w_rc, moe_b_rc, moe_w_rf, moe_b_rf, moe_w1, moe_w3, moe_w2):
    w = dict(norm_mix=norm_mix, norm_mem=norm_mem, norm_ffn=norm_ffn, norm_final=norm_final,
             w_in0=w_in0, w_out0=w_out0, s5_a_re=s5_a_re, s5_a_im=s5_a_im, s5_b_re=s5_b_re, s5_b_im=s5_b_im,
             s5_c_re=s5_c_re, s5_c_im=s5_c_im, s5_d=s5_d, s5_log_dt=s5_log_dt, s5_w_glu=s5_w_glu,
             rw_mu=rw_mu, rw_w0=rw_w0, rw_w2=rw_w2, rw_a0=rw_a0, rw_a2=rw_a2, rw_g2=rw_g2,
             rw_k_k=rw_k_k, rw_k_a=rw_k_a, rw_r_k=rw_r_k, rw_ln_w=rw_ln_w, rw_ln_b=rw_ln_b,
             w_in1=w_in1, w_out1=w_out1, w_mq=w_mq, w_mo=w_mo,
             moe_w_rc=moe_w_rc, moe_b_rc=moe_b_rc, moe_w_rf=moe_w_rf, moe_b_rf=moe_b_rf,
             moe_w1=moe_w1, moe_w3=moe_w3, moe_w2=moe_w2)
    nbp, t_len, _ = x_prompt.shape
    nbs = x_sample.shape[0]
    n_even, n_odd = state_s5_re.shape[0], state_ret.shape[0]
    for name in ('w_in0', 'w_out0', 'w_in1', 'w_out1', 'w_mq', 'w_mo'):
        w[name + '_bf'] = w[name].astype(BF16)
    w['s5p'] = [_s5_params(s5_a_re[i], s5_a_im[i], s5_b_re[i], s5_b_im[i], s5_c_re[i], s5_c_im[i], s5_log_dt[i])
                for i in range(n_even)]
    w['router'] = [_router_params(moe_w_rc[l], moe_b_rc[l], moe_w_rf[l], moe_b_rf[l]) for l in range(DEPTH)]
    w['moe_g'] = _group_weights(moe_w1, moe_w3, moe_w2)

    mem = mem_prompt.reshape(nbp * N_MEM, D_MODEL)
    w_kv = jnp.concatenate([w_mk, w_mv], axis=2).astype(BF16)
    mk, mv, mk_h, mv_h = mem_kv(mem, mem_norm, w_kv)
    mem_k_l = mk.reshape(DEPTH, nbp, N_MEM, D_MODEL)
    mem_v_l = mv.reshape(DEPTH, nbp, N_MEM, D_MODEL)
    mem_k_p = mk_h.reshape(DEPTH, nbp, N_MEM, MEM_HEADS, MEM_HD)
    mem_v_p = mv_h.reshape(DEPTH, nbp, N_MEM, MEM_HEADS, MEM_HD)

    assert n_even == 1 and n_odd == 1
    st = dict(s5_re=state_s5_re.reshape(nbs, S5_STATE), s5_im=state_s5_im.reshape(nbs, S5_STATE),
              rwkv=state_rwkv[0], shift=state_shift[0], ret=state_ret[0])
    (y_p, s5r_p, s5i_p, rw_p, sh_p, ret_p), (y_s, s5r_s, s5i_s, rw_s, sh_s, ret_s) = _forward(
        x_prompt.reshape(nbp * t_len, D_MODEL), x_sample.reshape(nbs, D_MODEL), nbp, w, st,
        mem_k_l, mem_v_l, cache_mem_k, cache_mem_v)
    return (y_p.reshape(nbp, t_len, D_MODEL), y_s.reshape(nbs, 1, D_MODEL),
            s5r_p, s5i_p, rw_p, sh_p, ret_p, mem_k_p, mem_v_p, s5r_s, s5i_s, rw_s, sh_s, ret_s)
```

```python
import collections
import functools
import math

import jax
import jax.numpy as jnp
from jax import lax
from jax.experimental import pallas as pl
from jax.experimental.pallas import tpu as pltpu

F32 = jnp.float32
BF16 = jnp.bfloat16

D_MODEL = 1024
DEPTH = 2
PAST_LEN = 16384
S5_WIDTH = 512
S5_GROUP = 16
S5_GROUPS = 32
S5_N = 64
S5_STATE = S5_GROUPS * S5_N
S5_GBLK = 8
RW_WIDTH = 512
RW_HD = 64
RW_HEADS = 8
RW_LORA = 256
RW_PROJ = 3 * RW_WIDTH + RW_LORA
IN0 = S5_WIDTH + RW_PROJ
RET_DK = 256
RET_HEADS = 4
RET_DV = 512
RET_CHUNK = 256
NQ = RET_HEADS * RET_DK
NV = RET_HEADS * RET_DV
IN1 = 2 * NQ + 2 * NV
N_MEM = 256
MEM_HEADS = 4
MEM_HD = 256
MOE_GROUPS = 4
MOE_PER_GROUP = 4
MOE_EXPERTS = 16
MOE_HIDDEN = 256
NORM_EPS = 1e-6
RW_GN_EPS = 64e-5
ROPE_BASE = 10000.0

VMEM_LIMIT = 56 * 1024 * 1024


def _cparams(*sem):
    return pltpu.CompilerParams(dimension_semantics=sem, vmem_limit_bytes=VMEM_LIMIT)


def _bdot(a, b):
    return jnp.dot(a.astype(BF16), b.astype(BF16), preferred_element_type=F32)


def _dot_nt(a, b):
    return lax.dot_general(a.astype(BF16), b.astype(BF16), (((1,), (1,)), ((), ())),
                           preferred_element_type=F32)


def _dot_tn(a, b):
    return lax.dot_general(a.astype(BF16), b.astype(BF16), (((0,), (0,)), ((), ())),
                           preferred_element_type=F32)


def _split3(x):
    hi = x.astype(BF16)
    r1 = x - hi.astype(F32)
    mid = r1.astype(BF16)
    lo = (r1 - mid.astype(F32)).astype(BF16)
    return hi, mid, lo


def _dot_exact_rhs(x, m_bf16, passes=3):
    hi, mid, lo = _split3(x)
    acc = jnp.dot(hi, m_bf16, preferred_element_type=F32)
    if passes > 1:
        acc = acc + jnp.dot(mid, m_bf16, preferred_element_type=F32)
    if passes > 2:
        acc = acc + jnp.dot(lo, m_bf16, preferred_element_type=F32)
    return acc


def _rms(x, g):
    ms = jnp.mean(x * x, axis=-1, keepdims=True)
    return x * lax.rsqrt(ms + NORM_EPS) * g


def _linear_kernel(*refs, norm, two, res):
    it = iter(refs)
    x_ref = next(it)
    g_ref = next(it) if norm else None
    w_ref = next(it)
    x2_ref = next(it) if two else None
    w2_ref = next(it) if two else None
    r_ref = next(it) if res else None
    o_refs = list(it)
    x = x_ref[...].astype(F32)
    if norm:
        x = _rms(x, g_ref[...])
    xb = x.astype(BF16)
    x2b = x2_ref[...].astype(BF16) if two else None
    col = 0
    for o_ref in o_refs:
        m = o_ref.shape[-1]
        step = next((s for s in (512, 256) if m % s == 0), m)
        for j in range(m // step):
            sl = slice(col + j * step, col + (j + 1) * step)
            acc = jnp.dot(xb, w_ref[:, sl], preferred_element_type=F32)
            if two:
                acc = acc + jnp.dot(x2b, w2_ref[:, sl], preferred_element_type=F32)
            if res:
                acc = acc + r_ref[:, sl]
            o_ref[:, j * step:(j + 1) * step] = acc.astype(o_ref.dtype)
        col += m


def _row_spec(tm, width, tmajor_b):
    if tmajor_b is None:
        return pl.BlockSpec((tm, width), lambda i: (i, 0))
    nb, tiles_per_b = tmajor_b
    return pl.BlockSpec((tm, width), lambda i: (i % tiles_per_b, i // tiles_per_b))


def linear(x, w, *, gain=None, x2=None, w2=None, residual=None, out_dtype=F32, tm=512,
           x_tmajor=False, out_tmajor=False, batch=None, splits=None, name="linear"):
    if x_tmajor:
        t_len, nb, k = x.shape
        n = t_len * nb
    else:
        n, k = x.shape
        nb = batch
        t_len = n // nb if nb else None
    m = w.shape[1]
    tm = min(tm, n if not (x_tmajor or out_tmajor) else t_len)
    assert n % tm == 0
    tiles_per_b = (t_len // tm) if (x_tmajor or out_tmajor) else None
    args, specs = [], []

    def add_rows(a, tmajor):
        width = a.shape[-1]
        args.append(a.reshape(t_len, nb * width) if tmajor else a)
        specs.append(_row_spec(tm, width, (nb, tiles_per_b) if tmajor else None))

    add_rows(x, x_tmajor)
    if gain is not None:
        args.append(gain.reshape(1, k).astype(F32))
        specs.append(pl.BlockSpec((1, k), lambda i: (0, 0)))
    args.append(w)
    specs.append(pl.BlockSpec(w.shape, lambda i: (0, 0)))
    if x2 is not None:
        add_rows(x2, x_tmajor)
        args.append(w2)
        specs.append(pl.BlockSpec(w2.shape, lambda i: (0, 0)))
    if residual is not None:
        add_rows(residual, False)
    widths = tuple(splits) if splits else (m,)
    assert sum(widths) == m
    if out_tmajor:
        out_shape = [jax.ShapeDtypeStruct((t_len, nb * mw), out_dtype) for mw in widths]
    else:
        out_shape = [jax.ShapeDtypeStruct((n, mw), out_dtype) for mw in widths]
    out_specs = [_row_spec(tm, mw, (nb, tiles_per_b) if out_tmajor else None) for mw in widths]
    kern = functools.partial(_linear_kernel, norm=gain is not None, two=x2 is not None,
                             res=residual is not None)
    outs = pl.pallas_call(
        kern, grid=(n // tm,), in_specs=specs, out_specs=out_specs, out_shape=out_shape,
        compiler_params=_cparams("parallel"), name=name)(*args)
    if out_tmajor:
        outs = [o.reshape(t_len, nb, mw) for o, mw in zip(outs, widths)]
    return outs if splits else outs[0]


def _s5_kernel(u_ref, h_re_ref, h_im_ref, abar_re_ref, abar_im_ref, bb_re_ref, bb_im_ref,
               cc_re_ref, cc_im_ref, d_ref, wglu_ref, y_ref, s_re_ref, s_im_ref,
               x_re, x_im, st_re, st_im, il_scr, *, tc, nb, flat):
    c = pl.program_id(0)
    nlb = S5_WIDTH // 128
    rows = tc * nb
    nblk = S5_GROUPS // S5_GBLK
    bw_in = S5_GBLK * S5_GROUP
    bw_st = S5_GBLK * S5_N

    @pl.when(c == 0)
    def _():
        st_re[...] = h_re_ref[...]
        st_im[...] = h_im_ref[...]

    if flat:
        for b in range(nb):
            for j in range(nlb):
                il_scr[j, pl.ds(b, tc, stride=nb), :] = u_ref[:, b * S5_WIDTH + j * 128:b * S5_WIDTH + (j + 1) * 128]
        u = jnp.concatenate([il_scr[j] for j in range(nlb)], axis=-1)
    else:
        u = u_ref[...].reshape(rows, S5_WIDTH)
    ub = u.astype(BF16)
    for gb in range(nblk):
        ui = ub[:, gb * bw_in:(gb + 1) * bw_in]
        x_re[:, gb * bw_st:(gb + 1) * bw_st] = jnp.dot(ui, bb_re_ref[gb], preferred_element_type=F32)
        x_im[:, gb * bw_st:(gb + 1) * bw_st] = jnp.dot(ui, bb_im_ref[gb], preferred_element_type=F32)

    lane_blk = 1024
    for lb in range(S5_STATE // lane_blk):
        sl = slice(lb * lane_blk, (lb + 1) * lane_blk)
        ar = jnp.broadcast_to(abar_re_ref[:, sl], (nb, lane_blk))
        ai = jnp.broadcast_to(abar_im_ref[:, sl], (nb, lane_blk))

        def body(t, carry, sl=sl, ar=ar, ai=ai):
            xr, xi = carry
            r0 = pl.multiple_of(t * nb, nb)
            br = x_re[pl.ds(r0, nb), sl]
            bi = x_im[pl.ds(r0, nb), sl]
            nr = ar * xr - ai * xi + br
            ni = ar * xi + ai * xr + bi
            x_re[pl.ds(r0, nb), sl] = nr
            x_im[pl.ds(r0, nb), sl] = ni
            return nr, ni

        fr, fi = lax.fori_loop(0, tc, body, (st_re[:, sl], st_im[:, sl]), unroll=min(tc, 4))
        st_re[:, sl] = fr
        st_im[:, sl] = fi

    for gb in range(nblk):
        xr = x_re[:, gb * bw_st:(gb + 1) * bw_st].astype(BF16)
        xi = x_im[:, gb * bw_st:(gb + 1) * bw_st].astype(BF16)
        yb = (jnp.dot(xr, cc_re_ref[gb], preferred_element_type=F32)
              - jnp.dot(xi, cc_im_ref[gb], preferred_element_type=F32))
        cs = slice(gb * bw_in, (gb + 1) * bw_in)
        yb = yb + d_ref[:, cs] * u[:, cs]
        x_re[:, cs] = jax.nn.gelu(yb)
    y = x_re[:, :S5_WIDTH]
    y = y * jax.nn.sigmoid(jnp.dot(y.astype(BF16), wglu_ref[...], preferred_element_type=F32))
    if flat:
        for j in range(nlb):
            il_scr[j] = y[:, j * 128:(j + 1) * 128]
        for b in range(nb):
            for j in range(nlb):
                y_ref[:, b * S5_WIDTH + j * 128:b * S5_WIDTH + (j + 1) * 128] = (
                    il_scr[j, pl.ds(b, tc, stride=nb), :].astype(y_ref.dtype))
    else:
        y_ref[...] = y.reshape(y_ref.shape).astype(y_ref.dtype)

    @pl.when(c == pl.num_programs(0) - 1)
    def _():
        s_re_ref[...] = st_re[...]
        s_im_ref[...] = st_im[...]


def _s5_params(a_re, a_im, b_re, b_im, c_re, c_im, log_dt):
    dt = jnp.exp(log_dt.astype(F32))[:, None]
    ar, ai = a_re.astype(F32), a_im.astype(F32)
    mag = jnp.exp(dt * ar)
    abar_re, abar_im = mag * jnp.cos(dt * ai), mag * jnp.sin(dt * ai)
    den = ar * ar + ai * ai
    nr = abar_re - 1.0
    coef_re = (nr * ar + abar_im * ai) / den
    coef_im = (abar_im * ar - nr * ai) / den
    cr, ci = coef_re[..., None], coef_im[..., None]
    brf, bif = b_re.astype(F32), b_im.astype(F32)
    bb_re = cr * brf - ci * bif
    bb_im = cr * bif + ci * brf
    nblk = S5_GROUPS // S5_GBLK
    eye = jnp.eye(S5_GBLK, dtype=F32)

    def blockdiag_in(bb):
        t = jnp.transpose(bb, (0, 2, 1)).reshape(nblk, S5_GBLK, S5_GROUP, S5_N)
        m = jnp.einsum('kgcn,gh->kgchn', t, eye)
        return m.reshape(nblk, S5_GBLK * S5_GROUP, S5_GBLK * S5_N).astype(BF16)

    def blockdiag_out(cc):
        t = jnp.transpose(cc.astype(F32), (0, 2, 1)).reshape(nblk, S5_GBLK, S5_N, S5_GROUP)
        m = jnp.einsum('khnc,hg->khngc', t, eye)
        return m.reshape(nblk, S5_GBLK * S5_N, S5_GBLK * S5_GROUP).astype(BF16)

    return (abar_re.reshape(1, S5_STATE), abar_im.reshape(1, S5_STATE),
            blockdiag_in(bb_re), blockdiag_in(bb_im), blockdiag_out(c_re), blockdiag_out(c_im))


def s5_mixer(u_tm, h_re, h_im, params, d_skip, w_glu, *, tc):
    t_len, nb, _ = u_tm.shape
    abar_re, abar_im, bb_re, bb_im, cc_re, cc_im = params
    tc = min(tc, t_len)
    assert t_len % tc == 0 and nb % 8 == 0
    rows = tc * nb
    flat = t_len > 1
    full = lambda a: pl.BlockSpec(a.shape, lambda c: (0,) * a.ndim)
    if flat:
        u_arg = u_tm.reshape(t_len, nb * S5_WIDTH)
        io_spec = pl.BlockSpec((tc, nb * S5_WIDTH), lambda c: (c, 0))
        y_shape = jax.ShapeDtypeStruct((t_len, nb * S5_WIDTH), BF16)
    else:
        u_arg = u_tm
        io_spec = pl.BlockSpec((tc, nb, S5_WIDTH), lambda c: (c, 0, 0))
        y_shape = jax.ShapeDtypeStruct((t_len, nb, S5_WIDTH), BF16)
    args = (u_arg, h_re, h_im, abar_re, abar_im, bb_re, bb_im, cc_re, cc_im,
            d_skip.reshape(1, S5_WIDTH).astype(F32), w_glu.astype(BF16))
    in_specs = [io_spec] + [full(a) for a in args[1:]]
    st_shape = jax.ShapeDtypeStruct((nb, S5_STATE), F32)
    st_spec = pl.BlockSpec((nb, S5_STATE), lambda c: (0, 0))
    scratch = [pltpu.VMEM((rows, S5_STATE), F32), pltpu.VMEM((rows, S5_STATE), F32),
               pltpu.VMEM((nb, S5_STATE), F32), pltpu.VMEM((nb, S5_STATE), F32),
               pltpu.VMEM((S5_WIDTH // 128, rows if flat else 8, 128), F32)]
    y, s_re, s_im = pl.pallas_call(
        functools.partial(_s5_kernel, tc=tc, nb=nb, flat=flat), grid=(t_len // tc,), in_specs=in_specs,
        out_specs=(io_spec, st_spec, st_spec), out_shape=(y_shape, st_shape, st_shape),
        scratch_shapes=scratch, compiler_params=_cparams("arbitrary"), name="s5_mixer")(*args)
    return y.reshape(t_len, nb, S5_WIDTH), s_re, s_im


def _head_ones():
    i = lax.broadcasted_iota(jnp.int32, (RW_WIDTH, RW_WIDTH), 0) // RW_HD
    j = lax.broadcasted_iota(jnp.int32, (RW_WIDTH, RW_WIDTH), 1) // RW_HD
    return jnp.where(i == j, 1.0, 0.0).astype(BF16)


def _softplus(z):
    return jnp.maximum(z, 0.0) + jnp.log1p(jnp.exp(-jnp.abs(z)))


def _rw_prep(p, p_prev, prm, ones_bd):
    mu, w0, w2, a0, a2, g2, k_k, k_a = prm
    xm = p + (p_prev - p) * mu
    o1, o2, o3 = RW_WIDTH, 2 * RW_WIDTH, 3 * RW_WIDTH
    r, k, v = xm[:, :o1], xm[:, o1:o2], xm[:, o2:o3]
    wd, ad, gd = xm[:, o3:o3 + 64], xm[:, o3 + 64:o3 + 128], xm[:, o3 + 128:]
    w = -_softplus(-(w0 + _bdot(jnp.tanh(wd), w2))) - 0.5
    lw = -jnp.exp(w)
    a = jax.nn.sigmoid(a0 + _bdot(ad, a2))
    g = _bdot(jax.nn.sigmoid(gd), g2)
    kk = k * k_k
    ss = _dot_exact_rhs(kk * kk, ones_bd, passes=1)
    kk = kk / jnp.maximum(jnp.sqrt(ss), 1e-12)
    k = k * (1.0 + (a - 1.0) * k_a)
    return r, lw, k, v, -kk, kk * a, g


def _rw_post(o, r, k, v, g, r_k, ln_w, ln_b, ones_bd):
    inv = 1.0 / RW_HD
    mean = _dot_exact_rhs(o, ones_bd, passes=2) * inv
    d = o - mean
    var = _dot_exact_rhs(d * d, ones_bd, passes=1) * inv
    on = d * lax.rsqrt(var + RW_GN_EPS) * ln_w + ln_b
    bonus = _dot_exact_rhs(r * k * r_k, ones_bd, passes=1) * v
    return (on + bonus) * g


def _rw_chunk_kernel(p_ref, shift_ref, h0_ref, mu_ref, w0_ref, w2_ref, a0_ref, a2_ref, g2_ref,
                     kk_ref, ka_ref, rk_ref, lnw_ref, lnb_ref,
                     y_ref, hfin_ref, shout_ref, prev_scr, h_scr, o_scr, *, c_len, bs, side=None):
    c = pl.program_id(1)
    nc = pl.num_programs(1)
    cl = c_len

    @pl.when(c == 0)
    def _():
        prev_scr[...] = shift_ref[:, 0, :]
        h_scr[...] = h0_ref[...]

    ones_bd = _head_ones()
    row = lax.broadcasted_iota(jnp.int32, (cl, RW_PROJ), 0)
    ps, pprevs = [], []
    for bi in range(bs):
        p = p_ref[:, bi * RW_PROJ:(bi + 1) * RW_PROJ]
        pprevs.append(jnp.where(row == 0, prev_scr[bi:bi + 1, :], pltpu.roll(p, 1, 0)))
        ps.append(p)
    p_all = jnp.concatenate(ps, axis=0) if bs > 1 else ps[0]
    pprev_all = jnp.concatenate(pprevs, axis=0) if bs > 1 else pprevs[0]
    prm = (mu_ref[...], w0_ref[...], w2_ref[...], a0_ref[...], a2_ref[...], g2_ref[...],
           kk_ref[...], ka_ref[...])
    r, lw, k, v, a, b, g = _rw_prep(p_all, pprev_all, prm, ones_bd)

    ti = lax.broadcasted_iota(jnp.int32, (cl, cl), 0)
    si = lax.broadcasted_iota(jnp.int32, (cl, cl), 1)
    lmat = jnp.where(ti >= si, 1.0, 0.0).astype(BF16)
    eye = jnp.where(ti == si, 1.0, 0.0)
    mi = lax.broadcasted_iota(jnp.int32, (2 * cl, 3 * cl), 0)
    mj = lax.broadcasted_iota(jnp.int32, (2 * cl, 3 * cl), 1)
    t_row = jnp.where(mi >= cl, mi - cl, mi)
    s_col = jnp.where(mj < cl, mj, jnp.where(mj >= 2 * cl, mj - 2 * cl, -4 * cl))
    keep = (t_row - s_col) >= jnp.where(mi >= cl, 0, 1)
    eye_bf = eye.astype(BF16)

    lhs_l, rhs_l, vh_l, hcat_l, kb_l, etot_l = [], [], [], [], [], []
    for bi in range(bs):
        rs = slice(bi * cl, (bi + 1) * cl)
        lw_b = lw[rs]
        l_hi, l_mid, l_lo = _split3(lw_b)
        cum = (jnp.dot(lmat, l_hi, preferred_element_type=F32)
               + jnp.dot(lmat, l_mid, preferred_element_type=F32)
               + jnp.dot(lmat, l_lo, preferred_element_type=F32))
        tot = cum[cl - 1:cl, :]
        e_neg = jnp.exp(-cum)
        e_rem = jnp.exp(tot - cum)
        at = (a[rs] * jnp.exp(cum - lw_b)).astype(BF16)
        rt = (r[rs] * jnp.exp(cum)).astype(BF16)
        bt = (b[rs] * e_neg).astype(BF16)
        kt = (k[rs] * e_neg).astype(BF16)
        bh = (b[rs] * e_rem).astype(BF16)
        kh = (k[rs] * e_rem).astype(BF16)
        e_tot = jnp.exp(tot)
        vb = v[rs].astype(BF16)
        for h in range(RW_HEADS):
            hs = slice(h * RW_HD, (h + 1) * RW_HD)
            lhs_l.append(jnp.concatenate([at[:, hs], rt[:, hs]], axis=0))
            rhs_l.append(jnp.concatenate([kt[:, hs], eye_bf, bt[:, hs]], axis=0))
            vh_l.append(vb[:, hs])
            kb_l.append(jnp.concatenate([kh[:, hs], bh[:, hs]], axis=0))
            etot_l.append(jnp.sum(eye * e_tot[:, hs], axis=-1, keepdims=True))
            hcat_l.append(h_scr[bi, h])

    nitem = bs * RW_HEADS
    items = range(nitem)
    aa_l = [jnp.where(keep, _dot_nt(lhs_l[i], rhs_l[i]), 0.0).astype(BF16) for i in items]
    pw_l = [aa_l[i][:cl, 2 * cl:] for i in items]
    tinv_l = [eye_bf + pw_l[i] for i in items]
    for _ in range(int(math.log2(cl)) - 1):
        pw_l = [jnp.dot(pw_l[i], pw_l[i], preferred_element_type=F32).astype(BF16) for i in items]
        tinv_l = [jnp.dot(tinv_l[i], eye_bf + pw_l[i], preferred_element_type=F32).astype(BF16) for i in items]
    vh_cat = [jnp.concatenate([vh_l[i], hcat_l[i].astype(BF16)], axis=0) for i in items]
    x1_l = [jnp.dot(aa_l[i][:cl, :2 * cl], vh_cat[i], preferred_element_type=F32).astype(BF16) for i in items]
    u_l = [jnp.dot(tinv_l[i], x1_l[i], preferred_element_type=F32).astype(BF16) for i in items]
    o_l = [jnp.dot(aa_l[i][cl:, :], jnp.concatenate([vh_cat[i], u_l[i]], axis=0),
                   preferred_element_type=F32) for i in items]
    hn_l = [hcat_l[i] * etot_l[i]
            + lax.dot_general(kb_l[i], jnp.concatenate([vh_l[i], u_l[i]], axis=0), (((0,), (0,)), ((), ())),
                              preferred_element_type=F32) for i in items]

    for bi in range(bs):
        for h in range(RW_HEADS):
            i = bi * RW_HEADS + h
            o_scr[bi * cl:(bi + 1) * cl, h * RW_HD:(h + 1) * RW_HD] = o_l[i]
            h_scr[bi, h] = hn_l[i]
        prev_scr[bi:bi + 1, :] = ps[bi][cl - 1:cl, :]

    y = _rw_post(o_scr[...], r, k, v, g, rk_ref[...], lnw_ref[...], lnb_ref[...], ones_bd)
    for bi in range(bs):
        y_ref[:, bi * RW_WIDTH:(bi + 1) * RW_WIDTH] = y[bi * cl:(bi + 1) * cl].astype(y_ref.dtype)
    if side is not None:
        side()

    @pl.when(c == nc - 1)
    def _():
        hfin_ref[...] = h_scr[...]
        for bi in range(bs):
            shout_ref[bi] = ps[bi][cl - 1:cl, :]


def _rw_param_args(mu, w0, w2, a0, a2, g2, k_k, k_a, r_k, ln_w, ln_b):
    row = lambda z: z.reshape(1, -1).astype(F32)
    return (row(mu), row(w0), w2.astype(BF16), row(a0), a2.astype(BF16), g2.astype(BF16),
            row(k_k), row(k_a), row(r_k), row(ln_w), row(ln_b))


def rwkv_prompt(p_tm, shift, s0, params, *, bs=4, side=None):
    c_len = RW_HD
    t_len, nb, _ = p_tm.shape
    assert t_len % c_len == 0 and nb % bs == 0
    prm = _rw_param_args(*params)
    const = lambda a: pl.BlockSpec(a.shape, lambda b, c: (0,) * a.ndim)
    st_spec = pl.BlockSpec((bs, RW_HEADS, RW_HD, RW_HD), lambda b, c: (b, 0, 0, 0))
    sh_spec = pl.BlockSpec((bs, 1, RW_PROJ), lambda b, c: (b, 0, 0))
    in_specs = [pl.BlockSpec((c_len, bs * RW_PROJ), lambda b, c: (c, b)), sh_spec, st_spec] + [const(a) for a in prm]
    out_shape = (jax.ShapeDtypeStruct((t_len, nb * RW_WIDTH), BF16),
                 jax.ShapeDtypeStruct((nb, RW_HEADS, RW_HD, RW_HD), F32),
                 jax.ShapeDtypeStruct((nb, 1, RW_PROJ), F32))
    out_specs = (pl.BlockSpec((c_len, bs * RW_WIDTH), lambda b, c: (c, b)), st_spec, sh_spec)
    scratch = [pltpu.VMEM((bs, RW_PROJ), F32), pltpu.VMEM((bs, RW_HEADS, RW_HD, RW_HD), F32),
               pltpu.VMEM((bs * c_len, RW_WIDTH), F32)]
    h0 = jnp.swapaxes(s0, -1, -2)
    args = [p_tm.reshape(t_len, nb * RW_PROJ), shift.reshape(nb, 1, RW_PROJ), h0, *prm]
    kern = functools.partial(_rw_chunk_kernel, c_len=c_len, bs=bs)
    grid = (nb // bs, t_len // c_len)
    out_shape, out_specs = list(out_shape), list(out_specs)
    if side is not None:
        assert side.steps == grid[0] * grid[1]
        kern = _with_side(kern, len(args), 3, side)
        args += list(side.args)
        in_specs += side.in_specs(grid[1])
        out_shape += list(side.out_shape)
        out_specs += side.out_specs(grid[1])
    outs = pl.pallas_call(
        kern, grid=grid, in_specs=in_specs, out_specs=out_specs, out_shape=out_shape,
        scratch_shapes=scratch, compiler_params=_cparams("parallel", "arbitrary"), name="rwkv_prompt")(*args)
    y, h_fin, sh = outs[:3]
    res = (y.reshape(t_len, nb, RW_WIDTH), jnp.swapaxes(h_fin, -1, -2), sh.reshape(nb, RW_PROJ))
    return res if side is None else (res, outs[3:])


def _rw_step_prep_kernel(p_ref, shift_ref, mu_ref, w0_ref, w2_ref, a0_ref, a2_ref, g2_ref, kk_ref, ka_ref,
                         r_ref, k_ref, v_ref, g_ref, rt_ref, wt_ref, kt_ref, at_ref, bt_ref, vt_ref):
    prm = (mu_ref[...], w0_ref[...], w2_ref[...], a0_ref[...], a2_ref[...], g2_ref[...],
           kk_ref[...], ka_ref[...])
    r, lw, k, v, a, b, g = _rw_prep(p_ref[...], shift_ref[...], prm, _head_ones())
    r_ref[...] = r
    k_ref[...] = k
    v_ref[...] = v
    g_ref[...] = g
    rt_ref[...] = r.T
    wt_ref[...] = jnp.exp(lw).T
    kt_ref[...] = k.T
    at_ref[...] = a.T
    bt_ref[...] = b.T
    vt_ref[...] = v.T


def _rw_step_core_kernel(s_ref, r_ref, w_ref, k_ref, a_ref, b_ref, v_ref, s_out_ref, o_ref):
    r, w, k, a, b = r_ref[0], w_ref[0], k_ref[0], a_ref[0], b_ref[0]
    for j in range(s_ref.shape[1]):
        s = s_ref[0, j]
        sa = jnp.sum(s * a, axis=0, keepdims=True)
        s_new = s * w + sa * b + v_ref[0, j:j + 1, :] * k
        s_out_ref[0, j] = s_new
        o_ref[0, j:j + 1, :] = jnp.sum(s_new * r, axis=0, keepdims=True)


def _rw_step_post_kernel(ot_ref, r_ref, k_ref, v_ref, g_ref, rk_ref, lnw_ref, lnb_ref, y_ref):
    y_ref[...] = _rw_post(ot_ref[...].T, r_ref[...], k_ref[...], v_ref[...], g_ref[...],
                          rk_ref[...], lnw_ref[...], lnb_ref[...], _head_ones()).astype(y_ref.dtype)


def rwkv_step(p, shift, s0, params, *, vb=32):
    n = p.shape[0]
    prm = _rw_param_args(*params)
    vec = jax.ShapeDtypeStruct((n, RW_WIDTH), F32)
    vec_t = jax.ShapeDtypeStruct((RW_WIDTH, n), F32)
    r, k, v, g, rt, wt, kt, at, bt, vt = pl.pallas_call(
        _rw_step_prep_kernel, out_shape=(vec,) * 4 + (vec_t,) * 6, name="rwkv_step_prep")(p, shift, *prm[:8])
    heads = lambda z: z.reshape(RW_HEADS, RW_HD, n)
    k_spec = pl.BlockSpec((1, RW_HD, n), lambda h, j: (h, 0, 0))
    v_spec = pl.BlockSpec((1, vb, n), lambda h, j: (h, j, 0))
    st_spec = pl.BlockSpec((1, vb, RW_HD, n), lambda h, j: (h, j, 0, 0))
    st = jnp.transpose(s0, (1, 2, 3, 0))
    s_new, ot = pl.pallas_call(
        _rw_step_core_kernel, grid=(RW_HEADS, RW_HD // vb),
        in_specs=[st_spec] + [k_spec] * 5 + [v_spec], out_specs=(st_spec, v_spec),
        out_shape=(jax.ShapeDtypeStruct(st.shape, F32), jax.ShapeDtypeStruct((RW_HEADS, RW_HD, n), F32)),
        compiler_params=_cparams("parallel", "parallel"), name="rwkv_step_core")(
            st, heads(rt), heads(wt), heads(kt), heads(at), heads(bt), heads(vt))
    y = pl.pallas_call(
        _rw_step_post_kernel, out_shape=jax.ShapeDtypeStruct((n, RW_WIDTH), BF16), name="rwkv_step_post")(
            ot.reshape(RW_WIDTH, n), r, k, v, g, *prm[8:])
    return y, jnp.transpose(s_new, (3, 0, 1, 2))


RET_LOG_G = tuple(math.log(1.0 - 2.0 ** (-5.0 - h)) for h in range(RET_HEADS))


def _rope_tables(pos, half):
    j = lax.broadcasted_iota(jnp.int32, (1, half), 1).astype(F32)
    inv = jnp.exp(j * (-math.log(ROPE_BASE) / half))
    ang = pos * inv
    return jnp.cos(ang), jnp.sin(ang)


def _rope(x, cos, sin):
    half = RET_DK // 2
    outs = []
    for h in range(RET_HEADS):
        x1 = x[:, h * RET_DK:h * RET_DK + half]
        x2 = x[:, h * RET_DK + half:(h + 1) * RET_DK]
        outs += [x1 * cos - x2 * sin, x1 * sin + x2 * cos]
    return jnp.concatenate(outs, axis=-1)


def _ret_norm_gate(o, g):
    o = o * lax.rsqrt(jnp.mean(o * o, axis=-1, keepdims=True) + NORM_EPS)
    return jax.nn.silu(g) * o


def _ret_tables_kernel(cos_ref, sin_ref, dmask_ref, qdec_ref, kdec_ref, *, c_len):
    t_len = cos_ref.shape[0]
    pos = lax.broadcasted_iota(jnp.int32, (t_len, 1), 0).astype(F32)
    cos, sin = _rope_tables(pos, RET_DK // 2)
    cos_ref[...] = cos
    sin_ref[...] = sin
    ti = lax.broadcasted_iota(jnp.int32, (c_len, 1), 0).astype(F32)
    ii = lax.broadcasted_iota(jnp.int32, (c_len, c_len), 0)
    jj = lax.broadcasted_iota(jnp.int32, (c_len, c_len), 1)
    diff = (ii - jj).astype(F32)
    for h in range(RET_HEADS):
        lg = RET_LOG_G[h]
        dmask_ref[h] = jnp.where(diff >= 0, jnp.exp(lg * jnp.maximum(diff, 0.0)), 0.0)
        qdec_ref[h] = jnp.exp(lg * (ti + 1.0))
        kdec_ref[h] = jnp.exp(lg * (c_len - 1.0 - ti))


def _ret_layer_kernel(x_ref, gain_ref, win_ref, wout_ref, cos_ref, sin_ref, dmask_ref, qdec_ref, kdec_ref,
                      o_ref, sfin_ref, s_scr, y_scr, *, c_len):
    c = pl.program_id(1)

    @pl.when(c == 0)
    def _():
        s_scr[...] = jnp.zeros_like(s_scr)

    x = x_ref[...]
    hb = _rms(x, gain_ref[...]).astype(BF16)
    proj = lambda lo, width: jnp.dot(hb, win_ref[:, lo:lo + width], preferred_element_type=F32)
    cos, sin = cos_ref[...], sin_ref[...]
    q = _rope(proj(0, NQ), cos, sin)
    k = _rope(proj(NQ, NQ), cos, sin) * (RET_DK ** -0.5)
    for h in range(RET_HEADS):
        c_dec = math.exp(RET_LOG_G[h] * c_len)
        qh = q[:, h * RET_DK:(h + 1) * RET_DK]
        kh = k[:, h * RET_DK:(h + 1) * RET_DK]
        vh = proj(2 * NQ + h * RET_DV, RET_DV).astype(BF16)
        s_h = s_scr[h]
        sc = _dot_nt(qh, kh) * dmask_ref[h]
        o = _bdot(sc, vh) + _bdot(qh * qdec_ref[h], s_h)
        s_scr[h] = s_h * c_dec + _dot_tn(kh * kdec_ref[h], vh)
        gh = proj(2 * NQ + NV + h * RET_DV, RET_DV)
        y_scr[:, h * RET_DV:(h + 1) * RET_DV] = _ret_norm_gate(o, gh).astype(BF16)
    o_ref[...] = x + jnp.dot(y_scr[...], wout_ref[...], preferred_element_type=F32)

    @pl.when(c == pl.num_programs(1) - 1)
    def _():
        sfin_ref[0] = s_scr[...]


def retention_layer_prompt(x, gain, w_in, w_out, *, nb, c_len=RET_CHUNK):
    n = x.shape[0]
    t_len = n // nb
    nc = t_len // c_len
    half = RET_DK // 2
    tabs = pl.pallas_call(
        functools.partial(_ret_tables_kernel, c_len=c_len),
        out_shape=(jax.ShapeDtypeStruct((t_len, half), F32), jax.ShapeDtypeStruct((t_len, half), F32),
                   jax.ShapeDtypeStruct((RET_HEADS, c_len, c_len), F32),
                   jax.ShapeDtypeStruct((RET_HEADS, c_len, 1), F32),
                   jax.ShapeDtypeStruct((RET_HEADS, c_len, 1), F32)),
        name="retention_tables")()
    row = pl.BlockSpec((c_len, D_MODEL), lambda b, c: (b * nc + c, 0))
    pos_spec = pl.BlockSpec((c_len, half), lambda b, c: (c, 0))
    const = lambda a: pl.BlockSpec(a.shape, lambda b, c: (0,) * a.ndim)
    st_spec = pl.BlockSpec((1, RET_HEADS, RET_DK, RET_DV), lambda b, c: (b, 0, 0, 0))
    gain = gain.reshape(1, D_MODEL)
    return pl.pallas_call(
        functools.partial(_ret_layer_kernel, c_len=c_len), grid=(nb, nc),
        in_specs=[row, const(gain), const(w_in), const(w_out), pos_spec, pos_spec] + [const(a) for a in tabs[2:]],
        out_specs=(row, st_spec),
        out_shape=(jax.ShapeDtypeStruct((n, D_MODEL), F32),
                   jax.ShapeDtypeStruct((nb, RET_HEADS, RET_DK, RET_DV), F32)),
        scratch_shapes=[pltpu.VMEM((RET_HEADS, RET_DK, RET_DV), F32), pltpu.VMEM((c_len, NV), BF16)],
        compiler_params=_cparams("parallel", "arbitrary"), name="retention_layer")(
            x, gain, w_in, w_out, *tabs)


def _ret_step_rope_kernel(q_ref, k_ref, qo_ref, ko_ref, *, pos0):
    pos = jnp.full((q_ref.shape[0], 1), pos0, F32)
    cos, sin = _rope_tables(pos, RET_DK // 2)
    qo_ref[...] = _rope(q_ref[...].astype(F32), cos, sin).T
    ko_ref[...] = (_rope(k_ref[...].astype(F32), cos, sin) * (RET_DK ** -0.5)).T


def _ret_step_core_kernel(s_ref, qt_ref, kt_ref, v_ref, g_ref, s_out_ref, y_ref):
    tb = s_ref.shape[0]
    step = pl.program_id(0) * pl.num_programs(1) + pl.program_id(1)
    lane = lax.broadcasted_iota(jnp.int32, qt_ref.shape, 1)
    for i in range(tb):
        mine = lane == step * tb + i
        q_col = jnp.sum(jnp.where(mine, qt_ref[...], 0.0), axis=-1, keepdims=True)
        k_col = jnp.sum(jnp.where(mine, kt_ref[...], 0.0), axis=-1, keepdims=True)
        for h in range(RET_HEADS):
            gam = math.exp(RET_LOG_G[h])
            s_h = s_ref[i, h]
            qc = q_col[h * RET_DK:(h + 1) * RET_DK]
            kc = k_col[h * RET_DK:(h + 1) * RET_DK]
            vs = slice(h * RET_DV, (h + 1) * RET_DV)
            vr = v_ref[i, :, vs].astype(F32)
            qk = jnp.sum(qc * kc, axis=0, keepdims=True)
            o = qk * vr + jnp.sum((qc * gam) * s_h, axis=0, keepdims=True)
            s_out_ref[i, h] = s_h * gam + kc * vr
            y_ref[i, :, vs] = _ret_norm_gate(o, g_ref[i, :, vs].astype(F32)).astype(y_ref.dtype)


def retention_step_job(q, k, v, g, s0, *, pos0, tb):
    n = q.shape[0]
    vec_t = jax.ShapeDtypeStruct((NQ, n), F32)
    qt, kt = pl.pallas_call(functools.partial(_ret_step_rope_kernel, pos0=pos0), out_shape=(vec_t, vec_t),
                            name="retention_step_rope")(q, k)
    st = lambda inner: pl.BlockSpec((tb, RET_HEADS, RET_DK, RET_DV), lambda i, j: (i * inner + j, 0, 0, 0))
    rw = lambda inner: pl.BlockSpec((tb, 1, NV), lambda i, j: (i * inner + j, 0, 0))
    whole = lambda inner: pl.BlockSpec((NQ, n), lambda i, j: (0, 0))
    return SideJob(
        body=_ret_step_core_kernel,
        args=(s0, qt, kt, v.reshape(n, 1, NV), g.reshape(n, 1, NV)),
        in_specs=lambda inner: [st(inner), whole(inner), whole(inner), rw(inner), rw(inner)],
        out_shape=(jax.ShapeDtypeStruct(s0.shape, F32), jax.ShapeDtypeStruct((n, 1, NV), BF16)),
        out_specs=lambda inner: [st(inner), rw(inner)],
        steps=n // tb)


def _xattn_prompt_kernel(x_ref, g_ref, wq_ref, mk_ref, mv_ref, wo_ref, *rest, pre):
    if pre:
        ya_ref, wa_ref, yb_ref, wb_ref, o_ref, att_scr = rest
        x = (x_ref[...] + jnp.dot(ya_ref[...], wa_ref[...], preferred_element_type=F32)
             + jnp.dot(yb_ref[...], wb_ref[...], preferred_element_type=F32))
    else:
        o_ref, att_scr = rest
        x = x_ref[...]
    q = jnp.dot(_rms(x, g_ref[...]).astype(BF16), wq_ref[...], preferred_element_type=F32)
    for h in range(MEM_HEADS):
        hs = slice(h * MEM_HD, (h + 1) * MEM_HD)
        s = _dot_nt(q[:, hs], mk_ref[0, :, hs]) * (MEM_HD ** -0.5)
        s = s - jnp.max(s, axis=-1, keepdims=True)
        e = jnp.exp(s)
        p = e / jnp.sum(e, axis=-1, keepdims=True)
        att_scr[:, hs] = _bdot(p, mv_ref[0, :, hs])
    o_ref[...] = x + jnp.dot(att_scr[...].astype(BF16), wo_ref[...], preferred_element_type=F32)


def xattn_prompt(x, gain, w_q, mem_k, mem_v, w_o, layer, *, nb, tm=1024, pre=None):
    n = x.shape[0]
    tiles_per_b = n // nb // tm
    mem_k = mem_k.reshape(-1, N_MEM, D_MODEL)
    mem_v = mem_v.reshape(-1, N_MEM, D_MODEL)
    row = pl.BlockSpec((tm, D_MODEL), lambda i: (i, 0))
    wspec = pl.BlockSpec((D_MODEL, D_MODEL), lambda i: (0, 0))
    mspec = pl.BlockSpec((1, N_MEM, D_MODEL), lambda i: (layer * nb + i // tiles_per_b, 0, 0))
    args = [x, gain.reshape(1, D_MODEL), w_q, mem_k, mem_v, w_o]
    in_specs = [row, pl.BlockSpec((1, D_MODEL), lambda i: (0, 0)), wspec, mspec, mspec, wspec]
    if pre is not None:
        for y, wy in (pre[:2], pre[2:]):
            t_len, _, kw = y.shape
            args += [y.reshape(t_len, nb * kw), wy]
            in_specs += [_row_spec(tm, kw, (nb, tiles_per_b)), pl.BlockSpec(wy.shape, lambda i: (0, 0))]
    return pl.pallas_call(
        functools.partial(_xattn_prompt_kernel, pre=pre is not None), grid=(n // tm,),
        in_specs=in_specs, out_specs=row, out_shape=jax.ShapeDtypeStruct((n, D_MODEL), F32),
        scratch_shapes=[pltpu.VMEM((tm, D_MODEL), F32)],
        compiler_params=_cparams("parallel"), name="xattn_prompt")(*args)


def _xattn_step_kernel(q_ref, mk_ref, mv_ref, o_ref, *, tb):
    half = N_MEM // 2
    both = lambda z: jnp.concatenate([z, z], axis=1)
    fold = lambda z, op: op(z[:, :MEM_HEADS], z[:, MEM_HEADS:])
    for i in range(tb):
        k8 = jnp.concatenate([mk_ref[0, i, :half], mk_ref[0, i, half:]], axis=1)
        v8 = jnp.concatenate([mv_ref[0, i, :half], mv_ref[0, i, half:]], axis=1)
        q8 = jnp.concatenate([q_ref[i], q_ref[i]], axis=0)
        s = jnp.sum(k8 * q8[None], axis=-1, keepdims=True) * (MEM_HD ** -0.5)
        smax = both(fold(jnp.max(s, axis=0, keepdims=True), jnp.maximum))
        e = jnp.exp(s - smax)
        den = both(fold(jnp.sum(e, axis=0, keepdims=True), jnp.add))
        o8 = jnp.sum((e / den) * v8, axis=0)
        o_ref[i] = o8[:MEM_HEADS] + o8[MEM_HEADS:]


def xattn_step_job(q, cache_k, cache_v, layer, *, tb):
    n = q.shape[0]

    def specs(inner):
        qspec = pl.BlockSpec((tb, MEM_HEADS, MEM_HD), lambda i, j: (i * inner + j, 0, 0))
        cspec = pl.BlockSpec((1, tb, N_MEM, MEM_HEADS, MEM_HD), lambda i, j: (layer, i * inner + j, 0, 0, 0))
        return qspec, cspec

    return SideJob(
        body=functools.partial(_xattn_step_kernel, tb=tb),
        args=(q.reshape(n, MEM_HEADS, MEM_HD), cache_k, cache_v),
        in_specs=lambda inner: [specs(inner)[0], specs(inner)[1], specs(inner)[1]],
        out_shape=(jax.ShapeDtypeStruct((n, MEM_HEADS, MEM_HD), F32),),
        out_specs=lambda inner: [specs(inner)[0]],
        steps=n // tb)


def run_job(job, name):
    return pl.pallas_call(
        job.body, grid=(job.steps, 1), in_specs=job.in_specs(1), out_specs=job.out_specs(1),
        out_shape=list(job.out_shape), compiler_params=_cparams("parallel", "arbitrary"), name=name)(*job.args)


ROUTER_LANES = 128
NEG_BIG = -1e30


def _moe_gates(logits):
    lane = lax.broadcasted_iota(jnp.int32, logits.shape, 1)
    first = lambda mask: jnp.min(jnp.where(mask, lane, ROUTER_LANES), axis=-1, keepdims=True)
    is_c = lane < MOE_GROUPS
    lc = jnp.where(is_c, logits, NEG_BIG)
    mc = jnp.max(lc, axis=-1, keepdims=True)
    g_idx = first(lc == mc)
    p_g = 1.0 / jnp.sum(jnp.where(is_c, jnp.exp(lc - mc), 0.0), axis=-1, keepdims=True)
    fl = lane - MOE_GROUPS
    in_g = (fl >= 0) & (fl < MOE_EXPERTS) & ((fl // MOE_PER_GROUP) == g_idx)
    lf = jnp.where(in_g, logits, NEG_BIG)
    m1 = jnp.max(lf, axis=-1, keepdims=True)
    i1 = first(lf == m1)
    lf2 = jnp.where(lane == i1, NEG_BIG, lf)
    m2 = jnp.max(lf2, axis=-1, keepdims=True)
    i2 = first(lf2 == m2)
    e2 = jnp.exp(m2 - m1)
    w_top = 1.0 / (1.0 + e2)
    gate = p_g * (jnp.where(lane == i1, w_top, 0.0) + jnp.where(lane == i2, e2 * w_top, 0.0))
    return gate, g_idx


MOE_CAP = 320
MOE_EPS = 2
MOE_STEPS = MOE_EXPERTS // MOE_EPS

SideJob = collections.namedtuple("SideJob", "body args in_specs out_shape out_specs steps")


def _with_side(main_kernel, n_in, n_out, side):
    ns_in, ns_out = len(side.args), len(side.out_shape)

    def kern(*refs):
        m_in = refs[:n_in]
        s_in = refs[n_in:n_in + ns_in]
        m_out = refs[n_in + ns_in:n_in + ns_in + n_out]
        s_out = refs[n_in + ns_in + n_out:n_in + ns_in + n_out + ns_out]
        scratch = refs[n_in + ns_in + n_out + ns_out:]
        main_kernel(*m_in, *m_out, *scratch, side=lambda: side.body(*s_in, *s_out))

    return kern


def _router_logits(h, wr_ref, br_ref):
    h_hi = h.astype(BF16)
    h_lo = (h - h_hi.astype(F32)).astype(BF16)
    rows = h.shape[0]
    res = jnp.dot(jnp.concatenate([h_hi, h_lo], axis=0), wr_ref[...], preferred_element_type=F32)
    acc = (res[:rows, :ROUTER_LANES] + res[:rows, ROUTER_LANES:]) + (res[rows:, :ROUTER_LANES] + res[rows:, ROUTER_LANES:])
    return acc + br_ref[...]


def _experts_ffn(hb, gate, e0, w1_ref, w3_ref, w2_ref):
    lane = lax.broadcasted_iota(jnp.int32, gate.shape, 1)
    acc = None
    for e in range(MOE_EPS):
        a1 = jnp.dot(hb, w1_ref[e].astype(BF16), preferred_element_type=F32)
        a3 = jnp.dot(hb, w3_ref[e].astype(BF16), preferred_element_type=F32)
        ge = jnp.sum(jnp.where(lane == MOE_GROUPS + e0 + e, gate, 0.0), axis=-1, keepdims=True)
        hid = (jax.nn.silu(a1) * a3 * ge).astype(BF16)
        part = jnp.dot(hid, w2_ref[e].astype(BF16), preferred_element_type=F32)
        acc = part if acc is None else acc + part
    return acc


def _moe_kernel(x_ref, g_ref, wr_ref, br_ref, el_ref, w1_ref, w3_ref, w2_ref, *rest, final_norm, cap, side=None):
    if final_norm:
        fin_ref, o_ref = rest[:2]
        rest = rest[2:]
    else:
        o_ref = rest[0]
        rest = rest[1:]
    h_scr, oh_scr, rk_scr, ohr_scr, rkr_scr, hg_scr, gg_scr, yg_scr, cnt_smem = rest
    tm = x_ref.shape[0]
    step = pl.program_id(1)
    grp = step // (MOE_PER_GROUP // MOE_EPS)
    first_half = step % (MOE_PER_GROUP // MOE_EPS) == 0
    last_half = step % (MOE_PER_GROUP // MOE_EPS) == MOE_PER_GROUP // MOE_EPS - 1

    @pl.when(step == 0)
    def _():
        x = x_ref[...]
        h = _rms(x, g_ref[...])
        gate, g_idx = _moe_gates(_router_logits(h, wr_ref, br_ref))
        g_hi = gate.astype(BF16)
        h_scr[:, :D_MODEL] = h.astype(BF16)
        h_scr[:, D_MODEL:D_MODEL + ROUTER_LANES] = g_hi
        h_scr[:, D_MODEL + ROUTER_LANES:] = (gate - g_hi.astype(F32)).astype(BF16)
        o_ref[...] = x
        lane = lax.broadcasted_iota(jnp.int32, gate.shape, 1)
        onehot = jnp.where(lane == g_idx, 1.0, 0.0)
        rank = jnp.dot(el_ref[...], onehot.astype(BF16), preferred_element_type=F32)
        oh_scr[...] = onehot
        rk_scr[...] = rank
        ohr_scr[...] = onehot.T[:8]
        rkr_scr[...] = rank.T[:8]
        cnt = jnp.sum(onehot, axis=0, keepdims=True)
        for gi in range(MOE_GROUPS):
            cnt_smem[gi] = cnt[0, gi].astype(jnp.int32)

    lane = lax.broadcasted_iota(jnp.int32, (tm, ROUTER_LANES), 1)

    def gather_mat(base):
        slot = jnp.where(ohr_scr[pl.ds(grp, 1), :] > 0.5, rkr_scr[pl.ds(grp, 1), :] - base, -1.0)
        c = lax.broadcasted_iota(jnp.int32, (cap, tm), 0).astype(F32)
        return jnp.where(c == slot, 1.0, 0.0).astype(BF16)

    def scatter_mat(base):
        member = jnp.sum(jnp.where(lane == grp, oh_scr[...], 0.0), axis=-1, keepdims=True)
        rank = jnp.sum(jnp.where(lane == grp, rk_scr[...], 0.0), axis=-1, keepdims=True)
        slot = jnp.where(member > 0.5, rank - base, -1.0)
        c = lax.broadcasted_iota(jnp.int32, (tm, cap), 1).astype(F32)
        return jnp.where(c == slot, 1.0, 0.0).astype(BF16)

    def gather(base):
        got = jnp.dot(gather_mat(base), h_scr[...], preferred_element_type=F32)
        gg = got[:, D_MODEL:D_MODEL + ROUTER_LANES] + got[:, D_MODEL + ROUTER_LANES:]
        return got[:, :D_MODEL].astype(BF16), gg

    @pl.when(first_half)
    def _():
        hg, gg = gather(0.0)
        hg_scr[...] = hg
        gg_scr[...] = gg
        yg_scr[...] = jnp.zeros_like(yg_scr)

    yg_scr[...] += _experts_ffn(hg_scr[...], gg_scr[...], step * MOE_EPS, w1_ref, w3_ref, w2_ref)
    if side is not None:
        side()

    @pl.when(last_half)
    def _():
        o_ref[...] += jnp.dot(scatter_mat(0.0), yg_scr[...].astype(BF16), preferred_element_type=F32)

    def extra_round(r, carry):
        base = (r * cap).astype(F32)
        hg, gg = gather(base)
        y = _experts_ffn(hg, gg, step * MOE_EPS, w1_ref, w3_ref, w2_ref)
        o_ref[...] += jnp.dot(scatter_mat(base), y.astype(BF16), preferred_element_type=F32)
        return carry

    lax.fori_loop(1, (cnt_smem[grp] + cap - 1) // cap, extra_round, 0)

    if final_norm:
        @pl.when(step == MOE_STEPS - 1)
        def _():
            o_ref[...] = _rms(o_ref[...], fin_ref[...])


def moe_dense(x, gain, w_r, b_r, w1, w3, w2, layer, *, tm=512, cap=MOE_CAP, final_gain=None, side=None):
    n = x.shape[0]
    tm = min(tm, n)
    cap = min(cap, tm)
    gain = gain.reshape(1, D_MODEL)
    row = pl.BlockSpec((tm, D_MODEL), lambda i, s: (i, 0))
    const2 = lambda a: pl.BlockSpec(a.shape, lambda i, s: (0,) * a.ndim)
    soff = layer * MOE_STEPS
    wspec = pl.BlockSpec((MOE_EPS, D_MODEL, MOE_HIDDEN), lambda i, s: (soff + s, 0, 0))
    earlier = jnp.tril(jnp.ones((tm, tm), BF16), -1)
    args = [x, gain, w_r, b_r, earlier, w1, w3, w2]
    in_specs = [row, const2(gain), const2(w_r), const2(b_r), const2(earlier), wspec, wspec,
                pl.BlockSpec((MOE_EPS, MOE_HIDDEN, D_MODEL), lambda i, s: (soff + s, 0, 0))]
    if final_gain is not None:
        args.append(final_gain.reshape(1, D_MODEL))
        in_specs.append(const2(args[-1]))
    kern = functools.partial(_moe_kernel, final_norm=final_gain is not None, cap=cap)
    out_shape = [jax.ShapeDtypeStruct((n, D_MODEL), F32)]
    out_specs = [row]
    grid = (n // tm, MOE_STEPS)
    if side is not None:
        assert side.steps == grid[0] * grid[1]
        kern = _with_side(kern, len(args), 1, side)
        args += list(side.args)
        in_specs += side.in_specs(MOE_STEPS)
        out_shape += list(side.out_shape)
        out_specs += side.out_specs(MOE_STEPS)
    outs = pl.pallas_call(
        kern, grid=grid, in_specs=in_specs, out_specs=out_specs, out_shape=out_shape,
        scratch_shapes=[pltpu.VMEM((tm, D_MODEL + 2 * ROUTER_LANES), BF16),
                        pltpu.VMEM((tm, ROUTER_LANES), F32),
                        pltpu.VMEM((tm, ROUTER_LANES), F32),
                        pltpu.VMEM((8, tm), F32),
                        pltpu.VMEM((8, tm), F32),
                        pltpu.VMEM((cap, D_MODEL), BF16),
                        pltpu.VMEM((cap, ROUTER_LANES), F32),
                        pltpu.VMEM((cap, D_MODEL), F32),
                        pltpu.SMEM((MOE_GROUPS,), jnp.int32)],
        compiler_params=_cparams("parallel", "arbitrary"), name="moe")(*args)
    return outs[0] if side is None else (outs[0], outs[1:])


def _group_weights(w1, w3, w2):
    ne = w1.shape[0] * MOE_EXPERTS
    return (w1.astype(BF16).reshape(ne, D_MODEL, MOE_HIDDEN), w3.astype(BF16).reshape(ne, D_MODEL, MOE_HIDDEN),
            w2.astype(BF16).reshape(ne, MOE_HIDDEN, D_MODEL))


def _router_params(w_rc, b_rc, w_rf, b_rf):
    pad = ROUTER_LANES - MOE_GROUPS - MOE_EXPERTS
    w_r = jnp.concatenate([w_rc, w_rf, jnp.zeros((D_MODEL, pad), F32)], axis=1).astype(F32)
    b_r = jnp.concatenate([b_rc, b_rf, jnp.zeros((pad,), F32)]).reshape(1, ROUTER_LANES).astype(F32)
    w_hi = w_r.astype(BF16)
    w_lo = (w_r - w_hi.astype(F32)).astype(BF16)
    return jnp.concatenate([w_hi, w_lo], axis=1), b_r


def _mem_kv_kernel(x_ref, g_ref, w_ref, kf_ref, vf_ref, kh_ref, vh_ref):
    h = _rms(x_ref[...], g_ref[0]).astype(BF16)
    for col, f_ref, h_ref in ((0, kf_ref, kh_ref), (D_MODEL, vf_ref, vh_ref)):
        acc = jnp.dot(h, w_ref[0, :, col:col + D_MODEL], preferred_element_type=F32)
        f_ref[0] = acc.astype(f_ref.dtype)
        for hd in range(MEM_HEADS):
            h_ref[0, :, hd, :] = acc[:, hd * MEM_HD:(hd + 1) * MEM_HD]


def mem_kv(mem, gains, w_kv, *, tm=512):
    rows = mem.shape[0]
    nl = w_kv.shape[0]
    flat = jax.ShapeDtypeStruct((nl, rows, D_MODEL), BF16)
    head = jax.ShapeDtypeStruct((nl, rows, MEM_HEADS, MEM_HD), F32)
    fspec = pl.BlockSpec((1, tm, D_MODEL), lambda l, i: (l, i, 0))
    hspec = pl.BlockSpec((1, tm, MEM_HEADS, MEM_HD), lambda l, i: (l, i, 0, 0))
    return pl.pallas_call(
        _mem_kv_kernel, grid=(nl, rows // tm),
        in_specs=[pl.BlockSpec((tm, D_MODEL), lambda l, i: (i, 0)),
                  pl.BlockSpec((1, 1, D_MODEL), lambda l, i: (l, 0, 0)),
                  pl.BlockSpec((1, D_MODEL, 2 * D_MODEL), lambda l, i: (l, 0, 0))],
        out_specs=(fspec, fspec, hspec, hspec), out_shape=(flat, flat, head, head),
        compiler_params=_cparams("parallel", "parallel"), name="mem_kv")(
            mem, gains.reshape(nl, 1, D_MODEL), w_kv)


def _forward(xp, xs, nbp, w, st, mem_k, mem_v, cache_k, cache_v):
    assert DEPTH == 2
    nbs = xs.shape[0]
    moe_tm = 1024
    moe_steps_p = (xp.shape[0] // moe_tm) * MOE_STEPS
    rwp = tuple(w[k][0] for k in ('rw_mu', 'rw_w0', 'rw_w2', 'rw_a0', 'rw_a2', 'rw_g2',
                                  'rw_k_k', 'rw_k_a', 'rw_r_k', 'rw_ln_w', 'rw_ln_b'))
    w_in0, w_out0 = w['w_in0_bf'][0], w['w_out0_bf'][0]
    moe = lambda x, layer, **kw: moe_dense(x, w['norm_ffn'][layer], *w['router'][layer], *w['moe_g'], layer, **kw)

    u, p_s = linear(xs, w_in0, gain=w['norm_mix'][0], splits=(S5_WIDTH, RW_PROJ))
    y_s5, s5r_s, s5i_s = s5_mixer(u.reshape(1, nbs, S5_WIDTH), st['s5_re'], st['s5_im'], w['s5p'][0],
                                  w['s5_d'][0], w['s5_w_glu'][0], tc=1)
    y_rw, rw_s = rwkv_step(p_s, st['shift'], st['rwkv'], rwp)
    xs = linear(y_s5.reshape(nbs, S5_WIDTH), w_out0[:S5_WIDTH], x2=y_rw, w2=w_out0[S5_WIDTH:], residual=xs)
    q_s = linear(xs, w['w_mq_bf'][0], gain=w['norm_mem'][0])

    zeros = lambda *shape: jnp.zeros(shape, F32)
    u, p_p = linear(xp, w_in0, gain=w['norm_mix'][0], splits=(S5_WIDTH, RW_PROJ), out_tmajor=True, batch=nbp)
    y_s5, s5r_p, s5i_p = s5_mixer(u, zeros(nbp, S5_STATE), zeros(nbp, S5_STATE), w['s5p'][0],
                                  w['s5_d'][0], w['s5_w_glu'][0], tc=128)
    rw_bs = 8
    rw_steps = (nbp // rw_bs) * (p_p.shape[0] // RW_HD)
    job = xattn_step_job(q_s, cache_k, cache_v, 0, tb=nbs // rw_steps)
    (y_rw, rw_p, sh_p), (att_s,) = rwkv_prompt(p_p, zeros(nbp, RW_PROJ), zeros(nbp, RW_HEADS, RW_HD, RW_HD),
                                               rwp, bs=rw_bs, side=job)
    xp = xattn_prompt(xp, w['norm_mem'][0], w['w_mq_bf'][0], mem_k, mem_v, w['w_mo_bf'][0], 0, nb=nbp,
                      pre=(y_s5, w_out0[:S5_WIDTH], y_rw, w_out0[S5_WIDTH:]))

    xs = linear(att_s.reshape(nbs, D_MODEL), w['w_mo_bf'][0], residual=xs)
    xs = moe(xs, 0)
    q, k, v, g = linear(xs, w['w_in1_bf'][0], gain=w['norm_mix'][1], out_dtype=BF16, splits=(NQ, NQ, NV, NV))
    job = retention_step_job(q, k, v, g, st['ret'], pos0=float(PAST_LEN), tb=nbs // moe_steps_p)
    xp, (ret_s, y_ret) = moe(xp, 0, tm=moe_tm, side=job)
    xs = linear(y_ret.reshape(nbs, NV), w['w_out1_bf'][0], residual=xs)
    q_s = linear(xs, w['w_mq_bf'][1], gain=w['norm_mem'][1])

    xp, ret_p = retention_layer_prompt(xp, w['norm_mix'][1], w['w_in1_bf'][0], w['w_out1_bf'][0], nb=nbp)
    xp = xattn_prompt(xp, w['norm_mem'][1], w['w_mq_bf'][1], mem_k, mem_v, w['w_mo_bf'][1], 1, nb=nbp)
    job = xattn_step_job(q_s, cache_k, cache_v, 1, tb=nbs // moe_steps_p)
    y_p, (att_s,) = moe(xp, 1, tm=moe_tm, final_gain=w['norm_final'], side=job)
    xs = linear(att_s.reshape(nbs, D_MODEL), w['w_mo_bf'][1], residual=xs)
    y_s = moe(xs, 1, final_gain=w['norm_final'])

    grp = lambda z, nb: z.reshape(1, nb, S5_GROUPS, S5_N)
    prompt_out = (y_p, grp(s5r_p, nbp), grp(s5i_p, nbp), rw_p[None], sh_p[None], ret_p[None])
    sample_out = (y_s, grp(s5r_s, nbs), grp(s5i_s, nbs), rw_s[None], p_s[None], ret_s[None])
    return prompt_out, sample_out


def kernel(x_prompt, x_sample, mem_prompt, state_s5_re, state_s5_im, state_rwkv, state_shift, state_ret, cache_mem_k, cache_mem_v, norm_mix, norm_mem, norm_ffn, norm_final, w_in0, w_out0, s5_a_re, s5_a_im, s5_b_re, s5_b_im, s5_c_re, s5_c_im, s5_d, s5_log_dt, s5_w_glu, rw_mu, rw_w0, rw_w2, rw_a0, rw_a2, rw_g2, rw_k_k, rw_k_a, rw_r_k, rw_ln_w, rw_ln_b, w_in1, w_out1, mem_norm, w_mq, w_mk, w_mv, w_mo, moe_w_rc, moe_b_rc, moe_w_rf, moe_b_rf, moe_w1, moe_w3, moe_w2):
    w = dict(norm_mix=norm_mix, norm_mem=norm_mem, norm_ffn=norm_ffn, norm_final=norm_final,
             w_in0=w_in0, w_out0=w_out0, s5_a_re=s5_a_re, s5_a_im=s5_a_im, s5_b_re=s5_b_re, s5_b_im=s5_b_im,
             s5_c_re=s5_c_re, s5_c_im=s5_c_im, s5_d=s5_d, s5_log_dt=s5_log_dt, s5_w_glu=s5_w_glu,
             rw_mu=rw_mu, rw_w0=rw_w0, rw_w2=rw_w2, rw_a0=rw_a0, rw_a2=rw_a2, rw_g2=rw_g2,
             rw_k_k=rw_k_k, rw_k_a=rw_k_a, rw_r_k=rw_r_k, rw_ln_w=rw_ln_w, rw_ln_b=rw_ln_b,
             w_in1=w_in1, w_out1=w_out1, w_mq=w_mq, w_mo=w_mo,
             moe_w_rc=moe_w_rc, moe_b_rc=moe_b_rc, moe_w_rf=moe_w_rf, moe_b_rf=moe_b_rf,
             moe_w1=moe_w1, moe_w3=moe_w3, moe_w2=moe_w2)
    nbp, t_len, _ = x_prompt.shape
    nbs = x_sample.shape[0]
    n_even, n_odd = state_s5_re.shape[0], state_ret.shape[0]
    for name in ('w_in0', 'w_out0', 'w_in1', 'w_out1', 'w_mq', 'w_mo'):
        w[name + '_bf'] = w[name].astype(BF16)
    w['s5p'] = [_s5_params(s5_a_re[i], s5_a_im[i], s5_b_re[i], s5_b_im[i], s5_c_re[i], s5_c_im[i], s5_log_dt[i])
                for i in range(n_even)]
    w['router'] = [_router_params(moe_w_rc[l], moe_b_rc[l], moe_w_rf[l], moe_b_rf[l]) for l in range(DEPTH)]
    w['moe_g'] = _group_weights(moe_w1, moe_w3, moe_w2)

    mem = mem_prompt.reshape(nbp * N_MEM, D_MODEL)
    w_kv = jnp.concatenate([w_mk, w_mv], axis=2).astype(BF16)
    mk, mv, mk_h, mv_h = mem_kv(mem, mem_norm, w_kv)
    mem_k_l = mk.reshape(DEPTH, nbp, N_MEM, D_MODEL)
    mem_v_l = mv.reshape(DEPTH, nbp, N_MEM, D_MODEL)
    mem_k_p = mk_h.reshape(DEPTH, nbp, N_MEM, MEM_HEADS, MEM_HD)
    mem_v_p = mv_h.reshape(DEPTH, nbp, N_MEM, MEM_HEADS, MEM_HD)

    assert n_even == 1 and n_odd == 1
    st = dict(s5_re=state_s5_re.reshape(nbs, S5_STATE), s5_im=state_s5_im.reshape(nbs, S5_STATE),
              rwkv=state_rwkv[0], shift=state_shift[0], ret=state_ret[0])
    (y_p, s5r_p, s5i_p, rw_p, sh_p, ret_p), (y_s, s5r_s, s5i_s, rw_s, sh_s, ret_s) = _forward(
        x_prompt.reshape(nbp * t_len, D_MODEL), x_sample.reshape(nbs, D_MODEL), nbp, w, st,
        mem_k_l, mem_v_l, cache_mem_k, cache_mem_v)
    return (y_p.reshape(nbp, t_len, D_MODEL), y_s.reshape(nbs, 1, D_MODEL),
            s5r_p, s5i_p, rw_p, sh_p, ret_p, mem_k_p, mem_v_p, s5r_s, s5i_s, rw_s, sh_s, ret_s)
```

```python
import collections
import functools
import math

import jax
import jax.numpy as jnp
from jax import lax
from jax.experimental import pallas as pl
from jax.experimental.pallas import tpu as pltpu

F32 = jnp.float32
BF16 = jnp.bfloat16

D_MODEL = 1024
DEPTH = 2
PAST_LEN = 16384
S5_WIDTH = 512
S5_GROUP = 16
S5_GROUPS = 32
S5_N = 64
S5_STATE = S5_GROUPS * S5_N
S5_GBLK = 8
RW_WIDTH = 512
RW_HD = 64
RW_HEADS = 8
RW_LORA = 256
RW_PROJ = 3 * RW_WIDTH + RW_LORA
IN0 = S5_WIDTH + RW_PROJ
RET_DK = 256
RET_HEADS = 4
RET_DV = 512
RET_CHUNK = 256
NQ = RET_HEADS * RET_DK
NV = RET_HEADS * RET_DV
IN1 = 2 * NQ + 2 * NV
N_MEM = 256
MEM_HEADS = 4
MEM_HD = 256
MOE_GROUPS = 4
MOE_PER_GROUP = 4
MOE_EXPERTS = 16
MOE_HIDDEN = 256
NORM_EPS = 1e-6
RW_GN_EPS = 64e-5
ROPE_BASE = 10000.0

VMEM_LIMIT = 56 * 1024 * 1024


def _cparams(*sem):
    return pltpu.CompilerParams(dimension_semantics=sem, vmem_limit_bytes=VMEM_LIMIT)


def _bdot(a, b):
    return jnp.dot(a.astype(BF16), b.astype(BF16), preferred_element_type=F32)


def _dot_nt(a, b):
    return lax.dot_general(a.astype(BF16), b.astype(BF16), (((1,), (1,)), ((), ())),
                           preferred_element_type=F32)


def _dot_tn(a, b):
    return lax.dot_general(a.astype(BF16), b.astype(BF16), (((0,), (0,)), ((), ())),
                           preferred_element_type=F32)


def _split3(x):
    hi = x.astype(BF16)
    r1 = x - hi.astype(F32)
    mid = r1.astype(BF16)
    lo = (r1 - mid.astype(F32)).astype(BF16)
    return hi, mid, lo


def _dot_exact_rhs(x, m_bf16, passes=3):
    hi, mid, lo = _split3(x)
    acc = jnp.dot(hi, m_bf16, preferred_element_type=F32)
    if passes > 1:
        acc = acc + jnp.dot(mid, m_bf16, preferred_element_type=F32)
    if passes > 2:
        acc = acc + jnp.dot(lo, m_bf16, preferred_element_type=F32)
    return acc


def _rms(x, g):
    ms = jnp.mean(x * x, axis=-1, keepdims=True)
    return x * lax.rsqrt(ms + NORM_EPS) * g


def _linear_kernel(*refs, norm, two, res):
    it = iter(refs)
    x_ref = next(it)
    g_ref = next(it) if norm else None
    w_ref = next(it)
    x2_ref = next(it) if two else None
    w2_ref = next(it) if two else None
    r_ref = next(it) if res else None
    o_refs = list(it)
    x = x_ref[...].astype(F32)
    if norm:
        x = _rms(x, g_ref[...])
    xb = x.astype(BF16)
    x2b = x2_ref[...].astype(BF16) if two else None
    col = 0
    for o_ref in o_refs:
        m = o_ref.shape[-1]
        step = next((s for s in (512, 256) if m % s == 0), m)
        for j in range(m // step):
            sl = slice(col + j * step, col + (j + 1) * step)
            acc = jnp.dot(xb, w_ref[:, sl], preferred_element_type=F32)
            if two:
                acc = acc + jnp.dot(x2b, w2_ref[:, sl], preferred_element_type=F32)
            if res:
                acc = acc + r_ref[:, sl]
            o_ref[:, j * step:(j + 1) * step] = acc.astype(o_ref.dtype)
        col += m


def _row_spec(tm, width, tmajor_b):
    if tmajor_b is None:
        return pl.BlockSpec((tm, width), lambda i: (i, 0))
    nb, tiles_per_b = tmajor_b
    return pl.BlockSpec((tm, width), lambda i: (i % tiles_per_b, i // tiles_per_b))


def linear(x, w, *, gain=None, x2=None, w2=None, residual=None, out_dtype=F32, tm=512,
           x_tmajor=False, out_tmajor=False, batch=None, splits=None, name="linear"):
    if x_tmajor:
        t_len, nb, k = x.shape
        n = t_len * nb
    else:
        n, k = x.shape
        nb = batch
        t_len = n // nb if nb else None
    m = w.shape[1]
    tm = min(tm, n if not (x_tmajor or out_tmajor) else t_len)
    assert n % tm == 0
    tiles_per_b = (t_len // tm) if (x_tmajor or out_tmajor) else None
    args, specs = [], []

    def add_rows(a, tmajor):
        width = a.shape[-1]
        args.append(a.reshape(t_len, nb * width) if tmajor else a)
        specs.append(_row_spec(tm, width, (nb, tiles_per_b) if tmajor else None))

    add_rows(x, x_tmajor)
    if gain is not None:
        args.append(gain.reshape(1, k).astype(F32))
        specs.append(pl.BlockSpec((1, k), lambda i: (0, 0)))
    args.append(w)
    specs.append(pl.BlockSpec(w.shape, lambda i: (0, 0)))
    if x2 is not None:
        add_rows(x2, x_tmajor)
        args.append(w2)
        specs.append(pl.BlockSpec(w2.shape, lambda i: (0, 0)))
    if residual is not None:
        add_rows(residual, False)
    widths = tuple(splits) if splits else (m,)
    assert sum(widths) == m
    if out_tmajor:
        out_shape = [jax.ShapeDtypeStruct((t_len, nb * mw), out_dtype) for mw in widths]
    else:
        out_shape = [jax.ShapeDtypeStruct((n, mw), out_dtype) for mw in widths]
    out_specs = [_row_spec(tm, mw, (nb, tiles_per_b) if out_tmajor else None) for mw in widths]
    kern = functools.partial(_linear_kernel, norm=gain is not None, two=x2 is not None,
                             res=residual is not None)
    outs = pl.pallas_call(
        kern, grid=(n // tm,), in_specs=specs, out_specs=out_specs, out_shape=out_shape,
        compiler_params=_cparams("parallel"), name=name)(*args)
    if out_tmajor:
        outs = [o.reshape(t_len, nb, mw) for o, mw in zip(outs, widths)]
    return outs if splits else outs[0]


def _s5_kernel(u_ref, h_re_ref, h_im_ref, abar_re_ref, abar_im_ref, bb_re_ref, bb_im_ref,
               cc_re_ref, cc_im_ref, d_ref, wglu_ref, y_ref, s_re_ref, s_im_ref,
               x_re, x_im, st_re, st_im, il_scr, *, tc, nb, flat):
    c = pl.program_id(0)
    nlb = S5_WIDTH // 128
    rows = tc * nb
    nblk = S5_GROUPS // S5_GBLK
    bw_in = S5_GBLK * S5_GROUP
    bw_st = S5_GBLK * S5_N

    @pl.when(c == 0)
    def _():
        st_re[...] = h_re_ref[...]
        st_im[...] = h_im_ref[...]

    if flat:
        for b in range(nb):
            for j in range(nlb):
                il_scr[j, pl.ds(b, tc, stride=nb), :] = u_ref[:, b * S5_WIDTH + j * 128:b * S5_WIDTH + (j + 1) * 128]
        u = jnp.concatenate([il_scr[j] for j in range(nlb)], axis=-1)
    else:
        u = u_ref[...].reshape(rows, S5_WIDTH)
    ub = u.astype(BF16)
    for gb in range(nblk):
        ui = ub[:, gb * bw_in:(gb + 1) * bw_in]
        x_re[:, gb * bw_st:(gb + 1) * bw_st] = jnp.dot(ui, bb_re_ref[gb], preferred_element_type=F32)
        x_im[:, gb * bw_st:(gb + 1) * bw_st] = jnp.dot(ui, bb_im_ref[gb], preferred_element_type=F32)

    lane_blk = 1024
    for lb in range(S5_STATE // lane_blk):
        sl = slice(lb * lane_blk, (lb + 1) * lane_blk)
        ar = jnp.broadcast_to(abar_re_ref[:, sl], (nb, lane_blk))
        ai = jnp.broadcast_to(abar_im_ref[:, sl], (nb, lane_blk))

        def body(t, carry, sl=sl, ar=ar, ai=ai):
            xr, xi = carry
            r0 = pl.multiple_of(t * nb, nb)
            br = x_re[pl.ds(r0, nb), sl]
            bi = x_im[pl.ds(r0, nb), sl]
            nr = ar * xr - ai * xi + br
            ni = ar * xi + ai * xr + bi
            x_re[pl.ds(r0, nb), sl] = nr
            x_im[pl.ds(r0, nb), sl] = ni
            return nr, ni

        fr, fi = lax.fori_loop(0, tc, body, (st_re[:, sl], st_im[:, sl]), unroll=min(tc, 4))
        st_re[:, sl] = fr
        st_im[:, sl] = fi

    for gb in range(nblk):
        xr = x_re[:, gb * bw_st:(gb + 1) * bw_st].astype(BF16)
        xi = x_im[:, gb * bw_st:(gb + 1) * bw_st].astype(BF16)
        yb = (jnp.dot(xr, cc_re_ref[gb], preferred_element_type=F32)
              - jnp.dot(xi, cc_im_ref[gb], preferred_element_type=F32))
        cs = slice(gb * bw_in, (gb + 1) * bw_in)
        yb = yb + d_ref[:, cs] * u[:, cs]
        x_re[:, cs] = jax.nn.gelu(yb)
    y = x_re[:, :S5_WIDTH]
    y = y * jax.nn.sigmoid(jnp.dot(y.astype(BF16), wglu_ref[...], preferred_element_type=F32))
    if flat:
        for j in range(nlb):
            il_scr[j] = y[:, j * 128:(j + 1) * 128]
        for b in range(nb):
            for j in range(nlb):
                y_ref[:, b * S5_WIDTH + j * 128:b * S5_WIDTH + (j + 1) * 128] = (
                    il_scr[j, pl.ds(b, tc, stride=nb), :].astype(y_ref.dtype))
    else:
        y_ref[...] = y.reshape(y_ref.shape).astype(y_ref.dtype)

    @pl.when(c == pl.num_programs(0) - 1)
    def _():
        s_re_ref[...] = st_re[...]
        s_im_ref[...] = st_im[...]


def _s5_params(a_re, a_im, b_re, b_im, c_re, c_im, log_dt):
    dt = jnp.exp(log_dt.astype(F32))[:, None]
    ar, ai = a_re.astype(F32), a_im.astype(F32)
    mag = jnp.exp(dt * ar)
    abar_re, abar_im = mag * jnp.cos(dt * ai), mag * jnp.sin(dt * ai)
    den = ar * ar + ai * ai
    nr = abar_re - 1.0
    coef_re = (nr * ar + abar_im * ai) / den
    coef_im = (abar_im * ar - nr * ai) / den
    cr, ci = coef_re[..., None], coef_im[..., None]
    brf, bif = b_re.astype(F32), b_im.astype(F32)
    bb_re = cr * brf - ci * bif
    bb_im = cr * bif + ci * brf
    nblk = S5_GROUPS // S5_GBLK
    eye = jnp.eye(S5_GBLK, dtype=F32)

    def blockdiag_in(bb):
        t = jnp.transpose(bb, (0, 2, 1)).reshape(nblk, S5_GBLK, S5_GROUP, S5_N)
        m = jnp.einsum('kgcn,gh->kgchn', t, eye)
        return m.reshape(nblk, S5_GBLK * S5_GROUP, S5_GBLK * S5_N).astype(BF16)

    def blockdiag_out(cc):
        t = jnp.transpose(cc.astype(F32), (0, 2, 1)).reshape(nblk, S5_GBLK, S5_N, S5_GROUP)
        m = jnp.einsum('khnc,hg->khngc', t, eye)
        return m.reshape(nblk, S5_GBLK * S5_N, S5_GBLK * S5_GROUP).astype(BF16)

    return (abar_re.reshape(1, S5_STATE), abar_im.reshape(1, S5_STATE),
            blockdiag_in(bb_re), blockdiag_in(bb_im), blockdiag_out(c_re), blockdiag_out(c_im))


def s5_mixer(u_tm, h_re, h_im, params, d_skip, w_glu, *, tc):
    t_len, nb, _ = u_tm.shape
    abar_re, abar_im, bb_re, bb_im, cc_re, cc_im = params
    tc = min(tc, t_len)
    assert t_len % tc == 0 and nb % 8 == 0
    rows = tc * nb
    flat = t_len > 1
    full = lambda a: pl.BlockSpec(a.shape, lambda c: (0,) * a.ndim)
    if flat:
        u_arg = u_tm.reshape(t_len, nb * S5_WIDTH)
        io_spec = pl.BlockSpec((tc, nb * S5_WIDTH), lambda c: (c, 0))
        y_shape = jax.ShapeDtypeStruct((t_len, nb * S5_WIDTH), BF16)
    else:
        u_arg = u_tm
        io_spec = pl.BlockSpec((tc, nb, S5_WIDTH), lambda c: (c, 0, 0))
        y_shape = jax.ShapeDtypeStruct((t_len, nb, S5_WIDTH), BF16)
    args = (u_arg, h_re, h_im, abar_re, abar_im, bb_re, bb_im, cc_re, cc_im,
            d_skip.reshape(1, S5_WIDTH).astype(F32), w_glu.astype(BF16))
    in_specs = [io_spec] + [full(a) for a in args[1:]]
    st_shape = jax.ShapeDtypeStruct((nb, S5_STATE), F32)
    st_spec = pl.BlockSpec((nb, S5_STATE), lambda c: (0, 0))
    scratch = [pltpu.VMEM((rows, S5_STATE), F32), pltpu.VMEM((rows, S5_STATE), F32),
               pltpu.VMEM((nb, S5_STATE), F32), pltpu.VMEM((nb, S5_STATE), F32),
               pltpu.VMEM((S5_WIDTH // 128, rows if flat else 8, 128), F32)]
    y, s_re, s_im = pl.pallas_call(
        functools.partial(_s5_kernel, tc=tc, nb=nb, flat=flat), grid=(t_len // tc,), in_specs=in_specs,
        out_specs=(io_spec, st_spec, st_spec), out_shape=(y_shape, st_shape, st_shape),
        scratch_shapes=scratch, compiler_params=_cparams("arbitrary"), name="s5_mixer")(*args)
    return y.reshape(t_len, nb, S5_WIDTH), s_re, s_im


def _head_ones():
    i = lax.broadcasted_iota(jnp.int32, (RW_WIDTH, RW_WIDTH), 0) // RW_HD
    j = lax.broadcasted_iota(jnp.int32, (RW_WIDTH, RW_WIDTH), 1) // RW_HD
    return jnp.where(i == j, 1.0, 0.0).astype(BF16)


def _softplus(z):
    return jnp.maximum(z, 0.0) + jnp.log1p(jnp.exp(-jnp.abs(z)))


def _rw_prep(p, p_prev, prm, ones_bd):
    mu, w0, w2, a0, a2, g2, k_k, k_a = prm
    xm = p + (p_prev - p) * mu
    o1, o2, o3 = RW_WIDTH, 2 * RW_WIDTH, 3 * RW_WIDTH
    r, k, v = xm[:, :o1], xm[:, o1:o2], xm[:, o2:o3]
    wd, ad, gd = xm[:, o3:o3 + 64], xm[:, o3 + 64:o3 + 128], xm[:, o3 + 128:]
    w = -_softplus(-(w0 + _bdot(jnp.tanh(wd), w2))) - 0.5
    lw = -jnp.exp(w)
    a = jax.nn.sigmoid(a0 + _bdot(ad, a2))
    g = _bdot(jax.nn.sigmoid(gd), g2)
    kk = k * k_k
    ss = _dot_exact_rhs(kk * kk, ones_bd, passes=1)
    kk = kk / jnp.maximum(jnp.sqrt(ss), 1e-12)
    k = k * (1.0 + (a - 1.0) * k_a)
    return r, lw, k, v, -kk, kk * a, g


def _rw_post(o, r, k, v, g, r_k, ln_w, ln_b, ones_bd):
    inv = 1.0 / RW_HD
    mean = _dot_exact_rhs(o, ones_bd, passes=2) * inv
    d = o - mean
    var = _dot_exact_rhs(d * d, ones_bd, passes=1) * inv
    on = d * lax.rsqrt(var + RW_GN_EPS) * ln_w + ln_b
    bonus = _dot_exact_rhs(r * k * r_k, ones_bd, passes=1) * v
    return (on + bonus) * g


def _rw_chunk_kernel(p_ref, shift_ref, h0_ref, mu_ref, w0_ref, w2_ref, a0_ref, a2_ref, g2_ref,
                     kk_ref, ka_ref, rk_ref, lnw_ref, lnb_ref,
                     y_ref, hfin_ref, shout_ref, prev_scr, h_scr, o_scr, *, c_len, bs, side=None):
    c = pl.program_id(1)
    nc = pl.num_programs(1)
    cl = c_len

    @pl.when(c == 0)
    def _():
        prev_scr[...] = shift_ref[:, 0, :]
        h_scr[...] = h0_ref[...]

    ones_bd = _head_ones()
    row = lax.broadcasted_iota(jnp.int32, (cl, RW_PROJ), 0)
    ps, pprevs = [], []
    for bi in range(bs):
        p = p_ref[:, bi * RW_PROJ:(bi + 1) * RW_PROJ]
        pprevs.append(jnp.where(row == 0, prev_scr[bi:bi + 1, :], pltpu.roll(p, 1, 0)))
        ps.append(p)
    p_all = jnp.concatenate(ps, axis=0) if bs > 1 else ps[0]
    pprev_all = jnp.concatenate(pprevs, axis=0) if bs > 1 else pprevs[0]
    prm = (mu_ref[...], w0_ref[...], w2_ref[...], a0_ref[...], a2_ref[...], g2_ref[...],
           kk_ref[...], ka_ref[...])
    r, lw, k, v, a, b, g = _rw_prep(p_all, pprev_all, prm, ones_bd)

    ti = lax.broadcasted_iota(jnp.int32, (cl, cl), 0)
    si = lax.broadcasted_iota(jnp.int32, (cl, cl), 1)
    lmat = jnp.where(ti >= si, 1.0, 0.0).astype(BF16)
    eye = jnp.where(ti == si, 1.0, 0.0)
    mi = lax.broadcasted_iota(jnp.int32, (2 * cl, 3 * cl), 0)
    mj = lax.broadcasted_iota(jnp.int32, (2 * cl, 3 * cl), 1)
    t_row = jnp.where(mi >= cl, mi - cl, mi)
    s_col = jnp.where(mj < cl, mj, jnp.where(mj >= 2 * cl, mj - 2 * cl, -4 * cl))
    keep = (t_row - s_col) >= jnp.where(mi >= cl, 0, 1)
    eye_bf = eye.astype(BF16)

    lhs_l, rhs_l, vh_l, hcat_l, kb_l, etot_l = [], [], [], [], [], []
    for bi in range(bs):
        rs = slice(bi * cl, (bi + 1) * cl)
        lw_b = lw[rs]
        l_hi, l_mid, l_lo = _split3(lw_b)
        cum = (jnp.dot(lmat, l_hi, preferred_element_type=F32)
               + jnp.dot(lmat, l_mid, preferred_element_type=F32)
               + jnp.dot(lmat, l_lo, preferred_element_type=F32))
        tot = cum[cl - 1:cl, :]
        e_neg = jnp.exp(-cum)
        e_rem = jnp.exp(tot - cum)
        at = (a[rs] * jnp.exp(cum - lw_b)).astype(BF16)
        rt = (r[rs] * jnp.exp(cum)).astype(BF16)
        bt = (b[rs] * e_neg).astype(BF16)
        kt = (k[rs] * e_neg).astype(BF16)
        bh = (b[rs] * e_rem).astype(BF16)
        kh = (k[rs] * e_rem).astype(BF16)
        e_tot = jnp.exp(tot)
        vb = v[rs].astype(BF16)
        for h in range(RW_HEADS):
            hs = slice(h * RW_HD, (h + 1) * RW_HD)
            lhs_l.append(jnp.concatenate([at[:, hs], rt[:, hs]], axis=0))
            rhs_l.append(jnp.concatenate([kt[:, hs], eye_bf, bt[:, hs]], axis=0))
            vh_l.append(vb[:, hs])
            kb_l.append(jnp.concatenate([kh[:, hs], bh[:, hs]], axis=0))
            etot_l.append(jnp.sum(eye * e_tot[:, hs], axis=-1, keepdims=True))
            hcat_l.append(h_scr[bi, h])

    nitem = bs * RW_HEADS
    items = range(nitem)
    aa_l = [jnp.where(keep, _dot_nt(lhs_l[i], rhs_l[i]), 0.0).astype(BF16) for i in items]
    pw_l = [aa_l[i][:cl, 2 * cl:] for i in items]
    tinv_l = [eye_bf + pw_l[i] for i in items]
    for _ in range(int(math.log2(cl)) - 1):
        pw_l = [jnp.dot(pw_l[i], pw_l[i], preferred_element_type=F32).astype(BF16) for i in items]
        tinv_l = [jnp.dot(tinv_l[i], eye_bf + pw_l[i], preferred_element_type=F32).astype(BF16) for i in items]
    vh_cat = [jnp.concatenate([vh_l[i], hcat_l[i].astype(BF16)], axis=0) for i in items]
    x1_l = [jnp.dot(aa_l[i][:cl, :2 * cl], vh_cat[i], preferred_element_type=F32).astype(BF16) for i in items]
    u_l = [jnp.dot(tinv_l[i], x1_l[i], preferred_element_type=F32).astype(BF16) for i in items]
    o_l = [jnp.dot(aa_l[i][cl:, :], jnp.concatenate([vh_cat[i], u_l[i]], axis=0),
                   preferred_element_type=F32) for i in items]
    hn_l = [hcat_l[i] * etot_l[i]
            + lax.dot_general(kb_l[i], jnp.concatenate([vh_l[i], u_l[i]], axis=0), (((0,), (0,)), ((), ())),
                              preferred_element_type=F32) for i in items]

    for bi in range(bs):
        for h in range(RW_HEADS):
            i = bi * RW_HEADS + h
            o_scr[bi * cl:(bi + 1) * cl, h * RW_HD:(h + 1) * RW_HD] = o_l[i]
            h_scr[bi, h] = hn_l[i]
        prev_scr[bi:bi + 1, :] = ps[bi][cl - 1:cl, :]

    y = _rw_post(o_scr[...], r, k, v, g, rk_ref[...], lnw_ref[...], lnb_ref[...], ones_bd)
    for bi in range(bs):
        y_ref[:, bi * RW_WIDTH:(bi + 1) * RW_WIDTH] = y[bi * cl:(bi + 1) * cl].astype(y_ref.dtype)
    if side is not None:
        side()

    @pl.when(c == nc - 1)
    def _():
        hfin_ref[...] = h_scr[...]
        for bi in range(bs):
            shout_ref[bi] = ps[bi][cl - 1:cl, :]


def _rw_param_args(mu, w0, w2, a0, a2, g2, k_k, k_a, r_k, ln_w, ln_b):
    row = lambda z: z.reshape(1, -1).astype(F32)
    return (row(mu), row(w0), w2.astype(BF16), row(a0), a2.astype(BF16), g2.astype(BF16),
            row(k_k), row(k_a), row(r_k), row(ln_w), row(ln_b))


def rwkv_prompt(p_tm, shift, s0, params, *, bs=4, side=None):
    c_len = RW_HD
    t_len, nb, _ = p_tm.shape
    assert t_len % c_len == 0 and nb % bs == 0
    prm = _rw_param_args(*params)
    const = lambda a: pl.BlockSpec(a.shape, lambda b, c: (0,) * a.ndim)
    st_spec = pl.BlockSpec((bs, RW_HEADS, RW_HD, RW_HD), lambda b, c: (b, 0, 0, 0))
    sh_spec = pl.BlockSpec((bs, 1, RW_PROJ), lambda b, c: (b, 0, 0))
    in_specs = [pl.BlockSpec((c_len, bs * RW_PROJ), lambda b, c: (c, b)), sh_spec, st_spec] + [const(a) for a in prm]
    out_shape = (jax.ShapeDtypeStruct((t_len, nb * RW_WIDTH), BF16),
                 jax.ShapeDtypeStruct((nb, RW_HEADS, RW_HD, RW_HD), F32),
                 jax.ShapeDtypeStruct((nb, 1, RW_PROJ), F32))
    out_specs = (pl.BlockSpec((c_len, bs * RW_WIDTH), lambda b, c: (c, b)), st_spec, sh_spec)
    scratch = [pltpu.VMEM((bs, RW_PROJ), F32), pltpu.VMEM((bs, RW_HEADS, RW_HD, RW_HD), F32),
               pltpu.VMEM((bs * c_len, RW_WIDTH), F32)]
    h0 = jnp.swapaxes(s0, -1, -2)
    args = [p_tm.reshape(t_len, nb * RW_PROJ), shift.reshape(nb, 1, RW_PROJ), h0, *prm]
    kern = functools.partial(_rw_chunk_kernel, c_len=c_len, bs=bs)
    grid = (nb // bs, t_len // c_len)
    out_shape, out_specs = list(out_shape), list(out_specs)
    if side is not None:
        assert side.steps == grid[0] * grid[1]
        kern = _with_side(kern, len(args), 3, side)
        args += list(side.args)
        in_specs += side.in_specs(grid[1])
        out_shape += list(side.out_shape)
        out_specs += side.out_specs(grid[1])
    outs = pl.pallas_call(
        kern, grid=grid, in_specs=in_specs, out_specs=out_specs, out_shape=out_shape,
        scratch_shapes=scratch, compiler_params=_cparams("parallel", "arbitrary"), name="rwkv_prompt")(*args)
    y, h_fin, sh = outs[:3]
    res = (y.reshape(t_len, nb, RW_WIDTH), jnp.swapaxes(h_fin, -1, -2), sh.reshape(nb, RW_PROJ))
    return res if side is None else (res, outs[3:])


def _rw_step_prep_kernel(p_ref, shift_ref, mu_ref, w0_ref, w2_ref, a0_ref, a2_ref, g2_ref, kk_ref, ka_ref,
                         r_ref, k_ref, v_ref, g_ref, rt_ref, wt_ref, kt_ref, at_ref, bt_ref, vt_ref):
    prm = (mu_ref[...], w0_ref[...], w2_ref[...], a0_ref[...], a2_ref[...], g2_ref[...],
           kk_ref[...], ka_ref[...])
    r, lw, k, v, a, b, g = _rw_prep(p_ref[...], shift_ref[...], prm, _head_ones())
    r_ref[...] = r
    k_ref[...] = k
    v_ref[...] = v
    g_ref[...] = g
    rt_ref[...] = r.T
    wt_ref[...] = jnp.exp(lw).T
    kt_ref[...] = k.T
    at_ref[...] = a.T
    bt_ref[...] = b.T
    vt_ref[...] = v.T


def _rw_step_core_kernel(s_ref, r_ref, w_ref, k_ref, a_ref, b_ref, v_ref, s_out_ref, o_ref):
    r, w, k, a, b = r_ref[0], w_ref[0], k_ref[0], a_ref[0], b_ref[0]
    for j in range(s_ref.shape[1]):
        s = s_ref[0, j]
        sa = jnp.sum(s * a, axis=0, keepdims=True)
        s_new = s * w + sa * b + v_ref[0, j:j + 1, :] * k
        s_out_ref[0, j] = s_new
        o_ref[0, j:j + 1, :] = jnp.sum(s_new * r, axis=0, keepdims=True)


def _rw_step_post_kernel(ot_ref, r_ref, k_ref, v_ref, g_ref, rk_ref, lnw_ref, lnb_ref, y_ref):
    y_ref[...] = _rw_post(ot_ref[...].T, r_ref[...], k_ref[...], v_ref[...], g_ref[...],
                          rk_ref[...], lnw_ref[...], lnb_ref[...], _head_ones()).astype(y_ref.dtype)


def rwkv_step(p, shift, s0, params, *, vb=32):
    n = p.shape[0]
    prm = _rw_param_args(*params)
    vec = jax.ShapeDtypeStruct((n, RW_WIDTH), F32)
    vec_t = jax.ShapeDtypeStruct((RW_WIDTH, n), F32)
    r, k, v, g, rt, wt, kt, at, bt, vt = pl.pallas_call(
        _rw_step_prep_kernel, out_shape=(vec,) * 4 + (vec_t,) * 6, name="rwkv_step_prep")(p, shift, *prm[:8])
    heads = lambda z: z.reshape(RW_HEADS, RW_HD, n)
    k_spec = pl.BlockSpec((1, RW_HD, n), lambda h, j: (h, 0, 0))
    v_spec = pl.BlockSpec((1, vb, n), lambda h, j: (h, j, 0))
    st_spec = pl.BlockSpec((1, vb, RW_HD, n), lambda h, j: (h, j, 0, 0))
    st = jnp.transpose(s0, (1, 2, 3, 0))
    s_new, ot = pl.pallas_call(
        _rw_step_core_kernel, grid=(RW_HEADS, RW_HD // vb),
        in_specs=[st_spec] + [k_spec] * 5 + [v_spec], out_specs=(st_spec, v_spec),
        out_shape=(jax.ShapeDtypeStruct(st.shape, F32), jax.ShapeDtypeStruct((RW_HEADS, RW_HD, n), F32)),
        compiler_params=_cparams("parallel", "parallel"), name="rwkv_step_core")(
            st, heads(rt), heads(wt), heads(kt), heads(at), heads(bt), heads(vt))
    y = pl.pallas_call(
        _rw_step_post_kernel, out_shape=jax.ShapeDtypeStruct((n, RW_WIDTH), BF16), name="rwkv_step_post")(
            ot.reshape(RW_WIDTH, n), r, k, v, g, *prm[8:])
    return y, jnp.transpose(s_new, (3, 0, 1, 2))


RET_LOG_G = tuple(math.log(1.0 - 2.0 ** (-5.0 - h)) for h in range(RET_HEADS))


def _rope_tables(pos, half):
    j = lax.broadcasted_iota(jnp.int32, (1, half), 1).astype(F32)
    inv = jnp.exp(j * (-math.log(ROPE_BASE) / half))
    ang = pos * inv
    return jnp.cos(ang), jnp.sin(ang)


def _rope(x, cos, sin):
    half = RET_DK // 2
    outs = []
    for h in range(RET_HEADS):
        x1 = x[:, h * RET_DK:h * RET_DK + half]
        x2 = x[:, h * RET_DK + half:(h + 1) * RET_DK]
        outs += [x1 * cos - x2 * sin, x1 * sin + x2 * cos]
    return jnp.concatenate(outs, axis=-1)


def _ret_norm_gate(o, g):
    o = o * lax.rsqrt(jnp.mean(o * o, axis=-1, keepdims=True) + NORM_EPS)
    return jax.nn.silu(g) * o


def _ret_tables_kernel(cos_ref, sin_ref, dmask_ref, qdec_ref, kdec_ref, *, c_len):
    t_len = cos_ref.shape[0]
    pos = lax.broadcasted_iota(jnp.int32, (t_len, 1), 0).astype(F32)
    cos, sin = _rope_tables(pos, RET_DK // 2)
    cos_ref[...] = cos
    sin_ref[...] = sin
    ti = lax.broadcasted_iota(jnp.int32, (c_len, 1), 0).astype(F32)
    ii = lax.broadcasted_iota(jnp.int32, (c_len, c_len), 0)
    jj = lax.broadcasted_iota(jnp.int32, (c_len, c_len), 1)
    diff = (ii - jj).astype(F32)
    for h in range(RET_HEADS):
        lg = RET_LOG_G[h]
        dmask_ref[h] = jnp.where(diff >= 0, jnp.exp(lg * jnp.maximum(diff, 0.0)), 0.0)
        qdec_ref[h] = jnp.exp(lg * (ti + 1.0))
        kdec_ref[h] = jnp.exp(lg * (c_len - 1.0 - ti))


def _ret_layer_kernel(x_ref, gain_ref, win_ref, wout_ref, cos_ref, sin_ref, dmask_ref, qdec_ref, kdec_ref,
                      o_ref, sfin_ref, s_scr, y_scr, *, c_len):
    c = pl.program_id(1)

    @pl.when(c == 0)
    def _():
        s_scr[...] = jnp.zeros_like(s_scr)

    x = x_ref[...]
    hb = _rms(x, gain_ref[...]).astype(BF16)
    proj = lambda lo, width: jnp.dot(hb, win_ref[:, lo:lo + width], preferred_element_type=F32)
    cos, sin = cos_ref[...], sin_ref[...]
    q = _rope(proj(0, NQ), cos, sin)
    k = _rope(proj(NQ, NQ), cos, sin) * (RET_DK ** -0.5)
    for h in range(RET_HEADS):
        c_dec = math.exp(RET_LOG_G[h] * c_len)
        qh = q[:, h * RET_DK:(h + 1) * RET_DK]
        kh = k[:, h * RET_DK:(h + 1) * RET_DK]
        vh = proj(2 * NQ + h * RET_DV, RET_DV).astype(BF16)
        s_h = s_scr[h]
        sc = _dot_nt(qh, kh) * dmask_ref[h]
        o = _bdot(sc, vh) + _bdot(qh * qdec_ref[h], s_h)
        s_scr[h] = s_h * c_dec + _dot_tn(kh * kdec_ref[h], vh)
        gh = proj(2 * NQ + NV + h * RET_DV, RET_DV)
        y_scr[:, h * RET_DV:(h + 1) * RET_DV] = _ret_norm_gate(o, gh).astype(BF16)
    o_ref[...] = x + jnp.dot(y_scr[...], wout_ref[...], preferred_element_type=F32)

    @pl.when(c == pl.num_programs(1) - 1)
    def _():
        sfin_ref[0] = s_scr[...]


def retention_layer_prompt(x, gain, w_in, w_out, *, nb, c_len=RET_CHUNK):
    n = x.shape[0]
    t_len = n // nb
    nc = t_len // c_len
    half = RET_DK // 2
    tabs = pl.pallas_call(
        functools.partial(_ret_tables_kernel, c_len=c_len),
        out_shape=(jax.ShapeDtypeStruct((t_len, half), F32), jax.ShapeDtypeStruct((t_len, half), F32),
                   jax.ShapeDtypeStruct((RET_HEADS, c_len, c_len), F32),
                   jax.ShapeDtypeStruct((RET_HEADS, c_len, 1), F32),
                   jax.ShapeDtypeStruct((RET_HEADS, c_len, 1), F32)),
        name="retention_tables")()
    row = pl.BlockSpec((c_len, D_MODEL), lambda b, c: (b * nc + c, 0))
    pos_spec = pl.BlockSpec((c_len, half), lambda b, c: (c, 0))
    const = lambda a: pl.BlockSpec(a.shape, lambda b, c: (0,) * a.ndim)
    st_spec = pl.BlockSpec((1, RET_HEADS, RET_DK, RET_DV), lambda b, c: (b, 0, 0, 0))
    gain = gain.reshape(1, D_MODEL)
    return pl.pallas_call(
        functools.partial(_ret_layer_kernel, c_len=c_len), grid=(nb, nc),
        in_specs=[row, const(gain), const(w_in), const(w_out), pos_spec, pos_spec] + [const(a) for a in tabs[2:]],
        out_specs=(row, st_spec),
        out_shape=(jax.ShapeDtypeStruct((n, D_MODEL), F32),
                   jax.ShapeDtypeStruct((nb, RET_HEADS, RET_DK, RET_DV), F32)),
        scratch_shapes=[pltpu.VMEM((RET_HEADS, RET_DK, RET_DV), F32), pltpu.VMEM((c_len, NV), BF16)],
        compiler_params=_cparams("parallel", "arbitrary"), name="retention_layer")(
            x, gain, w_in, w_out, *tabs)


def _ret_step_rope_kernel(q_ref, k_ref, qo_ref, ko_ref, *, pos0):
    pos = jnp.full((q_ref.shape[0], 1), pos0, F32)
    cos, sin = _rope_tables(pos, RET_DK // 2)
    qo_ref[...] = _rope(q_ref[...].astype(F32), cos, sin).T
    ko_ref[...] = (_rope(k_ref[...].astype(F32), cos, sin) * (RET_DK ** -0.5)).T


def _ret_step_core_kernel(s_ref, qt_ref, kt_ref, v_ref, g_ref, s_out_ref, y_ref):
    tb = s_ref.shape[0]
    step = pl.program_id(0) * pl.num_programs(1) + pl.program_id(1)
    lane = lax.broadcasted_iota(jnp.int32, qt_ref.shape, 1)
    for i in range(tb):
        mine = lane == step * tb + i
        q_col = jnp.sum(jnp.where(mine, qt_ref[...], 0.0), axis=-1, keepdims=True)
        k_col = jnp.sum(jnp.where(mine, kt_ref[...], 0.0), axis=-1, keepdims=True)
        for h in range(RET_HEADS):
            gam = math.exp(RET_LOG_G[h])
            s_h = s_ref[i, h]
            qc = q_col[h * RET_DK:(h + 1) * RET_DK]
            kc = k_col[h * RET_DK:(h + 1) * RET_DK]
            vs = slice(h * RET_DV, (h + 1) * RET_DV)
            vr = v_ref[i, :, vs].astype(F32)
            qk = jnp.sum(qc * kc, axis=0, keepdims=True)
            o = qk * vr + jnp.sum((qc * gam) * s_h, axis=0, keepdims=True)
            s_out_ref[i, h] = s_h * gam + kc * vr
            y_ref[i, :, vs] = _ret_norm_gate(o, g_ref[i, :, vs].astype(F32)).astype(y_ref.dtype)


def retention_step_job(q, k, v, g, s0, *, pos0, tb):
    n = q.shape[0]
    vec_t = jax.ShapeDtypeStruct((NQ, n), F32)
    qt, kt = pl.pallas_call(functools.partial(_ret_step_rope_kernel, pos0=pos0), out_shape=(vec_t, vec_t),
                            name="retention_step_rope")(q, k)
    st = lambda inner: pl.BlockSpec((tb, RET_HEADS, RET_DK, RET_DV), lambda i, j: (i * inner + j, 0, 0, 0))
    rw = lambda inner: pl.BlockSpec((tb, 1, NV), lambda i, j: (i * inner + j, 0, 0))
    whole = lambda inner: pl.BlockSpec((NQ, n), lambda i, j: (0, 0))
    return SideJob(
        body=_ret_step_core_kernel,
        args=(s0, qt, kt, v.reshape(n, 1, NV), g.reshape(n, 1, NV)),
        in_specs=lambda inner: [st(inner), whole(inner), whole(inner), rw(inner), rw(inner)],
        out_shape=(jax.ShapeDtypeStruct(s0.shape, F32), jax.ShapeDtypeStruct((n, 1, NV), BF16)),
        out_specs=lambda inner: [st(inner), rw(inner)],
        steps=n // tb)


def _xattn_prompt_kernel(x_ref, g_ref, wq_ref, mk_ref, mv_ref, wo_ref, *rest, pre):
    if pre:
        ya_ref, wa_ref, yb_ref, wb_ref, o_ref, att_scr = rest
        x = (x_ref[...] + jnp.dot(ya_ref[...], wa_ref[...], preferred_element_type=F32)
             + jnp.dot(yb_ref[...], wb_ref[...], preferred_element_type=F32))
    else:
        o_ref, att_scr = rest
        x = x_ref[...]
    q = jnp.dot(_rms(x, g_ref[...]).astype(BF16), wq_ref[...], preferred_element_type=F32)
    for h in range(MEM_HEADS):
        hs = slice(h * MEM_HD, (h + 1) * MEM_HD)
        s = _dot_nt(q[:, hs], mk_ref[0, :, hs]) * (MEM_HD ** -0.5)
        s = s - jnp.max(s, axis=-1, keepdims=True)
        e = jnp.exp(s)
        p = e / jnp.sum(e, axis=-1, keepdims=True)
        att_scr[:, hs] = _bdot(p, mv_ref[0, :, hs])
    o_ref[...] = x + jnp.dot(att_scr[...].astype(BF16), wo_ref[...], preferred_element_type=F32)


def xattn_prompt(x, gain, w_q, mem_k, mem_v, w_o, layer, *, nb, tm=1024, pre=None):
    n = x.shape[0]
    tiles_per_b = n // nb // tm
    mem_k = mem_k.reshape(-1, N_MEM, D_MODEL)
    mem_v = mem_v.reshape(-1, N_MEM, D_MODEL)
    row = pl.BlockSpec((tm, D_MODEL), lambda i: (i, 0))
    wspec = pl.BlockSpec((D_MODEL, D_MODEL), lambda i: (0, 0))
    mspec = pl.BlockSpec((1, N_MEM, D_MODEL), lambda i: (layer * nb + i // tiles_per_b, 0, 0))
    args = [x, gain.reshape(1, D_MODEL), w_q, mem_k, mem_v, w_o]
    in_specs = [row, pl.BlockSpec((1, D_MODEL), lambda i: (0, 0)), wspec, mspec, mspec, wspec]
    if pre is not None:
        for y, wy in (pre[:2], pre[2:]):
            t_len, _, kw = y.shape
            args += [y.reshape(t_len, nb * kw), wy]
            in_specs += [_row_spec(tm, kw, (nb, tiles_per_b)), pl.BlockSpec(wy.shape, lambda i: (0, 0))]
    return pl.pallas_call(
        functools.partial(_xattn_prompt_kernel, pre=pre is not None), grid=(n // tm,),
        in_specs=in_specs, out_specs=row, out_shape=jax.ShapeDtypeStruct((n, D_MODEL), F32),
        scratch_shapes=[pltpu.VMEM((tm, D_MODEL), F32)],
        compiler_params=_cparams("parallel"), name="xattn_prompt")(*args)


def _xattn_step_kernel(q_ref, mk_ref, mv_ref, o_ref, *, tb):
    half = N_MEM // 2
    both = lambda z: jnp.concatenate([z, z], axis=1)
    fold = lambda z, op: op(z[:, :MEM_HEADS], z[:, MEM_HEADS:])
    for i in range(tb):
        k8 = jnp.concatenate([mk_ref[0, i, :half], mk_ref[0, i, half:]], axis=1)
        v8 = jnp.concatenate([mv_ref[0, i, :half], mv_ref[0, i, half:]], axis=1)
        q8 = jnp.concatenate([q_ref[i], q_ref[i]], axis=0)
        s = jnp.sum(k8 * q8[None], axis=-1, keepdims=True) * (MEM_HD ** -0.5)
        smax = both(fold(jnp.max(s, axis=0, keepdims=True), jnp.maximum))
        e = jnp.exp(s - smax)
        den = both(fold(jnp.sum(e, axis=0, keepdims=True), jnp.add))
        o8 = jnp.sum((e / den) * v8, axis=0)
        o_ref[i] = o8[:MEM_HEADS] + o8[MEM_HEADS:]


def xattn_step_job(q, cache_k, cache_v, layer, *, tb):
    n = q.shape[0]

    def specs(inner):
        qspec = pl.BlockSpec((tb, MEM_HEADS, MEM_HD), lambda i, j: (i * inner + j, 0, 0))
        cspec = pl.BlockSpec((1, tb, N_MEM, MEM_HEADS, MEM_HD), lambda i, j: (layer, i * inner + j, 0, 0, 0))
        return qspec, cspec

    return SideJob(
        body=functools.partial(_xattn_step_kernel, tb=tb),
        args=(q.reshape(n, MEM_HEADS, MEM_HD), cache_k, cache_v),
        in_specs=lambda inner: [specs(inner)[0], specs(inner)[1], specs(inner)[1]],
        out_shape=(jax.ShapeDtypeStruct((n, MEM_HEADS, MEM_HD), F32),),
        out_specs=lambda inner: [specs(inner)[0]],
        steps=n // tb)


def run_job(job, name):
    return pl.pallas_call(
        job.body, grid=(job.steps, 1), in_specs=job.in_specs(1), out_specs=job.out_specs(1),
        out_shape=list(job.out_shape), compiler_params=_cparams("parallel", "arbitrary"), name=name)(*job.args)


ROUTER_LANES = 128
NEG_BIG = -1e30


def _moe_gates(logits):
    lane = lax.broadcasted_iota(jnp.int32, logits.shape, 1)
    first = lambda mask: jnp.min(jnp.where(mask, lane, ROUTER_LANES), axis=-1, keepdims=True)
    is_c = lane < MOE_GROUPS
    lc = jnp.where(is_c, logits, NEG_BIG)
    mc = jnp.max(lc, axis=-1, keepdims=True)
    g_idx = first(lc == mc)
    p_g = 1.0 / jnp.sum(jnp.where(is_c, jnp.exp(lc - mc), 0.0), axis=-1, keepdims=True)
    fl = lane - MOE_GROUPS
    in_g = (fl >= 0) & (fl < MOE_EXPERTS) & ((fl // MOE_PER_GROUP) == g_idx)
    lf = jnp.where(in_g, logits, NEG_BIG)
    m1 = jnp.max(lf, axis=-1, keepdims=True)
    i1 = first(lf == m1)
    lf2 = jnp.where(lane == i1, NEG_BIG, lf)
    m2 = jnp.max(lf2, axis=-1, keepdims=True)
    i2 = first(lf2 == m2)
    e2 = jnp.exp(m2 - m1)
    w_top = 1.0 / (1.0 + e2)
    gate = p_g * (jnp.where(lane == i1, w_top, 0.0) + jnp.where(lane == i2, e2 * w_top, 0.0))
    return gate, g_idx


MOE_CAP = 320
MOE_EPS = 2
MOE_STEPS = MOE_EXPERTS // MOE_EPS

SideJob = collections.namedtuple("SideJob", "body args in_specs out_shape out_specs steps")


def merge_jobs(a, b):
    assert a.steps == b.steps
    na_in, na_out, n_in = len(a.args), len(a.out_shape), len(a.args) + len(b.args)

    def body(*refs):
        a.body(*refs[:na_in], *refs[n_in:n_in + na_out])
        b.body(*refs[na_in:n_in], *refs[n_in + na_out:])

    return SideJob(body, tuple(a.args) + tuple(b.args), lambda inner: a.in_specs(inner) + b.in_specs(inner),
                   tuple(a.out_shape) + tuple(b.out_shape),
                   lambda inner: a.out_specs(inner) + b.out_specs(inner), a.steps)


def cast_job(arrays, steps):
    views = tuple(a.reshape(steps, -1, a.shape[-1]) for a in arrays)

    def body(*refs):
        for src, dst in zip(refs[:len(views)], refs[len(views):]):
            dst[...] = src[...].astype(BF16)

    specs = lambda inner: [pl.BlockSpec((1,) + v.shape[1:], lambda i, j: (i * inner + j, 0, 0)) for v in views]
    return SideJob(body, views, specs, tuple(jax.ShapeDtypeStruct(v.shape, BF16) for v in views), specs, steps)


def _with_side(main_kernel, n_in, n_out, side):
    ns_in, ns_out = len(side.args), len(side.out_shape)

    def kern(*refs):
        m_in = refs[:n_in]
        s_in = refs[n_in:n_in + ns_in]
        m_out = refs[n_in + ns_in:n_in + ns_in + n_out]
        s_out = refs[n_in + ns_in + n_out:n_in + ns_in + n_out + ns_out]
        scratch = refs[n_in + ns_in + n_out + ns_out:]
        main_kernel(*m_in, *m_out, *scratch, side=lambda: side.body(*s_in, *s_out))

    return kern


def _router_logits(h, wr_ref, br_ref):
    h_hi = h.astype(BF16)
    h_lo = (h - h_hi.astype(F32)).astype(BF16)
    rows = h.shape[0]
    res = jnp.dot(jnp.concatenate([h_hi, h_lo], axis=0), wr_ref[...], preferred_element_type=F32)
    acc = (res[:rows, :ROUTER_LANES] + res[:rows, ROUTER_LANES:]) + (res[rows:, :ROUTER_LANES] + res[rows:, ROUTER_LANES:])
    return acc + br_ref[...]


def _experts_ffn(hb, gate, e0, w1_ref, w3_ref, w2_ref):
    lane = lax.broadcasted_iota(jnp.int32, gate.shape, 1)
    acc = None
    for e in range(MOE_EPS):
        a1 = jnp.dot(hb, w1_ref[e].astype(BF16), preferred_element_type=F32)
        a3 = jnp.dot(hb, w3_ref[e].astype(BF16), preferred_element_type=F32)
        ge = jnp.sum(jnp.where(lane == MOE_GROUPS + e0 + e, gate, 0.0), axis=-1, keepdims=True)
        hid = (jax.nn.silu(a1) * a3 * ge).astype(BF16)
        part = jnp.dot(hid, w2_ref[e].astype(BF16), preferred_element_type=F32)
        acc = part if acc is None else acc + part
    return acc


def _moe_kernel(x_ref, g_ref, wr_ref, br_ref, el_ref, w1_ref, w3_ref, w2_ref, *rest, final_norm, cap, side=None):
    if final_norm:
        fin_ref, o_ref = rest[:2]
        rest = rest[2:]
    else:
        o_ref = rest[0]
        rest = rest[1:]
    h_scr, oh_scr, rk_scr, ohr_scr, rkr_scr, hg_scr, gg_scr, yg_scr, cnt_smem = rest
    tm = x_ref.shape[0]
    step = pl.program_id(1)
    grp = step // (MOE_PER_GROUP // MOE_EPS)
    first_half = step % (MOE_PER_GROUP // MOE_EPS) == 0
    last_half = step % (MOE_PER_GROUP // MOE_EPS) == MOE_PER_GROUP // MOE_EPS - 1

    @pl.when(step == 0)
    def _():
        x = x_ref[...]
        h = _rms(x, g_ref[...])
        gate, g_idx = _moe_gates(_router_logits(h, wr_ref, br_ref))
        g_hi = gate.astype(BF16)
        h_scr[:, :D_MODEL] = h.astype(BF16)
        h_scr[:, D_MODEL:D_MODEL + ROUTER_LANES] = g_hi
        h_scr[:, D_MODEL + ROUTER_LANES:] = (gate - g_hi.astype(F32)).astype(BF16)
        o_ref[...] = x
        lane = lax.broadcasted_iota(jnp.int32, gate.shape, 1)
        onehot = jnp.where(lane == g_idx, 1.0, 0.0)
        rank = jnp.dot(el_ref[...], onehot.astype(BF16), preferred_element_type=F32)
        oh_scr[...] = onehot
        rk_scr[...] = rank
        ohr_scr[...] = onehot.T[:8]
        rkr_scr[...] = rank.T[:8]
        cnt = jnp.sum(onehot, axis=0, keepdims=True)
        for gi in range(MOE_GROUPS):
            cnt_smem[gi] = cnt[0, gi].astype(jnp.int32)

    lane = lax.broadcasted_iota(jnp.int32, (tm, ROUTER_LANES), 1)

    def gather_mat(base):
        slot = jnp.where(ohr_scr[pl.ds(grp, 1), :] > 0.5, rkr_scr[pl.ds(grp, 1), :] - base, -1.0)
        c = lax.broadcasted_iota(jnp.int32, (cap, tm), 0).astype(F32)
        return jnp.where(c == slot, 1.0, 0.0).astype(BF16)

    def scatter_mat(base):
        member = jnp.sum(jnp.where(lane == grp, oh_scr[...], 0.0), axis=-1, keepdims=True)
        rank = jnp.sum(jnp.where(lane == grp, rk_scr[...], 0.0), axis=-1, keepdims=True)
        slot = jnp.where(member > 0.5, rank - base, -1.0)
        c = lax.broadcasted_iota(jnp.int32, (tm, cap), 1).astype(F32)
        return jnp.where(c == slot, 1.0, 0.0).astype(BF16)

    def gather(base):
        got = jnp.dot(gather_mat(base), h_scr[...], preferred_element_type=F32)
        gg = got[:, D_MODEL:D_MODEL + ROUTER_LANES] + got[:, D_MODEL + ROUTER_LANES:]
        return got[:, :D_MODEL].astype(BF16), gg

    @pl.when(first_half)
    def _():
        hg, gg = gather(0.0)
        hg_scr[...] = hg
        gg_scr[...] = gg
        yg_scr[...] = jnp.zeros_like(yg_scr)

    yg_scr[...] += _experts_ffn(hg_scr[...], gg_scr[...], step * MOE_EPS, w1_ref, w3_ref, w2_ref)
    if side is not None:
        side()

    @pl.when(last_half)
    def _():
        o_ref[...] += jnp.dot(scatter_mat(0.0), yg_scr[...].astype(BF16), preferred_element_type=F32)

    def extra_round(r, carry):
        base = (r * cap).astype(F32)
        hg, gg = gather(base)
        y = _experts_ffn(hg, gg, step * MOE_EPS, w1_ref, w3_ref, w2_ref)
        o_ref[...] += jnp.dot(scatter_mat(base), y.astype(BF16), preferred_element_type=F32)
        return carry

    lax.fori_loop(1, (cnt_smem[grp] + cap - 1) // cap, extra_round, 0)

    if final_norm:
        @pl.when(step == MOE_STEPS - 1)
        def _():
            o_ref[...] = _rms(o_ref[...], fin_ref[...])


def moe_dense(x, gain, w_r, b_r, w1, w3, w2, layer, *, tm=512, cap=MOE_CAP, final_gain=None, side=None):
    n = x.shape[0]
    tm = min(tm, n)
    cap = min(cap, tm)
    gain = gain.reshape(1, D_MODEL)
    row = pl.BlockSpec((tm, D_MODEL), lambda i, s: (i, 0))
    const2 = lambda a: pl.BlockSpec(a.shape, lambda i, s: (0,) * a.ndim)
    soff = layer * MOE_STEPS
    wspec = pl.BlockSpec((MOE_EPS, D_MODEL, MOE_HIDDEN), lambda i, s: (soff + s, 0, 0))
    earlier = jnp.tril(jnp.ones((tm, tm), BF16), -1)
    args = [x, gain, w_r, b_r, earlier, w1, w3, w2]
    in_specs = [row, const2(gain), const2(w_r), const2(b_r), const2(earlier), wspec, wspec,
                pl.BlockSpec((MOE_EPS, MOE_HIDDEN, D_MODEL), lambda i, s: (soff + s, 0, 0))]
    if final_gain is not None:
        args.append(final_gain.reshape(1, D_MODEL))
        in_specs.append(const2(args[-1]))
    kern = functools.partial(_moe_kernel, final_norm=final_gain is not None, cap=cap)
    out_shape = [jax.ShapeDtypeStruct((n, D_MODEL), F32)]
    out_specs = [row]
    grid = (n // tm, MOE_STEPS)
    if side is not None:
        assert side.steps == grid[0] * grid[1]
        kern = _with_side(kern, len(args), 1, side)
        args += list(side.args)
        in_specs += side.in_specs(MOE_STEPS)
        out_shape += list(side.out_shape)
        out_specs += side.out_specs(MOE_STEPS)
    outs = pl.pallas_call(
        kern, grid=grid, in_specs=in_specs, out_specs=out_specs, out_shape=out_shape,
        scratch_shapes=[pltpu.VMEM((tm, D_MODEL + 2 * ROUTER_LANES), BF16),
                        pltpu.VMEM((tm, ROUTER_LANES), F32),
                        pltpu.VMEM((tm, ROUTER_LANES), F32),
                        pltpu.VMEM((8, tm), F32),
                        pltpu.VMEM((8, tm), F32),
                        pltpu.VMEM((cap, D_MODEL), BF16),
                        pltpu.VMEM((cap, ROUTER_LANES), F32),
                        pltpu.VMEM((cap, D_MODEL), F32),
                        pltpu.SMEM((MOE_GROUPS,), jnp.int32)],
        compiler_params=_cparams("parallel", "arbitrary"), name="moe")(*args)
    return outs[0] if side is None else (outs[0], outs[1:])


def _group_weights(w1, w3, w2):
    ne = w1.shape[0] * MOE_EXPERTS
    return (w1.reshape(ne, D_MODEL, MOE_HIDDEN), w3.reshape(ne, D_MODEL, MOE_HIDDEN),
            w2.reshape(ne, MOE_HIDDEN, D_MODEL))


def _router_params(w_rc, b_rc, w_rf, b_rf):
    pad = ROUTER_LANES - MOE_GROUPS - MOE_EXPERTS
    w_r = jnp.concatenate([w_rc, w_rf, jnp.zeros((D_MODEL, pad), F32)], axis=1).astype(F32)
    b_r = jnp.concatenate([b_rc, b_rf, jnp.zeros((pad,), F32)]).reshape(1, ROUTER_LANES).astype(F32)
    w_hi = w_r.astype(BF16)
    w_lo = (w_r - w_hi.astype(F32)).astype(BF16)
    return jnp.concatenate([w_hi, w_lo], axis=1), b_r


def _mem_kv_kernel(x_ref, g_ref, w_ref, kf_ref, vf_ref, kh_ref, vh_ref):
    h = _rms(x_ref[...], g_ref[0]).astype(BF16)
    for col, f_ref, h_ref in ((0, kf_ref, kh_ref), (D_MODEL, vf_ref, vh_ref)):
        acc = jnp.dot(h, w_ref[0, :, col:col + D_MODEL], preferred_element_type=F32)
        f_ref[0] = acc.astype(f_ref.dtype)
        for hd in range(MEM_HEADS):
            h_ref[0, :, hd, :] = acc[:, hd * MEM_HD:(hd + 1) * MEM_HD]


def mem_kv(mem, gains, w_kv, *, tm=512):
    rows = mem.shape[0]
    nl = w_kv.shape[0]
    flat = jax.ShapeDtypeStruct((nl, rows, D_MODEL), BF16)
    head = jax.ShapeDtypeStruct((nl, rows, MEM_HEADS, MEM_HD), F32)
    fspec = pl.BlockSpec((1, tm, D_MODEL), lambda l, i: (l, i, 0))
    hspec = pl.BlockSpec((1, tm, MEM_HEADS, MEM_HD), lambda l, i: (l, i, 0, 0))
    return pl.pallas_call(
        _mem_kv_kernel, grid=(nl, rows // tm),
        in_specs=[pl.BlockSpec((tm, D_MODEL), lambda l, i: (i, 0)),
                  pl.BlockSpec((1, 1, D_MODEL), lambda l, i: (l, 0, 0)),
                  pl.BlockSpec((1, D_MODEL, 2 * D_MODEL), lambda l, i: (l, 0, 0))],
        out_specs=(fspec, fspec, hspec, hspec), out_shape=(flat, flat, head, head),
        compiler_params=_cparams("parallel", "parallel"), name="mem_kv")(
            mem, gains.reshape(nl, 1, D_MODEL), w_kv)


def _forward(xp, xs, nbp, w, st, mem_k, mem_v, cache_k, cache_v):
    assert DEPTH == 2
    nbs = xs.shape[0]
    moe_tm = 1024
    moe_steps_p = (xp.shape[0] // moe_tm) * MOE_STEPS
    rwp = tuple(w[k][0] for k in ('rw_mu', 'rw_w0', 'rw_w2', 'rw_a0', 'rw_a2', 'rw_g2',
                                  'rw_k_k', 'rw_k_a', 'rw_r_k', 'rw_ln_w', 'rw_ln_b'))
    w_in0, w_out0 = w['w_in0_bf'][0], w['w_out0_bf'][0]

    u, p_s = linear(xs, w_in0, gain=w['norm_mix'][0], splits=(S5_WIDTH, RW_PROJ))
    y_s5, s5r_s, s5i_s = s5_mixer(u.reshape(1, nbs, S5_WIDTH), st['s5_re'], st['s5_im'], w['s5p'][0],
                                  w['s5_d'][0], w['s5_w_glu'][0], tc=1)
    y_rw, rw_s = rwkv_step(p_s, st['shift'], st['rwkv'], rwp)
    xs = linear(y_s5.reshape(nbs, S5_WIDTH), w_out0[:S5_WIDTH], x2=y_rw, w2=w_out0[S5_WIDTH:], residual=xs)
    q_s = linear(xs, w['w_mq_bf'][0], gain=w['norm_mem'][0])

    zeros = lambda *shape: jnp.zeros(shape, F32)
    u, p_p = linear(xp, w_in0, gain=w['norm_mix'][0], splits=(S5_WIDTH, RW_PROJ), out_tmajor=True, batch=nbp)
    y_s5, s5r_p, s5i_p = s5_mixer(u, zeros(nbp, S5_STATE), zeros(nbp, S5_STATE), w['s5p'][0],
                                  w['s5_d'][0], w['s5_w_glu'][0], tc=128)
    rw_bs = 4
    rw_steps = (nbp // rw_bs) * (p_p.shape[0] // RW_HD)
    later = (w['moe_w1'], w['moe_w3'], w['moe_w2'], w['w_in1'][0], w['w_out1'][0])
    job = merge_jobs(xattn_step_job(q_s, cache_k, cache_v, 0, tb=nbs // rw_steps), cast_job(later, rw_steps))
    (y_rw, rw_p, sh_p), (att_s, *cast) = rwkv_prompt(
        p_p, zeros(nbp, RW_PROJ), zeros(nbp, RW_HEADS, RW_HD, RW_HD), rwp, bs=rw_bs, side=job)
    moe_w = _group_weights(*[c.reshape(a.shape) for c, a in zip(cast[:3], later[:3])])
    w_in1, w_out1 = (c.reshape(a.shape) for c, a in zip(cast[3:], later[3:]))
    moe = lambda x, layer, **kw: moe_dense(x, w['norm_ffn'][layer], *w['router'][layer], *moe_w, layer, **kw)
    xp = xattn_prompt(xp, w['norm_mem'][0], w['w_mq_bf'][0], mem_k, mem_v, w['w_mo_bf'][0], 0, nb=nbp,
                      pre=(y_s5, w_out0[:S5_WIDTH], y_rw, w_out0[S5_WIDTH:]))

    xs = linear(att_s.reshape(nbs, D_MODEL), w['w_mo_bf'][0], residual=xs)
    xs = moe(xs, 0)
    q, k, v, g = linear(xs, w_in1, gain=w['norm_mix'][1], out_dtype=BF16, splits=(NQ, NQ, NV, NV))
    job = retention_step_job(q, k, v, g, st['ret'], pos0=float(PAST_LEN), tb=nbs // moe_steps_p)
    xp, (ret_s, y_ret) = moe(xp, 0, tm=moe_tm, side=job)
    xs = linear(y_ret.reshape(nbs, NV), w_out1, residual=xs)
    q_s = linear(xs, w['w_mq_bf'][1], gain=w['norm_mem'][1])

    xp, ret_p = retention_layer_prompt(xp, w['norm_mix'][1], w_in1, w_out1, nb=nbp)
    xp = xattn_prompt(xp, w['norm_mem'][1], w['w_mq_bf'][1], mem_k, mem_v, w['w_mo_bf'][1], 1, nb=nbp)
    job = xattn_step_job(q_s, cache_k, cache_v, 1, tb=nbs // moe_steps_p)
    y_p, (att_s,) = moe(xp, 1, tm=moe_tm, final_gain=w['norm_final'], side=job)
    xs = linear(att_s.reshape(nbs, D_MODEL), w['w_mo_bf'][1], residual=xs)
    y_s = moe(xs, 1, final_gain=w['norm_final'])

    grp = lambda z, nb: z.reshape(1, nb, S5_GROUPS, S5_N)
    prompt_out = (y_p, grp(s5r_p, nbp), grp(s5i_p, nbp), rw_p[None], sh_p[None], ret_p[None])
    sample_out = (y_s, grp(s5r_s, nbs), grp(s5i_s, nbs), rw_s[None], p_s[None], ret_s[None])
    return prompt_out, sample_out


def kernel(x_prompt, x_sample, mem_prompt, state_s5_re, state_s5_im, state_rwkv, state_shift, state_ret, cache_mem_k, cache_mem_v, norm_mix, norm_mem, norm_ffn, norm_final, w_in0, w_out0, s5_a_re, s5_a_im, s5_b_re, s5_b_im, s5_c_re, s5_c_im, s5_d, s5_log_dt, s5_w_glu, rw_mu, rw_w0, rw_w2, rw_a0, rw_a2, rw_g2, rw_k_k, rw_k_a, rw_r_k, rw_ln_w, rw_ln_b, w_in1, w_out1, mem_norm, w_mq, w_mk, w_mv, w_mo, moe_w_rc, moe_b_rc, moe_w_rf, moe_b_rf, moe_w1, moe_w3, moe_w2):
    w = dict(norm_mix=norm_mix, norm_mem=norm_mem, norm_ffn=norm_ffn, norm_final=norm_final,
             w_in0=w_in0, w_out0=w_out0, s5_a_re=s5_a_re, s5_a_im=s5_a_im, s5_b_re=s5_b_re, s5_b_im=s5_b_im,
             s5_c_re=s5_c_re, s5_c_im=s5_c_im, s5_d=s5_d, s5_log_dt=s5_log_dt, s5_w_glu=s5_w_glu,
             rw_mu=rw_mu, rw_w0=rw_w0, rw_w2=rw_w2, rw_a0=rw_a0, rw_a2=rw_a2, rw_g2=rw_g2,
             rw_k_k=rw_k_k, rw_k_a=rw_k_a, rw_r_k=rw_r_k, rw_ln_w=rw_ln_w, rw_ln_b=rw_ln_b,
             w_in1=w_in1, w_out1=w_out1, w_mq=w_mq, w_mo=w_mo,
             moe_w_rc=moe_w_rc, moe_b_rc=moe_b_rc, moe_w_rf=moe_w_rf, moe_b_rf=moe_b_rf,
             moe_w1=moe_w1, moe_w3=moe_w3, moe_w2=moe_w2)
    nbp, t_len, _ = x_prompt.shape
    nbs = x_sample.shape[0]
    n_even, n_odd = state_s5_re.shape[0], state_ret.shape[0]
    for name in ('w_in0', 'w_out0', 'w_mq', 'w_mo'):
        w[name + '_bf'] = w[name].astype(BF16)
    w['s5p'] = [_s5_params(s5_a_re[i], s5_a_im[i], s5_b_re[i], s5_b_im[i], s5_c_re[i], s5_c_im[i], s5_log_dt[i])
                for i in range(n_even)]
    w['router'] = [_router_params(moe_w_rc[l], moe_b_rc[l], moe_w_rf[l], moe_b_rf[l]) for l in range(DEPTH)]

    mem = mem_prompt.reshape(nbp * N_MEM, D_MODEL)
    w_kv = jnp.concatenate([w_mk, w_mv], axis=2).astype(BF16)
    mk, mv, mk_h, mv_h = mem_kv(mem, mem_norm, w_kv)
    mem_k_l = mk.reshape(DEPTH, nbp, N_MEM, D_MODEL)
    mem_v_l = mv.reshape(DEPTH, nbp, N_MEM, D_MODEL)
    mem_k_p = mk_h.reshape(DEPTH, nbp, N_MEM, MEM_HEADS, MEM_HD)
    mem_v_p = mv_h.reshape(DEPTH, nbp, N_MEM, MEM_HEADS, MEM_HD)

    assert n_even == 1 and n_odd == 1
    st = dict(s5_re=state_s5_re.reshape(nbs, S5_STATE), s5_im=state_s5_im.reshape(nbs, S5_STATE),
              rwkv=state_rwkv[0], shift=state_shift[0], ret=state_ret[0])
    (y_p, s5r_p, s5i_p, rw_p, sh_p, ret_p), (y_s, s5r_s, s5i_s, rw_s, sh_s, ret_s) = _forward(
        x_prompt.reshape(nbp * t_len, D_MODEL), x_sample.reshape(nbs, D_MODEL), nbp, w, st,
        mem_k_l, mem_v_l, cache_mem_k, cache_mem_v)
    return (y_p.reshape(nbp, t_len, D_MODEL), y_s.reshape(nbs, 1, D_MODEL),
            s5r_p, s5i_p, rw_p, sh_p, ret_p, mem_k_p, mem_v_p, s5r_s, s5i_s, rw_s, sh_s, ret_s)
```

```python
import collections
import functools
import math

import jax
import jax.numpy as jnp
from jax import lax
from jax.experimental import pallas as pl
from jax.experimental.pallas import tpu as pltpu

F32 = jnp.float32
BF16 = jnp.bfloat16

D_MODEL = 1024
DEPTH = 2
PAST_LEN = 16384
S5_WIDTH = 512
S5_GROUP = 16
S5_GROUPS = 32
S5_N = 64
S5_STATE = S5_GROUPS * S5_N
S5_GBLK = 8
RW_WIDTH = 512
RW_HD = 64
RW_HEADS = 8
RW_LORA = 256
RW_PROJ = 3 * RW_WIDTH + RW_LORA
IN0 = S5_WIDTH + RW_PROJ
RET_DK = 256
RET_HEADS = 4
RET_DV = 512
RET_CHUNK = 256
NQ = RET_HEADS * RET_DK
NV = RET_HEADS * RET_DV
IN1 = 2 * NQ + 2 * NV
N_MEM = 256
MEM_HEADS = 4
MEM_HD = 256
MOE_GROUPS = 4
MOE_PER_GROUP = 4
MOE_EXPERTS = 16
MOE_HIDDEN = 256
NORM_EPS = 1e-6
RW_GN_EPS = 64e-5
ROPE_BASE = 10000.0

VMEM_LIMIT = 56 * 1024 * 1024


def _cparams(*sem):
    return pltpu.CompilerParams(dimension_semantics=sem, vmem_limit_bytes=VMEM_LIMIT)


def _bdot(a, b):
    return jnp.dot(a.astype(BF16), b.astype(BF16), preferred_element_type=F32)


def _dot_nt(a, b):
    return lax.dot_general(a.astype(BF16), b.astype(BF16), (((1,), (1,)), ((), ())),
                           preferred_element_type=F32)


def _dot_tn(a, b):
    return lax.dot_general(a.astype(BF16), b.astype(BF16), (((0,), (0,)), ((), ())),
                           preferred_element_type=F32)


def _split3(x):
    hi = x.astype(BF16)
    r1 = x - hi.astype(F32)
    mid = r1.astype(BF16)
    lo = (r1 - mid.astype(F32)).astype(BF16)
    return hi, mid, lo


def _dot_exact_rhs(x, m_bf16, passes=3):
    hi, mid, lo = _split3(x)
    acc = jnp.dot(hi, m_bf16, preferred_element_type=F32)
    if passes > 1:
        acc = acc + jnp.dot(mid, m_bf16, preferred_element_type=F32)
    if passes > 2:
        acc = acc + jnp.dot(lo, m_bf16, preferred_element_type=F32)
    return acc


def _rms(x, g):
    ms = jnp.mean(x * x, axis=-1, keepdims=True)
    return x * lax.rsqrt(ms + NORM_EPS) * g


def _linear_kernel(*refs, norm, two, res):
    it = iter(refs)
    x_ref = next(it)
    g_ref = next(it) if norm else None
    w_ref = next(it)
    x2_ref = next(it) if two else None
    w2_ref = next(it) if two else None
    r_ref = next(it) if res else None
    o_refs = list(it)
    x = x_ref[...].astype(F32)
    if norm:
        x = _rms(x, g_ref[...])
    xb = x.astype(BF16)
    x2b = x2_ref[...].astype(BF16) if two else None
    col = 0
    for o_ref in o_refs:
        m = o_ref.shape[-1]
        step = next((s for s in (512, 256) if m % s == 0), m)
        for j in range(m // step):
            sl = slice(col + j * step, col + (j + 1) * step)
            acc = jnp.dot(xb, w_ref[:, sl], preferred_element_type=F32)
            if two:
                acc = acc + jnp.dot(x2b, w2_ref[:, sl], preferred_element_type=F32)
            if res:
                acc = acc + r_ref[:, sl]
            o_ref[:, j * step:(j + 1) * step] = acc.astype(o_ref.dtype)
        col += m


def _row_spec(tm, width, tmajor_b):
    if tmajor_b is None:
        return pl.BlockSpec((tm, width), lambda i: (i, 0))
    nb, tiles_per_b = tmajor_b
    return pl.BlockSpec((tm, width), lambda i: (i % tiles_per_b, i // tiles_per_b))


def linear(x, w, *, gain=None, x2=None, w2=None, residual=None, out_dtype=F32, tm=512,
           out_tmajor=False, batch=None, splits=None, name="linear"):
    n, k = x.shape
    nb = batch
    t_len = n // nb if nb else None
    m = w.shape[1]
    tm = min(tm, t_len if out_tmajor else n)
    assert n % tm == 0
    tiles_per_b = (t_len // tm) if out_tmajor else None
    rows = lambda a: pl.BlockSpec((tm, a.shape[-1]), lambda i: (i, 0))
    args, specs = [x], [rows(x)]
    if gain is not None:
        args.append(gain.reshape(1, k).astype(F32))
        specs.append(pl.BlockSpec((1, k), lambda i: (0, 0)))
    args.append(w)
    specs.append(pl.BlockSpec(w.shape, lambda i: (0, 0)))
    if x2 is not None:
        args += [x2, w2]
        specs += [rows(x2), pl.BlockSpec(w2.shape, lambda i: (0, 0))]
    if residual is not None:
        args.append(residual)
        specs.append(rows(residual))
    widths = tuple(splits) if splits else (m,)
    assert sum(widths) == m
    if out_tmajor:
        out_shape = [jax.ShapeDtypeStruct((t_len, nb * mw), out_dtype) for mw in widths]
    else:
        out_shape = [jax.ShapeDtypeStruct((n, mw), out_dtype) for mw in widths]
    out_specs = [_row_spec(tm, mw, (nb, tiles_per_b) if out_tmajor else None) for mw in widths]
    kern = functools.partial(_linear_kernel, norm=gain is not None, two=x2 is not None,
                             res=residual is not None)
    outs = pl.pallas_call(
        kern, grid=(n // tm,), in_specs=specs, out_specs=out_specs, out_shape=out_shape,
        compiler_params=_cparams("parallel"), name=name)(*args)
    if out_tmajor:
        outs = [o.reshape(t_len, nb, mw) for o, mw in zip(outs, widths)]
    return outs if splits else outs[0]


def _s5_kernel(u_ref, h_re_ref, h_im_ref, abar_re_ref, abar_im_ref, bb_re_ref, bb_im_ref,
               cc_re_ref, cc_im_ref, d_ref, wglu_ref, y_ref, s_re_ref, s_im_ref,
               x_re, x_im, st_re, st_im, il_scr, *, tc, nb, flat):
    c = pl.program_id(0)
    nlb = S5_WIDTH // 128
    rows = tc * nb
    nblk = S5_GROUPS // S5_GBLK
    bw_in = S5_GBLK * S5_GROUP
    bw_st = S5_GBLK * S5_N

    @pl.when(c == 0)
    def _():
        st_re[...] = h_re_ref[...]
        st_im[...] = h_im_ref[...]

    if flat:
        for b in range(nb):
            for j in range(nlb):
                il_scr[j, pl.ds(b, tc, stride=nb), :] = u_ref[:, b * S5_WIDTH + j * 128:b * S5_WIDTH + (j + 1) * 128]
        u = jnp.concatenate([il_scr[j] for j in range(nlb)], axis=-1)
    else:
        u = u_ref[...].reshape(rows, S5_WIDTH)
    ub = u.astype(BF16)
    for gb in range(nblk):
        ui = ub[:, gb * bw_in:(gb + 1) * bw_in]
        x_re[:, gb * bw_st:(gb + 1) * bw_st] = jnp.dot(ui, bb_re_ref[gb], preferred_element_type=F32)
        x_im[:, gb * bw_st:(gb + 1) * bw_st] = jnp.dot(ui, bb_im_ref[gb], preferred_element_type=F32)

    lane_blk = 1024
    for lb in range(S5_STATE // lane_blk):
        sl = slice(lb * lane_blk, (lb + 1) * lane_blk)
        ar = jnp.broadcast_to(abar_re_ref[:, sl], (nb, lane_blk))
        ai = jnp.broadcast_to(abar_im_ref[:, sl], (nb, lane_blk))

        def body(t, carry, sl=sl, ar=ar, ai=ai):
            xr, xi = carry
            r0 = pl.multiple_of(t * nb, nb)
            br = x_re[pl.ds(r0, nb), sl]
            bi = x_im[pl.ds(r0, nb), sl]
            nr = ar * xr - ai * xi + br
            ni = ar * xi + ai * xr + bi
            x_re[pl.ds(r0, nb), sl] = nr
            x_im[pl.ds(r0, nb), sl] = ni
            return nr, ni

        fr, fi = lax.fori_loop(0, tc, body, (st_re[:, sl], st_im[:, sl]), unroll=min(tc, 4))
        st_re[:, sl] = fr
        st_im[:, sl] = fi

    for gb in range(nblk):
        xr = x_re[:, gb * bw_st:(gb + 1) * bw_st].astype(BF16)
        xi = x_im[:, gb * bw_st:(gb + 1) * bw_st].astype(BF16)
        yb = (jnp.dot(xr, cc_re_ref[gb], preferred_element_type=F32)
              - jnp.dot(xi, cc_im_ref[gb], preferred_element_type=F32))
        cs = slice(gb * bw_in, (gb + 1) * bw_in)
        yb = yb + d_ref[:, cs] * u[:, cs]
        x_re[:, cs] = jax.nn.gelu(yb)
    y = x_re[:, :S5_WIDTH]
    y = y * jax.nn.sigmoid(jnp.dot(y.astype(BF16), wglu_ref[...], preferred_element_type=F32))
    if flat:
        for j in range(nlb):
            il_scr[j] = y[:, j * 128:(j + 1) * 128]
        for b in range(nb):
            for j in range(nlb):
                y_ref[:, b * S5_WIDTH + j * 128:b * S5_WIDTH + (j + 1) * 128] = (
                    il_scr[j, pl.ds(b, tc, stride=nb), :].astype(y_ref.dtype))
    else:
        y_ref[...] = y.reshape(y_ref.shape).astype(y_ref.dtype)

    @pl.when(c == pl.num_programs(0) - 1)
    def _():
        s_re_ref[...] = st_re[...]
        s_im_ref[...] = st_im[...]


def _s5_params(a_re, a_im, b_re, b_im, c_re, c_im, log_dt):
    dt = jnp.exp(log_dt.astype(F32))[:, None]
    ar, ai = a_re.astype(F32), a_im.astype(F32)
    mag = jnp.exp(dt * ar)
    abar_re, abar_im = mag * jnp.cos(dt * ai), mag * jnp.sin(dt * ai)
    den = ar * ar + ai * ai
    nr = abar_re - 1.0
    coef_re = (nr * ar + abar_im * ai) / den
    coef_im = (abar_im * ar - nr * ai) / den
    cr, ci = coef_re[..., None], coef_im[..., None]
    brf, bif = b_re.astype(F32), b_im.astype(F32)
    bb_re = cr * brf - ci * bif
    bb_im = cr * bif + ci * brf
    nblk = S5_GROUPS // S5_GBLK
    eye = jnp.eye(S5_GBLK, dtype=F32)

    def blockdiag_in(bb):
        t = jnp.transpose(bb, (0, 2, 1)).reshape(nblk, S5_GBLK, S5_GROUP, S5_N)
        m = jnp.einsum('kgcn,gh->kgchn', t, eye)
        return m.reshape(nblk, S5_GBLK * S5_GROUP, S5_GBLK * S5_N).astype(BF16)

    def blockdiag_out(cc):
        t = jnp.transpose(cc.astype(F32), (0, 2, 1)).reshape(nblk, S5_GBLK, S5_N, S5_GROUP)
        m = jnp.einsum('khnc,hg->khngc', t, eye)
        return m.reshape(nblk, S5_GBLK * S5_N, S5_GBLK * S5_GROUP).astype(BF16)

    return (abar_re.reshape(1, S5_STATE), abar_im.reshape(1, S5_STATE),
            blockdiag_in(bb_re), blockdiag_in(bb_im), blockdiag_out(c_re), blockdiag_out(c_im))


def s5_mixer(u_tm, h_re, h_im, params, d_skip, w_glu, *, tc):
    t_len, nb, _ = u_tm.shape
    abar_re, abar_im, bb_re, bb_im, cc_re, cc_im = params
    tc = min(tc, t_len)
    assert t_len % tc == 0 and nb % 8 == 0
    rows = tc * nb
    flat = t_len > 1
    full = lambda a: pl.BlockSpec(a.shape, lambda c: (0,) * a.ndim)
    if flat:
        u_arg = u_tm.reshape(t_len, nb * S5_WIDTH)
        io_spec = pl.BlockSpec((tc, nb * S5_WIDTH), lambda c: (c, 0))
        y_shape = jax.ShapeDtypeStruct((t_len, nb * S5_WIDTH), BF16)
    else:
        u_arg = u_tm
        io_spec = pl.BlockSpec((tc, nb, S5_WIDTH), lambda c: (c, 0, 0))
        y_shape = jax.ShapeDtypeStruct((t_len, nb, S5_WIDTH), BF16)
    args = (u_arg, h_re, h_im, abar_re, abar_im, bb_re, bb_im, cc_re, cc_im,
            d_skip.reshape(1, S5_WIDTH).astype(F32), w_glu.astype(BF16))
    in_specs = [io_spec] + [full(a) for a in args[1:]]
    st_shape = jax.ShapeDtypeStruct((nb, S5_STATE), F32)
    st_spec = pl.BlockSpec((nb, S5_STATE), lambda c: (0, 0))
    scratch = [pltpu.VMEM((rows, S5_STATE), F32), pltpu.VMEM((rows, S5_STATE), F32),
               pltpu.VMEM((nb, S5_STATE), F32), pltpu.VMEM((nb, S5_STATE), F32),
               pltpu.VMEM((S5_WIDTH // 128, rows if flat else 8, 128), F32)]
    y, s_re, s_im = pl.pallas_call(
        functools.partial(_s5_kernel, tc=tc, nb=nb, flat=flat), grid=(t_len // tc,), in_specs=in_specs,
        out_specs=(io_spec, st_spec, st_spec), out_shape=(y_shape, st_shape, st_shape),
        scratch_shapes=scratch, compiler_params=_cparams("arbitrary"), name="s5_mixer")(*args)
    return y.reshape(t_len, nb, S5_WIDTH), s_re, s_im


def _head_ones():
    i = lax.broadcasted_iota(jnp.int32, (RW_WIDTH, RW_WIDTH), 0) // RW_HD
    j = lax.broadcasted_iota(jnp.int32, (RW_WIDTH, RW_WIDTH), 1) // RW_HD
    return jnp.where(i == j, 1.0, 0.0).astype(BF16)


def _softplus(z):
    return jnp.maximum(z, 0.0) + jnp.log1p(jnp.exp(-jnp.abs(z)))


def _rw_prep(p, p_prev, prm, ones_bd):
    mu, w0, w2, a0, a2, g2, k_k, k_a = prm
    xm = p + (p_prev - p) * mu
    o1, o2, o3 = RW_WIDTH, 2 * RW_WIDTH, 3 * RW_WIDTH
    r, k, v = xm[:, :o1], xm[:, o1:o2], xm[:, o2:o3]
    wd, ad, gd = xm[:, o3:o3 + 64], xm[:, o3 + 64:o3 + 128], xm[:, o3 + 128:]
    w = -_softplus(-(w0 + _bdot(jnp.tanh(wd), w2))) - 0.5
    lw = -jnp.exp(w)
    a = jax.nn.sigmoid(a0 + _bdot(ad, a2))
    g = _bdot(jax.nn.sigmoid(gd), g2)
    kk = k * k_k
    ss = _dot_exact_rhs(kk * kk, ones_bd, passes=1)
    kk = kk / jnp.maximum(jnp.sqrt(ss), 1e-12)
    k = k * (1.0 + (a - 1.0) * k_a)
    return r, lw, k, v, -kk, kk * a, g


def _rw_post(o, r, k, v, g, r_k, ln_w, ln_b, ones_bd):
    inv = 1.0 / RW_HD
    mean = _dot_exact_rhs(o, ones_bd, passes=2) * inv
    d = o - mean
    var = _dot_exact_rhs(d * d, ones_bd, passes=1) * inv
    on = d * lax.rsqrt(var + RW_GN_EPS) * ln_w + ln_b
    bonus = _dot_exact_rhs(r * k * r_k, ones_bd, passes=1) * v
    return (on + bonus) * g


def _rw_chunk_kernel(p_ref, shift_ref, h0_ref, mu_ref, w0_ref, w2_ref, a0_ref, a2_ref, g2_ref,
                     kk_ref, ka_ref, rk_ref, lnw_ref, lnb_ref,
                     y_ref, hfin_ref, shout_ref, prev_scr, h_scr, o_scr, *, c_len, bs, side=None):
    c = pl.program_id(1)
    nc = pl.num_programs(1)
    cl = c_len

    @pl.when(c == 0)
    def _():
        prev_scr[...] = shift_ref[:, 0, :]
        h_scr[...] = h0_ref[...]

    ones_bd = _head_ones()
    row = lax.broadcasted_iota(jnp.int32, (cl, RW_PROJ), 0)
    ps, pprevs = [], []
    for bi in range(bs):
        p = p_ref[:, bi * RW_PROJ:(bi + 1) * RW_PROJ]
        pprevs.append(jnp.where(row == 0, prev_scr[bi:bi + 1, :], pltpu.roll(p, 1, 0)))
        ps.append(p)
    p_all = jnp.concatenate(ps, axis=0) if bs > 1 else ps[0]
    pprev_all = jnp.concatenate(pprevs, axis=0) if bs > 1 else pprevs[0]
    prm = (mu_ref[...], w0_ref[...], w2_ref[...], a0_ref[...], a2_ref[...], g2_ref[...],
           kk_ref[...], ka_ref[...])
    r, lw, k, v, a, b, g = _rw_prep(p_all, pprev_all, prm, ones_bd)

    ti = lax.broadcasted_iota(jnp.int32, (cl, cl), 0)
    si = lax.broadcasted_iota(jnp.int32, (cl, cl), 1)
    lmat = jnp.where(ti >= si, 1.0, 0.0).astype(BF16)
    eye = jnp.where(ti == si, 1.0, 0.0)
    mi = lax.broadcasted_iota(jnp.int32, (2 * cl, 3 * cl), 0)
    mj = lax.broadcasted_iota(jnp.int32, (2 * cl, 3 * cl), 1)
    t_row = jnp.where(mi >= cl, mi - cl, mi)
    s_col = jnp.where(mj < cl, mj, jnp.where(mj >= 2 * cl, mj - 2 * cl, -4 * cl))
    keep = (t_row - s_col) >= jnp.where(mi >= cl, 0, 1)
    eye_bf = eye.astype(BF16)

    lhs_l, rhs_l, vh_l, hcat_l, kb_l, etot_l = [], [], [], [], [], []
    for bi in range(bs):
        rs = slice(bi * cl, (bi + 1) * cl)
        lw_b = lw[rs]
        l_hi, l_mid, l_lo = _split3(lw_b)
        cum = (jnp.dot(lmat, l_hi, preferred_element_type=F32)
               + jnp.dot(lmat, l_mid, preferred_element_type=F32)
               + jnp.dot(lmat, l_lo, preferred_element_type=F32))
        tot = cum[cl - 1:cl, :]
        e_neg = jnp.exp(-cum)
        e_rem = jnp.exp(tot - cum)
        at = (a[rs] * jnp.exp(cum - lw_b)).astype(BF16)
        rt = (r[rs] * jnp.exp(cum)).astype(BF16)
        bt = (b[rs] * e_neg).astype(BF16)
        kt = (k[rs] * e_neg).astype(BF16)
        bh = (b[rs] * e_rem).astype(BF16)
        kh = (k[rs] * e_rem).astype(BF16)
        e_tot = jnp.exp(tot)
        vb = v[rs].astype(BF16)
        for h in range(RW_HEADS):
            hs = slice(h * RW_HD, (h + 1) * RW_HD)
            lhs_l.append(jnp.concatenate([at[:, hs], rt[:, hs]], axis=0))
            rhs_l.append(jnp.concatenate([kt[:, hs], eye_bf, bt[:, hs]], axis=0))
            vh_l.append(vb[:, hs])
            kb_l.append(jnp.concatenate([kh[:, hs], bh[:, hs]], axis=0))
            etot_l.append(jnp.sum(eye * e_tot[:, hs], axis=-1, keepdims=True))
            hcat_l.append(h_scr[bi, h])

    nitem = bs * RW_HEADS
    items = range(nitem)
    aa_l = [jnp.where(keep, _dot_nt(lhs_l[i], rhs_l[i]), 0.0).astype(BF16) for i in items]
    pw_l = [aa_l[i][:cl, 2 * cl:] for i in items]
    tinv_l = [eye_bf + pw_l[i] for i in items]
    for _ in range(int(math.log2(cl)) - 1):
        pw_l = [jnp.dot(pw_l[i], pw_l[i], preferred_element_type=F32).astype(BF16) for i in items]
        tinv_l = [jnp.dot(tinv_l[i], eye_bf + pw_l[i], preferred_element_type=F32).astype(BF16) for i in items]
    vh_cat = [jnp.concatenate([vh_l[i], hcat_l[i].astype(BF16)], axis=0) for i in items]
    x1_l = [jnp.dot(aa_l[i][:cl, :2 * cl], vh_cat[i], preferred_element_type=F32).astype(BF16) for i in items]
    u_l = [jnp.dot(tinv_l[i], x1_l[i], preferred_element_type=F32).astype(BF16) for i in items]
    o_l = [jnp.dot(aa_l[i][cl:, :], jnp.concatenate([vh_cat[i], u_l[i]], axis=0),
                   preferred_element_type=F32) for i in items]
    hn_l = [hcat_l[i] * etot_l[i]
            + lax.dot_general(kb_l[i], jnp.concatenate([vh_l[i], u_l[i]], axis=0), (((0,), (0,)), ((), ())),
                              preferred_element_type=F32) for i in items]

    for bi in range(bs):
        for h in range(RW_HEADS):
            i = bi * RW_HEADS + h
            o_scr[bi * cl:(bi + 1) * cl, h * RW_HD:(h + 1) * RW_HD] = o_l[i]
            h_scr[bi, h] = hn_l[i]
        prev_scr[bi:bi + 1, :] = ps[bi][cl - 1:cl, :]

    y = _rw_post(o_scr[...], r, k, v, g, rk_ref[...], lnw_ref[...], lnb_ref[...], ones_bd)
    for bi in range(bs):
        y_ref[:, bi * RW_WIDTH:(bi + 1) * RW_WIDTH] = y[bi * cl:(bi + 1) * cl].astype(y_ref.dtype)
    if side is not None:
        side()

    @pl.when(c == nc - 1)
    def _():
        hfin_ref[...] = h_scr[...]
        for bi in range(bs):
            shout_ref[bi] = ps[bi][cl - 1:cl, :]


def _rw_param_args(mu, w0, w2, a0, a2, g2, k_k, k_a, r_k, ln_w, ln_b):
    row = lambda z: z.reshape(1, -1).astype(F32)
    return (row(mu), row(w0), w2.astype(BF16), row(a0), a2.astype(BF16), g2.astype(BF16),
            row(k_k), row(k_a), row(r_k), row(ln_w), row(ln_b))


def rwkv_prompt(p_tm, shift, s0, params, *, bs=4, side=None):
    c_len = RW_HD
    t_len, nb, _ = p_tm.shape
    assert t_len % c_len == 0 and nb % bs == 0
    prm = _rw_param_args(*params)
    const = lambda a: pl.BlockSpec(a.shape, lambda b, c: (0,) * a.ndim)
    st_spec = pl.BlockSpec((bs, RW_HEADS, RW_HD, RW_HD), lambda b, c: (b, 0, 0, 0))
    sh_spec = pl.BlockSpec((bs, 1, RW_PROJ), lambda b, c: (b, 0, 0))
    in_specs = [pl.BlockSpec((c_len, bs * RW_PROJ), lambda b, c: (c, b)), sh_spec, st_spec] + [const(a) for a in prm]
    out_shape = (jax.ShapeDtypeStruct((t_len, nb * RW_WIDTH), BF16),
                 jax.ShapeDtypeStruct((nb, RW_HEADS, RW_HD, RW_HD), F32),
                 jax.ShapeDtypeStruct((nb, 1, RW_PROJ), F32))
    out_specs = (pl.BlockSpec((c_len, bs * RW_WIDTH), lambda b, c: (c, b)), st_spec, sh_spec)
    scratch = [pltpu.VMEM((bs, RW_PROJ), F32), pltpu.VMEM((bs, RW_HEADS, RW_HD, RW_HD), F32),
               pltpu.VMEM((bs * c_len, RW_WIDTH), F32)]
    h0 = jnp.swapaxes(s0, -1, -2)
    args = [p_tm.reshape(t_len, nb * RW_PROJ), shift.reshape(nb, 1, RW_PROJ), h0, *prm]
    kern = functools.partial(_rw_chunk_kernel, c_len=c_len, bs=bs)
    grid = (nb // bs, t_len // c_len)
    out_shape, out_specs = list(out_shape), list(out_specs)
    if side is not None:
        assert side.steps == grid[0] * grid[1]
        kern = _with_side(kern, len(args), 3, side)
        args += list(side.args)
        in_specs += side.in_specs(grid[1])
        out_shape += list(side.out_shape)
        out_specs += side.out_specs(grid[1])
    outs = pl.pallas_call(
        kern, grid=grid, in_specs=in_specs, out_specs=out_specs, out_shape=out_shape,
        scratch_shapes=scratch, compiler_params=_cparams("parallel", "arbitrary"), name="rwkv_prompt")(*args)
    y, h_fin, sh = outs[:3]
    res = (y.reshape(t_len, nb, RW_WIDTH), jnp.swapaxes(h_fin, -1, -2), sh.reshape(nb, RW_PROJ))
    return res if side is None else (res, outs[3:])


def _rw_step_prep_kernel(p_ref, shift_ref, mu_ref, w0_ref, w2_ref, a0_ref, a2_ref, g2_ref, kk_ref, ka_ref,
                         r_ref, k_ref, v_ref, g_ref, rt_ref, wt_ref, kt_ref, at_ref, bt_ref, vt_ref):
    prm = (mu_ref[...], w0_ref[...], w2_ref[...], a0_ref[...], a2_ref[...], g2_ref[...],
           kk_ref[...], ka_ref[...])
    r, lw, k, v, a, b, g = _rw_prep(p_ref[...], shift_ref[...], prm, _head_ones())
    r_ref[...] = r
    k_ref[...] = k
    v_ref[...] = v
    g_ref[...] = g
    rt_ref[...] = r.T
    wt_ref[...] = jnp.exp(lw).T
    kt_ref[...] = k.T
    at_ref[...] = a.T
    bt_ref[...] = b.T
    vt_ref[...] = v.T


def _rw_step_core_kernel(s_ref, r_ref, w_ref, k_ref, a_ref, b_ref, v_ref, s_out_ref, o_ref):
    r, w, k, a, b = r_ref[0], w_ref[0], k_ref[0], a_ref[0], b_ref[0]
    for j in range(s_ref.shape[1]):
        s = s_ref[0, j]
        sa = jnp.sum(s * a, axis=0, keepdims=True)
        s_new = s * w + sa * b + v_ref[0, j:j + 1, :] * k
        s_out_ref[0, j] = s_new
        o_ref[0, j:j + 1, :] = jnp.sum(s_new * r, axis=0, keepdims=True)


def _rw_step_post_kernel(ot_ref, r_ref, k_ref, v_ref, g_ref, rk_ref, lnw_ref, lnb_ref, y_ref):
    y_ref[...] = _rw_post(ot_ref[...].T, r_ref[...], k_ref[...], v_ref[...], g_ref[...],
                          rk_ref[...], lnw_ref[...], lnb_ref[...], _head_ones()).astype(y_ref.dtype)


def rwkv_step(p, shift, s0, params, *, vb=32):
    n = p.shape[0]
    prm = _rw_param_args(*params)
    vec = jax.ShapeDtypeStruct((n, RW_WIDTH), F32)
    vec_t = jax.ShapeDtypeStruct((RW_WIDTH, n), F32)
    r, k, v, g, rt, wt, kt, at, bt, vt = pl.pallas_call(
        _rw_step_prep_kernel, out_shape=(vec,) * 4 + (vec_t,) * 6, name="rwkv_step_prep")(p, shift, *prm[:8])
    heads = lambda z: z.reshape(RW_HEADS, RW_HD, n)
    k_spec = pl.BlockSpec((1, RW_HD, n), lambda h, j: (h, 0, 0))
    v_spec = pl.BlockSpec((1, vb, n), lambda h, j: (h, j, 0))
    st_spec = pl.BlockSpec((1, vb, RW_HD, n), lambda h, j: (h, j, 0, 0))
    st = jnp.transpose(s0, (1, 2, 3, 0))
    s_new, ot = pl.pallas_call(
        _rw_step_core_kernel, grid=(RW_HEADS, RW_HD // vb),
        in_specs=[st_spec] + [k_spec] * 5 + [v_spec], out_specs=(st_spec, v_spec),
        out_shape=(jax.ShapeDtypeStruct(st.shape, F32), jax.ShapeDtypeStruct((RW_HEADS, RW_HD, n), F32)),
        compiler_params=_cparams("parallel", "parallel"), name="rwkv_step_core")(
            st, heads(rt), heads(wt), heads(kt), heads(at), heads(bt), heads(vt))
    y = pl.pallas_call(
        _rw_step_post_kernel, out_shape=jax.ShapeDtypeStruct((n, RW_WIDTH), BF16), name="rwkv_step_post")(
            ot.reshape(RW_WIDTH, n), r, k, v, g, *prm[8:])
    return y, jnp.transpose(s_new, (3, 0, 1, 2))


RET_LOG_G = tuple(math.log(1.0 - 2.0 ** (-5.0 - h)) for h in range(RET_HEADS))


def _rope_tables(pos, half):
    j = lax.broadcasted_iota(jnp.int32, (1, half), 1).astype(F32)
    inv = jnp.exp(j * (-math.log(ROPE_BASE) / half))
    ang = pos * inv
    return jnp.cos(ang), jnp.sin(ang)


def _rope(x, cos, sin):
    half = RET_DK // 2
    outs = []
    for h in range(RET_HEADS):
        x1 = x[:, h * RET_DK:h * RET_DK + half]
        x2 = x[:, h * RET_DK + half:(h + 1) * RET_DK]
        outs += [x1 * cos - x2 * sin, x1 * sin + x2 * cos]
    return jnp.concatenate(outs, axis=-1)


def _ret_norm_gate(o, g):
    o = o * lax.rsqrt(jnp.mean(o * o, axis=-1, keepdims=True) + NORM_EPS)
    return jax.nn.silu(g) * o


def _ret_tables_kernel(cos_ref, sin_ref, dmask_ref, qdec_ref, kdec_ref, *, c_len):
    t_len = cos_ref.shape[0]
    pos = lax.broadcasted_iota(jnp.int32, (t_len, 1), 0).astype(F32)
    cos, sin = _rope_tables(pos, RET_DK // 2)
    cos_ref[...] = cos
    sin_ref[...] = sin
    ti = lax.broadcasted_iota(jnp.int32, (c_len, 1), 0).astype(F32)
    ii = lax.broadcasted_iota(jnp.int32, (c_len, c_len), 0)
    jj = lax.broadcasted_iota(jnp.int32, (c_len, c_len), 1)
    diff = (ii - jj).astype(F32)
    for h in range(RET_HEADS):
        lg = RET_LOG_G[h]
        dmask_ref[h] = jnp.where(diff >= 0, jnp.exp(lg * jnp.maximum(diff, 0.0)), 0.0)
        qdec_ref[h] = jnp.exp(lg * (ti + 1.0))
        kdec_ref[h] = jnp.exp(lg * (c_len - 1.0 - ti))


def _ret_layer_kernel(x_ref, gain_ref, win_ref, wout_ref, cos_ref, sin_ref, dmask_ref, qdec_ref, kdec_ref,
                      o_ref, sfin_ref, s_scr, y_scr, *, c_len, side=None):
    c = pl.program_id(1)

    @pl.when(c == 0)
    def _():
        s_scr[...] = jnp.zeros_like(s_scr)

    x = x_ref[...]
    hb = _rms(x, gain_ref[...]).astype(BF16)
    proj = lambda lo, width: jnp.dot(hb, win_ref[:, lo:lo + width], preferred_element_type=F32)
    cos, sin = cos_ref[...], sin_ref[...]
    q = _rope(proj(0, NQ), cos, sin)
    k = _rope(proj(NQ, NQ), cos, sin) * (RET_DK ** -0.5)
    for h in range(RET_HEADS):
        c_dec = math.exp(RET_LOG_G[h] * c_len)
        qh = q[:, h * RET_DK:(h + 1) * RET_DK]
        kh = k[:, h * RET_DK:(h + 1) * RET_DK]
        vh = proj(2 * NQ + h * RET_DV, RET_DV).astype(BF16)
        s_h = s_scr[h]
        sc = _dot_nt(qh, kh) * dmask_ref[h]
        o = _bdot(sc, vh) + _bdot(qh * qdec_ref[h], s_h)
        s_scr[h] = s_h * c_dec + _dot_tn(kh * kdec_ref[h], vh)
        gh = proj(2 * NQ + NV + h * RET_DV, RET_DV)
        y_scr[:, h * RET_DV:(h + 1) * RET_DV] = _ret_norm_gate(o, gh).astype(BF16)
    o_ref[...] = x + jnp.dot(y_scr[...], wout_ref[...], preferred_element_type=F32)
    if side is not None:
        side()

    @pl.when(c == pl.num_programs(1) - 1)
    def _():
        sfin_ref[0] = s_scr[...]


def retention_layer_prompt(x, gain, w_in, w_out, *, nb, c_len=RET_CHUNK, side=None):
    n = x.shape[0]
    t_len = n // nb
    nc = t_len // c_len
    half = RET_DK // 2
    tabs = pl.pallas_call(
        functools.partial(_ret_tables_kernel, c_len=c_len),
        out_shape=(jax.ShapeDtypeStruct((t_len, half), F32), jax.ShapeDtypeStruct((t_len, half), F32),
                   jax.ShapeDtypeStruct((RET_HEADS, c_len, c_len), F32),
                   jax.ShapeDtypeStruct((RET_HEADS, c_len, 1), F32),
                   jax.ShapeDtypeStruct((RET_HEADS, c_len, 1), F32)),
        name="retention_tables")()
    row = pl.BlockSpec((c_len, D_MODEL), lambda b, c: (b * nc + c, 0))
    pos_spec = pl.BlockSpec((c_len, half), lambda b, c: (c, 0))
    const = lambda a: pl.BlockSpec(a.shape, lambda b, c: (0,) * a.ndim)
    st_spec = pl.BlockSpec((1, RET_HEADS, RET_DK, RET_DV), lambda b, c: (b, 0, 0, 0))
    gain = gain.reshape(1, D_MODEL)
    once = lambda a: pl.BlockSpec(a.shape, lambda b, c: (0,) * a.ndim, pipeline_mode=pl.Buffered(1))
    args = [x, gain, w_in, w_out, *tabs]
    in_specs = [row, const(gain), once(w_in), once(w_out), pos_spec, pos_spec] + [const(a) for a in tabs[2:]]
    out_shape = [jax.ShapeDtypeStruct((n, D_MODEL), F32), jax.ShapeDtypeStruct((nb, RET_HEADS, RET_DK, RET_DV), F32)]
    out_specs = [row, st_spec]
    kern = functools.partial(_ret_layer_kernel, c_len=c_len)
    if side is not None:
        assert side.steps == nb * nc
        kern = _with_side(kern, len(args), 2, side)
        args += list(side.args)
        in_specs += side.in_specs(nc)
        out_shape += list(side.out_shape)
        out_specs += side.out_specs(nc)
    outs = pl.pallas_call(
        kern, grid=(nb, nc), in_specs=in_specs, out_specs=out_specs, out_shape=out_shape,
        scratch_shapes=[pltpu.VMEM((RET_HEADS, RET_DK, RET_DV), F32), pltpu.VMEM((c_len, NV), BF16)],
        compiler_params=_cparams("parallel", "arbitrary"), name="retention_layer")(*args)
    return (outs[0], outs[1]) if side is None else ((outs[0], outs[1]), outs[2:])


def _ret_step_rope_kernel(q_ref, k_ref, qo_ref, ko_ref, *, pos0):
    pos = jnp.full((q_ref.shape[0], 1), pos0, F32)
    cos, sin = _rope_tables(pos, RET_DK // 2)
    qo_ref[...] = _rope(q_ref[...].astype(F32), cos, sin).T
    ko_ref[...] = (_rope(k_ref[...].astype(F32), cos, sin) * (RET_DK ** -0.5)).T


def _ret_step_core_kernel(s_ref, qt_ref, kt_ref, v_ref, g_ref, s_out_ref, y_ref):
    tb = s_ref.shape[0]
    step = pl.program_id(0) * pl.num_programs(1) + pl.program_id(1)
    lane = lax.broadcasted_iota(jnp.int32, qt_ref.shape, 1)
    for i in range(tb):
        mine = lane == step * tb + i
        q_col = jnp.sum(jnp.where(mine, qt_ref[...], 0.0), axis=-1, keepdims=True)
        k_col = jnp.sum(jnp.where(mine, kt_ref[...], 0.0), axis=-1, keepdims=True)
        for h in range(RET_HEADS):
            gam = math.exp(RET_LOG_G[h])
            s_h = s_ref[i, h]
            qc = q_col[h * RET_DK:(h + 1) * RET_DK]
            kc = k_col[h * RET_DK:(h + 1) * RET_DK]
            vs = slice(h * RET_DV, (h + 1) * RET_DV)
            vr = v_ref[i, :, vs].astype(F32)
            qk = jnp.sum(qc * kc, axis=0, keepdims=True)
            o = qk * vr + jnp.sum((qc * gam) * s_h, axis=0, keepdims=True)
            s_out_ref[i, h] = s_h * gam + kc * vr
            y_ref[i, :, vs] = _ret_norm_gate(o, g_ref[i, :, vs].astype(F32)).astype(y_ref.dtype)


def retention_step_job(q, k, v, g, s0, *, pos0, tb):
    n = q.shape[0]
    vec_t = jax.ShapeDtypeStruct((NQ, n), F32)
    qt, kt = pl.pallas_call(functools.partial(_ret_step_rope_kernel, pos0=pos0), out_shape=(vec_t, vec_t),
                            name="retention_step_rope")(q, k)
    st = lambda inner: pl.BlockSpec((tb, RET_HEADS, RET_DK, RET_DV), lambda i, j: (i * inner + j, 0, 0, 0))
    rw = lambda inner: pl.BlockSpec((tb, 1, NV), lambda i, j: (i * inner + j, 0, 0))
    whole = lambda inner: pl.BlockSpec((NQ, n), lambda i, j: (0, 0))
    return SideJob(
        body=_ret_step_core_kernel,
        args=(s0, qt, kt, v.reshape(n, 1, NV), g.reshape(n, 1, NV)),
        in_specs=lambda inner: [st(inner), whole(inner), whole(inner), rw(inner), rw(inner)],
        out_shape=(jax.ShapeDtypeStruct(s0.shape, F32), jax.ShapeDtypeStruct((n, 1, NV), BF16)),
        out_specs=lambda inner: [st(inner), rw(inner)],
        steps=n // tb)


def _xattn_prompt_kernel(x_ref, g_ref, wq_ref, mk_ref, mv_ref, wo_ref, *rest, pre):
    if pre:
        ya_ref, wa_ref, yb_ref, wb_ref, o_ref, att_scr = rest
        x = (x_ref[...] + jnp.dot(ya_ref[...], wa_ref[...], preferred_element_type=F32)
             + jnp.dot(yb_ref[...], wb_ref[...], preferred_element_type=F32))
    else:
        o_ref, att_scr = rest
        x = x_ref[...]
    q = jnp.dot(_rms(x, g_ref[...]).astype(BF16), wq_ref[...], preferred_element_type=F32)
    for h in range(MEM_HEADS):
        hs = slice(h * MEM_HD, (h + 1) * MEM_HD)
        s = _dot_nt(q[:, hs], mk_ref[0, :, hs]) * (MEM_HD ** -0.5)
        s = s - jnp.max(s, axis=-1, keepdims=True)
        e = jnp.exp(s)
        p = e / jnp.sum(e, axis=-1, keepdims=True)
        att_scr[:, hs] = _bdot(p, mv_ref[0, :, hs])
    o_ref[...] = x + jnp.dot(att_scr[...].astype(BF16), wo_ref[...], preferred_element_type=F32)


def xattn_prompt(x, gain, w_q, mem_k, mem_v, w_o, layer, *, nb, tm=1024, pre=None):
    n = x.shape[0]
    tiles_per_b = n // nb // tm
    mem_k = mem_k.reshape(-1, N_MEM, D_MODEL)
    mem_v = mem_v.reshape(-1, N_MEM, D_MODEL)
    row = pl.BlockSpec((tm, D_MODEL), lambda i: (i, 0))
    wspec = pl.BlockSpec((D_MODEL, D_MODEL), lambda i: (0, 0))
    mspec = pl.BlockSpec((1, N_MEM, D_MODEL), lambda i: (layer * nb + i // tiles_per_b, 0, 0))
    args = [x, gain.reshape(1, D_MODEL), w_q, mem_k, mem_v, w_o]
    in_specs = [row, pl.BlockSpec((1, D_MODEL), lambda i: (0, 0)), wspec, mspec, mspec, wspec]
    if pre is not None:
        for y, wy in (pre[:2], pre[2:]):
            t_len, _, kw = y.shape
            args += [y.reshape(t_len, nb * kw), wy]
            in_specs += [_row_spec(tm, kw, (nb, tiles_per_b)), pl.BlockSpec(wy.shape, lambda i: (0, 0))]
    return pl.pallas_call(
        functools.partial(_xattn_prompt_kernel, pre=pre is not None), grid=(n // tm,),
        in_specs=in_specs, out_specs=row, out_shape=jax.ShapeDtypeStruct((n, D_MODEL), F32),
        scratch_shapes=[pltpu.VMEM((tm, D_MODEL), F32)],
        compiler_params=_cparams("parallel"), name="xattn_prompt")(*args)


def _xattn_step_kernel(q_ref, mk_ref, mv_ref, o_ref, *, tb):
    half = N_MEM // 2
    both = lambda z: jnp.concatenate([z, z], axis=1)
    fold = lambda z, op: op(z[:, :MEM_HEADS], z[:, MEM_HEADS:])
    for i in range(tb):
        k8 = jnp.concatenate([mk_ref[0, i, :half], mk_ref[0, i, half:]], axis=1)
        v8 = jnp.concatenate([mv_ref[0, i, :half], mv_ref[0, i, half:]], axis=1)
        q8 = jnp.concatenate([q_ref[i], q_ref[i]], axis=0)
        s = jnp.sum(k8 * q8[None], axis=-1, keepdims=True) * (MEM_HD ** -0.5)
        smax = both(fold(jnp.max(s, axis=0, keepdims=True), jnp.maximum))
        e = jnp.exp(s - smax)
        den = both(fold(jnp.sum(e, axis=0, keepdims=True), jnp.add))
        o8 = jnp.sum((e / den) * v8, axis=0)
        o_ref[i] = o8[:MEM_HEADS] + o8[MEM_HEADS:]


def xattn_step_job(q, cache_k, cache_v, layer, *, tb):
    n = q.shape[0]

    def specs(inner):
        qspec = pl.BlockSpec((tb, MEM_HEADS, MEM_HD), lambda i, j: (i * inner + j, 0, 0))
        cspec = pl.BlockSpec((1, tb, N_MEM, MEM_HEADS, MEM_HD), lambda i, j: (layer, i * inner + j, 0, 0, 0))
        return qspec, cspec

    return SideJob(
        body=functools.partial(_xattn_step_kernel, tb=tb),
        args=(q.reshape(n, MEM_HEADS, MEM_HD), cache_k, cache_v),
        in_specs=lambda inner: [specs(inner)[0], specs(inner)[1], specs(inner)[1]],
        out_shape=(jax.ShapeDtypeStruct((n, MEM_HEADS, MEM_HD), F32),),
        out_specs=lambda inner: [specs(inner)[0]],
        steps=n // tb)


def run_job(job, name):
    return pl.pallas_call(
        job.body, grid=(job.steps, 1), in_specs=job.in_specs(1), out_specs=job.out_specs(1),
        out_shape=list(job.out_shape), compiler_params=_cparams("parallel", "arbitrary"), name=name)(*job.args)


ROUTER_LANES = 128
NEG_BIG = -1e30


def _moe_gates(logits):
    lane = lax.broadcasted_iota(jnp.int32, logits.shape, 1)
    first = lambda mask: jnp.min(jnp.where(mask, lane, ROUTER_LANES), axis=-1, keepdims=True)
    is_c = lane < MOE_GROUPS
    lc = jnp.where(is_c, logits, NEG_BIG)
    mc = jnp.max(lc, axis=-1, keepdims=True)
    g_idx = first(lc == mc)
    p_g = 1.0 / jnp.sum(jnp.where(is_c, jnp.exp(lc - mc), 0.0), axis=-1, keepdims=True)
    fl = lane - MOE_GROUPS
    in_g = (fl >= 0) & (fl < MOE_EXPERTS) & ((fl // MOE_PER_GROUP) == g_idx)
    lf = jnp.where(in_g, logits, NEG_BIG)
    m1 = jnp.max(lf, axis=-1, keepdims=True)
    i1 = first(lf == m1)
    lf2 = jnp.where(lane == i1, NEG_BIG, lf)
    m2 = jnp.max(lf2, axis=-1, keepdims=True)
    i2 = first(lf2 == m2)
    e2 = jnp.exp(m2 - m1)
    w_top = 1.0 / (1.0 + e2)
    gate = p_g * (jnp.where(lane == i1, w_top, 0.0) + jnp.where(lane == i2, e2 * w_top, 0.0))
    return gate, g_idx


MOE_CAP = 320
MOE_EPS = 2
MOE_STEPS = MOE_EXPERTS // MOE_EPS

SideJob = collections.namedtuple("SideJob", "body args in_specs out_shape out_specs steps")


def merge_jobs(a, b):
    assert a.steps == b.steps
    na_in, na_out, n_in = len(a.args), len(a.out_shape), len(a.args) + len(b.args)

    def body(*refs):
        a.body(*refs[:na_in], *refs[n_in:n_in + na_out])
        b.body(*refs[na_in:n_in], *refs[n_in + na_out:])

    return SideJob(body, tuple(a.args) + tuple(b.args), lambda inner: a.in_specs(inner) + b.in_specs(inner),
                   tuple(a.out_shape) + tuple(b.out_shape),
                   lambda inner: a.out_specs(inner) + b.out_specs(inner), a.steps)


def cast_job(arrays, steps):
    views = tuple(a.reshape(steps, -1, a.shape[-1]) for a in arrays)

    def body(*refs):
        for src, dst in zip(refs[:len(views)], refs[len(views):]):
            dst[...] = src[...].astype(BF16)

    specs = lambda inner: [pl.BlockSpec((1,) + v.shape[1:], lambda i, j: (i * inner + j, 0, 0)) for v in views]
    return SideJob(body, views, specs, tuple(jax.ShapeDtypeStruct(v.shape, BF16) for v in views), specs, steps)


def _with_side(main_kernel, n_in, n_out, side):
    ns_in, ns_out = len(side.args), len(side.out_shape)

    def kern(*refs):
        m_in = refs[:n_in]
        s_in = refs[n_in:n_in + ns_in]
        m_out = refs[n_in + ns_in:n_in + ns_in + n_out]
        s_out = refs[n_in + ns_in + n_out:n_in + ns_in + n_out + ns_out]
        scratch = refs[n_in + ns_in + n_out + ns_out:]
        main_kernel(*m_in, *m_out, *scratch, side=lambda: side.body(*s_in, *s_out))

    return kern


def _router_logits(h, wr_ref, br_ref):
    h_hi = h.astype(BF16)
    h_lo = (h - h_hi.astype(F32)).astype(BF16)
    rows = h.shape[0]
    res = jnp.dot(jnp.concatenate([h_hi, h_lo], axis=0), wr_ref[...], preferred_element_type=F32)
    acc = (res[:rows, :ROUTER_LANES] + res[:rows, ROUTER_LANES:]) + (res[rows:, :ROUTER_LANES] + res[rows:, ROUTER_LANES:])
    return acc + br_ref[...]


def _experts_ffn(hb, gate, e0, w1_ref, w3_ref, w2_ref):
    lane = lax.broadcasted_iota(jnp.int32, gate.shape, 1)
    acc = None
    for e in range(MOE_EPS):
        a1 = jnp.dot(hb, w1_ref[e].astype(BF16), preferred_element_type=F32)
        a3 = jnp.dot(hb, w3_ref[e].astype(BF16), preferred_element_type=F32)
        ge = jnp.sum(jnp.where(lane == MOE_GROUPS + e0 + e, gate, 0.0), axis=-1, keepdims=True)
        hid = (jax.nn.silu(a1) * a3 * ge).astype(BF16)
        part = jnp.dot(hid, w2_ref[e].astype(BF16), preferred_element_type=F32)
        acc = part if acc is None else acc + part
    return acc


def _moe_kernel(x_ref, g_ref, wr_ref, br_ref, el_ref, w1_ref, w3_ref, w2_ref, *rest, final_norm, cap, side=None):
    if final_norm:
        fin_ref, o_ref = rest[:2]
        rest = rest[2:]
    else:
        o_ref = rest[0]
        rest = rest[1:]
    h_scr, oh_scr, rk_scr, ohr_scr, rkr_scr, hg_scr, gg_scr, yg_scr, cnt_smem = rest
    tm = x_ref.shape[0]
    step = pl.program_id(1)
    grp = step // (MOE_PER_GROUP // MOE_EPS)
    first_half = step % (MOE_PER_GROUP // MOE_EPS) == 0
    last_half = step % (MOE_PER_GROUP // MOE_EPS) == MOE_PER_GROUP // MOE_EPS - 1

    @pl.when(step == 0)
    def _():
        x = x_ref[...]
        h = _rms(x, g_ref[...])
        gate, g_idx = _moe_gates(_router_logits(h, wr_ref, br_ref))
        g_hi = gate.astype(BF16)
        h_scr[:, :D_MODEL] = h.astype(BF16)
        h_scr[:, D_MODEL:D_MODEL + ROUTER_LANES] = g_hi
        h_scr[:, D_MODEL + ROUTER_LANES:] = (gate - g_hi.astype(F32)).astype(BF16)
        o_ref[...] = x
        lane = lax.broadcasted_iota(jnp.int32, gate.shape, 1)
        onehot = jnp.where(lane == g_idx, 1.0, 0.0)
        rank = jnp.dot(el_ref[...], onehot.astype(BF16), preferred_element_type=F32)
        oh_scr[...] = onehot
        rk_scr[...] = rank
        ohr_scr[...] = onehot.T[:8]
        rkr_scr[...] = rank.T[:8]
        cnt = jnp.sum(onehot, axis=0, keepdims=True)
        for gi in range(MOE_GROUPS):
            cnt_smem[gi] = cnt[0, gi].astype(jnp.int32)

    lane = lax.broadcasted_iota(jnp.int32, (tm, ROUTER_LANES), 1)

    def gather_mat(base):
        slot = jnp.where(ohr_scr[pl.ds(grp, 1), :] > 0.5, rkr_scr[pl.ds(grp, 1), :] - base, -1.0)
        c = lax.broadcasted_iota(jnp.int32, (cap, tm), 0).astype(F32)
        return jnp.where(c == slot, 1.0, 0.0).astype(BF16)

    def scatter_mat(base):
        member = jnp.sum(jnp.where(lane == grp, oh_scr[...], 0.0), axis=-1, keepdims=True)
        rank = jnp.sum(jnp.where(lane == grp, rk_scr[...], 0.0), axis=-1, keepdims=True)
        slot = jnp.where(member > 0.5, rank - base, -1.0)
        c = lax.broadcasted_iota(jnp.int32, (tm, cap), 1).astype(F32)
        return jnp.where(c == slot, 1.0, 0.0).astype(BF16)

    def gather(base):
        got = jnp.dot(gather_mat(base), h_scr[...], preferred_element_type=F32)
        gg = got[:, D_MODEL:D_MODEL + ROUTER_LANES] + got[:, D_MODEL + ROUTER_LANES:]
        return got[:, :D_MODEL].astype(BF16), gg

    @pl.when(first_half)
    def _():
        hg, gg = gather(0.0)
        hg_scr[...] = hg
        gg_scr[...] = gg
        yg_scr[...] = jnp.zeros_like(yg_scr)

    yg_scr[...] += _experts_ffn(hg_scr[...], gg_scr[...], step * MOE_EPS, w1_ref, w3_ref, w2_ref)
    if side is not None:
        side()

    @pl.when(last_half)
    def _():
        o_ref[...] += jnp.dot(scatter_mat(0.0), yg_scr[...].astype(BF16), preferred_element_type=F32)

    def extra_round(r, carry):
        base = (r * cap).astype(F32)
        hg, gg = gather(base)
        y = _experts_ffn(hg, gg, step * MOE_EPS, w1_ref, w3_ref, w2_ref)
        o_ref[...] += jnp.dot(scatter_mat(base), y.astype(BF16), preferred_element_type=F32)
        return carry

    lax.fori_loop(1, (cnt_smem[grp] + cap - 1) // cap, extra_round, 0)

    if final_norm:
        @pl.when(step == MOE_STEPS - 1)
        def _():
            o_ref[...] = _rms(o_ref[...], fin_ref[...])


def moe_dense(x, gain, w_r, b_r, w1, w3, w2, layer, *, tm=512, cap=MOE_CAP, final_gain=None, side=None):
    n = x.shape[0]
    tm = min(tm, n)
    cap = min(cap, tm)
    gain = gain.reshape(1, D_MODEL)
    row = pl.BlockSpec((tm, D_MODEL), lambda i, s: (i, 0))
    const2 = lambda a: pl.BlockSpec(a.shape, lambda i, s: (0,) * a.ndim)
    soff = layer * MOE_STEPS
    wspec = pl.BlockSpec((MOE_EPS, D_MODEL, MOE_HIDDEN), lambda i, s: (soff + s, 0, 0))
    earlier = jnp.tril(jnp.ones((tm, tm), BF16), -1)
    args = [x, gain, w_r, b_r, earlier, w1, w3, w2]
    in_specs = [row, const2(gain), const2(w_r), const2(b_r), const2(earlier), wspec, wspec,
                pl.BlockSpec((MOE_EPS, MOE_HIDDEN, D_MODEL), lambda i, s: (soff + s, 0, 0))]
    if final_gain is not None:
        args.append(final_gain.reshape(1, D_MODEL))
        in_specs.append(const2(args[-1]))
    kern = functools.partial(_moe_kernel, final_norm=final_gain is not None, cap=cap)
    out_shape = [jax.ShapeDtypeStruct((n, D_MODEL), F32)]
    out_specs = [row]
    grid = (n // tm, MOE_STEPS)
    if side is not None:
        assert side.steps == grid[0] * grid[1]
        kern = _with_side(kern, len(args), 1, side)
        args += list(side.args)
        in_specs += side.in_specs(MOE_STEPS)
        out_shape += list(side.out_shape)
        out_specs += side.out_specs(MOE_STEPS)
    outs = pl.pallas_call(
        kern, grid=grid, in_specs=in_specs, out_specs=out_specs, out_shape=out_shape,
        scratch_shapes=[pltpu.VMEM((tm, D_MODEL + 2 * ROUTER_LANES), BF16),
                        pltpu.VMEM((tm, ROUTER_LANES), F32),
                        pltpu.VMEM((tm, ROUTER_LANES), F32),
                        pltpu.VMEM((8, tm), F32),
                        pltpu.VMEM((8, tm), F32),
                        pltpu.VMEM((cap, D_MODEL), BF16),
                        pltpu.VMEM((cap, ROUTER_LANES), F32),
                        pltpu.VMEM((cap, D_MODEL), F32),
                        pltpu.SMEM((MOE_GROUPS,), jnp.int32)],
        compiler_params=_cparams("parallel", "arbitrary"), name="moe")(*args)
    return outs[0] if side is None else (outs[0], outs[1:])


def _group_weights(w1, w3, w2):
    ne = w1.shape[0] * MOE_EXPERTS
    return (w1.reshape(ne, D_MODEL, MOE_HIDDEN), w3.reshape(ne, D_MODEL, MOE_HIDDEN),
            w2.reshape(ne, MOE_HIDDEN, D_MODEL))


def _router_params(w_rc, b_rc, w_rf, b_rf):
    pad = ROUTER_LANES - MOE_GROUPS - MOE_EXPERTS
    w_r = jnp.concatenate([w_rc, w_rf, jnp.zeros((D_MODEL, pad), F32)], axis=1).astype(F32)
    b_r = jnp.concatenate([b_rc, b_rf, jnp.zeros((pad,), F32)]).reshape(1, ROUTER_LANES).astype(F32)
    w_hi = w_r.astype(BF16)
    w_lo = (w_r - w_hi.astype(F32)).astype(BF16)
    return jnp.concatenate([w_hi, w_lo], axis=1), b_r


def _mem_kv_kernel(x_ref, g_ref, w_ref, kf_ref, vf_ref, kh_ref, vh_ref):
    h = _rms(x_ref[...], g_ref[0]).astype(BF16)
    for col, f_ref, h_ref in ((0, kf_ref, kh_ref), (D_MODEL, vf_ref, vh_ref)):
        acc = jnp.dot(h, w_ref[0, :, col:col + D_MODEL], preferred_element_type=F32)
        f_ref[0] = acc.astype(f_ref.dtype)
        for hd in range(MEM_HEADS):
            h_ref[0, :, hd, :] = acc[:, hd * MEM_HD:(hd + 1) * MEM_HD]


def mem_kv(mem, gains, w_kv, *, tm=512):
    rows = mem.shape[0]
    nl = w_kv.shape[0]
    flat = jax.ShapeDtypeStruct((nl, rows, D_MODEL), BF16)
    head = jax.ShapeDtypeStruct((nl, rows, MEM_HEADS, MEM_HD), F32)
    fspec = pl.BlockSpec((1, tm, D_MODEL), lambda l, i: (l, i, 0))
    hspec = pl.BlockSpec((1, tm, MEM_HEADS, MEM_HD), lambda l, i: (l, i, 0, 0))
    return pl.pallas_call(
        _mem_kv_kernel, grid=(nl, rows // tm),
        in_specs=[pl.BlockSpec((tm, D_MODEL), lambda l, i: (i, 0)),
                  pl.BlockSpec((1, 1, D_MODEL), lambda l, i: (l, 0, 0)),
                  pl.BlockSpec((1, D_MODEL, 2 * D_MODEL), lambda l, i: (l, 0, 0))],
        out_specs=(fspec, fspec, hspec, hspec), out_shape=(flat, flat, head, head),
        compiler_params=_cparams("parallel", "parallel"), name="mem_kv")(
            mem, gains.reshape(nl, 1, D_MODEL), w_kv)


def _forward(xp, xs, nbp, w, st, mem_k, mem_v, cache_k, cache_v):
    assert DEPTH == 2
    nbs = xs.shape[0]
    moe_tm = 1024
    moe_steps_p = (xp.shape[0] // moe_tm) * MOE_STEPS
    rwp = tuple(w[k][0] for k in ('rw_mu', 'rw_w0', 'rw_w2', 'rw_a0', 'rw_a2', 'rw_g2',
                                  'rw_k_k', 'rw_k_a', 'rw_r_k', 'rw_ln_w', 'rw_ln_b'))
    w_in0, w_out0 = w['w_in0_bf'][0], w['w_out0_bf'][0]

    u, p_s = linear(xs, w_in0, gain=w['norm_mix'][0], splits=(S5_WIDTH, RW_PROJ))
    y_s5, s5r_s, s5i_s = s5_mixer(u.reshape(1, nbs, S5_WIDTH), st['s5_re'], st['s5_im'], w['s5p'][0],
                                  w['s5_d'][0], w['s5_w_glu'][0], tc=1)
    y_rw, rw_s = rwkv_step(p_s, st['shift'], st['rwkv'], rwp)
    xs = linear(y_s5.reshape(nbs, S5_WIDTH), w_out0[:S5_WIDTH], x2=y_rw, w2=w_out0[S5_WIDTH:], residual=xs)
    q_s = linear(xs, w['w_mq_bf'][0], gain=w['norm_mem'][0])

    zeros = lambda *shape: jnp.zeros(shape, F32)
    u, p_p = linear(xp, w_in0, gain=w['norm_mix'][0], splits=(S5_WIDTH, RW_PROJ), out_tmajor=True, batch=nbp)
    y_s5, s5r_p, s5i_p = s5_mixer(u, zeros(nbp, S5_STATE), zeros(nbp, S5_STATE), w['s5p'][0],
                                  w['s5_d'][0], w['s5_w_glu'][0], tc=128)
    rw_bs = 4
    rw_steps = (nbp // rw_bs) * (p_p.shape[0] // RW_HD)
    later = (w['moe_w1'], w['moe_w3'], w['moe_w2'], w['w_in1'][0], w['w_out1'][0])
    job = merge_jobs(xattn_step_job(q_s, cache_k, cache_v, 0, tb=nbs // rw_steps), cast_job(later, rw_steps))
    (y_rw, rw_p, sh_p), (att_s, *cast) = rwkv_prompt(
        p_p, zeros(nbp, RW_PROJ), zeros(nbp, RW_HEADS, RW_HD, RW_HD), rwp, bs=rw_bs, side=job)
    moe_w = _group_weights(*[c.reshape(a.shape) for c, a in zip(cast[:3], later[:3])])
    w_in1, w_out1 = (c.reshape(a.shape) for c, a in zip(cast[3:], later[3:]))
    moe = lambda x, layer, **kw: moe_dense(x, w['norm_ffn'][layer], *w['router'][layer], *moe_w, layer, **kw)
    xp = xattn_prompt(xp, w['norm_mem'][0], w['w_mq_bf'][0], mem_k, mem_v, w['w_mo_bf'][0], 0, nb=nbp,
                      pre=(y_s5, w_out0[:S5_WIDTH], y_rw, w_out0[S5_WIDTH:]))

    xs = linear(att_s.reshape(nbs, D_MODEL), w['w_mo_bf'][0], residual=xs)
    xs = moe(xs, 0)
    q, k, v, g = linear(xs, w_in1, gain=w['norm_mix'][1], out_dtype=BF16, splits=(NQ, NQ, NV, NV))
    xp = moe(xp, 0, tm=moe_tm)
    ret_steps = xp.shape[0] // RET_CHUNK
    job = retention_step_job(q, k, v, g, st['ret'], pos0=float(PAST_LEN), tb=nbs // ret_steps)
    (xp, ret_p), (ret_s, y_ret) = retention_layer_prompt(xp, w['norm_mix'][1], w_in1, w_out1, nb=nbp, side=job)
    xs = linear(y_ret.reshape(nbs, NV), w_out1, residual=xs)
    q_s = linear(xs, w['w_mq_bf'][1], gain=w['norm_mem'][1])

    xp = xattn_prompt(xp, w['norm_mem'][1], w['w_mq_bf'][1], mem_k, mem_v, w['w_mo_bf'][1], 1, nb=nbp)
    job = xattn_step_job(q_s, cache_k, cache_v, 1, tb=nbs // moe_steps_p)
    y_p, (att_s,) = moe(xp, 1, tm=moe_tm, final_gain=w['norm_final'], side=job)
    xs = linear(att_s.reshape(nbs, D_MODEL), w['w_mo_bf'][1], residual=xs)
    y_s = moe(xs, 1, final_gain=w['norm_final'])

    grp = lambda z, nb: z.reshape(1, nb, S5_GROUPS, S5_N)
    prompt_out = (y_p, grp(s5r_p, nbp), grp(s5i_p, nbp), rw_p[None], sh_p[None], ret_p[None])
    sample_out = (y_s, grp(s5r_s, nbs), grp(s5i_s, nbs), rw_s[None], p_s[None], ret_s[None])
    return prompt_out, sample_out


def kernel(x_prompt, x_sample, mem_prompt, state_s5_re, state_s5_im, state_rwkv, state_shift, state_ret, cache_mem_k, cache_mem_v, norm_mix, norm_mem, norm_ffn, norm_final, w_in0, w_out0, s5_a_re, s5_a_im, s5_b_re, s5_b_im, s5_c_re, s5_c_im, s5_d, s5_log_dt, s5_w_glu, rw_mu, rw_w0, rw_w2, rw_a0, rw_a2, rw_g2, rw_k_k, rw_k_a, rw_r_k, rw_ln_w, rw_ln_b, w_in1, w_out1, mem_norm, w_mq, w_mk, w_mv, w_mo, moe_w_rc, moe_b_rc, moe_w_rf, moe_b_rf, moe_w1, moe_w3, moe_w2):
    w = dict(norm_mix=norm_mix, norm_mem=norm_mem, norm_ffn=norm_ffn, norm_final=norm_final,
             w_in0=w_in0, w_out0=w_out0, s5_a_re=s5_a_re, s5_a_im=s5_a_im, s5_b_re=s5_b_re, s5_b_im=s5_b_im,
             s5_c_re=s5_c_re, s5_c_im=s5_c_im, s5_d=s5_d, s5_log_dt=s5_log_dt, s5_w_glu=s5_w_glu,
             rw_mu=rw_mu, rw_w0=rw_w0, rw_w2=rw_w2, rw_a0=rw_a0, rw_a2=rw_a2, rw_g2=rw_g2,
             rw_k_k=rw_k_k, rw_k_a=rw_k_a, rw_r_k=rw_r_k, rw_ln_w=rw_ln_w, rw_ln_b=rw_ln_b,
             w_in1=w_in1, w_out1=w_out1, w_mq=w_mq, w_mo=w_mo,
             moe_w_rc=moe_w_rc, moe_b_rc=moe_b_rc, moe_w_rf=moe_w_rf, moe_b_rf=moe_b_rf,
             moe_w1=moe_w1, moe_w3=moe_w3, moe_w2=moe_w2)
    nbp, t_len, _ = x_prompt.shape
    nbs = x_sample.shape[0]
    n_even, n_odd = state_s5_re.shape[0], state_ret.shape[0]
    for name in ('w_in0', 'w_out0', 'w_mq', 'w_mo'):
        w[name + '_bf'] = w[name].astype(BF16)
    w['s5p'] = [_s5_params(s5_a_re[i], s5_a_im[i], s5_b_re[i], s5_b_im[i], s5_c_re[i], s5_c_im[i], s5_log_dt[i])
                for i in range(n_even)]
    w['router'] = [_router_params(moe_w_rc[l], moe_b_rc[l], moe_w_rf[l], moe_b_rf[l]) for l in range(DEPTH)]

    mem = mem_prompt.reshape(nbp * N_MEM, D_MODEL)
    w_kv = jnp.concatenate([w_mk, w_mv], axis=2).astype(BF16)
    mk, mv, mk_h, mv_h = mem_kv(mem, mem_norm, w_kv)
    mem_k_l = mk.reshape(DEPTH, nbp, N_MEM, D_MODEL)
    mem_v_l = mv.reshape(DEPTH, nbp, N_MEM, D_MODEL)
    mem_k_p = mk_h.reshape(DEPTH, nbp, N_MEM, MEM_HEADS, MEM_HD)
    mem_v_p = mv_h.reshape(DEPTH, nbp, N_MEM, MEM_HEADS, MEM_HD)

    assert n_even == 1 and n_odd == 1
    st = dict(s5_re=state_s5_re.reshape(nbs, S5_STATE), s5_im=state_s5_im.reshape(nbs, S5_STATE),
              rwkv=state_rwkv[0], shift=state_shift[0], ret=state_ret[0])
    (y_p, s5r_p, s5i_p, rw_p, sh_p, ret_p), (y_s, s5r_s, s5i_s, rw_s, sh_s, ret_s) = _forward(
        x_prompt.reshape(nbp * t_len, D_MODEL), x_sample.reshape(nbs, D_MODEL), nbp, w, st,
        mem_k_l, mem_v_l, cache_mem_k, cache_mem_v)
    return (y_p.reshape(nbp, t_len, D_MODEL), y_s.reshape(nbs, 1, D_MODEL),
            s5r_p, s5i_p, rw_p, sh_p, ret_p, mem_k_p, mem_v_p, s5r_s, s5i_s, rw_s, sh_s, ret_s)
```

```python
import collections
import functools
import math

import jax
import jax.numpy as jnp
from jax import lax
from jax.experimental import pallas as pl
from jax.experimental.pallas import tpu as pltpu

F32 = jnp.float32
BF16 = jnp.bfloat16

D_MODEL = 1024
DEPTH = 2
PAST_LEN = 16384
S5_WIDTH = 512
S5_GROUP = 16
S5_GROUPS = 32
S5_N = 64
S5_STATE = S5_GROUPS * S5_N
S5_GBLK = 8
RW_WIDTH = 512
RW_HD = 64
RW_HEADS = 8
RW_LORA = 256
RW_PROJ = 3 * RW_WIDTH + RW_LORA
IN0 = S5_WIDTH + RW_PROJ
RET_DK = 256
RET_HEADS = 4
RET_DV = 512
RET_CHUNK = 256
NQ = RET_HEADS * RET_DK
NV = RET_HEADS * RET_DV
IN1 = 2 * NQ + 2 * NV
N_MEM = 256
MEM_HEADS = 4
MEM_HD = 256
MOE_GROUPS = 4
MOE_PER_GROUP = 4
MOE_EXPERTS = 16
MOE_HIDDEN = 256
NORM_EPS = 1e-6
RW_GN_EPS = 64e-5
ROPE_BASE = 10000.0

VMEM_LIMIT = 56 * 1024 * 1024


def _cparams(*sem):
    return pltpu.CompilerParams(dimension_semantics=sem, vmem_limit_bytes=VMEM_LIMIT)


def _bdot(a, b):
    return jnp.dot(a.astype(BF16), b.astype(BF16), preferred_element_type=F32)


def _dot_nt(a, b):
    return lax.dot_general(a.astype(BF16), b.astype(BF16), (((1,), (1,)), ((), ())),
                           preferred_element_type=F32)


def _dot_tn(a, b):
    return lax.dot_general(a.astype(BF16), b.astype(BF16), (((0,), (0,)), ((), ())),
                           preferred_element_type=F32)


def _split3(x):
    hi = x.astype(BF16)
    r1 = x - hi.astype(F32)
    mid = r1.astype(BF16)
    lo = (r1 - mid.astype(F32)).astype(BF16)
    return hi, mid, lo


def _dot_exact_rhs(x, m_bf16, passes=3):
    hi, mid, lo = _split3(x)
    acc = jnp.dot(hi, m_bf16, preferred_element_type=F32)
    if passes > 1:
        acc = acc + jnp.dot(mid, m_bf16, preferred_element_type=F32)
    if passes > 2:
        acc = acc + jnp.dot(lo, m_bf16, preferred_element_type=F32)
    return acc


def _rms(x, g):
    ms = jnp.mean(x * x, axis=-1, keepdims=True)
    return x * lax.rsqrt(ms + NORM_EPS) * g


def _linear_kernel(*refs, norm, two, res):
    it = iter(refs)
    x_ref = next(it)
    g_ref = next(it) if norm else None
    w_ref = next(it)
    x2_ref = next(it) if two else None
    w2_ref = next(it) if two else None
    r_ref = next(it) if res else None
    o_refs = list(it)
    x = x_ref[...].astype(F32)
    if norm:
        x = _rms(x, g_ref[...])
    xb = x.astype(BF16)
    x2b = x2_ref[...].astype(BF16) if two else None
    col = 0
    for o_ref in o_refs:
        m = o_ref.shape[-1]
        step = next((s for s in (512, 256) if m % s == 0), m)
        for j in range(m // step):
            sl = slice(col + j * step, col + (j + 1) * step)
            acc = jnp.dot(xb, w_ref[:, sl], preferred_element_type=F32)
            if two:
                acc = acc + jnp.dot(x2b, w2_ref[:, sl], preferred_element_type=F32)
            if res:
                acc = acc + r_ref[:, sl]
            o_ref[:, j * step:(j + 1) * step] = acc.astype(o_ref.dtype)
        col += m


def _row_spec(tm, width, tmajor_b):
    if tmajor_b is None:
        return pl.BlockSpec((tm, width), lambda i: (i, 0))
    nb, tiles_per_b = tmajor_b
    return pl.BlockSpec((tm, width), lambda i: (i % tiles_per_b, i // tiles_per_b))


def linear(x, w, *, gain=None, x2=None, w2=None, residual=None, out_dtype=F32, tm=512,
           out_tmajor=False, batch=None, splits=None, name="linear"):
    n, k = x.shape
    nb = batch
    t_len = n // nb if nb else None
    m = w.shape[1]
    tm = min(tm, t_len if out_tmajor else n)
    assert n % tm == 0
    tiles_per_b = (t_len // tm) if out_tmajor else None
    rows = lambda a: pl.BlockSpec((tm, a.shape[-1]), lambda i: (i, 0))
    args, specs = [x], [rows(x)]
    if gain is not None:
        args.append(gain.reshape(1, k).astype(F32))
        specs.append(pl.BlockSpec((1, k), lambda i: (0, 0)))
    args.append(w)
    specs.append(pl.BlockSpec(w.shape, lambda i: (0, 0)))
    if x2 is not None:
        args += [x2, w2]
        specs += [rows(x2), pl.BlockSpec(w2.shape, lambda i: (0, 0))]
    if residual is not None:
        args.append(residual)
        specs.append(rows(residual))
    widths = tuple(splits) if splits else (m,)
    assert sum(widths) == m
    if out_tmajor:
        out_shape = [jax.ShapeDtypeStruct((t_len, nb * mw), out_dtype) for mw in widths]
    else:
        out_shape = [jax.ShapeDtypeStruct((n, mw), out_dtype) for mw in widths]
    out_specs = [_row_spec(tm, mw, (nb, tiles_per_b) if out_tmajor else None) for mw in widths]
    kern = functools.partial(_linear_kernel, norm=gain is not None, two=x2 is not None,
                             res=residual is not None)
    outs = pl.pallas_call(
        kern, grid=(n // tm,), in_specs=specs, out_specs=out_specs, out_shape=out_shape,
        compiler_params=_cparams("parallel"), name=name)(*args)
    if out_tmajor:
        outs = [o.reshape(t_len, nb, mw) for o, mw in zip(outs, widths)]
    return outs if splits else outs[0]


def _s5_kernel(u_ref, h_re_ref, h_im_ref, abar_re_ref, abar_im_ref, bb_re_ref, bb_im_ref,
               cc_re_ref, cc_im_ref, d_ref, wglu_ref, y_ref, s_re_ref, s_im_ref,
               x_re, x_im, st_re, st_im, il_scr, *, tc, nb, flat):
    c = pl.program_id(0)
    nlb = S5_WIDTH // 128
    rows = tc * nb
    nblk = S5_GROUPS // S5_GBLK
    bw_in = S5_GBLK * S5_GROUP
    bw_st = S5_GBLK * S5_N

    @pl.when(c == 0)
    def _():
        st_re[...] = h_re_ref[...]
        st_im[...] = h_im_ref[...]

    if flat:
        for b in range(nb):
            for j in range(nlb):
                il_scr[j, pl.ds(b, tc, stride=nb), :] = u_ref[:, b * S5_WIDTH + j * 128:b * S5_WIDTH + (j + 1) * 128]
        u = jnp.concatenate([il_scr[j] for j in range(nlb)], axis=-1)
    else:
        u = u_ref[...].reshape(rows, S5_WIDTH)
    ub = u.astype(BF16)
    for gb in range(nblk):
        ui = ub[:, gb * bw_in:(gb + 1) * bw_in]
        x_re[:, gb * bw_st:(gb + 1) * bw_st] = jnp.dot(ui, bb_re_ref[gb], preferred_element_type=F32)
        x_im[:, gb * bw_st:(gb + 1) * bw_st] = jnp.dot(ui, bb_im_ref[gb], preferred_element_type=F32)

    lane_blk = 1024
    for lb in range(S5_STATE // lane_blk):
        sl = slice(lb * lane_blk, (lb + 1) * lane_blk)
        ar = jnp.broadcast_to(abar_re_ref[:, sl], (nb, lane_blk))
        ai = jnp.broadcast_to(abar_im_ref[:, sl], (nb, lane_blk))

        def body(t, carry, sl=sl, ar=ar, ai=ai):
            xr, xi = carry
            r0 = pl.multiple_of(t * nb, nb)
            br = x_re[pl.ds(r0, nb), sl]
            bi = x_im[pl.ds(r0, nb), sl]
            nr = ar * xr - ai * xi + br
            ni = ar * xi + ai * xr + bi
            x_re[pl.ds(r0, nb), sl] = nr
            x_im[pl.ds(r0, nb), sl] = ni
            return nr, ni

        fr, fi = lax.fori_loop(0, tc, body, (st_re[:, sl], st_im[:, sl]), unroll=min(tc, 4))
        st_re[:, sl] = fr
        st_im[:, sl] = fi

    for gb in range(nblk):
        xr = x_re[:, gb * bw_st:(gb + 1) * bw_st].astype(BF16)
        xi = x_im[:, gb * bw_st:(gb + 1) * bw_st].astype(BF16)
        yb = (jnp.dot(xr, cc_re_ref[gb], preferred_element_type=F32)
              - jnp.dot(xi, cc_im_ref[gb], preferred_element_type=F32))
        cs = slice(gb * bw_in, (gb + 1) * bw_in)
        yb = yb + d_ref[:, cs] * u[:, cs]
        x_re[:, cs] = jax.nn.gelu(yb)
    y = x_re[:, :S5_WIDTH]
    y = y * jax.nn.sigmoid(jnp.dot(y.astype(BF16), wglu_ref[...], preferred_element_type=F32))
    if flat:
        for j in range(nlb):
            il_scr[j] = y[:, j * 128:(j + 1) * 128]
        for b in range(nb):
            for j in range(nlb):
                y_ref[:, b * S5_WIDTH + j * 128:b * S5_WIDTH + (j + 1) * 128] = (
                    il_scr[j, pl.ds(b, tc, stride=nb), :].astype(y_ref.dtype))
    else:
        y_ref[...] = y.reshape(y_ref.shape).astype(y_ref.dtype)

    @pl.when(c == pl.num_programs(0) - 1)
    def _():
        s_re_ref[...] = st_re[...]
        s_im_ref[...] = st_im[...]


def _s5_params(a_re, a_im, b_re, b_im, c_re, c_im, log_dt):
    dt = jnp.exp(log_dt.astype(F32))[:, None]
    ar, ai = a_re.astype(F32), a_im.astype(F32)
    mag = jnp.exp(dt * ar)
    abar_re, abar_im = mag * jnp.cos(dt * ai), mag * jnp.sin(dt * ai)
    den = ar * ar + ai * ai
    nr = abar_re - 1.0
    coef_re = (nr * ar + abar_im * ai) / den
    coef_im = (abar_im * ar - nr * ai) / den
    cr, ci = coef_re[..., None], coef_im[..., None]
    brf, bif = b_re.astype(F32), b_im.astype(F32)
    bb_re = cr * brf - ci * bif
    bb_im = cr * bif + ci * brf
    nblk = S5_GROUPS // S5_GBLK
    eye = jnp.eye(S5_GBLK, dtype=F32)

    def blockdiag_in(bb):
        t = jnp.transpose(bb, (0, 2, 1)).reshape(nblk, S5_GBLK, S5_GROUP, S5_N)
        m = jnp.einsum('kgcn,gh->kgchn', t, eye)
        return m.reshape(nblk, S5_GBLK * S5_GROUP, S5_GBLK * S5_N).astype(BF16)

    def blockdiag_out(cc):
        t = jnp.transpose(cc.astype(F32), (0, 2, 1)).reshape(nblk, S5_GBLK, S5_N, S5_GROUP)
        m = jnp.einsum('khnc,hg->khngc', t, eye)
        return m.reshape(nblk, S5_GBLK * S5_N, S5_GBLK * S5_GROUP).astype(BF16)

    return (abar_re.reshape(1, S5_STATE), abar_im.reshape(1, S5_STATE),
            blockdiag_in(bb_re), blockdiag_in(bb_im), blockdiag_out(c_re), blockdiag_out(c_im))


def s5_mixer(u_tm, h_re, h_im, params, d_skip, w_glu, *, tc):
    t_len, nb, _ = u_tm.shape
    abar_re, abar_im, bb_re, bb_im, cc_re, cc_im = params
    tc = min(tc, t_len)
    assert t_len % tc == 0 and nb % 8 == 0
    rows = tc * nb
    flat = t_len > 1
    full = lambda a: pl.BlockSpec(a.shape, lambda c: (0,) * a.ndim)
    if flat:
        u_arg = u_tm.reshape(t_len, nb * S5_WIDTH)
        io_spec = pl.BlockSpec((tc, nb * S5_WIDTH), lambda c: (c, 0))
        y_shape = jax.ShapeDtypeStruct((t_len, nb * S5_WIDTH), BF16)
    else:
        u_arg = u_tm
        io_spec = pl.BlockSpec((tc, nb, S5_WIDTH), lambda c: (c, 0, 0))
        y_shape = jax.ShapeDtypeStruct((t_len, nb, S5_WIDTH), BF16)
    args = (u_arg, h_re, h_im, abar_re, abar_im, bb_re, bb_im, cc_re, cc_im,
            d_skip.reshape(1, S5_WIDTH).astype(F32), w_glu.astype(BF16))
    in_specs = [io_spec] + [full(a) for a in args[1:]]
    st_shape = jax.ShapeDtypeStruct((nb, S5_STATE), F32)
    st_spec = pl.BlockSpec((nb, S5_STATE), lambda c: (0, 0))
    scratch = [pltpu.VMEM((rows, S5_STATE), F32), pltpu.VMEM((rows, S5_STATE), F32),
               pltpu.VMEM((nb, S5_STATE), F32), pltpu.VMEM((nb, S5_STATE), F32),
               pltpu.VMEM((S5_WIDTH // 128, rows if flat else 8, 128), F32)]
    y, s_re, s_im = pl.pallas_call(
        functools.partial(_s5_kernel, tc=tc, nb=nb, flat=flat), grid=(t_len // tc,), in_specs=in_specs,
        out_specs=(io_spec, st_spec, st_spec), out_shape=(y_shape, st_shape, st_shape),
        scratch_shapes=scratch, compiler_params=_cparams("arbitrary"), name="s5_mixer")(*args)
    return y.reshape(t_len, nb, S5_WIDTH), s_re, s_im


def _head_ones():
    i = lax.broadcasted_iota(jnp.int32, (RW_WIDTH, RW_WIDTH), 0) // RW_HD
    j = lax.broadcasted_iota(jnp.int32, (RW_WIDTH, RW_WIDTH), 1) // RW_HD
    return jnp.where(i == j, 1.0, 0.0).astype(BF16)


def _softplus(z):
    return jnp.maximum(z, 0.0) + jnp.log1p(jnp.exp(-jnp.abs(z)))


def _rw_prep(p, p_prev, prm, ones_bd):
    mu, w0, w2, a0, a2, g2, k_k, k_a = prm
    xm = p + (p_prev - p) * mu
    o1, o2, o3 = RW_WIDTH, 2 * RW_WIDTH, 3 * RW_WIDTH
    r, k, v = xm[:, :o1], xm[:, o1:o2], xm[:, o2:o3]
    wd, ad, gd = xm[:, o3:o3 + 64], xm[:, o3 + 64:o3 + 128], xm[:, o3 + 128:]
    w = -_softplus(-(w0 + _bdot(jnp.tanh(wd), w2))) - 0.5
    lw = -jnp.exp(w)
    a = jax.nn.sigmoid(a0 + _bdot(ad, a2))
    g = _bdot(jax.nn.sigmoid(gd), g2)
    kk = k * k_k
    ss = _dot_exact_rhs(kk * kk, ones_bd, passes=1)
    kk = kk / jnp.maximum(jnp.sqrt(ss), 1e-12)
    k = k * (1.0 + (a - 1.0) * k_a)
    return r, lw, k, v, -kk, kk * a, g


def _rw_post(o, r, k, v, g, r_k, ln_w, ln_b, ones_bd):
    inv = 1.0 / RW_HD
    mean = _dot_exact_rhs(o, ones_bd, passes=2) * inv
    d = o - mean
    var = _dot_exact_rhs(d * d, ones_bd, passes=1) * inv
    on = d * lax.rsqrt(var + RW_GN_EPS) * ln_w + ln_b
    bonus = _dot_exact_rhs(r * k * r_k, ones_bd, passes=1) * v
    return (on + bonus) * g


def _rw_chunk_kernel(p_ref, shift_ref, h0_ref, mu_ref, w0_ref, w2_ref, a0_ref, a2_ref, g2_ref,
                     kk_ref, ka_ref, rk_ref, lnw_ref, lnb_ref,
                     y_ref, hfin_ref, shout_ref, prev_scr, h_scr, o_scr, *, c_len, bs, side=None):
    c = pl.program_id(1)
    nc = pl.num_programs(1)
    cl = c_len

    @pl.when(c == 0)
    def _():
        prev_scr[...] = shift_ref[:, 0, :]
        h_scr[...] = h0_ref[...]

    ones_bd = _head_ones()
    row = lax.broadcasted_iota(jnp.int32, (cl, RW_PROJ), 0)
    ps, pprevs = [], []
    for bi in range(bs):
        p = p_ref[:, bi * RW_PROJ:(bi + 1) * RW_PROJ]
        pprevs.append(jnp.where(row == 0, prev_scr[bi:bi + 1, :], pltpu.roll(p, 1, 0)))
        ps.append(p)
    p_all = jnp.concatenate(ps, axis=0) if bs > 1 else ps[0]
    pprev_all = jnp.concatenate(pprevs, axis=0) if bs > 1 else pprevs[0]
    prm = (mu_ref[...], w0_ref[...], w2_ref[...], a0_ref[...], a2_ref[...], g2_ref[...],
           kk_ref[...], ka_ref[...])
    r, lw, k, v, a, b, g = _rw_prep(p_all, pprev_all, prm, ones_bd)

    ti = lax.broadcasted_iota(jnp.int32, (cl, cl), 0)
    si = lax.broadcasted_iota(jnp.int32, (cl, cl), 1)
    lmat = jnp.where(ti >= si, 1.0, 0.0).astype(BF16)
    eye = jnp.where(ti == si, 1.0, 0.0)
    mi = lax.broadcasted_iota(jnp.int32, (2 * cl, 3 * cl), 0)
    mj = lax.broadcasted_iota(jnp.int32, (2 * cl, 3 * cl), 1)
    t_row = jnp.where(mi >= cl, mi - cl, mi)
    s_col = jnp.where(mj < cl, mj, jnp.where(mj >= 2 * cl, mj - 2 * cl, -4 * cl))
    keep = (t_row - s_col) >= jnp.where(mi >= cl, 0, 1)
    eye_bf = eye.astype(BF16)

    lhs_l, rhs_l, vh_l, hcat_l, kb_l, etot_l = [], [], [], [], [], []
    for bi in range(bs):
        rs = slice(bi * cl, (bi + 1) * cl)
        lw_b = lw[rs]
        l_hi, l_mid, l_lo = _split3(lw_b)
        cum = (jnp.dot(lmat, l_hi, preferred_element_type=F32)
               + jnp.dot(lmat, l_mid, preferred_element_type=F32)
               + jnp.dot(lmat, l_lo, preferred_element_type=F32))
        tot = cum[cl - 1:cl, :]
        e_neg = jnp.exp(-cum)
        e_rem = jnp.exp(tot - cum)
        at = (a[rs] * jnp.exp(cum - lw_b)).astype(BF16)
        rt = (r[rs] * jnp.exp(cum)).astype(BF16)
        bt = (b[rs] * e_neg).astype(BF16)
        kt = (k[rs] * e_neg).astype(BF16)
        bh = (b[rs] * e_rem).astype(BF16)
        kh = (k[rs] * e_rem).astype(BF16)
        e_tot = jnp.exp(tot)
        vb = v[rs].astype(BF16)
        for h in range(RW_HEADS):
            hs = slice(h * RW_HD, (h + 1) * RW_HD)
            lhs_l.append(jnp.concatenate([at[:, hs], rt[:, hs]], axis=0))
            rhs_l.append(jnp.concatenate([kt[:, hs], eye_bf, bt[:, hs]], axis=0))
            vh_l.append(vb[:, hs])
            kb_l.append(jnp.concatenate([kh[:, hs], bh[:, hs]], axis=0))
            etot_l.append(jnp.sum(eye * e_tot[:, hs], axis=-1, keepdims=True))
            hcat_l.append(h_scr[bi, h])

    nitem = bs * RW_HEADS
    items = range(nitem)
    aa_l = [jnp.where(keep, _dot_nt(lhs_l[i], rhs_l[i]), 0.0).astype(BF16) for i in items]
    pw_l = [aa_l[i][:cl, 2 * cl:] for i in items]
    tinv_l = [eye_bf + pw_l[i] for i in items]
    for _ in range(int(math.log2(cl)) - 1):
        pw_l = [jnp.dot(pw_l[i], pw_l[i], preferred_element_type=F32).astype(BF16) for i in items]
        tinv_l = [jnp.dot(tinv_l[i], eye_bf + pw_l[i], preferred_element_type=F32).astype(BF16) for i in items]
    vh_cat = [jnp.concatenate([vh_l[i], hcat_l[i].astype(BF16)], axis=0) for i in items]
    x1_l = [jnp.dot(aa_l[i][:cl, :2 * cl], vh_cat[i], preferred_element_type=F32).astype(BF16) for i in items]
    u_l = [jnp.dot(tinv_l[i], x1_l[i], preferred_element_type=F32).astype(BF16) for i in items]
    o_l = [jnp.dot(aa_l[i][cl:, :], jnp.concatenate([vh_cat[i], u_l[i]], axis=0),
                   preferred_element_type=F32) for i in items]
    hn_l = [hcat_l[i] * etot_l[i]
            + lax.dot_general(kb_l[i], jnp.concatenate([vh_l[i], u_l[i]], axis=0), (((0,), (0,)), ((), ())),
                              preferred_element_type=F32) for i in items]

    for bi in range(bs):
        for h in range(RW_HEADS):
            i = bi * RW_HEADS + h
            o_scr[bi * cl:(bi + 1) * cl, h * RW_HD:(h + 1) * RW_HD] = o_l[i]
            h_scr[bi, h] = hn_l[i]
        prev_scr[bi:bi + 1, :] = ps[bi][cl - 1:cl, :]

    y = _rw_post(o_scr[...], r, k, v, g, rk_ref[...], lnw_ref[...], lnb_ref[...], ones_bd)
    for bi in range(bs):
        y_ref[:, bi * RW_WIDTH:(bi + 1) * RW_WIDTH] = y[bi * cl:(bi + 1) * cl].astype(y_ref.dtype)
    if side is not None:
        side()

    @pl.when(c == nc - 1)
    def _():
        hfin_ref[...] = h_scr[...]
        for bi in range(bs):
            shout_ref[bi] = ps[bi][cl - 1:cl, :]


def _rw_param_args(mu, w0, w2, a0, a2, g2, k_k, k_a, r_k, ln_w, ln_b):
    row = lambda z: z.reshape(1, -1).astype(F32)
    return (row(mu), row(w0), w2.astype(BF16), row(a0), a2.astype(BF16), g2.astype(BF16),
            row(k_k), row(k_a), row(r_k), row(ln_w), row(ln_b))


def rwkv_prompt(p_tm, shift, s0, params, *, bs=4, side=None):
    c_len = RW_HD
    t_len, nb, _ = p_tm.shape
    assert t_len % c_len == 0 and nb % bs == 0
    prm = _rw_param_args(*params)
    const = lambda a: pl.BlockSpec(a.shape, lambda b, c: (0,) * a.ndim)
    st_spec = pl.BlockSpec((bs, RW_HEADS, RW_HD, RW_HD), lambda b, c: (b, 0, 0, 0))
    sh_spec = pl.BlockSpec((bs, 1, RW_PROJ), lambda b, c: (b, 0, 0))
    in_specs = [pl.BlockSpec((c_len, bs * RW_PROJ), lambda b, c: (c, b)), sh_spec, st_spec] + [const(a) for a in prm]
    out_shape = (jax.ShapeDtypeStruct((t_len, nb * RW_WIDTH), BF16),
                 jax.ShapeDtypeStruct((nb, RW_HEADS, RW_HD, RW_HD), F32),
                 jax.ShapeDtypeStruct((nb, 1, RW_PROJ), F32))
    out_specs = (pl.BlockSpec((c_len, bs * RW_WIDTH), lambda b, c: (c, b)), st_spec, sh_spec)
    scratch = [pltpu.VMEM((bs, RW_PROJ), F32), pltpu.VMEM((bs, RW_HEADS, RW_HD, RW_HD), F32),
               pltpu.VMEM((bs * c_len, RW_WIDTH), F32)]
    h0 = jnp.swapaxes(s0, -1, -2)
    args = [p_tm.reshape(t_len, nb * RW_PROJ), shift.reshape(nb, 1, RW_PROJ), h0, *prm]
    kern = functools.partial(_rw_chunk_kernel, c_len=c_len, bs=bs)
    grid = (nb // bs, t_len // c_len)
    out_shape, out_specs = list(out_shape), list(out_specs)
    if side is not None:
        assert side.steps == grid[0] * grid[1]
        kern = _with_side(kern, len(args), 3, side)
        args += list(side.args)
        in_specs += side.in_specs(grid[1])
        out_shape += list(side.out_shape)
        out_specs += side.out_specs(grid[1])
    outs = pl.pallas_call(
        kern, grid=grid, in_specs=in_specs, out_specs=out_specs, out_shape=out_shape,
        scratch_shapes=scratch, compiler_params=_cparams("parallel", "arbitrary"), name="rwkv_prompt")(*args)
    y, h_fin, sh = outs[:3]
    res = (y.reshape(t_len, nb, RW_WIDTH), jnp.swapaxes(h_fin, -1, -2), sh.reshape(nb, RW_PROJ))
    return res if side is None else (res, outs[3:])


def _rw_step_prep_kernel(p_ref, shift_ref, mu_ref, w0_ref, w2_ref, a0_ref, a2_ref, g2_ref, kk_ref, ka_ref,
                         r_ref, k_ref, v_ref, g_ref, rt_ref, wt_ref, kt_ref, at_ref, bt_ref, vt_ref):
    prm = (mu_ref[...], w0_ref[...], w2_ref[...], a0_ref[...], a2_ref[...], g2_ref[...],
           kk_ref[...], ka_ref[...])
    r, lw, k, v, a, b, g = _rw_prep(p_ref[...], shift_ref[...], prm, _head_ones())
    r_ref[...] = r
    k_ref[...] = k
    v_ref[...] = v
    g_ref[...] = g
    rt_ref[...] = r.T
    wt_ref[...] = jnp.exp(lw).T
    kt_ref[...] = k.T
    at_ref[...] = a.T
    bt_ref[...] = b.T
    vt_ref[...] = v.T


def _rw_step_core_kernel(s_ref, r_ref, w_ref, k_ref, a_ref, b_ref, v_ref, s_out_ref, o_ref):
    r, w, k, a, b = r_ref[0], w_ref[0], k_ref[0], a_ref[0], b_ref[0]
    for j in range(s_ref.shape[1]):
        s = s_ref[0, j]
        sa = jnp.sum(s * a, axis=0, keepdims=True)
        s_new = s * w + sa * b + v_ref[0, j:j + 1, :] * k
        s_out_ref[0, j] = s_new
        o_ref[0, j:j + 1, :] = jnp.sum(s_new * r, axis=0, keepdims=True)


def _rw_step_post_kernel(ot_ref, r_ref, k_ref, v_ref, g_ref, rk_ref, lnw_ref, lnb_ref, y_ref):
    y_ref[...] = _rw_post(ot_ref[...].T, r_ref[...], k_ref[...], v_ref[...], g_ref[...],
                          rk_ref[...], lnw_ref[...], lnb_ref[...], _head_ones()).astype(y_ref.dtype)


def rwkv_step(p, shift, s0, params, *, vb=32):
    n = p.shape[0]
    prm = _rw_param_args(*params)
    vec = jax.ShapeDtypeStruct((n, RW_WIDTH), F32)
    vec_t = jax.ShapeDtypeStruct((RW_WIDTH, n), F32)
    r, k, v, g, rt, wt, kt, at, bt, vt = pl.pallas_call(
        _rw_step_prep_kernel, out_shape=(vec,) * 4 + (vec_t,) * 6, name="rwkv_step_prep")(p, shift, *prm[:8])
    heads = lambda z: z.reshape(RW_HEADS, RW_HD, n)
    k_spec = pl.BlockSpec((1, RW_HD, n), lambda h, j: (h, 0, 0))
    v_spec = pl.BlockSpec((1, vb, n), lambda h, j: (h, j, 0))
    st_spec = pl.BlockSpec((1, vb, RW_HD, n), lambda h, j: (h, j, 0, 0))
    st = jnp.transpose(s0, (1, 2, 3, 0))
    s_new, ot = pl.pallas_call(
        _rw_step_core_kernel, grid=(RW_HEADS, RW_HD // vb),
        in_specs=[st_spec] + [k_spec] * 5 + [v_spec], out_specs=(st_spec, v_spec),
        out_shape=(jax.ShapeDtypeStruct(st.shape, F32), jax.ShapeDtypeStruct((RW_HEADS, RW_HD, n), F32)),
        compiler_params=_cparams("parallel", "parallel"), name="rwkv_step_core")(
            st, heads(rt), heads(wt), heads(kt), heads(at), heads(bt), heads(vt))
    y = pl.pallas_call(
        _rw_step_post_kernel, out_shape=jax.ShapeDtypeStruct((n, RW_WIDTH), BF16), name="rwkv_step_post")(
            ot.reshape(RW_WIDTH, n), r, k, v, g, *prm[8:])
    return y, jnp.transpose(s_new, (3, 0, 1, 2))


RET_LOG_G = tuple(math.log(1.0 - 2.0 ** (-5.0 - h)) for h in range(RET_HEADS))


def _rope_tables(pos, half):
    j = lax.broadcasted_iota(jnp.int32, (1, half), 1).astype(F32)
    inv = jnp.exp(j * (-math.log(ROPE_BASE) / half))
    ang = pos * inv
    return jnp.cos(ang), jnp.sin(ang)


def _rope(x, cos, sin):
    half = RET_DK // 2
    outs = []
    for h in range(RET_HEADS):
        x1 = x[:, h * RET_DK:h * RET_DK + half]
        x2 = x[:, h * RET_DK + half:(h + 1) * RET_DK]
        outs += [x1 * cos - x2 * sin, x1 * sin + x2 * cos]
    return jnp.concatenate(outs, axis=-1)


def _ret_norm_gate(o, g):
    o = o * lax.rsqrt(jnp.mean(o * o, axis=-1, keepdims=True) + NORM_EPS)
    return jax.nn.silu(g) * o


def _ret_tables_kernel(cos_ref, sin_ref, dmask_ref, qdec_ref, kdec_ref, *, c_len):
    t_len = cos_ref.shape[0]
    pos = lax.broadcasted_iota(jnp.int32, (t_len, 1), 0).astype(F32)
    cos, sin = _rope_tables(pos, RET_DK // 2)
    cos_ref[...] = cos
    sin_ref[...] = sin
    ti = lax.broadcasted_iota(jnp.int32, (c_len, 1), 0).astype(F32)
    ii = lax.broadcasted_iota(jnp.int32, (c_len, c_len), 0)
    jj = lax.broadcasted_iota(jnp.int32, (c_len, c_len), 1)
    diff = (ii - jj).astype(F32)
    for h in range(RET_HEADS):
        lg = RET_LOG_G[h]
        dmask_ref[h] = jnp.where(diff >= 0, jnp.exp(lg * jnp.maximum(diff, 0.0)), 0.0)
        qdec_ref[h] = jnp.exp(lg * (ti + 1.0))
        kdec_ref[h] = jnp.exp(lg * (c_len - 1.0 - ti))


def _ret_layer_kernel(x_ref, gain_ref, win_ref, wout_ref, cos_ref, sin_ref, dmask_ref, qdec_ref, kdec_ref,
                      o_ref, sfin_ref, s_scr, y_scr, *, c_len, side=None):
    c = pl.program_id(1)

    @pl.when(c == 0)
    def _():
        s_scr[...] = jnp.zeros_like(s_scr)

    x = x_ref[...]
    hb = _rms(x, gain_ref[...]).astype(BF16)
    proj = lambda lo, width: jnp.dot(hb, win_ref[:, lo:lo + width], preferred_element_type=F32)
    cos, sin = cos_ref[...], sin_ref[...]
    q = _rope(proj(0, NQ), cos, sin)
    k = _rope(proj(NQ, NQ), cos, sin) * (RET_DK ** -0.5)
    for h in range(RET_HEADS):
        c_dec = math.exp(RET_LOG_G[h] * c_len)
        qh = q[:, h * RET_DK:(h + 1) * RET_DK]
        kh = k[:, h * RET_DK:(h + 1) * RET_DK]
        vh = proj(2 * NQ + h * RET_DV, RET_DV).astype(BF16)
        s_h = s_scr[h]
        sc = _dot_nt(qh, kh) * dmask_ref[h]
        o = _bdot(sc, vh) + _bdot(qh * qdec_ref[h], s_h)
        s_scr[h] = s_h * c_dec + _dot_tn(kh * kdec_ref[h], vh)
        gh = proj(2 * NQ + NV + h * RET_DV, RET_DV)
        y_scr[:, h * RET_DV:(h + 1) * RET_DV] = _ret_norm_gate(o, gh).astype(BF16)
    o_ref[...] = x + jnp.dot(y_scr[...], wout_ref[...], preferred_element_type=F32)
    if side is not None:
        side()

    @pl.when(c == pl.num_programs(1) - 1)
    def _():
        sfin_ref[0] = s_scr[...]


def retention_layer_prompt(x, gain, w_in, w_out, *, nb, c_len=RET_CHUNK, side=None):
    n = x.shape[0]
    t_len = n // nb
    nc = t_len // c_len
    half = RET_DK // 2
    tabs = pl.pallas_call(
        functools.partial(_ret_tables_kernel, c_len=c_len),
        out_shape=(jax.ShapeDtypeStruct((t_len, half), F32), jax.ShapeDtypeStruct((t_len, half), F32),
                   jax.ShapeDtypeStruct((RET_HEADS, c_len, c_len), F32),
                   jax.ShapeDtypeStruct((RET_HEADS, c_len, 1), F32),
                   jax.ShapeDtypeStruct((RET_HEADS, c_len, 1), F32)),
        name="retention_tables")()
    row = pl.BlockSpec((c_len, D_MODEL), lambda b, c: (b * nc + c, 0))
    pos_spec = pl.BlockSpec((c_len, half), lambda b, c: (c, 0))
    const = lambda a: pl.BlockSpec(a.shape, lambda b, c: (0,) * a.ndim)
    st_spec = pl.BlockSpec((1, RET_HEADS, RET_DK, RET_DV), lambda b, c: (b, 0, 0, 0))
    gain = gain.reshape(1, D_MODEL)
    once = lambda a: pl.BlockSpec(a.shape, lambda b, c: (0,) * a.ndim, pipeline_mode=pl.Buffered(1))
    args = [x, gain, w_in, w_out, *tabs]
    in_specs = [row, const(gain), once(w_in), once(w_out), pos_spec, pos_spec] + [const(a) for a in tabs[2:]]
    out_shape = [jax.ShapeDtypeStruct((n, D_MODEL), F32), jax.ShapeDtypeStruct((nb, RET_HEADS, RET_DK, RET_DV), F32)]
    out_specs = [row, st_spec]
    kern = functools.partial(_ret_layer_kernel, c_len=c_len)
    if side is not None:
        assert side.steps == nb * nc
        kern = _with_side(kern, len(args), 2, side)
        args += list(side.args)
        in_specs += side.in_specs(nc)
        out_shape += list(side.out_shape)
        out_specs += side.out_specs(nc)
    outs = pl.pallas_call(
        kern, grid=(nb, nc), in_specs=in_specs, out_specs=out_specs, out_shape=out_shape,
        scratch_shapes=[pltpu.VMEM((RET_HEADS, RET_DK, RET_DV), F32), pltpu.VMEM((c_len, NV), BF16)],
        compiler_params=_cparams("parallel", "arbitrary"), name="retention_layer")(*args)
    return (outs[0], outs[1]) if side is None else ((outs[0], outs[1]), outs[2:])


def _ret_step_rope_kernel(q_ref, k_ref, qo_ref, ko_ref, *, pos0):
    pos = jnp.full((q_ref.shape[0], 1), pos0, F32)
    cos, sin = _rope_tables(pos, RET_DK // 2)
    qo_ref[...] = _rope(q_ref[...].astype(F32), cos, sin).T
    ko_ref[...] = (_rope(k_ref[...].astype(F32), cos, sin) * (RET_DK ** -0.5)).T


def _ret_step_core_kernel(s_ref, qt_ref, kt_ref, v_ref, g_ref, s_out_ref, y_ref):
    tb = s_ref.shape[0]
    step = pl.program_id(0) * pl.num_programs(1) + pl.program_id(1)
    lane = lax.broadcasted_iota(jnp.int32, qt_ref.shape, 1)
    for i in range(tb):
        mine = lane == step * tb + i
        q_col = jnp.sum(jnp.where(mine, qt_ref[...], 0.0), axis=-1, keepdims=True)
        k_col = jnp.sum(jnp.where(mine, kt_ref[...], 0.0), axis=-1, keepdims=True)
        for h in range(RET_HEADS):
            gam = math.exp(RET_LOG_G[h])
            s_h = s_ref[i, h]
            qc = q_col[h * RET_DK:(h + 1) * RET_DK]
            kc = k_col[h * RET_DK:(h + 1) * RET_DK]
            vs = slice(h * RET_DV, (h + 1) * RET_DV)
            vr = v_ref[i, :, vs].astype(F32)
            qk = jnp.sum(qc * kc, axis=0, keepdims=True)
            o = qk * vr + jnp.sum((qc * gam) * s_h, axis=0, keepdims=True)
            s_out_ref[i, h] = s_h * gam + kc * vr
            y_ref[i, :, vs] = _ret_norm_gate(o, g_ref[i, :, vs].astype(F32)).astype(y_ref.dtype)


def retention_step_job(q, k, v, g, s0, *, pos0, tb):
    n = q.shape[0]
    vec_t = jax.ShapeDtypeStruct((NQ, n), F32)
    qt, kt = pl.pallas_call(functools.partial(_ret_step_rope_kernel, pos0=pos0), out_shape=(vec_t, vec_t),
                            name="retention_step_rope")(q, k)
    st = lambda inner: pl.BlockSpec((tb, RET_HEADS, RET_DK, RET_DV), lambda i, j: (i * inner + j, 0, 0, 0))
    rw = lambda inner: pl.BlockSpec((tb, 1, NV), lambda i, j: (i * inner + j, 0, 0))
    whole = lambda inner: pl.BlockSpec((NQ, n), lambda i, j: (0, 0))
    return SideJob(
        body=_ret_step_core_kernel,
        args=(s0, qt, kt, v.reshape(n, 1, NV), g.reshape(n, 1, NV)),
        in_specs=lambda inner: [st(inner), whole(inner), whole(inner), rw(inner), rw(inner)],
        out_shape=(jax.ShapeDtypeStruct(s0.shape, F32), jax.ShapeDtypeStruct((n, 1, NV), BF16)),
        out_specs=lambda inner: [st(inner), rw(inner)],
        steps=n // tb)


def _xattn_prompt_kernel(x_ref, g_ref, wq_ref, mk_ref, mv_ref, wo_ref, *rest, pre):
    if pre:
        ya_ref, wa_ref, yb_ref, wb_ref, o_ref, att_scr = rest
        x = (x_ref[...] + jnp.dot(ya_ref[...], wa_ref[...], preferred_element_type=F32)
             + jnp.dot(yb_ref[...], wb_ref[...], preferred_element_type=F32))
    else:
        o_ref, att_scr = rest
        x = x_ref[...]
    q = jnp.dot(_rms(x, g_ref[...]).astype(BF16), wq_ref[...], preferred_element_type=F32)
    for h in range(MEM_HEADS):
        hs = slice(h * MEM_HD, (h + 1) * MEM_HD)
        s = _dot_nt(q[:, hs], mk_ref[0, :, hs]) * (MEM_HD ** -0.5)
        s = s - jnp.max(s, axis=-1, keepdims=True)
        e = jnp.exp(s)
        p = e / jnp.sum(e, axis=-1, keepdims=True)
        att_scr[:, hs] = _bdot(p, mv_ref[0, :, hs])
    o_ref[...] = x + jnp.dot(att_scr[...].astype(BF16), wo_ref[...], preferred_element_type=F32)


def xattn_prompt(x, gain, w_q, mem_k, mem_v, w_o, layer, *, nb, tm=1024, pre=None):
    n = x.shape[0]
    tiles_per_b = n // nb // tm
    mem_k = mem_k.reshape(-1, N_MEM, D_MODEL)
    mem_v = mem_v.reshape(-1, N_MEM, D_MODEL)
    row = pl.BlockSpec((tm, D_MODEL), lambda i: (i, 0))
    wspec = pl.BlockSpec((D_MODEL, D_MODEL), lambda i: (0, 0))
    mspec = pl.BlockSpec((1, N_MEM, D_MODEL), lambda i: (layer * nb + i // tiles_per_b, 0, 0))
    args = [x, gain.reshape(1, D_MODEL), w_q, mem_k, mem_v, w_o]
    in_specs = [row, pl.BlockSpec((1, D_MODEL), lambda i: (0, 0)), wspec, mspec, mspec, wspec]
    if pre is not None:
        for y, wy in (pre[:2], pre[2:]):
            t_len, _, kw = y.shape
            args += [y.reshape(t_len, nb * kw), wy]
            in_specs += [_row_spec(tm, kw, (nb, tiles_per_b)), pl.BlockSpec(wy.shape, lambda i: (0, 0))]
    return pl.pallas_call(
        functools.partial(_xattn_prompt_kernel, pre=pre is not None), grid=(n // tm,),
        in_specs=in_specs, out_specs=row, out_shape=jax.ShapeDtypeStruct((n, D_MODEL), F32),
        scratch_shapes=[pltpu.VMEM((tm, D_MODEL), F32)],
        compiler_params=_cparams("parallel"), name="xattn_prompt")(*args)


def _xattn_step_kernel(q_ref, mk_ref, mv_ref, o_ref, *, tb):
    half = N_MEM // 2
    both = lambda z: jnp.concatenate([z, z], axis=1)
    fold = lambda z, op: op(z[:, :MEM_HEADS], z[:, MEM_HEADS:])
    for i in range(tb):
        k8 = jnp.concatenate([mk_ref[0, i, :half], mk_ref[0, i, half:]], axis=1)
        v8 = jnp.concatenate([mv_ref[0, i, :half], mv_ref[0, i, half:]], axis=1)
        q8 = jnp.concatenate([q_ref[i], q_ref[i]], axis=0)
        s = jnp.sum(k8 * q8[None], axis=-1, keepdims=True) * (MEM_HD ** -0.5)
        smax = both(fold(jnp.max(s, axis=0, keepdims=True), jnp.maximum))
        e = jnp.exp(s - smax)
        den = both(fold(jnp.sum(e, axis=0, keepdims=True), jnp.add))
        o8 = jnp.sum((e / den) * v8, axis=0)
        o_ref[i] = o8[:MEM_HEADS] + o8[MEM_HEADS:]


def xattn_step_job(q, cache_k, cache_v, layer, *, tb):
    n = q.shape[0]

    def specs(inner):
        qspec = pl.BlockSpec((tb, MEM_HEADS, MEM_HD), lambda i, j: (i * inner + j, 0, 0))
        cspec = pl.BlockSpec((1, tb, N_MEM, MEM_HEADS, MEM_HD), lambda i, j: (layer, i * inner + j, 0, 0, 0))
        return qspec, cspec

    return SideJob(
        body=functools.partial(_xattn_step_kernel, tb=tb),
        args=(q.reshape(n, MEM_HEADS, MEM_HD), cache_k, cache_v),
        in_specs=lambda inner: [specs(inner)[0], specs(inner)[1], specs(inner)[1]],
        out_shape=(jax.ShapeDtypeStruct((n, MEM_HEADS, MEM_HD), F32),),
        out_specs=lambda inner: [specs(inner)[0]],
        steps=n // tb)


def run_job(job, name):
    return pl.pallas_call(
        job.body, grid=(job.steps, 1), in_specs=job.in_specs(1), out_specs=job.out_specs(1),
        out_shape=list(job.out_shape), compiler_params=_cparams("parallel", "arbitrary"), name=name)(*job.args)


ROUTER_LANES = 128
NEG_BIG = -1e30


def _moe_gates(logits):
    lane = lax.broadcasted_iota(jnp.int32, logits.shape, 1)
    first = lambda mask: jnp.min(jnp.where(mask, lane, ROUTER_LANES), axis=-1, keepdims=True)
    is_c = lane < MOE_GROUPS
    lc = jnp.where(is_c, logits, NEG_BIG)
    mc = jnp.max(lc, axis=-1, keepdims=True)
    g_idx = first(lc == mc)
    p_g = 1.0 / jnp.sum(jnp.where(is_c, jnp.exp(lc - mc), 0.0), axis=-1, keepdims=True)
    fl = lane - MOE_GROUPS
    in_g = (fl >= 0) & (fl < MOE_EXPERTS) & ((fl // MOE_PER_GROUP) == g_idx)
    lf = jnp.where(in_g, logits, NEG_BIG)
    m1 = jnp.max(lf, axis=-1, keepdims=True)
    i1 = first(lf == m1)
    lf2 = jnp.where(lane == i1, NEG_BIG, lf)
    m2 = jnp.max(lf2, axis=-1, keepdims=True)
    i2 = first(lf2 == m2)
    e2 = jnp.exp(m2 - m1)
    w_top = 1.0 / (1.0 + e2)
    gate = p_g * (jnp.where(lane == i1, w_top, 0.0) + jnp.where(lane == i2, e2 * w_top, 0.0))
    return gate, g_idx


MOE_CAP = 288
MOE_EPS = 2
MOE_STEPS = MOE_EXPERTS // MOE_EPS

SideJob = collections.namedtuple("SideJob", "body args in_specs out_shape out_specs steps")


def merge_jobs(a, b):
    assert a.steps == b.steps
    na_in, na_out, n_in = len(a.args), len(a.out_shape), len(a.args) + len(b.args)

    def body(*refs):
        a.body(*refs[:na_in], *refs[n_in:n_in + na_out])
        b.body(*refs[na_in:n_in], *refs[n_in + na_out:])

    return SideJob(body, tuple(a.args) + tuple(b.args), lambda inner: a.in_specs(inner) + b.in_specs(inner),
                   tuple(a.out_shape) + tuple(b.out_shape),
                   lambda inner: a.out_specs(inner) + b.out_specs(inner), a.steps)


def cast_job(arrays, steps):
    views = tuple(a.reshape(steps, -1, a.shape[-1]) for a in arrays)

    def body(*refs):
        for src, dst in zip(refs[:len(views)], refs[len(views):]):
            dst[...] = src[...].astype(BF16)

    specs = lambda inner: [pl.BlockSpec((1,) + v.shape[1:], lambda i, j: (i * inner + j, 0, 0)) for v in views]
    return SideJob(body, views, specs, tuple(jax.ShapeDtypeStruct(v.shape, BF16) for v in views), specs, steps)


def _with_side(main_kernel, n_in, n_out, side):
    ns_in, ns_out = len(side.args), len(side.out_shape)

    def kern(*refs):
        m_in = refs[:n_in]
        s_in = refs[n_in:n_in + ns_in]
        m_out = refs[n_in + ns_in:n_in + ns_in + n_out]
        s_out = refs[n_in + ns_in + n_out:n_in + ns_in + n_out + ns_out]
        scratch = refs[n_in + ns_in + n_out + ns_out:]
        main_kernel(*m_in, *m_out, *scratch, side=lambda: side.body(*s_in, *s_out))

    return kern


def _router_logits(h, wr_ref, br_ref):
    h_hi = h.astype(BF16)
    h_lo = (h - h_hi.astype(F32)).astype(BF16)
    rows = h.shape[0]
    res = jnp.dot(jnp.concatenate([h_hi, h_lo], axis=0), wr_ref[...], preferred_element_type=F32)
    acc = (res[:rows, :ROUTER_LANES] + res[:rows, ROUTER_LANES:]) + (res[rows:, :ROUTER_LANES] + res[rows:, ROUTER_LANES:])
    return acc + br_ref[...]


def _experts_ffn(hb, gate, e0, w1_ref, w3_ref, w2_ref):
    lane = lax.broadcasted_iota(jnp.int32, gate.shape, 1)
    acc = None
    for e in range(MOE_EPS):
        a1 = jnp.dot(hb, w1_ref[e].astype(BF16), preferred_element_type=F32)
        a3 = jnp.dot(hb, w3_ref[e].astype(BF16), preferred_element_type=F32)
        ge = jnp.sum(jnp.where(lane == MOE_GROUPS + e0 + e, gate, 0.0), axis=-1, keepdims=True)
        hid = (jax.nn.silu(a1) * a3 * ge).astype(BF16)
        part = jnp.dot(hid, w2_ref[e].astype(BF16), preferred_element_type=F32)
        acc = part if acc is None else acc + part
    return acc


def _moe_kernel(x_ref, g_ref, wr_ref, br_ref, el_ref, w1_ref, w3_ref, w2_ref, *rest, final_norm, cap, side=None):
    if final_norm:
        fin_ref, o_ref = rest[:2]
        rest = rest[2:]
    else:
        o_ref = rest[0]
        rest = rest[1:]
    h_scr, oh_scr, rk_scr, ohr_scr, rkr_scr, hg_scr, gg_scr, yg_scr, cnt_smem = rest
    tm = x_ref.shape[0]
    step = pl.program_id(1)
    grp = step // (MOE_PER_GROUP // MOE_EPS)
    first_half = step % (MOE_PER_GROUP // MOE_EPS) == 0
    last_half = step % (MOE_PER_GROUP // MOE_EPS) == MOE_PER_GROUP // MOE_EPS - 1

    @pl.when(step == 0)
    def _():
        x = x_ref[...]
        h = _rms(x, g_ref[...])
        gate, g_idx = _moe_gates(_router_logits(h, wr_ref, br_ref))
        g_hi = gate.astype(BF16)
        h_scr[:, :D_MODEL] = h.astype(BF16)
        h_scr[:, D_MODEL:D_MODEL + ROUTER_LANES] = g_hi
        h_scr[:, D_MODEL + ROUTER_LANES:] = (gate - g_hi.astype(F32)).astype(BF16)
        o_ref[...] = x
        lane = lax.broadcasted_iota(jnp.int32, gate.shape, 1)
        onehot = jnp.where(lane == g_idx, 1.0, 0.0)
        rank = jnp.dot(el_ref[...], onehot.astype(BF16), preferred_element_type=F32)
        oh_scr[...] = onehot
        rk_scr[...] = rank
        ohr_scr[...] = onehot.T[:8]
        rkr_scr[...] = rank.T[:8]
        cnt = jnp.sum(onehot, axis=0, keepdims=True)
        for gi in range(MOE_GROUPS):
            cnt_smem[gi] = cnt[0, gi].astype(jnp.int32)

    lane = lax.broadcasted_iota(jnp.int32, (tm, ROUTER_LANES), 1)

    def gather_mat(base):
        slot = jnp.where(ohr_scr[pl.ds(grp, 1), :] > 0.5, rkr_scr[pl.ds(grp, 1), :] - base, -1.0)
        c = lax.broadcasted_iota(jnp.int32, (cap, tm), 0).astype(F32)
        return jnp.where(c == slot, 1.0, 0.0).astype(BF16)

    def scatter_mat(base):
        member = jnp.sum(jnp.where(lane == grp, oh_scr[...], 0.0), axis=-1, keepdims=True)
        rank = jnp.sum(jnp.where(lane == grp, rk_scr[...], 0.0), axis=-1, keepdims=True)
        slot = jnp.where(member > 0.5, rank - base, -1.0)
        c = lax.broadcasted_iota(jnp.int32, (tm, cap), 1).astype(F32)
        return jnp.where(c == slot, 1.0, 0.0).astype(BF16)

    def gather(base):
        got = jnp.dot(gather_mat(base), h_scr[...], preferred_element_type=F32)
        gg = got[:, D_MODEL:D_MODEL + ROUTER_LANES] + got[:, D_MODEL + ROUTER_LANES:]
        return got[:, :D_MODEL].astype(BF16), gg

    @pl.when(first_half)
    def _():
        hg, gg = gather(0.0)
        hg_scr[...] = hg
        gg_scr[...] = gg
        yg_scr[...] = jnp.zeros_like(yg_scr)

    yg_scr[...] += _experts_ffn(hg_scr[...], gg_scr[...], step * MOE_EPS, w1_ref, w3_ref, w2_ref)
    if side is not None:
        side()

    @pl.when(last_half)
    def _():
        o_ref[...] += jnp.dot(scatter_mat(0.0), yg_scr[...].astype(BF16), preferred_element_type=F32)

    def extra_round(r, carry):
        base = (r * cap).astype(F32)
        hg, gg = gather(base)
        y = _experts_ffn(hg, gg, step * MOE_EPS, w1_ref, w3_ref, w2_ref)
        o_ref[...] += jnp.dot(scatter_mat(base), y.astype(BF16), preferred_element_type=F32)
        return carry

    lax.fori_loop(1, (cnt_smem[grp] + cap - 1) // cap, extra_round, 0)

    if final_norm:
        @pl.when(step == MOE_STEPS - 1)
        def _():
            o_ref[...] = _rms(o_ref[...], fin_ref[...])


def moe_dense(x, gain, w_r, b_r, w1, w3, w2, layer, *, tm=512, cap=MOE_CAP, final_gain=None, side=None):
    n = x.shape[0]
    tm = min(tm, n)
    cap = min(cap, tm)
    gain = gain.reshape(1, D_MODEL)
    row = pl.BlockSpec((tm, D_MODEL), lambda i, s: (i, 0))
    const2 = lambda a: pl.BlockSpec(a.shape, lambda i, s: (0,) * a.ndim)
    soff = layer * MOE_STEPS
    wspec = pl.BlockSpec((MOE_EPS, D_MODEL, MOE_HIDDEN), lambda i, s: (soff + s, 0, 0))
    earlier = jnp.tril(jnp.ones((tm, tm), BF16), -1)
    args = [x, gain, w_r, b_r, earlier, w1, w3, w2]
    in_specs = [row, const2(gain), const2(w_r), const2(b_r), const2(earlier), wspec, wspec,
                pl.BlockSpec((MOE_EPS, MOE_HIDDEN, D_MODEL), lambda i, s: (soff + s, 0, 0))]
    if final_gain is not None:
        args.append(final_gain.reshape(1, D_MODEL))
        in_specs.append(const2(args[-1]))
    kern = functools.partial(_moe_kernel, final_norm=final_gain is not None, cap=cap)
    out_shape = [jax.ShapeDtypeStruct((n, D_MODEL), F32)]
    out_specs = [row]
    grid = (n // tm, MOE_STEPS)
    if side is not None:
        assert side.steps == grid[0] * grid[1]
        kern = _with_side(kern, len(args), 1, side)
        args += list(side.args)
        in_specs += side.in_specs(MOE_STEPS)
        out_shape += list(side.out_shape)
        out_specs += side.out_specs(MOE_STEPS)
    outs = pl.pallas_call(
        kern, grid=grid, in_specs=in_specs, out_specs=out_specs, out_shape=out_shape,
        scratch_shapes=[pltpu.VMEM((tm, D_MODEL + 2 * ROUTER_LANES), BF16),
                        pltpu.VMEM((tm, ROUTER_LANES), F32),
                        pltpu.VMEM((tm, ROUTER_LANES), F32),
                        pltpu.VMEM((8, tm), F32),
                        pltpu.VMEM((8, tm), F32),
                        pltpu.VMEM((cap, D_MODEL), BF16),
                        pltpu.VMEM((cap, ROUTER_LANES), F32),
                        pltpu.VMEM((cap, D_MODEL), F32),
                        pltpu.SMEM((MOE_GROUPS,), jnp.int32)],
        compiler_params=_cparams("parallel", "arbitrary"), name="moe")(*args)
    return outs[0] if side is None else (outs[0], outs[1:])


def _group_weights(w1, w3, w2):
    ne = w1.shape[0] * MOE_EXPERTS
    return (w1.reshape(ne, D_MODEL, MOE_HIDDEN), w3.reshape(ne, D_MODEL, MOE_HIDDEN),
            w2.reshape(ne, MOE_HIDDEN, D_MODEL))


def _router_params(w_rc, b_rc, w_rf, b_rf):
    pad = ROUTER_LANES - MOE_GROUPS - MOE_EXPERTS
    w_r = jnp.concatenate([w_rc, w_rf, jnp.zeros((D_MODEL, pad), F32)], axis=1).astype(F32)
    b_r = jnp.concatenate([b_rc, b_rf, jnp.zeros((pad,), F32)]).reshape(1, ROUTER_LANES).astype(F32)
    w_hi = w_r.astype(BF16)
    w_lo = (w_r - w_hi.astype(F32)).astype(BF16)
    return jnp.concatenate([w_hi, w_lo], axis=1), b_r


def _mem_kv_kernel(x_ref, g_ref, w_ref, kf_ref, vf_ref, kh_ref, vh_ref):
    h = _rms(x_ref[...], g_ref[0]).astype(BF16)
    for col, f_ref, h_ref in ((0, kf_ref, kh_ref), (D_MODEL, vf_ref, vh_ref)):
        acc = jnp.dot(h, w_ref[0, :, col:col + D_MODEL], preferred_element_type=F32)
        f_ref[0] = acc.astype(f_ref.dtype)
        for hd in range(MEM_HEADS):
            h_ref[0, :, hd, :] = acc[:, hd * MEM_HD:(hd + 1) * MEM_HD]


def mem_kv(mem, gains, w_kv, *, tm=512):
    rows = mem.shape[0]
    nl = w_kv.shape[0]
    flat = jax.ShapeDtypeStruct((nl, rows, D_MODEL), BF16)
    head = jax.ShapeDtypeStruct((nl, rows, MEM_HEADS, MEM_HD), F32)
    fspec = pl.BlockSpec((1, tm, D_MODEL), lambda l, i: (l, i, 0))
    hspec = pl.BlockSpec((1, tm, MEM_HEADS, MEM_HD), lambda l, i: (l, i, 0, 0))
    return pl.pallas_call(
        _mem_kv_kernel, grid=(nl, rows // tm),
        in_specs=[pl.BlockSpec((tm, D_MODEL), lambda l, i: (i, 0)),
                  pl.BlockSpec((1, 1, D_MODEL), lambda l, i: (l, 0, 0)),
                  pl.BlockSpec((1, D_MODEL, 2 * D_MODEL), lambda l, i: (l, 0, 0))],
        out_specs=(fspec, fspec, hspec, hspec), out_shape=(flat, flat, head, head),
        compiler_params=_cparams("parallel", "parallel"), name="mem_kv")(
            mem, gains.reshape(nl, 1, D_MODEL), w_kv)


def _forward(xp, xs, nbp, w, st, mem_k, mem_v, cache_k, cache_v):
    assert DEPTH == 2
    nbs = xs.shape[0]
    moe_tm = 1024
    moe_steps_p = (xp.shape[0] // moe_tm) * MOE_STEPS
    rwp = tuple(w[k][0] for k in ('rw_mu', 'rw_w0', 'rw_w2', 'rw_a0', 'rw_a2', 'rw_g2',
                                  'rw_k_k', 'rw_k_a', 'rw_r_k', 'rw_ln_w', 'rw_ln_b'))
    w_in0, w_out0 = w['w_in0_bf'][0], w['w_out0_bf'][0]

    u, p_s = linear(xs, w_in0, gain=w['norm_mix'][0], splits=(S5_WIDTH, RW_PROJ))
    y_s5, s5r_s, s5i_s = s5_mixer(u.reshape(1, nbs, S5_WIDTH), st['s5_re'], st['s5_im'], w['s5p'][0],
                                  w['s5_d'][0], w['s5_w_glu'][0], tc=1)
    y_rw, rw_s = rwkv_step(p_s, st['shift'], st['rwkv'], rwp)
    xs = linear(y_s5.reshape(nbs, S5_WIDTH), w_out0[:S5_WIDTH], x2=y_rw, w2=w_out0[S5_WIDTH:], residual=xs)
    q_s = linear(xs, w['w_mq_bf'][0], gain=w['norm_mem'][0])

    zeros = lambda *shape: jnp.zeros(shape, F32)
    u, p_p = linear(xp, w_in0, gain=w['norm_mix'][0], splits=(S5_WIDTH, RW_PROJ), out_tmajor=True, batch=nbp)
    y_s5, s5r_p, s5i_p = s5_mixer(u, zeros(nbp, S5_STATE), zeros(nbp, S5_STATE), w['s5p'][0],
                                  w['s5_d'][0], w['s5_w_glu'][0], tc=128)
    rw_bs = 4
    rw_steps = (nbp // rw_bs) * (p_p.shape[0] // RW_HD)
    later = (w['moe_w1'], w['moe_w3'], w['moe_w2'], w['w_in1'][0], w['w_out1'][0])
    job = merge_jobs(xattn_step_job(q_s, cache_k, cache_v, 0, tb=nbs // rw_steps), cast_job(later, rw_steps))
    (y_rw, rw_p, sh_p), (att_s, *cast) = rwkv_prompt(
        p_p, zeros(nbp, RW_PROJ), zeros(nbp, RW_HEADS, RW_HD, RW_HD), rwp, bs=rw_bs, side=job)
    moe_w = _group_weights(*[c.reshape(a.shape) for c, a in zip(cast[:3], later[:3])])
    w_in1, w_out1 = (c.reshape(a.shape) for c, a in zip(cast[3:], later[3:]))
    moe = lambda x, layer, **kw: moe_dense(x, w['norm_ffn'][layer], *w['router'][layer], *moe_w, layer, **kw)
    xp = xattn_prompt(xp, w['norm_mem'][0], w['w_mq_bf'][0], mem_k, mem_v, w['w_mo_bf'][0], 0, nb=nbp,
                      pre=(y_s5, w_out0[:S5_WIDTH], y_rw, w_out0[S5_WIDTH:]))

    xs = linear(att_s.reshape(nbs, D_MODEL), w['w_mo_bf'][0], residual=xs)
    xs = moe(xs, 0)
    q, k, v, g = linear(xs, w_in1, gain=w['norm_mix'][1], out_dtype=BF16, splits=(NQ, NQ, NV, NV))
    xp = moe(xp, 0, tm=moe_tm)
    ret_steps = xp.shape[0] // RET_CHUNK
    job = retention_step_job(q, k, v, g, st['ret'], pos0=float(PAST_LEN), tb=nbs // ret_steps)
    (xp, ret_p), (ret_s, y_ret) = retention_layer_prompt(xp, w['norm_mix'][1], w_in1, w_out1, nb=nbp, side=job)
    xs = linear(y_ret.reshape(nbs, NV), w_out1, residual=xs)
    q_s = linear(xs, w['w_mq_bf'][1], gain=w['norm_mem'][1])

    xp = xattn_prompt(xp, w['norm_mem'][1], w['w_mq_bf'][1], mem_k, mem_v, w['w_mo_bf'][1], 1, nb=nbp)
    job = xattn_step_job(q_s, cache_k, cache_v, 1, tb=nbs // moe_steps_p)
    y_p, (att_s,) = moe(xp, 1, tm=moe_tm, final_gain=w['norm_final'], side=job)
    xs = linear(att_s.reshape(nbs, D_MODEL), w['w_mo_bf'][1], residual=xs)
    y_s = moe(xs, 1, final_gain=w['norm_final'])

    grp = lambda z, nb: z.reshape(1, nb, S5_GROUPS, S5_N)
    prompt_out = (y_p, grp(s5r_p, nbp), grp(s5i_p, nbp), rw_p[None], sh_p[None], ret_p[None])
    sample_out = (y_s, grp(s5r_s, nbs), grp(s5i_s, nbs), rw_s[None], p_s[None], ret_s[None])
    return prompt_out, sample_out


def kernel(x_prompt, x_sample, mem_prompt, state_s5_re, state_s5_im, state_rwkv, state_shift, state_ret, cache_mem_k, cache_mem_v, norm_mix, norm_mem, norm_ffn, norm_final, w_in0, w_out0, s5_a_re, s5_a_im, s5_b_re, s5_b_im, s5_c_re, s5_c_im, s5_d, s5_log_dt, s5_w_glu, rw_mu, rw_w0, rw_w2, rw_a0, rw_a2, rw_g2, rw_k_k, rw_k_a, rw_r_k, rw_ln_w, rw_ln_b, w_in1, w_out1, mem_norm, w_mq, w_mk, w_mv, w_mo, moe_w_rc, moe_b_rc, moe_w_rf, moe_b_rf, moe_w1, moe_w3, moe_w2):
    w = dict(norm_mix=norm_mix, norm_mem=norm_mem, norm_ffn=norm_ffn, norm_final=norm_final,
             w_in0=w_in0, w_out0=w_out0, s5_a_re=s5_a_re, s5_a_im=s5_a_im, s5_b_re=s5_b_re, s5_b_im=s5_b_im,
             s5_c_re=s5_c_re, s5_c_im=s5_c_im, s5_d=s5_d, s5_log_dt=s5_log_dt, s5_w_glu=s5_w_glu,
             rw_mu=rw_mu, rw_w0=rw_w0, rw_w2=rw_w2, rw_a0=rw_a0, rw_a2=rw_a2, rw_g2=rw_g2,
             rw_k_k=rw_k_k, rw_k_a=rw_k_a, rw_r_k=rw_r_k, rw_ln_w=rw_ln_w, rw_ln_b=rw_ln_b,
             w_in1=w_in1, w_out1=w_out1, w_mq=w_mq, w_mo=w_mo,
             moe_w_rc=moe_w_rc, moe_b_rc=moe_b_rc, moe_w_rf=moe_w_rf, moe_b_rf=moe_b_rf,
             moe_w1=moe_w1, moe_w3=moe_w3, moe_w2=moe_w2)
    nbp, t_len, _ = x_prompt.shape
    nbs = x_sample.shape[0]
    n_even, n_odd = state_s5_re.shape[0], state_ret.shape[0]
    for name in ('w_in0', 'w_out0', 'w_mq', 'w_mo'):
        w[name + '_bf'] = w[name].astype(BF16)
    w['s5p'] = [_s5_params(s5_a_re[i], s5_a_im[i], s5_b_re[i], s5_b_im[i], s5_c_re[i], s5_c_im[i], s5_log_dt[i])
                for i in range(n_even)]
    w['router'] = [_router_params(moe_w_rc[l], moe_b_rc[l], moe_w_rf[l], moe_b_rf[l]) for l in range(DEPTH)]

    mem = mem_prompt.reshape(nbp * N_MEM, D_MODEL)
    w_kv = jnp.concatenate([w_mk, w_mv], axis=2).astype(BF16)
    mk, mv, mk_h, mv_h = mem_kv(mem, mem_norm, w_kv)
    mem_k_l = mk.reshape(DEPTH, nbp, N_MEM, D_MODEL)
    mem_v_l = mv.reshape(DEPTH, nbp, N_MEM, D_MODEL)
    mem_k_p = mk_h.reshape(DEPTH, nbp, N_MEM, MEM_HEADS, MEM_HD)
    mem_v_p = mv_h.reshape(DEPTH, nbp, N_MEM, MEM_HEADS, MEM_HD)

    assert n_even == 1 and n_odd == 1
    st = dict(s5_re=state_s5_re.reshape(nbs, S5_STATE), s5_im=state_s5_im.reshape(nbs, S5_STATE),
              rwkv=state_rwkv[0], shift=state_shift[0], ret=state_ret[0])
    (y_p, s5r_p, s5i_p, rw_p, sh_p, ret_p), (y_s, s5r_s, s5i_s, rw_s, sh_s, ret_s) = _forward(
        x_prompt.reshape(nbp * t_len, D_MODEL), x_sample.reshape(nbs, D_MODEL), nbp, w, st,
        mem_k_l, mem_v_l, cache_mem_k, cache_mem_v)
    return (y_p.reshape(nbp, t_len, D_MODEL), y_s.reshape(nbs, 1, D_MODEL),
            s5r_p, s5i_p, rw_p, sh_p, ret_p, mem_k_p, mem_v_p, s5r_s, s5i_s, rw_s, sh_s, ret_s)
```

```python
import collections
import functools
import math

import jax
import jax.numpy as jnp
from jax import lax
from jax.experimental import pallas as pl
from jax.experimental.pallas import tpu as pltpu

F32 = jnp.float32
BF16 = jnp.bfloat16

D_MODEL = 1024
DEPTH = 2
PAST_LEN = 16384
S5_WIDTH = 512
S5_GROUP = 16
S5_GROUPS = 32
S5_N = 64
S5_STATE = S5_GROUPS * S5_N
S5_GBLK = 8
RW_WIDTH = 512
RW_HD = 64
RW_HEADS = 8
RW_LORA = 256
RW_PROJ = 3 * RW_WIDTH + RW_LORA
IN0 = S5_WIDTH + RW_PROJ
RET_DK = 256
RET_HEADS = 4
RET_DV = 512
RET_CHUNK = 256
NQ = RET_HEADS * RET_DK
NV = RET_HEADS * RET_DV
IN1 = 2 * NQ + 2 * NV
N_MEM = 256
MEM_HEADS = 4
MEM_HD = 256
MOE_GROUPS = 4
MOE_PER_GROUP = 4
MOE_EXPERTS = 16
MOE_HIDDEN = 256
NORM_EPS = 1e-6
RW_GN_EPS = 64e-5
ROPE_BASE = 10000.0

VMEM_LIMIT = 56 * 1024 * 1024


def _cparams(*sem):
    return pltpu.CompilerParams(dimension_semantics=sem, vmem_limit_bytes=VMEM_LIMIT)


def _bdot(a, b):
    return jnp.dot(a.astype(BF16), b.astype(BF16), preferred_element_type=F32)


def _dot_nt(a, b):
    return lax.dot_general(a.astype(BF16), b.astype(BF16), (((1,), (1,)), ((), ())),
                           preferred_element_type=F32)


def _dot_tn(a, b):
    return lax.dot_general(a.astype(BF16), b.astype(BF16), (((0,), (0,)), ((), ())),
                           preferred_element_type=F32)


def _split3(x):
    hi = x.astype(BF16)
    r1 = x - hi.astype(F32)
    mid = r1.astype(BF16)
    lo = (r1 - mid.astype(F32)).astype(BF16)
    return hi, mid, lo


def _dot_exact_rhs(x, m_bf16, passes=3):
    hi, mid, lo = _split3(x)
    acc = jnp.dot(hi, m_bf16, preferred_element_type=F32)
    if passes > 1:
        acc = acc + jnp.dot(mid, m_bf16, preferred_element_type=F32)
    if passes > 2:
        acc = acc + jnp.dot(lo, m_bf16, preferred_element_type=F32)
    return acc


def _rms(x, g):
    ms = jnp.mean(x * x, axis=-1, keepdims=True)
    return x * lax.rsqrt(ms + NORM_EPS) * g


def _linear_kernel(*refs, norm, two, res):
    it = iter(refs)
    x_ref = next(it)
    g_ref = next(it) if norm else None
    w_ref = next(it)
    x2_ref = next(it) if two else None
    w2_ref = next(it) if two else None
    r_ref = next(it) if res else None
    o_refs = list(it)
    x = x_ref[...].astype(F32)
    if norm:
        x = _rms(x, g_ref[...])
    xb = x.astype(BF16)
    x2b = x2_ref[...].astype(BF16) if two else None
    col = 0
    for o_ref in o_refs:
        m = o_ref.shape[-1]
        step = next((s for s in (512, 256) if m % s == 0), m)
        for j in range(m // step):
            sl = slice(col + j * step, col + (j + 1) * step)
            acc = jnp.dot(xb, w_ref[:, sl], preferred_element_type=F32)
            if two:
                acc = acc + jnp.dot(x2b, w2_ref[:, sl], preferred_element_type=F32)
            if res:
                acc = acc + r_ref[:, sl]
            o_ref[:, j * step:(j + 1) * step] = acc.astype(o_ref.dtype)
        col += m


def _row_spec(tm, width, tmajor_b):
    if tmajor_b is None:
        return pl.BlockSpec((tm, width), lambda i: (i, 0))
    nb, tiles_per_b = tmajor_b
    return pl.BlockSpec((tm, width), lambda i: (i % tiles_per_b, i // tiles_per_b))


def linear(x, w, *, gain=None, x2=None, w2=None, residual=None, out_dtype=F32, tm=512,
           out_tmajor=False, batch=None, splits=None, name="linear"):
    n, k = x.shape
    nb = batch
    t_len = n // nb if nb else None
    m = w.shape[1]
    tm = min(tm, t_len if out_tmajor else n)
    assert n % tm == 0
    tiles_per_b = (t_len // tm) if out_tmajor else None
    rows = lambda a: pl.BlockSpec((tm, a.shape[-1]), lambda i: (i, 0))
    args, specs = [x], [rows(x)]
    if gain is not None:
        args.append(gain.reshape(1, k).astype(F32))
        specs.append(pl.BlockSpec((1, k), lambda i: (0, 0)))
    args.append(w)
    specs.append(pl.BlockSpec(w.shape, lambda i: (0, 0)))
    if x2 is not None:
        args += [x2, w2]
        specs += [rows(x2), pl.BlockSpec(w2.shape, lambda i: (0, 0))]
    if residual is not None:
        args.append(residual)
        specs.append(rows(residual))
    widths = tuple(splits) if splits else (m,)
    assert sum(widths) == m
    if out_tmajor:
        out_shape = [jax.ShapeDtypeStruct((t_len, nb * mw), out_dtype) for mw in widths]
    else:
        out_shape = [jax.ShapeDtypeStruct((n, mw), out_dtype) for mw in widths]
    out_specs = [_row_spec(tm, mw, (nb, tiles_per_b) if out_tmajor else None) for mw in widths]
    kern = functools.partial(_linear_kernel, norm=gain is not None, two=x2 is not None,
                             res=residual is not None)
    outs = pl.pallas_call(
        kern, grid=(n // tm,), in_specs=specs, out_specs=out_specs, out_shape=out_shape,
        compiler_params=_cparams("parallel"), name=name)(*args)
    if out_tmajor:
        outs = [o.reshape(t_len, nb, mw) for o, mw in zip(outs, widths)]
    return outs if splits else outs[0]


def _s5_kernel(u_ref, h_re_ref, h_im_ref, abar_re_ref, abar_im_ref, bb_re_ref, bb_im_ref,
               cc_re_ref, cc_im_ref, d_ref, wglu_ref, y_ref, s_re_ref, s_im_ref,
               x_re, x_im, st_re, st_im, il_scr, *, tc, nb, flat):
    c = pl.program_id(0)
    nlb = S5_WIDTH // 128
    rows = tc * nb
    nblk = S5_GROUPS // S5_GBLK
    bw_in = S5_GBLK * S5_GROUP
    bw_st = S5_GBLK * S5_N

    @pl.when(c == 0)
    def _():
        st_re[...] = h_re_ref[...]
        st_im[...] = h_im_ref[...]

    if flat:
        for b in range(nb):
            for j in range(nlb):
                il_scr[j, pl.ds(b, tc, stride=nb), :] = u_ref[:, b * S5_WIDTH + j * 128:b * S5_WIDTH + (j + 1) * 128]
        u = jnp.concatenate([il_scr[j] for j in range(nlb)], axis=-1)
    else:
        u = u_ref[...].reshape(rows, S5_WIDTH)
    ub = u.astype(BF16)
    for gb in range(nblk):
        ui = ub[:, gb * bw_in:(gb + 1) * bw_in]
        x_re[:, gb * bw_st:(gb + 1) * bw_st] = jnp.dot(ui, bb_re_ref[gb], preferred_element_type=F32)
        x_im[:, gb * bw_st:(gb + 1) * bw_st] = jnp.dot(ui, bb_im_ref[gb], preferred_element_type=F32)

    lane_blk = 1024
    for lb in range(S5_STATE // lane_blk):
        sl = slice(lb * lane_blk, (lb + 1) * lane_blk)
        ar = jnp.broadcast_to(abar_re_ref[:, sl], (nb, lane_blk))
        ai = jnp.broadcast_to(abar_im_ref[:, sl], (nb, lane_blk))

        def body(t, carry, sl=sl, ar=ar, ai=ai):
            xr, xi = carry
            r0 = pl.multiple_of(t * nb, nb)
            br = x_re[pl.ds(r0, nb), sl]
            bi = x_im[pl.ds(r0, nb), sl]
            nr = ar * xr - ai * xi + br
            ni = ar * xi + ai * xr + bi
            x_re[pl.ds(r0, nb), sl] = nr
            x_im[pl.ds(r0, nb), sl] = ni
            return nr, ni

        fr, fi = lax.fori_loop(0, tc, body, (st_re[:, sl], st_im[:, sl]), unroll=min(tc, 4))
        st_re[:, sl] = fr
        st_im[:, sl] = fi

    for gb in range(nblk):
        xr = x_re[:, gb * bw_st:(gb + 1) * bw_st].astype(BF16)
        xi = x_im[:, gb * bw_st:(gb + 1) * bw_st].astype(BF16)
        yb = (jnp.dot(xr, cc_re_ref[gb], preferred_element_type=F32)
              - jnp.dot(xi, cc_im_ref[gb], preferred_element_type=F32))
        cs = slice(gb * bw_in, (gb + 1) * bw_in)
        yb = yb + d_ref[:, cs] * u[:, cs]
        x_re[:, cs] = jax.nn.gelu(yb)
    y = x_re[:, :S5_WIDTH]
    y = y * jax.nn.sigmoid(jnp.dot(y.astype(BF16), wglu_ref[...], preferred_element_type=F32))
    if flat:
        for j in range(nlb):
            il_scr[j] = y[:, j * 128:(j + 1) * 128]
        for b in range(nb):
            for j in range(nlb):
                y_ref[:, b * S5_WIDTH + j * 128:b * S5_WIDTH + (j + 1) * 128] = (
                    il_scr[j, pl.ds(b, tc, stride=nb), :].astype(y_ref.dtype))
    else:
        y_ref[...] = y.reshape(y_ref.shape).astype(y_ref.dtype)

    @pl.when(c == pl.num_programs(0) - 1)
    def _():
        s_re_ref[...] = st_re[...]
        s_im_ref[...] = st_im[...]


def _s5_params(a_re, a_im, b_re, b_im, c_re, c_im, log_dt):
    dt = jnp.exp(log_dt.astype(F32))[:, None]
    ar, ai = a_re.astype(F32), a_im.astype(F32)
    mag = jnp.exp(dt * ar)
    abar_re, abar_im = mag * jnp.cos(dt * ai), mag * jnp.sin(dt * ai)
    den = ar * ar + ai * ai
    nr = abar_re - 1.0
    coef_re = (nr * ar + abar_im * ai) / den
    coef_im = (abar_im * ar - nr * ai) / den
    cr, ci = coef_re[..., None], coef_im[..., None]
    brf, bif = b_re.astype(F32), b_im.astype(F32)
    bb_re = cr * brf - ci * bif
    bb_im = cr * bif + ci * brf
    nblk = S5_GROUPS // S5_GBLK
    eye = jnp.eye(S5_GBLK, dtype=F32)

    def blockdiag_in(bb):
        t = jnp.transpose(bb, (0, 2, 1)).reshape(nblk, S5_GBLK, S5_GROUP, S5_N)
        m = jnp.einsum('kgcn,gh->kgchn', t, eye)
        return m.reshape(nblk, S5_GBLK * S5_GROUP, S5_GBLK * S5_N).astype(BF16)

    def blockdiag_out(cc):
        t = jnp.transpose(cc.astype(F32), (0, 2, 1)).reshape(nblk, S5_GBLK, S5_N, S5_GROUP)
        m = jnp.einsum('khnc,hg->khngc', t, eye)
        return m.reshape(nblk, S5_GBLK * S5_N, S5_GBLK * S5_GROUP).astype(BF16)

    return (abar_re.reshape(1, S5_STATE), abar_im.reshape(1, S5_STATE),
            blockdiag_in(bb_re), blockdiag_in(bb_im), blockdiag_out(c_re), blockdiag_out(c_im))


def s5_mixer(u_tm, h_re, h_im, params, d_skip, w_glu, *, tc):
    t_len, nb, _ = u_tm.shape
    abar_re, abar_im, bb_re, bb_im, cc_re, cc_im = params
    tc = min(tc, t_len)
    assert t_len % tc == 0 and nb % 8 == 0
    rows = tc * nb
    flat = t_len > 1
    full = lambda a: pl.BlockSpec(a.shape, lambda c: (0,) * a.ndim)
    if flat:
        u_arg = u_tm.reshape(t_len, nb * S5_WIDTH)
        io_spec = pl.BlockSpec((tc, nb * S5_WIDTH), lambda c: (c, 0))
        y_shape = jax.ShapeDtypeStruct((t_len, nb * S5_WIDTH), BF16)
    else:
        u_arg = u_tm
        io_spec = pl.BlockSpec((tc, nb, S5_WIDTH), lambda c: (c, 0, 0))
        y_shape = jax.ShapeDtypeStruct((t_len, nb, S5_WIDTH), BF16)
    args = (u_arg, h_re, h_im, abar_re, abar_im, bb_re, bb_im, cc_re, cc_im,
            d_skip.reshape(1, S5_WIDTH).astype(F32), w_glu.astype(BF16))
    in_specs = [io_spec] + [full(a) for a in args[1:]]
    st_shape = jax.ShapeDtypeStruct((nb, S5_STATE), F32)
    st_spec = pl.BlockSpec((nb, S5_STATE), lambda c: (0, 0))
    scratch = [pltpu.VMEM((rows, S5_STATE), F32), pltpu.VMEM((rows, S5_STATE), F32),
               pltpu.VMEM((nb, S5_STATE), F32), pltpu.VMEM((nb, S5_STATE), F32),
               pltpu.VMEM((S5_WIDTH // 128, rows if flat else 8, 128), F32)]
    y, s_re, s_im = pl.pallas_call(
        functools.partial(_s5_kernel, tc=tc, nb=nb, flat=flat), grid=(t_len // tc,), in_specs=in_specs,
        out_specs=(io_spec, st_spec, st_spec), out_shape=(y_shape, st_shape, st_shape),
        scratch_shapes=scratch, compiler_params=_cparams("arbitrary"), name="s5_mixer")(*args)
    return y.reshape(t_len, nb, S5_WIDTH), s_re, s_im


def _head_ones():
    i = lax.broadcasted_iota(jnp.int32, (RW_WIDTH, RW_WIDTH), 0) // RW_HD
    j = lax.broadcasted_iota(jnp.int32, (RW_WIDTH, RW_WIDTH), 1) // RW_HD
    return jnp.where(i == j, 1.0, 0.0).astype(BF16)


def _softplus(z):
    return jnp.maximum(z, 0.0) + jnp.log1p(jnp.exp(-jnp.abs(z)))


def _rw_prep(p, p_prev, prm, ones_bd):
    mu, w0, w2, a0, a2, g2, k_k, k_a = prm
    xm = p + (p_prev - p) * mu
    o1, o2, o3 = RW_WIDTH, 2 * RW_WIDTH, 3 * RW_WIDTH
    r, k, v = xm[:, :o1], xm[:, o1:o2], xm[:, o2:o3]
    wd, ad, gd = xm[:, o3:o3 + 64], xm[:, o3 + 64:o3 + 128], xm[:, o3 + 128:]
    w = -_softplus(-(w0 + _bdot(jnp.tanh(wd), w2))) - 0.5
    lw = -jnp.exp(w)
    a = jax.nn.sigmoid(a0 + _bdot(ad, a2))
    g = _bdot(jax.nn.sigmoid(gd), g2)
    kk = k * k_k
    ss = _dot_exact_rhs(kk * kk, ones_bd, passes=1)
    kk = kk / jnp.maximum(jnp.sqrt(ss), 1e-12)
    k = k * (1.0 + (a - 1.0) * k_a)
    return r, lw, k, v, -kk, kk * a, g


def _rw_post(o, r, k, v, g, r_k, ln_w, ln_b, ones_bd):
    inv = 1.0 / RW_HD
    mean = _dot_exact_rhs(o, ones_bd, passes=2) * inv
    d = o - mean
    var = _dot_exact_rhs(d * d, ones_bd, passes=1) * inv
    on = d * lax.rsqrt(var + RW_GN_EPS) * ln_w + ln_b
    bonus = _dot_exact_rhs(r * k * r_k, ones_bd, passes=1) * v
    return (on + bonus) * g


def _rw_chunk_kernel(p_ref, shift_ref, h0_ref, mu_ref, w0_ref, w2_ref, a0_ref, a2_ref, g2_ref,
                     kk_ref, ka_ref, rk_ref, lnw_ref, lnb_ref,
                     y_ref, hfin_ref, shout_ref, prev_scr, h_scr, o_scr, *, c_len, bs, side=None):
    c = pl.program_id(1)
    nc = pl.num_programs(1)
    cl = c_len

    @pl.when(c == 0)
    def _():
        prev_scr[...] = shift_ref[:, 0, :]
        h_scr[...] = h0_ref[...]

    ones_bd = _head_ones()
    row = lax.broadcasted_iota(jnp.int32, (cl, RW_PROJ), 0)
    ps, pprevs = [], []
    for bi in range(bs):
        p = p_ref[:, bi * RW_PROJ:(bi + 1) * RW_PROJ]
        pprevs.append(jnp.where(row == 0, prev_scr[bi:bi + 1, :], pltpu.roll(p, 1, 0)))
        ps.append(p)
    p_all = jnp.concatenate(ps, axis=0) if bs > 1 else ps[0]
    pprev_all = jnp.concatenate(pprevs, axis=0) if bs > 1 else pprevs[0]
    prm = (mu_ref[...], w0_ref[...], w2_ref[...], a0_ref[...], a2_ref[...], g2_ref[...],
           kk_ref[...], ka_ref[...])
    r, lw, k, v, a, b, g = _rw_prep(p_all, pprev_all, prm, ones_bd)

    ti = lax.broadcasted_iota(jnp.int32, (cl, cl), 0)
    si = lax.broadcasted_iota(jnp.int32, (cl, cl), 1)
    lmat = jnp.where(ti >= si, 1.0, 0.0).astype(BF16)
    eye = jnp.where(ti == si, 1.0, 0.0)
    mi = lax.broadcasted_iota(jnp.int32, (2 * cl, 3 * cl), 0)
    mj = lax.broadcasted_iota(jnp.int32, (2 * cl, 3 * cl), 1)
    t_row = jnp.where(mi >= cl, mi - cl, mi)
    s_col = jnp.where(mj < cl, mj, jnp.where(mj >= 2 * cl, mj - 2 * cl, -4 * cl))
    keep = (t_row - s_col) >= jnp.where(mi >= cl, 0, 1)
    eye_bf = eye.astype(BF16)

    lhs_l, rhs_l, vh_l, hcat_l, kb_l, etot_l = [], [], [], [], [], []
    for bi in range(bs):
        rs = slice(bi * cl, (bi + 1) * cl)
        lw_b = lw[rs]
        l_hi, l_mid, l_lo = _split3(lw_b)
        cum = (jnp.dot(lmat, l_hi, preferred_element_type=F32)
               + jnp.dot(lmat, l_mid, preferred_element_type=F32)
               + jnp.dot(lmat, l_lo, preferred_element_type=F32))
        tot = cum[cl - 1:cl, :]
        e_neg = jnp.exp(-cum)
        e_rem = jnp.exp(tot - cum)
        at = (a[rs] * jnp.exp(cum - lw_b)).astype(BF16)
        rt = (r[rs] * jnp.exp(cum)).astype(BF16)
        bt = (b[rs] * e_neg).astype(BF16)
        kt = (k[rs] * e_neg).astype(BF16)
        bh = (b[rs] * e_rem).astype(BF16)
        kh = (k[rs] * e_rem).astype(BF16)
        e_tot = jnp.exp(tot)
        vb = v[rs].astype(BF16)
        for h in range(RW_HEADS):
            hs = slice(h * RW_HD, (h + 1) * RW_HD)
            lhs_l.append(jnp.concatenate([at[:, hs], rt[:, hs]], axis=0))
            rhs_l.append(jnp.concatenate([kt[:, hs], eye_bf, bt[:, hs]], axis=0))
            vh_l.append(vb[:, hs])
            kb_l.append(jnp.concatenate([kh[:, hs], bh[:, hs]], axis=0))
            etot_l.append(jnp.sum(eye * e_tot[:, hs], axis=-1, keepdims=True))
            hcat_l.append(h_scr[bi, h])

    nitem = bs * RW_HEADS
    items = range(nitem)
    aa_l = [jnp.where(keep, _dot_nt(lhs_l[i], rhs_l[i]), 0.0).astype(BF16) for i in items]
    pw_l = [aa_l[i][:cl, 2 * cl:] for i in items]
    tinv_l = [eye_bf + pw_l[i] for i in items]
    for _ in range(int(math.log2(cl)) - 1):
        pw_l = [jnp.dot(pw_l[i], pw_l[i], preferred_element_type=F32).astype(BF16) for i in items]
        tinv_l = [jnp.dot(tinv_l[i], eye_bf + pw_l[i], preferred_element_type=F32).astype(BF16) for i in items]
    vh_cat = [jnp.concatenate([vh_l[i], hcat_l[i].astype(BF16)], axis=0) for i in items]
    x1_l = [jnp.dot(aa_l[i][:cl, :2 * cl], vh_cat[i], preferred_element_type=F32).astype(BF16) for i in items]
    u_l = [jnp.dot(tinv_l[i], x1_l[i], preferred_element_type=F32).astype(BF16) for i in items]
    o_l = [jnp.dot(aa_l[i][cl:, :], jnp.concatenate([vh_cat[i], u_l[i]], axis=0),
                   preferred_element_type=F32) for i in items]
    hn_l = [hcat_l[i] * etot_l[i]
            + lax.dot_general(kb_l[i], jnp.concatenate([vh_l[i], u_l[i]], axis=0), (((0,), (0,)), ((), ())),
                              preferred_element_type=F32) for i in items]

    for bi in range(bs):
        for h in range(RW_HEADS):
            i = bi * RW_HEADS + h
            o_scr[bi * cl:(bi + 1) * cl, h * RW_HD:(h + 1) * RW_HD] = o_l[i]
            h_scr[bi, h] = hn_l[i]
        prev_scr[bi:bi + 1, :] = ps[bi][cl - 1:cl, :]

    y = _rw_post(o_scr[...], r, k, v, g, rk_ref[...], lnw_ref[...], lnb_ref[...], ones_bd)
    for bi in range(bs):
        y_ref[:, bi * RW_WIDTH:(bi + 1) * RW_WIDTH] = y[bi * cl:(bi + 1) * cl].astype(y_ref.dtype)
    if side is not None:
        side()

    @pl.when(c == nc - 1)
    def _():
        hfin_ref[...] = h_scr[...]
        for bi in range(bs):
            shout_ref[bi] = ps[bi][cl - 1:cl, :]


def _rw_param_args(mu, w0, w2, a0, a2, g2, k_k, k_a, r_k, ln_w, ln_b):
    row = lambda z: z.reshape(1, -1).astype(F32)
    return (row(mu), row(w0), w2.astype(BF16), row(a0), a2.astype(BF16), g2.astype(BF16),
            row(k_k), row(k_a), row(r_k), row(ln_w), row(ln_b))


def rwkv_prompt(p_tm, shift, s0, params, *, bs=4, side=None):
    c_len = RW_HD
    t_len, nb, _ = p_tm.shape
    assert t_len % c_len == 0 and nb % bs == 0
    prm = _rw_param_args(*params)
    const = lambda a: pl.BlockSpec(a.shape, lambda b, c: (0,) * a.ndim)
    st_spec = pl.BlockSpec((bs, RW_HEADS, RW_HD, RW_HD), lambda b, c: (b, 0, 0, 0))
    sh_spec = pl.BlockSpec((bs, 1, RW_PROJ), lambda b, c: (b, 0, 0))
    in_specs = [pl.BlockSpec((c_len, bs * RW_PROJ), lambda b, c: (c, b)), sh_spec, st_spec] + [const(a) for a in prm]
    out_shape = (jax.ShapeDtypeStruct((t_len, nb * RW_WIDTH), BF16),
                 jax.ShapeDtypeStruct((nb, RW_HEADS, RW_HD, RW_HD), F32),
                 jax.ShapeDtypeStruct((nb, 1, RW_PROJ), F32))
    out_specs = (pl.BlockSpec((c_len, bs * RW_WIDTH), lambda b, c: (c, b)), st_spec, sh_spec)
    scratch = [pltpu.VMEM((bs, RW_PROJ), F32), pltpu.VMEM((bs, RW_HEADS, RW_HD, RW_HD), F32),
               pltpu.VMEM((bs * c_len, RW_WIDTH), F32)]
    h0 = jnp.swapaxes(s0, -1, -2)
    args = [p_tm.reshape(t_len, nb * RW_PROJ), shift.reshape(nb, 1, RW_PROJ), h0, *prm]
    kern = functools.partial(_rw_chunk_kernel, c_len=c_len, bs=bs)
    grid = (nb // bs, t_len // c_len)
    out_shape, out_specs = list(out_shape), list(out_specs)
    if side is not None:
        assert side.steps == grid[0] * grid[1]
        kern = _with_side(kern, len(args), 3, side)
        args += list(side.args)
        in_specs += side.in_specs(grid[1])
        out_shape += list(side.out_shape)
        out_specs += side.out_specs(grid[1])
    outs = pl.pallas_call(
        kern, grid=grid, in_specs=in_specs, out_specs=out_specs, out_shape=out_shape,
        scratch_shapes=scratch, compiler_params=_cparams("parallel", "arbitrary"), name="rwkv_prompt")(*args)
    y, h_fin, sh = outs[:3]
    res = (y.reshape(t_len, nb, RW_WIDTH), jnp.swapaxes(h_fin, -1, -2), sh.reshape(nb, RW_PROJ))
    return res if side is None else (res, outs[3:])


def _rw_step_prep_kernel(p_ref, shift_ref, mu_ref, w0_ref, w2_ref, a0_ref, a2_ref, g2_ref, kk_ref, ka_ref,
                         r_ref, k_ref, v_ref, g_ref, rt_ref, wt_ref, kt_ref, at_ref, bt_ref, vt_ref):
    prm = (mu_ref[...], w0_ref[...], w2_ref[...], a0_ref[...], a2_ref[...], g2_ref[...],
           kk_ref[...], ka_ref[...])
    r, lw, k, v, a, b, g = _rw_prep(p_ref[...], shift_ref[...], prm, _head_ones())
    r_ref[...] = r
    k_ref[...] = k
    v_ref[...] = v
    g_ref[...] = g
    rt_ref[...] = r.T
    wt_ref[...] = jnp.exp(lw).T
    kt_ref[...] = k.T
    at_ref[...] = a.T
    bt_ref[...] = b.T
    vt_ref[...] = v.T


def _rw_step_core_kernel(s_ref, r_ref, w_ref, k_ref, a_ref, b_ref, v_ref, s_out_ref, o_ref):
    r, w, k, a, b = r_ref[0], w_ref[0], k_ref[0], a_ref[0], b_ref[0]
    for j in range(s_ref.shape[1]):
        s = s_ref[0, j]
        sa = jnp.sum(s * a, axis=0, keepdims=True)
        s_new = s * w + sa * b + v_ref[0, j:j + 1, :] * k
        s_out_ref[0, j] = s_new
        o_ref[0, j:j + 1, :] = jnp.sum(s_new * r, axis=0, keepdims=True)


def _rw_step_post_kernel(ot_ref, r_ref, k_ref, v_ref, g_ref, rk_ref, lnw_ref, lnb_ref, y_ref):
    y_ref[...] = _rw_post(ot_ref[...].T, r_ref[...], k_ref[...], v_ref[...], g_ref[...],
                          rk_ref[...], lnw_ref[...], lnb_ref[...], _head_ones()).astype(y_ref.dtype)


def rwkv_step(p, shift, s0, params, *, vb=32):
    n = p.shape[0]
    prm = _rw_param_args(*params)
    vec = jax.ShapeDtypeStruct((n, RW_WIDTH), F32)
    vec_t = jax.ShapeDtypeStruct((RW_WIDTH, n), F32)
    r, k, v, g, rt, wt, kt, at, bt, vt = pl.pallas_call(
        _rw_step_prep_kernel, out_shape=(vec,) * 4 + (vec_t,) * 6, name="rwkv_step_prep")(p, shift, *prm[:8])
    heads = lambda z: z.reshape(RW_HEADS, RW_HD, n)
    k_spec = pl.BlockSpec((1, RW_HD, n), lambda h, j: (h, 0, 0))
    v_spec = pl.BlockSpec((1, vb, n), lambda h, j: (h, j, 0))
    st_spec = pl.BlockSpec((1, vb, RW_HD, n), lambda h, j: (h, j, 0, 0))
    st = jnp.transpose(s0, (1, 2, 3, 0))
    s_new, ot = pl.pallas_call(
        _rw_step_core_kernel, grid=(RW_HEADS, RW_HD // vb),
        in_specs=[st_spec] + [k_spec] * 5 + [v_spec], out_specs=(st_spec, v_spec),
        out_shape=(jax.ShapeDtypeStruct(st.shape, F32), jax.ShapeDtypeStruct((RW_HEADS, RW_HD, n), F32)),
        compiler_params=_cparams("parallel", "parallel"), name="rwkv_step_core")(
            st, heads(rt), heads(wt), heads(kt), heads(at), heads(bt), heads(vt))
    y = pl.pallas_call(
        _rw_step_post_kernel, out_shape=jax.ShapeDtypeStruct((n, RW_WIDTH), BF16), name="rwkv_step_post")(
            ot.reshape(RW_WIDTH, n), r, k, v, g, *prm[8:])
    return y, jnp.transpose(s_new, (3, 0, 1, 2))


RET_LOG_G = tuple(math.log(1.0 - 2.0 ** (-5.0 - h)) for h in range(RET_HEADS))


def _rope_tables(pos, half):
    j = lax.broadcasted_iota(jnp.int32, (1, half), 1).astype(F32)
    inv = jnp.exp(j * (-math.log(ROPE_BASE) / half))
    ang = pos * inv
    return jnp.cos(ang), jnp.sin(ang)


def _rope(x, cos, sin):
    half = RET_DK // 2
    outs = []
    for h in range(RET_HEADS):
        x1 = x[:, h * RET_DK:h * RET_DK + half]
        x2 = x[:, h * RET_DK + half:(h + 1) * RET_DK]
        outs += [x1 * cos - x2 * sin, x1 * sin + x2 * cos]
    return jnp.concatenate(outs, axis=-1)


def _ret_norm_gate(o, g):
    o = o * lax.rsqrt(jnp.mean(o * o, axis=-1, keepdims=True) + NORM_EPS)
    return jax.nn.silu(g) * o


def _ret_tables_kernel(cos_ref, sin_ref, dmask_ref, qdec_ref, kdec_ref, *, c_len):
    t_len = cos_ref.shape[0]
    pos = lax.broadcasted_iota(jnp.int32, (t_len, 1), 0).astype(F32)
    cos, sin = _rope_tables(pos, RET_DK // 2)
    cos_ref[...] = cos
    sin_ref[...] = sin
    ti = lax.broadcasted_iota(jnp.int32, (c_len, 1), 0).astype(F32)
    ii = lax.broadcasted_iota(jnp.int32, (c_len, c_len), 0)
    jj = lax.broadcasted_iota(jnp.int32, (c_len, c_len), 1)
    diff = (ii - jj).astype(F32)
    for h in range(RET_HEADS):
        lg = RET_LOG_G[h]
        dmask_ref[h] = jnp.where(diff >= 0, jnp.exp(lg * jnp.maximum(diff, 0.0)), 0.0)
        qdec_ref[h] = jnp.exp(lg * (ti + 1.0))
        kdec_ref[h] = jnp.exp(lg * (c_len - 1.0 - ti))


def _ret_layer_kernel(x_ref, gain_ref, win_ref, wout_ref, cos_ref, sin_ref, dmask_ref, qdec_ref, kdec_ref,
                      o_ref, sfin_ref, s_scr, y_scr, *, c_len, side=None):
    c = pl.program_id(1)

    @pl.when(c == 0)
    def _():
        s_scr[...] = jnp.zeros_like(s_scr)

    x = x_ref[...]
    hb = _rms(x, gain_ref[...]).astype(BF16)
    proj = lambda lo, width: jnp.dot(hb, win_ref[:, lo:lo + width], preferred_element_type=F32)
    cos, sin = cos_ref[...], sin_ref[...]
    q = _rope(proj(0, NQ), cos, sin)
    k = _rope(proj(NQ, NQ), cos, sin) * (RET_DK ** -0.5)
    for h in range(RET_HEADS):
        c_dec = math.exp(RET_LOG_G[h] * c_len)
        qh = q[:, h * RET_DK:(h + 1) * RET_DK]
        kh = k[:, h * RET_DK:(h + 1) * RET_DK]
        vh = proj(2 * NQ + h * RET_DV, RET_DV).astype(BF16)
        s_h = s_scr[h]
        sc = _dot_nt(qh, kh) * dmask_ref[h]
        o = _bdot(sc, vh) + _bdot(qh * qdec_ref[h], s_h)
        s_scr[h] = s_h * c_dec + _dot_tn(kh * kdec_ref[h], vh)
        gh = proj(2 * NQ + NV + h * RET_DV, RET_DV)
        y_scr[:, h * RET_DV:(h + 1) * RET_DV] = _ret_norm_gate(o, gh).astype(BF16)
    o_ref[...] = x + jnp.dot(y_scr[...], wout_ref[...], preferred_element_type=F32)
    if side is not None:
        side()

    @pl.when(c == pl.num_programs(1) - 1)
    def _():
        sfin_ref[0] = s_scr[...]


def retention_layer_prompt(x, gain, w_in, w_out, *, nb, c_len=RET_CHUNK, side=None):
    n = x.shape[0]
    t_len = n // nb
    nc = t_len // c_len
    half = RET_DK // 2
    tabs = pl.pallas_call(
        functools.partial(_ret_tables_kernel, c_len=c_len),
        out_shape=(jax.ShapeDtypeStruct((t_len, half), F32), jax.ShapeDtypeStruct((t_len, half), F32),
                   jax.ShapeDtypeStruct((RET_HEADS, c_len, c_len), F32),
                   jax.ShapeDtypeStruct((RET_HEADS, c_len, 1), F32),
                   jax.ShapeDtypeStruct((RET_HEADS, c_len, 1), F32)),
        name="retention_tables")()
    row = pl.BlockSpec((c_len, D_MODEL), lambda b, c: (b * nc + c, 0))
    pos_spec = pl.BlockSpec((c_len, half), lambda b, c: (c, 0))
    const = lambda a: pl.BlockSpec(a.shape, lambda b, c: (0,) * a.ndim)
    st_spec = pl.BlockSpec((1, RET_HEADS, RET_DK, RET_DV), lambda b, c: (b, 0, 0, 0))
    gain = gain.reshape(1, D_MODEL)
    once = lambda a: pl.BlockSpec(a.shape, lambda b, c: (0,) * a.ndim, pipeline_mode=pl.Buffered(1))
    args = [x, gain, w_in, w_out, *tabs]
    in_specs = [row, const(gain), once(w_in), once(w_out), pos_spec, pos_spec] + [const(a) for a in tabs[2:]]
    out_shape = [jax.ShapeDtypeStruct((n, D_MODEL), F32), jax.ShapeDtypeStruct((nb, RET_HEADS, RET_DK, RET_DV), F32)]
    out_specs = [row, st_spec]
    kern = functools.partial(_ret_layer_kernel, c_len=c_len)
    if side is not None:
        assert side.steps == nb * nc
        kern = _with_side(kern, len(args), 2, side)
        args += list(side.args)
        in_specs += side.in_specs(nc)
        out_shape += list(side.out_shape)
        out_specs += side.out_specs(nc)
    outs = pl.pallas_call(
        kern, grid=(nb, nc), in_specs=in_specs, out_specs=out_specs, out_shape=out_shape,
        scratch_shapes=[pltpu.VMEM((RET_HEADS, RET_DK, RET_DV), F32), pltpu.VMEM((c_len, NV), BF16)],
        compiler_params=_cparams("parallel", "arbitrary"), name="retention_layer")(*args)
    return (outs[0], outs[1]) if side is None else ((outs[0], outs[1]), outs[2:])


def _ret_step_rope_kernel(q_ref, k_ref, qo_ref, ko_ref, *, pos0):
    pos = jnp.full((q_ref.shape[0], 1), pos0, F32)
    cos, sin = _rope_tables(pos, RET_DK // 2)
    qo_ref[...] = _rope(q_ref[...].astype(F32), cos, sin).T
    ko_ref[...] = (_rope(k_ref[...].astype(F32), cos, sin) * (RET_DK ** -0.5)).T


def _ret_step_core_kernel(s_ref, qt_ref, kt_ref, v_ref, g_ref, s_out_ref, y_ref):
    tb = s_ref.shape[0]
    step = pl.program_id(0) * pl.num_programs(1) + pl.program_id(1)
    lane = lax.broadcasted_iota(jnp.int32, qt_ref.shape, 1)
    for i in range(tb):
        mine = lane == step * tb + i
        q_col = jnp.sum(jnp.where(mine, qt_ref[...], 0.0), axis=-1, keepdims=True)
        k_col = jnp.sum(jnp.where(mine, kt_ref[...], 0.0), axis=-1, keepdims=True)
        for h in range(RET_HEADS):
            gam = math.exp(RET_LOG_G[h])
            s_h = s_ref[i, h]
            qc = q_col[h * RET_DK:(h + 1) * RET_DK]
            kc = k_col[h * RET_DK:(h + 1) * RET_DK]
            vs = slice(h * RET_DV, (h + 1) * RET_DV)
            vr = v_ref[i, :, vs].astype(F32)
            qk = jnp.sum(qc * kc, axis=0, keepdims=True)
            o = qk * vr + jnp.sum((qc * gam) * s_h, axis=0, keepdims=True)
            s_out_ref[i, h] = s_h * gam + kc * vr
            y_ref[i, :, vs] = _ret_norm_gate(o, g_ref[i, :, vs].astype(F32)).astype(y_ref.dtype)


def retention_step_job(q, k, v, g, s0, *, pos0, tb):
    n = q.shape[0]
    vec_t = jax.ShapeDtypeStruct((NQ, n), F32)
    qt, kt = pl.pallas_call(functools.partial(_ret_step_rope_kernel, pos0=pos0), out_shape=(vec_t, vec_t),
                            name="retention_step_rope")(q, k)
    st = lambda inner: pl.BlockSpec((tb, RET_HEADS, RET_DK, RET_DV), lambda i, j: (i * inner + j, 0, 0, 0))
    rw = lambda inner: pl.BlockSpec((tb, 1, NV), lambda i, j: (i * inner + j, 0, 0))
    whole = lambda inner: pl.BlockSpec((NQ, n), lambda i, j: (0, 0))
    return SideJob(
        body=_ret_step_core_kernel,
        args=(s0, qt, kt, v.reshape(n, 1, NV), g.reshape(n, 1, NV)),
        in_specs=lambda inner: [st(inner), whole(inner), whole(inner), rw(inner), rw(inner)],
        out_shape=(jax.ShapeDtypeStruct(s0.shape, F32), jax.ShapeDtypeStruct((n, 1, NV), BF16)),
        out_specs=lambda inner: [st(inner), rw(inner)],
        steps=n // tb)


def _xattn_prompt_kernel(x_ref, g_ref, wq_ref, mk_ref, mv_ref, wo_ref, *rest, pre):
    if pre:
        ya_ref, wa_ref, yb_ref, wb_ref, o_ref, att_scr = rest
        x = (x_ref[...] + jnp.dot(ya_ref[...], wa_ref[...], preferred_element_type=F32)
             + jnp.dot(yb_ref[...], wb_ref[...], preferred_element_type=F32))
    else:
        o_ref, att_scr = rest
        x = x_ref[...]
    q = jnp.dot(_rms(x, g_ref[...]).astype(BF16), wq_ref[...], preferred_element_type=F32)
    for h in range(MEM_HEADS):
        hs = slice(h * MEM_HD, (h + 1) * MEM_HD)
        s = _dot_nt(q[:, hs], mk_ref[0, :, hs]) * (MEM_HD ** -0.5)
        s = s - jnp.max(s, axis=-1, keepdims=True)
        e = jnp.exp(s)
        p = e / jnp.sum(e, axis=-1, keepdims=True)
        att_scr[:, hs] = _bdot(p, mv_ref[0, :, hs])
    o_ref[...] = x + jnp.dot(att_scr[...].astype(BF16), wo_ref[...], preferred_element_type=F32)


def xattn_prompt(x, gain, w_q, mem_k, mem_v, w_o, layer, *, nb, tm=1024, pre=None):
    n = x.shape[0]
    tiles_per_b = n // nb // tm
    mem_k = mem_k.reshape(-1, N_MEM, D_MODEL)
    mem_v = mem_v.reshape(-1, N_MEM, D_MODEL)
    row = pl.BlockSpec((tm, D_MODEL), lambda i: (i, 0))
    wspec = pl.BlockSpec((D_MODEL, D_MODEL), lambda i: (0, 0))
    mspec = pl.BlockSpec((1, N_MEM, D_MODEL), lambda i: (layer * nb + i // tiles_per_b, 0, 0))
    args = [x, gain.reshape(1, D_MODEL), w_q, mem_k, mem_v, w_o]
    in_specs = [row, pl.BlockSpec((1, D_MODEL), lambda i: (0, 0)), wspec, mspec, mspec, wspec]
    if pre is not None:
        for y, wy in (pre[:2], pre[2:]):
            t_len, _, kw = y.shape
            args += [y.reshape(t_len, nb * kw), wy]
            in_specs += [_row_spec(tm, kw, (nb, tiles_per_b)), pl.BlockSpec(wy.shape, lambda i: (0, 0))]
    return pl.pallas_call(
        functools.partial(_xattn_prompt_kernel, pre=pre is not None), grid=(n // tm,),
        in_specs=in_specs, out_specs=row, out_shape=jax.ShapeDtypeStruct((n, D_MODEL), F32),
        scratch_shapes=[pltpu.VMEM((tm, D_MODEL), F32)],
        compiler_params=_cparams("parallel"), name="xattn_prompt")(*args)


def _xattn_step_kernel(q_ref, mk_ref, mv_ref, o_ref, *, tb):
    half = N_MEM // 2
    both = lambda z: jnp.concatenate([z, z], axis=1)
    fold = lambda z, op: op(z[:, :MEM_HEADS], z[:, MEM_HEADS:])
    for i in range(tb):
        k8 = jnp.concatenate([mk_ref[0, i, :half], mk_ref[0, i, half:]], axis=1)
        v8 = jnp.concatenate([mv_ref[0, i, :half], mv_ref[0, i, half:]], axis=1)
        q8 = jnp.concatenate([q_ref[i], q_ref[i]], axis=0)
        s = jnp.sum(k8 * q8[None], axis=-1, keepdims=True) * (MEM_HD ** -0.5)
        smax = both(fold(jnp.max(s, axis=0, keepdims=True), jnp.maximum))
        e = jnp.exp(s - smax)
        den = both(fold(jnp.sum(e, axis=0, keepdims=True), jnp.add))
        o8 = jnp.sum((e / den) * v8, axis=0)
        o_ref[i] = o8[:MEM_HEADS] + o8[MEM_HEADS:]


def xattn_step_job(q, cache_k, cache_v, layer, *, tb):
    n = q.shape[0]

    def specs(inner):
        qspec = pl.BlockSpec((tb, MEM_HEADS, MEM_HD), lambda i, j: (i * inner + j, 0, 0))
        cspec = pl.BlockSpec((1, tb, N_MEM, MEM_HEADS, MEM_HD), lambda i, j: (layer, i * inner + j, 0, 0, 0))
        return qspec, cspec

    return SideJob(
        body=functools.partial(_xattn_step_kernel, tb=tb),
        args=(q.reshape(n, MEM_HEADS, MEM_HD), cache_k, cache_v),
        in_specs=lambda inner: [specs(inner)[0], specs(inner)[1], specs(inner)[1]],
        out_shape=(jax.ShapeDtypeStruct((n, MEM_HEADS, MEM_HD), F32),),
        out_specs=lambda inner: [specs(inner)[0]],
        steps=n // tb)


def run_job(job, name):
    return pl.pallas_call(
        job.body, grid=(job.steps, 1), in_specs=job.in_specs(1), out_specs=job.out_specs(1),
        out_shape=list(job.out_shape), compiler_params=_cparams("parallel", "arbitrary"), name=name)(*job.args)


ROUTER_LANES = 128
NEG_BIG = -1e30


def _moe_gates(logits):
    lane = lax.broadcasted_iota(jnp.int32, logits.shape, 1)
    first = lambda mask: jnp.min(jnp.where(mask, lane, ROUTER_LANES), axis=-1, keepdims=True)
    is_c = lane < MOE_GROUPS
    lc = jnp.where(is_c, logits, NEG_BIG)
    mc = jnp.max(lc, axis=-1, keepdims=True)
    g_idx = first(lc == mc)
    p_g = 1.0 / jnp.sum(jnp.where(is_c, jnp.exp(lc - mc), 0.0), axis=-1, keepdims=True)
    fl = lane - MOE_GROUPS
    in_g = (fl >= 0) & (fl < MOE_EXPERTS) & ((fl // MOE_PER_GROUP) == g_idx)
    lf = jnp.where(in_g, logits, NEG_BIG)
    m1 = jnp.max(lf, axis=-1, keepdims=True)
    i1 = first(lf == m1)
    lf2 = jnp.where(lane == i1, NEG_BIG, lf)
    m2 = jnp.max(lf2, axis=-1, keepdims=True)
    i2 = first(lf2 == m2)
    e2 = jnp.exp(m2 - m1)
    w_top = 1.0 / (1.0 + e2)
    gate = p_g * (jnp.where(lane == i1, w_top, 0.0) + jnp.where(lane == i2, e2 * w_top, 0.0))
    return gate, g_idx


MOE_CAP = 320
MOE_EPS = 2
MOE_STEPS = MOE_EXPERTS // MOE_EPS

SideJob = collections.namedtuple("SideJob", "body args in_specs out_shape out_specs steps")


def merge_jobs(a, b):
    assert a.steps == b.steps
    na_in, na_out, n_in = len(a.args), len(a.out_shape), len(a.args) + len(b.args)

    def body(*refs):
        a.body(*refs[:na_in], *refs[n_in:n_in + na_out])
        b.body(*refs[na_in:n_in], *refs[n_in + na_out:])

    return SideJob(body, tuple(a.args) + tuple(b.args), lambda inner: a.in_specs(inner) + b.in_specs(inner),
                   tuple(a.out_shape) + tuple(b.out_shape),
                   lambda inner: a.out_specs(inner) + b.out_specs(inner), a.steps)


def cast_job(arrays, steps):
    views = tuple(a.reshape(steps, -1, a.shape[-1]) for a in arrays)

    def body(*refs):
        for src, dst in zip(refs[:len(views)], refs[len(views):]):
            dst[...] = src[...].astype(BF16)

    specs = lambda inner: [pl.BlockSpec((1,) + v.shape[1:], lambda i, j: (i * inner + j, 0, 0)) for v in views]
    return SideJob(body, views, specs, tuple(jax.ShapeDtypeStruct(v.shape, BF16) for v in views), specs, steps)


def _with_side(main_kernel, n_in, n_out, side):
    ns_in, ns_out = len(side.args), len(side.out_shape)

    def kern(*refs):
        m_in = refs[:n_in]
        s_in = refs[n_in:n_in + ns_in]
        m_out = refs[n_in + ns_in:n_in + ns_in + n_out]
        s_out = refs[n_in + ns_in + n_out:n_in + ns_in + n_out + ns_out]
        scratch = refs[n_in + ns_in + n_out + ns_out:]
        main_kernel(*m_in, *m_out, *scratch, side=lambda: side.body(*s_in, *s_out))

    return kern


def _router_logits(h, wr_ref, br_ref):
    h_hi = h.astype(BF16)
    h_lo = (h - h_hi.astype(F32)).astype(BF16)
    rows = h.shape[0]
    res = jnp.dot(jnp.concatenate([h_hi, h_lo], axis=0), wr_ref[...], preferred_element_type=F32)
    acc = (res[:rows, :ROUTER_LANES] + res[:rows, ROUTER_LANES:]) + (res[rows:, :ROUTER_LANES] + res[rows:, ROUTER_LANES:])
    return acc + br_ref[...]


def _experts_ffn(hb, gate, e0, w1_ref, w3_ref, w2_ref):
    lane = lax.broadcasted_iota(jnp.int32, gate.shape, 1)
    acc = None
    for e in range(MOE_EPS):
        a1 = jnp.dot(hb, w1_ref[e].astype(BF16), preferred_element_type=F32)
        a3 = jnp.dot(hb, w3_ref[e].astype(BF16), preferred_element_type=F32)
        ge = jnp.sum(jnp.where(lane == MOE_GROUPS + e0 + e, gate, 0.0), axis=-1, keepdims=True)
        hid = (jax.nn.silu(a1) * a3 * ge).astype(BF16)
        part = jnp.dot(hid, w2_ref[e].astype(BF16), preferred_element_type=F32)
        acc = part if acc is None else acc + part
    return acc


def _moe_kernel(x_ref, g_ref, wr_ref, br_ref, el_ref, w1_ref, w3_ref, w2_ref, *rest, final_norm, cap, side=None):
    if final_norm:
        fin_ref, o_ref = rest[:2]
        rest = rest[2:]
    else:
        o_ref = rest[0]
        rest = rest[1:]
    h_scr, oh_scr, rk_scr, ohr_scr, rkr_scr, hg_scr, gg_scr, yg_scr, cnt_smem = rest
    tm = x_ref.shape[0]
    step = pl.program_id(1)
    grp = step // (MOE_PER_GROUP // MOE_EPS)
    first_half = step % (MOE_PER_GROUP // MOE_EPS) == 0
    last_half = step % (MOE_PER_GROUP // MOE_EPS) == MOE_PER_GROUP // MOE_EPS - 1

    @pl.when(step == 0)
    def _():
        x = x_ref[...]
        h = _rms(x, g_ref[...])
        gate, g_idx = _moe_gates(_router_logits(h, wr_ref, br_ref))
        g_hi = gate.astype(BF16)
        h_scr[:, :D_MODEL] = h.astype(BF16)
        h_scr[:, D_MODEL:D_MODEL + ROUTER_LANES] = g_hi
        h_scr[:, D_MODEL + ROUTER_LANES:] = (gate - g_hi.astype(F32)).astype(BF16)
        o_ref[...] = x
        lane = lax.broadcasted_iota(jnp.int32, gate.shape, 1)
        onehot = jnp.where(lane == g_idx, 1.0, 0.0)
        rank = jnp.dot(el_ref[...], onehot.astype(BF16), preferred_element_type=F32)
        oh_scr[...] = onehot
        rk_scr[...] = rank
        ohr_scr[...] = onehot.T[:8]
        rkr_scr[...] = rank.T[:8]
        cnt = jnp.sum(onehot, axis=0, keepdims=True)
        for gi in range(MOE_GROUPS):
            cnt_smem[gi] = cnt[0, gi].astype(jnp.int32)

    lane = lax.broadcasted_iota(jnp.int32, (tm, ROUTER_LANES), 1)

    def gather_mat(base):
        slot = jnp.where(ohr_scr[pl.ds(grp, 1), :] > 0.5, rkr_scr[pl.ds(grp, 1), :] - base, -1.0)
        c = lax.broadcasted_iota(jnp.int32, (cap, tm), 0).astype(F32)
        return jnp.where(c == slot, 1.0, 0.0).astype(BF16)

    def scatter_mat(base):
        member = jnp.sum(jnp.where(lane == grp, oh_scr[...], 0.0), axis=-1, keepdims=True)
        rank = jnp.sum(jnp.where(lane == grp, rk_scr[...], 0.0), axis=-1, keepdims=True)
        slot = jnp.where(member > 0.5, rank - base, -1.0)
        c = lax.broadcasted_iota(jnp.int32, (tm, cap), 1).astype(F32)
        return jnp.where(c == slot, 1.0, 0.0).astype(BF16)

    def gather(base):
        got = jnp.dot(gather_mat(base), h_scr[...], preferred_element_type=F32)
        gg = got[:, D_MODEL:D_MODEL + ROUTER_LANES] + got[:, D_MODEL + ROUTER_LANES:]
        return got[:, :D_MODEL].astype(BF16), gg

    @pl.when(first_half)
    def _():
        hg, gg = gather(0.0)
        hg_scr[...] = hg
        gg_scr[...] = gg
        yg_scr[...] = jnp.zeros_like(yg_scr)

    yg_scr[...] += _experts_ffn(hg_scr[...], gg_scr[...], step * MOE_EPS, w1_ref, w3_ref, w2_ref)
    if side is not None:
        side()

    @pl.when(last_half)
    def _():
        sm = scatter_mat(0.0)
        yb = yg_scr[...].astype(BF16)
        half = tm // 2
        for r0 in (0, half):
            o_ref[r0:r0 + half, :] += jnp.dot(sm[r0:r0 + half], yb, preferred_element_type=F32)

    def extra_round(r, carry):
        base = (r * cap).astype(F32)
        hg, gg = gather(base)
        y = _experts_ffn(hg, gg, step * MOE_EPS, w1_ref, w3_ref, w2_ref)
        o_ref[...] += jnp.dot(scatter_mat(base), y.astype(BF16), preferred_element_type=F32)
        return carry

    lax.fori_loop(1, (cnt_smem[grp] + cap - 1) // cap, extra_round, 0)

    if final_norm:
        @pl.when(step == MOE_STEPS - 1)
        def _():
            o_ref[...] = _rms(o_ref[...], fin_ref[...])


def moe_dense(x, gain, w_r, b_r, w1, w3, w2, layer, *, tm=512, cap=MOE_CAP, final_gain=None, side=None):
    n = x.shape[0]
    tm = min(tm, n)
    cap = min(cap, tm)
    gain = gain.reshape(1, D_MODEL)
    row = pl.BlockSpec((tm, D_MODEL), lambda i, s: (i, 0))
    const2 = lambda a: pl.BlockSpec(a.shape, lambda i, s: (0,) * a.ndim)
    soff = layer * MOE_STEPS
    wspec = pl.BlockSpec((MOE_EPS, D_MODEL, MOE_HIDDEN), lambda i, s: (soff + s, 0, 0))
    earlier = jnp.tril(jnp.ones((tm, tm), BF16), -1)
    args = [x, gain, w_r, b_r, earlier, w1, w3, w2]
    in_specs = [row, const2(gain), const2(w_r), const2(b_r), const2(earlier), wspec, wspec,
                pl.BlockSpec((MOE_EPS, MOE_HIDDEN, D_MODEL), lambda i, s: (soff + s, 0, 0))]
    if final_gain is not None:
        args.append(final_gain.reshape(1, D_MODEL))
        in_specs.append(const2(args[-1]))
    kern = functools.partial(_moe_kernel, final_norm=final_gain is not None, cap=cap)
    out_shape = [jax.ShapeDtypeStruct((n, D_MODEL), F32)]
    out_specs = [row]
    grid = (n // tm, MOE_STEPS)
    if side is not None:
        assert side.steps == grid[0] * grid[1]
        kern = _with_side(kern, len(args), 1, side)
        args += list(side.args)
        in_specs += side.in_specs(MOE_STEPS)
        out_shape += list(side.out_shape)
        out_specs += side.out_specs(MOE_STEPS)
    outs = pl.pallas_call(
        kern, grid=grid, in_specs=in_specs, out_specs=out_specs, out_shape=out_shape,
        scratch_shapes=[pltpu.VMEM((tm, D_MODEL + 2 * ROUTER_LANES), BF16),
                        pltpu.VMEM((tm, ROUTER_LANES), F32),
                        pltpu.VMEM((tm, ROUTER_LANES), F32),
                        pltpu.VMEM((8, tm), F32),
                        pltpu.VMEM((8, tm), F32),
                        pltpu.VMEM((cap, D_MODEL), BF16),
                        pltpu.VMEM((cap, ROUTER_LANES), F32),
                        pltpu.VMEM((cap, D_MODEL), F32),
                        pltpu.SMEM((MOE_GROUPS,), jnp.int32)],
        compiler_params=_cparams("parallel", "arbitrary"), name="moe")(*args)
    return outs[0] if side is None else (outs[0], outs[1:])


def _group_weights(w1, w3, w2):
    ne = w1.shape[0] * MOE_EXPERTS
    return (w1.reshape(ne, D_MODEL, MOE_HIDDEN), w3.reshape(ne, D_MODEL, MOE_HIDDEN),
            w2.reshape(ne, MOE_HIDDEN, D_MODEL))


def _router_params(w_rc, b_rc, w_rf, b_rf):
    pad = ROUTER_LANES - MOE_GROUPS - MOE_EXPERTS
    w_r = jnp.concatenate([w_rc, w_rf, jnp.zeros((D_MODEL, pad), F32)], axis=1).astype(F32)
    b_r = jnp.concatenate([b_rc, b_rf, jnp.zeros((pad,), F32)]).reshape(1, ROUTER_LANES).astype(F32)
    w_hi = w_r.astype(BF16)
    w_lo = (w_r - w_hi.astype(F32)).astype(BF16)
    return jnp.concatenate([w_hi, w_lo], axis=1), b_r


def _mem_kv_kernel(x_ref, g_ref, w_ref, kf_ref, vf_ref, kh_ref, vh_ref):
    h = _rms(x_ref[...], g_ref[0]).astype(BF16)
    for col, f_ref, h_ref in ((0, kf_ref, kh_ref), (D_MODEL, vf_ref, vh_ref)):
        acc = jnp.dot(h, w_ref[0, :, col:col + D_MODEL], preferred_element_type=F32)
        f_ref[0] = acc.astype(f_ref.dtype)
        for hd in range(MEM_HEADS):
            h_ref[0, :, hd, :] = acc[:, hd * MEM_HD:(hd + 1) * MEM_HD]


def mem_kv(mem, gains, w_kv, *, tm=512):
    rows = mem.shape[0]
    nl = w_kv.shape[0]
    flat = jax.ShapeDtypeStruct((nl, rows, D_MODEL), BF16)
    head = jax.ShapeDtypeStruct((nl, rows, MEM_HEADS, MEM_HD), F32)
    fspec = pl.BlockSpec((1, tm, D_MODEL), lambda l, i: (l, i, 0))
    hspec = pl.BlockSpec((1, tm, MEM_HEADS, MEM_HD), lambda l, i: (l, i, 0, 0))
    return pl.pallas_call(
        _mem_kv_kernel, grid=(nl, rows // tm),
        in_specs=[pl.BlockSpec((tm, D_MODEL), lambda l, i: (i, 0)),
                  pl.BlockSpec((1, 1, D_MODEL), lambda l, i: (l, 0, 0)),
                  pl.BlockSpec((1, D_MODEL, 2 * D_MODEL), lambda l, i: (l, 0, 0))],
        out_specs=(fspec, fspec, hspec, hspec), out_shape=(flat, flat, head, head),
        compiler_params=_cparams("parallel", "parallel"), name="mem_kv")(
            mem, gains.reshape(nl, 1, D_MODEL), w_kv)


def _forward(xp, xs, nbp, w, st, mem_k, mem_v, cache_k, cache_v):
    assert DEPTH == 2
    nbs = xs.shape[0]
    moe_tm = 1024
    moe_steps_p = (xp.shape[0] // moe_tm) * MOE_STEPS
    rwp = tuple(w[k][0] for k in ('rw_mu', 'rw_w0', 'rw_w2', 'rw_a0', 'rw_a2', 'rw_g2',
                                  'rw_k_k', 'rw_k_a', 'rw_r_k', 'rw_ln_w', 'rw_ln_b'))
    w_in0, w_out0 = w['w_in0_bf'][0], w['w_out0_bf'][0]

    u, p_s = linear(xs, w_in0, gain=w['norm_mix'][0], splits=(S5_WIDTH, RW_PROJ))
    y_s5, s5r_s, s5i_s = s5_mixer(u.reshape(1, nbs, S5_WIDTH), st['s5_re'], st['s5_im'], w['s5p'][0],
                                  w['s5_d'][0], w['s5_w_glu'][0], tc=1)
    y_rw, rw_s = rwkv_step(p_s, st['shift'], st['rwkv'], rwp)
    xs = linear(y_s5.reshape(nbs, S5_WIDTH), w_out0[:S5_WIDTH], x2=y_rw, w2=w_out0[S5_WIDTH:], residual=xs)
    q_s = linear(xs, w['w_mq_bf'][0], gain=w['norm_mem'][0])

    zeros = lambda *shape: jnp.zeros(shape, F32)
    u, p_p = linear(xp, w_in0, gain=w['norm_mix'][0], splits=(S5_WIDTH, RW_PROJ), out_tmajor=True, batch=nbp)
    y_s5, s5r_p, s5i_p = s5_mixer(u, zeros(nbp, S5_STATE), zeros(nbp, S5_STATE), w['s5p'][0],
                                  w['s5_d'][0], w['s5_w_glu'][0], tc=128)
    rw_bs = 4
    rw_steps = (nbp // rw_bs) * (p_p.shape[0] // RW_HD)
    later = (w['moe_w1'], w['moe_w3'], w['moe_w2'], w['w_in1'][0], w['w_out1'][0])
    job = merge_jobs(xattn_step_job(q_s, cache_k, cache_v, 0, tb=nbs // rw_steps), cast_job(later, rw_steps))
    (y_rw, rw_p, sh_p), (att_s, *cast) = rwkv_prompt(
        p_p, zeros(nbp, RW_PROJ), zeros(nbp, RW_HEADS, RW_HD, RW_HD), rwp, bs=rw_bs, side=job)
    moe_w = _group_weights(*[c.reshape(a.shape) for c, a in zip(cast[:3], later[:3])])
    w_in1, w_out1 = (c.reshape(a.shape) for c, a in zip(cast[3:], later[3:]))
    moe = lambda x, layer, **kw: moe_dense(x, w['norm_ffn'][layer], *w['router'][layer], *moe_w, layer, **kw)
    xp = xattn_prompt(xp, w['norm_mem'][0], w['w_mq_bf'][0], mem_k, mem_v, w['w_mo_bf'][0], 0, nb=nbp,
                      pre=(y_s5, w_out0[:S5_WIDTH], y_rw, w_out0[S5_WIDTH:]))

    xs = linear(att_s.reshape(nbs, D_MODEL), w['w_mo_bf'][0], residual=xs)
    xs = moe(xs, 0)
    q, k, v, g = linear(xs, w_in1, gain=w['norm_mix'][1], out_dtype=BF16, splits=(NQ, NQ, NV, NV))
    xp = moe(xp, 0, tm=moe_tm)
    ret_steps = xp.shape[0] // RET_CHUNK
    job = retention_step_job(q, k, v, g, st['ret'], pos0=float(PAST_LEN), tb=nbs // ret_steps)
    (xp, ret_p), (ret_s, y_ret) = retention_layer_prompt(xp, w['norm_mix'][1], w_in1, w_out1, nb=nbp, side=job)
    xs = linear(y_ret.reshape(nbs, NV), w_out1, residual=xs)
    q_s = linear(xs, w['w_mq_bf'][1], gain=w['norm_mem'][1])

    xp = xattn_prompt(xp, w['norm_mem'][1], w['w_mq_bf'][1], mem_k, mem_v, w['w_mo_bf'][1], 1, nb=nbp)
    job = xattn_step_job(q_s, cache_k, cache_v, 1, tb=nbs // moe_steps_p)
    y_p, (att_s,) = moe(xp, 1, tm=moe_tm, final_gain=w['norm_final'], side=job)
    xs = linear(att_s.reshape(nbs, D_MODEL), w['w_mo_bf'][1], residual=xs)
    y_s = moe(xs, 1, final_gain=w['norm_final'])

    grp = lambda z, nb: z.reshape(1, nb, S5_GROUPS, S5_N)
    prompt_out = (y_p, grp(s5r_p, nbp), grp(s5i_p, nbp), rw_p[None], sh_p[None], ret_p[None])
    sample_out = (y_s, grp(s5r_s, nbs), grp(s5i_s, nbs), rw_s[None], p_s[None], ret_s[None])
    return prompt_out, sample_out


def kernel(x_prompt, x_sample, mem_prompt, state_s5_re, state_s5_im, state_rwkv, state_shift, state_ret, cache_mem_k, cache_mem_v, norm_mix, norm_mem, norm_ffn, norm_final, w_in0, w_out0, s5_a_re, s5_a_im, s5_b_re, s5_b_im, s5_c_re, s5_c_im, s5_d, s5_log_dt, s5_w_glu, rw_mu, rw_w0, rw_w2, rw_a0, rw_a2, rw_g2, rw_k_k, rw_k_a, rw_r_k, rw_ln_w, rw_ln_b, w_in1, w_out1, mem_norm, w_mq, w_mk, w_mv, w_mo, moe_w_rc, moe_b_rc, moe_w_rf, moe_b_rf, moe_w1, moe_w3, moe_w2):
    w = dict(norm_mix=norm_mix, norm_mem=norm_mem, norm_ffn=norm_ffn, norm_final=norm_final,
             w_in0=w_in0, w_out0=w_out0, s5_a_re=s5_a_re, s5_a_im=s5_a_im, s5_b_re=s5_b_re, s5_b_im=s5_b_im,
             s5_c_re=s5_c_re, s5_c_im=s5_c_im, s5_d=s5_d, s5_log_dt=s5_log_dt, s5_w_glu=s5_w_glu,
             rw_mu=rw_mu, rw_w0=rw_w0, rw_w2=rw_w2, rw_a0=rw_a0, rw_a2=rw_a2, rw_g2=rw_g2,
             rw_k_k=rw_k_k, rw_k_a=rw_k_a, rw_r_k=rw_r_k, rw_ln_w=rw_ln_w, rw_ln_b=rw_ln_b,
             w_in1=w_in1, w_out1=w_out1, w_mq=w_mq, w_mo=w_mo,
             moe_w_rc=moe_w_rc, moe_b_rc=moe_b_rc, moe_w_rf=moe_w_rf, moe_b_rf=moe_b_rf,
             moe_w1=moe_w1, moe_w3=moe_w3, moe_w2=moe_w2)
    nbp, t_len, _ = x_prompt.shape
    nbs = x_sample.shape[0]
    n_even, n_odd = state_s5_re.shape[0], state_ret.shape[0]
    for name in ('w_in0', 'w_out0', 'w_mq', 'w_mo'):
        w[name + '_bf'] = w[name].astype(BF16)
    w['s5p'] = [_s5_params(s5_a_re[i], s5_a_im[i], s5_b_re[i], s5_b_im[i], s5_c_re[i], s5_c_im[i], s5_log_dt[i])
                for i in range(n_even)]
    w['router'] = [_router_params(moe_w_rc[l], moe_b_rc[l], moe_w_rf[l], moe_b_rf[l]) for l in range(DEPTH)]

    mem = mem_prompt.reshape(nbp * N_MEM, D_MODEL)
    w_kv = jnp.concatenate([w_mk, w_mv], axis=2).astype(BF16)
    mk, mv, mk_h, mv_h = mem_kv(mem, mem_norm, w_kv)
    mem_k_l = mk.reshape(DEPTH, nbp, N_MEM, D_MODEL)
    mem_v_l = mv.reshape(DEPTH, nbp, N_MEM, D_MODEL)
    mem_k_p = mk_h.reshape(DEPTH, nbp, N_MEM, MEM_HEADS, MEM_HD)
    mem_v_p = mv_h.reshape(DEPTH, nbp, N_MEM, MEM_HEADS, MEM_HD)

    assert n_even == 1 and n_odd == 1
    st = dict(s5_re=state_s5_re.reshape(nbs, S5_STATE), s5_im=state_s5_im.reshape(nbs, S5_STATE),
              rwkv=state_rwkv[0], shift=state_shift[0], ret=state_ret[0])
    (y_p, s5r_p, s5i_p, rw_p, sh_p, ret_p), (y_s, s5r_s, s5i_s, rw_s, sh_s, ret_s) = _forward(
        x_prompt.reshape(nbp * t_len, D_MODEL), x_sample.reshape(nbs, D_MODEL), nbp, w, st,
        mem_k_l, mem_v_l, cache_mem_k, cache_mem_v)
    return (y_p.reshape(nbp, t_len, D_MODEL), y_s.reshape(nbs, 1, D_MODEL),
            s5r_p, s5i_p, rw_p, sh_p, ret_p, mem_k_p, mem_v_p, s5r_s, s5i_s, rw_s, sh_s, ret_s)
```
